```python
import math
import jax, jax.numpy as jnp
from jax import lax
import numpy as np

D_MODEL = 1024
BATCH = 8
SEQ = 4096
DEPTH = 2

D_PLE = 256
A_HEADS = 4
A_DK = 128
A_DV = 128
A_WIDTH = A_HEADS * A_DK
B_HEADS = 8
B_DH = 64
B_WIDTH = B_HEADS * B_DH
D_MIX = A_WIDTH + B_WIDTH
IN_WIDTHS = (A_WIDTH, A_WIDTH, A_HEADS * A_DV, A_HEADS * A_DV,
             B_WIDTH, B_WIDTH, B_WIDTH, B_WIDTH)
D_IN = sum(IN_WIDTHS)
CHUNK = 64
Q_BLOCK = 128
EPS = 1e-6

kernel_name = "hymba_hgrn2_stickbreaking_trunk"


def rmsnorm(x, g):
    xf = x.astype(jnp.float32)
    y = xf * lax.rsqrt(jnp.mean(xf * xf, axis=-1, keepdims=True) + EPS)
    return (y * g.astype(jnp.float32)).astype(x.dtype)


def head_rmsnorm(o, g):
    B_, S_, H, d = o.shape
    y = o * lax.rsqrt(jnp.mean(o * o, axis=-1, keepdims=True) + EPS)
    return y.reshape(B_, S_, H * d) * g.astype(jnp.float32)


def hgrn2_chunkwise(q, k, v, log_f):
    B_, S_, H, dk = q.shape
    dv = v.shape[-1]
    n = S_ // CHUNK

    def to_chunks(t):
        return t.reshape(B_, n, CHUNK, H, t.shape[-1]).transpose(1, 0, 3, 2, 4)

    causal = jnp.tril(jnp.ones((CHUNK, CHUNK), dtype=bool))[:, :, None]

    def step(state, inp):
        qc, kc, vc, gc = inp
        b = jnp.cumsum(gc, axis=2)
        diff = b[:, :, :, None, :] - b[:, :, None, :, :]
        decay = jnp.where(causal, jnp.exp(jnp.where(causal, diff, 0.0)), 0.0)
        scores = jnp.einsum('bhtk,bhtsk,bhsk->bhts', qc, decay, kc)
        o = (jnp.einsum('bhts,bhsv->bhtv', scores, vc)
             + jnp.einsum('bhtk,bhkv->bhtv', qc * jnp.exp(b), state))
        b_last = b[:, :, -1:, :]
        state = (jnp.exp(b_last[:, :, 0, :, None]) * state
                 + jnp.einsum('bhsk,bhsv->bhkv', kc * jnp.exp(b_last - b), vc))
        return state, o

    s0 = jnp.zeros((B_, H, dk, dv), jnp.float32)
    _, o = lax.scan(step, s0, (to_chunks(q), to_chunks(k), to_chunks(v), to_chunks(log_f)))
    return o.transpose(1, 0, 3, 2, 4).reshape(B_, S_, H, dv)


def stick_breaking(q, k, v):
    S_ = q.shape[2]
    scale = B_DH ** -0.5
    outs = []
    for blk in range(S_ // Q_BLOCK):
        t0 = blk * Q_BLOCK
        t1 = t0 + Q_BLOCK
        z = jnp.einsum('bhtd,bhsd->bhts', q[:, :, t0:t1], k[:, :, :t1]) * scale
        mask = jnp.arange(t1)[None, :] < (t0 + jnp.arange(Q_BLOCK))[:, None]
        log_1m = jnp.where(mask, -jax.nn.softplus(z), 0.0)
        log_rest = lax.cumsum(log_1m, axis=3, reverse=True) - log_1m
        w = jnp.where(mask, jnp.exp(jax.nn.log_sigmoid(z) + log_rest), 0.0)
        outs.append(jnp.einsum('bhts,bhsd->bhtd', w, v[:, :, :t1]))
    return jnp.concatenate(outs, axis=2)


def mixer_layer(h, norm_g, w_in, a_norm_g, b_norm_g, w_out, lb):
    B_, S_, _ = h.shape
    f32 = jnp.float32
    u = rmsnorm(h, norm_g)
    proj = jnp.einsum('bsd,de->bse', u, w_in)
    a_q, a_f, a_i, a_g, b_q, b_k, b_v, b_g = jnp.split(
        proj, [int(c) for c in np.cumsum(IN_WIDTHS)[:-1]], axis=-1)

    lb = lb.astype(f32)
    k_a = (1.0 - lb) * jax.nn.sigmoid(-a_f.astype(f32))
    log_f = jnp.log1p(-k_a)
    q_a = jax.nn.silu(a_q.astype(f32))
    hd = lambda t, d: t.reshape(B_, S_, A_HEADS, d)
    o_a = hgrn2_chunkwise(hd(q_a, A_DK), hd(k_a, A_DK), hd(a_i.astype(f32), A_DV), hd(log_f, A_DK))
    o_a = head_rmsnorm(o_a, a_norm_g) * jax.nn.silu(a_g.astype(f32))

    to_heads = lambda t: t.astype(f32).reshape(B_, S_, B_HEADS, B_DH).transpose(0, 2, 1, 3)
    o_b = stick_breaking(to_heads(b_q), to_heads(b_k), to_heads(b_v)).transpose(0, 2, 1, 3)
    o_b = head_rmsnorm(o_b, b_norm_g) * jax.nn.silu(b_g.astype(f32))

    y = jnp.concatenate([o_a, o_b], axis=-1).astype(h.dtype)
    return h + jnp.einsum('bse,ed->bsd', y, w_out)


def _fwd_setup_inputs(seed: int = 0) -> dict:
    key = jax.random.key(seed)
    ks = jax.random.split(key, 14)
    f32 = jnp.float32
    nrm = lambda k, shape, s: jax.random.normal(k, shape, f32) * s
    return {
        "x": nrm(ks[0], (BATCH, SEQ, D_MODEL), 1.0),
        "p": nrm(ks[1], (DEPTH, BATCH, SEQ, D_PLE), 1.0),
        "norm_mix": 1.0 + nrm(ks[2], (DEPTH, D_MODEL), 0.02),
        "w_in": nrm(ks[3], (DEPTH, D_MODEL, D_IN), D_MODEL ** -0.5),
        "a_out_norm": 1.0 + nrm(ks[4], (DEPTH, A_HEADS * A_DV), 0.02),
        "b_out_norm": 1.0 + nrm(ks[5], (DEPTH, B_WIDTH), 0.02),
        "w_out": nrm(ks[6], (DEPTH, D_MIX, D_MODEL), 0.5 * D_MIX ** -0.5),
        "lb_logits": nrm(ks[7], (DEPTH, A_WIDTH), 0.1),
        "ple_gate_norm": 1.0 + nrm(ks[8], (DEPTH, D_MODEL), 0.02),
        "w_ple_gate": nrm(ks[9], (DEPTH, D_MODEL, D_MODEL), D_MODEL ** -0.5),
        "w_ple_proj": nrm(ks[10], (DEPTH, D_PLE, D_MODEL), D_PLE ** -0.5),
        "ple_post_norm": 1.0 + nrm(ks[11], (DEPTH, D_MODEL), 0.02),
        "final_norm": 1.0 + nrm(ks[12], (D_MODEL,), 0.02),
    }


def _fwd_reference(x, p, norm_mix, w_in, a_out_norm, b_out_norm, w_out, lb_logits,
              ple_gate_norm, w_ple_gate, w_ple_proj, ple_post_norm, final_norm):
    sm = jax.nn.softmax(lb_logits.astype(jnp.float32), axis=0)
    lower_bounds = jnp.cumsum(sm, axis=0) - sm[0:1]

    h = x
    for i in range(DEPTH):
        h = mixer_layer(h, norm_mix[i], w_in[i], a_out_norm[i], b_out_norm[i], w_out[i], lower_bounds[i])
        pe = rmsnorm(jnp.einsum('bsc,cd->bsd', p[i], w_ple_proj[i]), ple_post_norm[i])
        gate = jax.nn.sigmoid(jnp.einsum('bsd,de->bse', rmsnorm(h, ple_gate_norm[i]), w_ple_gate[i]))
        h = h + gate * pe
    return rmsnorm(h, final_norm)


import jax as _jax
import jax.numpy as _jnp

TWIN_FORMAT = 'train_step'
FWD_PARAMS = ['x', 'p', 'norm_mix', 'w_in', 'a_out_norm', 'b_out_norm', 'w_out', 'lb_logits', 'ple_gate_norm', 'w_ple_gate', 'w_ple_proj', 'ple_post_norm', 'final_norm']
TWIN_WEIGHTS = ['norm_mix', 'w_in', 'a_out_norm', 'b_out_norm', 'w_out', 'lb_logits', 'ple_gate_norm', 'w_ple_gate', 'w_ple_proj', 'ple_post_norm', 'final_norm']
TWIN_DIFF_INPUT = 'x'
TWIN_INPUTS = ['x', 'p', 'norm_mix', 'w_in', 'a_out_norm', 'b_out_norm', 'w_out', 'lb_logits', 'ple_gate_norm', 'w_ple_gate', 'w_ple_proj', 'ple_post_norm', 'final_norm', 'loss_target', 'm_norm_mix', 'm_w_in', 'm_a_out_norm', 'm_b_out_norm', 'm_w_out', 'm_lb_logits', 'm_ple_gate_norm', 'm_w_ple_gate', 'm_w_ple_proj', 'm_ple_post_norm', 'm_final_norm', 'v_norm_mix', 'v_w_in', 'v_a_out_norm', 'v_b_out_norm', 'v_w_out', 'v_lb_logits', 'v_ple_gate_norm', 'v_w_ple_gate', 'v_w_ple_proj', 'v_ple_post_norm', 'v_final_norm']
TWIN_OUTPUTS = ['loss', 'grad_x', 'grad_norm_mix', 'grad_w_in', 'grad_a_out_norm', 'grad_b_out_norm', 'grad_w_out', 'grad_lb_logits', 'grad_ple_gate_norm', 'grad_w_ple_gate', 'grad_w_ple_proj', 'grad_ple_post_norm', 'grad_final_norm', 'delta_norm_mix', 'delta_w_in', 'delta_a_out_norm', 'delta_b_out_norm', 'delta_w_out', 'delta_lb_logits', 'delta_ple_gate_norm', 'delta_w_ple_gate', 'delta_w_ple_proj', 'delta_ple_post_norm', 'delta_final_norm', 'new_m_norm_mix', 'new_m_w_in', 'new_m_a_out_norm', 'new_m_b_out_norm', 'new_m_w_out', 'new_m_lb_logits', 'new_m_ple_gate_norm', 'new_m_w_ple_gate', 'new_m_w_ple_proj', 'new_m_ple_post_norm', 'new_m_final_norm', 'new_v_norm_mix', 'new_v_w_in', 'new_v_a_out_norm', 'new_v_b_out_norm', 'new_v_w_out', 'new_v_lb_logits', 'new_v_ple_gate_norm', 'new_v_w_ple_gate', 'new_v_w_ple_proj', 'new_v_ple_post_norm', 'new_v_final_norm']
TWIN_LEAF_KINDS = {'loss': 'loss', 'grad_x': 'grad_x', 'grad_norm_mix': 'grad_w', 'grad_w_in': 'grad_w', 'grad_a_out_norm': 'grad_w', 'grad_b_out_norm': 'grad_w', 'grad_w_out': 'grad_w', 'grad_lb_logits': 'grad_w', 'grad_ple_gate_norm': 'grad_w', 'grad_w_ple_gate': 'grad_w', 'grad_w_ple_proj': 'grad_w', 'grad_ple_post_norm': 'grad_w', 'grad_final_norm': 'grad_w', 'delta_norm_mix': 'delta_w', 'delta_w_in': 'delta_w', 'delta_a_out_norm': 'delta_w', 'delta_b_out_norm': 'delta_w', 'delta_w_out': 'delta_w', 'delta_lb_logits': 'delta_w', 'delta_ple_gate_norm': 'delta_w', 'delta_w_ple_gate': 'delta_w', 'delta_w_ple_proj': 'delta_w', 'delta_ple_post_norm': 'delta_w', 'delta_final_norm': 'delta_w', 'new_m_norm_mix': 'new_m', 'new_m_w_in': 'new_m', 'new_m_a_out_norm': 'new_m', 'new_m_b_out_norm': 'new_m', 'new_m_w_out': 'new_m', 'new_m_lb_logits': 'new_m', 'new_m_ple_gate_norm': 'new_m', 'new_m_w_ple_gate': 'new_m', 'new_m_w_ple_proj': 'new_m', 'new_m_ple_post_norm': 'new_m', 'new_m_final_norm': 'new_m', 'new_v_norm_mix': 'new_v', 'new_v_w_in': 'new_v', 'new_v_a_out_norm': 'new_v', 'new_v_b_out_norm': 'new_v', 'new_v_w_out': 'new_v', 'new_v_lb_logits': 'new_v', 'new_v_ple_gate_norm': 'new_v', 'new_v_w_ple_gate': 'new_v', 'new_v_w_ple_proj': 'new_v', 'new_v_ple_post_norm': 'new_v', 'new_v_final_norm': 'new_v'}


def _forward(args):
    return _fwd_reference(*[args[k] for k in FWD_PARAMS])


def _output_shape():
    out = _jax.eval_shape(lambda: _forward(_fwd_setup_inputs(0)))
    return out.shape, out.dtype

N_MICROBATCH = 1
ADAM_LR = 0.001
ADAM_B1 = 0.9
ADAM_B2 = 0.999
ADAM_EPS = 1e-08
ADAM_WD = 0.01
ADAM_STEP = 10
PER_EXAMPLE_BATCH_AXIS = {'x': 0, 'p': 1, 'loss_target': 0}
SHARED_INPUTS = []
_WEIGHT_DTYPES = {'norm_mix': _jnp.float32, 'w_in': _jnp.float32, 'a_out_norm': _jnp.float32, 'b_out_norm': _jnp.float32, 'w_out': _jnp.float32, 'lb_logits': _jnp.float32, 'ple_gate_norm': _jnp.float32, 'w_ple_gate': _jnp.float32, 'w_ple_proj': _jnp.float32, 'ple_post_norm': _jnp.float32, 'final_norm': _jnp.float32}
MOMENT_SCALE = {'norm_mix': 6.436684e-02, 'w_in': 3.227780e-02, 'a_out_norm': 4.269108e-02, 'b_out_norm': 4.131343e-02, 'w_out': 8.474375e-02, 'lb_logits': 3.929865e-03, 'ple_gate_norm': 3.212698e-02, 'w_ple_gate': 2.957978e-02, 'w_ple_proj': 7.581137e-02, 'ple_post_norm': 1.433736e-01, 'final_norm': 3.199668e+01}


def _to_microbatches(a, axis):
    t = _jnp.moveaxis(a, axis, 0)
    t = t.reshape((N_MICROBATCH, t.shape[0] // N_MICROBATCH) + t.shape[1:])
    return _jnp.moveaxis(t, 1, axis + 1)


def setup_inputs(seed: int = 0) -> dict:
    inp = _fwd_setup_inputs(seed)
    key = _jax.random.fold_in(_jax.random.key(seed), 7919)
    shape, _ = _output_shape()
    out = dict(inp)
    out["loss_target"] = _jax.random.normal(_jax.random.fold_in(key, 0), shape, _jnp.float32)
    for i, name in enumerate(TWIN_WEIGHTS):
        w = inp[name].astype(_jnp.float32)
        if MOMENT_SCALE is None:
            s = _jnp.sqrt(_jnp.mean(_jnp.square(w)) + 1e-30)
        else:
            s = MOMENT_SCALE[name]
        km, kv = _jax.random.split(_jax.random.fold_in(key, i + 1))
        out[name] = w
        out["m_" + name] = s * _jax.random.normal(km, w.shape, _jnp.float32)
        out["v_" + name] = (s * s) * _jax.random.uniform(kv, w.shape, _jnp.float32, 0.5, 1.5)
    if N_MICROBATCH > 1:
        for name, axis in PER_EXAMPLE_BATCH_AXIS.items():
            out[name] = _to_microbatches(out[name], axis)
    return {'x': out['x'], 'p': out['p'], 'norm_mix': out['norm_mix'], 'w_in': out['w_in'], 'a_out_norm': out['a_out_norm'], 'b_out_norm': out['b_out_norm'], 'w_out': out['w_out'], 'lb_logits': out['lb_logits'], 'ple_gate_norm': out['ple_gate_norm'], 'w_ple_gate': out['w_ple_gate'], 'w_ple_proj': out['w_ple_proj'], 'ple_post_norm': out['ple_post_norm'], 'final_norm': out['final_norm'], 'loss_target': out['loss_target'], 'm_norm_mix': out['m_norm_mix'], 'm_w_in': out['m_w_in'], 'm_a_out_norm': out['m_a_out_norm'], 'm_b_out_norm': out['m_b_out_norm'], 'm_w_out': out['m_w_out'], 'm_lb_logits': out['m_lb_logits'], 'm_ple_gate_norm': out['m_ple_gate_norm'], 'm_w_ple_gate': out['m_w_ple_gate'], 'm_w_ple_proj': out['m_w_ple_proj'], 'm_ple_post_norm': out['m_ple_post_norm'], 'm_final_norm': out['m_final_norm'], 'v_norm_mix': out['v_norm_mix'], 'v_w_in': out['v_w_in'], 'v_a_out_norm': out['v_a_out_norm'], 'v_b_out_norm': out['v_b_out_norm'], 'v_w_out': out['v_w_out'], 'v_lb_logits': out['v_lb_logits'], 'v_ple_gate_norm': out['v_ple_gate_norm'], 'v_w_ple_gate': out['v_w_ple_gate'], 'v_w_ple_proj': out['v_w_ple_proj'], 'v_ple_post_norm': out['v_ple_post_norm'], 'v_final_norm': out['v_final_norm']}


def _loss(weights, diff, rest, loss_target):
    with _jax.named_scope("forward"):
        args = {**rest, TWIN_DIFF_INPUT: diff, **{k: w.astype(_WEIGHT_DTYPES[k]) for k, w in weights.items()}}
        y = _forward(args)
    with _jax.named_scope("loss_head"):
        err = _jnp.square(y.astype(_jnp.float32) - loss_target)
        return 0.5 * _jnp.sum(_jnp.mean(err, axis=-1)) if err.ndim else 0.5 * err


def _adamw(w, g, m, v):
    m = ADAM_B1 * m + (1.0 - ADAM_B1) * g
    v = ADAM_B2 * v + (1.0 - ADAM_B2) * _jnp.square(g)
    m_hat = m / (1.0 - ADAM_B1 ** ADAM_STEP)
    v_hat = v / (1.0 - ADAM_B2 ** ADAM_STEP)
    delta = -ADAM_LR * (m_hat / (_jnp.sqrt(v_hat) + ADAM_EPS) + ADAM_WD * w)
    return delta, m, v


def reference(x, p, norm_mix, w_in, a_out_norm, b_out_norm, w_out, lb_logits, ple_gate_norm, w_ple_gate, w_ple_proj, ple_post_norm, final_norm, loss_target, m_norm_mix, m_w_in, m_a_out_norm, m_b_out_norm, m_w_out, m_lb_logits, m_ple_gate_norm, m_w_ple_gate, m_w_ple_proj, m_ple_post_norm, m_final_norm, v_norm_mix, v_w_in, v_a_out_norm, v_b_out_norm, v_w_out, v_lb_logits, v_ple_gate_norm, v_w_ple_gate, v_w_ple_proj, v_ple_post_norm, v_final_norm):
    given = dict(x=x, p=p, norm_mix=norm_mix, w_in=w_in, a_out_norm=a_out_norm, b_out_norm=b_out_norm, w_out=w_out, lb_logits=lb_logits, ple_gate_norm=ple_gate_norm, w_ple_gate=w_ple_gate, w_ple_proj=w_ple_proj, ple_post_norm=ple_post_norm, final_norm=final_norm, loss_target=loss_target, m_norm_mix=m_norm_mix, m_w_in=m_w_in, m_a_out_norm=m_a_out_norm, m_b_out_norm=m_b_out_norm, m_w_out=m_w_out, m_lb_logits=m_lb_logits, m_ple_gate_norm=m_ple_gate_norm, m_w_ple_gate=m_w_ple_gate, m_w_ple_proj=m_w_ple_proj, m_ple_post_norm=m_ple_post_norm, m_final_norm=m_final_norm, v_norm_mix=v_norm_mix, v_w_in=v_w_in, v_a_out_norm=v_a_out_norm, v_b_out_norm=v_b_out_norm, v_w_out=v_w_out, v_lb_logits=v_lb_logits, v_ple_gate_norm=v_ple_gate_norm, v_w_ple_gate=v_w_ple_gate, v_w_ple_proj=v_w_ple_proj, v_ple_post_norm=v_ple_post_norm, v_final_norm=v_final_norm)
    weights = {n: given[n] for n in TWIN_WEIGHTS}
    shared = {n: given[n] for n in SHARED_INPUTS}
    per_example = {n: given[n] for n in ['x', 'p']}
    grad_fn = _jax.value_and_grad(_loss, argnums=(0, 1))

    def one_microbatch(ex, loss_target):
        ex = dict(ex)
        diff = ex.pop(TWIN_DIFF_INPUT)
        return grad_fn(weights, diff, {**shared, **ex}, loss_target)

    if N_MICROBATCH == 1:
        loss, (grad_w, grad_x) = one_microbatch(per_example, given["loss_target"])
    else:
        def body(carry, xs):
            loss_sum, grad_sum = carry
            l_k, (gw_k, gx_k) = one_microbatch(xs[0], xs[1])
            with _jax.named_scope("update"):
                return (loss_sum + l_k, _jax.tree.map(_jnp.add, grad_sum, gw_k)), gx_k

        init = (_jnp.zeros((), _jnp.float32), _jax.tree.map(_jnp.zeros_like, weights))
        (loss, grad_w), grad_x = _jax.lax.scan(body, init, (per_example, given["loss_target"]))
    with _jax.named_scope("update"):
        delta_w, new_m, new_v = {}, {}, {}
        for n in TWIN_WEIGHTS:
            delta_w[n], new_m[n], new_v[n] = _adamw(weights[n], grad_w[n], given["m_" + n], given["v_" + n])
    return (loss, grad_x, *[grad_w[n] for n in TWIN_WEIGHTS], *[delta_w[n] for n in TWIN_WEIGHTS],
            *[new_m[n] for n in TWIN_WEIGHTS], *[new_v[n] for n in TWIN_WEIGHTS])
```

```python
import functools
import math

import numpy as np
import jax
import jax.numpy as jnp
from jax import lax
from jax.experimental import pallas as pl
from jax.experimental.pallas import tpu as pltpu

F32 = jnp.float32
BF16 = jnp.bfloat16
MESH = pl.DeviceIdType.MESH
HIGHEST = lax.Precision.HIGHEST

D_MODEL = 1024
D_PLE = 256
D_IN = 4096
A_HEADS, A_D = 4, 128
B_HEADS, B_D = 8, 64
GROUP = 512
EPS = 1e-6
N_SHARD = 4
N_DEV = 8

HG_CHUNK = 128
HG_LEVELS = 7
SB_BLOCK = 128

ADAM_LR, ADAM_B1, ADAM_B2, ADAM_EPS, ADAM_WD, ADAM_STEP = 0.001, 0.9, 0.999, 1e-08, 0.01, 10

VMEM_LIMIT = 48 * 1024 * 1024

ROWS_W_IN = 2 * D_MODEL
ROWS_W_OUT = 2 * (D_MODEL // N_SHARD)
ROWS_W_PG = 2 * (D_MODEL // N_SHARD)
ROWS_W_PP = 2 * (D_PLE * (D_MODEL // N_SHARD) // D_MODEL)
ROWS_FLAT = ROWS_W_IN + ROWS_W_OUT + ROWS_W_PG + ROWS_W_PP
HALF_FLAT = ROWS_FLAT // 2

SMALL_ROWS = 88


def _sds(shape, dtype=F32):
    return jax.ShapeDtypeStruct(shape, dtype)


def _params(sem=None):
    kw = dict(vmem_limit_bytes=VMEM_LIMIT)
    if sem is not None:
        kw["dimension_semantics"] = sem
    return pltpu.CompilerParams(**kw)


def _dot(a, b, precision=None):
    return lax.dot_general(a, b, (((1,), (0,)), ((), ())), preferred_element_type=F32, precision=precision)


def _dot_nt(a, b, precision=None):
    return lax.dot_general(a, b, (((1,), (1,)), ((), ())), preferred_element_type=F32, precision=precision)


def _dot_tn(a, b, precision=None):
    return lax.dot_general(a, b, (((0,), (0,)), ((), ())), preferred_element_type=F32, precision=precision)


def _bf(x):
    return x.astype(BF16)


def _split(x):
    hi = x.astype(BF16)
    lo = (x - hi.astype(F32)).astype(BF16)
    return hi, lo


def _rms(x):
    r = lax.rsqrt(jnp.mean(x * x, axis=-1, keepdims=True) + EPS)
    return x * r, r


def _rms_bwd(dxh, xh, r):
    return r * (dxh - xh * jnp.mean(dxh * xh, axis=-1, keepdims=True))


def _sigmoid(x):
    return 1.0 / (1.0 + jnp.exp(-x))


def _silu_grad(x, sig):
    return sig * (1.0 + x * (1.0 - sig))


def _row_tile(t, want):
    return min(t, want)


def _inproj(h, g, w):
    t = h.shape[0]
    tm = _row_tile(t, 512)
    tn = 1024

    def body(h_ref, g_ref, w_ref, o_ref):
        xh, _ = _rms(h_ref[...])
        o_ref[...] = _dot(_bf(xh * g_ref[...]), w_ref[...])

    return pl.pallas_call(
        body, name="inproj", grid=(D_IN // tn, t // tm),
        in_specs=[pl.BlockSpec((tm, D_MODEL), lambda j, i: (i, 0)),
                  pl.BlockSpec((1, D_MODEL), lambda j, i: (0, 0)),
                  pl.BlockSpec((D_MODEL, tn), lambda j, i: (0, j))],
        out_specs=pl.BlockSpec((tm, tn), lambda j, i: (i, j)),
        out_shape=_sds((t, D_IN)), compiler_params=_params(("arbitrary", "arbitrary")),
    )(h, g, w)


def _outproj(h, ya, yb, w):
    t = h.shape[0]
    tm = _row_tile(t, 512)

    def body(h_ref, ya_ref, yb_ref, w_ref, o_ref):
        o_ref[...] = (h_ref[...] + _dot(_bf(ya_ref[...]), w_ref[pl.ds(0, GROUP), :])
                      + _dot(_bf(yb_ref[...]), w_ref[pl.ds(GROUP, GROUP), :]))

    return pl.pallas_call(
        body, name="outproj", grid=(t // tm,),
        in_specs=[pl.BlockSpec((tm, D_MODEL), lambda i: (i, 0)),
                  pl.BlockSpec((tm, GROUP), lambda i: (i, 0)),
                  pl.BlockSpec((tm, GROUP), lambda i: (i, 0)),
                  pl.BlockSpec((D_MODEL, D_MODEL), lambda i: (0, 0))],
        out_specs=pl.BlockSpec((tm, D_MODEL), lambda i: (i, 0)),
        out_shape=_sds((t, D_MODEL)), compiler_params=_params(("arbitrary",)),
    )(h, ya, yb, w)


def _ple_fwd(h, p, w_pp, w_pg, g_post, g_gate):
    t = h.shape[0]
    tm = _row_tile(t, 256)

    def body(h_ref, p_ref, wpp_ref, wpg_ref, gp_ref, gg_ref, o_ref):
        x = h_ref[...]
        ph, _ = _rms(_dot(_bf(p_ref[...]), wpp_ref[...]))
        xh, _ = _rms(x)
        gate = _sigmoid(_dot(_bf(xh * gg_ref[...]), wpg_ref[...]))
        o_ref[...] = x + gate * (ph * gp_ref[...])

    return pl.pallas_call(
        body, name="ple_fwd", grid=(t // tm,),
        in_specs=[pl.BlockSpec((tm, D_MODEL), lambda i: (i, 0)),
                  pl.BlockSpec((tm, D_PLE), lambda i: (i, 0)),
                  pl.BlockSpec((D_PLE, D_MODEL), lambda i: (0, 0)),
                  pl.BlockSpec((D_MODEL, D_MODEL), lambda i: (0, 0)),
                  pl.BlockSpec((1, D_MODEL), lambda i: (0, 0)),
                  pl.BlockSpec((1, D_MODEL), lambda i: (0, 0))],
        out_specs=pl.BlockSpec((tm, D_MODEL), lambda i: (i, 0)),
        out_shape=_sds((t, D_MODEL)), compiler_params=_params(("arbitrary",)),
    )(h, p, w_pp, w_pg, g_post, g_gate)


def _ple_bwd(dh2, h, p, w_pp, w_pg, g_post, g_gate):
    t = h.shape[0]
    tm = _row_tile(t, 256)

    def body(d_ref, h_ref, p_ref, wpp_ref, wpg_ref, gp_ref, gg_ref, dh_ref, dwpg_ref, dwpp_ref, dgg_ref, dgp_ref):
        @pl.when(pl.program_id(0) == 0)
        def _():
            dwpg_ref[...] = jnp.zeros_like(dwpg_ref)
            dwpp_ref[...] = jnp.zeros_like(dwpp_ref)
            dgg_ref[...] = jnp.zeros_like(dgg_ref)
            dgp_ref[...] = jnp.zeros_like(dgp_ref)

        d = d_ref[...]
        x = h_ref[...]
        gp = gp_ref[...]
        gg = gg_ref[...]
        pb = _bf(p_ref[...])
        ph, rp = _rms(_dot(pb, wpp_ref[...]))
        pe = ph * gp
        xh, rx = _rms(x)
        un = _bf(xh * gg)
        gate = _sigmoid(_dot(un, wpg_ref[...]))
        dgpre = _bf(d * pe * gate * (1.0 - gate))
        dun = _dot_nt(dgpre, wpg_ref[...])
        dh_ref[...] = d + _rms_bwd(dun * gg, xh, rx)
        dgg_ref[...] += jnp.sum(dun * xh, axis=0, keepdims=True)
        dwpg_ref[...] += _dot_tn(un, dgpre)
        dpe = d * gate
        dgp_ref[...] += jnp.sum(dpe * ph, axis=0, keepdims=True)
        dwpp_ref[...] += _dot_tn(pb, _bf(_rms_bwd(dpe * gp, ph, rp)))

    return pl.pallas_call(
        body, name="ple_bwd", grid=(t // tm,),
        in_specs=[pl.BlockSpec((tm, D_MODEL), lambda i: (i, 0)),
                  pl.BlockSpec((tm, D_MODEL), lambda i: (i, 0)),
                  pl.BlockSpec((tm, D_PLE), lambda i: (i, 0)),
                  pl.BlockSpec((D_PLE, D_MODEL), lambda i: (0, 0)),
                  pl.BlockSpec((D_MODEL, D_MODEL), lambda i: (0, 0)),
                  pl.BlockSpec((1, D_MODEL), lambda i: (0, 0)),
                  pl.BlockSpec((1, D_MODEL), lambda i: (0, 0))],
        out_specs=[pl.BlockSpec((tm, D_MODEL), lambda i: (i, 0)),
                   pl.BlockSpec((D_MODEL, D_MODEL), lambda i: (0, 0)),
                   pl.BlockSpec((D_PLE, D_MODEL), lambda i: (0, 0)),
                   pl.BlockSpec((1, D_MODEL), lambda i: (0, 0)),
                   pl.BlockSpec((1, D_MODEL), lambda i: (0, 0))],
        out_shape=[_sds((t, D_MODEL)), _sds((D_MODEL, D_MODEL)), _sds((D_PLE, D_MODEL)),
                   _sds((1, D_MODEL)), _sds((1, D_MODEL))],
        compiler_params=_params(("arbitrary",)),
    )(dh2, h, p, w_pp, w_pg, g_post, g_gate)


def _outproj_bwd(dh, ya, yb, w):
    t = dh.shape[0]
    tm = _row_tile(t, 512)

    def body(d_ref, ya_ref, yb_ref, w_ref, dya_ref, dyb_ref, dw_ref):
        @pl.when(pl.program_id(0) == 0)
        def _():
            dw_ref[...] = jnp.zeros_like(dw_ref)

        d = _bf(d_ref[...])
        dya_ref[...] = _dot_nt(d, w_ref[pl.ds(0, GROUP), :])
        dyb_ref[...] = _dot_nt(d, w_ref[pl.ds(GROUP, GROUP), :])
        dw_ref[pl.ds(0, GROUP), :] += _dot_tn(_bf(ya_ref[...]), d)
        dw_ref[pl.ds(GROUP, GROUP), :] += _dot_tn(_bf(yb_ref[...]), d)

    return pl.pallas_call(
        body, name="outproj_bwd", grid=(t // tm,),
        in_specs=[pl.BlockSpec((tm, D_MODEL), lambda i: (i, 0)),
                  pl.BlockSpec((tm, GROUP), lambda i: (i, 0)),
                  pl.BlockSpec((tm, GROUP), lambda i: (i, 0)),
                  pl.BlockSpec((D_MODEL, D_MODEL), lambda i: (0, 0))],
        out_specs=[pl.BlockSpec((tm, GROUP), lambda i: (i, 0)),
                   pl.BlockSpec((tm, GROUP), lambda i: (i, 0)),
                   pl.BlockSpec((D_MODEL, D_MODEL), lambda i: (0, 0))],
        out_shape=[_sds((t, GROUP)), _sds((t, GROUP)), _sds((D_MODEL, D_MODEL))],
        compiler_params=_params(("arbitrary",)),
    )(dh, ya, yb, w)


def _inproj_bwd_dx(dres, h, g, w, parts):
    t = h.shape[0]
    tm = _row_tile(t, 256)

    def body(dres_ref, h_ref, g_ref, w_ref, *rest):
        part_refs, (dh_ref, dg_ref) = rest[:8], rest[8:]

        @pl.when(pl.program_id(0) == 0)
        def _():
            dg_ref[...] = jnp.zeros_like(dg_ref)

        du = jnp.zeros((tm, D_MODEL), F32)
        for i, pr in enumerate(part_refs):
            du = du + _dot_nt(_bf(pr[...]), w_ref[:, pl.ds(i * GROUP, GROUP)])
        xh, r = _rms(h_ref[...])
        dg_ref[...] += jnp.sum(du * xh, axis=0, keepdims=True)
        dh_ref[...] = dres_ref[...] + _rms_bwd(du * g_ref[...], xh, r)

    return pl.pallas_call(
        body, name="inproj_bwd_dx", grid=(t // tm,),
        in_specs=[pl.BlockSpec((tm, D_MODEL), lambda i: (i, 0)),
                  pl.BlockSpec((tm, D_MODEL), lambda i: (i, 0)),
                  pl.BlockSpec((1, D_MODEL), lambda i: (0, 0)),
                  pl.BlockSpec((D_MODEL, D_IN), lambda i: (0, 0))]
                 + [pl.BlockSpec((tm, GROUP), lambda i: (i, 0)) for _ in range(8)],
        out_specs=[pl.BlockSpec((tm, D_MODEL), lambda i: (i, 0)),
                   pl.BlockSpec((1, D_MODEL), lambda i: (0, 0))],
        out_shape=[_sds((t, D_MODEL)), _sds((1, D_MODEL))],
        compiler_params=_params(("arbitrary",)),
    )(dres, h, g, w, *parts)


def _inproj_bwd_dw(h, g, parts):
    t = h.shape[0]
    tm = _row_tile(t, 512)
    stacked = jnp.stack(parts)

    def body(h_ref, g_ref, d_ref, dw_ref):
        @pl.when(pl.program_id(1) == 0)
        def _():
            dw_ref[...] = jnp.zeros_like(dw_ref)

        xh, _ = _rms(h_ref[...])
        dw_ref[...] += _dot_tn(_bf(xh * g_ref[...]), _bf(d_ref[...]))

    return pl.pallas_call(
        body, name="inproj_bwd_dw", grid=(8, t // tm),
        in_specs=[pl.BlockSpec((tm, D_MODEL), lambda j, i: (i, 0)),
                  pl.BlockSpec((1, D_MODEL), lambda j, i: (0, 0)),
                  pl.BlockSpec((None, tm, GROUP), lambda j, i: (j, i, 0))],
        out_specs=pl.BlockSpec((None, D_MODEL, GROUP), lambda j, i: (j, 0, 0)),
        out_shape=_sds((8, D_MODEL, GROUP)), compiler_params=_params(("arbitrary", "arbitrary")),
    )(h, g, stacked)


def _final(h, g, target):
    t = h.shape[0]
    tm = _row_tile(t, 512)

    def body(h_ref, g_ref, t_ref, dh_ref, dg_ref, loss_ref):
        @pl.when(pl.program_id(0) == 0)
        def _():
            dg_ref[...] = jnp.zeros_like(dg_ref)
            loss_ref[...] = jnp.zeros_like(loss_ref)

        xh, r = _rms(h_ref[...])
        gg = g_ref[...]
        err = xh * gg - t_ref[...]
        part = 0.5 * jnp.sum(jnp.mean(err * err, axis=-1, keepdims=True), axis=0, keepdims=True)
        loss_ref[...] += jnp.broadcast_to(part, loss_ref.shape)
        dy = err * (1.0 / D_MODEL)
        dg_ref[...] += jnp.sum(dy * xh, axis=0, keepdims=True)
        dh_ref[...] = _rms_bwd(dy * gg, xh, r)

    return pl.pallas_call(
        body, name="final", grid=(t // tm,),
        in_specs=[pl.BlockSpec((tm, D_MODEL), lambda i: (i, 0)),
                  pl.BlockSpec((1, D_MODEL), lambda i: (0, 0)),
                  pl.BlockSpec((tm, D_MODEL), lambda i: (i, 0))],
        out_specs=[pl.BlockSpec((tm, D_MODEL), lambda i: (i, 0)),
                   pl.BlockSpec((1, D_MODEL), lambda i: (0, 0)),
                   pl.BlockSpec((1, 128), lambda i: (0, 0))],
        out_shape=[_sds((t, D_MODEL)), _sds((1, D_MODEL)), _sds((1, 128))],
        compiler_params=_params(("arbitrary",)),
    )(h, g, target)


def _hgrn_consts():
    c, nl = HG_CHUNK, HG_LEVELS
    t = np.arange(c)
    tril = np.tril(np.ones((c, c), np.float32))
    masks = np.zeros((nl + 1, c, c), np.float32)
    masks[0] = np.eye(c, dtype=np.float32)
    dmat = np.zeros(((nl + 2) * c, c), np.float32)
    dmat[0:c] = tril
    for l in range(nl):
        m = c >> (l + 1)
        blk = t // (2 * m)
        r = blk * 2 * m + m - 1
        upper = (t % (2 * m)) >= m
        masks[l + 1] = ((blk[:, None] == blk[None, :]) & upper[:, None] & (~upper)[None, :]).astype(np.float32)
        dmat[(l + 1) * c:(l + 2) * c] = tril[t] - tril[r]
    dmat[(nl + 1) * c:] = np.triu(np.ones((c, c), np.float32), k=1)
    return jnp.asarray(masks), jnp.asarray(dmat)


def _hgrn_common(aq, af, lb, dmat_ref):
    c, nl = HG_CHUNK, HG_LEVELS
    sq = _sigmoid(aq)
    q = aq * sq
    sneg = _sigmoid(-af)
    kk = (1.0 - lb) * sneg
    logf = jnp.log1p(-kk)
    x_all = _dot(dmat_ref[pl.ds(0, (nl + 1) * c), :], logf, HIGHEST)
    b = x_all[0:c]
    b_last = jnp.sum(logf, axis=0, keepdims=True)
    return sq, q, sneg, kk, x_all, b, b_last


def _hgrn_level(x_all, l, q, kk):
    c = HG_CHUNK
    x = x_all[(l + 1) * c:(l + 2) * c]
    qf = jnp.exp(jnp.minimum(x, 0.0))
    kf = jnp.exp(-jnp.maximum(x, 0.0))
    return qf, kf, _bf(q * qf), _bf(kk * kf)


def _hgrn_scores(x_all, q, kk, mask_ref):
    p = mask_ref[0] * _dot_nt(_bf(q), _bf(kk))
    for l in range(HG_LEVELS):
        _, _, ql, kl = _hgrn_level(x_all, l, q, kk)
        p = p + mask_ref[l + 1] * _dot_nt(ql, kl)
    return p


def _hgrn_specs(n_chunks, rev):
    c = HG_CHUNK
    cidx = (lambda n: n_chunks - 1 - n) if rev else (lambda n: n)
    col = lambda g: pl.BlockSpec((c, A_D), lambda h, n: (cidx(n), g * A_HEADS + h))
    vec = pl.BlockSpec((1, A_D), lambda h, n: (0, h))
    mask = pl.BlockSpec((HG_LEVELS + 1, c, c), lambda h, n: (0, 0, 0))
    dmat = pl.BlockSpec(((HG_LEVELS + 2) * c, c), lambda h, n: (0, 0))
    return cidx, col, vec, mask, dmat


def _hgrn_fwd(proj, lb, gain):
    t = proj.shape[0]
    c = HG_CHUNK
    nch = t // c
    masks, dmat = _hgrn_consts()
    cidx, col, vec, mask_spec, dmat_spec = _hgrn_specs(nch, False)

    def body(aq_ref, af_ref, ai_ref, ag_ref, lb_ref, gain_ref, mask_ref, dmat_ref, y_ref, st_ref, s_scr):
        @pl.when(pl.program_id(1) == 0)
        def _():
            s_scr[...] = jnp.zeros_like(s_scr)

        v = ai_ref[...]
        ag = ag_ref[...]
        _, q, _, kk, x_all, b, b_last = _hgrn_common(aq_ref[...], af_ref[...], lb_ref[...], dmat_ref)
        p = _hgrn_scores(x_all, q, kk, mask_ref)
        s = s_scr[...]
        st_ref[...] = s
        vb = _bf(v)
        o = _dot(_bf(p), vb) + _dot_nt(_bf(q * jnp.exp(b)), _bf(s))
        s_scr[...] = s * jnp.exp(b_last) + _dot_tn(vb, _bf(kk * jnp.exp(b_last - b)))
        oh, _ = _rms(o)
        y_ref[...] = oh * gain_ref[...] * (ag * _sigmoid(ag))

    return pl.pallas_call(
        body, name="hgrn_fwd", grid=(A_HEADS, nch),
        in_specs=[col(0), col(1), col(2), col(3), vec, vec, mask_spec, dmat_spec],
        out_specs=[pl.BlockSpec((c, A_D), lambda h, n: (n, h)),
                   pl.BlockSpec((None, None, A_D, A_D), lambda h, n: (h, n, 0, 0))],
        out_shape=[_sds((t, GROUP)), _sds((A_HEADS, nch, A_D, A_D))],
        scratch_shapes=[pltpu.VMEM((A_D, A_D), F32)],
        compiler_params=_params(("arbitrary", "arbitrary")),
    )(proj, proj, proj, proj, lb, gain, masks, dmat)


def _hgrn_bwd(proj, lb, gain, states, dya):
    t = proj.shape[0]
    c, nl = HG_CHUNK, HG_LEVELS
    nch = t // c
    masks, dmat = _hgrn_consts()
    cidx, col, vec, mask_spec, dmat_spec = _hgrn_specs(nch, True)

    def body(aq_ref, af_ref, ai_ref, ag_ref, lb_ref, gain_ref, mask_ref, dmat_ref, st_ref, dy_ref,
             daq_ref, daf_ref, dai_ref, dag_ref, dlb_ref, dgain_ref, ds_scr, z_scr):
        @pl.when(pl.program_id(1) == 0)
        def _():
            ds_scr[...] = jnp.zeros_like(ds_scr)
            dlb_ref[...] = jnp.zeros_like(dlb_ref)
            dgain_ref[...] = jnp.zeros_like(dgain_ref)

        aq = aq_ref[...]
        v = ai_ref[...]
        ag = ag_ref[...]
        lb = lb_ref[...]
        gain = gain_ref[...]
        dy = dy_ref[...]
        sq, q, sneg, kk, x_all, b, b_last = _hgrn_common(aq, af_ref[...], lb, dmat_ref)
        eb = jnp.exp(b)
        ebl = jnp.exp(b_last - b)
        ebl_row = jnp.exp(b_last)
        qe = _bf(q * eb)
        ke = _bf(kk * ebl)
        vb = _bf(v)
        s = st_ref[...]
        sb = _bf(s)
        ds = ds_scr[...]
        dsb = _bf(ds)

        pb = _bf(_hgrn_scores(x_all, q, kk, mask_ref))
        o = _dot(pb, vb) + _dot_nt(qe, sb)

        oh, r = _rms(o)
        sg_sig = _sigmoid(ag)
        sg = ag * sg_sig
        dag_ref[...] = dy * oh * gain * _silu_grad(ag, sg_sig)
        dgain_ref[...] += jnp.sum(dy * oh * sg, axis=0, keepdims=True)
        do = _bf(_rms_bwd(dy * gain * sg, oh, r))

        dp = _dot_nt(do, vb)
        dai_ref[...] = _dot_tn(pb, do) + _dot_nt(ke, dsb)
        dpd = jnp.sum(mask_ref[0] * dp, axis=1, keepdims=True)
        dq_s = eb * _dot(do, sb)
        dk_s = ebl * _dot(vb, dsb)
        z_scr[pl.ds(0, c), :] = q * dq_s
        z_scr[pl.ds((nl + 1) * c, c), :] = kk * dk_s
        dq = dq_s + dpd * kk
        dk = dk_s + dpd * q
        for l in range(nl):
            qf, kf, ql, kl = _hgrn_level(x_all, l, q, kk)
            dpl = _bf(mask_ref[l + 1] * dp)
            dq_l = qf * _dot(dpl, kl)
            dk_l = kf * _dot_tn(dpl, ql)
            z_scr[pl.ds((l + 1) * c, c), :] = q * dq_l - kk * dk_l
            dq = dq + dq_l
            dk = dk + dk_l

        dlogf = (_dot_tn(dmat_ref[...], z_scr[...], HIGHEST)
                 + ebl_row * jnp.sum(ds * s, axis=0, keepdims=True))
        dkk = dk - dlogf / (1.0 - kk)
        daf_ref[...] = dkk * (1.0 - lb) * (-(sneg * (1.0 - sneg)))
        dlb_ref[...] += jnp.sum(dkk * (-sneg), axis=0, keepdims=True)
        daq_ref[...] = dq * _silu_grad(aq, sq)
        ds_scr[...] = ds * ebl_row + _dot_tn(do, qe)

    out_col = pl.BlockSpec((c, A_D), lambda h, n: (cidx(n), h))
    return pl.pallas_call(
        body, name="hgrn_bwd", grid=(A_HEADS, nch),
        in_specs=[col(0), col(1), col(2), col(3), vec, vec, mask_spec, dmat_spec,
                  pl.BlockSpec((None, None, A_D, A_D), lambda h, n: (h, cidx(n), 0, 0)),
                  out_col],
        out_specs=[out_col, out_col, out_col, out_col, vec, vec],
        out_shape=[_sds((t, GROUP))] * 4 + [_sds((1, GROUP))] * 2,
        scratch_shapes=[pltpu.VMEM((A_D, A_D), F32), pltpu.VMEM(((nl + 2) * c, A_D), F32)],
        compiler_params=_params(("arbitrary", "arbitrary")),
    )(proj, proj, proj, proj, lb, gain, masks, dmat, states, dya)


def _sb_consts():
    j = np.arange(SB_BLOCK)
    strict = (j[:, None] > j[None, :]).astype(np.float32)
    incl = (j[:, None] >= j[None, :]).astype(np.float32)
    return jnp.asarray(strict, BF16), jnp.asarray(incl, BF16)


def _sb_tile(q, k_blk, strict, carry, diag_shift):
    n = SB_BLOCK
    z = _dot_nt(q, k_blk)
    row = lax.broadcasted_iota(jnp.int32, (n, n), 0)
    col = lax.broadcasted_iota(jnp.int32, (n, n), 1)
    mask = (col - row) < diag_shift
    sp_raw = jnp.maximum(z, 0.0) + jnp.log(1.0 + jnp.exp(-jnp.abs(z)))
    sp = jnp.where(mask, sp_raw, 0.0)
    hi, lo = _split(sp)
    cs = _dot(hi, strict) + _dot(lo, strict)
    logsig = z - sp_raw
    w = jnp.where(mask, jnp.exp(logsig - cs - carry), 0.0)
    return mask, sp, logsig, w


def _sb_fwd(q, k, v, bg, gain):
    t = q.shape[1]
    n = SB_BLOCK
    strict, _ = _sb_consts()

    def body(q_ref, k_ref, v_ref, bg_ref, gain_ref, m_ref, o_ref, y_ref):
        qi = pl.program_id(1)
        qb = q_ref[...]
        strict_m = m_ref[...]

        def step(j, state):
            carry, acc = state
            off = pl.multiple_of((qi - j) * n, n)
            k_blk = k_ref[pl.ds(off, n), :]
            v_blk = v_ref[pl.ds(off, n), :]
            _, sp, _, w = _sb_tile(qb, k_blk, strict_m, carry, j * n)
            whi, wlo = _split(w)
            acc = acc + _dot(whi, v_blk) + _dot(wlo, v_blk)
            return carry + jnp.sum(sp, axis=1, keepdims=True), acc

        _, o = lax.fori_loop(0, qi + 1, step, (jnp.zeros((n, 1), F32), jnp.zeros((n, B_D), F32)))
        o_ref[...] = o
        oh, _ = _rms(o)
        bg = bg_ref[...]
        y_ref[...] = oh * gain_ref[...] * (bg * _sigmoid(bg))

    blk = pl.BlockSpec((None, n, B_D), lambda h, i: (h, i, 0))
    full = pl.BlockSpec((None, t, B_D), lambda h, i: (h, 0, 0))
    return pl.pallas_call(
        body, name="sb_fwd", grid=(B_HEADS, t // n),
        in_specs=[blk, full, full, blk, pl.BlockSpec((None, 1, B_D), lambda h, i: (h, 0, 0)),
                  pl.BlockSpec((n, n), lambda h, i: (0, 0))],
        out_specs=[blk, blk],
        out_shape=[_sds((B_HEADS, t, B_D)), _sds((B_HEADS, t, B_D))],
        compiler_params=_params(("arbitrary", "arbitrary")),
    )(q, k, v, bg, gain, strict)


def _sb_bwd(q, k, v, o, dy, bg, gain):
    t = q.shape[1]
    n = SB_BLOCK
    strict, incl = _sb_consts()

    def body(q_ref, k_ref, v_ref, o_ref, dy_ref, bg_ref, gain_ref, ms_ref, mi_ref,
             dq_ref, dk_ref, dv_ref, dbg_ref, dgain_ref):
        qi = pl.program_id(1)

        @pl.when(qi == 0)
        def _():
            dk_ref[...] = jnp.zeros_like(dk_ref)
            dv_ref[...] = jnp.zeros_like(dv_ref)
            dgain_ref[...] = jnp.zeros_like(dgain_ref)

        qb = q_ref[...]
        strict_m = ms_ref[...]
        incl_m = mi_ref[...]
        o = o_ref[...]
        dy = dy_ref[...]
        bg = bg_ref[...]
        gain = gain_ref[...]
        oh, r = _rms(o)
        sig = _sigmoid(bg)
        sg = bg * sig
        dbg_ref[...] = dy * oh * gain * _silu_grad(bg, sig)
        dgain_ref[...] += jnp.sum(dy * oh * sg, axis=0, keepdims=True)
        do = _bf(_rms_bwd(dy * gain * sg, oh, r))
        total = jnp.sum(do.astype(F32) * o, axis=1, keepdims=True)

        def step(j, state):
            carry, gcarry, dq = state
            off = pl.multiple_of((qi - j) * n, n)
            k_blk = k_ref[pl.ds(off, n), :]
            v_blk = v_ref[pl.ds(off, n), :]
            mask, sp, logsig, w = _sb_tile(qb, k_blk, strict_m, carry, j * n)
            g = _dot_nt(do, v_blk) * w
            ghi, glo = _split(g)
            before = total - gcarry - (_dot(ghi, incl_m) + _dot(glo, incl_m))
            sig_z = jnp.exp(logsig)
            dz = _bf(jnp.where(mask, g * (1.0 - sig_z) - sig_z * before, 0.0))
            dk_ref[pl.ds(off, n), :] += _dot_tn(dz, qb)
            dv_ref[pl.ds(off, n), :] += _dot_tn(_bf(w), do)
            return (carry + jnp.sum(sp, axis=1, keepdims=True), gcarry + jnp.sum(g, axis=1, keepdims=True),
                    dq + _dot(dz, k_blk))

        zero = jnp.zeros((n, 1), F32)
        _, _, dq = lax.fori_loop(0, qi + 1, step, (zero, zero, jnp.zeros((n, B_D), F32)))
        dq_ref[...] = dq

    blk = pl.BlockSpec((None, n, B_D), lambda h, i: (h, i, 0))
    full = pl.BlockSpec((None, t, B_D), lambda h, i: (h, 0, 0))
    vec = pl.BlockSpec((None, 1, B_D), lambda h, i: (h, 0, 0))
    mat = pl.BlockSpec((n, n), lambda h, i: (0, 0))
    return pl.pallas_call(
        body, name="sb_bwd", grid=(B_HEADS, t // n),
        in_specs=[blk, full, full, blk, blk, blk, vec, mat, mat],
        out_specs=[blk, full, full, blk, vec],
        out_shape=[_sds((B_HEADS, t, B_D))] * 4 + [_sds((B_HEADS, 1, B_D))],
        compiler_params=_params(("arbitrary", "arbitrary")),
    )(q, k, v, o, dy, bg, gain, strict, incl)


def _to_heads(x, dtype=F32):
    t = x.shape[0]
    return x.reshape(t, B_HEADS, B_D).transpose(1, 0, 2).astype(dtype)


def _from_heads(x):
    return x.transpose(1, 0, 2).reshape(x.shape[1], GROUP)


def _adamw(w, g, m, v):
    rows, cols = w.shape
    tr = rows
    for cand in (400, 256, 128, 64, 32, 16, 8):
        if rows % cand == 0:
            tr = cand
            break

    def body(w_ref, g_ref, m_ref, v_ref, d_ref, nm_ref, nv_ref):
        g_ = g_ref[...]
        m_ = ADAM_B1 * m_ref[...] + (1.0 - ADAM_B1) * g_
        v_ = ADAM_B2 * v_ref[...] + (1.0 - ADAM_B2) * (g_ * g_)
        m_hat = m_ / (1.0 - ADAM_B1 ** ADAM_STEP)
        v_hat = v_ / (1.0 - ADAM_B2 ** ADAM_STEP)
        d_ref[...] = -ADAM_LR * (m_hat / (jnp.sqrt(v_hat) + ADAM_EPS) + ADAM_WD * w_ref[...])
        nm_ref[...] = m_
        nv_ref[...] = v_

    spec = pl.BlockSpec((tr, cols), lambda i: (i, 0))
    return pl.pallas_call(
        body, name="adamw", grid=(rows // tr,), in_specs=[spec] * 4, out_specs=[spec] * 3,
        out_shape=[_sds((rows, cols))] * 3, compiler_params=_params(("arbitrary",)),
    )(w, g, m, v)


def _sum_leading(x):
    n, rows, cols = x.shape
    tr = 400 if rows % 400 == 0 else 200

    def body(x_ref, o_ref):
        acc = x_ref[0]
        for i in range(1, n):
            acc = acc + x_ref[i]
        o_ref[...] = acc

    return pl.pallas_call(
        body, name="sum_leading", grid=(rows // tr,),
        in_specs=[pl.BlockSpec((n, tr, cols), lambda i: (0, i, 0))],
        out_specs=pl.BlockSpec((tr, cols), lambda i: (i, 0)),
        out_shape=_sds((rows, cols)), compiler_params=_params(("arbitrary",)),
    )(x)


_ANY = pl.BlockSpec(memory_space=pl.ANY)


def _place():
    return lax.axis_index("x"), lax.axis_index("y"), lax.axis_index("c")


def _gather_weights(flat):
    half = HALF_FLAT

    def body(x_ref, out_ref, send_sems, recv_sems, local_sem):
        x, y, c = _place()
        me = 2 * x + y
        sibling = (x, y, 1 - c)

        def rows(shard, hc):
            return out_ref.at[shard, pl.ds(hc * half, half), :]

        def copy(k, shard, hc, to, src=None):
            return pltpu.make_async_remote_copy(
                src_ref=rows(shard, hc) if src is None else src, dst_ref=rows(shard, hc),
                send_sem=send_sems.at[k], recv_sem=recv_sems.at[k], device_id=to, device_id_type=MESH)

        mine = pltpu.make_async_copy(x_ref, out_ref.at[me], local_sem)
        mine.start()
        peers = [me ^ k for k in (1, 2, 3)]
        first = [copy(k, me, c, (p >> 1, p & 1, c), src=x_ref.at[pl.ds(c * half, half), :])
                 for k, p in enumerate(peers)]
        for cp in first:
            cp.start()
        passed = [copy(3 + k, p, c, sibling) for k, p in enumerate(peers)]
        for k, p in enumerate(peers):
            copy(k, p, c, sibling).wait_recv()
            passed[k].start()
        for k, p in enumerate(peers):
            copy(3 + k, p, 1 - c, sibling).wait_recv()
        for cp in first + passed:
            cp.wait_send()
        mine.wait()

    return pl.pallas_call(
        body, name="gather_weights", in_specs=[_ANY], out_specs=_ANY,
        out_shape=_sds((N_SHARD, ROWS_FLAT, D_MODEL), BF16),
        scratch_shapes=[pltpu.SemaphoreType.DMA((6,)), pltpu.SemaphoreType.DMA((6,)), pltpu.SemaphoreType.DMA],
    )(flat)


def _swap_halves(grads):
    half = HALF_FLAT

    def body(g_ref, out_ref, send_sem, recv_sem, local_sem):
        x, y, c = _place()
        mine = pltpu.make_async_copy(g_ref.at[:, pl.ds(c * half, half), :], out_ref.at[0], local_sem)
        mine.start()
        cp = pltpu.make_async_remote_copy(
            src_ref=g_ref.at[:, pl.ds((1 - c) * half, half), :], dst_ref=out_ref.at[1],
            send_sem=send_sem, recv_sem=recv_sem, device_id=(x, y, 1 - c), device_id_type=MESH)
        cp.start()
        cp.wait()
        mine.wait()

    return pl.pallas_call(
        body, name="swap_halves", in_specs=[_ANY], out_specs=_ANY,
        out_shape=_sds((2, N_SHARD, half, D_MODEL)),
        scratch_shapes=[pltpu.SemaphoreType.DMA, pltpu.SemaphoreType.DMA, pltpu.SemaphoreType.DMA],
    )(grads)


def _scatter_shards(part):
    def body(p_ref, out_ref, send_sems, recv_sems, local_sem):
        x, y, c = _place()
        me = 2 * x + y
        mine = pltpu.make_async_copy(p_ref.at[me], out_ref.at[me], local_sem)
        mine.start()
        peers = [me ^ k for k in (1, 2, 3)]
        sends = [pltpu.make_async_remote_copy(
            src_ref=p_ref.at[p], dst_ref=out_ref.at[me], send_sem=send_sems.at[k], recv_sem=recv_sems.at[k],
            device_id=(p >> 1, p & 1, c), device_id_type=MESH) for k, p in enumerate(peers)]
        for cp in sends:
            cp.start()
        for k, p in enumerate(peers):
            pltpu.make_async_remote_copy(
                src_ref=p_ref.at[p], dst_ref=out_ref.at[p], send_sem=send_sems.at[k], recv_sem=recv_sems.at[k],
                device_id=(p >> 1, p & 1, c), device_id_type=MESH).wait_recv()
        for cp in sends:
            cp.wait_send()
        mine.wait()

    return pl.pallas_call(
        body, name="scatter_shards", in_specs=[_ANY], out_specs=_ANY,
        out_shape=_sds((N_SHARD, HALF_FLAT, D_MODEL)),
        scratch_shapes=[pltpu.SemaphoreType.DMA((3,)), pltpu.SemaphoreType.DMA((3,)), pltpu.SemaphoreType.DMA],
    )(part)


def _join_halves(mine_half):
    half = HALF_FLAT

    def body(r_ref, out_ref, send_sem, recv_sem, local_sem):
        x, y, c = _place()
        mine = pltpu.make_async_copy(r_ref, out_ref.at[pl.ds(c * half, half), :], local_sem)
        mine.start()
        send = pltpu.make_async_remote_copy(
            src_ref=r_ref, dst_ref=out_ref.at[pl.ds(c * half, half), :],
            send_sem=send_sem, recv_sem=recv_sem, device_id=(x, y, 1 - c), device_id_type=MESH)
        send.start()
        pltpu.make_async_remote_copy(
            src_ref=r_ref, dst_ref=out_ref.at[pl.ds((1 - c) * half, half), :],
            send_sem=send_sem, recv_sem=recv_sem, device_id=(x, y, 1 - c), device_id_type=MESH).wait_recv()
        send.wait_send()
        mine.wait()

    return pl.pallas_call(
        body, name="join_halves", in_specs=[_ANY], out_specs=_ANY,
        out_shape=_sds((ROWS_FLAT, D_MODEL)),
        scratch_shapes=[pltpu.SemaphoreType.DMA, pltpu.SemaphoreType.DMA, pltpu.SemaphoreType.DMA],
    )(mine_half)


def _allreduce_small(vec):
    def body(v_ref, out_ref, buf, send_sems, recv_sems):
        x, y, c = _place()
        me = 4 * x + 2 * y + c
        buf[me] = v_ref[...]
        peers = [me ^ k for k in range(1, N_DEV)]
        sends = [pltpu.make_async_remote_copy(
            src_ref=v_ref, dst_ref=buf.at[me], send_sem=send_sems.at[k], recv_sem=recv_sems.at[k],
            device_id=(p >> 2, (p >> 1) & 1, p & 1), device_id_type=MESH) for k, p in enumerate(peers)]
        for cp in sends:
            cp.start()
        for k, p in enumerate(peers):
            pltpu.make_async_remote_copy(
                src_ref=v_ref, dst_ref=buf.at[p], send_sem=send_sems.at[k], recv_sem=recv_sems.at[k],
                device_id=(p >> 2, (p >> 1) & 1, p & 1), device_id_type=MESH).wait_recv()
        for cp in sends:
            cp.wait_send()
        acc = buf[0]
        for d in range(1, N_DEV):
            acc = acc + buf[d]
        out_ref[...] = acc

    vm = pl.BlockSpec(memory_space=pltpu.VMEM)
    return pl.pallas_call(
        body, name="allreduce_small", in_specs=[vm], out_specs=vm, out_shape=_sds((SMALL_ROWS, 128)),
        scratch_shapes=[pltpu.VMEM((N_DEV, SMALL_ROWS, 128), F32),
                        pltpu.SemaphoreType.DMA((N_DEV - 1,)), pltpu.SemaphoreType.DMA((N_DEV - 1,))],
    )(vec)


def _flatten_shard(w_in, w_out, w_pg, w_pp):
    return jnp.concatenate([w_in.reshape(-1, D_MODEL), w_out.reshape(-1, D_MODEL), w_pg.reshape(-1, D_MODEL),
                            w_pp.reshape(-1, D_MODEL)], axis=0)


def _unflatten_shard(flat):
    a, b, c = ROWS_W_IN, ROWS_W_IN + ROWS_W_OUT, ROWS_W_IN + ROWS_W_OUT + ROWS_W_PG
    q = D_MODEL // N_SHARD
    return (flat[:a].reshape(2, D_MODEL, D_MODEL), flat[a:b].reshape(2, q, D_MODEL),
            flat[b:c].reshape(2, q, D_MODEL), flat[c:].reshape(2, D_PLE, q))


def _full_weights(gathered):
    a, b, c = ROWS_W_IN, ROWS_W_IN + ROWS_W_OUT, ROWS_W_IN + ROWS_W_OUT + ROWS_W_PG
    q = D_MODEL // N_SHARD
    rpp = ROWS_W_PP // 2
    out = []
    for l in range(2):
        w_in = gathered[:, l * D_MODEL:(l + 1) * D_MODEL, :].transpose(1, 0, 2).reshape(D_MODEL, D_IN)
        w_out = gathered[:, a + l * q:a + (l + 1) * q, :].reshape(D_MODEL, D_MODEL)
        w_pg = gathered[:, b + l * q:b + (l + 1) * q, :].reshape(D_MODEL, D_MODEL)
        w_pp = gathered[:, c + l * rpp:c + (l + 1) * rpp, :].reshape(N_SHARD, D_PLE, q).transpose(1, 0, 2)
        out.append((w_in, w_out, w_pg, w_pp.reshape(D_PLE, D_MODEL)))
    return out


def _shard_major(dw_in, dw_out, dw_pg, dw_pp):
    q = D_MODEL // N_SHARD
    rpp = ROWS_W_PP // 2
    parts = []
    for l in range(2):
        g = dw_in[l].reshape(N_SHARD, 2, D_MODEL, GROUP).transpose(0, 2, 1, 3).reshape(N_SHARD, D_MODEL, D_MODEL)
        parts.append(g)
    for l in range(2):
        parts.append(dw_out[l].reshape(N_SHARD, q, D_MODEL))
    for l in range(2):
        parts.append(dw_pg[l].reshape(N_SHARD, q, D_MODEL))
    for l in range(2):
        parts.append(dw_pp[l].reshape(D_PLE, N_SHARD, q).transpose(1, 0, 2).reshape(N_SHARD, rpp, D_MODEL))
    return jnp.concatenate(parts, axis=1)


def _lower_bounds(lb_logits):
    sm = jax.nn.softmax(lb_logits.astype(F32), axis=0)
    return jnp.cumsum(sm, axis=0) - sm[0:1]


def kernel(x, p, norm_mix, w_in, a_out_norm, b_out_norm, w_out, lb_logits, ple_gate_norm, w_ple_gate, w_ple_proj, ple_post_norm, final_norm, loss_target, m_norm_mix, m_w_in, m_a_out_norm, m_b_out_norm, m_w_out, m_lb_logits, m_ple_gate_norm, m_w_ple_gate, m_w_ple_proj, m_ple_post_norm, m_final_norm, v_norm_mix, v_w_in, v_a_out_norm, v_b_out_norm, v_w_out, v_lb_logits, v_ple_gate_norm, v_w_ple_gate, v_w_ple_proj, v_ple_post_norm, v_final_norm):
    t = x.shape[1]
    h0 = x.reshape(t, D_MODEL)
    target = loss_target.reshape(t, D_MODEL)
    pl_in = p.reshape(2, t, D_PLE)

    w_flat = _flatten_shard(w_in, w_out, w_ple_gate, w_ple_proj)
    weights = _full_weights(_gather_weights(w_flat.astype(BF16)))
    lbs, lbs_vjp = jax.vjp(_lower_bounds, lb_logits)

    saved = []
    h = h0
    for l in range(2):
        wi, wo, wpg, wpp = weights[l]
        g_mix = norm_mix[l].reshape(1, D_MODEL)
        lb = lbs[l].reshape(1, GROUP)
        ga = a_out_norm[l].reshape(1, GROUP)
        gb = b_out_norm[l].reshape(B_HEADS, 1, B_D)
        proj = _inproj(h, g_mix, wi)
        ya, states = _hgrn_fwd(proj, lb, ga)
        bq = _to_heads(proj[:, 4 * GROUP:5 * GROUP] * (B_D ** -0.5), BF16)
        bk = _to_heads(proj[:, 5 * GROUP:6 * GROUP], BF16)
        bv = _to_heads(proj[:, 6 * GROUP:7 * GROUP], BF16)
        bg = _to_heads(proj[:, 7 * GROUP:8 * GROUP])
        ob, yb_h = _sb_fwd(bq, bk, bv, bg, gb)
        yb = _from_heads(yb_h)
        h1 = _outproj(h, ya, yb, wo)
        g_post = ple_post_norm[l].reshape(1, D_MODEL)
        g_gate = ple_gate_norm[l].reshape(1, D_MODEL)
        h2 = _ple_fwd(h1, pl_in[l], wpp, wpg, g_post, g_gate)
        saved.append((h, proj, states, ya, yb, bq, bk, bv, bg, ob, h1))
        h = h2

    dh, d_final, loss_part = _final(h, final_norm.reshape(1, D_MODEL), target)

    dw_in_l, dw_out_l, dw_pg_l, dw_pp_l = [None] * 2, [None] * 2, [None] * 2, [None] * 2
    d_mix, d_a, d_b, d_lb, d_gate, d_post = [None] * 2, [None] * 2, [None] * 2, [None] * 2, [None] * 2, [None] * 2
    for l in (1, 0):
        wi, wo, wpg, wpp = weights[l]
        h_in, proj, states, ya, yb, bq, bk, bv, bg, ob, h1 = saved[l]
        g_mix = norm_mix[l].reshape(1, D_MODEL)
        lb = lbs[l].reshape(1, GROUP)
        ga = a_out_norm[l].reshape(1, GROUP)
        gb = b_out_norm[l].reshape(B_HEADS, 1, B_D)
        g_post = ple_post_norm[l].reshape(1, D_MODEL)
        g_gate = ple_gate_norm[l].reshape(1, D_MODEL)
        dh1, dw_pg_l[l], dw_pp_l[l], d_gate[l], d_post[l] = _ple_bwd(dh, h1, pl_in[l], wpp, wpg, g_post, g_gate)
        dya, dyb, dw_out_l[l] = _outproj_bwd(dh1, ya, yb, wo)
        dbq, dbk, dbv, dbg, dgb = _sb_bwd(bq, bk, bv, ob, _to_heads(dyb), bg, gb)
        daq, daf, dai, dag, d_lb[l], d_a[l] = _hgrn_bwd(proj, lb, ga, states, dya)
        d_b[l] = dgb.reshape(1, GROUP)
        parts = [daq, daf, dai, dag, _from_heads(dbq) * (B_D ** -0.5), _from_heads(dbk), _from_heads(dbv),
                 _from_heads(dbg)]
        dw_in_l[l] = _inproj_bwd_dw(h_in, g_mix, parts)
        dh, d_mix[l] = _inproj_bwd_dx(dh1, h_in, g_mix, wi, parts)
    grad_x = dh.reshape(x.shape)

    swapped = _swap_halves(_shard_major(dw_in_l, dw_out_l, dw_pg_l, dw_pp_l))
    chip_sum = _sum_leading(swapped.reshape(2, N_SHARD * HALF_FLAT, D_MODEL)).reshape(N_SHARD, HALF_FLAT, D_MODEL)
    g_flat = _join_halves(_sum_leading(_scatter_shards(chip_sum)))
    g_w_in, g_w_out, g_w_pg, g_w_pp = _unflatten_shard(g_flat)

    small = jnp.concatenate([
        jnp.concatenate(d_mix).reshape(-1, 128), jnp.concatenate(d_a).reshape(-1, 128),
        jnp.concatenate(d_b).reshape(-1, 128), jnp.concatenate(d_lb).reshape(-1, 128),
        jnp.concatenate(d_gate).reshape(-1, 128), jnp.concatenate(d_post).reshape(-1, 128),
        d_final.reshape(-1, 128), jnp.broadcast_to(loss_part, (8, 128))], axis=0)
    small = _allreduce_small(small)
    loss = small[80, 0]
    g_norm_mix = small[0:16].reshape(2, D_MODEL)
    g_a = small[16:24].reshape(2, GROUP)
    g_b = small[24:32].reshape(2, GROUP)
    (g_lb,) = lbs_vjp(small[32:40].reshape(2, GROUP))
    g_gate = small[40:56].reshape(2, D_MODEL)
    g_post = small[56:72].reshape(2, D_MODEL)
    g_final = small[72:80].reshape(D_MODEL)

    d_flat, nm_flat, nv_flat = _adamw(w_flat, g_flat, _flatten_shard(m_w_in, m_w_out, m_w_ple_gate, m_w_ple_proj),
                                      _flatten_shard(v_w_in, v_w_out, v_w_ple_gate, v_w_ple_proj))
    d_w_in, d_w_out, d_w_pg, d_w_pp = _unflatten_shard(d_flat)
    nm_w_in, nm_w_out, nm_w_pg, nm_w_pp = _unflatten_shard(nm_flat)
    nv_w_in, nv_w_out, nv_w_pg, nv_w_pp = _unflatten_shard(nv_flat)

    small_w = [norm_mix, a_out_norm, b_out_norm, lb_logits, ple_gate_norm, ple_post_norm, final_norm]
    small_g = [g_norm_mix, g_a, g_b, g_lb, g_gate, g_post, g_final]
    small_m = [m_norm_mix, m_a_out_norm, m_b_out_norm, m_lb_logits, m_ple_gate_norm, m_ple_post_norm, m_final_norm]
    small_v = [v_norm_mix, v_a_out_norm, v_b_out_norm, v_lb_logits, v_ple_gate_norm, v_ple_post_norm, v_final_norm]
    pack = lambda arrs: jnp.concatenate([a.reshape(-1, 128) for a in arrs], axis=0)
    ds, nms, nvs = _adamw(pack(small_w), pack(small_g), pack(small_m), pack(small_v))

    def unpack(packed):
        out, r = [], 0
        for a in small_w:
            n = a.size // 128
            out.append(packed[r:r + n].reshape(a.shape))
            r += n
        return out

    d_s, nm_s, nv_s = unpack(ds), unpack(nms), unpack(nvs)

    def ordered(s, big):
        return [s[0], big[0], s[1], s[2], big[1], s[3], s[4], big[2], big[3], s[5], s[6]]

    grads = ordered(small_g, [g_w_in, g_w_out, g_w_pg, g_w_pp])
    deltas = ordered(d_s, [d_w_in, d_w_out, d_w_pg, d_w_pp])
    new_m = ordered(nm_s, [nm_w_in, nm_w_out, nm_w_pg, nm_w_pp])
    new_v = ordered(nv_s, [nv_w_in, nv_w_out, nv_w_pg, nv_w_pp])
    return (loss, grad_x, *grads, *deltas, *new_m, *new_v)
```

```python
import functools
import math

import numpy as np
import jax
import jax.numpy as jnp
from jax import lax
from jax.experimental import pallas as pl
from jax.experimental.pallas import tpu as pltpu

F32 = jnp.float32
BF16 = jnp.bfloat16
MESH = pl.DeviceIdType.MESH
HIGHEST = lax.Precision.HIGHEST

D_MODEL = 1024
D_PLE = 256
D_IN = 4096
A_HEADS, A_D = 4, 128
B_HEADS, B_D = 8, 64
GROUP = 512
EPS = 1e-6
N_SHARD = 4
N_DEV = 8

HG_CHUNK = 128
HG_LEVELS = 7
SB_TQ = 512
SB_TK = 128

ADAM_LR, ADAM_B1, ADAM_B2, ADAM_EPS, ADAM_WD, ADAM_STEP = 0.001, 0.9, 0.999, 1e-08, 0.01, 10

VMEM_LIMIT = 48 * 1024 * 1024

ROWS_W_IN = 2 * D_MODEL
ROWS_W_OUT = 2 * (D_MODEL // N_SHARD)
ROWS_W_PG = 2 * (D_MODEL // N_SHARD)
ROWS_W_PP = 2 * (D_PLE * (D_MODEL // N_SHARD) // D_MODEL)
ROWS_FLAT = ROWS_W_IN + ROWS_W_OUT + ROWS_W_PG + ROWS_W_PP
HALF_FLAT = ROWS_FLAT // 2

SMALL_ROWS = 88


def _sds(shape, dtype=F32):
    return jax.ShapeDtypeStruct(shape, dtype)


def _params(sem=None):
    kw = dict(vmem_limit_bytes=VMEM_LIMIT)
    if sem is not None:
        kw["dimension_semantics"] = sem
    return pltpu.CompilerParams(**kw)


def _dot(a, b, precision=None):
    return lax.dot_general(a, b, (((1,), (0,)), ((), ())), preferred_element_type=F32, precision=precision)


def _dot_nt(a, b, precision=None):
    return lax.dot_general(a, b, (((1,), (1,)), ((), ())), preferred_element_type=F32, precision=precision)


def _dot_tn(a, b, precision=None):
    return lax.dot_general(a, b, (((0,), (0,)), ((), ())), preferred_element_type=F32, precision=precision)


def _bf(x):
    return x.astype(BF16)


def _split(x):
    hi = x.astype(BF16)
    lo = (x - hi.astype(F32)).astype(BF16)
    return hi, lo


def _rms(x):
    r = lax.rsqrt(jnp.mean(x * x, axis=-1, keepdims=True) + EPS)
    return x * r, r


def _rms_bwd(dxh, xh, r):
    return r * (dxh - xh * jnp.mean(dxh * xh, axis=-1, keepdims=True))


def _sigmoid(x):
    return 1.0 / (1.0 + jnp.exp(-x))


def _silu_grad(x, sig):
    return sig * (1.0 + x * (1.0 - sig))


def _row_tile(t, want):
    return min(t, want)


def _inproj(h, g, w):
    t = h.shape[0]
    tm = _row_tile(t, 512)
    tn = 1024

    def body(h_ref, g_ref, w_ref, o_ref):
        xh, _ = _rms(h_ref[...])
        o_ref[...] = _dot(_bf(xh * g_ref[...]), w_ref[...])

    return pl.pallas_call(
        body, name="inproj", grid=(D_IN // tn, t // tm),
        in_specs=[pl.BlockSpec((tm, D_MODEL), lambda j, i: (i, 0)),
                  pl.BlockSpec((1, D_MODEL), lambda j, i: (0, 0)),
                  pl.BlockSpec((D_MODEL, tn), lambda j, i: (0, j))],
        out_specs=pl.BlockSpec((tm, tn), lambda j, i: (i, j)),
        out_shape=_sds((t, D_IN)), compiler_params=_params(("arbitrary", "arbitrary")),
    )(h, g, w)


def _outproj(h, ya, yb, w):
    t = h.shape[0]
    tm = _row_tile(t, 512)

    def body(h_ref, ya_ref, yb_ref, w_ref, o_ref):
        o_ref[...] = (h_ref[...] + _dot(_bf(ya_ref[...]), w_ref[pl.ds(0, GROUP), :])
                      + _dot(_bf(yb_ref[...]), w_ref[pl.ds(GROUP, GROUP), :]))

    return pl.pallas_call(
        body, name="outproj", grid=(t // tm,),
        in_specs=[pl.BlockSpec((tm, D_MODEL), lambda i: (i, 0)),
                  pl.BlockSpec((tm, GROUP), lambda i: (i, 0)),
                  pl.BlockSpec((tm, GROUP), lambda i: (i, 0)),
                  pl.BlockSpec((D_MODEL, D_MODEL), lambda i: (0, 0))],
        out_specs=pl.BlockSpec((tm, D_MODEL), lambda i: (i, 0)),
        out_shape=_sds((t, D_MODEL)), compiler_params=_params(("arbitrary",)),
    )(h, ya, yb, w)


def _ple_fwd(h, p, w_pp, w_pg, g_post, g_gate):
    t = h.shape[0]
    tm = _row_tile(t, 256)

    def body(h_ref, p_ref, wpp_ref, wpg_ref, gp_ref, gg_ref, o_ref):
        x = h_ref[...]
        ph, _ = _rms(_dot(_bf(p_ref[...]), wpp_ref[...]))
        xh, _ = _rms(x)
        gate = _sigmoid(_dot(_bf(xh * gg_ref[...]), wpg_ref[...]))
        o_ref[...] = x + gate * (ph * gp_ref[...])

    return pl.pallas_call(
        body, name="ple_fwd", grid=(t // tm,),
        in_specs=[pl.BlockSpec((tm, D_MODEL), lambda i: (i, 0)),
                  pl.BlockSpec((tm, D_PLE), lambda i: (i, 0)),
                  pl.BlockSpec((D_PLE, D_MODEL), lambda i: (0, 0)),
                  pl.BlockSpec((D_MODEL, D_MODEL), lambda i: (0, 0)),
                  pl.BlockSpec((1, D_MODEL), lambda i: (0, 0)),
                  pl.BlockSpec((1, D_MODEL), lambda i: (0, 0))],
        out_specs=pl.BlockSpec((tm, D_MODEL), lambda i: (i, 0)),
        out_shape=_sds((t, D_MODEL)), compiler_params=_params(("arbitrary",)),
    )(h, p, w_pp, w_pg, g_post, g_gate)


def _ple_bwd(dh2, h, p, w_pp, w_pg, g_post, g_gate):
    t = h.shape[0]
    tm = _row_tile(t, 256)

    def body(d_ref, h_ref, p_ref, wpp_ref, wpg_ref, gp_ref, gg_ref, dh_ref, dwpg_ref, dwpp_ref, dgg_ref, dgp_ref):
        @pl.when(pl.program_id(0) == 0)
        def _():
            dwpg_ref[...] = jnp.zeros_like(dwpg_ref)
            dwpp_ref[...] = jnp.zeros_like(dwpp_ref)
            dgg_ref[...] = jnp.zeros_like(dgg_ref)
            dgp_ref[...] = jnp.zeros_like(dgp_ref)

        d = d_ref[...]
        x = h_ref[...]
        gp = gp_ref[...]
        gg = gg_ref[...]
        pb = _bf(p_ref[...])
        ph, rp = _rms(_dot(pb, wpp_ref[...]))
        pe = ph * gp
        xh, rx = _rms(x)
        un = _bf(xh * gg)
        gate = _sigmoid(_dot(un, wpg_ref[...]))
        dgpre = _bf(d * pe * gate * (1.0 - gate))
        dun = _dot_nt(dgpre, wpg_ref[...])
        dh_ref[...] = d + _rms_bwd(dun * gg, xh, rx)
        dgg_ref[...] += jnp.sum(dun * xh, axis=0, keepdims=True)
        dwpg_ref[...] += _dot_tn(un, dgpre)
        dpe = d * gate
        dgp_ref[...] += jnp.sum(dpe * ph, axis=0, keepdims=True)
        dwpp_ref[...] += _dot_tn(pb, _bf(_rms_bwd(dpe * gp, ph, rp)))

    return pl.pallas_call(
        body, name="ple_bwd", grid=(t // tm,),
        in_specs=[pl.BlockSpec((tm, D_MODEL), lambda i: (i, 0)),
                  pl.BlockSpec((tm, D_MODEL), lambda i: (i, 0)),
                  pl.BlockSpec((tm, D_PLE), lambda i: (i, 0)),
                  pl.BlockSpec((D_PLE, D_MODEL), lambda i: (0, 0)),
                  pl.BlockSpec((D_MODEL, D_MODEL), lambda i: (0, 0)),
                  pl.BlockSpec((1, D_MODEL), lambda i: (0, 0)),
                  pl.BlockSpec((1, D_MODEL), lambda i: (0, 0))],
        out_specs=[pl.BlockSpec((tm, D_MODEL), lambda i: (i, 0)),
                   pl.BlockSpec((D_MODEL, D_MODEL), lambda i: (0, 0)),
                   pl.BlockSpec((D_PLE, D_MODEL), lambda i: (0, 0)),
                   pl.BlockSpec((1, D_MODEL), lambda i: (0, 0)),
                   pl.BlockSpec((1, D_MODEL), lambda i: (0, 0))],
        out_shape=[_sds((t, D_MODEL)), _sds((D_MODEL, D_MODEL)), _sds((D_PLE, D_MODEL)),
                   _sds((1, D_MODEL)), _sds((1, D_MODEL))],
        compiler_params=_params(("arbitrary",)),
    )(dh2, h, p, w_pp, w_pg, g_post, g_gate)


def _outproj_bwd(dh, ya, yb, w):
    t = dh.shape[0]
    tm = _row_tile(t, 512)

    def body(d_ref, ya_ref, yb_ref, w_ref, dya_ref, dyb_ref, dw_ref):
        @pl.when(pl.program_id(0) == 0)
        def _():
            dw_ref[...] = jnp.zeros_like(dw_ref)

        d = _bf(d_ref[...])
        dya_ref[...] = _dot_nt(d, w_ref[pl.ds(0, GROUP), :])
        dyb_ref[...] = _dot_nt(d, w_ref[pl.ds(GROUP, GROUP), :])
        dw_ref[pl.ds(0, GROUP), :] += _dot_tn(_bf(ya_ref[...]), d)
        dw_ref[pl.ds(GROUP, GROUP), :] += _dot_tn(_bf(yb_ref[...]), d)

    return pl.pallas_call(
        body, name="outproj_bwd", grid=(t // tm,),
        in_specs=[pl.BlockSpec((tm, D_MODEL), lambda i: (i, 0)),
                  pl.BlockSpec((tm, GROUP), lambda i: (i, 0)),
                  pl.BlockSpec((tm, GROUP), lambda i: (i, 0)),
                  pl.BlockSpec((D_MODEL, D_MODEL), lambda i: (0, 0))],
        out_specs=[pl.BlockSpec((tm, GROUP), lambda i: (i, 0)),
                   pl.BlockSpec((tm, GROUP), lambda i: (i, 0)),
                   pl.BlockSpec((D_MODEL, D_MODEL), lambda i: (0, 0))],
        out_shape=[_sds((t, GROUP)), _sds((t, GROUP)), _sds((D_MODEL, D_MODEL))],
        compiler_params=_params(("arbitrary",)),
    )(dh, ya, yb, w)


def _inproj_bwd_dx(dres, h, g, w, parts):
    t = h.shape[0]
    tm = _row_tile(t, 256)

    def body(dres_ref, h_ref, g_ref, w_ref, *rest):
        part_refs, (dh_ref, dg_ref) = rest[:8], rest[8:]

        @pl.when(pl.program_id(0) == 0)
        def _():
            dg_ref[...] = jnp.zeros_like(dg_ref)

        du = jnp.zeros((tm, D_MODEL), F32)
        for i, pr in enumerate(part_refs):
            du = du + _dot_nt(_bf(pr[...]), w_ref[:, pl.ds(i * GROUP, GROUP)])
        xh, r = _rms(h_ref[...])
        dg_ref[...] += jnp.sum(du * xh, axis=0, keepdims=True)
        dh_ref[...] = dres_ref[...] + _rms_bwd(du * g_ref[...], xh, r)

    return pl.pallas_call(
        body, name="inproj_bwd_dx", grid=(t // tm,),
        in_specs=[pl.BlockSpec((tm, D_MODEL), lambda i: (i, 0)),
                  pl.BlockSpec((tm, D_MODEL), lambda i: (i, 0)),
                  pl.BlockSpec((1, D_MODEL), lambda i: (0, 0)),
                  pl.BlockSpec((D_MODEL, D_IN), lambda i: (0, 0))]
                 + [pl.BlockSpec((tm, GROUP), lambda i: (i, 0)) for _ in range(8)],
        out_specs=[pl.BlockSpec((tm, D_MODEL), lambda i: (i, 0)),
                   pl.BlockSpec((1, D_MODEL), lambda i: (0, 0))],
        out_shape=[_sds((t, D_MODEL)), _sds((1, D_MODEL))],
        compiler_params=_params(("arbitrary",)),
    )(dres, h, g, w, *parts)


def _inproj_bwd_dw(h, g, parts):
    t = h.shape[0]
    tm = _row_tile(t, 512)
    stacked = jnp.stack(parts)

    def body(h_ref, g_ref, d_ref, dw_ref):
        @pl.when(pl.program_id(1) == 0)
        def _():
            dw_ref[...] = jnp.zeros_like(dw_ref)

        xh, _ = _rms(h_ref[...])
        dw_ref[...] += _dot_tn(_bf(xh * g_ref[...]), _bf(d_ref[...]))

    return pl.pallas_call(
        body, name="inproj_bwd_dw", grid=(8, t // tm),
        in_specs=[pl.BlockSpec((tm, D_MODEL), lambda j, i: (i, 0)),
                  pl.BlockSpec((1, D_MODEL), lambda j, i: (0, 0)),
                  pl.BlockSpec((None, tm, GROUP), lambda j, i: (j, i, 0))],
        out_specs=pl.BlockSpec((None, D_MODEL, GROUP), lambda j, i: (j, 0, 0)),
        out_shape=_sds((8, D_MODEL, GROUP)), compiler_params=_params(("arbitrary", "arbitrary")),
    )(h, g, stacked)


def _final(h, g, target):
    t = h.shape[0]
    tm = _row_tile(t, 512)

    def body(h_ref, g_ref, t_ref, dh_ref, dg_ref, loss_ref):
        @pl.when(pl.program_id(0) == 0)
        def _():
            dg_ref[...] = jnp.zeros_like(dg_ref)
            loss_ref[...] = jnp.zeros_like(loss_ref)

        xh, r = _rms(h_ref[...])
        gg = g_ref[...]
        err = xh * gg - t_ref[...]
        part = 0.5 * jnp.sum(jnp.mean(err * err, axis=-1, keepdims=True), axis=0, keepdims=True)
        loss_ref[...] += jnp.broadcast_to(part, loss_ref.shape)
        dy = err * (1.0 / D_MODEL)
        dg_ref[...] += jnp.sum(dy * xh, axis=0, keepdims=True)
        dh_ref[...] = _rms_bwd(dy * gg, xh, r)

    return pl.pallas_call(
        body, name="final", grid=(t // tm,),
        in_specs=[pl.BlockSpec((tm, D_MODEL), lambda i: (i, 0)),
                  pl.BlockSpec((1, D_MODEL), lambda i: (0, 0)),
                  pl.BlockSpec((tm, D_MODEL), lambda i: (i, 0))],
        out_specs=[pl.BlockSpec((tm, D_MODEL), lambda i: (i, 0)),
                   pl.BlockSpec((1, D_MODEL), lambda i: (0, 0)),
                   pl.BlockSpec((1, 128), lambda i: (0, 0))],
        out_shape=[_sds((t, D_MODEL)), _sds((1, D_MODEL)), _sds((1, 128))],
        compiler_params=_params(("arbitrary",)),
    )(h, g, target)


def _hgrn_consts():
    c, nl = HG_CHUNK, HG_LEVELS
    t = np.arange(c)
    tril = np.tril(np.ones((c, c), np.float32))
    masks = np.zeros((nl + 1, c, c), np.float32)
    masks[0] = np.eye(c, dtype=np.float32)
    dmat = np.zeros(((nl + 2) * c, c), np.float32)
    dmat[0:c] = tril
    for l in range(nl):
        m = c >> (l + 1)
        blk = t // (2 * m)
        r = blk * 2 * m + m - 1
        upper = (t % (2 * m)) >= m
        masks[l + 1] = ((blk[:, None] == blk[None, :]) & upper[:, None] & (~upper)[None, :]).astype(np.float32)
        dmat[(l + 1) * c:(l + 2) * c] = tril[t] - tril[r]
    dmat[(nl + 1) * c:] = np.triu(np.ones((c, c), np.float32), k=1)
    return jnp.asarray(masks), jnp.asarray(dmat)


def _hgrn_common(aq, af, lb, dmat_ref):
    c, nl = HG_CHUNK, HG_LEVELS
    sq = _sigmoid(aq)
    q = aq * sq
    sneg = _sigmoid(-af)
    kk = (1.0 - lb) * sneg
    logf = jnp.log1p(-kk)
    x_all = _dot(dmat_ref[pl.ds(0, (nl + 1) * c), :], logf, HIGHEST)
    b = x_all[0:c]
    b_last = jnp.sum(logf, axis=0, keepdims=True)
    return sq, q, sneg, kk, x_all, b, b_last


def _hgrn_level(x_all, l, q, kk):
    c = HG_CHUNK
    x = x_all[(l + 1) * c:(l + 2) * c]
    qf = jnp.exp(jnp.minimum(x, 0.0))
    kf = jnp.exp(-jnp.maximum(x, 0.0))
    return qf, kf, _bf(q * qf), _bf(kk * kf)


def _hgrn_scores(x_all, q, kk, mask_ref):
    p = mask_ref[0] * _dot_nt(_bf(q), _bf(kk))
    for l in range(HG_LEVELS):
        _, _, ql, kl = _hgrn_level(x_all, l, q, kk)
        p = p + mask_ref[l + 1] * _dot_nt(ql, kl)
    return p


def _hgrn_specs(n_chunks, rev):
    c = HG_CHUNK
    cidx = (lambda n: n_chunks - 1 - n) if rev else (lambda n: n)
    col = lambda g: pl.BlockSpec((c, A_D), lambda h, n: (cidx(n), g * A_HEADS + h))
    vec = pl.BlockSpec((1, A_D), lambda h, n: (0, h))
    mask = pl.BlockSpec((HG_LEVELS + 1, c, c), lambda h, n: (0, 0, 0))
    dmat = pl.BlockSpec(((HG_LEVELS + 2) * c, c), lambda h, n: (0, 0))
    return cidx, col, vec, mask, dmat


def _hgrn_fwd(proj, lb, gain):
    t = proj.shape[0]
    c = HG_CHUNK
    nch = t // c
    masks, dmat = _hgrn_consts()
    cidx, col, vec, mask_spec, dmat_spec = _hgrn_specs(nch, False)

    def body(aq_ref, af_ref, ai_ref, ag_ref, lb_ref, gain_ref, mask_ref, dmat_ref, y_ref, st_ref, s_scr):
        @pl.when(pl.program_id(1) == 0)
        def _():
            s_scr[...] = jnp.zeros_like(s_scr)

        v = ai_ref[...]
        ag = ag_ref[...]
        _, q, _, kk, x_all, b, b_last = _hgrn_common(aq_ref[...], af_ref[...], lb_ref[...], dmat_ref)
        p = _hgrn_scores(x_all, q, kk, mask_ref)
        s = s_scr[...]
        st_ref[...] = s
        vb = _bf(v)
        o = _dot(_bf(p), vb) + _dot_nt(_bf(q * jnp.exp(b)), _bf(s))
        s_scr[...] = s * jnp.exp(b_last) + _dot_tn(vb, _bf(kk * jnp.exp(b_last - b)))
        oh, _ = _rms(o)
        y_ref[...] = oh * gain_ref[...] * (ag * _sigmoid(ag))

    return pl.pallas_call(
        body, name="hgrn_fwd", grid=(A_HEADS, nch),
        in_specs=[col(0), col(1), col(2), col(3), vec, vec, mask_spec, dmat_spec],
        out_specs=[pl.BlockSpec((c, A_D), lambda h, n: (n, h)),
                   pl.BlockSpec((None, None, A_D, A_D), lambda h, n: (h, n, 0, 0))],
        out_shape=[_sds((t, GROUP)), _sds((A_HEADS, nch, A_D, A_D))],
        scratch_shapes=[pltpu.VMEM((A_D, A_D), F32)],
        compiler_params=_params(("arbitrary", "arbitrary")),
    )(proj, proj, proj, proj, lb, gain, masks, dmat)


def _hgrn_bwd(proj, lb, gain, states, dya):
    t = proj.shape[0]
    c, nl = HG_CHUNK, HG_LEVELS
    nch = t // c
    masks, dmat = _hgrn_consts()
    cidx, col, vec, mask_spec, dmat_spec = _hgrn_specs(nch, True)

    def body(aq_ref, af_ref, ai_ref, ag_ref, lb_ref, gain_ref, mask_ref, dmat_ref, st_ref, dy_ref,
             daq_ref, daf_ref, dai_ref, dag_ref, dlb_ref, dgain_ref, ds_scr, z_scr):
        @pl.when(pl.program_id(1) == 0)
        def _():
            ds_scr[...] = jnp.zeros_like(ds_scr)
            dlb_ref[...] = jnp.zeros_like(dlb_ref)
            dgain_ref[...] = jnp.zeros_like(dgain_ref)

        aq = aq_ref[...]
        v = ai_ref[...]
        ag = ag_ref[...]
        lb = lb_ref[...]
        gain = gain_ref[...]
        dy = dy_ref[...]
        sq, q, sneg, kk, x_all, b, b_last = _hgrn_common(aq, af_ref[...], lb, dmat_ref)
        eb = jnp.exp(b)
        ebl = jnp.exp(b_last - b)
        ebl_row = jnp.exp(b_last)
        qe = _bf(q * eb)
        ke = _bf(kk * ebl)
        vb = _bf(v)
        s = st_ref[...]
        sb = _bf(s)
        ds = ds_scr[...]
        dsb = _bf(ds)

        pb = _bf(_hgrn_scores(x_all, q, kk, mask_ref))
        o = _dot(pb, vb) + _dot_nt(qe, sb)

        oh, r = _rms(o)
        sg_sig = _sigmoid(ag)
        sg = ag * sg_sig
        dag_ref[...] = dy * oh * gain * _silu_grad(ag, sg_sig)
        dgain_ref[...] += jnp.sum(dy * oh * sg, axis=0, keepdims=True)
        do = _bf(_rms_bwd(dy * gain * sg, oh, r))

        dp = _dot_nt(do, vb)
        dai_ref[...] = _dot_tn(pb, do) + _dot_nt(ke, dsb)
        dpd = jnp.sum(mask_ref[0] * dp, axis=1, keepdims=True)
        dq_s = eb * _dot(do, sb)
        dk_s = ebl * _dot(vb, dsb)
        z_scr[pl.ds(0, c), :] = q * dq_s
        z_scr[pl.ds((nl + 1) * c, c), :] = kk * dk_s
        dq = dq_s + dpd * kk
        dk = dk_s + dpd * q
        for l in range(nl):
            qf, kf, ql, kl = _hgrn_level(x_all, l, q, kk)
            dpl = _bf(mask_ref[l + 1] * dp)
            dq_l = qf * _dot(dpl, kl)
            dk_l = kf * _dot_tn(dpl, ql)
            z_scr[pl.ds((l + 1) * c, c), :] = q * dq_l - kk * dk_l
            dq = dq + dq_l
            dk = dk + dk_l

        dlogf = (_dot_tn(dmat_ref[...], z_scr[...], HIGHEST)
                 + ebl_row * jnp.sum(ds * s, axis=0, keepdims=True))
        dkk = dk - dlogf / (1.0 - kk)
        daf_ref[...] = dkk * (1.0 - lb) * (-(sneg * (1.0 - sneg)))
        dlb_ref[...] += jnp.sum(dkk * (-sneg), axis=0, keepdims=True)
        daq_ref[...] = dq * _silu_grad(aq, sq)
        ds_scr[...] = ds * ebl_row + _dot_tn(do, qe)

    out_col = pl.BlockSpec((c, A_D), lambda h, n: (cidx(n), h))
    return pl.pallas_call(
        body, name="hgrn_bwd", grid=(A_HEADS, nch),
        in_specs=[col(0), col(1), col(2), col(3), vec, vec, mask_spec, dmat_spec,
                  pl.BlockSpec((None, None, A_D, A_D), lambda h, n: (h, cidx(n), 0, 0)),
                  out_col],
        out_specs=[out_col, out_col, out_col, out_col, vec, vec],
        out_shape=[_sds((t, GROUP))] * 4 + [_sds((1, GROUP))] * 2,
        scratch_shapes=[pltpu.VMEM((A_D, A_D), F32), pltpu.VMEM(((nl + 2) * c, A_D), F32)],
        compiler_params=_params(("arbitrary", "arbitrary")),
    )(proj, proj, proj, proj, lb, gain, masks, dmat, states, dya)


def _sb_consts():
    j = np.arange(SB_TK)
    ones = np.ones((SB_TK, SB_TK), np.float32)
    strict = np.concatenate([(j[:, None] > j[None, :]).astype(np.float32), ones], axis=1)
    incl = np.concatenate([(j[:, None] >= j[None, :]).astype(np.float32), ones], axis=1)
    return jnp.asarray(strict, BF16), jnp.asarray(incl, BF16)


def _sb_softplus(z, masked):
    lg = jnp.log(1.0 + jnp.exp(-jnp.abs(z)))
    sp = jnp.maximum(z, 0.0) + lg
    logsig = jnp.minimum(z, 0.0) - lg
    mask = None
    if masked:
        mask = lax.broadcasted_iota(jnp.int32, z.shape, 1) < lax.broadcasted_iota(jnp.int32, z.shape, 0)
        sp = jnp.where(mask, sp, 0.0)
    return mask, sp, logsig


def _sb_sweep(qi, group_fn, state):
    nd = SB_TQ // SB_TK
    state = group_fn([(pl.multiple_of((qi * nd + d) * SB_TK, SB_TK), d * SB_TK, True) for d in reversed(range(nd))],
                     state)

    def step(j, st):
        return group_fn([(pl.multiple_of(((qi - j) * nd - 1 - g) * SB_TK, SB_TK), 0, False) for g in range(nd)], st)

    return lax.fori_loop(0, qi, step, state)


def _set_rows(r0, full, new):
    return new if r0 == 0 else jnp.concatenate([full[:r0], new], axis=0)


def _sb_fwd(q, k, v, bg, gain):
    t = q.shape[1]
    tq = SB_TQ
    strict, _ = _sb_consts()

    def body(q_ref, k_ref, v_ref, bg_ref, gain_ref, m_ref, o_ref, y_ref):
        qb = q_ref[...]
        cmat = m_ref[...]

        def group(tiles, state):
            carry, acc = state
            kv = [(k_ref[pl.ds(off, SB_TK), :], v_ref[pl.ds(off, SB_TK), :]) for off, _, _ in tiles]
            zs = [_dot_nt(qb[r0:], kb) for (_, r0, _), (kb, _) in zip(tiles, kv)]
            sps = [_sb_softplus(z, masked) for z, (_, _, masked) in zip(zs, tiles)]
            cs2s = []
            for _, sp, _ in sps:
                hi, lo = _split(sp)
                cs2s.append(_dot(hi, cmat) + _dot(lo, cmat))
            ws = []
            for (mask, _, logsig), cs2, (_, r0, masked) in zip(sps, cs2s, tiles):
                w = jnp.exp(logsig - cs2[:, :SB_TK] - carry[r0:])
                ws.append(_split(jnp.where(mask, w, 0.0) if masked else w))
                carry = _set_rows(r0, carry, carry[r0:] + cs2[:, SB_TK:])
            for (whi, wlo), (_, vb), (_, r0, _) in zip(ws, kv, tiles):
                acc = _set_rows(r0, acc, acc[r0:] + _dot(whi, vb) + _dot(wlo, vb))
            return carry, acc

        _, o = _sb_sweep(pl.program_id(1), group, (jnp.zeros((tq, SB_TK), F32), jnp.zeros((tq, B_D), F32)))
        o_ref[...] = o
        oh, _ = _rms(o)
        bg = bg_ref[...]
        y_ref[...] = oh * gain_ref[...] * (bg * _sigmoid(bg))

    blk = pl.BlockSpec((None, tq, B_D), lambda h, i: (h, i, 0))
    full = pl.BlockSpec((None, t, B_D), lambda h, i: (h, 0, 0))
    return pl.pallas_call(
        body, name="sb_fwd", grid=(B_HEADS, t // tq),
        in_specs=[blk, full, full, blk, pl.BlockSpec((None, 1, B_D), lambda h, i: (h, 0, 0)),
                  pl.BlockSpec((SB_TK, 2 * SB_TK), lambda h, i: (0, 0))],
        out_specs=[blk, blk],
        out_shape=[_sds((B_HEADS, t, B_D)), _sds((B_HEADS, t, B_D))],
        compiler_params=_params(("arbitrary", "arbitrary")),
    )(q, k, v, bg, gain, strict)


def _sb_bwd(q, k, v, o, dy, bg, gain):
    t = q.shape[1]
    tq = SB_TQ
    strict, incl = _sb_consts()

    def body(q_ref, k_ref, v_ref, o_ref, dy_ref, bg_ref, gain_ref, ms_ref, mi_ref,
             dq_ref, dk_ref, dv_ref, dbg_ref, dgain_ref):
        qi = pl.program_id(1)

        @pl.when(qi == 0)
        def _():
            dk_ref[...] = jnp.zeros_like(dk_ref)
            dv_ref[...] = jnp.zeros_like(dv_ref)
            dgain_ref[...] = jnp.zeros_like(dgain_ref)

        qb = q_ref[...]
        cmat = ms_ref[...]
        imat = mi_ref[...]
        o = o_ref[...]
        dy = dy_ref[...]
        bg = bg_ref[...]
        gain = gain_ref[...]
        oh, r = _rms(o)
        sig = _sigmoid(bg)
        sg = bg * sig
        dbg_ref[...] = dy * oh * gain * _silu_grad(bg, sig)
        dgain_ref[...] += jnp.sum(dy * oh * sg, axis=0, keepdims=True)
        do = _bf(_rms_bwd(dy * gain * sg, oh, r))
        total = jnp.broadcast_to(jnp.sum(do.astype(F32) * o, axis=1, keepdims=True), (tq, SB_TK))

        def group(tiles, state):
            carry, gcarry, dq = state
            kv = [(k_ref[pl.ds(off, SB_TK), :], v_ref[pl.ds(off, SB_TK), :]) for off, _, _ in tiles]
            zs = [_dot_nt(qb[r0:], kb) for (_, r0, _), (kb, _) in zip(tiles, kv)]
            dws = [_dot_nt(do[r0:], vb) for (_, r0, _), (_, vb) in zip(tiles, kv)]
            sps = [_sb_softplus(z, masked) for z, (_, _, masked) in zip(zs, tiles)]
            cs2s = []
            for _, sp, _ in sps:
                hi, lo = _split(sp)
                cs2s.append(_dot(hi, cmat) + _dot(lo, cmat))
            ws, gs = [], []
            for (mask, _, logsig), cs2, dw, (_, r0, masked) in zip(sps, cs2s, dws, tiles):
                w = jnp.exp(logsig - cs2[:, :SB_TK] - carry[r0:])
                w = jnp.where(mask, w, 0.0) if masked else w
                ws.append(_bf(w))
                gs.append(dw * w)
                carry = _set_rows(r0, carry, carry[r0:] + cs2[:, SB_TK:])
            s2s = []
            for g in gs:
                ghi, glo = _split(g)
                s2s.append(_dot(ghi, imat) + _dot(glo, imat))
            dzs = []
            for (mask, _, logsig), g, s2, (_, r0, masked) in zip(sps, gs, s2s, tiles):
                before = total[r0:] - gcarry[r0:] - s2[:, :SB_TK]
                sig_z = jnp.exp(logsig)
                dz = g * (1.0 - sig_z) - sig_z * before
                dzs.append(_bf(jnp.where(mask, dz, 0.0) if masked else dz))
                gcarry = _set_rows(r0, gcarry, gcarry[r0:] + s2[:, SB_TK:])
            for dz, wb, (kb, _), (off, r0, _) in zip(dzs, ws, kv, tiles):
                dq = _set_rows(r0, dq, dq[r0:] + _dot(dz, kb))
                dk_ref[pl.ds(off, SB_TK), :] += _dot_tn(dz, qb[r0:])
                dv_ref[pl.ds(off, SB_TK), :] += _dot_tn(wb, do[r0:])
            return carry, gcarry, dq

        zero = jnp.zeros((tq, SB_TK), F32)
        _, _, dq = _sb_sweep(qi, group, (zero, zero, jnp.zeros((tq, B_D), F32)))
        dq_ref[...] = dq

    blk = pl.BlockSpec((None, tq, B_D), lambda h, i: (h, i, 0))
    full = pl.BlockSpec((None, t, B_D), lambda h, i: (h, 0, 0))
    vec = pl.BlockSpec((None, 1, B_D), lambda h, i: (h, 0, 0))
    mat = pl.BlockSpec((SB_TK, 2 * SB_TK), lambda h, i: (0, 0))
    return pl.pallas_call(
        body, name="sb_bwd", grid=(B_HEADS, t // tq),
        in_specs=[blk, full, full, blk, blk, blk, vec, mat, mat],
        out_specs=[blk, full, full, blk, vec],
        out_shape=[_sds((B_HEADS, t, B_D))] * 4 + [_sds((B_HEADS, 1, B_D))],
        compiler_params=_params(("arbitrary", "arbitrary")),
    )(q, k, v, o, dy, bg, gain, strict, incl)


def _to_heads(x, dtype=F32):
    t = x.shape[0]
    return x.reshape(t, B_HEADS, B_D).transpose(1, 0, 2).astype(dtype)


def _from_heads(x):
    return x.transpose(1, 0, 2).reshape(x.shape[1], GROUP)


def _adamw(w, g, m, v):
    rows, cols = w.shape
    tr = rows
    for cand in (400, 256, 128, 64, 32, 16, 8):
        if rows % cand == 0:
            tr = cand
            break

    def body(w_ref, g_ref, m_ref, v_ref, d_ref, nm_ref, nv_ref):
        g_ = g_ref[...]
        m_ = ADAM_B1 * m_ref[...] + (1.0 - ADAM_B1) * g_
        v_ = ADAM_B2 * v_ref[...] + (1.0 - ADAM_B2) * (g_ * g_)
        m_hat = m_ / (1.0 - ADAM_B1 ** ADAM_STEP)
        v_hat = v_ / (1.0 - ADAM_B2 ** ADAM_STEP)
        d_ref[...] = -ADAM_LR * (m_hat / (jnp.sqrt(v_hat) + ADAM_EPS) + ADAM_WD * w_ref[...])
        nm_ref[...] = m_
        nv_ref[...] = v_

    spec = pl.BlockSpec((tr, cols), lambda i: (i, 0))
    return pl.pallas_call(
        body, name="adamw", grid=(rows // tr,), in_specs=[spec] * 4, out_specs=[spec] * 3,
        out_shape=[_sds((rows, cols))] * 3, compiler_params=_params(("arbitrary",)),
    )(w, g, m, v)


def _sum_leading(x):
    n, rows, cols = x.shape
    tr = 400 if rows % 400 == 0 else 200

    def body(x_ref, o_ref):
        acc = x_ref[0]
        for i in range(1, n):
            acc = acc + x_ref[i]
        o_ref[...] = acc

    return pl.pallas_call(
        body, name="sum_leading", grid=(rows // tr,),
        in_specs=[pl.BlockSpec((n, tr, cols), lambda i: (0, i, 0))],
        out_specs=pl.BlockSpec((tr, cols), lambda i: (i, 0)),
        out_shape=_sds((rows, cols)), compiler_params=_params(("arbitrary",)),
    )(x)


_ANY = pl.BlockSpec(memory_space=pl.ANY)


def _place():
    return lax.axis_index("x"), lax.axis_index("y"), lax.axis_index("c")


def _gather_weights(flat):
    half = HALF_FLAT

    def body(x_ref, out_ref, send_sems, recv_sems, local_sem):
        x, y, c = _place()
        me = 2 * x + y
        sibling = (x, y, 1 - c)

        def rows(shard, hc):
            return out_ref.at[shard, pl.ds(hc * half, half), :]

        def copy(k, shard, hc, to, src=None):
            return pltpu.make_async_remote_copy(
                src_ref=rows(shard, hc) if src is None else src, dst_ref=rows(shard, hc),
                send_sem=send_sems.at[k], recv_sem=recv_sems.at[k], device_id=to, device_id_type=MESH)

        mine = pltpu.make_async_copy(x_ref, out_ref.at[me], local_sem)
        mine.start()
        peers = [me ^ k for k in (1, 2, 3)]
        first = [copy(k, me, c, (p >> 1, p & 1, c), src=x_ref.at[pl.ds(c * half, half), :])
                 for k, p in enumerate(peers)]
        for cp in first:
            cp.start()
        passed = [copy(3 + k, p, c, sibling) for k, p in enumerate(peers)]
        for k, p in enumerate(peers):
            copy(k, p, c, sibling).wait_recv()
            passed[k].start()
        for k, p in enumerate(peers):
            copy(3 + k, p, 1 - c, sibling).wait_recv()
        for cp in first + passed:
            cp.wait_send()
        mine.wait()

    return pl.pallas_call(
        body, name="gather_weights", in_specs=[_ANY], out_specs=_ANY,
        out_shape=_sds((N_SHARD, ROWS_FLAT, D_MODEL), BF16),
        scratch_shapes=[pltpu.SemaphoreType.DMA((6,)), pltpu.SemaphoreType.DMA((6,)), pltpu.SemaphoreType.DMA],
    )(flat)


def _swap_halves(grads):
    half = HALF_FLAT

    def body(g_ref, out_ref, send_sem, recv_sem, local_sem):
        x, y, c = _place()
        mine = pltpu.make_async_copy(g_ref.at[:, pl.ds(c * half, half), :], out_ref.at[0], local_sem)
        mine.start()
        cp = pltpu.make_async_remote_copy(
            src_ref=g_ref.at[:, pl.ds((1 - c) * half, half), :], dst_ref=out_ref.at[1],
            send_sem=send_sem, recv_sem=recv_sem, device_id=(x, y, 1 - c), device_id_type=MESH)
        cp.start()
        cp.wait()
        mine.wait()

    return pl.pallas_call(
        body, name="swap_halves", in_specs=[_ANY], out_specs=_ANY,
        out_shape=_sds((2, N_SHARD, half, D_MODEL)),
        scratch_shapes=[pltpu.SemaphoreType.DMA, pltpu.SemaphoreType.DMA, pltpu.SemaphoreType.DMA],
    )(grads)


def _scatter_shards(part):
    def body(p_ref, out_ref, send_sems, recv_sems, local_sem):
        x, y, c = _place()
        me = 2 * x + y
        mine = pltpu.make_async_copy(p_ref.at[me], out_ref.at[me], local_sem)
        mine.start()
        peers = [me ^ k for k in (1, 2, 3)]
        sends = [pltpu.make_async_remote_copy(
            src_ref=p_ref.at[p], dst_ref=out_ref.at[me], send_sem=send_sems.at[k], recv_sem=recv_sems.at[k],
            device_id=(p >> 1, p & 1, c), device_id_type=MESH) for k, p in enumerate(peers)]
        for cp in sends:
            cp.start()
        for k, p in enumerate(peers):
            pltpu.make_async_remote_copy(
                src_ref=p_ref.at[p], dst_ref=out_ref.at[p], send_sem=send_sems.at[k], recv_sem=recv_sems.at[k],
                device_id=(p >> 1, p & 1, c), device_id_type=MESH).wait_recv()
        for cp in sends:
            cp.wait_send()
        mine.wait()

    return pl.pallas_call(
        body, name="scatter_shards", in_specs=[_ANY], out_specs=_ANY,
        out_shape=_sds((N_SHARD, HALF_FLAT, D_MODEL)),
        scratch_shapes=[pltpu.SemaphoreType.DMA((3,)), pltpu.SemaphoreType.DMA((3,)), pltpu.SemaphoreType.DMA],
    )(part)


def _join_halves(mine_half):
    half = HALF_FLAT

    def body(r_ref, out_ref, send_sem, recv_sem, local_sem):
        x, y, c = _place()
        mine = pltpu.make_async_copy(r_ref, out_ref.at[pl.ds(c * half, half), :], local_sem)
        mine.start()
        send = pltpu.make_async_remote_copy(
            src_ref=r_ref, dst_ref=out_ref.at[pl.ds(c * half, half), :],
            send_sem=send_sem, recv_sem=recv_sem, device_id=(x, y, 1 - c), device_id_type=MESH)
        send.start()
        pltpu.make_async_remote_copy(
            src_ref=r_ref, dst_ref=out_ref.at[pl.ds((1 - c) * half, half), :],
            send_sem=send_sem, recv_sem=recv_sem, device_id=(x, y, 1 - c), device_id_type=MESH).wait_recv()
        send.wait_send()
        mine.wait()

    return pl.pallas_call(
        body, name="join_halves", in_specs=[_ANY], out_specs=_ANY,
        out_shape=_sds((ROWS_FLAT, D_MODEL)),
        scratch_shapes=[pltpu.SemaphoreType.DMA, pltpu.SemaphoreType.DMA, pltpu.SemaphoreType.DMA],
    )(mine_half)


def _allreduce_small(vec):
    def body(v_ref, out_ref, buf, send_sems, recv_sems):
        x, y, c = _place()
        me = 4 * x + 2 * y + c
        buf[me] = v_ref[...]
        peers = [me ^ k for k in range(1, N_DEV)]
        sends = [pltpu.make_async_remote_copy(
            src_ref=v_ref, dst_ref=buf.at[me], send_sem=send_sems.at[k], recv_sem=recv_sems.at[k],
            device_id=(p >> 2, (p >> 1) & 1, p & 1), device_id_type=MESH) for k, p in enumerate(peers)]
        for cp in sends:
            cp.start()
        for k, p in enumerate(peers):
            pltpu.make_async_remote_copy(
                src_ref=v_ref, dst_ref=buf.at[p], send_sem=send_sems.at[k], recv_sem=recv_sems.at[k],
                device_id=(p >> 2, (p >> 1) & 1, p & 1), device_id_type=MESH).wait_recv()
        for cp in sends:
            cp.wait_send()
        acc = buf[0]
        for d in range(1, N_DEV):
            acc = acc + buf[d]
        out_ref[...] = acc

    vm = pl.BlockSpec(memory_space=pltpu.VMEM)
    return pl.pallas_call(
        body, name="allreduce_small", in_specs=[vm], out_specs=vm, out_shape=_sds((SMALL_ROWS, 128)),
        scratch_shapes=[pltpu.VMEM((N_DEV, SMALL_ROWS, 128), F32),
                        pltpu.SemaphoreType.DMA((N_DEV - 1,)), pltpu.SemaphoreType.DMA((N_DEV - 1,))],
    )(vec)


def _flatten_shard(w_in, w_out, w_pg, w_pp):
    return jnp.concatenate([w_in.reshape(-1, D_MODEL), w_out.reshape(-1, D_MODEL), w_pg.reshape(-1, D_MODEL),
                            w_pp.reshape(-1, D_MODEL)], axis=0)


def _unflatten_shard(flat):
    a, b, c = ROWS_W_IN, ROWS_W_IN + ROWS_W_OUT, ROWS_W_IN + ROWS_W_OUT + ROWS_W_PG
    q = D_MODEL // N_SHARD
    return (flat[:a].reshape(2, D_MODEL, D_MODEL), flat[a:b].reshape(2, q, D_MODEL),
            flat[b:c].reshape(2, q, D_MODEL), flat[c:].reshape(2, D_PLE, q))


def _full_weights(gathered):
    a, b, c = ROWS_W_IN, ROWS_W_IN + ROWS_W_OUT, ROWS_W_IN + ROWS_W_OUT + ROWS_W_PG
    q = D_MODEL // N_SHARD
    rpp = ROWS_W_PP // 2
    out = []
    for l in range(2):
        w_in = gathered[:, l * D_MODEL:(l + 1) * D_MODEL, :].transpose(1, 0, 2).reshape(D_MODEL, D_IN)
        w_out = gathered[:, a + l * q:a + (l + 1) * q, :].reshape(D_MODEL, D_MODEL)
        w_pg = gathered[:, b + l * q:b + (l + 1) * q, :].reshape(D_MODEL, D_MODEL)
        w_pp = gathered[:, c + l * rpp:c + (l + 1) * rpp, :].reshape(N_SHARD, D_PLE, q).transpose(1, 0, 2)
        out.append((w_in, w_out, w_pg, w_pp.reshape(D_PLE, D_MODEL)))
    return out


def _shard_major(dw_in, dw_out, dw_pg, dw_pp):
    q = D_MODEL // N_SHARD
    rpp = ROWS_W_PP // 2
    parts = []
    for l in range(2):
        g = dw_in[l].reshape(N_SHARD, 2, D_MODEL, GROUP).transpose(0, 2, 1, 3).reshape(N_SHARD, D_MODEL, D_MODEL)
        parts.append(g)
    for l in range(2):
        parts.append(dw_out[l].reshape(N_SHARD, q, D_MODEL))
    for l in range(2):
        parts.append(dw_pg[l].reshape(N_SHARD, q, D_MODEL))
    for l in range(2):
        parts.append(dw_pp[l].reshape(D_PLE, N_SHARD, q).transpose(1, 0, 2).reshape(N_SHARD, rpp, D_MODEL))
    return jnp.concatenate(parts, axis=1)


def _lower_bounds(lb_logits):
    sm = jax.nn.softmax(lb_logits.astype(F32), axis=0)
    return jnp.cumsum(sm, axis=0) - sm[0:1]


def kernel(x, p, norm_mix, w_in, a_out_norm, b_out_norm, w_out, lb_logits, ple_gate_norm, w_ple_gate, w_ple_proj, ple_post_norm, final_norm, loss_target, m_norm_mix, m_w_in, m_a_out_norm, m_b_out_norm, m_w_out, m_lb_logits, m_ple_gate_norm, m_w_ple_gate, m_w_ple_proj, m_ple_post_norm, m_final_norm, v_norm_mix, v_w_in, v_a_out_norm, v_b_out_norm, v_w_out, v_lb_logits, v_ple_gate_norm, v_w_ple_gate, v_w_ple_proj, v_ple_post_norm, v_final_norm):
    t = x.shape[1]
    h0 = x.reshape(t, D_MODEL)
    target = loss_target.reshape(t, D_MODEL)
    pl_in = p.reshape(2, t, D_PLE)

    w_flat = _flatten_shard(w_in, w_out, w_ple_gate, w_ple_proj)
    weights = _full_weights(_gather_weights(w_flat.astype(BF16)))
    lbs, lbs_vjp = jax.vjp(_lower_bounds, lb_logits)

    saved = []
    h = h0
    for l in range(2):
        wi, wo, wpg, wpp = weights[l]
        g_mix = norm_mix[l].reshape(1, D_MODEL)
        lb = lbs[l].reshape(1, GROUP)
        ga = a_out_norm[l].reshape(1, GROUP)
        gb = b_out_norm[l].reshape(B_HEADS, 1, B_D)
        proj = _inproj(h, g_mix, wi)
        ya, states = _hgrn_fwd(proj, lb, ga)
        bq = _to_heads(proj[:, 4 * GROUP:5 * GROUP] * (B_D ** -0.5), BF16)
        bk = _to_heads(proj[:, 5 * GROUP:6 * GROUP], BF16)
        bv = _to_heads(proj[:, 6 * GROUP:7 * GROUP], BF16)
        bg = _to_heads(proj[:, 7 * GROUP:8 * GROUP])
        ob, yb_h = _sb_fwd(bq, bk, bv, bg, gb)
        yb = _from_heads(yb_h)
        h1 = _outproj(h, ya, yb, wo)
        g_post = ple_post_norm[l].reshape(1, D_MODEL)
        g_gate = ple_gate_norm[l].reshape(1, D_MODEL)
        h2 = _ple_fwd(h1, pl_in[l], wpp, wpg, g_post, g_gate)
        saved.append((h, proj, states, ya, yb, bq, bk, bv, bg, ob, h1))
        h = h2

    dh, d_final, loss_part = _final(h, final_norm.reshape(1, D_MODEL), target)

    dw_in_l, dw_out_l, dw_pg_l, dw_pp_l = [None] * 2, [None] * 2, [None] * 2, [None] * 2
    d_mix, d_a, d_b, d_lb, d_gate, d_post = [None] * 2, [None] * 2, [None] * 2, [None] * 2, [None] * 2, [None] * 2
    for l in (1, 0):
        wi, wo, wpg, wpp = weights[l]
        h_in, proj, states, ya, yb, bq, bk, bv, bg, ob, h1 = saved[l]
        g_mix = norm_mix[l].reshape(1, D_MODEL)
        lb = lbs[l].reshape(1, GROUP)
        ga = a_out_norm[l].reshape(1, GROUP)
        gb = b_out_norm[l].reshape(B_HEADS, 1, B_D)
        g_post = ple_post_norm[l].reshape(1, D_MODEL)
        g_gate = ple_gate_norm[l].reshape(1, D_MODEL)
        dh1, dw_pg_l[l], dw_pp_l[l], d_gate[l], d_post[l] = _ple_bwd(dh, h1, pl_in[l], wpp, wpg, g_post, g_gate)
        dya, dyb, dw_out_l[l] = _outproj_bwd(dh1, ya, yb, wo)
        dbq, dbk, dbv, dbg, dgb = _sb_bwd(bq, bk, bv, ob, _to_heads(dyb), bg, gb)
        daq, daf, dai, dag, d_lb[l], d_a[l] = _hgrn_bwd(proj, lb, ga, states, dya)
        d_b[l] = dgb.reshape(1, GROUP)
        parts = [daq, daf, dai, dag, _from_heads(dbq) * (B_D ** -0.5), _from_heads(dbk), _from_heads(dbv),
                 _from_heads(dbg)]
        dw_in_l[l] = _inproj_bwd_dw(h_in, g_mix, parts)
        dh, d_mix[l] = _inproj_bwd_dx(dh1, h_in, g_mix, wi, parts)
    grad_x = dh.reshape(x.shape)

    swapped = _swap_halves(_shard_major(dw_in_l, dw_out_l, dw_pg_l, dw_pp_l))
    chip_sum = _sum_leading(swapped.reshape(2, N_SHARD * HALF_FLAT, D_MODEL)).reshape(N_SHARD, HALF_FLAT, D_MODEL)
    g_flat = _join_halves(_sum_leading(_scatter_shards(chip_sum)))
    g_w_in, g_w_out, g_w_pg, g_w_pp = _unflatten_shard(g_flat)

    small = jnp.concatenate([
        jnp.concatenate(d_mix).reshape(-1, 128), jnp.concatenate(d_a).reshape(-1, 128),
        jnp.concatenate(d_b).reshape(-1, 128), jnp.concatenate(d_lb).reshape(-1, 128),
        jnp.concatenate(d_gate).reshape(-1, 128), jnp.concatenate(d_post).reshape(-1, 128),
        d_final.reshape(-1, 128), jnp.broadcast_to(loss_part, (8, 128))], axis=0)
    small = _allreduce_small(small)
    loss = small[80, 0]
    g_norm_mix = small[0:16].reshape(2, D_MODEL)
    g_a = small[16:24].reshape(2, GROUP)
    g_b = small[24:32].reshape(2, GROUP)
    (g_lb,) = lbs_vjp(small[32:40].reshape(2, GROUP))
    g_gate = small[40:56].reshape(2, D_MODEL)
    g_post = small[56:72].reshape(2, D_MODEL)
    g_final = small[72:80].reshape(D_MODEL)

    d_flat, nm_flat, nv_flat = _adamw(w_flat, g_flat, _flatten_shard(m_w_in, m_w_out, m_w_ple_gate, m_w_ple_proj),
                                      _flatten_shard(v_w_in, v_w_out, v_w_ple_gate, v_w_ple_proj))
    d_w_in, d_w_out, d_w_pg, d_w_pp = _unflatten_shard(d_flat)
    nm_w_in, nm_w_out, nm_w_pg, nm_w_pp = _unflatten_shard(nm_flat)
    nv_w_in, nv_w_out, nv_w_pg, nv_w_pp = _unflatten_shard(nv_flat)

    small_w = [norm_mix, a_out_norm, b_out_norm, lb_logits, ple_gate_norm, ple_post_norm, final_norm]
    small_g = [g_norm_mix, g_a, g_b, g_lb, g_gate, g_post, g_final]
    small_m = [m_norm_mix, m_a_out_norm, m_b_out_norm, m_lb_logits, m_ple_gate_norm, m_ple_post_norm, m_final_norm]
    small_v = [v_norm_mix, v_a_out_norm, v_b_out_norm, v_lb_logits, v_ple_gate_norm, v_ple_post_norm, v_final_norm]
    pack = lambda arrs: jnp.concatenate([a.reshape(-1, 128) for a in arrs], axis=0)
    ds, nms, nvs = _adamw(pack(small_w), pack(small_g), pack(small_m), pack(small_v))

    def unpack(packed):
        out, r = [], 0
        for a in small_w:
            n = a.size // 128
            out.append(packed[r:r + n].reshape(a.shape))
            r += n
        return out

    d_s, nm_s, nv_s = unpack(ds), unpack(nms), unpack(nvs)

    def ordered(s, big):
        return [s[0], big[0], s[1], s[2], big[1], s[3], s[4], big[2], big[3], s[5], s[6]]

    grads = ordered(small_g, [g_w_in, g_w_out, g_w_pg, g_w_pp])
    deltas = ordered(d_s, [d_w_in, d_w_out, d_w_pg, d_w_pp])
    new_m = ordered(nm_s, [nm_w_in, nm_w_out, nm_w_pg, nm_w_pp])
    new_v = ordered(nv_s, [nv_w_in, nv_w_out, nv_w_pg, nv_w_pp])
    return (loss, grad_x, *grads, *deltas, *new_m, *new_v)
```

```python
import functools
import math

import numpy as np
import jax
import jax.numpy as jnp
from jax import lax
from jax.experimental import pallas as pl
from jax.experimental.pallas import tpu as pltpu

F32 = jnp.float32
BF16 = jnp.bfloat16
MESH = pl.DeviceIdType.MESH

D_MODEL = 1024
D_PLE = 256
D_IN = 4096
A_HEADS, A_D = 4, 128
B_HEADS, B_D = 8, 64
GROUP = 512
EPS = 1e-6
N_SHARD = 4
N_DEV = 8

HG_CHUNK = 128
HG_LEVELS = 7
SB_TQ = 512
SB_TK = 128

ADAM_LR, ADAM_B1, ADAM_B2, ADAM_EPS, ADAM_WD, ADAM_STEP = 0.001, 0.9, 0.999, 1e-08, 0.01, 10

VMEM_LIMIT = 48 * 1024 * 1024

ROWS_W_IN = 2 * D_MODEL
ROWS_W_OUT = 2 * (D_MODEL // N_SHARD)
ROWS_W_PG = 2 * (D_MODEL // N_SHARD)
ROWS_W_PP = 2 * (D_PLE * (D_MODEL // N_SHARD) // D_MODEL)
ROWS_FLAT = ROWS_W_IN + ROWS_W_OUT + ROWS_W_PG + ROWS_W_PP
HALF_FLAT = ROWS_FLAT // 2
N_CHUNK = 10
CHUNK_ROWS = HALF_FLAT // N_CHUNK

SMALL_ROWS = 88


def _sds(shape, dtype=F32):
    return jax.ShapeDtypeStruct(shape, dtype)


def _params(sem=None):
    kw = dict(vmem_limit_bytes=VMEM_LIMIT)
    if sem is not None:
        kw["dimension_semantics"] = sem
    return pltpu.CompilerParams(**kw)


def _dot(a, b, precision=None):
    return lax.dot_general(a, b, (((1,), (0,)), ((), ())), preferred_element_type=F32, precision=precision)


def _dot_nt(a, b, precision=None):
    return lax.dot_general(a, b, (((1,), (1,)), ((), ())), preferred_element_type=F32, precision=precision)


def _dot_tn(a, b, precision=None):
    return lax.dot_general(a, b, (((0,), (0,)), ((), ())), preferred_element_type=F32, precision=precision)


def _bf(x):
    return x.astype(BF16)


def _split(x):
    hi = x.astype(BF16)
    lo = (x - hi.astype(F32)).astype(BF16)
    return hi, lo


def _rms(x):
    r = lax.rsqrt(jnp.mean(x * x, axis=-1, keepdims=True) + EPS)
    return x * r, r


def _rms_bwd(dxh, xh, r):
    return r * (dxh - xh * jnp.mean(dxh * xh, axis=-1, keepdims=True))


def _sigmoid(x):
    return 1.0 / (1.0 + jnp.exp(-x))


def _silu_grad(x, sig):
    return sig * (1.0 + x * (1.0 - sig))


def _row_tile(t, want):
    return min(t, want)


def _inproj(h, g, w):
    t = h.shape[0]
    tm = _row_tile(t, 512)
    tn = 1024

    def body(h_ref, g_ref, w_ref, o_ref):
        xh, _ = _rms(h_ref[...])
        o_ref[...] = _dot(_bf(xh * g_ref[...]), w_ref[...])

    return pl.pallas_call(
        body, name="inproj", grid=(D_IN // tn, t // tm),
        in_specs=[pl.BlockSpec((tm, D_MODEL), lambda j, i: (i, 0)),
                  pl.BlockSpec((1, D_MODEL), lambda j, i: (0, 0)),
                  pl.BlockSpec((D_MODEL, tn), lambda j, i: (0, j))],
        out_specs=pl.BlockSpec((tm, tn), lambda j, i: (i, j)),
        out_shape=_sds((t, D_IN)), compiler_params=_params(("arbitrary", "arbitrary")),
    )(h, g, w)


def _outproj(h, ya, yb, w):
    t = h.shape[0]
    tm = _row_tile(t, 512)

    def body(h_ref, ya_ref, yb_ref, w_ref, o_ref):
        o_ref[...] = (h_ref[...] + _dot(_bf(ya_ref[...]), w_ref[pl.ds(0, GROUP), :])
                      + _dot(_bf(yb_ref[...]), w_ref[pl.ds(GROUP, GROUP), :]))

    return pl.pallas_call(
        body, name="outproj", grid=(t // tm,),
        in_specs=[pl.BlockSpec((tm, D_MODEL), lambda i: (i, 0)),
                  pl.BlockSpec((tm, GROUP), lambda i: (i, 0)),
                  pl.BlockSpec((tm, GROUP), lambda i: (i, 0)),
                  pl.BlockSpec((D_MODEL, D_MODEL), lambda i: (0, 0))],
        out_specs=pl.BlockSpec((tm, D_MODEL), lambda i: (i, 0)),
        out_shape=_sds((t, D_MODEL)), compiler_params=_params(("arbitrary",)),
    )(h, ya, yb, w)


def _ple_fwd(h, p, w_pp, w_pg, g_post, g_gate):
    t = h.shape[0]
    tm = _row_tile(t, 256)

    def body(h_ref, p_ref, wpp_ref, wpg_ref, gp_ref, gg_ref, o_ref):
        x = h_ref[...]
        ph, _ = _rms(_dot(_bf(p_ref[...]), wpp_ref[...]))
        xh, _ = _rms(x)
        gate = _sigmoid(_dot(_bf(xh * gg_ref[...]), wpg_ref[...]))
        o_ref[...] = x + gate * (ph * gp_ref[...])

    return pl.pallas_call(
        body, name="ple_fwd", grid=(t // tm,),
        in_specs=[pl.BlockSpec((tm, D_MODEL), lambda i: (i, 0)),
                  pl.BlockSpec((tm, D_PLE), lambda i: (i, 0)),
                  pl.BlockSpec((D_PLE, D_MODEL), lambda i: (0, 0)),
                  pl.BlockSpec((D_MODEL, D_MODEL), lambda i: (0, 0)),
                  pl.BlockSpec((1, D_MODEL), lambda i: (0, 0)),
                  pl.BlockSpec((1, D_MODEL), lambda i: (0, 0))],
        out_specs=pl.BlockSpec((tm, D_MODEL), lambda i: (i, 0)),
        out_shape=_sds((t, D_MODEL)), compiler_params=_params(("arbitrary",)),
    )(h, p, w_pp, w_pg, g_post, g_gate)


def _ple_bwd(dh2, h, p, w_pp, w_pg, g_post, g_gate):
    t = h.shape[0]
    tm = _row_tile(t, 256)

    def body(d_ref, h_ref, p_ref, wpp_ref, wpg_ref, gp_ref, gg_ref, dh_ref, dwpg_ref, dwpp_ref, dgg_ref, dgp_ref):
        @pl.when(pl.program_id(0) == 0)
        def _():
            dwpg_ref[...] = jnp.zeros_like(dwpg_ref)
            dwpp_ref[...] = jnp.zeros_like(dwpp_ref)
            dgg_ref[...] = jnp.zeros_like(dgg_ref)
            dgp_ref[...] = jnp.zeros_like(dgp_ref)

        d = d_ref[...]
        x = h_ref[...]
        gp = gp_ref[...]
        gg = gg_ref[...]
        pb = _bf(p_ref[...])
        ph, rp = _rms(_dot(pb, wpp_ref[...]))
        pe = ph * gp
        xh, rx = _rms(x)
        un = _bf(xh * gg)
        gate = _sigmoid(_dot(un, wpg_ref[...]))
        dgpre = _bf(d * pe * gate * (1.0 - gate))
        dun = _dot_nt(dgpre, wpg_ref[...])
        dh_ref[...] = d + _rms_bwd(dun * gg, xh, rx)
        dgg_ref[...] += jnp.sum(dun * xh, axis=0, keepdims=True)
        dwpg_ref[...] += _dot_tn(un, dgpre)
        dpe = d * gate
        dgp_ref[...] += jnp.sum(dpe * ph, axis=0, keepdims=True)
        dwpp_ref[...] += _dot_tn(pb, _bf(_rms_bwd(dpe * gp, ph, rp)))

    return pl.pallas_call(
        body, name="ple_bwd", grid=(t // tm,),
        in_specs=[pl.BlockSpec((tm, D_MODEL), lambda i: (i, 0)),
                  pl.BlockSpec((tm, D_MODEL), lambda i: (i, 0)),
                  pl.BlockSpec((tm, D_PLE), lambda i: (i, 0)),
                  pl.BlockSpec((D_PLE, D_MODEL), lambda i: (0, 0)),
                  pl.BlockSpec((D_MODEL, D_MODEL), lambda i: (0, 0)),
                  pl.BlockSpec((1, D_MODEL), lambda i: (0, 0)),
                  pl.BlockSpec((1, D_MODEL), lambda i: (0, 0))],
        out_specs=[pl.BlockSpec((tm, D_MODEL), lambda i: (i, 0)),
                   pl.BlockSpec((D_MODEL, D_MODEL), lambda i: (0, 0)),
                   pl.BlockSpec((D_PLE, D_MODEL), lambda i: (0, 0)),
                   pl.BlockSpec((1, D_MODEL), lambda i: (0, 0)),
                   pl.BlockSpec((1, D_MODEL), lambda i: (0, 0))],
        out_shape=[_sds((t, D_MODEL)), _sds((D_MODEL, D_MODEL)), _sds((D_PLE, D_MODEL)),
                   _sds((1, D_MODEL)), _sds((1, D_MODEL))],
        compiler_params=_params(("arbitrary",)),
    )(dh2, h, p, w_pp, w_pg, g_post, g_gate)


def _outproj_bwd(dh, ya, yb, w):
    t = dh.shape[0]
    tm = _row_tile(t, 512)

    def body(d_ref, ya_ref, yb_ref, w_ref, dya_ref, dyb_ref, dw_ref):
        @pl.when(pl.program_id(0) == 0)
        def _():
            dw_ref[...] = jnp.zeros_like(dw_ref)

        d = _bf(d_ref[...])
        dya_ref[...] = _dot_nt(d, w_ref[pl.ds(0, GROUP), :])
        dyb_ref[...] = _dot_nt(d, w_ref[pl.ds(GROUP, GROUP), :])
        dw_ref[pl.ds(0, GROUP), :] += _dot_tn(_bf(ya_ref[...]), d)
        dw_ref[pl.ds(GROUP, GROUP), :] += _dot_tn(_bf(yb_ref[...]), d)

    return pl.pallas_call(
        body, name="outproj_bwd", grid=(t // tm,),
        in_specs=[pl.BlockSpec((tm, D_MODEL), lambda i: (i, 0)),
                  pl.BlockSpec((tm, GROUP), lambda i: (i, 0)),
                  pl.BlockSpec((tm, GROUP), lambda i: (i, 0)),
                  pl.BlockSpec((D_MODEL, D_MODEL), lambda i: (0, 0))],
        out_specs=[pl.BlockSpec((tm, GROUP), lambda i: (i, 0)),
                   pl.BlockSpec((tm, GROUP), lambda i: (i, 0)),
                   pl.BlockSpec((D_MODEL, D_MODEL), lambda i: (0, 0))],
        out_shape=[_sds((t, GROUP)), _sds((t, GROUP)), _sds((D_MODEL, D_MODEL))],
        compiler_params=_params(("arbitrary",)),
    )(dh, ya, yb, w)


def _inproj_bwd_dx(dres, h, g, w, da, db):
    t = h.shape[0]
    tm = _row_tile(t, 256)

    def body(dres_ref, h_ref, g_ref, w_ref, da_ref, db_ref, dh_ref, dg_ref):

        @pl.when(pl.program_id(0) == 0)
        def _():
            dg_ref[...] = jnp.zeros_like(dg_ref)

        du = jnp.zeros((tm, D_MODEL), F32)
        for i in range(8):
            part = da_ref[i] if i < 4 else db_ref[i - 4]
            du = du + _dot_nt(_bf(part), w_ref[:, pl.ds(i * GROUP, GROUP)])
        xh, r = _rms(h_ref[...])
        dg_ref[...] += jnp.sum(du * xh, axis=0, keepdims=True)
        dh_ref[...] = dres_ref[...] + _rms_bwd(du * g_ref[...], xh, r)

    return pl.pallas_call(
        body, name="inproj_bwd_dx", grid=(t // tm,),
        in_specs=[pl.BlockSpec((tm, D_MODEL), lambda i: (i, 0)),
                  pl.BlockSpec((tm, D_MODEL), lambda i: (i, 0)),
                  pl.BlockSpec((1, D_MODEL), lambda i: (0, 0)),
                  pl.BlockSpec((D_MODEL, D_IN), lambda i: (0, 0)),
                  pl.BlockSpec((4, tm, GROUP), lambda i: (0, i, 0)),
                  pl.BlockSpec((4, tm, GROUP), lambda i: (0, i, 0))],
        out_specs=[pl.BlockSpec((tm, D_MODEL), lambda i: (i, 0)),
                   pl.BlockSpec((1, D_MODEL), lambda i: (0, 0))],
        out_shape=[_sds((t, D_MODEL)), _sds((1, D_MODEL))],
        compiler_params=_params(("arbitrary",)),
    )(dres, h, g, w, da, db)


def _inproj_bwd_dw(h, g, stacked):
    t = h.shape[0]
    tm = _row_tile(t, 512)

    def body(h_ref, g_ref, d_ref, dw_ref):
        @pl.when(pl.program_id(1) == 0)
        def _():
            dw_ref[...] = jnp.zeros_like(dw_ref)

        xh, _ = _rms(h_ref[...])
        dw_ref[...] += _dot_tn(_bf(xh * g_ref[...]), _bf(d_ref[...]))

    return pl.pallas_call(
        body, name="inproj_bwd_dw", grid=(4, t // tm),
        in_specs=[pl.BlockSpec((tm, D_MODEL), lambda j, i: (i, 0)),
                  pl.BlockSpec((1, D_MODEL), lambda j, i: (0, 0)),
                  pl.BlockSpec((None, tm, GROUP), lambda j, i: (j, i, 0))],
        out_specs=pl.BlockSpec((None, D_MODEL, GROUP), lambda j, i: (j, 0, 0)),
        out_shape=_sds((4, D_MODEL, GROUP)), compiler_params=_params(("arbitrary", "arbitrary")),
    )(h, g, stacked)


def _final(h, g, target):
    t = h.shape[0]
    tm = _row_tile(t, 512)

    def body(h_ref, g_ref, t_ref, dh_ref, dg_ref, loss_ref):
        @pl.when(pl.program_id(0) == 0)
        def _():
            dg_ref[...] = jnp.zeros_like(dg_ref)
            loss_ref[...] = jnp.zeros_like(loss_ref)

        xh, r = _rms(h_ref[...])
        gg = g_ref[...]
        err = xh * gg - t_ref[...]
        part = 0.5 * jnp.sum(jnp.mean(err * err, axis=-1, keepdims=True), axis=0, keepdims=True)
        loss_ref[...] += jnp.broadcast_to(part, loss_ref.shape)
        dy = err * (1.0 / D_MODEL)
        dg_ref[...] += jnp.sum(dy * xh, axis=0, keepdims=True)
        dh_ref[...] = _rms_bwd(dy * gg, xh, r)

    return pl.pallas_call(
        body, name="final", grid=(t // tm,),
        in_specs=[pl.BlockSpec((tm, D_MODEL), lambda i: (i, 0)),
                  pl.BlockSpec((1, D_MODEL), lambda i: (0, 0)),
                  pl.BlockSpec((tm, D_MODEL), lambda i: (i, 0))],
        out_specs=[pl.BlockSpec((tm, D_MODEL), lambda i: (i, 0)),
                   pl.BlockSpec((1, D_MODEL), lambda i: (0, 0)),
                   pl.BlockSpec((1, 128), lambda i: (0, 0))],
        out_shape=[_sds((t, D_MODEL)), _sds((1, D_MODEL)), _sds((1, 128))],
        compiler_params=_params(("arbitrary",)),
    )(h, g, target)


def _hgrn_consts():
    c, nl = HG_CHUNK, HG_LEVELS
    t = np.arange(c)
    tril = np.tril(np.ones((c, c), np.float32))
    masks = np.zeros((nl + 1, c, c), np.float32)
    masks[0] = np.eye(c, dtype=np.float32)
    dmat = np.zeros(((nl + 2) * c, c), np.float32)
    dmat[0:c] = tril
    for l in range(nl):
        m = c >> (l + 1)
        blk = t // (2 * m)
        r = blk * 2 * m + m - 1
        upper = (t % (2 * m)) >= m
        masks[l + 1] = ((blk[:, None] == blk[None, :]) & upper[:, None] & (~upper)[None, :]).astype(np.float32)
        dmat[(l + 1) * c:(l + 2) * c] = tril[t] - tril[r]
    dmat[(nl + 1) * c:] = np.triu(np.ones((c, c), np.float32), k=1)
    return jnp.asarray(masks), jnp.asarray(dmat, BF16)


def _hgrn_common(aq, af, lb, dmat_ref):
    c, nl = HG_CHUNK, HG_LEVELS
    sq = _sigmoid(aq)
    q = aq * sq
    sneg = _sigmoid(-af)
    kk = (1.0 - lb) * sneg
    logf = jnp.log1p(-kk)
    dm = dmat_ref[pl.ds(0, (nl + 1) * c), :]
    lhi, llo = _split(logf)
    x_all = _dot(dm, lhi) + _dot(dm, llo)
    b = x_all[0:c]
    b_last = jnp.sum(logf, axis=0, keepdims=True)
    return sq, q, sneg, kk, x_all, b, b_last


def _hgrn_level(x_all, l, q, kk):
    c = HG_CHUNK
    x = x_all[(l + 1) * c:(l + 2) * c]
    qf = jnp.exp(jnp.minimum(x, 0.0))
    kf = jnp.exp(-jnp.maximum(x, 0.0))
    return qf, kf, _bf(q * qf), _bf(kk * kf)


def _hgrn_scores(x_all, q, kk, mask_ref):
    p = mask_ref[0] * _dot_nt(_bf(q), _bf(kk))
    for l in range(HG_LEVELS):
        _, _, ql, kl = _hgrn_level(x_all, l, q, kk)
        p = p + mask_ref[l + 1] * _dot_nt(ql, kl)
    return p


def _hgrn_specs(n_chunks, rev):
    c = HG_CHUNK
    cidx = (lambda n: n_chunks - 1 - n) if rev else (lambda n: n)
    col = lambda g: pl.BlockSpec((c, A_D), lambda h, n: (cidx(n), g * A_HEADS + h))
    vec = pl.BlockSpec((1, A_D), lambda h, n: (0, h))
    mask = pl.BlockSpec((HG_LEVELS + 1, c, c), lambda h, n: (0, 0, 0))
    dmat = pl.BlockSpec(((HG_LEVELS + 2) * c, c), lambda h, n: (0, 0))
    return cidx, col, vec, mask, dmat


def _hgrn_fwd(proj, lb, gain):
    t = proj.shape[0]
    c = HG_CHUNK
    nch = t // c
    masks, dmat = _hgrn_consts()
    cidx, col, vec, mask_spec, dmat_spec = _hgrn_specs(nch, False)

    def body(aq_ref, af_ref, ai_ref, ag_ref, lb_ref, gain_ref, mask_ref, dmat_ref, y_ref, st_ref, s_scr):
        @pl.when(pl.program_id(1) == 0)
        def _():
            s_scr[...] = jnp.zeros_like(s_scr)

        v = ai_ref[...]
        ag = ag_ref[...]
        _, q, _, kk, x_all, b, b_last = _hgrn_common(aq_ref[...], af_ref[...], lb_ref[...], dmat_ref)
        p = _hgrn_scores(x_all, q, kk, mask_ref)
        s = s_scr[...]
        st_ref[...] = s
        vb = _bf(v)
        o = _dot(_bf(p), vb) + _dot_nt(_bf(q * jnp.exp(b)), _bf(s))
        s_scr[...] = s * jnp.exp(b_last) + _dot_tn(vb, _bf(kk * jnp.exp(b_last - b)))
        oh, _ = _rms(o)
        y_ref[...] = oh * gain_ref[...] * (ag * _sigmoid(ag))

    return pl.pallas_call(
        body, name="hgrn_fwd", grid=(A_HEADS, nch),
        in_specs=[col(0), col(1), col(2), col(3), vec, vec, mask_spec, dmat_spec],
        out_specs=[pl.BlockSpec((c, A_D), lambda h, n: (n, h)),
                   pl.BlockSpec((None, None, A_D, A_D), lambda h, n: (h, n, 0, 0))],
        out_shape=[_sds((t, GROUP)), _sds((A_HEADS, nch, A_D, A_D))],
        scratch_shapes=[pltpu.VMEM((A_D, A_D), F32)],
        compiler_params=_params(("arbitrary", "arbitrary")),
    )(proj, proj, proj, proj, lb, gain, masks, dmat)


def _hgrn_bwd(proj, lb, gain, states, dya):
    t = proj.shape[0]
    c, nl = HG_CHUNK, HG_LEVELS
    nch = t // c
    masks, dmat = _hgrn_consts()
    cidx, col, vec, mask_spec, dmat_spec = _hgrn_specs(nch, True)

    def body(aq_ref, af_ref, ai_ref, ag_ref, lb_ref, gain_ref, mask_ref, dmat_ref, st_ref, dy_ref,
             da_ref, dlb_ref, dgain_ref, ds_scr, z_scr):
        @pl.when(pl.program_id(1) == 0)
        def _():
            ds_scr[...] = jnp.zeros_like(ds_scr)
            dlb_ref[...] = jnp.zeros_like(dlb_ref)
            dgain_ref[...] = jnp.zeros_like(dgain_ref)

        aq = aq_ref[...]
        v = ai_ref[...]
        ag = ag_ref[...]
        lb = lb_ref[...]
        gain = gain_ref[...]
        dy = dy_ref[...]
        sq, q, sneg, kk, x_all, b, b_last = _hgrn_common(aq, af_ref[...], lb, dmat_ref)
        eb = jnp.exp(b)
        ebl = jnp.exp(b_last - b)
        ebl_row = jnp.exp(b_last)
        qe = _bf(q * eb)
        ke = _bf(kk * ebl)
        vb = _bf(v)
        s = st_ref[...]
        sb = _bf(s)
        ds = ds_scr[...]
        dsb = _bf(ds)

        pb = _bf(_hgrn_scores(x_all, q, kk, mask_ref))
        o = _dot(pb, vb) + _dot_nt(qe, sb)

        oh, r = _rms(o)
        sg_sig = _sigmoid(ag)
        sg = ag * sg_sig
        da_ref[3] = dy * oh * gain * _silu_grad(ag, sg_sig)
        dgain_ref[...] += jnp.sum(dy * oh * sg, axis=0, keepdims=True)
        do = _bf(_rms_bwd(dy * gain * sg, oh, r))

        dp = _dot_nt(do, vb)
        da_ref[2] = _dot_tn(pb, do) + _dot_nt(ke, dsb)
        dpd = jnp.sum(mask_ref[0] * dp, axis=1, keepdims=True)
        dq_s = eb * _dot(do, sb)
        dk_s = ebl * _dot(vb, dsb)
        z_scr[pl.ds(0, c), :] = q * dq_s
        z_scr[pl.ds((nl + 1) * c, c), :] = kk * dk_s
        dq = dq_s + dpd * kk
        dk = dk_s + dpd * q
        for l in range(nl):
            qf, kf, ql, kl = _hgrn_level(x_all, l, q, kk)
            dpl = _bf(mask_ref[l + 1] * dp)
            dq_l = qf * _dot(dpl, kl)
            dk_l = kf * _dot_tn(dpl, ql)
            z_scr[pl.ds((l + 1) * c, c), :] = q * dq_l - kk * dk_l
            dq = dq + dq_l
            dk = dk + dk_l

        zhi, zlo = _split(z_scr[...])
        dlogf = (_dot_tn(dmat_ref[...], zhi) + _dot_tn(dmat_ref[...], zlo)
                 + ebl_row * jnp.sum(ds * s, axis=0, keepdims=True))
        dkk = dk - dlogf / (1.0 - kk)
        da_ref[1] = dkk * (1.0 - lb) * (-(sneg * (1.0 - sneg)))
        dlb_ref[...] += jnp.sum(dkk * (-sneg), axis=0, keepdims=True)
        da_ref[0] = dq * _silu_grad(aq, sq)
        ds_scr[...] = ds * ebl_row + _dot_tn(do, qe)

    out_col = pl.BlockSpec((c, A_D), lambda h, n: (cidx(n), h))
    return pl.pallas_call(
        body, name="hgrn_bwd", grid=(A_HEADS, nch),
        in_specs=[col(0), col(1), col(2), col(3), vec, vec, mask_spec, dmat_spec,
                  pl.BlockSpec((None, None, A_D, A_D), lambda h, n: (h, cidx(n), 0, 0)),
                  out_col],
        out_specs=[pl.BlockSpec((4, c, A_D), lambda h, n: (0, cidx(n), h)), vec, vec],
        out_shape=[_sds((4, t, GROUP))] + [_sds((1, GROUP))] * 2,
        scratch_shapes=[pltpu.VMEM((A_D, A_D), F32), pltpu.VMEM(((nl + 2) * c, A_D), F32)],
        compiler_params=_params(("arbitrary", "arbitrary")),
    )(proj, proj, proj, proj, lb, gain, masks, dmat, states, dya)


def _sb_consts():
    j = np.arange(SB_TK)
    ones = np.ones((SB_TK, SB_TK), np.float32)
    strict = np.concatenate([(j[:, None] > j[None, :]).astype(np.float32), ones], axis=1)
    incl = np.concatenate([(j[:, None] >= j[None, :]).astype(np.float32), ones], axis=1)
    return jnp.asarray(strict, BF16), jnp.asarray(incl, BF16)


def _sb_softplus(z, masked):
    lg = jnp.log(1.0 + jnp.exp(-jnp.abs(z)))
    sp = jnp.maximum(z, 0.0) + lg
    logsig = jnp.minimum(z, 0.0) - lg
    mask = None
    if masked:
        mask = lax.broadcasted_iota(jnp.int32, z.shape, 1) < lax.broadcasted_iota(jnp.int32, z.shape, 0)
        sp = jnp.where(mask, sp, 0.0)
    return mask, sp, logsig


def _sb_sweep(qi, group_fn, state):
    nd = SB_TQ // SB_TK
    state = group_fn([(pl.multiple_of((qi * nd + d) * SB_TK, SB_TK), d * SB_TK, True) for d in reversed(range(nd))],
                     state)

    def step(j, st):
        return group_fn([(pl.multiple_of(((qi - j) * nd - 1 - g) * SB_TK, SB_TK), 0, False) for g in range(nd)], st)

    return lax.fori_loop(0, qi, step, state)


def _set_rows(r0, full, new):
    return new if r0 == 0 else jnp.concatenate([full[:r0], new], axis=0)


def _sb_fwd(q, k, v, bg, gain):
    t = q.shape[1]
    tq = SB_TQ
    strict, _ = _sb_consts()

    def body(q_ref, k_ref, v_ref, bg_ref, gain_ref, m_ref, o_ref, y_ref):
        qb = q_ref[...]
        cmat = m_ref[...]

        def group(tiles, state):
            carry, acc = state
            kv = [(k_ref[pl.ds(off, SB_TK), :], v_ref[pl.ds(off, SB_TK), :]) for off, _, _ in tiles]
            zs = [_dot_nt(qb[r0:], kb) for (_, r0, _), (kb, _) in zip(tiles, kv)]
            sps = [_sb_softplus(z, masked) for z, (_, _, masked) in zip(zs, tiles)]
            cs2s = []
            for _, sp, _ in sps:
                hi, lo = _split(sp)
                cs2s.append(_dot(hi, cmat) + _dot(lo, cmat))
            ws = []
            for (mask, _, logsig), cs2, (_, r0, masked) in zip(sps, cs2s, tiles):
                w = jnp.exp(logsig - cs2[:, :SB_TK] - carry[r0:])
                ws.append(_split(jnp.where(mask, w, 0.0) if masked else w))
                carry = _set_rows(r0, carry, carry[r0:] + cs2[:, SB_TK:])
            for (whi, wlo), (_, vb), (_, r0, _) in zip(ws, kv, tiles):
                acc = _set_rows(r0, acc, acc[r0:] + _dot(whi, vb) + _dot(wlo, vb))
            return carry, acc

        _, o = _sb_sweep(pl.program_id(1), group, (jnp.zeros((tq, SB_TK), F32), jnp.zeros((tq, B_D), F32)))
        o_ref[...] = o
        oh, _ = _rms(o)
        bg = bg_ref[...]
        y_ref[...] = oh * gain_ref[...] * (bg * _sigmoid(bg))

    blk = pl.BlockSpec((None, tq, B_D), lambda h, i: (h, i, 0))
    full = pl.BlockSpec((None, t, B_D), lambda h, i: (h, 0, 0))
    return pl.pallas_call(
        body, name="sb_fwd", grid=(B_HEADS, t // tq),
        in_specs=[blk, full, full, blk, pl.BlockSpec((None, 1, B_D), lambda h, i: (h, 0, 0)),
                  pl.BlockSpec((SB_TK, 2 * SB_TK), lambda h, i: (0, 0))],
        out_specs=[blk, blk],
        out_shape=[_sds((B_HEADS, t, B_D)), _sds((B_HEADS, t, B_D))],
        compiler_params=_params(("arbitrary", "arbitrary")),
    )(q, k, v, bg, gain, strict)


def _sb_bwd(q, k, v, o, dy, bg, gain):
    t = q.shape[1]
    tq = SB_TQ
    strict, incl = _sb_consts()

    def body(q_ref, k_ref, v_ref, o_ref, dy_ref, bg_ref, gain_ref, ms_ref, mi_ref,
             dq_ref, dk_ref, dv_ref, dbg_ref, dgain_ref):
        qi = pl.program_id(1)

        @pl.when(qi == 0)
        def _():
            dk_ref[...] = jnp.zeros_like(dk_ref)
            dv_ref[...] = jnp.zeros_like(dv_ref)
            dgain_ref[...] = jnp.zeros_like(dgain_ref)

        qb = q_ref[...]
        cmat = ms_ref[...]
        imat = mi_ref[...]
        o = o_ref[...]
        dy = dy_ref[...]
        bg = bg_ref[...]
        gain = gain_ref[...]
        oh, r = _rms(o)
        sig = _sigmoid(bg)
        sg = bg * sig
        dbg_ref[...] = dy * oh * gain * _silu_grad(bg, sig)
        dgain_ref[...] += jnp.sum(dy * oh * sg, axis=0, keepdims=True)
        do = _bf(_rms_bwd(dy * gain * sg, oh, r))
        total = jnp.broadcast_to(jnp.sum(do.astype(F32) * o, axis=1, keepdims=True), (tq, SB_TK))

        def group(tiles, state):
            carry, gcarry, dq = state
            kv = [(k_ref[pl.ds(off, SB_TK), :], v_ref[pl.ds(off, SB_TK), :]) for off, _, _ in tiles]
            zs = [_dot_nt(qb[r0:], kb) for (_, r0, _), (kb, _) in zip(tiles, kv)]
            dws = [_dot_nt(do[r0:], vb) for (_, r0, _), (_, vb) in zip(tiles, kv)]
            sps = [_sb_softplus(z, masked) for z, (_, _, masked) in zip(zs, tiles)]
            cs2s = []
            for _, sp, _ in sps:
                hi, lo = _split(sp)
                cs2s.append(_dot(hi, cmat) + _dot(lo, cmat))
            ws, gs = [], []
            for (mask, _, logsig), cs2, dw, (_, r0, masked) in zip(sps, cs2s, dws, tiles):
                w = jnp.exp(logsig - cs2[:, :SB_TK] - carry[r0:])
                w = jnp.where(mask, w, 0.0) if masked else w
                ws.append(_bf(w))
                gs.append(dw * w)
                carry = _set_rows(r0, carry, carry[r0:] + cs2[:, SB_TK:])
            s2s = []
            for g in gs:
                ghi, glo = _split(g)
                s2s.append(_dot(ghi, imat) + _dot(glo, imat))
            dzs = []
            for (mask, _, logsig), g, s2, (_, r0, masked) in zip(sps, gs, s2s, tiles):
                before = total[r0:] - gcarry[r0:] - s2[:, :SB_TK]
                sig_z = jnp.exp(logsig)
                dz = g * (1.0 - sig_z) - sig_z * before
                dzs.append(_bf(jnp.where(mask, dz, 0.0) if masked else dz))
                gcarry = _set_rows(r0, gcarry, gcarry[r0:] + s2[:, SB_TK:])
            for dz, wb, (kb, _), (off, r0, _) in zip(dzs, ws, kv, tiles):
                dq = _set_rows(r0, dq, dq[r0:] + _dot(dz, kb))
                dk_ref[pl.ds(off, SB_TK), :] += _dot_tn(dz, qb[r0:])
                dv_ref[pl.ds(off, SB_TK), :] += _dot_tn(wb, do[r0:])
            return carry, gcarry, dq

        zero = jnp.zeros((tq, SB_TK), F32)
        _, _, dq = _sb_sweep(qi, group, (zero, zero, jnp.zeros((tq, B_D), F32)))
        dq_ref[...] = dq

    blk = pl.BlockSpec((None, tq, B_D), lambda h, i: (h, i, 0))
    full = pl.BlockSpec((None, t, B_D), lambda h, i: (h, 0, 0))
    vec = pl.BlockSpec((None, 1, B_D), lambda h, i: (h, 0, 0))
    mat = pl.BlockSpec((SB_TK, 2 * SB_TK), lambda h, i: (0, 0))
    return pl.pallas_call(
        body, name="sb_bwd", grid=(B_HEADS, t // tq),
        in_specs=[blk, full, full, blk, blk, blk, vec, mat, mat],
        out_specs=[blk, full, full, blk, vec],
        out_shape=[_sds((B_HEADS, t, B_D))] * 4 + [_sds((B_HEADS, 1, B_D))],
        compiler_params=_params(("arbitrary", "arbitrary")),
    )(q, k, v, o, dy, bg, gain, strict, incl)


def _to_heads(x, dtype=F32):
    t = x.shape[0]
    return x.reshape(t, B_HEADS, B_D).transpose(1, 0, 2).astype(dtype)


def _from_heads(x):
    return x.transpose(1, 0, 2).reshape(x.shape[1], GROUP)


def _adamw(w, g, m, v):
    rows, cols = w.shape
    tr = rows
    for cand in (400, 256, 128, 64, 32, 16, 8):
        if rows % cand == 0:
            tr = cand
            break

    def body(w_ref, g_ref, m_ref, v_ref, d_ref, nm_ref, nv_ref):
        g_ = g_ref[...]
        m_ = ADAM_B1 * m_ref[...] + (1.0 - ADAM_B1) * g_
        v_ = ADAM_B2 * v_ref[...] + (1.0 - ADAM_B2) * (g_ * g_)
        m_hat = m_ / (1.0 - ADAM_B1 ** ADAM_STEP)
        v_hat = v_ / (1.0 - ADAM_B2 ** ADAM_STEP)
        d_ref[...] = -ADAM_LR * (m_hat / (jnp.sqrt(v_hat) + ADAM_EPS) + ADAM_WD * w_ref[...])
        nm_ref[...] = m_
        nv_ref[...] = v_

    spec = pl.BlockSpec((tr, cols), lambda i: (i, 0))
    return pl.pallas_call(
        body, name="adamw", grid=(rows // tr,), in_specs=[spec] * 4, out_specs=[spec] * 3,
        out_shape=[_sds((rows, cols))] * 3, compiler_params=_params(("arbitrary",)),
    )(w, g, m, v)


def _sum_leading(x):
    n, rows, cols = x.shape
    tr = 400 if rows % 400 == 0 else 200

    def body(x_ref, o_ref):
        acc = x_ref[0]
        for i in range(1, n):
            acc = acc + x_ref[i]
        o_ref[...] = acc

    return pl.pallas_call(
        body, name="sum_leading", grid=(rows // tr,),
        in_specs=[pl.BlockSpec((n, tr, cols), lambda i: (0, i, 0))],
        out_specs=pl.BlockSpec((tr, cols), lambda i: (i, 0)),
        out_shape=_sds((rows, cols)), compiler_params=_params(("arbitrary",)),
    )(x)


_ANY = pl.BlockSpec(memory_space=pl.ANY)


def _place():
    return lax.axis_index("x"), lax.axis_index("y"), lax.axis_index("c")


def _gather_weights(flat):
    half, ch, nc = HALF_FLAT, CHUNK_ROWS, N_CHUNK

    def body(x_ref, out_ref, send_sems, recv_sems, local_sems):
        x, y, c = _place()
        me = 2 * x + y
        sibling = (x, y, 1 - c)

        def rows(shard, hc, r):
            return out_ref.at[shard, pl.ds(hc * half + r * ch, ch), :]

        def copy(k, shard, hc, r, to, src=None):
            return pltpu.make_async_remote_copy(
                src_ref=rows(shard, hc, r) if src is None else src, dst_ref=rows(shard, hc, r),
                send_sem=send_sems.at[k * nc + r], recv_sem=recv_sems.at[k * nc + r], device_id=to, device_id_type=MESH)

        mine = [pltpu.make_async_copy(x_ref.at[pl.ds(r * 2 * ch, 2 * ch), :], out_ref.at[me, pl.ds(r * 2 * ch, 2 * ch), :],
                                      local_sems.at[r]) for r in range(nc)]
        for cp in mine:
            cp.start()
        peers = [me ^ k for k in (1, 2, 3)]
        first = [copy(k, me, c, r, (p >> 1, p & 1, c), src=x_ref.at[pl.ds(c * half + r * ch, ch), :])
                 for k, p in enumerate(peers) for r in range(nc)]
        for cp in first:
            cp.start()
        passed = []
        for k, p in enumerate(peers):
            for r in range(nc):
                copy(k, p, c, r, sibling).wait_recv()
                passed.append(copy(3 + k, p, c, r, sibling))
                passed[-1].start()
        for k, p in enumerate(peers):
            for r in range(nc):
                copy(3 + k, p, 1 - c, r, sibling).wait_recv()
        for cp in first + passed:
            cp.wait_send()
        for cp in mine:
            cp.wait()

    return pl.pallas_call(
        body, name="gather_weights", in_specs=[_ANY], out_specs=_ANY,
        out_shape=_sds((N_SHARD, ROWS_FLAT, D_MODEL), BF16),
        scratch_shapes=[pltpu.SemaphoreType.DMA((6 * nc,)), pltpu.SemaphoreType.DMA((6 * nc,)),
                        pltpu.SemaphoreType.DMA((nc,))],
    )(flat)


def _swap_halves(grads):
    half, ch, nc = HALF_FLAT, CHUNK_ROWS, N_CHUNK

    def body(g_ref, out_ref, send_sems, recv_sems):
        x, y, c = _place()
        copies = [pltpu.make_async_remote_copy(
            src_ref=g_ref.at[j, pl.ds((1 - c) * half + r * ch, ch), :], dst_ref=out_ref.at[j, pl.ds(r * ch, ch), :],
            send_sem=send_sems.at[j * nc + r], recv_sem=recv_sems.at[j * nc + r],
            device_id=(x, y, 1 - c), device_id_type=MESH) for j in range(N_SHARD) for r in range(nc)]
        for cp in copies:
            cp.start()
        for cp in copies:
            cp.wait()

    return pl.pallas_call(
        body, name="swap_halves", in_specs=[_ANY], out_specs=_ANY,
        out_shape=_sds((N_SHARD, half, D_MODEL)),
        scratch_shapes=[pltpu.SemaphoreType.DMA((N_SHARD * nc,)), pltpu.SemaphoreType.DMA((N_SHARD * nc,))],
    )(grads)


def _add_my_half(grads, recv):
    tr = 400
    nb = HALF_FLAT // tr
    core = lax.axis_index("c").astype(jnp.int32).reshape(1)

    def body(c_ref, g_ref, r_ref, o_ref):
        o_ref[...] = g_ref[...] + r_ref[...]

    return pl.pallas_call(
        body, name="add_my_half",
        grid_spec=pltpu.PrefetchScalarGridSpec(
            num_scalar_prefetch=1, grid=(N_SHARD, nb),
            in_specs=[pl.BlockSpec((None, tr, D_MODEL), lambda j, i, c_ref: (j, c_ref[0] * nb + i, 0)),
                      pl.BlockSpec((None, tr, D_MODEL), lambda j, i, c_ref: (j, i, 0))],
            out_specs=pl.BlockSpec((None, tr, D_MODEL), lambda j, i, c_ref: (j, i, 0))),
        out_shape=_sds((N_SHARD, HALF_FLAT, D_MODEL)), compiler_params=_params(("arbitrary", "arbitrary")),
    )(core, grads, recv)


def _scatter_shards(part):
    ch, nc = CHUNK_ROWS, N_CHUNK

    def body(p_ref, out_ref, send_sems, recv_sems, local_sems):
        x, y, c = _place()
        me = 2 * x + y

        def rows(ref, slot, r):
            return ref.at[slot, pl.ds(r * ch, ch), :]

        mine = [pltpu.make_async_copy(rows(p_ref, me, r), rows(out_ref, me, r), local_sems.at[r]) for r in range(nc)]
        for cp in mine:
            cp.start()
        peers = [me ^ k for k in (1, 2, 3)]

        def copy(k, p, r, slot):
            return pltpu.make_async_remote_copy(
                src_ref=rows(p_ref, p, r), dst_ref=rows(out_ref, slot, r), send_sem=send_sems.at[k * nc + r],
                recv_sem=recv_sems.at[k * nc + r], device_id=(p >> 1, p & 1, c), device_id_type=MESH)

        sends = [copy(k, p, r, me) for k, p in enumerate(peers) for r in range(nc)]
        for cp in sends:
            cp.start()
        for k, p in enumerate(peers):
            for r in range(nc):
                copy(k, p, r, p).wait_recv()
        for cp in sends:
            cp.wait_send()
        for cp in mine:
            cp.wait()

    return pl.pallas_call(
        body, name="scatter_shards", in_specs=[_ANY], out_specs=_ANY,
        out_shape=_sds((N_SHARD, HALF_FLAT, D_MODEL)),
        scratch_shapes=[pltpu.SemaphoreType.DMA((3 * nc,)), pltpu.SemaphoreType.DMA((3 * nc,)),
                        pltpu.SemaphoreType.DMA((nc,))],
    )(part)


def _join_halves(mine_half):
    half, ch, nc = HALF_FLAT, CHUNK_ROWS, N_CHUNK

    def body(r_ref, out_ref, send_sems, recv_sems, local_sems):
        x, y, c = _place()

        def rows(hc, r):
            return out_ref.at[pl.ds(hc * half + r * ch, ch), :]

        def copy(hc, r):
            return pltpu.make_async_remote_copy(
                src_ref=r_ref.at[pl.ds(r * ch, ch), :], dst_ref=rows(hc, r), send_sem=send_sems.at[r],
                recv_sem=recv_sems.at[r], device_id=(x, y, 1 - c), device_id_type=MESH)

        mine = [pltpu.make_async_copy(r_ref.at[pl.ds(r * ch, ch), :], rows(c, r), local_sems.at[r]) for r in range(nc)]
        sends = [copy(c, r) for r in range(nc)]
        for cp in mine + sends:
            cp.start()
        for r in range(nc):
            copy(1 - c, r).wait_recv()
        for cp in sends:
            cp.wait_send()
        for cp in mine:
            cp.wait()

    return pl.pallas_call(
        body, name="join_halves", in_specs=[_ANY], out_specs=_ANY,
        out_shape=_sds((ROWS_FLAT, D_MODEL)),
        scratch_shapes=[pltpu.SemaphoreType.DMA((nc,)), pltpu.SemaphoreType.DMA((nc,)), pltpu.SemaphoreType.DMA((nc,))],
    )(mine_half)


def _allreduce_small(vec):
    def body(v_ref, out_ref, buf, send_sems, recv_sems):
        x, y, c = _place()
        me = 4 * x + 2 * y + c
        buf[me] = v_ref[...]
        peers = [me ^ k for k in range(1, N_DEV)]
        sends = [pltpu.make_async_remote_copy(
            src_ref=v_ref, dst_ref=buf.at[me], send_sem=send_sems.at[k], recv_sem=recv_sems.at[k],
            device_id=(p >> 2, (p >> 1) & 1, p & 1), device_id_type=MESH) for k, p in enumerate(peers)]
        for cp in sends:
            cp.start()
        for k, p in enumerate(peers):
            pltpu.make_async_remote_copy(
                src_ref=v_ref, dst_ref=buf.at[p], send_sem=send_sems.at[k], recv_sem=recv_sems.at[k],
                device_id=(p >> 2, (p >> 1) & 1, p & 1), device_id_type=MESH).wait_recv()
        for cp in sends:
            cp.wait_send()
        acc = buf[0]
        for d in range(1, N_DEV):
            acc = acc + buf[d]
        out_ref[...] = acc

    vm = pl.BlockSpec(memory_space=pltpu.VMEM)
    return pl.pallas_call(
        body, name="allreduce_small", in_specs=[vm], out_specs=vm, out_shape=_sds((SMALL_ROWS, 128)),
        scratch_shapes=[pltpu.VMEM((N_DEV, SMALL_ROWS, 128), F32),
                        pltpu.SemaphoreType.DMA((N_DEV - 1,)), pltpu.SemaphoreType.DMA((N_DEV - 1,))],
    )(vec)


def _flatten_shard(w_in, w_out, w_pg, w_pp):
    return jnp.concatenate([w_in.reshape(-1, D_MODEL), w_out.reshape(-1, D_MODEL), w_pg.reshape(-1, D_MODEL),
                            w_pp.reshape(-1, D_MODEL)], axis=0)


def _unflatten_shard(flat):
    a, b, c = ROWS_W_IN, ROWS_W_IN + ROWS_W_OUT, ROWS_W_IN + ROWS_W_OUT + ROWS_W_PG
    q = D_MODEL // N_SHARD
    return (flat[:a].reshape(2, D_MODEL, D_MODEL), flat[a:b].reshape(2, q, D_MODEL),
            flat[b:c].reshape(2, q, D_MODEL), flat[c:].reshape(2, D_PLE, q))


def _full_weights(gathered):
    a, b, c = ROWS_W_IN, ROWS_W_IN + ROWS_W_OUT, ROWS_W_IN + ROWS_W_OUT + ROWS_W_PG
    q = D_MODEL // N_SHARD
    rpp = ROWS_W_PP // 2
    out = []
    for l in range(2):
        w_in = gathered[:, l * D_MODEL:(l + 1) * D_MODEL, :].transpose(1, 0, 2).reshape(D_MODEL, D_IN)
        w_out = gathered[:, a + l * q:a + (l + 1) * q, :].reshape(D_MODEL, D_MODEL)
        w_pg = gathered[:, b + l * q:b + (l + 1) * q, :].reshape(D_MODEL, D_MODEL)
        w_pp = gathered[:, c + l * rpp:c + (l + 1) * rpp, :].reshape(N_SHARD, D_PLE, q).transpose(1, 0, 2)
        out.append((w_in, w_out, w_pg, w_pp.reshape(D_PLE, D_MODEL)))
    return out


def _shard_major(dw_in, dw_out, dw_pg, dw_pp):
    q = D_MODEL // N_SHARD
    rpp = ROWS_W_PP // 2
    parts = []
    for l in range(2):
        g = dw_in[l].reshape(N_SHARD, 2, D_MODEL, GROUP).transpose(0, 2, 1, 3).reshape(N_SHARD, D_MODEL, D_MODEL)
        parts.append(g)
    for l in range(2):
        parts.append(dw_out[l].reshape(N_SHARD, q, D_MODEL))
    for l in range(2):
        parts.append(dw_pg[l].reshape(N_SHARD, q, D_MODEL))
    for l in range(2):
        parts.append(dw_pp[l].reshape(D_PLE, N_SHARD, q).transpose(1, 0, 2).reshape(N_SHARD, rpp, D_MODEL))
    return jnp.concatenate(parts, axis=1)


def _lower_bounds(lb_logits):
    sm = jax.nn.softmax(lb_logits.astype(F32), axis=0)
    return jnp.cumsum(sm, axis=0) - sm[0:1]


def kernel(x, p, norm_mix, w_in, a_out_norm, b_out_norm, w_out, lb_logits, ple_gate_norm, w_ple_gate, w_ple_proj, ple_post_norm, final_norm, loss_target, m_norm_mix, m_w_in, m_a_out_norm, m_b_out_norm, m_w_out, m_lb_logits, m_ple_gate_norm, m_w_ple_gate, m_w_ple_proj, m_ple_post_norm, m_final_norm, v_norm_mix, v_w_in, v_a_out_norm, v_b_out_norm, v_w_out, v_lb_logits, v_ple_gate_norm, v_w_ple_gate, v_w_ple_proj, v_ple_post_norm, v_final_norm):
    t = x.shape[1]
    h0 = x.reshape(t, D_MODEL)
    target = loss_target.reshape(t, D_MODEL)
    pl_in = p.reshape(2, t, D_PLE)

    w_flat = _flatten_shard(w_in, w_out, w_ple_gate, w_ple_proj)
    weights = _full_weights(_gather_weights(w_flat.astype(BF16)))
    lbs, lbs_vjp = jax.vjp(_lower_bounds, lb_logits)

    saved = []
    h = h0
    for l in range(2):
        wi, wo, wpg, wpp = weights[l]
        g_mix = norm_mix[l].reshape(1, D_MODEL)
        lb = lbs[l].reshape(1, GROUP)
        ga = a_out_norm[l].reshape(1, GROUP)
        gb = b_out_norm[l].reshape(B_HEADS, 1, B_D)
        proj = _inproj(h, g_mix, wi)
        ya, states = _hgrn_fwd(proj, lb, ga)
        bq = _to_heads(proj[:, 4 * GROUP:5 * GROUP] * (B_D ** -0.5), BF16)
        bk = _to_heads(proj[:, 5 * GROUP:6 * GROUP], BF16)
        bv = _to_heads(proj[:, 6 * GROUP:7 * GROUP], BF16)
        bg = _to_heads(proj[:, 7 * GROUP:8 * GROUP])
        ob, yb_h = _sb_fwd(bq, bk, bv, bg, gb)
        yb = _from_heads(yb_h)
        h1 = _outproj(h, ya, yb, wo)
        g_post = ple_post_norm[l].reshape(1, D_MODEL)
        g_gate = ple_gate_norm[l].reshape(1, D_MODEL)
        h2 = _ple_fwd(h1, pl_in[l], wpp, wpg, g_post, g_gate)
        saved.append((h, proj, states, ya, yb, bq, bk, bv, bg, ob, h1))
        h = h2

    dh, d_final, loss_part = _final(h, final_norm.reshape(1, D_MODEL), target)

    dw_in_l, dw_out_l, dw_pg_l, dw_pp_l = [None] * 2, [None] * 2, [None] * 2, [None] * 2
    d_mix, d_a, d_b, d_lb, d_gate, d_post = [None] * 2, [None] * 2, [None] * 2, [None] * 2, [None] * 2, [None] * 2
    for l in (1, 0):
        wi, wo, wpg, wpp = weights[l]
        h_in, proj, states, ya, yb, bq, bk, bv, bg, ob, h1 = saved[l]
        g_mix = norm_mix[l].reshape(1, D_MODEL)
        lb = lbs[l].reshape(1, GROUP)
        ga = a_out_norm[l].reshape(1, GROUP)
        gb = b_out_norm[l].reshape(B_HEADS, 1, B_D)
        g_post = ple_post_norm[l].reshape(1, D_MODEL)
        g_gate = ple_gate_norm[l].reshape(1, D_MODEL)
        dh1, dw_pg_l[l], dw_pp_l[l], d_gate[l], d_post[l] = _ple_bwd(dh, h1, pl_in[l], wpp, wpg, g_post, g_gate)
        dya, dyb, dw_out_l[l] = _outproj_bwd(dh1, ya, yb, wo)
        dbq, dbk, dbv, dbg, dgb = _sb_bwd(bq, bk, bv, ob, _to_heads(dyb), bg, gb)
        da, d_lb[l], d_a[l] = _hgrn_bwd(proj, lb, ga, states, dya)
        d_b[l] = dgb.reshape(1, GROUP)
        db = jnp.stack([_from_heads(dbq) * (B_D ** -0.5), _from_heads(dbk), _from_heads(dbv), _from_heads(dbg)])
        dw_in_l[l] = jnp.concatenate([_inproj_bwd_dw(h_in, g_mix, da), _inproj_bwd_dw(h_in, g_mix, db)])
        dh, d_mix[l] = _inproj_bwd_dx(dh1, h_in, g_mix, wi, da, db)
    grad_x = dh.reshape(x.shape)

    g_full = _shard_major(dw_in_l, dw_out_l, dw_pg_l, dw_pp_l)
    chip_sum = _add_my_half(g_full, _swap_halves(g_full))
    g_flat = _join_halves(_sum_leading(_scatter_shards(chip_sum)))
    g_w_in, g_w_out, g_w_pg, g_w_pp = _unflatten_shard(g_flat)

    small = jnp.concatenate([
        jnp.concatenate(d_mix).reshape(-1, 128), jnp.concatenate(d_a).reshape(-1, 128),
        jnp.concatenate(d_b).reshape(-1, 128), jnp.concatenate(d_lb).reshape(-1, 128),
        jnp.concatenate(d_gate).reshape(-1, 128), jnp.concatenate(d_post).reshape(-1, 128),
        d_final.reshape(-1, 128), jnp.broadcast_to(loss_part, (8, 128))], axis=0)
    small = _allreduce_small(small)
    loss = small[80, 0]
    g_norm_mix = small[0:16].reshape(2, D_MODEL)
    g_a = small[16:24].reshape(2, GROUP)
    g_b = small[24:32].reshape(2, GROUP)
    (g_lb,) = lbs_vjp(small[32:40].reshape(2, GROUP))
    g_gate = small[40:56].reshape(2, D_MODEL)
    g_post = small[56:72].reshape(2, D_MODEL)
    g_final = small[72:80].reshape(D_MODEL)

    d_flat, nm_flat, nv_flat = _adamw(w_flat, g_flat, _flatten_shard(m_w_in, m_w_out, m_w_ple_gate, m_w_ple_proj),
                                      _flatten_shard(v_w_in, v_w_out, v_w_ple_gate, v_w_ple_proj))
    d_w_in, d_w_out, d_w_pg, d_w_pp = _unflatten_shard(d_flat)
    nm_w_in, nm_w_out, nm_w_pg, nm_w_pp = _unflatten_shard(nm_flat)
    nv_w_in, nv_w_out, nv_w_pg, nv_w_pp = _unflatten_shard(nv_flat)

    small_w = [norm_mix, a_out_norm, b_out_norm, lb_logits, ple_gate_norm, ple_post_norm, final_norm]
    small_g = [g_norm_mix, g_a, g_b, g_lb, g_gate, g_post, g_final]
    small_m = [m_norm_mix, m_a_out_norm, m_b_out_norm, m_lb_logits, m_ple_gate_norm, m_ple_post_norm, m_final_norm]
    small_v = [v_norm_mix, v_a_out_norm, v_b_out_norm, v_lb_logits, v_ple_gate_norm, v_ple_post_norm, v_final_norm]
    pack = lambda arrs: jnp.concatenate([a.reshape(-1, 128) for a in arrs], axis=0)
    ds, nms, nvs = _adamw(pack(small_w), pack(small_g), pack(small_m), pack(small_v))

    def unpack(packed):
        out, r = [], 0
        for a in small_w:
            n = a.size // 128
            out.append(packed[r:r + n].reshape(a.shape))
            r += n
        return out

    d_s, nm_s, nv_s = unpack(ds), unpack(nms), unpack(nvs)

    def ordered(s, big):
        return [s[0], big[0], s[1], s[2], big[1], s[3], s[4], big[2], big[3], s[5], s[6]]

    grads = ordered(small_g, [g_w_in, g_w_out, g_w_pg, g_w_pp])
    deltas = ordered(d_s, [d_w_in, d_w_out, d_w_pg, d_w_pp])
    new_m = ordered(nm_s, [nm_w_in, nm_w_out, nm_w_pg, nm_w_pp])
    new_v = ordered(nv_s, [nv_w_in, nv_w_out, nv_w_pg, nv_w_pp])
    return (loss, grad_x, *grads, *deltas, *new_m, *new_v)
```

```python
import functools
import math

import numpy as np
import jax
import jax.numpy as jnp
from jax import lax
from jax.experimental import pallas as pl
from jax.experimental.pallas import tpu as pltpu

F32 = jnp.float32
BF16 = jnp.bfloat16
MESH = pl.DeviceIdType.MESH

D_MODEL = 1024
D_PLE = 256
D_IN = 4096
A_HEADS, A_D = 4, 128
B_HEADS, B_D = 8, 64
GROUP = 512
EPS = 1e-6
N_SHARD = 4
N_DEV = 8

HG_CHUNK = 128
HG_LEVELS = 7
SB_TQ = 512
SB_TK = 128

ADAM_LR, ADAM_B1, ADAM_B2, ADAM_EPS, ADAM_WD, ADAM_STEP = 0.001, 0.9, 0.999, 1e-08, 0.01, 10

VMEM_LIMIT = 48 * 1024 * 1024

ROWS_W_IN = 2 * D_MODEL
ROWS_W_OUT = 2 * (D_MODEL // N_SHARD)
ROWS_W_PG = 2 * (D_MODEL // N_SHARD)
ROWS_W_PP = 2 * (D_PLE * (D_MODEL // N_SHARD) // D_MODEL)
ROWS_FLAT = ROWS_W_IN + ROWS_W_OUT + ROWS_W_PG + ROWS_W_PP
HALF_FLAT = ROWS_FLAT // 2
N_CHUNK = 10
CHUNK_ROWS = HALF_FLAT // N_CHUNK

SMALL_ROWS = 88


def _sds(shape, dtype=F32):
    return jax.ShapeDtypeStruct(shape, dtype)


def _params(sem=None):
    kw = dict(vmem_limit_bytes=VMEM_LIMIT)
    if sem is not None:
        kw["dimension_semantics"] = sem
    return pltpu.CompilerParams(**kw)


def _dot(a, b, precision=None):
    return lax.dot_general(a, b, (((1,), (0,)), ((), ())), preferred_element_type=F32, precision=precision)


def _dot_nt(a, b, precision=None):
    return lax.dot_general(a, b, (((1,), (1,)), ((), ())), preferred_element_type=F32, precision=precision)


def _dot_tn(a, b, precision=None):
    return lax.dot_general(a, b, (((0,), (0,)), ((), ())), preferred_element_type=F32, precision=precision)


def _bf(x):
    return x.astype(BF16)


def _split(x):
    hi = x.astype(BF16)
    lo = (x - hi.astype(F32)).astype(BF16)
    return hi, lo


def _rms(x):
    r = lax.rsqrt(jnp.mean(x * x, axis=-1, keepdims=True) + EPS)
    return x * r, r


def _rms_bwd(dxh, xh, r):
    return r * (dxh - xh * jnp.mean(dxh * xh, axis=-1, keepdims=True))


def _sigmoid(x):
    return 1.0 / (1.0 + jnp.exp(-x))


def _silu_grad(x, sig):
    return sig * (1.0 + x * (1.0 - sig))


def _row_tile(t, want):
    return min(t, want)


def _inproj(h, g, w):
    t = h.shape[0]
    tm = _row_tile(t, 512)
    tn = 1024
    scale = jnp.ones((1, D_IN), F32).at[:, 4 * GROUP:5 * GROUP].set(B_D ** -0.5)

    def body(h_ref, g_ref, w_ref, s_ref, o_ref, ob_ref):
        xh, _ = _rms(h_ref[...])
        acc = _dot(_bf(xh * g_ref[...]), w_ref[...])
        o_ref[...] = acc
        ob_ref[...] = _bf(acc * s_ref[...])

    return pl.pallas_call(
        body, name="inproj", grid=(D_IN // tn, t // tm),
        in_specs=[pl.BlockSpec((tm, D_MODEL), lambda j, i: (i, 0)),
                  pl.BlockSpec((1, D_MODEL), lambda j, i: (0, 0)),
                  pl.BlockSpec((D_MODEL, tn), lambda j, i: (0, j)),
                  pl.BlockSpec((1, tn), lambda j, i: (0, j))],
        out_specs=[pl.BlockSpec((tm, tn), lambda j, i: (i, j)), pl.BlockSpec((tm, tn), lambda j, i: (i, j))],
        out_shape=[_sds((t, D_IN)), _sds((t, D_IN), BF16)], compiler_params=_params(("arbitrary", "arbitrary")),
    )(h, g, w, scale)


def _outproj(h, ya, yb, w):
    t = h.shape[0]
    tm = _row_tile(t, 512)

    def body(h_ref, ya_ref, yb_ref, w_ref, o_ref):
        o_ref[...] = (h_ref[...] + _dot(_bf(ya_ref[...]), w_ref[pl.ds(0, GROUP), :])
                      + _dot(_bf(yb_ref[...]), w_ref[pl.ds(GROUP, GROUP), :]))

    return pl.pallas_call(
        body, name="outproj", grid=(t // tm,),
        in_specs=[pl.BlockSpec((tm, D_MODEL), lambda i: (i, 0)),
                  pl.BlockSpec((tm, GROUP), lambda i: (i, 0)),
                  pl.BlockSpec((tm, GROUP), lambda i: (i, 0)),
                  pl.BlockSpec((D_MODEL, D_MODEL), lambda i: (0, 0))],
        out_specs=pl.BlockSpec((tm, D_MODEL), lambda i: (i, 0)),
        out_shape=_sds((t, D_MODEL)), compiler_params=_params(("arbitrary",)),
    )(h, ya, yb, w)


def _ple_fwd(h, p, w_pp, w_pg, g_post, g_gate):
    t = h.shape[0]
    tm = _row_tile(t, 256)

    def body(h_ref, p_ref, wpp_ref, wpg_ref, gp_ref, gg_ref, o_ref):
        x = h_ref[...]
        ph, _ = _rms(_dot(_bf(p_ref[...]), wpp_ref[...]))
        xh, _ = _rms(x)
        gate = _sigmoid(_dot(_bf(xh * gg_ref[...]), wpg_ref[...]))
        o_ref[...] = x + gate * (ph * gp_ref[...])

    return pl.pallas_call(
        body, name="ple_fwd", grid=(t // tm,),
        in_specs=[pl.BlockSpec((tm, D_MODEL), lambda i: (i, 0)),
                  pl.BlockSpec((tm, D_PLE), lambda i: (i, 0)),
                  pl.BlockSpec((D_PLE, D_MODEL), lambda i: (0, 0)),
                  pl.BlockSpec((D_MODEL, D_MODEL), lambda i: (0, 0)),
                  pl.BlockSpec((1, D_MODEL), lambda i: (0, 0)),
                  pl.BlockSpec((1, D_MODEL), lambda i: (0, 0))],
        out_specs=pl.BlockSpec((tm, D_MODEL), lambda i: (i, 0)),
        out_shape=_sds((t, D_MODEL)), compiler_params=_params(("arbitrary",)),
    )(h, p, w_pp, w_pg, g_post, g_gate)


def _ple_bwd(dh2, h, p, w_pp, w_pg, g_post, g_gate):
    t = h.shape[0]
    tm = _row_tile(t, 256)

    def body(d_ref, h_ref, p_ref, wpp_ref, wpg_ref, gp_ref, gg_ref, dh_ref, dwpg_ref, dwpp_ref, dgg_ref, dgp_ref):
        @pl.when(pl.program_id(0) == 0)
        def _():
            dwpg_ref[...] = jnp.zeros_like(dwpg_ref)
            dwpp_ref[...] = jnp.zeros_like(dwpp_ref)
            dgg_ref[...] = jnp.zeros_like(dgg_ref)
            dgp_ref[...] = jnp.zeros_like(dgp_ref)

        d = d_ref[...]
        x = h_ref[...]
        gp = gp_ref[...]
        gg = gg_ref[...]
        pb = _bf(p_ref[...])
        ph, rp = _rms(_dot(pb, wpp_ref[...]))
        pe = ph * gp
        xh, rx = _rms(x)
        un = _bf(xh * gg)
        gate = _sigmoid(_dot(un, wpg_ref[...]))
        dgpre = _bf(d * pe * gate * (1.0 - gate))
        dun = _dot_nt(dgpre, wpg_ref[...])
        dh_ref[...] = d + _rms_bwd(dun * gg, xh, rx)
        dgg_ref[...] += jnp.sum(dun * xh, axis=0, keepdims=True)
        dwpg_ref[...] += _dot_tn(un, dgpre)
        dpe = d * gate
        dgp_ref[...] += jnp.sum(dpe * ph, axis=0, keepdims=True)
        dwpp_ref[...] += _dot_tn(pb, _bf(_rms_bwd(dpe * gp, ph, rp)))

    return pl.pallas_call(
        body, name="ple_bwd", grid=(t // tm,),
        in_specs=[pl.BlockSpec((tm, D_MODEL), lambda i: (i, 0)),
                  pl.BlockSpec((tm, D_MODEL), lambda i: (i, 0)),
                  pl.BlockSpec((tm, D_PLE), lambda i: (i, 0)),
                  pl.BlockSpec((D_PLE, D_MODEL), lambda i: (0, 0)),
                  pl.BlockSpec((D_MODEL, D_MODEL), lambda i: (0, 0)),
                  pl.BlockSpec((1, D_MODEL), lambda i: (0, 0)),
                  pl.BlockSpec((1, D_MODEL), lambda i: (0, 0))],
        out_specs=[pl.BlockSpec((tm, D_MODEL), lambda i: (i, 0)),
                   pl.BlockSpec((D_MODEL, D_MODEL), lambda i: (0, 0)),
                   pl.BlockSpec((D_PLE, D_MODEL), lambda i: (0, 0)),
                   pl.BlockSpec((1, D_MODEL), lambda i: (0, 0)),
                   pl.BlockSpec((1, D_MODEL), lambda i: (0, 0))],
        out_shape=[_sds((t, D_MODEL)), _sds((D_MODEL, D_MODEL)), _sds((D_PLE, D_MODEL)),
                   _sds((1, D_MODEL)), _sds((1, D_MODEL))],
        compiler_params=_params(("arbitrary",)),
    )(dh2, h, p, w_pp, w_pg, g_post, g_gate)


def _outproj_bwd(dh, ya, yb, w):
    t = dh.shape[0]
    tm = _row_tile(t, 512)

    def body(d_ref, ya_ref, yb_ref, w_ref, dya_ref, dyb_ref, dw_ref):
        @pl.when(pl.program_id(0) == 0)
        def _():
            dw_ref[...] = jnp.zeros_like(dw_ref)

        d = _bf(d_ref[...])
        dya_ref[...] = _dot_nt(d, w_ref[pl.ds(0, GROUP), :])
        dyb_ref[...] = _dot_nt(d, w_ref[pl.ds(GROUP, GROUP), :])
        dw_ref[pl.ds(0, GROUP), :] += _dot_tn(_bf(ya_ref[...]), d)
        dw_ref[pl.ds(GROUP, GROUP), :] += _dot_tn(_bf(yb_ref[...]), d)

    return pl.pallas_call(
        body, name="outproj_bwd", grid=(t // tm,),
        in_specs=[pl.BlockSpec((tm, D_MODEL), lambda i: (i, 0)),
                  pl.BlockSpec((tm, GROUP), lambda i: (i, 0)),
                  pl.BlockSpec((tm, GROUP), lambda i: (i, 0)),
                  pl.BlockSpec((D_MODEL, D_MODEL), lambda i: (0, 0))],
        out_specs=[pl.BlockSpec((tm, GROUP), lambda i: (i, 0)),
                   pl.BlockSpec((tm, GROUP), lambda i: (i, 0)),
                   pl.BlockSpec((D_MODEL, D_MODEL), lambda i: (0, 0))],
        out_shape=[_sds((t, GROUP)), _sds((t, GROUP)), _sds((D_MODEL, D_MODEL))],
        compiler_params=_params(("arbitrary",)),
    )(dh, ya, yb, w)


def _inproj_bwd_dx(dres, h, g, w, da, db):
    t = h.shape[0]
    tm = _row_tile(t, 256)

    def body(dres_ref, h_ref, g_ref, w_ref, da_ref, db_ref, dh_ref, dg_ref):
        @pl.when(pl.program_id(0) == 0)
        def _():
            dg_ref[...] = jnp.zeros_like(dg_ref)

        du = jnp.zeros((tm, D_MODEL), F32)
        for i in range(8):
            part = da_ref[i] if i < 4 else db_ref[i - 4]
            du = du + _dot_nt(_bf(part), w_ref[:, pl.ds(i * GROUP, GROUP)])
        xh, r = _rms(h_ref[...])
        dg_ref[...] += jnp.sum(du * xh, axis=0, keepdims=True)
        dh_ref[...] = dres_ref[...] + _rms_bwd(du * g_ref[...], xh, r)

    return pl.pallas_call(
        body, name="inproj_bwd_dx", grid=(t // tm,),
        in_specs=[pl.BlockSpec((tm, D_MODEL), lambda i: (i, 0)),
                  pl.BlockSpec((tm, D_MODEL), lambda i: (i, 0)),
                  pl.BlockSpec((1, D_MODEL), lambda i: (0, 0)),
                  pl.BlockSpec((D_MODEL, D_IN), lambda i: (0, 0)),
                  pl.BlockSpec((4, tm, GROUP), lambda i: (0, i, 0)),
                  pl.BlockSpec((4, tm, GROUP), lambda i: (0, i, 0))],
        out_specs=[pl.BlockSpec((tm, D_MODEL), lambda i: (i, 0)),
                   pl.BlockSpec((1, D_MODEL), lambda i: (0, 0))],
        out_shape=[_sds((t, D_MODEL)), _sds((1, D_MODEL))],
        compiler_params=_params(("arbitrary",)),
    )(dres, h, g, w, da, db)


def _inproj_bwd_dw(h, g, stacked):
    t = h.shape[0]
    tm = _row_tile(t, 512)

    def body(h_ref, g_ref, d_ref, dw_ref):
        @pl.when(pl.program_id(1) == 0)
        def _():
            dw_ref[...] = jnp.zeros_like(dw_ref)

        xh, _ = _rms(h_ref[...])
        dw_ref[...] += _dot_tn(_bf(xh * g_ref[...]), _bf(d_ref[...]))

    return pl.pallas_call(
        body, name="inproj_bwd_dw", grid=(4, t // tm),
        in_specs=[pl.BlockSpec((tm, D_MODEL), lambda j, i: (i, 0)),
                  pl.BlockSpec((1, D_MODEL), lambda j, i: (0, 0)),
                  pl.BlockSpec((None, tm, GROUP), lambda j, i: (j, i, 0))],
        out_specs=pl.BlockSpec((None, D_MODEL, GROUP), lambda j, i: (j, 0, 0)),
        out_shape=_sds((4, D_MODEL, GROUP)), compiler_params=_params(("arbitrary", "arbitrary")),
    )(h, g, stacked)


def _final(h, g, target):
    t = h.shape[0]
    tm = _row_tile(t, 512)

    def body(h_ref, g_ref, t_ref, dh_ref, dg_ref, loss_ref):
        @pl.when(pl.program_id(0) == 0)
        def _():
            dg_ref[...] = jnp.zeros_like(dg_ref)
            loss_ref[...] = jnp.zeros_like(loss_ref)

        xh, r = _rms(h_ref[...])
        gg = g_ref[...]
        err = xh * gg - t_ref[...]
        part = 0.5 * jnp.sum(jnp.mean(err * err, axis=-1, keepdims=True), axis=0, keepdims=True)
        loss_ref[...] += jnp.broadcast_to(part, loss_ref.shape)
        dy = err * (1.0 / D_MODEL)
        dg_ref[...] += jnp.sum(dy * xh, axis=0, keepdims=True)
        dh_ref[...] = _rms_bwd(dy * gg, xh, r)

    return pl.pallas_call(
        body, name="final", grid=(t // tm,),
        in_specs=[pl.BlockSpec((tm, D_MODEL), lambda i: (i, 0)),
                  pl.BlockSpec((1, D_MODEL), lambda i: (0, 0)),
                  pl.BlockSpec((tm, D_MODEL), lambda i: (i, 0))],
        out_specs=[pl.BlockSpec((tm, D_MODEL), lambda i: (i, 0)),
                   pl.BlockSpec((1, D_MODEL), lambda i: (0, 0)),
                   pl.BlockSpec((1, 128), lambda i: (0, 0))],
        out_shape=[_sds((t, D_MODEL)), _sds((1, D_MODEL)), _sds((1, 128))],
        compiler_params=_params(("arbitrary",)),
    )(h, g, target)


def _hgrn_consts():
    c, nl = HG_CHUNK, HG_LEVELS
    t = np.arange(c)
    tril = np.tril(np.ones((c, c), np.float32))
    masks = np.zeros((nl + 1, c, c), np.float32)
    masks[0] = np.eye(c, dtype=np.float32)
    dmat = np.zeros(((nl + 2) * c, c), np.float32)
    dmat[0:c] = tril
    for l in range(nl):
        m = c >> (l + 1)
        blk = t // (2 * m)
        r = blk * 2 * m + m - 1
        upper = (t % (2 * m)) >= m
        masks[l + 1] = ((blk[:, None] == blk[None, :]) & upper[:, None] & (~upper)[None, :]).astype(np.float32)
        dmat[(l + 1) * c:(l + 2) * c] = tril[t] - tril[r]
    dmat[(nl + 1) * c:] = np.triu(np.ones((c, c), np.float32), k=1)
    return jnp.asarray(masks), jnp.asarray(dmat, BF16)


def _hgrn_common(aq, af, lb, dmat_ref):
    c, nl = HG_CHUNK, HG_LEVELS
    sq = _sigmoid(aq)
    q = aq * sq
    sneg = _sigmoid(-af)
    kk = (1.0 - lb) * sneg
    logf = jnp.log1p(-kk)
    dm = dmat_ref[pl.ds(0, (nl + 1) * c), :]
    lhi, llo = _split(logf)
    x_all = _dot(dm, lhi) + _dot(dm, llo)
    b = x_all[0:c]
    b_last = jnp.sum(logf, axis=0, keepdims=True)
    return sq, q, sneg, kk, x_all, b, b_last


def _hgrn_level(x_all, l, q, kk):
    c = HG_CHUNK
    x = x_all[(l + 1) * c:(l + 2) * c]
    qf = jnp.exp(jnp.minimum(x, 0.0))
    kf = jnp.exp(-jnp.maximum(x, 0.0))
    return qf, kf, _bf(q * qf), _bf(kk * kf)


def _hgrn_scores(x_all, q, kk, mask_ref):
    p = mask_ref[0] * _dot_nt(_bf(q), _bf(kk))
    for l in range(HG_LEVELS):
        _, _, ql, kl = _hgrn_level(x_all, l, q, kk)
        p = p + mask_ref[l + 1] * _dot_nt(ql, kl)
    return p


def _hgrn_specs(n_chunks, rev):
    c = HG_CHUNK
    cidx = (lambda n: n_chunks - 1 - n) if rev else (lambda n: n)
    col = lambda g: pl.BlockSpec((c, A_D), lambda h, n: (cidx(n), g * A_HEADS + h))
    vec = pl.BlockSpec((1, A_D), lambda h, n: (0, h))
    mask = pl.BlockSpec((HG_LEVELS + 1, c, c), lambda h, n: (0, 0, 0))
    dmat = pl.BlockSpec(((HG_LEVELS + 2) * c, c), lambda h, n: (0, 0))
    return cidx, col, vec, mask, dmat


def _hgrn_fwd(proj, lb, gain):
    t = proj.shape[0]
    c = HG_CHUNK
    nch = t // c
    masks, dmat = _hgrn_consts()
    cidx, col, vec, mask_spec, dmat_spec = _hgrn_specs(nch, False)

    def body(aq_ref, af_ref, ai_ref, ag_ref, lb_ref, gain_ref, mask_ref, dmat_ref, y_ref, st_ref, s_scr):
        @pl.when(pl.program_id(1) == 0)
        def _():
            s_scr[...] = jnp.zeros_like(s_scr)

        v = ai_ref[...]
        ag = ag_ref[...]
        _, q, _, kk, x_all, b, b_last = _hgrn_common(aq_ref[...], af_ref[...], lb_ref[...], dmat_ref)
        p = _hgrn_scores(x_all, q, kk, mask_ref)
        s = s_scr[...]
        st_ref[...] = s
        vb = _bf(v)
        o = _dot(_bf(p), vb) + _dot_nt(_bf(q * jnp.exp(b)), _bf(s))
        s_scr[...] = s * jnp.exp(b_last) + _dot_tn(vb, _bf(kk * jnp.exp(b_last - b)))
        oh, _ = _rms(o)
        y_ref[...] = oh * gain_ref[...] * (ag * _sigmoid(ag))

    return pl.pallas_call(
        body, name="hgrn_fwd", grid=(A_HEADS, nch),
        in_specs=[col(0), col(1), col(2), col(3), vec, vec, mask_spec, dmat_spec],
        out_specs=[pl.BlockSpec((c, A_D), lambda h, n: (n, h)),
                   pl.BlockSpec((None, None, A_D, A_D), lambda h, n: (h, n, 0, 0))],
        out_shape=[_sds((t, GROUP)), _sds((A_HEADS, nch, A_D, A_D))],
        scratch_shapes=[pltpu.VMEM((A_D, A_D), F32)],
        compiler_params=_params(("arbitrary", "arbitrary")),
    )(proj, proj, proj, proj, lb, gain, masks, dmat)


def _hgrn_bwd(proj, lb, gain, states, dya):
    t = proj.shape[0]
    c, nl = HG_CHUNK, HG_LEVELS
    nch = t // c
    masks, dmat = _hgrn_consts()
    cidx, col, vec, mask_spec, dmat_spec = _hgrn_specs(nch, True)

    def body(aq_ref, af_ref, ai_ref, ag_ref, lb_ref, gain_ref, mask_ref, dmat_ref, st_ref, dy_ref,
             da_ref, dlb_ref, dgain_ref, ds_scr, z_scr):
        @pl.when(pl.program_id(1) == 0)
        def _():
            ds_scr[...] = jnp.zeros_like(ds_scr)
            dlb_ref[...] = jnp.zeros_like(dlb_ref)
            dgain_ref[...] = jnp.zeros_like(dgain_ref)

        aq = aq_ref[...]
        v = ai_ref[...]
        ag = ag_ref[...]
        lb = lb_ref[...]
        gain = gain_ref[...]
        dy = dy_ref[...]
        sq, q, sneg, kk, x_all, b, b_last = _hgrn_common(aq, af_ref[...], lb, dmat_ref)
        eb = jnp.exp(b)
        ebl = jnp.exp(b_last - b)
        ebl_row = jnp.exp(b_last)
        qe = _bf(q * eb)
        ke = _bf(kk * ebl)
        vb = _bf(v)
        s = st_ref[...]
        sb = _bf(s)
        ds = ds_scr[...]
        dsb = _bf(ds)

        pb = _bf(_hgrn_scores(x_all, q, kk, mask_ref))
        o = _dot(pb, vb) + _dot_nt(qe, sb)

        oh, r = _rms(o)
        sg_sig = _sigmoid(ag)
        sg = ag * sg_sig
        da_ref[3] = dy * oh * gain * _silu_grad(ag, sg_sig)
        dgain_ref[...] += jnp.sum(dy * oh * sg, axis=0, keepdims=True)
        do = _bf(_rms_bwd(dy * gain * sg, oh, r))

        dp = _dot_nt(do, vb)
        da_ref[2] = _dot_tn(pb, do) + _dot_nt(ke, dsb)
        dpd = jnp.sum(mask_ref[0] * dp, axis=1, keepdims=True)
        dq_s = eb * _dot(do, sb)
        dk_s = ebl * _dot(vb, dsb)
        z_scr[pl.ds(0, c), :] = q * dq_s
        z_scr[pl.ds((nl + 1) * c, c), :] = kk * dk_s
        dq = dq_s + dpd * kk
        dk = dk_s + dpd * q
        for l in range(nl):
            qf, kf, ql, kl = _hgrn_level(x_all, l, q, kk)
            dpl = _bf(mask_ref[l + 1] * dp)
            dq_l = qf * _dot(dpl, kl)
            dk_l = kf * _dot_tn(dpl, ql)
            z_scr[pl.ds((l + 1) * c, c), :] = q * dq_l - kk * dk_l
            dq = dq + dq_l
            dk = dk + dk_l

        zhi, zlo = _split(z_scr[...])
        dlogf = (_dot_tn(dmat_ref[...], zhi) + _dot_tn(dmat_ref[...], zlo)
                 + ebl_row * jnp.sum(ds * s, axis=0, keepdims=True))
        dkk = dk - dlogf / (1.0 - kk)
        da_ref[1] = dkk * (1.0 - lb) * (-(sneg * (1.0 - sneg)))
        dlb_ref[...] += jnp.sum(dkk * (-sneg), axis=0, keepdims=True)
        da_ref[0] = dq * _silu_grad(aq, sq)
        ds_scr[...] = ds * ebl_row + _dot_tn(do, qe)

    out_col = pl.BlockSpec((c, A_D), lambda h, n: (cidx(n), h))
    return pl.pallas_call(
        body, name="hgrn_bwd", grid=(A_HEADS, nch),
        in_specs=[col(0), col(1), col(2), col(3), vec, vec, mask_spec, dmat_spec,
                  pl.BlockSpec((None, None, A_D, A_D), lambda h, n: (h, cidx(n), 0, 0)),
                  out_col],
        out_specs=[pl.BlockSpec((4, c, A_D), lambda h, n: (0, cidx(n), h)), vec, vec],
        out_shape=[_sds((4, t, GROUP))] + [_sds((1, GROUP))] * 2,
        scratch_shapes=[pltpu.VMEM((A_D, A_D), F32), pltpu.VMEM(((nl + 2) * c, A_D), F32)],
        compiler_params=_params(("arbitrary", "arbitrary")),
    )(proj, proj, proj, proj, lb, gain, masks, dmat, states, dya)


def _sb_consts():
    j = np.arange(SB_TK)
    ones = np.ones((SB_TK, SB_TK), np.float32)
    strict = np.concatenate([(j[:, None] > j[None, :]).astype(np.float32), ones], axis=1)
    incl = np.concatenate([(j[:, None] >= j[None, :]).astype(np.float32), ones], axis=1)
    return jnp.asarray(strict, BF16), jnp.asarray(incl, BF16)


def _sb_softplus(z, masked):
    lg = jnp.log(1.0 + jnp.exp(-jnp.abs(z)))
    sp = jnp.maximum(z, 0.0) + lg
    logsig = jnp.minimum(z, 0.0) - lg
    mask = None
    if masked:
        mask = lax.broadcasted_iota(jnp.int32, z.shape, 1) < lax.broadcasted_iota(jnp.int32, z.shape, 0)
        sp = jnp.where(mask, sp, 0.0)
    return mask, sp, logsig


def _sb_sweep(qi, group_fn, state):
    nd = SB_TQ // SB_TK
    state = group_fn([(pl.multiple_of((qi * nd + d) * SB_TK, SB_TK), d * SB_TK, True) for d in reversed(range(nd))],
                     state)

    def step(j, st):
        return group_fn([(pl.multiple_of(((qi - j) * nd - 1 - g) * SB_TK, SB_TK), 0, False) for g in range(nd)], st)

    return lax.fori_loop(0, qi, step, state)


def _set_rows(r0, full, new):
    return new if r0 == 0 else jnp.concatenate([full[:r0], new], axis=0)


def _sb_specs(t, tq):
    col = lambda g: pl.BlockSpec((tq, 2 * B_D), lambda p, i, h: (i, g * (GROUP // (2 * B_D)) + p))
    full = lambda g: pl.BlockSpec((t, 2 * B_D), lambda p, i, h: (0, g * (GROUP // (2 * B_D)) + p))
    vec = pl.BlockSpec((1, 2 * B_D), lambda p, i, h: (0, p))
    mat = pl.BlockSpec((SB_TK, 2 * SB_TK), lambda p, i, h: (0, 0))
    return col, full, vec, mat


def _head_lanes(h):
    return (lax.broadcasted_iota(jnp.int32, (1, 2 * B_D), 1) >= B_D) == (h == 1)


def _put(ref, h, val):
    @pl.when(h == 0)
    def _():
        ref[...] = val

    @pl.when(h == 1)
    def _():
        ref[...] += val


def _sb_fwd(proj_bf, proj, gain):
    t = proj.shape[0]
    tq = SB_TQ
    strict, _ = _sb_consts()

    def body(q_ref, k_ref, v_ref, bg_ref, gain_ref, m_ref, o_ref, y_ref):
        h = pl.program_id(2)
        lanes = _head_lanes(h)
        qb = jnp.where(lanes, q_ref[...], jnp.zeros_like(q_ref))
        cmat = m_ref[...]

        def group(tiles, state):
            carry, acc = state
            kv = [(k_ref[pl.ds(off, SB_TK), :], v_ref[pl.ds(off, SB_TK), :]) for off, _, _ in tiles]
            zs = [_dot_nt(qb[r0:], kb) for (_, r0, _), (kb, _) in zip(tiles, kv)]
            sps = [_sb_softplus(z, masked) for z, (_, _, masked) in zip(zs, tiles)]
            cs2s = [_dot(_bf(sp), cmat) for _, sp, _ in sps]
            ws = []
            for (mask, _, logsig), cs2, (_, r0, masked) in zip(sps, cs2s, tiles):
                w = jnp.exp(logsig - cs2[:, :SB_TK] - carry[r0:])
                ws.append(_split(jnp.where(mask, w, 0.0) if masked else w))
                carry = _set_rows(r0, carry, carry[r0:] + cs2[:, SB_TK:])
            for (whi, wlo), (_, vb), (_, r0, _) in zip(ws, kv, tiles):
                acc = _set_rows(r0, acc, acc[r0:] + _dot(whi, vb) + _dot(wlo, vb))
            return carry, acc

        _, acc = _sb_sweep(pl.program_id(1), group, (jnp.zeros((tq, SB_TK), F32), jnp.zeros((tq, 2 * B_D), F32)))
        o = jnp.where(lanes, acc, 0.0)
        oh = o * lax.rsqrt(jnp.sum(o * o, axis=-1, keepdims=True) * (1.0 / B_D) + EPS)
        bg = bg_ref[...]
        _put(o_ref, h, o)
        _put(y_ref, h, oh * gain_ref[...] * (bg * _sigmoid(bg)))

    col, full, vec, mat = _sb_specs(t, tq)
    out = pl.BlockSpec((tq, 2 * B_D), lambda p, i, h: (i, p))
    return pl.pallas_call(
        body, name="sb_fwd", grid=(B_HEADS // 2, t // tq, 2),
        in_specs=[col(4), full(5), full(6), col(7), vec, mat],
        out_specs=[out, out],
        out_shape=[_sds((t, GROUP)), _sds((t, GROUP))],
        compiler_params=_params(("arbitrary", "arbitrary", "arbitrary")),
    )(proj_bf, proj_bf, proj_bf, proj, gain, strict)


def _sb_bwd(proj_bf, proj, o, dy, gain):
    t = proj.shape[0]
    tq = SB_TQ
    strict, incl = _sb_consts()

    def body(q_ref, k_ref, v_ref, bg_ref, o_ref, dy_ref, gain_ref, ms_ref, mi_ref,
             dq_ref, dk_ref, dv_ref, dbg_ref, dgain_ref):
        qi = pl.program_id(1)
        h = pl.program_id(2)
        lanes = _head_lanes(h)

        @pl.when((qi == 0) & (h == 0))
        def _():
            dk_ref[...] = jnp.zeros_like(dk_ref)
            dv_ref[...] = jnp.zeros_like(dv_ref)
            dgain_ref[...] = jnp.zeros_like(dgain_ref)

        qb = jnp.where(lanes, q_ref[...], jnp.zeros_like(q_ref))
        cmat = ms_ref[...]
        imat = mi_ref[...]
        o = jnp.where(lanes, o_ref[...], 0.0)
        dy = jnp.where(lanes, dy_ref[...], 0.0)
        bg = bg_ref[...]
        gain = gain_ref[...]
        r = lax.rsqrt(jnp.sum(o * o, axis=-1, keepdims=True) * (1.0 / B_D) + EPS)
        oh = o * r
        sig = _sigmoid(bg)
        sg = bg * sig
        _put(dbg_ref, h, dy * oh * gain * _silu_grad(bg, sig))
        dgain_ref[...] += jnp.sum(dy * oh * sg, axis=0, keepdims=True)
        doh = dy * gain * sg
        do = _bf(r * (doh - oh * (jnp.sum(doh * oh, axis=-1, keepdims=True) * (1.0 / B_D))))
        total = jnp.broadcast_to(jnp.sum(do.astype(F32) * o, axis=1, keepdims=True), (tq, SB_TK))

        def group(tiles, state):
            carry, gcarry, dq = state
            kv = [(k_ref[pl.ds(off, SB_TK), :], v_ref[pl.ds(off, SB_TK), :]) for off, _, _ in tiles]
            zs = [_dot_nt(qb[r0:], kb) for (_, r0, _), (kb, _) in zip(tiles, kv)]
            dws = [_dot_nt(do[r0:], vb) for (_, r0, _), (_, vb) in zip(tiles, kv)]
            sps = [_sb_softplus(z, masked) for z, (_, _, masked) in zip(zs, tiles)]
            cs2s = [_dot(_bf(sp), cmat) for _, sp, _ in sps]
            ws, gs = [], []
            for (mask, _, logsig), cs2, dw, (_, r0, masked) in zip(sps, cs2s, dws, tiles):
                w = jnp.exp(logsig - cs2[:, :SB_TK] - carry[r0:])
                w = jnp.where(mask, w, 0.0) if masked else w
                ws.append(_bf(w))
                gs.append(dw * w)
                carry = _set_rows(r0, carry, carry[r0:] + cs2[:, SB_TK:])
            s2s = []
            for g in gs:
                ghi, glo = _split(g)
                s2s.append(_dot(ghi, imat) + _dot(glo, imat))
            dzs = []
            for (mask, _, logsig), g, s2, (_, r0, masked) in zip(sps, gs, s2s, tiles):
                before = total[r0:] - gcarry[r0:] - s2[:, :SB_TK]
                sig_z = jnp.exp(logsig)
                dz = g * (1.0 - sig_z) - sig_z * before
                dzs.append(_bf(jnp.where(mask, dz, 0.0) if masked else dz))
                gcarry = _set_rows(r0, gcarry, gcarry[r0:] + s2[:, SB_TK:])
            for dz, wb, (kb, _), (off, r0, _) in zip(dzs, ws, kv, tiles):
                dq = _set_rows(r0, dq, dq[r0:] + _dot(dz, kb))
                dk_ref[pl.ds(off, SB_TK), :] += _dot_tn(dz, qb[r0:])
                dv_ref[pl.ds(off, SB_TK), :] += _dot_tn(wb, do[r0:])
            return carry, gcarry, dq

        zero = jnp.zeros((tq, SB_TK), F32)
        _, _, dq = _sb_sweep(qi, group, (zero, zero, jnp.zeros((tq, 2 * B_D), F32)))
        _put(dq_ref, h, jnp.where(lanes, dq * (B_D ** -0.5), 0.0))

    col, full, vec, mat = _sb_specs(t, tq)
    blk = pl.BlockSpec((tq, 2 * B_D), lambda p, i, h: (i, p))
    whole = pl.BlockSpec((t, 2 * B_D), lambda p, i, h: (0, p))
    return pl.pallas_call(
        body, name="sb_bwd", grid=(B_HEADS // 2, t // tq, 2),
        in_specs=[col(4), full(5), full(6), col(7), blk, blk, vec, mat, mat],
        out_specs=[blk, whole, whole, blk, vec],
        out_shape=[_sds((t, GROUP))] * 4 + [_sds((1, GROUP))],
        compiler_params=_params(("arbitrary", "arbitrary", "arbitrary")),
    )(proj_bf, proj_bf, proj_bf, proj, o, dy, gain, strict, incl)


def _adamw(w, g, m, v):
    rows, cols = w.shape
    tr = rows
    for cand in (400, 256, 128, 64, 32, 16, 8):
        if rows % cand == 0:
            tr = cand
            break

    def body(w_ref, g_ref, m_ref, v_ref, d_ref, nm_ref, nv_ref):
        g_ = g_ref[...]
        m_ = ADAM_B1 * m_ref[...] + (1.0 - ADAM_B1) * g_
        v_ = ADAM_B2 * v_ref[...] + (1.0 - ADAM_B2) * (g_ * g_)
        m_hat = m_ / (1.0 - ADAM_B1 ** ADAM_STEP)
        v_hat = v_ / (1.0 - ADAM_B2 ** ADAM_STEP)
        d_ref[...] = -ADAM_LR * (m_hat / (jnp.sqrt(v_hat) + ADAM_EPS) + ADAM_WD * w_ref[...])
        nm_ref[...] = m_
        nv_ref[...] = v_

    spec = pl.BlockSpec((tr, cols), lambda i: (i, 0))
    return pl.pallas_call(
        body, name="adamw", grid=(rows // tr,), in_specs=[spec] * 4, out_specs=[spec] * 3,
        out_shape=[_sds((rows, cols))] * 3, compiler_params=_params(("arbitrary",)),
    )(w, g, m, v)


_ANY = pl.BlockSpec(memory_space=pl.ANY)


def _place():
    return lax.axis_index("x"), lax.axis_index("y"), lax.axis_index("c")


def _gather_weights(flat):
    half, ch, nc = HALF_FLAT, CHUNK_ROWS, N_CHUNK

    def body(x_ref, out_ref, send_sems, recv_sems):
        x, y, c = _place()
        me = 2 * x + y
        sibling = (x, y, 1 - c)

        def rows(shard, hc, r):
            return out_ref.at[shard, pl.ds(hc * half + r * ch, ch), :]

        def copy(k, shard, hc, r, to, src=None):
            return pltpu.make_async_remote_copy(
                src_ref=rows(shard, hc, r) if src is None else src, dst_ref=rows(shard, hc, r),
                send_sem=send_sems.at[k * nc + r], recv_sem=recv_sems.at[k * nc + r], device_id=to, device_id_type=MESH)

        peers = [me ^ k for k in (1, 2, 3)]
        first = [copy(k, me, c, r, (p >> 1, p & 1, c), src=x_ref.at[pl.ds(c * half + r * ch, ch), :])
                 for k, p in enumerate(peers) for r in range(nc)]
        for cp in first:
            cp.start()
        passed = []
        for k, p in enumerate(peers):
            for r in range(nc):
                copy(k, p, c, r, sibling).wait_recv()
                passed.append(copy(3 + k, p, c, r, sibling))
                passed[-1].start()
        for k, p in enumerate(peers):
            for r in range(nc):
                copy(3 + k, p, 1 - c, r, sibling).wait_recv()
        for cp in first + passed:
            cp.wait_send()

    return pl.pallas_call(
        body, name="gather_weights", in_specs=[_ANY], out_specs=_ANY,
        out_shape=_sds((N_SHARD, ROWS_FLAT, D_MODEL), BF16),
        scratch_shapes=[pltpu.SemaphoreType.DMA((6 * nc,)), pltpu.SemaphoreType.DMA((6 * nc,))],
    )(flat)


def _swap_halves(grads):
    half, ch, nc = HALF_FLAT, CHUNK_ROWS, N_CHUNK

    def body(g_ref, out_ref, send_sems, recv_sems):
        x, y, c = _place()
        copies = [pltpu.make_async_remote_copy(
            src_ref=g_ref.at[j, pl.ds((1 - c) * half + r * ch, ch), :], dst_ref=out_ref.at[j, pl.ds(r * ch, ch), :],
            send_sem=send_sems.at[j * nc + r], recv_sem=recv_sems.at[j * nc + r],
            device_id=(x, y, 1 - c), device_id_type=MESH) for j in range(N_SHARD) for r in range(nc)]
        for cp in copies:
            cp.start()
        for cp in copies:
            cp.wait()

    return pl.pallas_call(
        body, name="swap_halves", in_specs=[_ANY], out_specs=_ANY,
        out_shape=_sds((N_SHARD, half, D_MODEL)),
        scratch_shapes=[pltpu.SemaphoreType.DMA((N_SHARD * nc,)), pltpu.SemaphoreType.DMA((N_SHARD * nc,))],
    )(grads)


def _add_my_half(grads, recv):
    tr = 400
    nb = HALF_FLAT // tr
    core = lax.axis_index("c").astype(jnp.int32).reshape(1)

    def body(c_ref, g_ref, r_ref, o_ref, ob_ref):
        acc = g_ref[...] + r_ref[...]
        o_ref[...] = acc
        ob_ref[...] = _bf(acc)

    out = pl.BlockSpec((None, tr, D_MODEL), lambda j, i, c_ref: (j, i, 0))
    return pl.pallas_call(
        body, name="add_my_half",
        grid_spec=pltpu.PrefetchScalarGridSpec(
            num_scalar_prefetch=1, grid=(N_SHARD, nb),
            in_specs=[pl.BlockSpec((None, tr, D_MODEL), lambda j, i, c_ref: (j, c_ref[0] * nb + i, 0)), out],
            out_specs=[out, out]),
        out_shape=[_sds((N_SHARD, HALF_FLAT, D_MODEL)), _sds((N_SHARD, HALF_FLAT, D_MODEL), BF16)],
        compiler_params=_params(("arbitrary", "arbitrary")),
    )(core, grads, recv)


def _scatter_shards(part):
    ch, nc = CHUNK_ROWS, N_CHUNK

    def body(p_ref, out_ref, send_sems, recv_sems):
        x, y, c = _place()
        me = 2 * x + y
        peers = [me ^ k for k in (1, 2, 3)]
        sends = [pltpu.make_async_remote_copy(
            src_ref=p_ref.at[p, pl.ds(r * ch, ch), :], dst_ref=out_ref.at[k, pl.ds(r * ch, ch), :],
            send_sem=send_sems.at[k * nc + r], recv_sem=recv_sems.at[k * nc + r],
            device_id=(p >> 1, p & 1, c), device_id_type=MESH) for k, p in enumerate(peers) for r in range(nc)]
        for cp in sends:
            cp.start()
        for cp in sends:
            cp.wait()

    return pl.pallas_call(
        body, name="scatter_shards", in_specs=[_ANY], out_specs=_ANY,
        out_shape=_sds((3, HALF_FLAT, D_MODEL), BF16),
        scratch_shapes=[pltpu.SemaphoreType.DMA((3 * nc,)), pltpu.SemaphoreType.DMA((3 * nc,))],
    )(part)


def _sum_scattered(part, recv):
    tr = 400
    chip = (2 * lax.axis_index("x") + lax.axis_index("y")).astype(jnp.int32).reshape(1)

    def body(c_ref, p_ref, r_ref, o_ref):
        acc = p_ref[...]
        for k in range(3):
            acc = acc + r_ref[k].astype(F32)
        o_ref[...] = acc

    return pl.pallas_call(
        body, name="sum_scattered",
        grid_spec=pltpu.PrefetchScalarGridSpec(
            num_scalar_prefetch=1, grid=(HALF_FLAT // tr,),
            in_specs=[pl.BlockSpec((None, tr, D_MODEL), lambda i, c_ref: (c_ref[0], i, 0)),
                      pl.BlockSpec((3, tr, D_MODEL), lambda i, c_ref: (0, i, 0))],
            out_specs=pl.BlockSpec((tr, D_MODEL), lambda i, c_ref: (i, 0))),
        out_shape=_sds((HALF_FLAT, D_MODEL)), compiler_params=_params(("arbitrary",)),
    )(chip, part, recv)


def _swap_reduced(mine_half):
    ch, nc = CHUNK_ROWS, N_CHUNK

    def body(r_ref, out_ref, send_sems, recv_sems):
        x, y, c = _place()
        copies = [pltpu.make_async_remote_copy(
            src_ref=r_ref.at[pl.ds(r * ch, ch), :], dst_ref=out_ref.at[pl.ds(r * ch, ch), :],
            send_sem=send_sems.at[r], recv_sem=recv_sems.at[r], device_id=(x, y, 1 - c), device_id_type=MESH)
            for r in range(nc)]
        for cp in copies:
            cp.start()
        for cp in copies:
            cp.wait()

    return pl.pallas_call(
        body, name="swap_reduced", in_specs=[_ANY], out_specs=_ANY,
        out_shape=_sds((HALF_FLAT, D_MODEL)),
        scratch_shapes=[pltpu.SemaphoreType.DMA((nc,)), pltpu.SemaphoreType.DMA((nc,))],
    )(mine_half)


def _allreduce_small(vec):
    def body(v_ref, out_ref, buf, send_sems, recv_sems):
        x, y, c = _place()
        me = 4 * x + 2 * y + c
        buf[me] = v_ref[...]
        peers = [me ^ k for k in range(1, N_DEV)]
        sends = [pltpu.make_async_remote_copy(
            src_ref=v_ref, dst_ref=buf.at[me], send_sem=send_sems.at[k], recv_sem=recv_sems.at[k],
            device_id=(p >> 2, (p >> 1) & 1, p & 1), device_id_type=MESH) for k, p in enumerate(peers)]
        for cp in sends:
            cp.start()
        for k, p in enumerate(peers):
            pltpu.make_async_remote_copy(
                src_ref=v_ref, dst_ref=buf.at[p], send_sem=send_sems.at[k], recv_sem=recv_sems.at[k],
                device_id=(p >> 2, (p >> 1) & 1, p & 1), device_id_type=MESH).wait_recv()
        for cp in sends:
            cp.wait_send()
        acc = buf[0]
        for d in range(1, N_DEV):
            acc = acc + buf[d]
        out_ref[...] = acc

    vm = pl.BlockSpec(memory_space=pltpu.VMEM)
    return pl.pallas_call(
        body, name="allreduce_small", in_specs=[vm], out_specs=vm, out_shape=_sds((SMALL_ROWS, 128)),
        scratch_shapes=[pltpu.VMEM((N_DEV, SMALL_ROWS, 128), F32),
                        pltpu.SemaphoreType.DMA((N_DEV - 1,)), pltpu.SemaphoreType.DMA((N_DEV - 1,))],
    )(vec)


def _flatten_shard(w_in, w_out, w_pg, w_pp):
    return jnp.concatenate([w_in.reshape(-1, D_MODEL), w_out.reshape(-1, D_MODEL), w_pg.reshape(-1, D_MODEL),
                            w_pp.reshape(-1, D_MODEL)], axis=0)


def _unflatten_shard(flat):
    a, b, c = ROWS_W_IN, ROWS_W_IN + ROWS_W_OUT, ROWS_W_IN + ROWS_W_OUT + ROWS_W_PG
    q = D_MODEL // N_SHARD
    return (flat[:a].reshape(2, D_MODEL, D_MODEL), flat[a:b].reshape(2, q, D_MODEL),
            flat[b:c].reshape(2, q, D_MODEL), flat[c:].reshape(2, D_PLE, q))


def _full_weights(gathered):
    a, b, c = ROWS_W_IN, ROWS_W_IN + ROWS_W_OUT, ROWS_W_IN + ROWS_W_OUT + ROWS_W_PG
    q = D_MODEL // N_SHARD
    rpp = ROWS_W_PP // 2
    out = []
    for l in range(2):
        w_in = gathered[:, l * D_MODEL:(l + 1) * D_MODEL, :].transpose(1, 0, 2).reshape(D_MODEL, D_IN)
        w_out = gathered[:, a + l * q:a + (l + 1) * q, :].reshape(D_MODEL, D_MODEL)
        w_pg = gathered[:, b + l * q:b + (l + 1) * q, :].reshape(D_MODEL, D_MODEL)
        w_pp = gathered[:, c + l * rpp:c + (l + 1) * rpp, :].reshape(N_SHARD, D_PLE, q).transpose(1, 0, 2)
        out.append((w_in, w_out, w_pg, w_pp.reshape(D_PLE, D_MODEL)))
    return out


def _shard_major(dw_in, dw_out, dw_pg, dw_pp):
    q = D_MODEL // N_SHARD
    rpp = ROWS_W_PP // 2
    parts = []
    for l in range(2):
        g = dw_in[l].reshape(N_SHARD, 2, D_MODEL, GROUP).transpose(0, 2, 1, 3).reshape(N_SHARD, D_MODEL, D_MODEL)
        parts.append(g)
    for l in range(2):
        parts.append(dw_out[l].reshape(N_SHARD, q, D_MODEL))
    for l in range(2):
        parts.append(dw_pg[l].reshape(N_SHARD, q, D_MODEL))
    for l in range(2):
        parts.append(dw_pp[l].reshape(D_PLE, N_SHARD, q).transpose(1, 0, 2).reshape(N_SHARD, rpp, D_MODEL))
    return jnp.concatenate(parts, axis=1)


def _lower_bounds(lb_logits):
    sm = jax.nn.softmax(lb_logits.astype(F32), axis=0)
    return jnp.cumsum(sm, axis=0) - sm[0:1]


def kernel(x, p, norm_mix, w_in, a_out_norm, b_out_norm, w_out, lb_logits, ple_gate_norm, w_ple_gate, w_ple_proj, ple_post_norm, final_norm, loss_target, m_norm_mix, m_w_in, m_a_out_norm, m_b_out_norm, m_w_out, m_lb_logits, m_ple_gate_norm, m_w_ple_gate, m_w_ple_proj, m_ple_post_norm, m_final_norm, v_norm_mix, v_w_in, v_a_out_norm, v_b_out_norm, v_w_out, v_lb_logits, v_ple_gate_norm, v_w_ple_gate, v_w_ple_proj, v_ple_post_norm, v_final_norm):
    t = x.shape[1]
    h0 = x.reshape(t, D_MODEL)
    target = loss_target.reshape(t, D_MODEL)
    pl_in = p.reshape(2, t, D_PLE)

    w_flat = _flatten_shard(w_in, w_out, w_ple_gate, w_ple_proj)
    w_flat_bf = w_flat.astype(BF16)
    chip = 2 * lax.axis_index("x") + lax.axis_index("y")
    weights = _full_weights(lax.dynamic_update_slice(_gather_weights(w_flat_bf), w_flat_bf[None], (chip, 0, 0)))
    lbs, lbs_vjp = jax.vjp(_lower_bounds, lb_logits)

    saved = []
    h = h0
    for l in range(2):
        wi, wo, wpg, wpp = weights[l]
        g_mix = norm_mix[l].reshape(1, D_MODEL)
        lb = lbs[l].reshape(1, GROUP)
        ga = a_out_norm[l].reshape(1, GROUP)
        gb = b_out_norm[l].reshape(1, GROUP)
        proj, proj_bf = _inproj(h, g_mix, wi)
        ya, states = _hgrn_fwd(proj, lb, ga)
        ob, yb = _sb_fwd(proj_bf, proj, gb)
        h1 = _outproj(h, ya, yb, wo)
        g_post = ple_post_norm[l].reshape(1, D_MODEL)
        g_gate = ple_gate_norm[l].reshape(1, D_MODEL)
        h2 = _ple_fwd(h1, pl_in[l], wpp, wpg, g_post, g_gate)
        saved.append((h, proj, proj_bf, states, ya, yb, ob, h1))
        h = h2

    dh, d_final, loss_part = _final(h, final_norm.reshape(1, D_MODEL), target)

    dw_in_l, dw_out_l, dw_pg_l, dw_pp_l = [None] * 2, [None] * 2, [None] * 2, [None] * 2
    d_mix, d_a, d_b, d_lb, d_gate, d_post = [None] * 2, [None] * 2, [None] * 2, [None] * 2, [None] * 2, [None] * 2
    for l in (1, 0):
        wi, wo, wpg, wpp = weights[l]
        h_in, proj, proj_bf, states, ya, yb, ob, h1 = saved[l]
        g_mix = norm_mix[l].reshape(1, D_MODEL)
        lb = lbs[l].reshape(1, GROUP)
        ga = a_out_norm[l].reshape(1, GROUP)
        gb = b_out_norm[l].reshape(1, GROUP)
        g_post = ple_post_norm[l].reshape(1, D_MODEL)
        g_gate = ple_gate_norm[l].reshape(1, D_MODEL)
        dh1, dw_pg_l[l], dw_pp_l[l], d_gate[l], d_post[l] = _ple_bwd(dh, h1, pl_in[l], wpp, wpg, g_post, g_gate)
        dya, dyb, dw_out_l[l] = _outproj_bwd(dh1, ya, yb, wo)
        dbq, dbk, dbv, dbg, d_b[l] = _sb_bwd(proj_bf, proj, ob, dyb, gb)
        da, d_lb[l], d_a[l] = _hgrn_bwd(proj, lb, ga, states, dya)
        db = jnp.stack([dbq, dbk, dbv, dbg])
        dw_in_l[l] = jnp.concatenate([_inproj_bwd_dw(h_in, g_mix, da), _inproj_bwd_dw(h_in, g_mix, db)])
        dh, d_mix[l] = _inproj_bwd_dx(dh1, h_in, g_mix, wi, da, db)
    grad_x = dh.reshape(x.shape)

    g_full = _shard_major(dw_in_l, dw_out_l, dw_pg_l, dw_pp_l)
    chip_sum, chip_sum_bf = _add_my_half(g_full, _swap_halves(g_full))
    mine_half = _sum_scattered(chip_sum, _scatter_shards(chip_sum_bf))
    other_half = _swap_reduced(mine_half)
    south = lax.axis_index("c") == 0
    g_flat = jnp.concatenate([jnp.where(south, mine_half, other_half), jnp.where(south, other_half, mine_half)])
    g_w_in, g_w_out, g_w_pg, g_w_pp = _unflatten_shard(g_flat)

    small = jnp.concatenate([
        jnp.concatenate(d_mix).reshape(-1, 128), jnp.concatenate(d_a).reshape(-1, 128),
        jnp.concatenate(d_b).reshape(-1, 128), jnp.concatenate(d_lb).reshape(-1, 128),
        jnp.concatenate(d_gate).reshape(-1, 128), jnp.concatenate(d_post).reshape(-1, 128),
        d_final.reshape(-1, 128), jnp.broadcast_to(loss_part, (8, 128))], axis=0)
    small = _allreduce_small(small)
    loss = small[80, 0]
    g_norm_mix = small[0:16].reshape(2, D_MODEL)
    g_a = small[16:24].reshape(2, GROUP)
    g_b = small[24:32].reshape(2, GROUP)
    (g_lb,) = lbs_vjp(small[32:40].reshape(2, GROUP))
    g_gate = small[40:56].reshape(2, D_MODEL)
    g_post = small[56:72].reshape(2, D_MODEL)
    g_final = small[72:80].reshape(D_MODEL)

    d_flat, nm_flat, nv_flat = _adamw(w_flat, g_flat, _flatten_shard(m_w_in, m_w_out, m_w_ple_gate, m_w_ple_proj),
                                      _flatten_shard(v_w_in, v_w_out, v_w_ple_gate, v_w_ple_proj))
    d_w_in, d_w_out, d_w_pg, d_w_pp = _unflatten_shard(d_flat)
    nm_w_in, nm_w_out, nm_w_pg, nm_w_pp = _unflatten_shard(nm_flat)
    nv_w_in, nv_w_out, nv_w_pg, nv_w_pp = _unflatten_shard(nv_flat)

    small_w = [norm_mix, a_out_norm, b_out_norm, lb_logits, ple_gate_norm, ple_post_norm, final_norm]
    small_g = [g_norm_mix, g_a, g_b, g_lb, g_gate, g_post, g_final]
    small_m = [m_norm_mix, m_a_out_norm, m_b_out_norm, m_lb_logits, m_ple_gate_norm, m_ple_post_norm, m_final_norm]
    small_v = [v_norm_mix, v_a_out_norm, v_b_out_norm, v_lb_logits, v_ple_gate_norm, v_ple_post_norm, v_final_norm]
    pack = lambda arrs: jnp.concatenate([a.reshape(-1, 128) for a in arrs], axis=0)
    ds, nms, nvs = _adamw(pack(small_w), pack(small_g), pack(small_m), pack(small_v))

    def unpack(packed):
        out, r = [], 0
        for a in small_w:
            n = a.size // 128
            out.append(packed[r:r + n].reshape(a.shape))
            r += n
        return out

    d_s, nm_s, nv_s = unpack(ds), unpack(nms), unpack(nvs)

    def ordered(s, big):
        return [s[0], big[0], s[1], s[2], big[1], s[3], s[4], big[2], big[3], s[5], s[6]]

    grads = ordered(small_g, [g_w_in, g_w_out, g_w_pg, g_w_pp])
    deltas = ordered(d_s, [d_w_in, d_w_out, d_w_pg, d_w_pp])
    new_m = ordered(nm_s, [nm_w_in, nm_w_out, nm_w_pg, nm_w_pp])
    new_v = ordered(nv_s, [nv_w_in, nv_w_out, nv_w_pg, nv_w_pp])
    return (loss, grad_x, *grads, *deltas, *new_m, *new_v)
```

```python
import functools
import math

import numpy as np
import jax
import jax.numpy as jnp
from jax import lax
from jax.experimental import pallas as pl
from jax.experimental.pallas import tpu as pltpu

F32 = jnp.float32
BF16 = jnp.bfloat16
MESH = pl.DeviceIdType.MESH

D_MODEL = 1024
D_PLE = 256
D_IN = 4096
A_HEADS, A_D = 4, 128
B_HEADS, B_D = 8, 64
GROUP = 512
EPS = 1e-6
N_SHARD = 4
N_DEV = 8

HG_CHUNK = 128
HG_LEVELS = 7
SB_TQ = 512
SB_TK = 128

ADAM_LR, ADAM_B1, ADAM_B2, ADAM_EPS, ADAM_WD, ADAM_STEP = 0.001, 0.9, 0.999, 1e-08, 0.01, 10

VMEM_LIMIT = 48 * 1024 * 1024
VMEM_LIMIT_BIG = 58 * 1024 * 1024

ROWS_W_IN = 2 * D_MODEL
ROWS_W_OUT = 2 * (D_MODEL // N_SHARD)
ROWS_W_PG = 2 * (D_MODEL // N_SHARD)
ROWS_W_PP = 2 * (D_PLE * (D_MODEL // N_SHARD) // D_MODEL)
ROWS_FLAT = ROWS_W_IN + ROWS_W_OUT + ROWS_W_PG + ROWS_W_PP
HALF_FLAT = ROWS_FLAT // 2
N_CHUNK = 10
CHUNK_ROWS = HALF_FLAT // N_CHUNK

SMALL_ROWS = 88


def _sds(shape, dtype=F32):
    return jax.ShapeDtypeStruct(shape, dtype)


def _params(sem=None, vmem_limit=VMEM_LIMIT):
    kw = dict(vmem_limit_bytes=vmem_limit)
    if sem is not None:
        kw["dimension_semantics"] = sem
    return pltpu.CompilerParams(**kw)


def _dot(a, b, precision=None):
    return lax.dot_general(a, b, (((1,), (0,)), ((), ())), preferred_element_type=F32, precision=precision)


def _dot_nt(a, b, precision=None):
    return lax.dot_general(a, b, (((1,), (1,)), ((), ())), preferred_element_type=F32, precision=precision)


def _dot_tn(a, b, precision=None):
    return lax.dot_general(a, b, (((0,), (0,)), ((), ())), preferred_element_type=F32, precision=precision)


def _bf(x):
    return x.astype(BF16)


def _split(x):
    hi = x.astype(BF16)
    lo = (x - hi.astype(F32)).astype(BF16)
    return hi, lo


def _rms(x):
    r = lax.rsqrt(jnp.mean(x * x, axis=-1, keepdims=True) + EPS)
    return x * r, r


def _rms_bwd(dxh, xh, r):
    return r * (dxh - xh * jnp.mean(dxh * xh, axis=-1, keepdims=True))


def _sigmoid(x):
    return 1.0 / (1.0 + jnp.exp(-x))


def _silu_grad(x, sig):
    return sig * (1.0 + x * (1.0 - sig))


def _row_tile(t, want):
    return min(t, want)


def _inproj(h, g, gathered, layer):
    t = h.shape[0]
    tm = _row_tile(t, 512)
    tn = D_MODEL
    scale = jnp.ones((1, D_IN), F32).at[:, 4 * GROUP:5 * GROUP].set(B_D ** -0.5)

    def body(h_ref, g_ref, w_ref, s_ref, o_ref, ob_ref):
        xh, _ = _rms(h_ref[...])
        acc = _dot(_bf(xh * g_ref[...]), w_ref[...])
        o_ref[...] = acc
        ob_ref[...] = _bf(acc * s_ref[...])

    return pl.pallas_call(
        body, name="inproj", grid=(D_IN // tn, t // tm),
        in_specs=[pl.BlockSpec((tm, D_MODEL), lambda j, i: (i, 0)),
                  pl.BlockSpec((1, D_MODEL), lambda j, i: (0, 0)),
                  pl.BlockSpec((None, D_MODEL, tn), lambda j, i: (j, layer, 0)),
                  pl.BlockSpec((1, tn), lambda j, i: (0, j))],
        out_specs=[pl.BlockSpec((tm, tn), lambda j, i: (i, j)), pl.BlockSpec((tm, tn), lambda j, i: (i, j))],
        out_shape=[_sds((t, D_IN)), _sds((t, D_IN), BF16)], compiler_params=_params(("arbitrary", "arbitrary")),
    )(h, g, gathered, scale)


def _rows_spec(first_row):
    q = D_MODEL // N_SHARD
    return pl.BlockSpec((N_SHARD, q, D_MODEL), lambda i: (0, first_row // q, 0))


def _outproj(h, ya, yb, gathered, layer):
    t = h.shape[0]
    tm = _row_tile(t, 512)

    def body(h_ref, ya_ref, yb_ref, w_ref, o_ref):
        o_ref[...] = (h_ref[...] + _dot(_bf(ya_ref[...]), w_ref[0:2].reshape(GROUP, D_MODEL))
                      + _dot(_bf(yb_ref[...]), w_ref[2:4].reshape(GROUP, D_MODEL)))

    return pl.pallas_call(
        body, name="outproj", grid=(t // tm,),
        in_specs=[pl.BlockSpec((tm, D_MODEL), lambda i: (i, 0)),
                  pl.BlockSpec((tm, GROUP), lambda i: (i, 0)),
                  pl.BlockSpec((tm, GROUP), lambda i: (i, 0)),
                  _rows_spec(ROWS_W_IN + layer * (D_MODEL // N_SHARD))],
        out_specs=pl.BlockSpec((tm, D_MODEL), lambda i: (i, 0)),
        out_shape=_sds((t, D_MODEL)), compiler_params=_params(("arbitrary",)),
    )(h, ya, yb, gathered)


def _ple_fwd(h, p, w_pp, gathered, layer, g_post, g_gate):
    t = h.shape[0]
    tm = _row_tile(t, 256)

    def body(h_ref, p_ref, wpp_ref, wpg_ref, gp_ref, gg_ref, o_ref):
        x = h_ref[...]
        ph, _ = _rms(_dot(_bf(p_ref[...]), wpp_ref[...]))
        xh, _ = _rms(x)
        gate = _sigmoid(_dot(_bf(xh * gg_ref[...]), wpg_ref[...].reshape(D_MODEL, D_MODEL)))
        o_ref[...] = x + gate * (ph * gp_ref[...])

    return pl.pallas_call(
        body, name="ple_fwd", grid=(t // tm,),
        in_specs=[pl.BlockSpec((tm, D_MODEL), lambda i: (i, 0)),
                  pl.BlockSpec((tm, D_PLE), lambda i: (i, 0)),
                  pl.BlockSpec((D_PLE, D_MODEL), lambda i: (0, 0)),
                  _rows_spec(ROWS_W_IN + ROWS_W_OUT + layer * (D_MODEL // N_SHARD)),
                  pl.BlockSpec((1, D_MODEL), lambda i: (0, 0)),
                  pl.BlockSpec((1, D_MODEL), lambda i: (0, 0))],
        out_specs=pl.BlockSpec((tm, D_MODEL), lambda i: (i, 0)),
        out_shape=_sds((t, D_MODEL)), compiler_params=_params(("arbitrary",)),
    )(h, p, w_pp, gathered, g_post, g_gate)


def _ple_bwd(dh2, h, p, w_pp, gathered, layer, g_post, g_gate):
    t = h.shape[0]
    tm = _row_tile(t, 256)

    def body(d_ref, h_ref, p_ref, wpp_ref, wpg_ref, gp_ref, gg_ref, dh_ref, dwpg_ref, dwpp_ref, dgg_ref, dgp_ref):
        @pl.when(pl.program_id(0) == 0)
        def _():
            dwpg_ref[...] = jnp.zeros_like(dwpg_ref)
            dwpp_ref[...] = jnp.zeros_like(dwpp_ref)
            dgg_ref[...] = jnp.zeros_like(dgg_ref)
            dgp_ref[...] = jnp.zeros_like(dgp_ref)

        d = d_ref[...]
        x = h_ref[...]
        gp = gp_ref[...]
        gg = gg_ref[...]
        pb = _bf(p_ref[...])
        ph, rp = _rms(_dot(pb, wpp_ref[...]))
        pe = ph * gp
        xh, rx = _rms(x)
        un = _bf(xh * gg)
        wpg = wpg_ref[...].reshape(D_MODEL, D_MODEL)
        gate = _sigmoid(_dot(un, wpg))
        dgpre = _bf(d * pe * gate * (1.0 - gate))
        dun = _dot_nt(dgpre, wpg)
        dh_ref[...] = d + _rms_bwd(dun * gg, xh, rx)
        dgg_ref[...] += jnp.sum(dun * xh, axis=0, keepdims=True)
        dwpg_ref[...] += _dot_tn(un, dgpre)
        dpe = d * gate
        dgp_ref[...] += jnp.sum(dpe * ph, axis=0, keepdims=True)
        dwpp_ref[...] += _dot_tn(pb, _bf(_rms_bwd(dpe * gp, ph, rp)))

    return pl.pallas_call(
        body, name="ple_bwd", grid=(t // tm,),
        in_specs=[pl.BlockSpec((tm, D_MODEL), lambda i: (i, 0)),
                  pl.BlockSpec((tm, D_MODEL), lambda i: (i, 0)),
                  pl.BlockSpec((tm, D_PLE), lambda i: (i, 0)),
                  pl.BlockSpec((D_PLE, D_MODEL), lambda i: (0, 0)),
                  _rows_spec(ROWS_W_IN + ROWS_W_OUT + layer * (D_MODEL // N_SHARD)),
                  pl.BlockSpec((1, D_MODEL), lambda i: (0, 0)),
                  pl.BlockSpec((1, D_MODEL), lambda i: (0, 0))],
        out_specs=[pl.BlockSpec((tm, D_MODEL), lambda i: (i, 0)),
                   pl.BlockSpec((D_MODEL, D_MODEL), lambda i: (0, 0)),
                   pl.BlockSpec((D_PLE, D_MODEL), lambda i: (0, 0)),
                   pl.BlockSpec((1, D_MODEL), lambda i: (0, 0)),
                   pl.BlockSpec((1, D_MODEL), lambda i: (0, 0))],
        out_shape=[_sds((t, D_MODEL)), _sds((D_MODEL, D_MODEL)), _sds((D_PLE, D_MODEL)),
                   _sds((1, D_MODEL)), _sds((1, D_MODEL))],
        compiler_params=_params(("arbitrary",)),
    )(dh2, h, p, w_pp, gathered, g_post, g_gate)


def _outproj_bwd(dh, ya, yb, gathered, layer):
    t = dh.shape[0]
    tm = _row_tile(t, 512)

    def body(d_ref, ya_ref, yb_ref, w_ref, dya_ref, dyb_ref, dw_ref):
        @pl.when(pl.program_id(0) == 0)
        def _():
            dw_ref[...] = jnp.zeros_like(dw_ref)

        d = _bf(d_ref[...])
        dya_ref[...] = _dot_nt(d, w_ref[0:2].reshape(GROUP, D_MODEL))
        dyb_ref[...] = _dot_nt(d, w_ref[2:4].reshape(GROUP, D_MODEL))
        dw_ref[pl.ds(0, GROUP), :] += _dot_tn(_bf(ya_ref[...]), d)
        dw_ref[pl.ds(GROUP, GROUP), :] += _dot_tn(_bf(yb_ref[...]), d)

    return pl.pallas_call(
        body, name="outproj_bwd", grid=(t // tm,),
        in_specs=[pl.BlockSpec((tm, D_MODEL), lambda i: (i, 0)),
                  pl.BlockSpec((tm, GROUP), lambda i: (i, 0)),
                  pl.BlockSpec((tm, GROUP), lambda i: (i, 0)),
                  _rows_spec(ROWS_W_IN + layer * (D_MODEL // N_SHARD))],
        out_specs=[pl.BlockSpec((tm, GROUP), lambda i: (i, 0)),
                   pl.BlockSpec((tm, GROUP), lambda i: (i, 0)),
                   pl.BlockSpec((D_MODEL, D_MODEL), lambda i: (0, 0))],
        out_shape=[_sds((t, GROUP)), _sds((t, GROUP)), _sds((D_MODEL, D_MODEL))],
        compiler_params=_params(("arbitrary",)),
    )(dh, ya, yb, gathered)


def _inproj_bwd_dx(dres, h, g, gathered, layer, da, db):
    t = h.shape[0]
    tm = _row_tile(t, 256)

    def body(dres_ref, h_ref, g_ref, w_ref, da_ref, db_ref, dh_ref, dg_ref):
        @pl.when(pl.program_id(0) == 0)
        def _():
            dg_ref[...] = jnp.zeros_like(dg_ref)

        du = jnp.zeros((tm, D_MODEL), F32)
        for i in range(8):
            part = da_ref[i] if i < 4 else db_ref[i - 4]
            du = du + _dot_nt(part, w_ref[i // 2, :, pl.ds((i % 2) * GROUP, GROUP)])
        xh, r = _rms(h_ref[...])
        dg_ref[...] += jnp.sum(du * xh, axis=0, keepdims=True)
        dh_ref[...] = dres_ref[...] + _rms_bwd(du * g_ref[...], xh, r)

    return pl.pallas_call(
        body, name="inproj_bwd_dx", grid=(t // tm,),
        in_specs=[pl.BlockSpec((tm, D_MODEL), lambda i: (i, 0)),
                  pl.BlockSpec((tm, D_MODEL), lambda i: (i, 0)),
                  pl.BlockSpec((1, D_MODEL), lambda i: (0, 0)),
                  pl.BlockSpec((N_SHARD, D_MODEL, D_MODEL), lambda i: (0, layer, 0)),
                  pl.BlockSpec((4, tm, GROUP), lambda i: (0, i, 0)),
                  pl.BlockSpec((4, tm, GROUP), lambda i: (0, i, 0))],
        out_specs=[pl.BlockSpec((tm, D_MODEL), lambda i: (i, 0)),
                   pl.BlockSpec((1, D_MODEL), lambda i: (0, 0))],
        out_shape=[_sds((t, D_MODEL)), _sds((1, D_MODEL))],
        compiler_params=_params(("arbitrary",)),
    )(dres, h, g, gathered, da, db)


def _inproj_bwd_dw(h, g, da, db):
    t = h.shape[0]
    tm = _row_tile(t, 512)

    def body(h_ref, g_ref, da_ref, db_ref, dw_ref):
        @pl.when(pl.program_id(0) == 0)
        def _():
            dw_ref[...] = jnp.zeros_like(dw_ref)

        xh, _ = _rms(h_ref[...])
        u = _bf(xh * g_ref[...])
        for i in range(8):
            dw_ref[i] += _dot_tn(u, da_ref[i] if i < 4 else db_ref[i - 4])

    return pl.pallas_call(
        body, name="inproj_bwd_dw", grid=(t // tm,),
        in_specs=[pl.BlockSpec((tm, D_MODEL), lambda i: (i, 0)),
                  pl.BlockSpec((1, D_MODEL), lambda i: (0, 0)),
                  pl.BlockSpec((4, tm, GROUP), lambda i: (0, i, 0)),
                  pl.BlockSpec((4, tm, GROUP), lambda i: (0, i, 0))],
        out_specs=pl.BlockSpec((8, D_MODEL, GROUP), lambda i: (0, 0, 0)),
        out_shape=_sds((8, D_MODEL, GROUP)), compiler_params=_params(("arbitrary",), VMEM_LIMIT_BIG),
    )(h, g, da, db)


def _final(h, g, target):
    t = h.shape[0]
    tm = _row_tile(t, 512)

    def body(h_ref, g_ref, t_ref, dh_ref, dg_ref, loss_ref):
        @pl.when(pl.program_id(0) == 0)
        def _():
            dg_ref[...] = jnp.zeros_like(dg_ref)
            loss_ref[...] = jnp.zeros_like(loss_ref)

        xh, r = _rms(h_ref[...])
        gg = g_ref[...]
        err = xh * gg - t_ref[...]
        part = 0.5 * jnp.sum(jnp.mean(err * err, axis=-1, keepdims=True), axis=0, keepdims=True)
        loss_ref[...] += jnp.broadcast_to(part, loss_ref.shape)
        dy = err * (1.0 / D_MODEL)
        dg_ref[...] += jnp.sum(dy * xh, axis=0, keepdims=True)
        dh_ref[...] = _rms_bwd(dy * gg, xh, r)

    return pl.pallas_call(
        body, name="final", grid=(t // tm,),
        in_specs=[pl.BlockSpec((tm, D_MODEL), lambda i: (i, 0)),
                  pl.BlockSpec((1, D_MODEL), lambda i: (0, 0)),
                  pl.BlockSpec((tm, D_MODEL), lambda i: (i, 0))],
        out_specs=[pl.BlockSpec((tm, D_MODEL), lambda i: (i, 0)),
                   pl.BlockSpec((1, D_MODEL), lambda i: (0, 0)),
                   pl.BlockSpec((1, 128), lambda i: (0, 0))],
        out_shape=[_sds((t, D_MODEL)), _sds((1, D_MODEL)), _sds((1, 128))],
        compiler_params=_params(("arbitrary",)),
    )(h, g, target)


def _hgrn_consts():
    c, nl = HG_CHUNK, HG_LEVELS
    t = np.arange(c)
    tril = np.tril(np.ones((c, c), np.float32))
    masks = np.zeros((nl + 1, c, c), np.float32)
    masks[0] = np.eye(c, dtype=np.float32)
    dmat = np.zeros(((nl + 2) * c, c), np.float32)
    dmat[0:c] = tril
    for l in range(nl):
        m = c >> (l + 1)
        blk = t // (2 * m)
        r = blk * 2 * m + m - 1
        upper = (t % (2 * m)) >= m
        masks[l + 1] = ((blk[:, None] == blk[None, :]) & upper[:, None] & (~upper)[None, :]).astype(np.float32)
        dmat[(l + 1) * c:(l + 2) * c] = tril[t] - tril[r]
    dmat[(nl + 1) * c:] = np.triu(np.ones((c, c), np.float32), k=1)
    return jnp.asarray(masks), jnp.asarray(dmat, BF16)


HG_HEADS = 4


def _hgrn_pre(aq, af, lb):
    sq = _sigmoid(aq)
    sneg = _sigmoid(-af)
    kk = (1.0 - lb) * sneg
    return sq, aq * sq, sneg, kk, jnp.log1p(-kk)


def _hgrn_x(logf, dmat_ref):
    dm = dmat_ref[pl.ds(0, (HG_LEVELS + 1) * HG_CHUNK), :]
    lhi, llo = _split(logf)
    return _dot(dm, lhi) + _dot(dm, llo)


def _hgrn_level(x_all, l, q, kk):
    c = HG_CHUNK
    x = x_all[(l + 1) * c:(l + 2) * c]
    qf = jnp.exp(jnp.minimum(x, 0.0))
    kf = jnp.exp(-jnp.maximum(x, 0.0))
    return qf, kf, _bf(q * qf), _bf(kk * kf)


def _hgrn_scores(xs, qs, kks, mask_ref):
    ps = [mask_ref[0] * _dot_nt(_bf(q), _bf(kk)) for q, kk in zip(qs, kks)]
    for l in range(HG_LEVELS):
        for i, (x_all, q, kk) in enumerate(zip(xs, qs, kks)):
            _, _, ql, kl = _hgrn_level(x_all, l, q, kk)
            ps[i] = ps[i] + mask_ref[l + 1] * _dot_nt(ql, kl)
    return ps


def _hgrn_specs(n_chunks, rev):
    c, w = HG_CHUNK, HG_HEADS * A_D
    cidx = (lambda n: n_chunks - 1 - n) if rev else (lambda n: n)
    col = lambda g: pl.BlockSpec((c, w), lambda h, n: (cidx(n), g * (A_HEADS // HG_HEADS) + h))
    vec = pl.BlockSpec((1, w), lambda h, n: (0, h))
    mask = pl.BlockSpec((HG_LEVELS + 1, c, c), lambda h, n: (0, 0, 0))
    dmat = pl.BlockSpec(((HG_LEVELS + 2) * c, c), lambda h, n: (0, 0))
    state = pl.BlockSpec((HG_HEADS, None, A_D, A_D), lambda h, n: (h, cidx(n), 0, 0))
    return cidx, col, vec, mask, dmat, state


def _lanes(i):
    return pl.ds(i * A_D, A_D)


def _hgrn_fwd(proj, lb, gain):
    t = proj.shape[0]
    c = HG_CHUNK
    nch = t // c
    masks, dmat = _hgrn_consts()
    cidx, col, vec, mask_spec, dmat_spec, state_spec = _hgrn_specs(nch, False)
    heads = range(HG_HEADS)

    def body(aq_ref, af_ref, ai_ref, ag_ref, lb_ref, gain_ref, mask_ref, dmat_ref, y_ref, st_ref, s_scr):
        @pl.when(pl.program_id(1) == 0)
        def _():
            s_scr[...] = jnp.zeros_like(s_scr)

        pre = [_hgrn_pre(aq_ref[:, _lanes(i)], af_ref[:, _lanes(i)], lb_ref[:, _lanes(i)]) for i in heads]
        qs, kks = [p[1] for p in pre], [p[3] for p in pre]
        xs = [_hgrn_x(p[4], dmat_ref) for p in pre]
        bs = [x[0:c] for x in xs]
        b_lasts = [jnp.sum(p[4], axis=0, keepdims=True) for p in pre]
        ps = _hgrn_scores(xs, qs, kks, mask_ref)
        ss = [s_scr[i] for i in heads]
        vbs = [_bf(ai_ref[:, _lanes(i)]) for i in heads]
        os_ = [_dot(_bf(ps[i]), vbs[i]) + _dot_nt(_bf(qs[i] * jnp.exp(bs[i])), _bf(ss[i])) for i in heads]
        for i in heads:
            st_ref[i] = ss[i]
            s_scr[i] = ss[i] * jnp.exp(b_lasts[i]) + _dot_tn(vbs[i], _bf(kks[i] * jnp.exp(b_lasts[i] - bs[i])))
            oh, _ = _rms(os_[i])
            ag = ag_ref[:, _lanes(i)]
            y_ref[:, _lanes(i)] = oh * gain_ref[:, _lanes(i)] * (ag * _sigmoid(ag))

    return pl.pallas_call(
        body, name="hgrn_fwd", grid=(A_HEADS // HG_HEADS, nch),
        in_specs=[col(0), col(1), col(2), col(3), vec, vec, mask_spec, dmat_spec],
        out_specs=[pl.BlockSpec((c, HG_HEADS * A_D), lambda h, n: (n, h)), state_spec],
        out_shape=[_sds((t, GROUP)), _sds((A_HEADS, nch, A_D, A_D))],
        scratch_shapes=[pltpu.VMEM((HG_HEADS, A_D, A_D), F32)],
        compiler_params=_params(("arbitrary", "arbitrary")),
    )(proj, proj, proj, proj, lb, gain, masks, dmat)


def _hgrn_bwd(proj, lb, gain, states, dya):
    t = proj.shape[0]
    c, nl = HG_CHUNK, HG_LEVELS
    nch = t // c
    masks, dmat = _hgrn_consts()
    cidx, col, vec, mask_spec, dmat_spec, state_spec = _hgrn_specs(nch, True)
    heads = range(HG_HEADS)

    def body(aq_ref, af_ref, ai_ref, ag_ref, lb_ref, gain_ref, mask_ref, dmat_ref, st_ref, dy_ref,
             da_ref, dlb_ref, dgain_ref, ds_scr, z_scr):
        @pl.when(pl.program_id(1) == 0)
        def _():
            ds_scr[...] = jnp.zeros_like(ds_scr)
            dlb_ref[...] = jnp.zeros_like(dlb_ref)
            dgain_ref[...] = jnp.zeros_like(dgain_ref)

        aqs = [aq_ref[:, _lanes(i)] for i in heads]
        lbs = [lb_ref[:, _lanes(i)] for i in heads]
        pre = [_hgrn_pre(aqs[i], af_ref[:, _lanes(i)], lbs[i]) for i in heads]
        sqs, qs, snegs, kks = ([p[j] for p in pre] for j in range(4))
        xs = [_hgrn_x(p[4], dmat_ref) for p in pre]
        bs = [x[0:c] for x in xs]
        b_lasts = [jnp.sum(p[4], axis=0, keepdims=True) for p in pre]
        ebs = [jnp.exp(b) for b in bs]
        ebls = [jnp.exp(bl - b) for bl, b in zip(b_lasts, bs)]
        ebl_rows = [jnp.exp(bl) for bl in b_lasts]
        qes = [_bf(q * eb) for q, eb in zip(qs, ebs)]
        kes = [_bf(kk * ebl) for kk, ebl in zip(kks, ebls)]
        vbs = [_bf(ai_ref[:, _lanes(i)]) for i in heads]
        ss = [st_ref[i] for i in heads]
        sbs = [_bf(s) for s in ss]
        dss = [ds_scr[i] for i in heads]
        dsbs = [_bf(ds) for ds in dss]

        pbs = [_bf(p) for p in _hgrn_scores(xs, qs, kks, mask_ref)]
        os_ = [_dot(pbs[i], vbs[i]) + _dot_nt(qes[i], sbs[i]) for i in heads]

        dos = []
        for i in heads:
            ag, gain, dy = ag_ref[:, _lanes(i)], gain_ref[:, _lanes(i)], dy_ref[:, _lanes(i)]
            oh, r = _rms(os_[i])
            sg_sig = _sigmoid(ag)
            sg = ag * sg_sig
            da_ref[3, :, _lanes(i)] = _bf(dy * oh * gain * _silu_grad(ag, sg_sig))
            dgain_ref[:, _lanes(i)] += jnp.sum(dy * oh * sg, axis=0, keepdims=True)
            dos.append(_bf(_rms_bwd(dy * gain * sg, oh, r)))

        dps = [_dot_nt(dos[i], vbs[i]) for i in heads]
        for i in heads:
            da_ref[2, :, _lanes(i)] = _bf(_dot_tn(pbs[i], dos[i]) + _dot_nt(kes[i], dsbs[i]))
        dq_ss = [ebs[i] * _dot(dos[i], sbs[i]) for i in heads]
        dk_ss = [ebls[i] * _dot(vbs[i], dsbs[i]) for i in heads]
        dqs, dks = [], []
        for i in heads:
            dpd = jnp.sum(mask_ref[0] * dps[i], axis=1, keepdims=True)
            z_scr[i, pl.ds(0, c), :] = qs[i] * dq_ss[i]
            z_scr[i, pl.ds((nl + 1) * c, c), :] = kks[i] * dk_ss[i]
            dqs.append(dq_ss[i] + dpd * kks[i])
            dks.append(dk_ss[i] + dpd * qs[i])
        for l in range(nl):
            for i in heads:
                qf, kf, ql, kl = _hgrn_level(xs[i], l, qs[i], kks[i])
                dpl = _bf(mask_ref[l + 1] * dps[i])
                dq_l = qf * _dot(dpl, kl)
                dk_l = kf * _dot_tn(dpl, ql)
                z_scr[i, pl.ds((l + 1) * c, c), :] = qs[i] * dq_l - kks[i] * dk_l
                dqs[i] = dqs[i] + dq_l
                dks[i] = dks[i] + dk_l

        zsplits = [_split(z_scr[i]) for i in heads]
        dlogfs = [_dot_tn(dmat_ref[...], zhi) + _dot_tn(dmat_ref[...], zlo) for zhi, zlo in zsplits]
        ds_new = [_dot_tn(dos[i], qes[i]) for i in heads]
        for i in heads:
            dlogf = dlogfs[i] + ebl_rows[i] * jnp.sum(dss[i] * ss[i], axis=0, keepdims=True)
            dkk = dks[i] - dlogf / (1.0 - kks[i])
            da_ref[1, :, _lanes(i)] = _bf(dkk * (1.0 - lbs[i]) * (-(snegs[i] * (1.0 - snegs[i]))))
            dlb_ref[:, _lanes(i)] += jnp.sum(dkk * (-snegs[i]), axis=0, keepdims=True)
            da_ref[0, :, _lanes(i)] = _bf(dqs[i] * _silu_grad(aqs[i], sqs[i]))
            ds_scr[i] = dss[i] * ebl_rows[i] + ds_new[i]

    w = HG_HEADS * A_D
    return pl.pallas_call(
        body, name="hgrn_bwd", grid=(A_HEADS // HG_HEADS, nch),
        in_specs=[col(0), col(1), col(2), col(3), vec, vec, mask_spec, dmat_spec, state_spec,
                  pl.BlockSpec((c, w), lambda h, n: (cidx(n), h))],
        out_specs=[pl.BlockSpec((4, c, w), lambda h, n: (0, cidx(n), h)), vec, vec],
        out_shape=[_sds((4, t, GROUP), BF16)] + [_sds((1, GROUP))] * 2,
        scratch_shapes=[pltpu.VMEM((HG_HEADS, A_D, A_D), F32), pltpu.VMEM((HG_HEADS, (nl + 2) * c, A_D), F32)],
        compiler_params=_params(("arbitrary", "arbitrary")),
    )(proj, proj, proj, proj, lb, gain, masks, dmat, states, dya)


def _sb_consts():
    j = np.arange(SB_TK)
    ones = np.ones((SB_TK, SB_TK), np.float32)
    strict = np.concatenate([(j[:, None] > j[None, :]).astype(np.float32), ones], axis=1)
    incl = np.concatenate([(j[:, None] >= j[None, :]).astype(np.float32), ones], axis=1)
    return jnp.asarray(strict, BF16), jnp.asarray(incl, BF16)


def _sb_softplus(z, masked):
    lg = jnp.log(1.0 + jnp.exp(-jnp.abs(z)))
    sp = jnp.maximum(z, 0.0) + lg
    logsig = jnp.minimum(z, 0.0) - lg
    mask = None
    if masked:
        mask = lax.broadcasted_iota(jnp.int32, z.shape, 1) < lax.broadcasted_iota(jnp.int32, z.shape, 0)
        sp = jnp.where(mask, sp, 0.0)
    return mask, sp, logsig


def _sb_sweep(qi, group_fn, state):
    nd = SB_TQ // SB_TK
    state = group_fn([(pl.multiple_of((qi * nd + d) * SB_TK, SB_TK), d * SB_TK, True) for d in reversed(range(nd))],
                     state)

    def step(j, st):
        return group_fn([(pl.multiple_of(((qi - j) * nd - 1 - g) * SB_TK, SB_TK), 0, False) for g in range(nd)], st)

    return lax.fori_loop(0, qi, step, state)


def _set_rows(r0, full, new):
    return new if r0 == 0 else jnp.concatenate([full[:r0], new], axis=0)


def _sb_specs(t, tq):
    col = lambda g: pl.BlockSpec((tq, 2 * B_D), lambda p, i, h: (i, g * (GROUP // (2 * B_D)) + p))
    full = lambda g: pl.BlockSpec((t, 2 * B_D), lambda p, i, h: (0, g * (GROUP // (2 * B_D)) + p))
    vec = pl.BlockSpec((1, 2 * B_D), lambda p, i, h: (0, p))
    mat = pl.BlockSpec((SB_TK, 2 * SB_TK), lambda p, i, h: (0, 0))
    return col, full, vec, mat


def _head_lanes(h):
    return (lax.broadcasted_iota(jnp.int32, (1, 2 * B_D), 1) >= B_D) == (h == 1)


def _put(ref, h, val):
    @pl.when(h == 0)
    def _():
        ref[...] = val

    @pl.when(h == 1)
    def _():
        ref[...] += val


def _sb_fwd(proj_bf, proj, gain):
    t = proj.shape[0]
    tq = SB_TQ
    strict, _ = _sb_consts()

    def body(q_ref, k_ref, v_ref, bg_ref, gain_ref, m_ref, o_ref, y_ref):
        h = pl.program_id(2)
        lanes = _head_lanes(h)
        qb = jnp.where(lanes, q_ref[...], jnp.zeros_like(q_ref))
        cmat = m_ref[...]

        def group(tiles, state):
            carry, acc = state
            kv = [(k_ref[pl.ds(off, SB_TK), :], v_ref[pl.ds(off, SB_TK), :]) for off, _, _ in tiles]
            zs = [_dot_nt(qb[r0:], kb) for (_, r0, _), (kb, _) in zip(tiles, kv)]
            sps = [_sb_softplus(z, masked) for z, (_, _, masked) in zip(zs, tiles)]
            cs2s = [_dot(_bf(sp), cmat) for _, sp, _ in sps]
            ws = []
            for (mask, _, logsig), cs2, (_, r0, masked) in zip(sps, cs2s, tiles):
                w = jnp.exp(logsig - cs2[:, :SB_TK] - carry[r0:])
                ws.append(_split(jnp.where(mask, w, 0.0) if masked else w))
                carry = _set_rows(r0, carry, carry[r0:] + cs2[:, SB_TK:])
            for (whi, wlo), (_, vb), (_, r0, _) in zip(ws, kv, tiles):
                acc = _set_rows(r0, acc, acc[r0:] + _dot(whi, vb) + _dot(wlo, vb))
            return carry, acc

        _, acc = _sb_sweep(pl.program_id(1), group, (jnp.zeros((tq, SB_TK), F32), jnp.zeros((tq, 2 * B_D), F32)))
        o = jnp.where(lanes, acc, 0.0)
        oh = o * lax.rsqrt(jnp.sum(o * o, axis=-1, keepdims=True) * (1.0 / B_D) + EPS)
        bg = bg_ref[...]
        _put(o_ref, h, o)
        _put(y_ref, h, oh * gain_ref[...] * (bg * _sigmoid(bg)))

    col, full, vec, mat = _sb_specs(t, tq)
    out = pl.BlockSpec((tq, 2 * B_D), lambda p, i, h: (i, p))
    return pl.pallas_call(
        body, name="sb_fwd", grid=(B_HEADS // 2, t // tq, 2),
        in_specs=[col(4), full(5), full(6), col(7), vec, mat],
        out_specs=[out, out],
        out_shape=[_sds((t, GROUP)), _sds((t, GROUP))],
        compiler_params=_params(("arbitrary", "arbitrary", "arbitrary")),
    )(proj_bf, proj_bf, proj_bf, proj, gain, strict)


def _sb_bwd(proj_bf, proj, o, dy, gain):
    t = proj.shape[0]
    tq = SB_TQ
    strict, incl = _sb_consts()

    def body(q_ref, k_ref, v_ref, bg_ref, o_ref, dy_ref, gain_ref, ms_ref, mi_ref,
             dq_ref, dk_ref, dv_ref, dbg_ref, dgain_ref):
        qi = pl.program_id(1)
        h = pl.program_id(2)
        lanes = _head_lanes(h)

        @pl.when((qi == 0) & (h == 0))
        def _():
            dk_ref[...] = jnp.zeros_like(dk_ref)
            dv_ref[...] = jnp.zeros_like(dv_ref)
            dgain_ref[...] = jnp.zeros_like(dgain_ref)

        qb = jnp.where(lanes, q_ref[...], jnp.zeros_like(q_ref))
        cmat = ms_ref[...]
        imat = mi_ref[...]
        o = jnp.where(lanes, o_ref[...], 0.0)
        dy = jnp.where(lanes, dy_ref[...], 0.0)
        bg = bg_ref[...]
        gain = gain_ref[...]
        r = lax.rsqrt(jnp.sum(o * o, axis=-1, keepdims=True) * (1.0 / B_D) + EPS)
        oh = o * r
        sig = _sigmoid(bg)
        sg = bg * sig
        _put(dbg_ref, h, dy * oh * gain * _silu_grad(bg, sig))
        dgain_ref[...] += jnp.sum(dy * oh * sg, axis=0, keepdims=True)
        doh = dy * gain * sg
        do = _bf(r * (doh - oh * (jnp.sum(doh * oh, axis=-1, keepdims=True) * (1.0 / B_D))))
        total = jnp.broadcast_to(jnp.sum(do.astype(F32) * o, axis=1, keepdims=True), (tq, SB_TK))

        def group(tiles, state):
            carry, gcarry, dq = state
            kv = [(k_ref[pl.ds(off, SB_TK), :], v_ref[pl.ds(off, SB_TK), :]) for off, _, _ in tiles]
            zs = [_dot_nt(qb[r0:], kb) for (_, r0, _), (kb, _) in zip(tiles, kv)]
            dws = [_dot_nt(do[r0:], vb) for (_, r0, _), (_, vb) in zip(tiles, kv)]
            sps = [_sb_softplus(z, masked) for z, (_, _, masked) in zip(zs, tiles)]
            cs2s = [_dot(_bf(sp), cmat) for _, sp, _ in sps]
            ws, gs = [], []
            for (mask, _, logsig), cs2, dw, (_, r0, masked) in zip(sps, cs2s, dws, tiles):
                w = jnp.exp(logsig - cs2[:, :SB_TK] - carry[r0:])
                w = jnp.where(mask, w, 0.0) if masked else w
                ws.append(_bf(w))
                gs.append(dw * w)
                carry = _set_rows(r0, carry, carry[r0:] + cs2[:, SB_TK:])
            s2s = []
            for g in gs:
                ghi, glo = _split(g)
                s2s.append(_dot(ghi, imat) + _dot(glo, imat))
            dzs = []
            for (mask, _, logsig), g, s2, (_, r0, masked) in zip(sps, gs, s2s, tiles):
                before = total[r0:] - gcarry[r0:] - s2[:, :SB_TK]
                sig_z = jnp.exp(logsig)
                dz = g * (1.0 - sig_z) - sig_z * before
                dzs.append(_bf(jnp.where(mask, dz, 0.0) if masked else dz))
                gcarry = _set_rows(r0, gcarry, gcarry[r0:] + s2[:, SB_TK:])
            for dz, wb, (kb, _), (off, r0, _) in zip(dzs, ws, kv, tiles):
                dq = _set_rows(r0, dq, dq[r0:] + _dot(dz, kb))
                dk_ref[pl.ds(off, SB_TK), :] += _dot_tn(dz, qb[r0:])
                dv_ref[pl.ds(off, SB_TK), :] += _dot_tn(wb, do[r0:])
            return carry, gcarry, dq

        zero = jnp.zeros((tq, SB_TK), F32)
        _, _, dq = _sb_sweep(qi, group, (zero, zero, jnp.zeros((tq, 2 * B_D), F32)))
        _put(dq_ref, h, jnp.where(lanes, dq * (B_D ** -0.5), 0.0))

    col, full, vec, mat = _sb_specs(t, tq)
    blk = pl.BlockSpec((tq, 2 * B_D), lambda p, i, h: (i, p))
    whole = pl.BlockSpec((t, 2 * B_D), lambda p, i, h: (0, p))
    return pl.pallas_call(
        body, name="sb_bwd", grid=(B_HEADS // 2, t // tq, 2),
        in_specs=[col(4), full(5), full(6), col(7), blk, blk, vec, mat, mat],
        out_specs=[blk, whole, whole, blk, vec],
        out_shape=[_sds((t, GROUP))] * 4 + [_sds((1, GROUP))],
        compiler_params=_params(("arbitrary", "arbitrary", "arbitrary")),
    )(proj_bf, proj_bf, proj_bf, proj, o, dy, gain, strict, incl)


def _adamw(w, g, m, v):
    rows, cols = w.shape
    tr = rows
    for cand in (400, 256, 128, 64, 32, 16, 8):
        if rows % cand == 0:
            tr = cand
            break

    def body(w_ref, g_ref, m_ref, v_ref, d_ref, nm_ref, nv_ref):
        g_ = g_ref[...]
        m_ = ADAM_B1 * m_ref[...] + (1.0 - ADAM_B1) * g_
        v_ = ADAM_B2 * v_ref[...] + (1.0 - ADAM_B2) * (g_ * g_)
        m_hat = m_ / (1.0 - ADAM_B1 ** ADAM_STEP)
        v_hat = v_ / (1.0 - ADAM_B2 ** ADAM_STEP)
        d_ref[...] = -ADAM_LR * (m_hat / (jnp.sqrt(v_hat) + ADAM_EPS) + ADAM_WD * w_ref[...])
        nm_ref[...] = m_
        nv_ref[...] = v_

    spec = pl.BlockSpec((tr, cols), lambda i: (i, 0))
    return pl.pallas_call(
        body, name="adamw", grid=(rows // tr,), in_specs=[spec] * 4, out_specs=[spec] * 3,
        out_shape=[_sds((rows, cols))] * 3, compiler_params=_params(("arbitrary",)),
    )(w, g, m, v)


_ANY = pl.BlockSpec(memory_space=pl.ANY)


def _place():
    return lax.axis_index("x"), lax.axis_index("y"), lax.axis_index("c")


def _gather_weights(flat):
    half, ch, nc = HALF_FLAT, CHUNK_ROWS, N_CHUNK

    def body(x_ref, out_ref, send_sems, recv_sems):
        x, y, c = _place()
        me = 2 * x + y
        sibling = (x, y, 1 - c)

        def rows(shard, hc, r):
            return out_ref.at[shard, pl.ds(hc * half + r * ch, ch), :]

        def copy(k, shard, hc, r, to, src=None):
            return pltpu.make_async_remote_copy(
                src_ref=rows(shard, hc, r) if src is None else src, dst_ref=rows(shard, hc, r),
                send_sem=send_sems.at[k * nc + r], recv_sem=recv_sems.at[k * nc + r], device_id=to, device_id_type=MESH)

        peers = [me ^ k for k in (1, 2, 3)]
        first = [copy(k, me, c, r, (p >> 1, p & 1, c), src=x_ref.at[pl.ds(c * half + r * ch, ch), :])
                 for k, p in enumerate(peers) for r in range(nc)]
        for cp in first:
            cp.start()
        passed = []
        for k, p in enumerate(peers):
            for r in range(nc):
                copy(k, p, c, r, sibling).wait_recv()
                passed.append(copy(3 + k, p, c, r, sibling))
                passed[-1].start()
        for k, p in enumerate(peers):
            for r in range(nc):
                copy(3 + k, p, 1 - c, r, sibling).wait_recv()
        for cp in first + passed:
            cp.wait_send()

    return pl.pallas_call(
        body, name="gather_weights", in_specs=[_ANY], out_specs=_ANY,
        out_shape=_sds((N_SHARD, ROWS_FLAT, D_MODEL), BF16),
        scratch_shapes=[pltpu.SemaphoreType.DMA((6 * nc,)), pltpu.SemaphoreType.DMA((6 * nc,))],
    )(flat)


def _swap_halves(grads):
    half, ch, nc = HALF_FLAT, CHUNK_ROWS, N_CHUNK

    def body(g_ref, out_ref, send_sems, recv_sems):
        x, y, c = _place()
        copies = [pltpu.make_async_remote_copy(
            src_ref=g_ref.at[j, pl.ds((1 - c) * half + r * ch, ch), :], dst_ref=out_ref.at[j, pl.ds(r * ch, ch), :],
            send_sem=send_sems.at[j * nc + r], recv_sem=recv_sems.at[j * nc + r],
            device_id=(x, y, 1 - c), device_id_type=MESH) for j in range(N_SHARD) for r in range(nc)]
        for cp in copies:
            cp.start()
        for cp in copies:
            cp.wait()

    return pl.pallas_call(
        body, name="swap_halves", in_specs=[_ANY], out_specs=_ANY,
        out_shape=_sds((N_SHARD, half, D_MODEL)),
        scratch_shapes=[pltpu.SemaphoreType.DMA((N_SHARD * nc,)), pltpu.SemaphoreType.DMA((N_SHARD * nc,))],
    )(grads)


def _add_my_half(grads, recv):
    tr = 400
    nb = HALF_FLAT // tr
    core = lax.axis_index("c").astype(jnp.int32).reshape(1)

    def body(c_ref, g_ref, r_ref, o_ref, ob_ref):
        acc = g_ref[...] + r_ref[...]
        o_ref[...] = acc
        ob_ref[...] = _bf(acc)

    out = pl.BlockSpec((None, tr, D_MODEL), lambda j, i, c_ref: (j, i, 0))
    return pl.pallas_call(
        body, name="add_my_half",
        grid_spec=pltpu.PrefetchScalarGridSpec(
            num_scalar_prefetch=1, grid=(N_SHARD, nb),
            in_specs=[pl.BlockSpec((None, tr, D_MODEL), lambda j, i, c_ref: (j, c_ref[0] * nb + i, 0)), out],
            out_specs=[out, out]),
        out_shape=[_sds((N_SHARD, HALF_FLAT, D_MODEL)), _sds((N_SHARD, HALF_FLAT, D_MODEL), BF16)],
        compiler_params=_params(("arbitrary", "arbitrary")),
    )(core, grads, recv)


def _scatter_shards(part):
    ch, nc = CHUNK_ROWS, N_CHUNK

    def body(p_ref, out_ref, send_sems, recv_sems):
        x, y, c = _place()
        me = 2 * x + y
        peers = [me ^ k for k in (1, 2, 3)]
        sends = [pltpu.make_async_remote_copy(
            src_ref=p_ref.at[p, pl.ds(r * ch, ch), :], dst_ref=out_ref.at[k, pl.ds(r * ch, ch), :],
            send_sem=send_sems.at[k * nc + r], recv_sem=recv_sems.at[k * nc + r],
            device_id=(p >> 1, p & 1, c), device_id_type=MESH) for k, p in enumerate(peers) for r in range(nc)]
        for cp in sends:
            cp.start()
        for cp in sends:
            cp.wait()

    return pl.pallas_call(
        body, name="scatter_shards", in_specs=[_ANY], out_specs=_ANY,
        out_shape=_sds((3, HALF_FLAT, D_MODEL), BF16),
        scratch_shapes=[pltpu.SemaphoreType.DMA((3 * nc,)), pltpu.SemaphoreType.DMA((3 * nc,))],
    )(part)


def _sum_scattered(part, recv):
    tr = 400
    chip = (2 * lax.axis_index("x") + lax.axis_index("y")).astype(jnp.int32).reshape(1)

    def body(c_ref, p_ref, r_ref, o_ref):
        acc = p_ref[...]
        for k in range(3):
            acc = acc + r_ref[k].astype(F32)
        o_ref[...] = acc

    return pl.pallas_call(
        body, name="sum_scattered",
        grid_spec=pltpu.PrefetchScalarGridSpec(
            num_scalar_prefetch=1, grid=(HALF_FLAT // tr,),
            in_specs=[pl.BlockSpec((None, tr, D_MODEL), lambda i, c_ref: (c_ref[0], i, 0)),
                      pl.BlockSpec((3, tr, D_MODEL), lambda i, c_ref: (0, i, 0))],
            out_specs=pl.BlockSpec((tr, D_MODEL), lambda i, c_ref: (i, 0))),
        out_shape=_sds((HALF_FLAT, D_MODEL)), compiler_params=_params(("arbitrary",)),
    )(chip, part, recv)


def _swap_reduced(mine_half):
    ch, nc = CHUNK_ROWS, N_CHUNK

    def body(r_ref, out_ref, send_sems, recv_sems):
        x, y, c = _place()
        copies = [pltpu.make_async_remote_copy(
            src_ref=r_ref.at[pl.ds(r * ch, ch), :], dst_ref=out_ref.at[pl.ds(r * ch, ch), :],
            send_sem=send_sems.at[r], recv_sem=recv_sems.at[r], device_id=(x, y, 1 - c), device_id_type=MESH)
            for r in range(nc)]
        for cp in copies:
            cp.start()
        for cp in copies:
            cp.wait()

    return pl.pallas_call(
        body, name="swap_reduced", in_specs=[_ANY], out_specs=_ANY,
        out_shape=_sds((HALF_FLAT, D_MODEL)),
        scratch_shapes=[pltpu.SemaphoreType.DMA((nc,)), pltpu.SemaphoreType.DMA((nc,))],
    )(mine_half)


def _allreduce_small(vec):
    def body(v_ref, out_ref, buf, send_sems, recv_sems):
        x, y, c = _place()
        me = 4 * x + 2 * y + c
        buf[me] = v_ref[...]
        peers = [me ^ k for k in range(1, N_DEV)]
        sends = [pltpu.make_async_remote_copy(
            src_ref=v_ref, dst_ref=buf.at[me], send_sem=send_sems.at[k], recv_sem=recv_sems.at[k],
            device_id=(p >> 2, (p >> 1) & 1, p & 1), device_id_type=MESH) for k, p in enumerate(peers)]
        for cp in sends:
            cp.start()
        for k, p in enumerate(peers):
            pltpu.make_async_remote_copy(
                src_ref=v_ref, dst_ref=buf.at[p], send_sem=send_sems.at[k], recv_sem=recv_sems.at[k],
                device_id=(p >> 2, (p >> 1) & 1, p & 1), device_id_type=MESH).wait_recv()
        for cp in sends:
            cp.wait_send()
        acc = buf[0]
        for d in range(1, N_DEV):
            acc = acc + buf[d]
        out_ref[...] = acc

    vm = pl.BlockSpec(memory_space=pltpu.VMEM)
    return pl.pallas_call(
        body, name="allreduce_small", in_specs=[vm], out_specs=vm, out_shape=_sds((SMALL_ROWS, 128)),
        scratch_shapes=[pltpu.VMEM((N_DEV, SMALL_ROWS, 128), F32),
                        pltpu.SemaphoreType.DMA((N_DEV - 1,)), pltpu.SemaphoreType.DMA((N_DEV - 1,))],
    )(vec)


def _flatten_shard(w_in, w_out, w_pg, w_pp):
    return jnp.concatenate([w_in.reshape(-1, D_MODEL), w_out.reshape(-1, D_MODEL), w_pg.reshape(-1, D_MODEL),
                            w_pp.reshape(-1, D_MODEL)], axis=0)


def _unflatten_shard(flat):
    a, b, c = ROWS_W_IN, ROWS_W_IN + ROWS_W_OUT, ROWS_W_IN + ROWS_W_OUT + ROWS_W_PG
    q = D_MODEL // N_SHARD
    return (flat[:a].reshape(2, D_MODEL, D_MODEL), flat[a:b].reshape(2, q, D_MODEL),
            flat[b:c].reshape(2, q, D_MODEL), flat[c:].reshape(2, D_PLE, q))


def _full_w_pp(gathered):
    c = ROWS_W_IN + ROWS_W_OUT + ROWS_W_PG
    q = D_MODEL // N_SHARD
    rpp = ROWS_W_PP // 2
    return [gathered[:, c + l * rpp:c + (l + 1) * rpp, :].reshape(N_SHARD, D_PLE, q).transpose(1, 0, 2)
            .reshape(D_PLE, D_MODEL) for l in range(2)]


def _shard_major(dw_in, dw_out, dw_pg, dw_pp):
    q = D_MODEL // N_SHARD
    rpp = ROWS_W_PP // 2
    parts = []
    for l in range(2):
        g = dw_in[l].reshape(N_SHARD, 2, D_MODEL, GROUP).transpose(0, 2, 1, 3).reshape(N_SHARD, D_MODEL, D_MODEL)
        parts.append(g)
    for l in range(2):
        parts.append(dw_out[l].reshape(N_SHARD, q, D_MODEL))
    for l in range(2):
        parts.append(dw_pg[l].reshape(N_SHARD, q, D_MODEL))
    for l in range(2):
        parts.append(dw_pp[l].reshape(D_PLE, N_SHARD, q).transpose(1, 0, 2).reshape(N_SHARD, rpp, D_MODEL))
    return jnp.concatenate(parts, axis=1)


def _lower_bounds(lb_logits):
    sm = jax.nn.softmax(lb_logits.astype(F32), axis=0)
    return jnp.cumsum(sm, axis=0) - sm[0:1]


def kernel(x, p, norm_mix, w_in, a_out_norm, b_out_norm, w_out, lb_logits, ple_gate_norm, w_ple_gate, w_ple_proj, ple_post_norm, final_norm, loss_target, m_norm_mix, m_w_in, m_a_out_norm, m_b_out_norm, m_w_out, m_lb_logits, m_ple_gate_norm, m_w_ple_gate, m_w_ple_proj, m_ple_post_norm, m_final_norm, v_norm_mix, v_w_in, v_a_out_norm, v_b_out_norm, v_w_out, v_lb_logits, v_ple_gate_norm, v_w_ple_gate, v_w_ple_proj, v_ple_post_norm, v_final_norm):
    t = x.shape[1]
    h0 = x.reshape(t, D_MODEL)
    target = loss_target.reshape(t, D_MODEL)
    pl_in = p.reshape(2, t, D_PLE)

    w_flat_bf = _flatten_shard(_bf(w_in), _bf(w_out), _bf(w_ple_gate), _bf(w_ple_proj))
    chip = 2 * lax.axis_index("x") + lax.axis_index("y")
    gathered = lax.dynamic_update_slice(_gather_weights(w_flat_bf), w_flat_bf[None], (chip, 0, 0))
    w_pps = _full_w_pp(gathered)
    lbs, lbs_vjp = jax.vjp(_lower_bounds, lb_logits)

    saved = []
    h = h0
    for l in range(2):
        g_mix = norm_mix[l].reshape(1, D_MODEL)
        lb = lbs[l].reshape(1, GROUP)
        ga = a_out_norm[l].reshape(1, GROUP)
        gb = b_out_norm[l].reshape(1, GROUP)
        proj, proj_bf = _inproj(h, g_mix, gathered, l)
        ya, states = _hgrn_fwd(proj, lb, ga)
        ob, yb = _sb_fwd(proj_bf, proj, gb)
        h1 = _outproj(h, ya, yb, gathered, l)
        g_post = ple_post_norm[l].reshape(1, D_MODEL)
        g_gate = ple_gate_norm[l].reshape(1, D_MODEL)
        h2 = _ple_fwd(h1, pl_in[l], w_pps[l], gathered, l, g_post, g_gate)
        saved.append((h, proj, proj_bf, states, ya, yb, ob, h1))
        h = h2

    dh, d_final, loss_part = _final(h, final_norm.reshape(1, D_MODEL), target)

    dw_in_l, dw_out_l, dw_pg_l, dw_pp_l = [None] * 2, [None] * 2, [None] * 2, [None] * 2
    d_mix, d_a, d_b, d_lb, d_gate, d_post = [None] * 2, [None] * 2, [None] * 2, [None] * 2, [None] * 2, [None] * 2
    for l in (1, 0):
        h_in, proj, proj_bf, states, ya, yb, ob, h1 = saved[l]
        g_mix = norm_mix[l].reshape(1, D_MODEL)
        lb = lbs[l].reshape(1, GROUP)
        ga = a_out_norm[l].reshape(1, GROUP)
        gb = b_out_norm[l].reshape(1, GROUP)
        g_post = ple_post_norm[l].reshape(1, D_MODEL)
        g_gate = ple_gate_norm[l].reshape(1, D_MODEL)
        dh1, dw_pg_l[l], dw_pp_l[l], d_gate[l], d_post[l] = _ple_bwd(dh, h1, pl_in[l], w_pps[l], gathered, l, g_post,
                                                                     g_gate)
        dya, dyb, dw_out_l[l] = _outproj_bwd(dh1, ya, yb, gathered, l)
        dbq, dbk, dbv, dbg, d_b[l] = _sb_bwd(proj_bf, proj, ob, dyb, gb)
        da, d_lb[l], d_a[l] = _hgrn_bwd(proj, lb, ga, states, dya)
        db = jnp.stack([dbq, dbk, dbv, dbg]).astype(BF16)
        dw_in_l[l] = _inproj_bwd_dw(h_in, g_mix, da, db)
        dh, d_mix[l] = _inproj_bwd_dx(dh1, h_in, g_mix, gathered, l, da, db)
    grad_x = dh.reshape(x.shape)

    g_full = _shard_major(dw_in_l, dw_out_l, dw_pg_l, dw_pp_l)
    chip_sum, chip_sum_bf = _add_my_half(g_full, _swap_halves(g_full))
    mine_half = _sum_scattered(chip_sum, _scatter_shards(chip_sum_bf))
    other_half = _swap_reduced(mine_half)
    south = lax.axis_index("c") == 0
    g_flat = jnp.concatenate([jnp.where(south, mine_half, other_half), jnp.where(south, other_half, mine_half)])
    g_w_in, g_w_out, g_w_pg, g_w_pp = _unflatten_shard(g_flat)

    small = jnp.concatenate([
        jnp.concatenate(d_mix).reshape(-1, 128), jnp.concatenate(d_a).reshape(-1, 128),
        jnp.concatenate(d_b).reshape(-1, 128), jnp.concatenate(d_lb).reshape(-1, 128),
        jnp.concatenate(d_gate).reshape(-1, 128), jnp.concatenate(d_post).reshape(-1, 128),
        d_final.reshape(-1, 128), jnp.broadcast_to(loss_part, (8, 128))], axis=0)
    small = _allreduce_small(small)
    loss = small[80, 0]
    g_norm_mix = small[0:16].reshape(2, D_MODEL)
    g_a = small[16:24].reshape(2, GROUP)
    g_b = small[24:32].reshape(2, GROUP)
    (g_lb,) = lbs_vjp(small[32:40].reshape(2, GROUP))
    g_gate = small[40:56].reshape(2, D_MODEL)
    g_post = small[56:72].reshape(2, D_MODEL)
    g_final = small[72:80].reshape(D_MODEL)

    def adam_matrix(w, g, m, v):
        d, nm, nv = _adamw(w.reshape(-1, D_MODEL), g.reshape(-1, D_MODEL), m.reshape(-1, D_MODEL), v.reshape(-1, D_MODEL))
        return d.reshape(w.shape), nm.reshape(w.shape), nv.reshape(w.shape)

    d_w_in, nm_w_in, nv_w_in = adam_matrix(w_in, g_w_in, m_w_in, v_w_in)
    d_w_out, nm_w_out, nv_w_out = adam_matrix(w_out, g_w_out, m_w_out, v_w_out)
    d_w_pg, nm_w_pg, nv_w_pg = adam_matrix(w_ple_gate, g_w_pg, m_w_ple_gate, v_w_ple_gate)
    d_w_pp, nm_w_pp, nv_w_pp = adam_matrix(w_ple_proj, g_w_pp, m_w_ple_proj, v_w_ple_proj)

    small_w = [norm_mix, a_out_norm, b_out_norm, lb_logits, ple_gate_norm, ple_post_norm, final_norm]
    small_g = [g_norm_mix, g_a, g_b, g_lb, g_gate, g_post, g_final]
    small_m = [m_norm_mix, m_a_out_norm, m_b_out_norm, m_lb_logits, m_ple_gate_norm, m_ple_post_norm, m_final_norm]
    small_v = [v_norm_mix, v_a_out_norm, v_b_out_norm, v_lb_logits, v_ple_gate_norm, v_ple_post_norm, v_final_norm]
    pack = lambda arrs: jnp.concatenate([a.reshape(-1, 128) for a in arrs], axis=0)
    ds, nms, nvs = _adamw(pack(small_w), pack(small_g), pack(small_m), pack(small_v))

    def unpack(packed):
        out, r = [], 0
        for a in small_w:
            n = a.size // 128
            out.append(packed[r:r + n].reshape(a.shape))
            r += n
        return out

    d_s, nm_s, nv_s = unpack(ds), unpack(nms), unpack(nvs)

    def ordered(s, big):
        return [s[0], big[0], s[1], s[2], big[1], s[3], s[4], big[2], big[3], s[5], s[6]]

    grads = ordered(small_g, [g_w_in, g_w_out, g_w_pg, g_w_pp])
    deltas = ordered(d_s, [d_w_in, d_w_out, d_w_pg, d_w_pp])
    new_m = ordered(nm_s, [nm_w_in, nm_w_out, nm_w_pg, nm_w_pp])
    new_v = ordered(nv_s, [nv_w_in, nv_w_out, nv_w_pg, nv_w_pp])
    return (loss, grad_x, *grads, *deltas, *new_m, *new_v)
```

```python
import functools
import math

import numpy as np
import jax
import jax.numpy as jnp
from jax import lax
from jax.experimental import pallas as pl
from jax.experimental.pallas import tpu as pltpu

F32 = jnp.float32
BF16 = jnp.bfloat16
MESH = pl.DeviceIdType.MESH

D_MODEL = 1024
D_PLE = 256
D_IN = 4096
A_HEADS, A_D = 4, 128
B_HEADS, B_D = 8, 64
GROUP = 512
EPS = 1e-6
N_SHARD = 4
N_DEV = 8

HG_CHUNK = 128
HG_LEVELS = 7
SB_TQ = 1024
SB_TK = 128
LOG2E = 1.4426950408889634
LN2 = 0.6931471805599453

ADAM_LR, ADAM_B1, ADAM_B2, ADAM_EPS, ADAM_WD, ADAM_STEP = 0.001, 0.9, 0.999, 1e-08, 0.01, 10

VMEM_LIMIT = 48 * 1024 * 1024
VMEM_LIMIT_BIG = 58 * 1024 * 1024

ROWS_W_IN = 2 * D_MODEL
ROWS_W_OUT = 2 * (D_MODEL // N_SHARD)
ROWS_W_PG = 2 * (D_MODEL // N_SHARD)
ROWS_W_PP = 2 * (D_PLE * (D_MODEL // N_SHARD) // D_MODEL)
ROWS_FLAT = ROWS_W_IN + ROWS_W_OUT + ROWS_W_PG + ROWS_W_PP
HALF_FLAT = ROWS_FLAT // 2
N_CHUNK = 10
CHUNK_ROWS = HALF_FLAT // N_CHUNK

SMALL_ROWS = 88


def _sds(shape, dtype=F32):
    return jax.ShapeDtypeStruct(shape, dtype)


def _params(sem=None, vmem_limit=VMEM_LIMIT):
    kw = dict(vmem_limit_bytes=vmem_limit)
    if sem is not None:
        kw["dimension_semantics"] = sem
    return pltpu.CompilerParams(**kw)


def _dot(a, b, precision=None):
    return lax.dot_general(a, b, (((1,), (0,)), ((), ())), preferred_element_type=F32, precision=precision)


def _dot_nt(a, b, precision=None):
    return lax.dot_general(a, b, (((1,), (1,)), ((), ())), preferred_element_type=F32, precision=precision)


def _dot_tn(a, b, precision=None):
    return lax.dot_general(a, b, (((0,), (0,)), ((), ())), preferred_element_type=F32, precision=precision)


def _bf(x):
    return x.astype(BF16)


def _split(x):
    hi = x.astype(BF16)
    lo = (x - hi.astype(F32)).astype(BF16)
    return hi, lo


def _rms(x):
    r = lax.rsqrt(jnp.mean(x * x, axis=-1, keepdims=True) + EPS)
    return x * r, r


def _rms_bwd(dxh, xh, r):
    return r * (dxh - xh * jnp.mean(dxh * xh, axis=-1, keepdims=True))


def _sigmoid(x):
    return 1.0 / (1.0 + jnp.exp(-x))


def _silu_grad(x, sig):
    return sig * (1.0 + x * (1.0 - sig))


def _row_tile(t, want):
    return min(t, want)


def _inproj(h, g, gathered, layer):
    t = h.shape[0]
    tm = _row_tile(t, 512)
    tn = D_MODEL
    scale = jnp.ones((1, D_IN), F32).at[:, 4 * GROUP:5 * GROUP].set(B_D ** -0.5 * LOG2E)

    def body(h_ref, g_ref, w_ref, s_ref, o_ref, ob_ref):
        xh, _ = _rms(h_ref[...])
        acc = _dot(_bf(xh * g_ref[...]), w_ref[...])
        o_ref[...] = acc
        ob_ref[...] = _bf(acc * s_ref[...])

    return pl.pallas_call(
        body, name="inproj", grid=(D_IN // tn, t // tm),
        in_specs=[pl.BlockSpec((tm, D_MODEL), lambda j, i: (i, 0)),
                  pl.BlockSpec((1, D_MODEL), lambda j, i: (0, 0)),
                  pl.BlockSpec((None, D_MODEL, tn), lambda j, i: (j, layer, 0)),
                  pl.BlockSpec((1, tn), lambda j, i: (0, j))],
        out_specs=[pl.BlockSpec((tm, tn), lambda j, i: (i, j)), pl.BlockSpec((tm, tn), lambda j, i: (i, j))],
        out_shape=[_sds((t, D_IN)), _sds((t, D_IN), BF16)], compiler_params=_params(("arbitrary", "arbitrary")),
    )(h, g, gathered, scale)


def _rows_spec(first_row):
    q = D_MODEL // N_SHARD
    return pl.BlockSpec((N_SHARD, q, D_MODEL), lambda i: (0, first_row // q, 0))


def _outproj(h, ya, yb, gathered, layer):
    t = h.shape[0]
    tm = _row_tile(t, 512)

    def body(h_ref, ya_ref, yb_ref, w_ref, o_ref):
        o_ref[...] = (h_ref[...] + _dot(_bf(ya_ref[...]), w_ref[0:2].reshape(GROUP, D_MODEL))
                      + _dot(_bf(yb_ref[...]), w_ref[2:4].reshape(GROUP, D_MODEL)))

    return pl.pallas_call(
        body, name="outproj", grid=(t // tm,),
        in_specs=[pl.BlockSpec((tm, D_MODEL), lambda i: (i, 0)),
                  pl.BlockSpec((tm, GROUP), lambda i: (i, 0)),
                  pl.BlockSpec((tm, GROUP), lambda i: (i, 0)),
                  _rows_spec(ROWS_W_IN + layer * (D_MODEL // N_SHARD))],
        out_specs=pl.BlockSpec((tm, D_MODEL), lambda i: (i, 0)),
        out_shape=_sds((t, D_MODEL)), compiler_params=_params(("arbitrary",)),
    )(h, ya, yb, gathered)


def _ple_fwd(h, p, w_pp, gathered, layer, g_post, g_gate):
    t = h.shape[0]
    tm = _row_tile(t, 256)

    def body(h_ref, p_ref, wpp_ref, wpg_ref, gp_ref, gg_ref, o_ref):
        x = h_ref[...]
        ph, _ = _rms(_dot(_bf(p_ref[...]), wpp_ref[...]))
        xh, _ = _rms(x)
        gate = _sigmoid(_dot(_bf(xh * gg_ref[...]), wpg_ref[...].reshape(D_MODEL, D_MODEL)))
        o_ref[...] = x + gate * (ph * gp_ref[...])

    return pl.pallas_call(
        body, name="ple_fwd", grid=(t // tm,),
        in_specs=[pl.BlockSpec((tm, D_MODEL), lambda i: (i, 0)),
                  pl.BlockSpec((tm, D_PLE), lambda i: (i, 0)),
                  pl.BlockSpec((D_PLE, D_MODEL), lambda i: (0, 0)),
                  _rows_spec(ROWS_W_IN + ROWS_W_OUT + layer * (D_MODEL // N_SHARD)),
                  pl.BlockSpec((1, D_MODEL), lambda i: (0, 0)),
                  pl.BlockSpec((1, D_MODEL), lambda i: (0, 0))],
        out_specs=pl.BlockSpec((tm, D_MODEL), lambda i: (i, 0)),
        out_shape=_sds((t, D_MODEL)), compiler_params=_params(("arbitrary",)),
    )(h, p, w_pp, gathered, g_post, g_gate)


def _ple_bwd(dh2, h, p, w_pp, gathered, layer, g_post, g_gate):
    t = h.shape[0]
    tm = _row_tile(t, 256)

    def body(d_ref, h_ref, p_ref, wpp_ref, wpg_ref, gp_ref, gg_ref, dh_ref, dwpg_ref, dwpp_ref, dgg_ref, dgp_ref):
        @pl.when(pl.program_id(0) == 0)
        def _():
            dwpg_ref[...] = jnp.zeros_like(dwpg_ref)
            dwpp_ref[...] = jnp.zeros_like(dwpp_ref)
            dgg_ref[...] = jnp.zeros_like(dgg_ref)
            dgp_ref[...] = jnp.zeros_like(dgp_ref)

        d = d_ref[...]
        x = h_ref[...]
        gp = gp_ref[...]
        gg = gg_ref[...]
        pb = _bf(p_ref[...])
        ph, rp = _rms(_dot(pb, wpp_ref[...]))
        pe = ph * gp
        xh, rx = _rms(x)
        un = _bf(xh * gg)
        wpg = wpg_ref[...].reshape(D_MODEL, D_MODEL)
        gate = _sigmoid(_dot(un, wpg))
        dgpre = _bf(d * pe * gate * (1.0 - gate))
        dun = _dot_nt(dgpre, wpg)
        dh_ref[...] = d + _rms_bwd(dun * gg, xh, rx)
        dgg_ref[...] += jnp.sum(dun * xh, axis=0, keepdims=True)
        dwpg_ref[...] += _dot_tn(un, dgpre)
        dpe = d * gate
        dgp_ref[...] += jnp.sum(dpe * ph, axis=0, keepdims=True)
        dwpp_ref[...] += _dot_tn(pb, _bf(_rms_bwd(dpe * gp, ph, rp)))

    return pl.pallas_call(
        body, name="ple_bwd", grid=(t // tm,),
        in_specs=[pl.BlockSpec((tm, D_MODEL), lambda i: (i, 0)),
                  pl.BlockSpec((tm, D_MODEL), lambda i: (i, 0)),
                  pl.BlockSpec((tm, D_PLE), lambda i: (i, 0)),
                  pl.BlockSpec((D_PLE, D_MODEL), lambda i: (0, 0)),
                  _rows_spec(ROWS_W_IN + ROWS_W_OUT + layer * (D_MODEL // N_SHARD)),
                  pl.BlockSpec((1, D_MODEL), lambda i: (0, 0)),
                  pl.BlockSpec((1, D_MODEL), lambda i: (0, 0))],
        out_specs=[pl.BlockSpec((tm, D_MODEL), lambda i: (i, 0)),
                   pl.BlockSpec((D_MODEL, D_MODEL), lambda i: (0, 0)),
                   pl.BlockSpec((D_PLE, D_MODEL), lambda i: (0, 0)),
                   pl.BlockSpec((1, D_MODEL), lambda i: (0, 0)),
                   pl.BlockSpec((1, D_MODEL), lambda i: (0, 0))],
        out_shape=[_sds((t, D_MODEL)), _sds((D_MODEL, D_MODEL)), _sds((D_PLE, D_MODEL)),
                   _sds((1, D_MODEL)), _sds((1, D_MODEL))],
        compiler_params=_params(("arbitrary",)),
    )(dh2, h, p, w_pp, gathered, g_post, g_gate)


def _outproj_bwd(dh, ya, yb, gathered, layer):
    t = dh.shape[0]
    tm = _row_tile(t, 512)

    def body(d_ref, ya_ref, yb_ref, w_ref, dya_ref, dyb_ref, dw_ref):
        @pl.when(pl.program_id(0) == 0)
        def _():
            dw_ref[...] = jnp.zeros_like(dw_ref)

        d = _bf(d_ref[...])
        dya_ref[...] = _dot_nt(d, w_ref[0:2].reshape(GROUP, D_MODEL))
        dyb_ref[...] = _dot_nt(d, w_ref[2:4].reshape(GROUP, D_MODEL))
        dw_ref[pl.ds(0, GROUP), :] += _dot_tn(_bf(ya_ref[...]), d)
        dw_ref[pl.ds(GROUP, GROUP), :] += _dot_tn(_bf(yb_ref[...]), d)

    return pl.pallas_call(
        body, name="outproj_bwd", grid=(t // tm,),
        in_specs=[pl.BlockSpec((tm, D_MODEL), lambda i: (i, 0)),
                  pl.BlockSpec((tm, GROUP), lambda i: (i, 0)),
                  pl.BlockSpec((tm, GROUP), lambda i: (i, 0)),
                  _rows_spec(ROWS_W_IN + layer * (D_MODEL // N_SHARD))],
        out_specs=[pl.BlockSpec((tm, GROUP), lambda i: (i, 0)),
                   pl.BlockSpec((tm, GROUP), lambda i: (i, 0)),
                   pl.BlockSpec((D_MODEL, D_MODEL), lambda i: (0, 0))],
        out_shape=[_sds((t, GROUP)), _sds((t, GROUP)), _sds((D_MODEL, D_MODEL))],
        compiler_params=_params(("arbitrary",)),
    )(dh, ya, yb, gathered)


def _inproj_bwd_dx(dres, h, g, gathered, layer, da, db):
    t = h.shape[0]
    tm = _row_tile(t, 256)

    def body(dres_ref, h_ref, g_ref, w_ref, da_ref, db_ref, dh_ref, dg_ref):
        @pl.when(pl.program_id(0) == 0)
        def _():
            dg_ref[...] = jnp.zeros_like(dg_ref)

        du = jnp.zeros((tm, D_MODEL), F32)
        for i in range(8):
            part = da_ref[i] if i < 4 else db_ref[i - 4]
            du = du + _dot_nt(part, w_ref[i // 2, :, pl.ds((i % 2) * GROUP, GROUP)])
        xh, r = _rms(h_ref[...])
        dg_ref[...] += jnp.sum(du * xh, axis=0, keepdims=True)
        dh_ref[...] = dres_ref[...] + _rms_bwd(du * g_ref[...], xh, r)

    return pl.pallas_call(
        body, name="inproj_bwd_dx", grid=(t // tm,),
        in_specs=[pl.BlockSpec((tm, D_MODEL), lambda i: (i, 0)),
                  pl.BlockSpec((tm, D_MODEL), lambda i: (i, 0)),
                  pl.BlockSpec((1, D_MODEL), lambda i: (0, 0)),
                  pl.BlockSpec((N_SHARD, D_MODEL, D_MODEL), lambda i: (0, layer, 0)),
                  pl.BlockSpec((4, tm, GROUP), lambda i: (0, i, 0)),
                  pl.BlockSpec((4, tm, GROUP), lambda i: (0, i, 0))],
        out_specs=[pl.BlockSpec((tm, D_MODEL), lambda i: (i, 0)),
                   pl.BlockSpec((1, D_MODEL), lambda i: (0, 0))],
        out_shape=[_sds((t, D_MODEL)), _sds((1, D_MODEL))],
        compiler_params=_params(("arbitrary",)),
    )(dres, h, g, gathered, da, db)


def _inproj_bwd_dw(h, g, da, db):
    t = h.shape[0]
    tm = _row_tile(t, 512)

    def body(h_ref, g_ref, da_ref, db_ref, dw_ref):
        @pl.when(pl.program_id(0) == 0)
        def _():
            dw_ref[...] = jnp.zeros_like(dw_ref)

        xh, _ = _rms(h_ref[...])
        u = _bf(xh * g_ref[...])
        for i in range(8):
            dw_ref[i // 2, :, pl.ds((i % 2) * GROUP, GROUP)] += _dot_tn(u, da_ref[i] if i < 4 else db_ref[i - 4])

    return pl.pallas_call(
        body, name="inproj_bwd_dw", grid=(t // tm,),
        in_specs=[pl.BlockSpec((tm, D_MODEL), lambda i: (i, 0)),
                  pl.BlockSpec((1, D_MODEL), lambda i: (0, 0)),
                  pl.BlockSpec((4, tm, GROUP), lambda i: (0, i, 0)),
                  pl.BlockSpec((4, tm, GROUP), lambda i: (0, i, 0))],
        out_specs=pl.BlockSpec((N_SHARD, D_MODEL, D_MODEL), lambda i: (0, 0, 0)),
        out_shape=_sds((N_SHARD, D_MODEL, D_MODEL)), compiler_params=_params(("arbitrary",), VMEM_LIMIT_BIG),
    )(h, g, da, db)


def _final(h, g, target):
    t = h.shape[0]
    tm = _row_tile(t, 512)

    def body(h_ref, g_ref, t_ref, dh_ref, dg_ref, loss_ref):
        @pl.when(pl.program_id(0) == 0)
        def _():
            dg_ref[...] = jnp.zeros_like(dg_ref)
            loss_ref[...] = jnp.zeros_like(loss_ref)

        xh, r = _rms(h_ref[...])
        gg = g_ref[...]
        err = xh * gg - t_ref[...]
        part = 0.5 * jnp.sum(jnp.mean(err * err, axis=-1, keepdims=True), axis=0, keepdims=True)
        loss_ref[...] += jnp.broadcast_to(part, loss_ref.shape)
        dy = err * (1.0 / D_MODEL)
        dg_ref[...] += jnp.sum(dy * xh, axis=0, keepdims=True)
        dh_ref[...] = _rms_bwd(dy * gg, xh, r)

    return pl.pallas_call(
        body, name="final", grid=(t // tm,),
        in_specs=[pl.BlockSpec((tm, D_MODEL), lambda i: (i, 0)),
                  pl.BlockSpec((1, D_MODEL), lambda i: (0, 0)),
                  pl.BlockSpec((tm, D_MODEL), lambda i: (i, 0))],
        out_specs=[pl.BlockSpec((tm, D_MODEL), lambda i: (i, 0)),
                   pl.BlockSpec((1, D_MODEL), lambda i: (0, 0)),
                   pl.BlockSpec((1, 128), lambda i: (0, 0))],
        out_shape=[_sds((t, D_MODEL)), _sds((1, D_MODEL)), _sds((1, 128))],
        compiler_params=_params(("arbitrary",)),
    )(h, g, target)


def _hgrn_consts():
    c, nl = HG_CHUNK, HG_LEVELS
    t = np.arange(c)
    tril = np.tril(np.ones((c, c), np.float32))
    masks = np.zeros((nl + 1, c, c), np.float32)
    masks[0] = np.eye(c, dtype=np.float32)
    dmat = np.zeros(((nl + 2) * c, c), np.float32)
    dmat[0:c] = tril
    for l in range(nl):
        m = c >> (l + 1)
        blk = t // (2 * m)
        r = blk * 2 * m + m - 1
        upper = (t % (2 * m)) >= m
        masks[l + 1] = ((blk[:, None] == blk[None, :]) & upper[:, None] & (~upper)[None, :]).astype(np.float32)
        dmat[(l + 1) * c:(l + 2) * c] = tril[t] - tril[r]
    dmat[(nl + 1) * c:] = np.triu(np.ones((c, c), np.float32), k=1)
    return jnp.asarray(masks), jnp.asarray(dmat, BF16)


HG_HEADS = 4


def _hgrn_pre(aq, af, lb):
    sq = _sigmoid(aq)
    sneg = _sigmoid(-af)
    kk = (1.0 - lb) * sneg
    return sq, aq * sq, sneg, kk, jnp.log1p(-kk)


def _hgrn_x(logf, dmat_ref):
    dm = dmat_ref[pl.ds(0, (HG_LEVELS + 1) * HG_CHUNK), :]
    lhi, llo = _split(logf)
    return _dot(dm, lhi) + _dot(dm, llo)


def _hgrn_level(x_all, l, q, kk):
    c = HG_CHUNK
    x = x_all[(l + 1) * c:(l + 2) * c]
    qf = jnp.exp(jnp.minimum(x, 0.0))
    kf = jnp.exp(-jnp.maximum(x, 0.0))
    return qf, kf, _bf(q * qf), _bf(kk * kf)


def _hgrn_scores(xs, qs, kks, mask_ref):
    ps = [mask_ref[0] * _dot_nt(_bf(q), _bf(kk)) for q, kk in zip(qs, kks)]
    for l in range(HG_LEVELS):
        for i, (x_all, q, kk) in enumerate(zip(xs, qs, kks)):
            _, _, ql, kl = _hgrn_level(x_all, l, q, kk)
            ps[i] = ps[i] + mask_ref[l + 1] * _dot_nt(ql, kl)
    return ps


def _hgrn_specs(n_chunks, rev):
    c, w = HG_CHUNK, HG_HEADS * A_D
    cidx = (lambda n: n_chunks - 1 - n) if rev else (lambda n: n)
    col = lambda g: pl.BlockSpec((c, w), lambda h, n: (cidx(n), g * (A_HEADS // HG_HEADS) + h))
    vec = pl.BlockSpec((1, w), lambda h, n: (0, h))
    mask = pl.BlockSpec((HG_LEVELS + 1, c, c), lambda h, n: (0, 0, 0))
    dmat = pl.BlockSpec(((HG_LEVELS + 2) * c, c), lambda h, n: (0, 0))
    state = pl.BlockSpec((HG_HEADS, None, A_D, A_D), lambda h, n: (h, cidx(n), 0, 0))
    return cidx, col, vec, mask, dmat, state


def _lanes(i):
    return pl.ds(i * A_D, A_D)


def _hgrn_fwd(proj, lb, gain):
    t = proj.shape[0]
    c = HG_CHUNK
    nch = t // c
    masks, dmat = _hgrn_consts()
    cidx, col, vec, mask_spec, dmat_spec, state_spec = _hgrn_specs(nch, False)
    heads = range(HG_HEADS)

    def body(aq_ref, af_ref, ai_ref, ag_ref, lb_ref, gain_ref, mask_ref, dmat_ref, y_ref, st_ref, s_scr):
        @pl.when(pl.program_id(1) == 0)
        def _():
            s_scr[...] = jnp.zeros_like(s_scr)

        pre = [_hgrn_pre(aq_ref[:, _lanes(i)], af_ref[:, _lanes(i)], lb_ref[:, _lanes(i)]) for i in heads]
        qs, kks = [p[1] for p in pre], [p[3] for p in pre]
        xs = [_hgrn_x(p[4], dmat_ref) for p in pre]
        bs = [x[0:c] for x in xs]
        b_lasts = [jnp.sum(p[4], axis=0, keepdims=True) for p in pre]
        ps = _hgrn_scores(xs, qs, kks, mask_ref)
        ss = [s_scr[i] for i in heads]
        vbs = [_bf(ai_ref[:, _lanes(i)]) for i in heads]
        os_ = [_dot(_bf(ps[i]), vbs[i]) + _dot_nt(_bf(qs[i] * jnp.exp(bs[i])), _bf(ss[i])) for i in heads]
        for i in heads:
            st_ref[i] = ss[i]
            s_scr[i] = ss[i] * jnp.exp(b_lasts[i]) + _dot_tn(vbs[i], _bf(kks[i] * jnp.exp(b_lasts[i] - bs[i])))
            oh, _ = _rms(os_[i])
            ag = ag_ref[:, _lanes(i)]
            y_ref[:, _lanes(i)] = oh * gain_ref[:, _lanes(i)] * (ag * _sigmoid(ag))

    return pl.pallas_call(
        body, name="hgrn_fwd", grid=(A_HEADS // HG_HEADS, nch),
        in_specs=[col(0), col(1), col(2), col(3), vec, vec, mask_spec, dmat_spec],
        out_specs=[pl.BlockSpec((c, HG_HEADS * A_D), lambda h, n: (n, h)), state_spec],
        out_shape=[_sds((t, GROUP)), _sds((A_HEADS, nch, A_D, A_D))],
        scratch_shapes=[pltpu.VMEM((HG_HEADS, A_D, A_D), F32)],
        compiler_params=_params(("arbitrary", "arbitrary")),
    )(proj, proj, proj, proj, lb, gain, masks, dmat)


def _hgrn_bwd(proj, lb, gain, states, dya):
    t = proj.shape[0]
    c, nl = HG_CHUNK, HG_LEVELS
    nch = t // c
    masks, dmat = _hgrn_consts()
    cidx, col, vec, mask_spec, dmat_spec, state_spec = _hgrn_specs(nch, True)
    heads = range(HG_HEADS)

    def body(aq_ref, af_ref, ai_ref, ag_ref, lb_ref, gain_ref, mask_ref, dmat_ref, st_ref, dy_ref,
             da_ref, dlb_ref, dgain_ref, ds_scr, z_scr):
        @pl.when(pl.program_id(1) == 0)
        def _():
            ds_scr[...] = jnp.zeros_like(ds_scr)
            dlb_ref[...] = jnp.zeros_like(dlb_ref)
            dgain_ref[...] = jnp.zeros_like(dgain_ref)

        aqs = [aq_ref[:, _lanes(i)] for i in heads]
        lbs = [lb_ref[:, _lanes(i)] for i in heads]
        pre = [_hgrn_pre(aqs[i], af_ref[:, _lanes(i)], lbs[i]) for i in heads]
        sqs, qs, snegs, kks = ([p[j] for p in pre] for j in range(4))
        xs = [_hgrn_x(p[4], dmat_ref) for p in pre]
        bs = [x[0:c] for x in xs]
        b_lasts = [jnp.sum(p[4], axis=0, keepdims=True) for p in pre]
        ebs = [jnp.exp(b) for b in bs]
        ebls = [jnp.exp(bl - b) for bl, b in zip(b_lasts, bs)]
        ebl_rows = [jnp.exp(bl) for bl in b_lasts]
        qes = [_bf(q * eb) for q, eb in zip(qs, ebs)]
        kes = [_bf(kk * ebl) for kk, ebl in zip(kks, ebls)]
        vbs = [_bf(ai_ref[:, _lanes(i)]) for i in heads]
        ss = [st_ref[i] for i in heads]
        sbs = [_bf(s) for s in ss]
        dss = [ds_scr[i] for i in heads]
        dsbs = [_bf(ds) for ds in dss]

        pbs = [_bf(p) for p in _hgrn_scores(xs, qs, kks, mask_ref)]
        os_ = [_dot(pbs[i], vbs[i]) + _dot_nt(qes[i], sbs[i]) for i in heads]

        dos = []
        for i in heads:
            ag, gain, dy = ag_ref[:, _lanes(i)], gain_ref[:, _lanes(i)], dy_ref[:, _lanes(i)]
            oh, r = _rms(os_[i])
            sg_sig = _sigmoid(ag)
            sg = ag * sg_sig
            da_ref[3, :, _lanes(i)] = _bf(dy * oh * gain * _silu_grad(ag, sg_sig))
            dgain_ref[:, _lanes(i)] += jnp.sum(dy * oh * sg, axis=0, keepdims=True)
            dos.append(_bf(_rms_bwd(dy * gain * sg, oh, r)))

        dps = [_dot_nt(dos[i], vbs[i]) for i in heads]
        for i in heads:
            da_ref[2, :, _lanes(i)] = _bf(_dot_tn(pbs[i], dos[i]) + _dot_nt(kes[i], dsbs[i]))
        dq_ss = [ebs[i] * _dot(dos[i], sbs[i]) for i in heads]
        dk_ss = [ebls[i] * _dot(vbs[i], dsbs[i]) for i in heads]
        dqs, dks = [], []
        for i in heads:
            dpd = jnp.sum(mask_ref[0] * dps[i], axis=1, keepdims=True)
            z_scr[i, pl.ds(0, c), :] = qs[i] * dq_ss[i]
            z_scr[i, pl.ds((nl + 1) * c, c), :] = kks[i] * dk_ss[i]
            dqs.append(dq_ss[i] + dpd * kks[i])
            dks.append(dk_ss[i] + dpd * qs[i])
        for l in range(nl):
            for i in heads:
                qf, kf, ql, kl = _hgrn_level(xs[i], l, qs[i], kks[i])
                dpl = _bf(mask_ref[l + 1] * dps[i])
                dq_l = qf * _dot(dpl, kl)
                dk_l = kf * _dot_tn(dpl, ql)
                z_scr[i, pl.ds((l + 1) * c, c), :] = qs[i] * dq_l - kks[i] * dk_l
                dqs[i] = dqs[i] + dq_l
                dks[i] = dks[i] + dk_l

        zsplits = [_split(z_scr[i]) for i in heads]
        dlogfs = [_dot_tn(dmat_ref[...], zhi) + _dot_tn(dmat_ref[...], zlo) for zhi, zlo in zsplits]
        ds_new = [_dot_tn(dos[i], qes[i]) for i in heads]
        for i in heads:
            dlogf = dlogfs[i] + ebl_rows[i] * jnp.sum(dss[i] * ss[i], axis=0, keepdims=True)
            dkk = dks[i] - dlogf / (1.0 - kks[i])
            da_ref[1, :, _lanes(i)] = _bf(dkk * (1.0 - lbs[i]) * (-(snegs[i] * (1.0 - snegs[i]))))
            dlb_ref[:, _lanes(i)] += jnp.sum(dkk * (-snegs[i]), axis=0, keepdims=True)
            da_ref[0, :, _lanes(i)] = _bf(dqs[i] * _silu_grad(aqs[i], sqs[i]))
            ds_scr[i] = dss[i] * ebl_rows[i] + ds_new[i]

    w = HG_HEADS * A_D
    return pl.pallas_call(
        body, name="hgrn_bwd", grid=(A_HEADS // HG_HEADS, nch),
        in_specs=[col(0), col(1), col(2), col(3), vec, vec, mask_spec, dmat_spec, state_spec,
                  pl.BlockSpec((c, w), lambda h, n: (cidx(n), h))],
        out_specs=[pl.BlockSpec((4, c, w), lambda h, n: (0, cidx(n), h)), vec, vec],
        out_shape=[_sds((4, t, GROUP), BF16)] + [_sds((1, GROUP))] * 2,
        scratch_shapes=[pltpu.VMEM((HG_HEADS, A_D, A_D), F32), pltpu.VMEM((HG_HEADS, (nl + 2) * c, A_D), F32)],
        compiler_params=_params(("arbitrary", "arbitrary")),
    )(proj, proj, proj, proj, lb, gain, masks, dmat, states, dya)


def _sb_consts():
    j = np.arange(SB_TK)
    strict = (j[:, None] > j[None, :]).astype(np.float32)
    incl = (j[:, None] >= j[None, :]).astype(np.float32)
    return jnp.asarray(strict, BF16), jnp.asarray(incl, BF16)


def _lane0(x):
    return jnp.broadcast_to(x[:, 0:1], x.shape)


def _sb_softplus(z, masked):
    logsig = jnp.minimum(z, 0.0) - jnp.log2(1.0 + jnp.exp2(-jnp.abs(z)))
    sp = z - logsig
    mask = None
    if masked:
        mask = lax.broadcasted_iota(jnp.int32, z.shape, 1) < lax.broadcasted_iota(jnp.int32, z.shape, 0)
        sp = jnp.where(mask, sp, 0.0)
    return mask, sp, logsig


def _sb_cumsum(sp, cmat):
    cs = _dot(_bf(sp), cmat)
    return cs, _lane0(cs + sp)


def _sb_sweep(qi, group_fn, state):
    nd = SB_TQ // SB_TK
    state = group_fn([(pl.multiple_of((qi * nd + d) * SB_TK, SB_TK), d * SB_TK, True) for d in reversed(range(nd))],
                     state)

    def step(j, st):
        return group_fn([(pl.multiple_of(((qi - j) * nd - 1 - g) * SB_TK, SB_TK), 0, False) for g in range(nd)], st)

    return lax.fori_loop(0, qi, step, state)


def _set_rows(r0, full, new):
    return new if r0 == 0 else jnp.concatenate([full[:r0], new], axis=0)


def _sb_specs(t, tq):
    col = lambda g: pl.BlockSpec((tq, 2 * B_D), lambda p, i, h: (i, g * (GROUP // (2 * B_D)) + p))
    full = lambda g: pl.BlockSpec((t, 2 * B_D), lambda p, i, h: (0, g * (GROUP // (2 * B_D)) + p))
    vec = pl.BlockSpec((1, 2 * B_D), lambda p, i, h: (0, p))
    mat = pl.BlockSpec((SB_TK, SB_TK), lambda p, i, h: (0, 0))
    return col, full, vec, mat


def _head_lanes(h):
    return (lax.broadcasted_iota(jnp.int32, (1, 2 * B_D), 1) >= B_D) == (h == 1)


def _put(ref, h, val):
    @pl.when(h == 0)
    def _():
        ref[...] = val

    @pl.when(h == 1)
    def _():
        ref[...] += val


def _sb_fwd(proj_bf, proj, gain):
    t = proj.shape[0]
    tq = SB_TQ
    strict, _ = _sb_consts()

    def body(q_ref, k_ref, v_ref, bg_ref, gain_ref, m_ref, o_ref, y_ref):
        h = pl.program_id(2)
        lanes = _head_lanes(h)
        qb = jnp.where(lanes, q_ref[...], jnp.zeros_like(q_ref))
        cmat = m_ref[...]

        def group(tiles, state):
            carry, acc = state
            kv = [(k_ref[pl.ds(off, SB_TK), :], v_ref[pl.ds(off, SB_TK), :]) for off, _, _ in tiles]
            zs = [_dot_nt(qb[r0:], kb) for (_, r0, _), (kb, _) in zip(tiles, kv)]
            sps = [_sb_softplus(z, masked) for z, (_, _, masked) in zip(zs, tiles)]
            css = [_sb_cumsum(sp, cmat) for _, sp, _ in sps]
            ws = []
            for (mask, _, logsig), (cs, tot), (_, r0, masked) in zip(sps, css, tiles):
                w = jnp.exp2(logsig - cs - carry[r0:])
                ws.append(_split(jnp.where(mask, w, 0.0) if masked else w))
                carry = _set_rows(r0, carry, carry[r0:] + tot)
            for (whi, wlo), (_, vb), (_, r0, _) in zip(ws, kv, tiles):
                acc = _set_rows(r0, acc, acc[r0:] + _dot(whi, vb) + _dot(wlo, vb))
            return carry, acc

        _, acc = _sb_sweep(pl.program_id(1), group, (jnp.zeros((tq, SB_TK), F32), jnp.zeros((tq, 2 * B_D), F32)))
        o = jnp.where(lanes, acc, 0.0)
        oh = o * lax.rsqrt(jnp.sum(o * o, axis=-1, keepdims=True) * (1.0 / B_D) + EPS)
        bg = bg_ref[...]
        _put(o_ref, h, o)
        _put(y_ref, h, oh * gain_ref[...] * (bg * _sigmoid(bg)))

    col, full, vec, mat = _sb_specs(t, tq)
    out = pl.BlockSpec((tq, 2 * B_D), lambda p, i, h: (i, p))
    return pl.pallas_call(
        body, name="sb_fwd", grid=(B_HEADS // 2, t // tq, 2),
        in_specs=[col(4), full(5), full(6), col(7), vec, mat],
        out_specs=[out, out],
        out_shape=[_sds((t, GROUP)), _sds((t, GROUP))],
        compiler_params=_params(("arbitrary", "arbitrary", "arbitrary")),
    )(proj_bf, proj_bf, proj_bf, proj, gain, strict)


def _sb_bwd(proj_bf, proj, o, dy, gain):
    t = proj.shape[0]
    tq = SB_TQ
    strict, incl = _sb_consts()

    def body(q_ref, k_ref, v_ref, bg_ref, o_ref, dy_ref, gain_ref, ms_ref, mi_ref,
             dq_ref, dk_ref, dv_ref, dbg_ref, dgain_ref):
        qi = pl.program_id(1)
        h = pl.program_id(2)
        lanes = _head_lanes(h)

        @pl.when((qi == 0) & (h == 0))
        def _():
            dk_ref[...] = jnp.zeros_like(dk_ref)
            dv_ref[...] = jnp.zeros_like(dv_ref)
            dgain_ref[...] = jnp.zeros_like(dgain_ref)

        qb = jnp.where(lanes, q_ref[...], jnp.zeros_like(q_ref))
        cmat = ms_ref[...]
        imat = mi_ref[...]
        o = jnp.where(lanes, o_ref[...], 0.0)
        dy = jnp.where(lanes, dy_ref[...], 0.0)
        bg = bg_ref[...]
        gain = gain_ref[...]
        r = lax.rsqrt(jnp.sum(o * o, axis=-1, keepdims=True) * (1.0 / B_D) + EPS)
        oh = o * r
        sig = _sigmoid(bg)
        sg = bg * sig
        _put(dbg_ref, h, dy * oh * gain * _silu_grad(bg, sig))
        dgain_ref[...] += jnp.sum(dy * oh * sg, axis=0, keepdims=True)
        doh = dy * gain * sg
        do = _bf(r * (doh - oh * (jnp.sum(doh * oh, axis=-1, keepdims=True) * (1.0 / B_D))))
        total = jnp.broadcast_to(jnp.sum(do.astype(F32) * o, axis=1, keepdims=True), (tq, SB_TK))

        def group(tiles, state):
            carry, gcarry, dq = state
            kv = [(k_ref[pl.ds(off, SB_TK), :], v_ref[pl.ds(off, SB_TK), :]) for off, _, _ in tiles]
            zs = [_dot_nt(qb[r0:], kb) for (_, r0, _), (kb, _) in zip(tiles, kv)]
            dws = [_dot_nt(do[r0:], vb) for (_, r0, _), (_, vb) in zip(tiles, kv)]
            sps = [_sb_softplus(z, masked) for z, (_, _, masked) in zip(zs, tiles)]
            css = [_sb_cumsum(sp, cmat) for _, sp, _ in sps]
            ws, gs = [], []
            for (mask, _, logsig), (cs, tot), dw, (_, r0, masked) in zip(sps, css, dws, tiles):
                w = jnp.exp2(logsig - cs - carry[r0:])
                w = jnp.where(mask, w, 0.0) if masked else w
                ws.append(_bf(w))
                gs.append(dw * w)
                carry = _set_rows(r0, carry, carry[r0:] + tot)
            s2s = []
            for g in gs:
                ghi, glo = _split(g)
                s2s.append(_dot(ghi, imat) + _dot(glo, imat))
            dzs = []
            for (mask, _, logsig), g, s2, (_, r0, masked) in zip(sps, gs, s2s, tiles):
                before = total[r0:] - gcarry[r0:] - s2
                dz = g - jnp.exp2(logsig) * (g + before)
                dzs.append(_bf(jnp.where(mask, dz, 0.0) if masked else dz))
                gcarry = _set_rows(r0, gcarry, gcarry[r0:] + _lane0(s2))
            for dz, wb, (kb, _), (off, r0, _) in zip(dzs, ws, kv, tiles):
                dq = _set_rows(r0, dq, dq[r0:] + _dot(dz, kb))
                dk_ref[pl.ds(off, SB_TK), :] += _dot_tn(dz, qb[r0:])
                dv_ref[pl.ds(off, SB_TK), :] += _dot_tn(wb, do[r0:])
            return carry, gcarry, dq

        zero = jnp.zeros((tq, SB_TK), F32)
        _, _, dq = _sb_sweep(qi, group, (zero, zero, jnp.zeros((tq, 2 * B_D), F32)))
        _put(dq_ref, h, jnp.where(lanes, dq * (B_D ** -0.5), 0.0))

    col, full, vec, mat = _sb_specs(t, tq)
    blk = pl.BlockSpec((tq, 2 * B_D), lambda p, i, h: (i, p))
    whole = pl.BlockSpec((t, 2 * B_D), lambda p, i, h: (0, p))
    return pl.pallas_call(
        body, name="sb_bwd", grid=(B_HEADS // 2, t // tq, 2),
        in_specs=[col(4), full(5), full(6), col(7), blk, blk, vec, mat, mat],
        out_specs=[blk, whole, whole, blk, vec],
        out_shape=[_sds((t, GROUP))] * 4 + [_sds((1, GROUP))],
        compiler_params=_params(("arbitrary", "arbitrary", "arbitrary")),
    )(proj_bf, proj_bf, proj_bf, proj, o, dy, gain, strict, incl)


def _adamw(w, g, m, v):
    rows, cols = w.shape
    tr = rows
    for cand in (400, 256, 128, 64, 32, 16, 8):
        if rows % cand == 0:
            tr = cand
            break

    def body(w_ref, g_ref, m_ref, v_ref, d_ref, nm_ref, nv_ref):
        g_ = g_ref[...]
        m_ = ADAM_B1 * m_ref[...] + (1.0 - ADAM_B1) * g_
        v_ = ADAM_B2 * v_ref[...] + (1.0 - ADAM_B2) * (g_ * g_)
        m_hat = m_ / (1.0 - ADAM_B1 ** ADAM_STEP)
        v_hat = v_ / (1.0 - ADAM_B2 ** ADAM_STEP)
        d_ref[...] = -ADAM_LR * (m_hat / (jnp.sqrt(v_hat) + ADAM_EPS) + ADAM_WD * w_ref[...])
        nm_ref[...] = m_
        nv_ref[...] = v_

    spec = pl.BlockSpec((tr, cols), lambda i: (i, 0))
    return pl.pallas_call(
        body, name="adamw", grid=(rows // tr,), in_specs=[spec] * 4, out_specs=[spec] * 3,
        out_shape=[_sds((rows, cols))] * 3, compiler_params=_params(("arbitrary",)),
    )(w, g, m, v)


_ANY = pl.BlockSpec(memory_space=pl.ANY)


def _place():
    return lax.axis_index("x"), lax.axis_index("y"), lax.axis_index("c")


def _gather_weights(flat):
    half, ch, nc = HALF_FLAT, CHUNK_ROWS, N_CHUNK

    def body(x_ref, out_ref, send_sems, recv_sems):
        x, y, c = _place()
        me = 2 * x + y
        sibling = (x, y, 1 - c)

        def rows(shard, hc, r):
            return out_ref.at[shard, pl.ds(hc * half + r * ch, ch), :]

        def copy(k, shard, hc, r, to, src=None):
            return pltpu.make_async_remote_copy(
                src_ref=rows(shard, hc, r) if src is None else src, dst_ref=rows(shard, hc, r),
                send_sem=send_sems.at[k * nc + r], recv_sem=recv_sems.at[k * nc + r], device_id=to, device_id_type=MESH)

        peers = [me ^ k for k in (1, 2, 3)]
        first = [copy(k, me, c, r, (p >> 1, p & 1, c), src=x_ref.at[pl.ds(c * half + r * ch, ch), :])
                 for k, p in enumerate(peers) for r in range(nc)]
        for cp in first:
            cp.start()
        passed = []
        for k, p in enumerate(peers):
            for r in range(nc):
                copy(k, p, c, r, sibling).wait_recv()
                passed.append(copy(3 + k, p, c, r, sibling))
                passed[-1].start()
        for k, p in enumerate(peers):
            for r in range(nc):
                copy(3 + k, p, 1 - c, r, sibling).wait_recv()
        for cp in first + passed:
            cp.wait_send()

    return pl.pallas_call(
        body, name="gather_weights", in_specs=[_ANY], out_specs=_ANY,
        out_shape=_sds((N_SHARD, ROWS_FLAT, D_MODEL), BF16),
        scratch_shapes=[pltpu.SemaphoreType.DMA((6 * nc,)), pltpu.SemaphoreType.DMA((6 * nc,))],
    )(flat)


def _swap_halves(grads):
    half, ch, nc = HALF_FLAT, CHUNK_ROWS, N_CHUNK

    def body(g_ref, out_ref, send_sems, recv_sems):
        x, y, c = _place()
        copies = [pltpu.make_async_remote_copy(
            src_ref=g_ref.at[j, pl.ds((1 - c) * half + r * ch, ch), :], dst_ref=out_ref.at[j, pl.ds(r * ch, ch), :],
            send_sem=send_sems.at[j * nc + r], recv_sem=recv_sems.at[j * nc + r],
            device_id=(x, y, 1 - c), device_id_type=MESH) for j in range(N_SHARD) for r in range(nc)]
        for cp in copies:
            cp.start()
        for cp in copies:
            cp.wait()

    return pl.pallas_call(
        body, name="swap_halves", in_specs=[_ANY], out_specs=_ANY,
        out_shape=_sds((N_SHARD, half, D_MODEL)),
        scratch_shapes=[pltpu.SemaphoreType.DMA((N_SHARD * nc,)), pltpu.SemaphoreType.DMA((N_SHARD * nc,))],
    )(grads)


def _add_my_half(grads, recv):
    tr = 400
    nb = HALF_FLAT // tr
    core = lax.axis_index("c").astype(jnp.int32).reshape(1)

    def body(c_ref, g_ref, r_ref, o_ref, ob_ref):
        acc = g_ref[...] + r_ref[...]
        o_ref[...] = acc
        ob_ref[...] = _bf(acc)

    out = pl.BlockSpec((None, tr, D_MODEL), lambda j, i, c_ref: (j, i, 0))
    return pl.pallas_call(
        body, name="add_my_half",
        grid_spec=pltpu.PrefetchScalarGridSpec(
            num_scalar_prefetch=1, grid=(N_SHARD, nb),
            in_specs=[pl.BlockSpec((None, tr, D_MODEL), lambda j, i, c_ref: (j, c_ref[0] * nb + i, 0)), out],
            out_specs=[out, out]),
        out_shape=[_sds((N_SHARD, HALF_FLAT, D_MODEL)), _sds((N_SHARD, HALF_FLAT, D_MODEL), BF16)],
        compiler_params=_params(("arbitrary", "arbitrary")),
    )(core, grads, recv)


def _scatter_shards(part):
    ch, nc = CHUNK_ROWS, N_CHUNK

    def body(p_ref, out_ref, send_sems, recv_sems):
        x, y, c = _place()
        me = 2 * x + y
        peers = [me ^ k for k in (1, 2, 3)]
        sends = [pltpu.make_async_remote_copy(
            src_ref=p_ref.at[p, pl.ds(r * ch, ch), :], dst_ref=out_ref.at[k, pl.ds(r * ch, ch), :],
            send_sem=send_sems.at[k * nc + r], recv_sem=recv_sems.at[k * nc + r],
            device_id=(p >> 1, p & 1, c), device_id_type=MESH) for k, p in enumerate(peers) for r in range(nc)]
        for cp in sends:
            cp.start()
        for cp in sends:
            cp.wait()

    return pl.pallas_call(
        body, name="scatter_shards", in_specs=[_ANY], out_specs=_ANY,
        out_shape=_sds((3, HALF_FLAT, D_MODEL), BF16),
        scratch_shapes=[pltpu.SemaphoreType.DMA((3 * nc,)), pltpu.SemaphoreType.DMA((3 * nc,))],
    )(part)


def _sum_scattered(part, recv):
    tr = 400
    chip = (2 * lax.axis_index("x") + lax.axis_index("y")).astype(jnp.int32).reshape(1)

    def body(c_ref, p_ref, r_ref, o_ref):
        acc = p_ref[...]
        for k in range(3):
            acc = acc + r_ref[k].astype(F32)
        o_ref[...] = acc

    return pl.pallas_call(
        body, name="sum_scattered",
        grid_spec=pltpu.PrefetchScalarGridSpec(
            num_scalar_prefetch=1, grid=(HALF_FLAT // tr,),
            in_specs=[pl.BlockSpec((None, tr, D_MODEL), lambda i, c_ref: (c_ref[0], i, 0)),
                      pl.BlockSpec((3, tr, D_MODEL), lambda i, c_ref: (0, i, 0))],
            out_specs=pl.BlockSpec((tr, D_MODEL), lambda i, c_ref: (i, 0))),
        out_shape=_sds((HALF_FLAT, D_MODEL)), compiler_params=_params(("arbitrary",)),
    )(chip, part, recv)


def _swap_reduced(mine_half):
    ch, nc = CHUNK_ROWS, N_CHUNK

    def body(r_ref, out_ref, send_sems, recv_sems):
        x, y, c = _place()
        copies = [pltpu.make_async_remote_copy(
            src_ref=r_ref.at[pl.ds(r * ch, ch), :], dst_ref=out_ref.at[pl.ds(r * ch, ch), :],
            send_sem=send_sems.at[r], recv_sem=recv_sems.at[r], device_id=(x, y, 1 - c), device_id_type=MESH)
            for r in range(nc)]
        for cp in copies:
            cp.start()
        for cp in copies:
            cp.wait()

    return pl.pallas_call(
        body, name="swap_reduced", in_specs=[_ANY], out_specs=_ANY,
        out_shape=_sds((HALF_FLAT, D_MODEL)),
        scratch_shapes=[pltpu.SemaphoreType.DMA((nc,)), pltpu.SemaphoreType.DMA((nc,))],
    )(mine_half)


def _allreduce_small(vec):
    def body(v_ref, out_ref, buf, send_sems, recv_sems):
        x, y, c = _place()
        me = 4 * x + 2 * y + c
        buf[me] = v_ref[...]
        peers = [me ^ k for k in range(1, N_DEV)]
        sends = [pltpu.make_async_remote_copy(
            src_ref=v_ref, dst_ref=buf.at[me], send_sem=send_sems.at[k], recv_sem=recv_sems.at[k],
            device_id=(p >> 2, (p >> 1) & 1, p & 1), device_id_type=MESH) for k, p in enumerate(peers)]
        for cp in sends:
            cp.start()
        for k, p in enumerate(peers):
            pltpu.make_async_remote_copy(
                src_ref=v_ref, dst_ref=buf.at[p], send_sem=send_sems.at[k], recv_sem=recv_sems.at[k],
                device_id=(p >> 2, (p >> 1) & 1, p & 1), device_id_type=MESH).wait_recv()
        for cp in sends:
            cp.wait_send()
        acc = buf[0]
        for d in range(1, N_DEV):
            acc = acc + buf[d]
        out_ref[...] = acc

    vm = pl.BlockSpec(memory_space=pltpu.VMEM)
    return pl.pallas_call(
        body, name="allreduce_small", in_specs=[vm], out_specs=vm, out_shape=_sds((SMALL_ROWS, 128)),
        scratch_shapes=[pltpu.VMEM((N_DEV, SMALL_ROWS, 128), F32),
                        pltpu.SemaphoreType.DMA((N_DEV - 1,)), pltpu.SemaphoreType.DMA((N_DEV - 1,))],
    )(vec)


def _flatten_shard(w_in, w_out, w_pg, w_pp):
    return jnp.concatenate([w_in.reshape(-1, D_MODEL), w_out.reshape(-1, D_MODEL), w_pg.reshape(-1, D_MODEL),
                            w_pp.reshape(-1, D_MODEL)], axis=0)


def _unflatten_shard(flat):
    a, b, c = ROWS_W_IN, ROWS_W_IN + ROWS_W_OUT, ROWS_W_IN + ROWS_W_OUT + ROWS_W_PG
    q = D_MODEL // N_SHARD
    return (flat[:a].reshape(2, D_MODEL, D_MODEL), flat[a:b].reshape(2, q, D_MODEL),
            flat[b:c].reshape(2, q, D_MODEL), flat[c:].reshape(2, D_PLE, q))


def _full_w_pp(gathered):
    c = ROWS_W_IN + ROWS_W_OUT + ROWS_W_PG
    q = D_MODEL // N_SHARD
    rpp = ROWS_W_PP // 2
    return [gathered[:, c + l * rpp:c + (l + 1) * rpp, :].reshape(N_SHARD, D_PLE, q).transpose(1, 0, 2)
            .reshape(D_PLE, D_MODEL) for l in range(2)]


def _shard_major(dw_in, dw_out, dw_pg, dw_pp):
    q = D_MODEL // N_SHARD
    rpp = ROWS_W_PP // 2
    parts = []
    parts.extend(dw_in)
    for l in range(2):
        parts.append(dw_out[l].reshape(N_SHARD, q, D_MODEL))
    for l in range(2):
        parts.append(dw_pg[l].reshape(N_SHARD, q, D_MODEL))
    for l in range(2):
        parts.append(dw_pp[l].reshape(D_PLE, N_SHARD, q).transpose(1, 0, 2).reshape(N_SHARD, rpp, D_MODEL))
    return jnp.concatenate(parts, axis=1)


def _lower_bounds(lb_logits):
    sm = jax.nn.softmax(lb_logits.astype(F32), axis=0)
    return jnp.cumsum(sm, axis=0) - sm[0:1]


def kernel(x, p, norm_mix, w_in, a_out_norm, b_out_norm, w_out, lb_logits, ple_gate_norm, w_ple_gate, w_ple_proj, ple_post_norm, final_norm, loss_target, m_norm_mix, m_w_in, m_a_out_norm, m_b_out_norm, m_w_out, m_lb_logits, m_ple_gate_norm, m_w_ple_gate, m_w_ple_proj, m_ple_post_norm, m_final_norm, v_norm_mix, v_w_in, v_a_out_norm, v_b_out_norm, v_w_out, v_lb_logits, v_ple_gate_norm, v_w_ple_gate, v_w_ple_proj, v_ple_post_norm, v_final_norm):
    t = x.shape[1]
    h0 = x.reshape(t, D_MODEL)
    target = loss_target.reshape(t, D_MODEL)
    pl_in = p.reshape(2, t, D_PLE)

    w_flat_bf = _flatten_shard(_bf(w_in), _bf(w_out), _bf(w_ple_gate), _bf(w_ple_proj))
    chip = 2 * lax.axis_index("x") + lax.axis_index("y")
    gathered = lax.dynamic_update_slice(_gather_weights(w_flat_bf), w_flat_bf[None], (chip, 0, 0))
    w_pps = _full_w_pp(gathered)
    lbs, lbs_vjp = jax.vjp(_lower_bounds, lb_logits)

    saved = []
    h = h0
    for l in range(2):
        g_mix = norm_mix[l].reshape(1, D_MODEL)
        lb = lbs[l].reshape(1, GROUP)
        ga = a_out_norm[l].reshape(1, GROUP)
        gb = b_out_norm[l].reshape(1, GROUP)
        proj, proj_bf = _inproj(h, g_mix, gathered, l)
        ya, states = _hgrn_fwd(proj, lb, ga)
        ob, yb = _sb_fwd(proj_bf, proj, gb)
        h1 = _outproj(h, ya, yb, gathered, l)
        g_post = ple_post_norm[l].reshape(1, D_MODEL)
        g_gate = ple_gate_norm[l].reshape(1, D_MODEL)
        h2 = _ple_fwd(h1, pl_in[l], w_pps[l], gathered, l, g_post, g_gate)
        saved.append((h, proj, proj_bf, states, ya, yb, ob, h1))
        h = h2

    dh, d_final, loss_part = _final(h, final_norm.reshape(1, D_MODEL), target)

    dw_in_l, dw_out_l, dw_pg_l, dw_pp_l = [None] * 2, [None] * 2, [None] * 2, [None] * 2
    d_mix, d_a, d_b, d_lb, d_gate, d_post = [None] * 2, [None] * 2, [None] * 2, [None] * 2, [None] * 2, [None] * 2
    for l in (1, 0):
        h_in, proj, proj_bf, states, ya, yb, ob, h1 = saved[l]
        g_mix = norm_mix[l].reshape(1, D_MODEL)
        lb = lbs[l].reshape(1, GROUP)
        ga = a_out_norm[l].reshape(1, GROUP)
        gb = b_out_norm[l].reshape(1, GROUP)
        g_post = ple_post_norm[l].reshape(1, D_MODEL)
        g_gate = ple_gate_norm[l].reshape(1, D_MODEL)
        dh1, dw_pg_l[l], dw_pp_l[l], d_gate[l], d_post[l] = _ple_bwd(dh, h1, pl_in[l], w_pps[l], gathered, l, g_post,
                                                                     g_gate)
        dya, dyb, dw_out_l[l] = _outproj_bwd(dh1, ya, yb, gathered, l)
        dbq, dbk, dbv, dbg, d_b[l] = _sb_bwd(proj_bf, proj, ob, dyb, gb)
        da, d_lb[l], d_a[l] = _hgrn_bwd(proj, lb, ga, states, dya)
        db = jnp.stack([dbq, dbk * LN2, dbv, dbg]).astype(BF16)
        dw_in_l[l] = _inproj_bwd_dw(h_in, g_mix, da, db)
        dh, d_mix[l] = _inproj_bwd_dx(dh1, h_in, g_mix, gathered, l, da, db)
    grad_x = dh.reshape(x.shape)

    g_full = _shard_major(dw_in_l, dw_out_l, dw_pg_l, dw_pp_l)
    chip_sum, chip_sum_bf = _add_my_half(g_full, _swap_halves(g_full))
    mine_half = _sum_scattered(chip_sum, _scatter_shards(chip_sum_bf))
    other_half = _swap_reduced(mine_half)
    south = lax.axis_index("c") == 0
    g_flat = jnp.concatenate([jnp.where(south, mine_half, other_half), jnp.where(south, other_half, mine_half)])
    g_w_in, g_w_out, g_w_pg, g_w_pp = _unflatten_shard(g_flat)

    small = jnp.concatenate([
        jnp.concatenate(d_mix).reshape(-1, 128), jnp.concatenate(d_a).reshape(-1, 128),
        jnp.concatenate(d_b).reshape(-1, 128), jnp.concatenate(d_lb).reshape(-1, 128),
        jnp.concatenate(d_gate).reshape(-1, 128), jnp.concatenate(d_post).reshape(-1, 128),
        d_final.reshape(-1, 128), jnp.broadcast_to(loss_part, (8, 128))], axis=0)
    small = _allreduce_small(small)
    loss = small[80, 0]
    g_norm_mix = small[0:16].reshape(2, D_MODEL)
    g_a = small[16:24].reshape(2, GROUP)
    g_b = small[24:32].reshape(2, GROUP)
    (g_lb,) = lbs_vjp(small[32:40].reshape(2, GROUP))
    g_gate = small[40:56].reshape(2, D_MODEL)
    g_post = small[56:72].reshape(2, D_MODEL)
    g_final = small[72:80].reshape(D_MODEL)

    def adam_matrix(w, g, m, v):
        d, nm, nv = _adamw(w.reshape(-1, D_MODEL), g.reshape(-1, D_MODEL), m.reshape(-1, D_MODEL), v.reshape(-1, D_MODEL))
        return d.reshape(w.shape), nm.reshape(w.shape), nv.reshape(w.shape)

    d_w_in, nm_w_in, nv_w_in = adam_matrix(w_in, g_w_in, m_w_in, v_w_in)
    d_w_out, nm_w_out, nv_w_out = adam_matrix(w_out, g_w_out, m_w_out, v_w_out)
    d_w_pg, nm_w_pg, nv_w_pg = adam_matrix(w_ple_gate, g_w_pg, m_w_ple_gate, v_w_ple_gate)
    d_w_pp, nm_w_pp, nv_w_pp = adam_matrix(w_ple_proj, g_w_pp, m_w_ple_proj, v_w_ple_proj)

    small_w = [norm_mix, a_out_norm, b_out_norm, lb_logits, ple_gate_norm, ple_post_norm, final_norm]
    small_g = [g_norm_mix, g_a, g_b, g_lb, g_gate, g_post, g_final]
    small_m = [m_norm_mix, m_a_out_norm, m_b_out_norm, m_lb_logits, m_ple_gate_norm, m_ple_post_norm, m_final_norm]
    small_v = [v_norm_mix, v_a_out_norm, v_b_out_norm, v_lb_logits, v_ple_gate_norm, v_ple_post_norm, v_final_norm]
    pack = lambda arrs: jnp.concatenate([a.reshape(-1, 128) for a in arrs], axis=0)
    ds, nms, nvs = _adamw(pack(small_w), pack(small_g), pack(small_m), pack(small_v))

    def unpack(packed):
        out, r = [], 0
        for a in small_w:
            n = a.size // 128
            out.append(packed[r:r + n].reshape(a.shape))
            r += n
        return out

    d_s, nm_s, nv_s = unpack(ds), unpack(nms), unpack(nvs)

    def ordered(s, big):
        return [s[0], big[0], s[1], s[2], big[1], s[3], s[4], big[2], big[3], s[5], s[6]]

    grads = ordered(small_g, [g_w_in, g_w_out, g_w_pg, g_w_pp])
    deltas = ordered(d_s, [d_w_in, d_w_out, d_w_pg, d_w_pp])
    new_m = ordered(nm_s, [nm_w_in, nm_w_out, nm_w_pg, nm_w_pp])
    new_v = ordered(nv_s, [nv_w_in, nv_w_out, nv_w_pg, nv_w_pp])
    return (loss, grad_x, *grads, *deltas, *new_m, *new_v)
```

```python
import functools
import math

import numpy as np
import jax
import jax.numpy as jnp
from jax import lax
from jax.experimental import pallas as pl
from jax.experimental.pallas import tpu as pltpu

F32 = jnp.float32
BF16 = jnp.bfloat16
MESH = pl.DeviceIdType.MESH

D_MODEL = 1024
D_PLE = 256
D_IN = 4096
A_HEADS, A_D = 4, 128
B_HEADS, B_D = 8, 64
GROUP = 512
EPS = 1e-6
N_SHARD = 4
N_DEV = 8

HG_CHUNK = 128
HG_LEVELS = 7
SB_TQ = 1024
SB_TK = 128
LOG2E = 1.4426950408889634
LN2 = 0.6931471805599453

ADAM_LR, ADAM_B1, ADAM_B2, ADAM_EPS, ADAM_WD, ADAM_STEP = 0.001, 0.9, 0.999, 1e-08, 0.01, 10

VMEM_LIMIT = 48 * 1024 * 1024
VMEM_LIMIT_BIG = 58 * 1024 * 1024

ROWS_W_IN = 2 * D_MODEL
ROWS_W_OUT = 2 * (D_MODEL // N_SHARD)
ROWS_W_PG = 2 * (D_MODEL // N_SHARD)
ROWS_W_PP = 2 * (D_PLE * (D_MODEL // N_SHARD) // D_MODEL)
ROWS_FLAT = ROWS_W_IN + ROWS_W_OUT + ROWS_W_PG + ROWS_W_PP
HALF_FLAT = ROWS_FLAT // 2
N_CHUNK = 10
CHUNK_ROWS = HALF_FLAT // N_CHUNK

SMALL_ROWS = 88


def _sds(shape, dtype=F32):
    return jax.ShapeDtypeStruct(shape, dtype)


def _params(sem=None, vmem_limit=VMEM_LIMIT):
    kw = dict(vmem_limit_bytes=vmem_limit)
    if sem is not None:
        kw["dimension_semantics"] = sem
    return pltpu.CompilerParams(**kw)


def _dot(a, b, precision=None):
    return lax.dot_general(a, b, (((1,), (0,)), ((), ())), preferred_element_type=F32, precision=precision)


def _dot_nt(a, b, precision=None):
    return lax.dot_general(a, b, (((1,), (1,)), ((), ())), preferred_element_type=F32, precision=precision)


def _dot_tn(a, b, precision=None):
    return lax.dot_general(a, b, (((0,), (0,)), ((), ())), preferred_element_type=F32, precision=precision)


def _bf(x):
    return x.astype(BF16)


def _split(x):
    hi = x.astype(BF16)
    lo = (x - hi.astype(F32)).astype(BF16)
    return hi, lo


def _rms(x):
    r = lax.rsqrt(jnp.mean(x * x, axis=-1, keepdims=True) + EPS)
    return x * r, r


def _rms_bwd(dxh, xh, r):
    return r * (dxh - xh * jnp.mean(dxh * xh, axis=-1, keepdims=True))


def _sigmoid(x):
    return 1.0 / (1.0 + jnp.exp(-x))


def _silu_grad(x, sig):
    return sig * (1.0 + x * (1.0 - sig))


def _row_tile(t, want):
    return min(t, want)


def _inproj(h, g, gathered, layer):
    t = h.shape[0]
    tm = _row_tile(t, 512)
    tn = D_MODEL
    scale = jnp.ones((1, D_IN), F32).at[:, 4 * GROUP:5 * GROUP].set(B_D ** -0.5 * LOG2E)

    def body(h_ref, g_ref, w_ref, s_ref, o_ref, ob_ref):
        xh, _ = _rms(h_ref[...])
        acc = _dot(_bf(xh * g_ref[...]), w_ref[...])
        o_ref[...] = acc
        ob_ref[...] = _bf(acc * s_ref[...])

    return pl.pallas_call(
        body, name="inproj", grid=(D_IN // tn, t // tm),
        in_specs=[pl.BlockSpec((tm, D_MODEL), lambda j, i: (i, 0)),
                  pl.BlockSpec((1, D_MODEL), lambda j, i: (0, 0)),
                  pl.BlockSpec((None, D_MODEL, tn), lambda j, i: (j, layer, 0)),
                  pl.BlockSpec((1, tn), lambda j, i: (0, j))],
        out_specs=[pl.BlockSpec((tm, tn), lambda j, i: (i, j)), pl.BlockSpec((tm, tn), lambda j, i: (i, j))],
        out_shape=[_sds((t, D_IN)), _sds((t, D_IN), BF16)], compiler_params=_params(("arbitrary", "arbitrary")),
    )(h, g, gathered, scale)


def _rows_spec(first_row):
    q = D_MODEL // N_SHARD
    return pl.BlockSpec((N_SHARD, q, D_MODEL), lambda i: (0, first_row // q, 0))


def _outproj(h, ya, yb, gathered, layer):
    t = h.shape[0]
    tm = _row_tile(t, 512)

    def body(h_ref, ya_ref, yb_ref, w_ref, o_ref):
        o_ref[...] = (h_ref[...] + _dot(_bf(ya_ref[...]), w_ref[0:2].reshape(GROUP, D_MODEL))
                      + _dot(_bf(yb_ref[...]), w_ref[2:4].reshape(GROUP, D_MODEL)))

    return pl.pallas_call(
        body, name="outproj", grid=(t // tm,),
        in_specs=[pl.BlockSpec((tm, D_MODEL), lambda i: (i, 0)),
                  pl.BlockSpec((tm, GROUP), lambda i: (i, 0)),
                  pl.BlockSpec((tm, GROUP), lambda i: (i, 0)),
                  _rows_spec(ROWS_W_IN + layer * (D_MODEL // N_SHARD))],
        out_specs=pl.BlockSpec((tm, D_MODEL), lambda i: (i, 0)),
        out_shape=_sds((t, D_MODEL)), compiler_params=_params(("arbitrary",)),
    )(h, ya, yb, gathered)


def _ple_fwd(h, p, w_pp, gathered, layer, g_post, g_gate):
    t = h.shape[0]
    tm = _row_tile(t, 256)

    def body(h_ref, p_ref, wpp_ref, wpg_ref, gp_ref, gg_ref, o_ref):
        x = h_ref[...]
        ph, _ = _rms(_dot(_bf(p_ref[...]), wpp_ref[...]))
        xh, _ = _rms(x)
        gate = _sigmoid(_dot(_bf(xh * gg_ref[...]), wpg_ref[...].reshape(D_MODEL, D_MODEL)))
        o_ref[...] = x + gate * (ph * gp_ref[...])

    return pl.pallas_call(
        body, name="ple_fwd", grid=(t // tm,),
        in_specs=[pl.BlockSpec((tm, D_MODEL), lambda i: (i, 0)),
                  pl.BlockSpec((tm, D_PLE), lambda i: (i, 0)),
                  pl.BlockSpec((D_PLE, D_MODEL), lambda i: (0, 0)),
                  _rows_spec(ROWS_W_IN + ROWS_W_OUT + layer * (D_MODEL // N_SHARD)),
                  pl.BlockSpec((1, D_MODEL), lambda i: (0, 0)),
                  pl.BlockSpec((1, D_MODEL), lambda i: (0, 0))],
        out_specs=pl.BlockSpec((tm, D_MODEL), lambda i: (i, 0)),
        out_shape=_sds((t, D_MODEL)), compiler_params=_params(("arbitrary",)),
    )(h, p, w_pp, gathered, g_post, g_gate)


def _ple_bwd(dh2, h, p, w_pp, gathered, layer, g_post, g_gate):
    t = h.shape[0]
    tm = _row_tile(t, 256)

    def body(d_ref, h_ref, p_ref, wpp_ref, wpg_ref, gp_ref, gg_ref, dh_ref, dwpg_ref, dwpp_ref, dgg_ref, dgp_ref):
        @pl.when(pl.program_id(0) == 0)
        def _():
            dwpg_ref[...] = jnp.zeros_like(dwpg_ref)
            dwpp_ref[...] = jnp.zeros_like(dwpp_ref)
            dgg_ref[...] = jnp.zeros_like(dgg_ref)
            dgp_ref[...] = jnp.zeros_like(dgp_ref)

        d = d_ref[...]
        x = h_ref[...]
        gp = gp_ref[...]
        gg = gg_ref[...]
        pb = _bf(p_ref[...])
        ph, rp = _rms(_dot(pb, wpp_ref[...]))
        pe = ph * gp
        xh, rx = _rms(x)
        un = _bf(xh * gg)
        wpg = wpg_ref[...].reshape(D_MODEL, D_MODEL)
        gate = _sigmoid(_dot(un, wpg))
        dgpre = _bf(d * pe * gate * (1.0 - gate))
        dun = _dot_nt(dgpre, wpg)
        dh_ref[...] = d + _rms_bwd(dun * gg, xh, rx)
        dgg_ref[...] += jnp.sum(dun * xh, axis=0, keepdims=True)
        dwpg_ref[...] += _dot_tn(un, dgpre)
        dpe = d * gate
        dgp_ref[...] += jnp.sum(dpe * ph, axis=0, keepdims=True)
        dwpp_ref[...] += _dot_tn(pb, _bf(_rms_bwd(dpe * gp, ph, rp)))

    return pl.pallas_call(
        body, name="ple_bwd", grid=(t // tm,),
        in_specs=[pl.BlockSpec((tm, D_MODEL), lambda i: (i, 0)),
                  pl.BlockSpec((tm, D_MODEL), lambda i: (i, 0)),
                  pl.BlockSpec((tm, D_PLE), lambda i: (i, 0)),
                  pl.BlockSpec((D_PLE, D_MODEL), lambda i: (0, 0)),
                  _rows_spec(ROWS_W_IN + ROWS_W_OUT + layer * (D_MODEL // N_SHARD)),
                  pl.BlockSpec((1, D_MODEL), lambda i: (0, 0)),
                  pl.BlockSpec((1, D_MODEL), lambda i: (0, 0))],
        out_specs=[pl.BlockSpec((tm, D_MODEL), lambda i: (i, 0)),
                   pl.BlockSpec((D_MODEL, D_MODEL), lambda i: (0, 0)),
                   pl.BlockSpec((D_PLE, D_MODEL), lambda i: (0, 0)),
                   pl.BlockSpec((1, D_MODEL), lambda i: (0, 0)),
                   pl.BlockSpec((1, D_MODEL), lambda i: (0, 0))],
        out_shape=[_sds((t, D_MODEL)), _sds((D_MODEL, D_MODEL)), _sds((D_PLE, D_MODEL)),
                   _sds((1, D_MODEL)), _sds((1, D_MODEL))],
        compiler_params=_params(("arbitrary",)),
    )(dh2, h, p, w_pp, gathered, g_post, g_gate)


def _outproj_bwd(dh, ya, yb, gathered, layer):
    t = dh.shape[0]
    tm = _row_tile(t, 512)

    def body(d_ref, ya_ref, yb_ref, w_ref, dya_ref, dyb_ref, dw_ref):
        @pl.when(pl.program_id(0) == 0)
        def _():
            dw_ref[...] = jnp.zeros_like(dw_ref)

        d = _bf(d_ref[...])
        dya_ref[...] = _dot_nt(d, w_ref[0:2].reshape(GROUP, D_MODEL))
        dyb_ref[...] = _dot_nt(d, w_ref[2:4].reshape(GROUP, D_MODEL))
        dw_ref[pl.ds(0, GROUP), :] += _dot_tn(_bf(ya_ref[...]), d)
        dw_ref[pl.ds(GROUP, GROUP), :] += _dot_tn(_bf(yb_ref[...]), d)

    return pl.pallas_call(
        body, name="outproj_bwd", grid=(t // tm,),
        in_specs=[pl.BlockSpec((tm, D_MODEL), lambda i: (i, 0)),
                  pl.BlockSpec((tm, GROUP), lambda i: (i, 0)),
                  pl.BlockSpec((tm, GROUP), lambda i: (i, 0)),
                  _rows_spec(ROWS_W_IN + layer * (D_MODEL // N_SHARD))],
        out_specs=[pl.BlockSpec((tm, GROUP), lambda i: (i, 0)),
                   pl.BlockSpec((tm, GROUP), lambda i: (i, 0)),
                   pl.BlockSpec((D_MODEL, D_MODEL), lambda i: (0, 0))],
        out_shape=[_sds((t, GROUP)), _sds((t, GROUP)), _sds((D_MODEL, D_MODEL))],
        compiler_params=_params(("arbitrary",)),
    )(dh, ya, yb, gathered)


def _inproj_bwd_dx(dres, h, g, gathered, layer, da, db):
    t = h.shape[0]
    tm = _row_tile(t, 256)

    def body(dres_ref, h_ref, g_ref, w_ref, da_ref, db_ref, dh_ref, dg_ref):
        @pl.when(pl.program_id(0) == 0)
        def _():
            dg_ref[...] = jnp.zeros_like(dg_ref)

        du = jnp.zeros((tm, D_MODEL), F32)
        for i in range(8):
            part = da_ref[i] if i < 4 else db_ref[i - 4]
            du = du + _dot_nt(part, w_ref[i // 2, :, pl.ds((i % 2) * GROUP, GROUP)])
        xh, r = _rms(h_ref[...])
        dg_ref[...] += jnp.sum(du * xh, axis=0, keepdims=True)
        dh_ref[...] = dres_ref[...] + _rms_bwd(du * g_ref[...], xh, r)

    return pl.pallas_call(
        body, name="inproj_bwd_dx", grid=(t // tm,),
        in_specs=[pl.BlockSpec((tm, D_MODEL), lambda i: (i, 0)),
                  pl.BlockSpec((tm, D_MODEL), lambda i: (i, 0)),
                  pl.BlockSpec((1, D_MODEL), lambda i: (0, 0)),
                  pl.BlockSpec((N_SHARD, D_MODEL, D_MODEL), lambda i: (0, layer, 0)),
                  pl.BlockSpec((4, tm, GROUP), lambda i: (0, i, 0)),
                  pl.BlockSpec((4, tm, GROUP), lambda i: (0, i, 0))],
        out_specs=[pl.BlockSpec((tm, D_MODEL), lambda i: (i, 0)),
                   pl.BlockSpec((1, D_MODEL), lambda i: (0, 0))],
        out_shape=[_sds((t, D_MODEL)), _sds((1, D_MODEL))],
        compiler_params=_params(("arbitrary",)),
    )(dres, h, g, gathered, da, db)


def _inproj_bwd_dw(h, g, da, db):
    t = h.shape[0]
    tm = _row_tile(t, 512)

    def body(h_ref, g_ref, da_ref, db_ref, dw_ref):
        @pl.when(pl.program_id(0) == 0)
        def _():
            dw_ref[...] = jnp.zeros_like(dw_ref)

        xh, _ = _rms(h_ref[...])
        u = _bf(xh * g_ref[...])
        for i in range(8):
            dw_ref[i // 2, :, pl.ds((i % 2) * GROUP, GROUP)] += _dot_tn(u, da_ref[i] if i < 4 else db_ref[i - 4])

    return pl.pallas_call(
        body, name="inproj_bwd_dw", grid=(t // tm,),
        in_specs=[pl.BlockSpec((tm, D_MODEL), lambda i: (i, 0)),
                  pl.BlockSpec((1, D_MODEL), lambda i: (0, 0)),
                  pl.BlockSpec((4, tm, GROUP), lambda i: (0, i, 0)),
                  pl.BlockSpec((4, tm, GROUP), lambda i: (0, i, 0))],
        out_specs=pl.BlockSpec((N_SHARD, D_MODEL, D_MODEL), lambda i: (0, 0, 0)),
        out_shape=_sds((N_SHARD, D_MODEL, D_MODEL)), compiler_params=_params(("arbitrary",), VMEM_LIMIT_BIG),
    )(h, g, da, db)


def _final(h, g, target):
    t = h.shape[0]
    tm = _row_tile(t, 512)

    def body(h_ref, g_ref, t_ref, dh_ref, dg_ref, loss_ref):
        @pl.when(pl.program_id(0) == 0)
        def _():
            dg_ref[...] = jnp.zeros_like(dg_ref)
            loss_ref[...] = jnp.zeros_like(loss_ref)

        xh, r = _rms(h_ref[...])
        gg = g_ref[...]
        err = xh * gg - t_ref[...]
        part = 0.5 * jnp.sum(jnp.mean(err * err, axis=-1, keepdims=True), axis=0, keepdims=True)
        loss_ref[...] += jnp.broadcast_to(part, loss_ref.shape)
        dy = err * (1.0 / D_MODEL)
        dg_ref[...] += jnp.sum(dy * xh, axis=0, keepdims=True)
        dh_ref[...] = _rms_bwd(dy * gg, xh, r)

    return pl.pallas_call(
        body, name="final", grid=(t // tm,),
        in_specs=[pl.BlockSpec((tm, D_MODEL), lambda i: (i, 0)),
                  pl.BlockSpec((1, D_MODEL), lambda i: (0, 0)),
                  pl.BlockSpec((tm, D_MODEL), lambda i: (i, 0))],
        out_specs=[pl.BlockSpec((tm, D_MODEL), lambda i: (i, 0)),
                   pl.BlockSpec((1, D_MODEL), lambda i: (0, 0)),
                   pl.BlockSpec((1, 128), lambda i: (0, 0))],
        out_shape=[_sds((t, D_MODEL)), _sds((1, D_MODEL)), _sds((1, 128))],
        compiler_params=_params(("arbitrary",)),
    )(h, g, target)


def _hgrn_consts():
    c, nl = HG_CHUNK, HG_LEVELS
    t = np.arange(c)
    tril = np.tril(np.ones((c, c), np.float32))
    masks = np.zeros((nl + 1, c, c), np.float32)
    masks[0] = np.eye(c, dtype=np.float32)
    dmat = np.zeros(((nl + 2) * c, c), np.float32)
    dmat[0:c] = tril
    for l in range(nl):
        m = c >> (l + 1)
        blk = t // (2 * m)
        r = blk * 2 * m + m - 1
        upper = (t % (2 * m)) >= m
        masks[l + 1] = ((blk[:, None] == blk[None, :]) & upper[:, None] & (~upper)[None, :]).astype(np.float32)
        dmat[(l + 1) * c:(l + 2) * c] = tril[t] - tril[r]
    dmat[(nl + 1) * c:] = np.triu(np.ones((c, c), np.float32), k=1)
    return jnp.asarray(masks), jnp.asarray(dmat, BF16)


HG_HEADS = 4


def _hgrn_pre(aq, af, lb):
    sq = _sigmoid(aq)
    sneg = _sigmoid(-af)
    kk = (1.0 - lb) * sneg
    return sq, aq * sq, sneg, kk, jnp.log1p(-kk)


def _hgrn_x(logf, dmat_ref):
    dm = dmat_ref[pl.ds(0, (HG_LEVELS + 1) * HG_CHUNK), :]
    lhi, llo = _split(logf)
    return _dot(dm, lhi) + _dot(dm, llo)


def _hgrn_level(x_all, l, q, kk):
    c = HG_CHUNK
    x = x_all[(l + 1) * c:(l + 2) * c]
    qf = jnp.exp(jnp.minimum(x, 0.0))
    kf = jnp.exp(-jnp.maximum(x, 0.0))
    return qf, kf, _bf(q * qf), _bf(kk * kf)


def _hgrn_scores(xs, qs, kks, mask_ref):
    ps = [mask_ref[0] * _dot_nt(_bf(q), _bf(kk)) for q, kk in zip(qs, kks)]
    for l in range(HG_LEVELS):
        for i, (x_all, q, kk) in enumerate(zip(xs, qs, kks)):
            _, _, ql, kl = _hgrn_level(x_all, l, q, kk)
            ps[i] = ps[i] + mask_ref[l + 1] * _dot_nt(ql, kl)
    return ps


def _hgrn_specs(n_chunks, rev):
    c, w = HG_CHUNK, HG_HEADS * A_D
    cidx = (lambda n: n_chunks - 1 - n) if rev else (lambda n: n)
    col = lambda g: pl.BlockSpec((c, w), lambda h, n: (cidx(n), g * (A_HEADS // HG_HEADS) + h))
    vec = pl.BlockSpec((1, w), lambda h, n: (0, h))
    mask = pl.BlockSpec((HG_LEVELS + 1, c, c), lambda h, n: (0, 0, 0))
    dmat = pl.BlockSpec(((HG_LEVELS + 2) * c, c), lambda h, n: (0, 0))
    state = pl.BlockSpec((HG_HEADS, None, A_D, A_D), lambda h, n: (h, cidx(n), 0, 0))
    return cidx, col, vec, mask, dmat, state


def _lanes(i):
    return pl.ds(i * A_D, A_D)


def _hgrn_fwd(proj, lb, gain):
    t = proj.shape[0]
    c = HG_CHUNK
    nch = t // c
    masks, dmat = _hgrn_consts()
    cidx, col, vec, mask_spec, dmat_spec, state_spec = _hgrn_specs(nch, False)
    heads = range(HG_HEADS)

    def body(aq_ref, af_ref, ai_ref, ag_ref, lb_ref, gain_ref, mask_ref, dmat_ref, y_ref, st_ref, s_scr):
        @pl.when(pl.program_id(1) == 0)
        def _():
            s_scr[...] = jnp.zeros_like(s_scr)

        pre = [_hgrn_pre(aq_ref[:, _lanes(i)], af_ref[:, _lanes(i)], lb_ref[:, _lanes(i)]) for i in heads]
        qs, kks = [p[1] for p in pre], [p[3] for p in pre]
        xs = [_hgrn_x(p[4], dmat_ref) for p in pre]
        bs = [x[0:c] for x in xs]
        b_lasts = [jnp.sum(p[4], axis=0, keepdims=True) for p in pre]
        ps = _hgrn_scores(xs, qs, kks, mask_ref)
        ss = [s_scr[i] for i in heads]
        vbs = [_bf(ai_ref[:, _lanes(i)]) for i in heads]
        os_ = [_dot(_bf(ps[i]), vbs[i]) + _dot_nt(_bf(qs[i] * jnp.exp(bs[i])), _bf(ss[i])) for i in heads]
        for i in heads:
            st_ref[i] = ss[i]
            s_scr[i] = ss[i] * jnp.exp(b_lasts[i]) + _dot_tn(vbs[i], _bf(kks[i] * jnp.exp(b_lasts[i] - bs[i])))
            oh, _ = _rms(os_[i])
            ag = ag_ref[:, _lanes(i)]
            y_ref[:, _lanes(i)] = oh * gain_ref[:, _lanes(i)] * (ag * _sigmoid(ag))

    return pl.pallas_call(
        body, name="hgrn_fwd", grid=(A_HEADS // HG_HEADS, nch),
        in_specs=[col(0), col(1), col(2), col(3), vec, vec, mask_spec, dmat_spec],
        out_specs=[pl.BlockSpec((c, HG_HEADS * A_D), lambda h, n: (n, h)), state_spec],
        out_shape=[_sds((t, GROUP)), _sds((A_HEADS, nch, A_D, A_D))],
        scratch_shapes=[pltpu.VMEM((HG_HEADS, A_D, A_D), F32)],
        compiler_params=_params(("arbitrary", "arbitrary")),
    )(proj, proj, proj, proj, lb, gain, masks, dmat)


def _hgrn_bwd(proj, lb, gain, states, dya):
    t = proj.shape[0]
    c, nl = HG_CHUNK, HG_LEVELS
    nch = t // c
    masks, dmat = _hgrn_consts()
    cidx, col, vec, mask_spec, dmat_spec, state_spec = _hgrn_specs(nch, True)
    heads = range(HG_HEADS)

    def body(aq_ref, af_ref, ai_ref, ag_ref, lb_ref, gain_ref, mask_ref, dmat_ref, st_ref, dy_ref,
             da_ref, dlb_ref, dgain_ref, ds_scr, z_scr):
        @pl.when(pl.program_id(1) == 0)
        def _():
            ds_scr[...] = jnp.zeros_like(ds_scr)
            dlb_ref[...] = jnp.zeros_like(dlb_ref)
            dgain_ref[...] = jnp.zeros_like(dgain_ref)

        aqs = [aq_ref[:, _lanes(i)] for i in heads]
        lbs = [lb_ref[:, _lanes(i)] for i in heads]
        pre = [_hgrn_pre(aqs[i], af_ref[:, _lanes(i)], lbs[i]) for i in heads]
        sqs, qs, snegs, kks = ([p[j] for p in pre] for j in range(4))
        xs = [_hgrn_x(p[4], dmat_ref) for p in pre]
        bs = [x[0:c] for x in xs]
        b_lasts = [jnp.sum(p[4], axis=0, keepdims=True) for p in pre]
        ebs = [jnp.exp(b) for b in bs]
        ebls = [jnp.exp(bl - b) for bl, b in zip(b_lasts, bs)]
        ebl_rows = [jnp.exp(bl) for bl in b_lasts]
        qes = [_bf(q * eb) for q, eb in zip(qs, ebs)]
        kes = [_bf(kk * ebl) for kk, ebl in zip(kks, ebls)]
        vbs = [_bf(ai_ref[:, _lanes(i)]) for i in heads]
        ss = [st_ref[i] for i in heads]
        sbs = [_bf(s) for s in ss]
        dss = [ds_scr[i] for i in heads]
        dsbs = [_bf(ds) for ds in dss]

        pbs = [_bf(p) for p in _hgrn_scores(xs, qs, kks, mask_ref)]
        os_ = [_dot(pbs[i], vbs[i]) + _dot_nt(qes[i], sbs[i]) for i in heads]

        dos = []
        for i in heads:
            ag, gain, dy = ag_ref[:, _lanes(i)], gain_ref[:, _lanes(i)], dy_ref[:, _lanes(i)]
            oh, r = _rms(os_[i])
            sg_sig = _sigmoid(ag)
            sg = ag * sg_sig
            da_ref[3, :, _lanes(i)] = _bf(dy * oh * gain * _silu_grad(ag, sg_sig))
            dgain_ref[:, _lanes(i)] += jnp.sum(dy * oh * sg, axis=0, keepdims=True)
            dos.append(_bf(_rms_bwd(dy * gain * sg, oh, r)))

        dps = [_dot_nt(dos[i], vbs[i]) for i in heads]
        for i in heads:
            da_ref[2, :, _lanes(i)] = _bf(_dot_tn(pbs[i], dos[i]) + _dot_nt(kes[i], dsbs[i]))
        dq_ss = [ebs[i] * _dot(dos[i], sbs[i]) for i in heads]
        dk_ss = [ebls[i] * _dot(vbs[i], dsbs[i]) for i in heads]
        dqs, dks = [], []
        for i in heads:
            dpd = jnp.sum(mask_ref[0] * dps[i], axis=1, keepdims=True)
            z_scr[i, pl.ds(0, c), :] = qs[i] * dq_ss[i]
            z_scr[i, pl.ds((nl + 1) * c, c), :] = kks[i] * dk_ss[i]
            dqs.append(dq_ss[i] + dpd * kks[i])
            dks.append(dk_ss[i] + dpd * qs[i])
        for l in range(nl):
            for i in heads:
                qf, kf, ql, kl = _hgrn_level(xs[i], l, qs[i], kks[i])
                dpl = _bf(mask_ref[l + 1] * dps[i])
                dq_l = qf * _dot(dpl, kl)
                dk_l = kf * _dot_tn(dpl, ql)
                z_scr[i, pl.ds((l + 1) * c, c), :] = qs[i] * dq_l - kks[i] * dk_l
                dqs[i] = dqs[i] + dq_l
                dks[i] = dks[i] + dk_l

        zsplits = [_split(z_scr[i]) for i in heads]
        dlogfs = [_dot_tn(dmat_ref[...], zhi) + _dot_tn(dmat_ref[...], zlo) for zhi, zlo in zsplits]
        ds_new = [_dot_tn(dos[i], qes[i]) for i in heads]
        for i in heads:
            dlogf = dlogfs[i] + ebl_rows[i] * jnp.sum(dss[i] * ss[i], axis=0, keepdims=True)
            dkk = dks[i] - dlogf / (1.0 - kks[i])
            da_ref[1, :, _lanes(i)] = _bf(dkk * (1.0 - lbs[i]) * (-(snegs[i] * (1.0 - snegs[i]))))
            dlb_ref[:, _lanes(i)] += jnp.sum(dkk * (-snegs[i]), axis=0, keepdims=True)
            da_ref[0, :, _lanes(i)] = _bf(dqs[i] * _silu_grad(aqs[i], sqs[i]))
            ds_scr[i] = dss[i] * ebl_rows[i] + ds_new[i]

    w = HG_HEADS * A_D
    return pl.pallas_call(
        body, name="hgrn_bwd", grid=(A_HEADS // HG_HEADS, nch),
        in_specs=[col(0), col(1), col(2), col(3), vec, vec, mask_spec, dmat_spec, state_spec,
                  pl.BlockSpec((c, w), lambda h, n: (cidx(n), h))],
        out_specs=[pl.BlockSpec((4, c, w), lambda h, n: (0, cidx(n), h)), vec, vec],
        out_shape=[_sds((4, t, GROUP), BF16)] + [_sds((1, GROUP))] * 2,
        scratch_shapes=[pltpu.VMEM((HG_HEADS, A_D, A_D), F32), pltpu.VMEM((HG_HEADS, (nl + 2) * c, A_D), F32)],
        compiler_params=_params(("arbitrary", "arbitrary")),
    )(proj, proj, proj, proj, lb, gain, masks, dmat, states, dya)


def _sb_consts():
    j = np.arange(SB_TK)
    strict = (j[:, None] > j[None, :]).astype(np.float32)
    incl = (j[:, None] >= j[None, :]).astype(np.float32)
    return jnp.asarray(strict, BF16), jnp.asarray(incl, BF16)


def _lane0(x):
    return jnp.broadcast_to(x[:, 0:1], x.shape)


def _sb_softplus(z, masked):
    logsig = jnp.minimum(z, 0.0) - jnp.log2(1.0 + jnp.exp2(-jnp.abs(z)))
    sp = z - logsig
    mask = None
    if masked:
        mask = lax.broadcasted_iota(jnp.int32, z.shape, 1) < lax.broadcasted_iota(jnp.int32, z.shape, 0)
        sp = jnp.where(mask, sp, 0.0)
    return mask, sp, logsig


def _sb_cumsum(sp, cmat):
    cs = _dot(_bf(sp), cmat)
    return cs, _lane0(cs + sp)


def _sb_sweep(qi, group_fn, state):
    nd = SB_TQ // SB_TK
    state = group_fn([(pl.multiple_of((qi * nd + d) * SB_TK, SB_TK), d * SB_TK, True) for d in reversed(range(nd))],
                     state)

    def step(j, st):
        return group_fn([(pl.multiple_of(((qi - j) * nd - 1 - g) * SB_TK, SB_TK), 0, False) for g in range(nd)], st)

    return lax.fori_loop(0, qi, step, state)


def _set_rows(r0, full, new):
    return new if r0 == 0 else jnp.concatenate([full[:r0], new], axis=0)


def _sb_specs(t, tq):
    col = lambda g: pl.BlockSpec((tq, 2 * B_D), lambda p, i, h: (i, g * (GROUP // (2 * B_D)) + p))
    full = lambda g: pl.BlockSpec((t, 2 * B_D), lambda p, i, h: (0, g * (GROUP // (2 * B_D)) + p))
    vec = pl.BlockSpec((1, 2 * B_D), lambda p, i, h: (0, p))
    mat = pl.BlockSpec((SB_TK, SB_TK), lambda p, i, h: (0, 0))
    return col, full, vec, mat


def _head_lanes(h):
    return (lax.broadcasted_iota(jnp.int32, (1, 2 * B_D), 1) >= B_D) == (h == 1)


def _put(ref, h, val):
    @pl.when(h == 0)
    def _():
        ref[...] = val

    @pl.when(h == 1)
    def _():
        ref[...] += val


def _sb_fwd(proj_bf, proj, gain, gather=None):
    t = proj.shape[0]
    tq = SB_TQ
    strict, _ = _sb_consts()
    n_steps = (B_HEADS // 2, t // tq, 2)

    def body(q_ref, k_ref, v_ref, bg_ref, gain_ref, m_ref, *rest):
        if gather is None:
            o_ref, y_ref = rest
        else:
            flat_ref, _, o_ref, y_ref, gathered_ref, send_sems, recv_sems = rest
            start, forward, finish = _gather_plan(flat_ref, gathered_ref, send_sems, recv_sems, *gather[2:])
            step = (pl.program_id(0) * n_steps[1] + pl.program_id(1)) * n_steps[2] + pl.program_id(2)
            pl.when(step == 0)(start)
            pl.when(step == 2 * n_steps[1] * n_steps[2])(forward)
            pl.when(step == n_steps[0] * n_steps[1] * n_steps[2] - 1)(finish)
        h = pl.program_id(2)
        lanes = _head_lanes(h)
        qb = jnp.where(lanes, q_ref[...], jnp.zeros_like(q_ref))
        cmat = m_ref[...]

        def group(tiles, state):
            carry, acc = state
            kv = [(k_ref[pl.ds(off, SB_TK), :], v_ref[pl.ds(off, SB_TK), :]) for off, _, _ in tiles]
            zs = [_dot_nt(qb[r0:], kb) for (_, r0, _), (kb, _) in zip(tiles, kv)]
            sps = [_sb_softplus(z, masked) for z, (_, _, masked) in zip(zs, tiles)]
            css = [_sb_cumsum(sp, cmat) for _, sp, _ in sps]
            ws = []
            for (mask, _, logsig), (cs, tot), (_, r0, masked) in zip(sps, css, tiles):
                w = jnp.exp2(logsig - cs - carry[r0:])
                ws.append(_split(jnp.where(mask, w, 0.0) if masked else w))
                carry = _set_rows(r0, carry, carry[r0:] + tot)
            for (whi, wlo), (_, vb), (_, r0, _) in zip(ws, kv, tiles):
                acc = _set_rows(r0, acc, acc[r0:] + _dot(whi, vb) + _dot(wlo, vb))
            return carry, acc

        _, acc = _sb_sweep(pl.program_id(1), group, (jnp.zeros((tq, SB_TK), F32), jnp.zeros((tq, 2 * B_D), F32)))
        o = jnp.where(lanes, acc, 0.0)
        oh = o * lax.rsqrt(jnp.sum(o * o, axis=-1, keepdims=True) * (1.0 / B_D) + EPS)
        bg = bg_ref[...]
        _put(o_ref, h, o)
        _put(y_ref, h, oh * gain_ref[...] * (bg * _sigmoid(bg)))

    col, full, vec, mat = _sb_specs(t, tq)
    out = pl.BlockSpec((tq, 2 * B_D), lambda p, i, h: (i, p))
    in_specs = [col(4), full(5), full(6), col(7), vec, mat]
    out_specs = [out, out]
    out_shape = [_sds((t, GROUP)), _sds((t, GROUP))]
    operands = [proj_bf, proj_bf, proj_bf, proj, gain, strict]
    extra = {}
    if gather is not None:
        in_specs += [_ANY, _ANY]
        out_specs += [_ANY]
        out_shape += [_sds(gather[1].shape, gather[1].dtype)]
        operands += [gather[0], gather[1]]
        extra = dict(input_output_aliases={7: 2}, scratch_shapes=_gather_sems(gather[4]))
    return pl.pallas_call(
        body, name="sb_fwd" if gather is None else "sb_fwd_gather", grid=n_steps,
        in_specs=in_specs, out_specs=out_specs, out_shape=out_shape,
        compiler_params=_params(("arbitrary", "arbitrary", "arbitrary")), **extra,
    )(*operands)


def _sb_bwd(proj_bf, proj, o, dy, gain):
    t = proj.shape[0]
    tq = SB_TQ
    strict, incl = _sb_consts()

    def body(q_ref, k_ref, v_ref, bg_ref, o_ref, dy_ref, gain_ref, ms_ref, mi_ref,
             dq_ref, dk_ref, dv_ref, dbg_ref, dgain_ref):
        qi = pl.program_id(1)
        h = pl.program_id(2)
        lanes = _head_lanes(h)

        @pl.when((qi == 0) & (h == 0))
        def _():
            dk_ref[...] = jnp.zeros_like(dk_ref)
            dv_ref[...] = jnp.zeros_like(dv_ref)
            dgain_ref[...] = jnp.zeros_like(dgain_ref)

        qb = jnp.where(lanes, q_ref[...], jnp.zeros_like(q_ref))
        cmat = ms_ref[...]
        imat = mi_ref[...]
        o = jnp.where(lanes, o_ref[...], 0.0)
        dy = jnp.where(lanes, dy_ref[...], 0.0)
        bg = bg_ref[...]
        gain = gain_ref[...]
        r = lax.rsqrt(jnp.sum(o * o, axis=-1, keepdims=True) * (1.0 / B_D) + EPS)
        oh = o * r
        sig = _sigmoid(bg)
        sg = bg * sig
        _put(dbg_ref, h, dy * oh * gain * _silu_grad(bg, sig))
        dgain_ref[...] += jnp.sum(dy * oh * sg, axis=0, keepdims=True)
        doh = dy * gain * sg
        do = _bf(r * (doh - oh * (jnp.sum(doh * oh, axis=-1, keepdims=True) * (1.0 / B_D))))
        total = jnp.broadcast_to(jnp.sum(do.astype(F32) * o, axis=1, keepdims=True), (tq, SB_TK))

        def group(tiles, state):
            carry, gcarry, dq = state
            kv = [(k_ref[pl.ds(off, SB_TK), :], v_ref[pl.ds(off, SB_TK), :]) for off, _, _ in tiles]
            zs = [_dot_nt(qb[r0:], kb) for (_, r0, _), (kb, _) in zip(tiles, kv)]
            dws = [_dot_nt(do[r0:], vb) for (_, r0, _), (_, vb) in zip(tiles, kv)]
            sps = [_sb_softplus(z, masked) for z, (_, _, masked) in zip(zs, tiles)]
            css = [_sb_cumsum(sp, cmat) for _, sp, _ in sps]
            ws, gs = [], []
            for (mask, _, logsig), (cs, tot), dw, (_, r0, masked) in zip(sps, css, dws, tiles):
                w = jnp.exp2(logsig - cs - carry[r0:])
                w = jnp.where(mask, w, 0.0) if masked else w
                ws.append(_bf(w))
                gs.append(dw * w)
                carry = _set_rows(r0, carry, carry[r0:] + tot)
            s2s = []
            for g in gs:
                ghi, glo = _split(g)
                s2s.append(_dot(ghi, imat) + _dot(glo, imat))
            dzs = []
            for (mask, _, logsig), g, s2, (_, r0, masked) in zip(sps, gs, s2s, tiles):
                before = total[r0:] - gcarry[r0:] - s2
                dz = g - jnp.exp2(logsig) * (g + before)
                dzs.append(_bf(jnp.where(mask, dz, 0.0) if masked else dz))
                gcarry = _set_rows(r0, gcarry, gcarry[r0:] + _lane0(s2))
            for dz, wb, (kb, _), (off, r0, _) in zip(dzs, ws, kv, tiles):
                dq = _set_rows(r0, dq, dq[r0:] + _dot(dz, kb))
                dk_ref[pl.ds(off, SB_TK), :] += _dot_tn(dz, qb[r0:])
                dv_ref[pl.ds(off, SB_TK), :] += _dot_tn(wb, do[r0:])
            return carry, gcarry, dq

        zero = jnp.zeros((tq, SB_TK), F32)
        _, _, dq = _sb_sweep(qi, group, (zero, zero, jnp.zeros((tq, 2 * B_D), F32)))
        _put(dq_ref, h, jnp.where(lanes, dq * (B_D ** -0.5), 0.0))

    col, full, vec, mat = _sb_specs(t, tq)
    blk = pl.BlockSpec((tq, 2 * B_D), lambda p, i, h: (i, p))
    whole = pl.BlockSpec((t, 2 * B_D), lambda p, i, h: (0, p))
    return pl.pallas_call(
        body, name="sb_bwd", grid=(B_HEADS // 2, t // tq, 2),
        in_specs=[col(4), full(5), full(6), col(7), blk, blk, vec, mat, mat],
        out_specs=[blk, whole, whole, blk, vec],
        out_shape=[_sds((t, GROUP))] * 4 + [_sds((1, GROUP))],
        compiler_params=_params(("arbitrary", "arbitrary", "arbitrary")),
    )(proj_bf, proj_bf, proj_bf, proj, o, dy, gain, strict, incl)


def _adamw(w, g, m, v):
    rows, cols = w.shape
    tr = rows
    for cand in (400, 256, 128, 64, 32, 16, 8):
        if rows % cand == 0:
            tr = cand
            break

    def body(w_ref, g_ref, m_ref, v_ref, d_ref, nm_ref, nv_ref):
        g_ = g_ref[...]
        m_ = ADAM_B1 * m_ref[...] + (1.0 - ADAM_B1) * g_
        v_ = ADAM_B2 * v_ref[...] + (1.0 - ADAM_B2) * (g_ * g_)
        m_hat = m_ / (1.0 - ADAM_B1 ** ADAM_STEP)
        v_hat = v_ / (1.0 - ADAM_B2 ** ADAM_STEP)
        d_ref[...] = -ADAM_LR * (m_hat / (jnp.sqrt(v_hat) + ADAM_EPS) + ADAM_WD * w_ref[...])
        nm_ref[...] = m_
        nv_ref[...] = v_

    spec = pl.BlockSpec((tr, cols), lambda i: (i, 0))
    return pl.pallas_call(
        body, name="adamw", grid=(rows // tr,), in_specs=[spec] * 4, out_specs=[spec] * 3,
        out_shape=[_sds((rows, cols))] * 3, compiler_params=_params(("arbitrary",)),
    )(w, g, m, v)


_ANY = pl.BlockSpec(memory_space=pl.ANY)


def _place():
    return lax.axis_index("x"), lax.axis_index("y"), lax.axis_index("c")


def _gather_plan(x_ref, out_ref, send_sems, recv_sems, row0, nrows, nc):
    x, y, c = _place()
    me = 2 * x + y
    sibling = (x, y, 1 - c)
    half = nrows // 2
    ch = half // nc
    peers = [me ^ k for k in (1, 2, 3)]

    def rows(shard, hc, r):
        return out_ref.at[shard, pl.ds(row0 + hc * half + r * ch, ch), :]

    def copy(k, shard, hc, r, to, src=None):
        return pltpu.make_async_remote_copy(
            src_ref=rows(shard, hc, r) if src is None else src, dst_ref=rows(shard, hc, r),
            send_sem=send_sems.at[k * nc + r], recv_sem=recv_sems.at[k * nc + r], device_id=to, device_id_type=MESH)

    def first(k, p, r):
        return copy(k, me, c, r, (p >> 1, p & 1, c), src=x_ref.at[pl.ds(row0 + c * half + r * ch, ch), :])

    def start():
        for k, p in enumerate(peers):
            for r in range(nc):
                first(k, p, r).start()

    def forward():
        for k, p in enumerate(peers):
            for r in range(nc):
                copy(k, p, c, r, sibling).wait_recv()
                copy(3 + k, p, c, r, sibling).start()

    def finish():
        for k, p in enumerate(peers):
            for r in range(nc):
                copy(3 + k, p, 1 - c, r, sibling).wait_recv()
        for k, p in enumerate(peers):
            for r in range(nc):
                first(k, p, r).wait_send()
                copy(3 + k, p, c, r, sibling).wait_send()

    return start, forward, finish


def _gather_sems(nc):
    return [pltpu.SemaphoreType.DMA((6 * nc,)), pltpu.SemaphoreType.DMA((6 * nc,))]


def _gather_weights(flat, row0, nrows, nc):
    def body(x_ref, out_ref, send_sems, recv_sems):
        start, forward, finish = _gather_plan(x_ref, out_ref, send_sems, recv_sems, row0, nrows, nc)
        start()
        forward()
        finish()

    return pl.pallas_call(
        body, name="gather_weights", in_specs=[_ANY], out_specs=_ANY,
        out_shape=_sds((N_SHARD, ROWS_FLAT, D_MODEL), BF16), scratch_shapes=_gather_sems(nc),
    )(flat)


def _swap_halves(grads):
    half, ch, nc = HALF_FLAT, CHUNK_ROWS, N_CHUNK

    def body(g_ref, out_ref, send_sems, recv_sems):
        x, y, c = _place()
        copies = [pltpu.make_async_remote_copy(
            src_ref=g_ref.at[j, pl.ds((1 - c) * half + r * ch, ch), :], dst_ref=out_ref.at[j, pl.ds(r * ch, ch), :],
            send_sem=send_sems.at[j * nc + r], recv_sem=recv_sems.at[j * nc + r],
            device_id=(x, y, 1 - c), device_id_type=MESH) for j in range(N_SHARD) for r in range(nc)]
        for cp in copies:
            cp.start()
        for cp in copies:
            cp.wait()

    return pl.pallas_call(
        body, name="swap_halves", in_specs=[_ANY], out_specs=_ANY,
        out_shape=_sds((N_SHARD, half, D_MODEL)),
        scratch_shapes=[pltpu.SemaphoreType.DMA((N_SHARD * nc,)), pltpu.SemaphoreType.DMA((N_SHARD * nc,))],
    )(grads)


def _add_my_half(grads, recv):
    tr = 400
    nb = HALF_FLAT // tr
    core = lax.axis_index("c").astype(jnp.int32).reshape(1)

    def body(c_ref, g_ref, r_ref, o_ref, ob_ref):
        acc = g_ref[...] + r_ref[...]
        o_ref[...] = acc
        ob_ref[...] = _bf(acc)

    out = pl.BlockSpec((None, tr, D_MODEL), lambda j, i, c_ref: (j, i, 0))
    return pl.pallas_call(
        body, name="add_my_half",
        grid_spec=pltpu.PrefetchScalarGridSpec(
            num_scalar_prefetch=1, grid=(N_SHARD, nb),
            in_specs=[pl.BlockSpec((None, tr, D_MODEL), lambda j, i, c_ref: (j, c_ref[0] * nb + i, 0)), out],
            out_specs=[out, out]),
        out_shape=[_sds((N_SHARD, HALF_FLAT, D_MODEL)), _sds((N_SHARD, HALF_FLAT, D_MODEL), BF16)],
        compiler_params=_params(("arbitrary", "arbitrary")),
    )(core, grads, recv)


def _scatter_shards(part):
    ch, nc = CHUNK_ROWS, N_CHUNK

    def body(p_ref, out_ref, send_sems, recv_sems):
        x, y, c = _place()
        me = 2 * x + y
        peers = [me ^ k for k in (1, 2, 3)]
        sends = [pltpu.make_async_remote_copy(
            src_ref=p_ref.at[p, pl.ds(r * ch, ch), :], dst_ref=out_ref.at[k, pl.ds(r * ch, ch), :],
            send_sem=send_sems.at[k * nc + r], recv_sem=recv_sems.at[k * nc + r],
            device_id=(p >> 1, p & 1, c), device_id_type=MESH) for k, p in enumerate(peers) for r in range(nc)]
        for cp in sends:
            cp.start()
        for cp in sends:
            cp.wait()

    return pl.pallas_call(
        body, name="scatter_shards", in_specs=[_ANY], out_specs=_ANY,
        out_shape=_sds((3, HALF_FLAT, D_MODEL), BF16),
        scratch_shapes=[pltpu.SemaphoreType.DMA((3 * nc,)), pltpu.SemaphoreType.DMA((3 * nc,))],
    )(part)


def _sum_scattered(part, recv):
    tr = 400
    chip = (2 * lax.axis_index("x") + lax.axis_index("y")).astype(jnp.int32).reshape(1)

    def body(c_ref, p_ref, r_ref, o_ref):
        acc = p_ref[...]
        for k in range(3):
            acc = acc + r_ref[k].astype(F32)
        o_ref[...] = acc

    return pl.pallas_call(
        body, name="sum_scattered",
        grid_spec=pltpu.PrefetchScalarGridSpec(
            num_scalar_prefetch=1, grid=(HALF_FLAT // tr,),
            in_specs=[pl.BlockSpec((None, tr, D_MODEL), lambda i, c_ref: (c_ref[0], i, 0)),
                      pl.BlockSpec((3, tr, D_MODEL), lambda i, c_ref: (0, i, 0))],
            out_specs=pl.BlockSpec((tr, D_MODEL), lambda i, c_ref: (i, 0))),
        out_shape=_sds((HALF_FLAT, D_MODEL)), compiler_params=_params(("arbitrary",)),
    )(chip, part, recv)


def _swap_reduced(mine_half):
    ch, nc = CHUNK_ROWS, N_CHUNK

    def body(r_ref, out_ref, send_sems, recv_sems):
        x, y, c = _place()
        copies = [pltpu.make_async_remote_copy(
            src_ref=r_ref.at[pl.ds(r * ch, ch), :], dst_ref=out_ref.at[pl.ds(r * ch, ch), :],
            send_sem=send_sems.at[r], recv_sem=recv_sems.at[r], device_id=(x, y, 1 - c), device_id_type=MESH)
            for r in range(nc)]
        for cp in copies:
            cp.start()
        for cp in copies:
            cp.wait()

    return pl.pallas_call(
        body, name="swap_reduced", in_specs=[_ANY], out_specs=_ANY,
        out_shape=_sds((HALF_FLAT, D_MODEL)),
        scratch_shapes=[pltpu.SemaphoreType.DMA((nc,)), pltpu.SemaphoreType.DMA((nc,))],
    )(mine_half)


def _allreduce_small(vec):
    def body(v_ref, out_ref, buf, send_sems, recv_sems):
        x, y, c = _place()
        me = 4 * x + 2 * y + c
        buf[me] = v_ref[...]
        peers = [me ^ k for k in range(1, N_DEV)]
        sends = [pltpu.make_async_remote_copy(
            src_ref=v_ref, dst_ref=buf.at[me], send_sem=send_sems.at[k], recv_sem=recv_sems.at[k],
            device_id=(p >> 2, (p >> 1) & 1, p & 1), device_id_type=MESH) for k, p in enumerate(peers)]
        for cp in sends:
            cp.start()
        for k, p in enumerate(peers):
            pltpu.make_async_remote_copy(
                src_ref=v_ref, dst_ref=buf.at[p], send_sem=send_sems.at[k], recv_sem=recv_sems.at[k],
                device_id=(p >> 2, (p >> 1) & 1, p & 1), device_id_type=MESH).wait_recv()
        for cp in sends:
            cp.wait_send()
        acc = buf[0]
        for d in range(1, N_DEV):
            acc = acc + buf[d]
        out_ref[...] = acc

    vm = pl.BlockSpec(memory_space=pltpu.VMEM)
    return pl.pallas_call(
        body, name="allreduce_small", in_specs=[vm], out_specs=vm, out_shape=_sds((SMALL_ROWS, 128)),
        scratch_shapes=[pltpu.VMEM((N_DEV, SMALL_ROWS, 128), F32),
                        pltpu.SemaphoreType.DMA((N_DEV - 1,)), pltpu.SemaphoreType.DMA((N_DEV - 1,))],
    )(vec)


def _flatten_shard(w_in, w_out, w_pg, w_pp):
    return jnp.concatenate([w_in.reshape(-1, D_MODEL), w_out.reshape(-1, D_MODEL), w_pg.reshape(-1, D_MODEL),
                            w_pp.reshape(-1, D_MODEL)], axis=0)


def _unflatten_shard(flat):
    a, b, c = ROWS_W_IN, ROWS_W_IN + ROWS_W_OUT, ROWS_W_IN + ROWS_W_OUT + ROWS_W_PG
    q = D_MODEL // N_SHARD
    return (flat[:a].reshape(2, D_MODEL, D_MODEL), flat[a:b].reshape(2, q, D_MODEL),
            flat[b:c].reshape(2, q, D_MODEL), flat[c:].reshape(2, D_PLE, q))


def _full_w_pp(gathered):
    c = ROWS_W_IN + ROWS_W_OUT + ROWS_W_PG
    q = D_MODEL // N_SHARD
    rpp = ROWS_W_PP // 2
    return [gathered[:, c + l * rpp:c + (l + 1) * rpp, :].reshape(N_SHARD, D_PLE, q).transpose(1, 0, 2)
            .reshape(D_PLE, D_MODEL) for l in range(2)]


def _shard_major(dw_in, dw_out, dw_pg, dw_pp):
    q = D_MODEL // N_SHARD
    rpp = ROWS_W_PP // 2
    parts = []
    parts.extend(dw_in)
    for l in range(2):
        parts.append(dw_out[l].reshape(N_SHARD, q, D_MODEL))
    for l in range(2):
        parts.append(dw_pg[l].reshape(N_SHARD, q, D_MODEL))
    for l in range(2):
        parts.append(dw_pp[l].reshape(D_PLE, N_SHARD, q).transpose(1, 0, 2).reshape(N_SHARD, rpp, D_MODEL))
    return jnp.concatenate(parts, axis=1)


def _lower_bounds(lb_logits):
    sm = jax.nn.softmax(lb_logits.astype(F32), axis=0)
    return jnp.cumsum(sm, axis=0) - sm[0:1]


def kernel(x, p, norm_mix, w_in, a_out_norm, b_out_norm, w_out, lb_logits, ple_gate_norm, w_ple_gate, w_ple_proj, ple_post_norm, final_norm, loss_target, m_norm_mix, m_w_in, m_a_out_norm, m_b_out_norm, m_w_out, m_lb_logits, m_ple_gate_norm, m_w_ple_gate, m_w_ple_proj, m_ple_post_norm, m_final_norm, v_norm_mix, v_w_in, v_a_out_norm, v_b_out_norm, v_w_out, v_lb_logits, v_ple_gate_norm, v_w_ple_gate, v_w_ple_proj, v_ple_post_norm, v_final_norm):
    t = x.shape[1]
    h0 = x.reshape(t, D_MODEL)
    target = loss_target.reshape(t, D_MODEL)
    pl_in = p.reshape(2, t, D_PLE)

    w_flat_bf = _flatten_shard(_bf(w_in), _bf(w_out), _bf(w_ple_gate), _bf(w_ple_proj))
    chip = 2 * lax.axis_index("x") + lax.axis_index("y")
    gathered = lax.dynamic_update_slice(_gather_weights(w_flat_bf, 0, D_MODEL, 2), w_flat_bf[None], (chip, 0, 0))
    lbs, lbs_vjp = jax.vjp(_lower_bounds, lb_logits)

    saved = []
    h = h0
    for l in range(2):
        g_mix = norm_mix[l].reshape(1, D_MODEL)
        lb = lbs[l].reshape(1, GROUP)
        ga = a_out_norm[l].reshape(1, GROUP)
        gb = b_out_norm[l].reshape(1, GROUP)
        proj, proj_bf = _inproj(h, g_mix, gathered, l)
        ya, states = _hgrn_fwd(proj, lb, ga)
        if l == 0:
            ob, yb, gathered = _sb_fwd(proj_bf, proj, gb, (w_flat_bf, gathered, D_MODEL, ROWS_FLAT - D_MODEL, 4))
            w_pps = _full_w_pp(gathered)
        else:
            ob, yb = _sb_fwd(proj_bf, proj, gb)
        h1 = _outproj(h, ya, yb, gathered, l)
        g_post = ple_post_norm[l].reshape(1, D_MODEL)
        g_gate = ple_gate_norm[l].reshape(1, D_MODEL)
        h2 = _ple_fwd(h1, pl_in[l], w_pps[l], gathered, l, g_post, g_gate)
        saved.append((h, proj, proj_bf, states, ya, yb, ob, h1))
        h = h2

    dh, d_final, loss_part = _final(h, final_norm.reshape(1, D_MODEL), target)

    dw_in_l, dw_out_l, dw_pg_l, dw_pp_l = [None] * 2, [None] * 2, [None] * 2, [None] * 2
    d_mix, d_a, d_b, d_lb, d_gate, d_post = [None] * 2, [None] * 2, [None] * 2, [None] * 2, [None] * 2, [None] * 2
    for l in (1, 0):
        h_in, proj, proj_bf, states, ya, yb, ob, h1 = saved[l]
        g_mix = norm_mix[l].reshape(1, D_MODEL)
        lb = lbs[l].reshape(1, GROUP)
        ga = a_out_norm[l].reshape(1, GROUP)
        gb = b_out_norm[l].reshape(1, GROUP)
        g_post = ple_post_norm[l].reshape(1, D_MODEL)
        g_gate = ple_gate_norm[l].reshape(1, D_MODEL)
        dh1, dw_pg_l[l], dw_pp_l[l], d_gate[l], d_post[l] = _ple_bwd(dh, h1, pl_in[l], w_pps[l], gathered, l, g_post,
                                                                     g_gate)
        dya, dyb, dw_out_l[l] = _outproj_bwd(dh1, ya, yb, gathered, l)
        dbq, dbk, dbv, dbg, d_b[l] = _sb_bwd(proj_bf, proj, ob, dyb, gb)
        da, d_lb[l], d_a[l] = _hgrn_bwd(proj, lb, ga, states, dya)
        db = jnp.stack([dbq, dbk * LN2, dbv, dbg]).astype(BF16)
        dw_in_l[l] = _inproj_bwd_dw(h_in, g_mix, da, db)
        dh, d_mix[l] = _inproj_bwd_dx(dh1, h_in, g_mix, gathered, l, da, db)
    grad_x = dh.reshape(x.shape)

    g_full = _shard_major(dw_in_l, dw_out_l, dw_pg_l, dw_pp_l)
    chip_sum, chip_sum_bf = _add_my_half(g_full, _swap_halves(g_full))
    mine_half = _sum_scattered(chip_sum, _scatter_shards(chip_sum_bf))
    other_half = _swap_reduced(mine_half)
    south = lax.axis_index("c") == 0
    g_flat = jnp.concatenate([jnp.where(south, mine_half, other_half), jnp.where(south, other_half, mine_half)])
    g_w_in, g_w_out, g_w_pg, g_w_pp = _unflatten_shard(g_flat)

    small = jnp.concatenate([
        jnp.concatenate(d_mix).reshape(-1, 128), jnp.concatenate(d_a).reshape(-1, 128),
        jnp.concatenate(d_b).reshape(-1, 128), jnp.concatenate(d_lb).reshape(-1, 128),
        jnp.concatenate(d_gate).reshape(-1, 128), jnp.concatenate(d_post).reshape(-1, 128),
        d_final.reshape(-1, 128), jnp.broadcast_to(loss_part, (8, 128))], axis=0)
    small = _allreduce_small(small)
    loss = small[80, 0]
    g_norm_mix = small[0:16].reshape(2, D_MODEL)
    g_a = small[16:24].reshape(2, GROUP)
    g_b = small[24:32].reshape(2, GROUP)
    (g_lb,) = lbs_vjp(small[32:40].reshape(2, GROUP))
    g_gate = small[40:56].reshape(2, D_MODEL)
    g_post = small[56:72].reshape(2, D_MODEL)
    g_final = small[72:80].reshape(D_MODEL)

    def adam_matrix(w, g, m, v):
        d, nm, nv = _adamw(w.reshape(-1, D_MODEL), g.reshape(-1, D_MODEL), m.reshape(-1, D_MODEL), v.reshape(-1, D_MODEL))
        return d.reshape(w.shape), nm.reshape(w.shape), nv.reshape(w.shape)

    d_w_in, nm_w_in, nv_w_in = adam_matrix(w_in, g_w_in, m_w_in, v_w_in)
    d_w_out, nm_w_out, nv_w_out = adam_matrix(w_out, g_w_out, m_w_out, v_w_out)
    d_w_pg, nm_w_pg, nv_w_pg = adam_matrix(w_ple_gate, g_w_pg, m_w_ple_gate, v_w_ple_gate)
    d_w_pp, nm_w_pp, nv_w_pp = adam_matrix(w_ple_proj, g_w_pp, m_w_ple_proj, v_w_ple_proj)

    small_w = [norm_mix, a_out_norm, b_out_norm, lb_logits, ple_gate_norm, ple_post_norm, final_norm]
    small_g = [g_norm_mix, g_a, g_b, g_lb, g_gate, g_post, g_final]
    small_m = [m_norm_mix, m_a_out_norm, m_b_out_norm, m_lb_logits, m_ple_gate_norm, m_ple_post_norm, m_final_norm]
    small_v = [v_norm_mix, v_a_out_norm, v_b_out_norm, v_lb_logits, v_ple_gate_norm, v_ple_post_norm, v_final_norm]
    pack = lambda arrs: jnp.concatenate([a.reshape(-1, 128) for a in arrs], axis=0)
    ds, nms, nvs = _adamw(pack(small_w), pack(small_g), pack(small_m), pack(small_v))

    def unpack(packed):
        out, r = [], 0
        for a in small_w:
            n = a.size // 128
            out.append(packed[r:r + n].reshape(a.shape))
            r += n
        return out

    d_s, nm_s, nv_s = unpack(ds), unpack(nms), unpack(nvs)

    def ordered(s, big):
        return [s[0], big[0], s[1], s[2], big[1], s[3], s[4], big[2], big[3], s[5], s[6]]

    grads = ordered(small_g, [g_w_in, g_w_out, g_w_pg, g_w_pp])
    deltas = ordered(d_s, [d_w_in, d_w_out, d_w_pg, d_w_pp])
    new_m = ordered(nm_s, [nm_w_in, nm_w_out, nm_w_pg, nm_w_pp])
    new_v = ordered(nv_s, [nv_w_in, nv_w_out, nv_w_pg, nv_w_pp])
    return (loss, grad_x, *grads, *deltas, *new_m, *new_v)
```

```python
import functools
import math

import numpy as np
import jax
import jax.numpy as jnp
from jax import lax
from jax.experimental import pallas as pl
from jax.experimental.pallas import tpu as pltpu

F32 = jnp.float32
BF16 = jnp.bfloat16
MESH = pl.DeviceIdType.MESH

D_MODEL = 1024
D_PLE = 256
D_IN = 4096
A_HEADS, A_D = 4, 128
B_HEADS, B_D = 8, 64
GROUP = 512
EPS = 1e-6
N_SHARD = 4
N_DEV = 8

HG_CHUNK = 128
HG_LEVELS = 7
SB_TQ = 1024
SB_TK = 128
LOG2E = 1.4426950408889634
LN2 = 0.6931471805599453

ADAM_LR, ADAM_B1, ADAM_B2, ADAM_EPS, ADAM_WD, ADAM_STEP = 0.001, 0.9, 0.999, 1e-08, 0.01, 10

VMEM_LIMIT = 48 * 1024 * 1024
VMEM_LIMIT_BIG = 58 * 1024 * 1024

ROWS_W_IN = 2 * D_MODEL
ROWS_W_OUT = 2 * (D_MODEL // N_SHARD)
ROWS_W_PG = 2 * (D_MODEL // N_SHARD)
ROWS_W_PP = 2 * (D_PLE * (D_MODEL // N_SHARD) // D_MODEL)
ROWS_FLAT = ROWS_W_IN + ROWS_W_OUT + ROWS_W_PG + ROWS_W_PP
HALF_FLAT = ROWS_FLAT // 2
N_CHUNK = 10
CHUNK_ROWS = HALF_FLAT // N_CHUNK

ROWS_LAYER = ROWS_FLAT // 2
HALF_LAYER = ROWS_LAYER // 2
RS_CHUNKS = HALF_LAYER // CHUNK_ROWS

SMALL_ROWS = 88


def _sds(shape, dtype=F32):
    return jax.ShapeDtypeStruct(shape, dtype)


def _params(sem=None, vmem_limit=VMEM_LIMIT):
    kw = dict(vmem_limit_bytes=vmem_limit)
    if sem is not None:
        kw["dimension_semantics"] = sem
    return pltpu.CompilerParams(**kw)


def _dot(a, b, precision=None):
    return lax.dot_general(a, b, (((1,), (0,)), ((), ())), preferred_element_type=F32, precision=precision)


def _dot_nt(a, b, precision=None):
    return lax.dot_general(a, b, (((1,), (1,)), ((), ())), preferred_element_type=F32, precision=precision)


def _dot_tn(a, b, precision=None):
    return lax.dot_general(a, b, (((0,), (0,)), ((), ())), preferred_element_type=F32, precision=precision)


def _bf(x):
    return x.astype(BF16)


def _split(x):
    hi = x.astype(BF16)
    lo = (x - hi.astype(F32)).astype(BF16)
    return hi, lo


def _rms(x):
    r = lax.rsqrt(jnp.mean(x * x, axis=-1, keepdims=True) + EPS)
    return x * r, r


def _rms_bwd(dxh, xh, r):
    return r * (dxh - xh * jnp.mean(dxh * xh, axis=-1, keepdims=True))


def _sigmoid(x):
    return 1.0 / (1.0 + jnp.exp(-x))


def _silu_grad(x, sig):
    return sig * (1.0 + x * (1.0 - sig))


def _row_tile(t, want):
    return min(t, want)


def _inproj(h, g, gathered, layer):
    t = h.shape[0]
    tm = _row_tile(t, 512)
    tn = D_MODEL
    scale = jnp.ones((1, D_IN), F32).at[:, 4 * GROUP:5 * GROUP].set(B_D ** -0.5 * LOG2E)

    def body(h_ref, g_ref, w_ref, s_ref, o_ref, ob_ref):
        xh, _ = _rms(h_ref[...])
        acc = _dot(_bf(xh * g_ref[...]), w_ref[...])
        o_ref[...] = acc
        ob_ref[...] = _bf(acc * s_ref[...])

    return pl.pallas_call(
        body, name="inproj", grid=(D_IN // tn, t // tm),
        in_specs=[pl.BlockSpec((tm, D_MODEL), lambda j, i: (i, 0)),
                  pl.BlockSpec((1, D_MODEL), lambda j, i: (0, 0)),
                  pl.BlockSpec((None, D_MODEL, tn), lambda j, i: (j, layer, 0)),
                  pl.BlockSpec((1, tn), lambda j, i: (0, j))],
        out_specs=[pl.BlockSpec((tm, tn), lambda j, i: (i, j)), pl.BlockSpec((tm, tn), lambda j, i: (i, j))],
        out_shape=[_sds((t, D_IN)), _sds((t, D_IN), BF16)], compiler_params=_params(("arbitrary", "arbitrary")),
    )(h, g, gathered, scale)


def _rows_spec(first_row):
    q = D_MODEL // N_SHARD
    return pl.BlockSpec((N_SHARD, q, D_MODEL), lambda i: (0, first_row // q, 0))


def _outproj(h, ya, yb, gathered, layer):
    t = h.shape[0]
    tm = _row_tile(t, 512)

    def body(h_ref, ya_ref, yb_ref, w_ref, o_ref):
        o_ref[...] = (h_ref[...] + _dot(_bf(ya_ref[...]), w_ref[0:2].reshape(GROUP, D_MODEL))
                      + _dot(_bf(yb_ref[...]), w_ref[2:4].reshape(GROUP, D_MODEL)))

    return pl.pallas_call(
        body, name="outproj", grid=(t // tm,),
        in_specs=[pl.BlockSpec((tm, D_MODEL), lambda i: (i, 0)),
                  pl.BlockSpec((tm, GROUP), lambda i: (i, 0)),
                  pl.BlockSpec((tm, GROUP), lambda i: (i, 0)),
                  _rows_spec(ROWS_W_IN + layer * (D_MODEL // N_SHARD))],
        out_specs=pl.BlockSpec((tm, D_MODEL), lambda i: (i, 0)),
        out_shape=_sds((t, D_MODEL)), compiler_params=_params(("arbitrary",)),
    )(h, ya, yb, gathered)


def _ple_fwd(h, p, w_pp, gathered, layer, g_post, g_gate):
    t = h.shape[0]
    tm = _row_tile(t, 256)

    def body(h_ref, p_ref, wpp_ref, wpg_ref, gp_ref, gg_ref, o_ref):
        x = h_ref[...]
        ph, _ = _rms(_dot(_bf(p_ref[...]), wpp_ref[...]))
        xh, _ = _rms(x)
        gate = _sigmoid(_dot(_bf(xh * gg_ref[...]), wpg_ref[...].reshape(D_MODEL, D_MODEL)))
        o_ref[...] = x + gate * (ph * gp_ref[...])

    return pl.pallas_call(
        body, name="ple_fwd", grid=(t // tm,),
        in_specs=[pl.BlockSpec((tm, D_MODEL), lambda i: (i, 0)),
                  pl.BlockSpec((tm, D_PLE), lambda i: (i, 0)),
                  pl.BlockSpec((D_PLE, D_MODEL), lambda i: (0, 0)),
                  _rows_spec(ROWS_W_IN + ROWS_W_OUT + layer * (D_MODEL // N_SHARD)),
                  pl.BlockSpec((1, D_MODEL), lambda i: (0, 0)),
                  pl.BlockSpec((1, D_MODEL), lambda i: (0, 0))],
        out_specs=pl.BlockSpec((tm, D_MODEL), lambda i: (i, 0)),
        out_shape=_sds((t, D_MODEL)), compiler_params=_params(("arbitrary",)),
    )(h, p, w_pp, gathered, g_post, g_gate)


def _ple_bwd(dh2, h, p, w_pp, gathered, layer, g_post, g_gate, exchange=None):
    t = h.shape[0]
    tm = _row_tile(t, 256)

    def body(d_ref, h_ref, p_ref, wpp_ref, wpg_ref, gp_ref, gg_ref, dh_ref, dwpg_ref, dwpp_ref, dgg_ref, dgp_ref):
        @pl.when(pl.program_id(0) == 0)
        def _():
            dwpg_ref[...] = jnp.zeros_like(dwpg_ref)
            dwpp_ref[...] = jnp.zeros_like(dwpp_ref)
            dgg_ref[...] = jnp.zeros_like(dgg_ref)
            dgp_ref[...] = jnp.zeros_like(dgp_ref)

        d = d_ref[...]
        x = h_ref[...]
        gp = gp_ref[...]
        gg = gg_ref[...]
        pb = _bf(p_ref[...])
        ph, rp = _rms(_dot(pb, wpp_ref[...]))
        pe = ph * gp
        xh, rx = _rms(x)
        un = _bf(xh * gg)
        wpg = wpg_ref[...].reshape(D_MODEL, D_MODEL)
        gate = _sigmoid(_dot(un, wpg))
        dgpre = _bf(d * pe * gate * (1.0 - gate))
        dun = _dot_nt(dgpre, wpg)
        dh_ref[...] = d + _rms_bwd(dun * gg, xh, rx)
        dgg_ref[...] += jnp.sum(dun * xh, axis=0, keepdims=True)
        dwpg_ref[...] += _dot_tn(un, dgpre)
        dpe = d * gate
        dgp_ref[...] += jnp.sum(dpe * ph, axis=0, keepdims=True)
        dwpp_ref[...] += _dot_tn(pb, _bf(_rms_bwd(dpe * gp, ph, rp)))

    in_specs = [pl.BlockSpec((tm, D_MODEL), lambda i: (i, 0)),
                pl.BlockSpec((tm, D_MODEL), lambda i: (i, 0)),
                pl.BlockSpec((tm, D_PLE), lambda i: (i, 0)),
                pl.BlockSpec((D_PLE, D_MODEL), lambda i: (0, 0)),
                _rows_spec(ROWS_W_IN + ROWS_W_OUT + layer * (D_MODEL // N_SHARD)),
                pl.BlockSpec((1, D_MODEL), lambda i: (0, 0)),
                pl.BlockSpec((1, D_MODEL), lambda i: (0, 0))]
    out_specs = [pl.BlockSpec((tm, D_MODEL), lambda i: (i, 0)),
                 pl.BlockSpec((D_MODEL, D_MODEL), lambda i: (0, 0)),
                 pl.BlockSpec((D_PLE, D_MODEL), lambda i: (0, 0)),
                 pl.BlockSpec((1, D_MODEL), lambda i: (0, 0)),
                 pl.BlockSpec((1, D_MODEL), lambda i: (0, 0))]
    out_shape = [_sds((t, D_MODEL)), _sds((D_MODEL, D_MODEL)), _sds((D_PLE, D_MODEL)),
                 _sds((1, D_MODEL)), _sds((1, D_MODEL))]
    operands = [dh2, h, p, w_pp, gathered, g_post, g_gate]
    scratch = []
    if exchange is not None:
        body = _with_exchange(body, 7, 5, exchange, lambda: pl.program_id(0), t // tm)
        xi, xo, xs, scratch, xop = _exchange_args(exchange)
        in_specs, out_specs, out_shape, operands = in_specs + xi, out_specs + xo, out_shape + xs, operands + xop
    return pl.pallas_call(
        body, name="ple_bwd" if exchange is None else "ple_bwd_exchange", grid=(t // tm,),
        in_specs=in_specs, out_specs=out_specs, out_shape=out_shape, scratch_shapes=scratch,
        compiler_params=_params(("arbitrary",)),
    )(*operands)


def _outproj_bwd(dh, ya, yb, gathered, layer):
    t = dh.shape[0]
    tm = _row_tile(t, 512)

    def body(d_ref, ya_ref, yb_ref, w_ref, dya_ref, dyb_ref, dw_ref):
        @pl.when(pl.program_id(0) == 0)
        def _():
            dw_ref[...] = jnp.zeros_like(dw_ref)

        d = _bf(d_ref[...])
        dya_ref[...] = _dot_nt(d, w_ref[0:2].reshape(GROUP, D_MODEL))
        dyb_ref[...] = _dot_nt(d, w_ref[2:4].reshape(GROUP, D_MODEL))
        dw_ref[pl.ds(0, GROUP), :] += _dot_tn(_bf(ya_ref[...]), d)
        dw_ref[pl.ds(GROUP, GROUP), :] += _dot_tn(_bf(yb_ref[...]), d)

    return pl.pallas_call(
        body, name="outproj_bwd", grid=(t // tm,),
        in_specs=[pl.BlockSpec((tm, D_MODEL), lambda i: (i, 0)),
                  pl.BlockSpec((tm, GROUP), lambda i: (i, 0)),
                  pl.BlockSpec((tm, GROUP), lambda i: (i, 0)),
                  _rows_spec(ROWS_W_IN + layer * (D_MODEL // N_SHARD))],
        out_specs=[pl.BlockSpec((tm, GROUP), lambda i: (i, 0)),
                   pl.BlockSpec((tm, GROUP), lambda i: (i, 0)),
                   pl.BlockSpec((D_MODEL, D_MODEL), lambda i: (0, 0))],
        out_shape=[_sds((t, GROUP)), _sds((t, GROUP)), _sds((D_MODEL, D_MODEL))],
        compiler_params=_params(("arbitrary",)),
    )(dh, ya, yb, gathered)


def _inproj_bwd_dx(dres, h, g, gathered, layer, da, db):
    t = h.shape[0]
    tm = _row_tile(t, 256)

    def body(dres_ref, h_ref, g_ref, w_ref, da_ref, db_ref, dh_ref, dg_ref):
        @pl.when(pl.program_id(0) == 0)
        def _():
            dg_ref[...] = jnp.zeros_like(dg_ref)

        du = jnp.zeros((tm, D_MODEL), F32)
        for i in range(8):
            part = da_ref[i] if i < 4 else db_ref[i - 4]
            du = du + _dot_nt(part, w_ref[i // 2, :, pl.ds((i % 2) * GROUP, GROUP)])
        xh, r = _rms(h_ref[...])
        dg_ref[...] += jnp.sum(du * xh, axis=0, keepdims=True)
        dh_ref[...] = dres_ref[...] + _rms_bwd(du * g_ref[...], xh, r)

    return pl.pallas_call(
        body, name="inproj_bwd_dx", grid=(t // tm,),
        in_specs=[pl.BlockSpec((tm, D_MODEL), lambda i: (i, 0)),
                  pl.BlockSpec((tm, D_MODEL), lambda i: (i, 0)),
                  pl.BlockSpec((1, D_MODEL), lambda i: (0, 0)),
                  pl.BlockSpec((N_SHARD, D_MODEL, D_MODEL), lambda i: (0, layer, 0)),
                  pl.BlockSpec((4, tm, GROUP), lambda i: (0, i, 0)),
                  pl.BlockSpec((4, tm, GROUP), lambda i: (0, i, 0))],
        out_specs=[pl.BlockSpec((tm, D_MODEL), lambda i: (i, 0)),
                   pl.BlockSpec((1, D_MODEL), lambda i: (0, 0))],
        out_shape=[_sds((t, D_MODEL)), _sds((1, D_MODEL))],
        compiler_params=_params(("arbitrary",)),
    )(dres, h, g, gathered, da, db)


def _inproj_bwd_dw(h, g, da, db):
    t = h.shape[0]
    tm = _row_tile(t, 512)

    def body(h_ref, g_ref, da_ref, db_ref, dw_ref):
        @pl.when(pl.program_id(0) == 0)
        def _():
            dw_ref[...] = jnp.zeros_like(dw_ref)

        xh, _ = _rms(h_ref[...])
        u = _bf(xh * g_ref[...])
        for i in range(8):
            dw_ref[i // 2, :, pl.ds((i % 2) * GROUP, GROUP)] += _dot_tn(u, da_ref[i] if i < 4 else db_ref[i - 4])

    return pl.pallas_call(
        body, name="inproj_bwd_dw", grid=(t // tm,),
        in_specs=[pl.BlockSpec((tm, D_MODEL), lambda i: (i, 0)),
                  pl.BlockSpec((1, D_MODEL), lambda i: (0, 0)),
                  pl.BlockSpec((4, tm, GROUP), lambda i: (0, i, 0)),
                  pl.BlockSpec((4, tm, GROUP), lambda i: (0, i, 0))],
        out_specs=pl.BlockSpec((N_SHARD, D_MODEL, D_MODEL), lambda i: (0, 0, 0)),
        out_shape=_sds((N_SHARD, D_MODEL, D_MODEL)), compiler_params=_params(("arbitrary",), VMEM_LIMIT_BIG),
    )(h, g, da, db)


def _final(h, g, target):
    t = h.shape[0]
    tm = _row_tile(t, 512)

    def body(h_ref, g_ref, t_ref, dh_ref, dg_ref, loss_ref):
        @pl.when(pl.program_id(0) == 0)
        def _():
            dg_ref[...] = jnp.zeros_like(dg_ref)
            loss_ref[...] = jnp.zeros_like(loss_ref)

        xh, r = _rms(h_ref[...])
        gg = g_ref[...]
        err = xh * gg - t_ref[...]
        part = 0.5 * jnp.sum(jnp.mean(err * err, axis=-1, keepdims=True), axis=0, keepdims=True)
        loss_ref[...] += jnp.broadcast_to(part, loss_ref.shape)
        dy = err * (1.0 / D_MODEL)
        dg_ref[...] += jnp.sum(dy * xh, axis=0, keepdims=True)
        dh_ref[...] = _rms_bwd(dy * gg, xh, r)

    return pl.pallas_call(
        body, name="final", grid=(t // tm,),
        in_specs=[pl.BlockSpec((tm, D_MODEL), lambda i: (i, 0)),
                  pl.BlockSpec((1, D_MODEL), lambda i: (0, 0)),
                  pl.BlockSpec((tm, D_MODEL), lambda i: (i, 0))],
        out_specs=[pl.BlockSpec((tm, D_MODEL), lambda i: (i, 0)),
                   pl.BlockSpec((1, D_MODEL), lambda i: (0, 0)),
                   pl.BlockSpec((1, 128), lambda i: (0, 0))],
        out_shape=[_sds((t, D_MODEL)), _sds((1, D_MODEL)), _sds((1, 128))],
        compiler_params=_params(("arbitrary",)),
    )(h, g, target)


def _hgrn_consts():
    c, nl = HG_CHUNK, HG_LEVELS
    t = np.arange(c)
    tril = np.tril(np.ones((c, c), np.float32))
    masks = np.zeros((nl + 1, c, c), np.float32)
    masks[0] = np.eye(c, dtype=np.float32)
    dmat = np.zeros(((nl + 2) * c, c), np.float32)
    dmat[0:c] = tril
    for l in range(nl):
        m = c >> (l + 1)
        blk = t // (2 * m)
        r = blk * 2 * m + m - 1
        upper = (t % (2 * m)) >= m
        masks[l + 1] = ((blk[:, None] == blk[None, :]) & upper[:, None] & (~upper)[None, :]).astype(np.float32)
        dmat[(l + 1) * c:(l + 2) * c] = tril[t] - tril[r]
    dmat[(nl + 1) * c:] = np.triu(np.ones((c, c), np.float32), k=1)
    return jnp.asarray(masks), jnp.asarray(dmat, BF16)


HG_HEADS = 4


def _hgrn_pre(aq, af, lb):
    sq = _sigmoid(aq)
    sneg = _sigmoid(-af)
    kk = (1.0 - lb) * sneg
    return sq, aq * sq, sneg, kk, jnp.log1p(-kk)


def _hgrn_x(logf, dmat_ref):
    dm = dmat_ref[pl.ds(0, (HG_LEVELS + 1) * HG_CHUNK), :]
    lhi, llo = _split(logf)
    return _dot(dm, lhi) + _dot(dm, llo)


def _hgrn_level(x_all, l, q, kk):
    c = HG_CHUNK
    x = x_all[(l + 1) * c:(l + 2) * c]
    qf = jnp.exp(jnp.minimum(x, 0.0))
    kf = jnp.exp(-jnp.maximum(x, 0.0))
    return qf, kf, _bf(q * qf), _bf(kk * kf)


def _hgrn_scores(xs, qs, kks, mask_ref):
    ps = [mask_ref[0] * _dot_nt(_bf(q), _bf(kk)) for q, kk in zip(qs, kks)]
    for l in range(HG_LEVELS):
        for i, (x_all, q, kk) in enumerate(zip(xs, qs, kks)):
            _, _, ql, kl = _hgrn_level(x_all, l, q, kk)
            ps[i] = ps[i] + mask_ref[l + 1] * _dot_nt(ql, kl)
    return ps


def _hgrn_specs(n_chunks, rev):
    c, w = HG_CHUNK, HG_HEADS * A_D
    cidx = (lambda n: n_chunks - 1 - n) if rev else (lambda n: n)
    col = lambda g: pl.BlockSpec((c, w), lambda h, n: (cidx(n), g * (A_HEADS // HG_HEADS) + h))
    vec = pl.BlockSpec((1, w), lambda h, n: (0, h))
    mask = pl.BlockSpec((HG_LEVELS + 1, c, c), lambda h, n: (0, 0, 0))
    dmat = pl.BlockSpec(((HG_LEVELS + 2) * c, c), lambda h, n: (0, 0))
    state = pl.BlockSpec((HG_HEADS, None, A_D, A_D), lambda h, n: (h, cidx(n), 0, 0))
    return cidx, col, vec, mask, dmat, state


def _lanes(i):
    return pl.ds(i * A_D, A_D)


def _hgrn_fwd(proj, lb, gain):
    t = proj.shape[0]
    c = HG_CHUNK
    nch = t // c
    masks, dmat = _hgrn_consts()
    cidx, col, vec, mask_spec, dmat_spec, state_spec = _hgrn_specs(nch, False)
    heads = range(HG_HEADS)

    def body(aq_ref, af_ref, ai_ref, ag_ref, lb_ref, gain_ref, mask_ref, dmat_ref, y_ref, st_ref, s_scr):
        @pl.when(pl.program_id(1) == 0)
        def _():
            s_scr[...] = jnp.zeros_like(s_scr)

        pre = [_hgrn_pre(aq_ref[:, _lanes(i)], af_ref[:, _lanes(i)], lb_ref[:, _lanes(i)]) for i in heads]
        qs, kks = [p[1] for p in pre], [p[3] for p in pre]
        xs = [_hgrn_x(p[4], dmat_ref) for p in pre]
        bs = [x[0:c] for x in xs]
        b_lasts = [jnp.sum(p[4], axis=0, keepdims=True) for p in pre]
        ps = _hgrn_scores(xs, qs, kks, mask_ref)
        ss = [s_scr[i] for i in heads]
        vbs = [_bf(ai_ref[:, _lanes(i)]) for i in heads]
        os_ = [_dot(_bf(ps[i]), vbs[i]) + _dot_nt(_bf(qs[i] * jnp.exp(bs[i])), _bf(ss[i])) for i in heads]
        for i in heads:
            st_ref[i] = ss[i]
            s_scr[i] = ss[i] * jnp.exp(b_lasts[i]) + _dot_tn(vbs[i], _bf(kks[i] * jnp.exp(b_lasts[i] - bs[i])))
            oh, _ = _rms(os_[i])
            ag = ag_ref[:, _lanes(i)]
            y_ref[:, _lanes(i)] = oh * gain_ref[:, _lanes(i)] * (ag * _sigmoid(ag))

    return pl.pallas_call(
        body, name="hgrn_fwd", grid=(A_HEADS // HG_HEADS, nch),
        in_specs=[col(0), col(1), col(2), col(3), vec, vec, mask_spec, dmat_spec],
        out_specs=[pl.BlockSpec((c, HG_HEADS * A_D), lambda h, n: (n, h)), state_spec],
        out_shape=[_sds((t, GROUP)), _sds((A_HEADS, nch, A_D, A_D))],
        scratch_shapes=[pltpu.VMEM((HG_HEADS, A_D, A_D), F32)],
        compiler_params=_params(("arbitrary", "arbitrary")),
    )(proj, proj, proj, proj, lb, gain, masks, dmat)


def _hgrn_bwd(proj, lb, gain, states, dya):
    t = proj.shape[0]
    c, nl = HG_CHUNK, HG_LEVELS
    nch = t // c
    masks, dmat = _hgrn_consts()
    cidx, col, vec, mask_spec, dmat_spec, state_spec = _hgrn_specs(nch, True)
    heads = range(HG_HEADS)

    def body(aq_ref, af_ref, ai_ref, ag_ref, lb_ref, gain_ref, mask_ref, dmat_ref, st_ref, dy_ref,
             da_ref, dlb_ref, dgain_ref, ds_scr, z_scr):
        @pl.when(pl.program_id(1) == 0)
        def _():
            ds_scr[...] = jnp.zeros_like(ds_scr)
            dlb_ref[...] = jnp.zeros_like(dlb_ref)
            dgain_ref[...] = jnp.zeros_like(dgain_ref)

        aqs = [aq_ref[:, _lanes(i)] for i in heads]
        lbs = [lb_ref[:, _lanes(i)] for i in heads]
        pre = [_hgrn_pre(aqs[i], af_ref[:, _lanes(i)], lbs[i]) for i in heads]
        sqs, qs, snegs, kks = ([p[j] for p in pre] for j in range(4))
        xs = [_hgrn_x(p[4], dmat_ref) for p in pre]
        bs = [x[0:c] for x in xs]
        b_lasts = [jnp.sum(p[4], axis=0, keepdims=True) for p in pre]
        ebs = [jnp.exp(b) for b in bs]
        ebls = [jnp.exp(bl - b) for bl, b in zip(b_lasts, bs)]
        ebl_rows = [jnp.exp(bl) for bl in b_lasts]
        qes = [_bf(q * eb) for q, eb in zip(qs, ebs)]
        kes = [_bf(kk * ebl) for kk, ebl in zip(kks, ebls)]
        vbs = [_bf(ai_ref[:, _lanes(i)]) for i in heads]
        ss = [st_ref[i] for i in heads]
        sbs = [_bf(s) for s in ss]
        dss = [ds_scr[i] for i in heads]
        dsbs = [_bf(ds) for ds in dss]

        pbs = [_bf(p) for p in _hgrn_scores(xs, qs, kks, mask_ref)]
        os_ = [_dot(pbs[i], vbs[i]) + _dot_nt(qes[i], sbs[i]) for i in heads]

        dos = []
        for i in heads:
            ag, gain, dy = ag_ref[:, _lanes(i)], gain_ref[:, _lanes(i)], dy_ref[:, _lanes(i)]
            oh, r = _rms(os_[i])
            sg_sig = _sigmoid(ag)
            sg = ag * sg_sig
            da_ref[3, :, _lanes(i)] = _bf(dy * oh * gain * _silu_grad(ag, sg_sig))
            dgain_ref[:, _lanes(i)] += jnp.sum(dy * oh * sg, axis=0, keepdims=True)
            dos.append(_bf(_rms_bwd(dy * gain * sg, oh, r)))

        dps = [_dot_nt(dos[i], vbs[i]) for i in heads]
        for i in heads:
            da_ref[2, :, _lanes(i)] = _bf(_dot_tn(pbs[i], dos[i]) + _dot_nt(kes[i], dsbs[i]))
        dq_ss = [ebs[i] * _dot(dos[i], sbs[i]) for i in heads]
        dk_ss = [ebls[i] * _dot(vbs[i], dsbs[i]) for i in heads]
        dqs, dks = [], []
        for i in heads:
            dpd = jnp.sum(mask_ref[0] * dps[i], axis=1, keepdims=True)
            z_scr[i, pl.ds(0, c), :] = qs[i] * dq_ss[i]
            z_scr[i, pl.ds((nl + 1) * c, c), :] = kks[i] * dk_ss[i]
            dqs.append(dq_ss[i] + dpd * kks[i])
            dks.append(dk_ss[i] + dpd * qs[i])
        for l in range(nl):
            for i in heads:
                qf, kf, ql, kl = _hgrn_level(xs[i], l, qs[i], kks[i])
                dpl = _bf(mask_ref[l + 1] * dps[i])
                dq_l = qf * _dot(dpl, kl)
                dk_l = kf * _dot_tn(dpl, ql)
                z_scr[i, pl.ds((l + 1) * c, c), :] = qs[i] * dq_l - kks[i] * dk_l
                dqs[i] = dqs[i] + dq_l
                dks[i] = dks[i] + dk_l

        zsplits = [_split(z_scr[i]) for i in heads]
        dlogfs = [_dot_tn(dmat_ref[...], zhi) + _dot_tn(dmat_ref[...], zlo) for zhi, zlo in zsplits]
        ds_new = [_dot_tn(dos[i], qes[i]) for i in heads]
        for i in heads:
            dlogf = dlogfs[i] + ebl_rows[i] * jnp.sum(dss[i] * ss[i], axis=0, keepdims=True)
            dkk = dks[i] - dlogf / (1.0 - kks[i])
            da_ref[1, :, _lanes(i)] = _bf(dkk * (1.0 - lbs[i]) * (-(snegs[i] * (1.0 - snegs[i]))))
            dlb_ref[:, _lanes(i)] += jnp.sum(dkk * (-snegs[i]), axis=0, keepdims=True)
            da_ref[0, :, _lanes(i)] = _bf(dqs[i] * _silu_grad(aqs[i], sqs[i]))
            ds_scr[i] = dss[i] * ebl_rows[i] + ds_new[i]

    w = HG_HEADS * A_D
    return pl.pallas_call(
        body, name="hgrn_bwd", grid=(A_HEADS // HG_HEADS, nch),
        in_specs=[col(0), col(1), col(2), col(3), vec, vec, mask_spec, dmat_spec, state_spec,
                  pl.BlockSpec((c, w), lambda h, n: (cidx(n), h))],
        out_specs=[pl.BlockSpec((4, c, w), lambda h, n: (0, cidx(n), h)), vec, vec],
        out_shape=[_sds((4, t, GROUP), BF16)] + [_sds((1, GROUP))] * 2,
        scratch_shapes=[pltpu.VMEM((HG_HEADS, A_D, A_D), F32), pltpu.VMEM((HG_HEADS, (nl + 2) * c, A_D), F32)],
        compiler_params=_params(("arbitrary", "arbitrary")),
    )(proj, proj, proj, proj, lb, gain, masks, dmat, states, dya)


def _sb_consts():
    j = np.arange(SB_TK)
    strict = (j[:, None] > j[None, :]).astype(np.float32)
    incl = (j[:, None] >= j[None, :]).astype(np.float32)
    return jnp.asarray(strict, BF16), jnp.asarray(incl, BF16)


def _lane0(x):
    return jnp.broadcast_to(x[:, 0:1], x.shape)


def _sb_softplus(z, masked):
    logsig = jnp.minimum(z, 0.0) - jnp.log2(1.0 + jnp.exp2(-jnp.abs(z)))
    sp = z - logsig
    mask = None
    if masked:
        mask = lax.broadcasted_iota(jnp.int32, z.shape, 1) < lax.broadcasted_iota(jnp.int32, z.shape, 0)
        sp = jnp.where(mask, sp, 0.0)
    return mask, sp, logsig


def _sb_cumsum(sp, cmat):
    cs = _dot(_bf(sp), cmat)
    return cs, _lane0(cs + sp)


def _sb_sweep(qi, group_fn, state):
    nd = SB_TQ // SB_TK
    state = group_fn([(pl.multiple_of((qi * nd + d) * SB_TK, SB_TK), d * SB_TK, True) for d in reversed(range(nd))],
                     state)

    def step(j, st):
        return group_fn([(pl.multiple_of(((qi - j) * nd - 1 - g) * SB_TK, SB_TK), 0, False) for g in range(nd)], st)

    return lax.fori_loop(0, qi, step, state)


def _set_rows(r0, full, new):
    return new if r0 == 0 else jnp.concatenate([full[:r0], new], axis=0)


def _sb_specs(t, tq):
    col = lambda g: pl.BlockSpec((tq, 2 * B_D), lambda p, i, h: (i, g * (GROUP // (2 * B_D)) + p))
    full = lambda g: pl.BlockSpec((t, 2 * B_D), lambda p, i, h: (0, g * (GROUP // (2 * B_D)) + p))
    vec = pl.BlockSpec((1, 2 * B_D), lambda p, i, h: (0, p))
    mat = pl.BlockSpec((SB_TK, SB_TK), lambda p, i, h: (0, 0))
    return col, full, vec, mat


def _head_lanes(h):
    return (lax.broadcasted_iota(jnp.int32, (1, 2 * B_D), 1) >= B_D) == (h == 1)


def _put(ref, h, val):
    @pl.when(h == 0)
    def _():
        ref[...] = val

    @pl.when(h == 1)
    def _():
        ref[...] += val


def _sb_fwd(proj_bf, proj, gain, gather=None):
    t = proj.shape[0]
    tq = SB_TQ
    strict, _ = _sb_consts()
    n_steps = (B_HEADS // 2, t // tq, 2)

    def body(q_ref, k_ref, v_ref, bg_ref, gain_ref, m_ref, *rest):
        if gather is None:
            o_ref, y_ref = rest
        else:
            flat_ref, _, o_ref, y_ref, gathered_ref, send_sems, recv_sems = rest
            start, forward, finish = _gather_plan(flat_ref, gathered_ref, send_sems, recv_sems, *gather[2:])
            step = (pl.program_id(0) * n_steps[1] + pl.program_id(1)) * n_steps[2] + pl.program_id(2)
            pl.when(step == 0)(start)
            pl.when(step == 2 * n_steps[1] * n_steps[2])(forward)
            pl.when(step == n_steps[0] * n_steps[1] * n_steps[2] - 1)(finish)
        h = pl.program_id(2)
        lanes = _head_lanes(h)
        qb = jnp.where(lanes, q_ref[...], jnp.zeros_like(q_ref))
        cmat = m_ref[...]

        def group(tiles, state):
            carry, acc = state
            kv = [(k_ref[pl.ds(off, SB_TK), :], v_ref[pl.ds(off, SB_TK), :]) for off, _, _ in tiles]
            zs = [_dot_nt(qb[r0:], kb) for (_, r0, _), (kb, _) in zip(tiles, kv)]
            sps = [_sb_softplus(z, masked) for z, (_, _, masked) in zip(zs, tiles)]
            css = [_sb_cumsum(sp, cmat) for _, sp, _ in sps]
            ws = []
            for (mask, _, logsig), (cs, tot), (_, r0, masked) in zip(sps, css, tiles):
                w = jnp.exp2(logsig - cs - carry[r0:])
                ws.append(_split(jnp.where(mask, w, 0.0) if masked else w))
                carry = _set_rows(r0, carry, carry[r0:] + tot)
            for (whi, wlo), (_, vb), (_, r0, _) in zip(ws, kv, tiles):
                acc = _set_rows(r0, acc, acc[r0:] + _dot(whi, vb) + _dot(wlo, vb))
            return carry, acc

        _, acc = _sb_sweep(pl.program_id(1), group, (jnp.zeros((tq, SB_TK), F32), jnp.zeros((tq, 2 * B_D), F32)))
        o = jnp.where(lanes, acc, 0.0)
        oh = o * lax.rsqrt(jnp.sum(o * o, axis=-1, keepdims=True) * (1.0 / B_D) + EPS)
        bg = bg_ref[...]
        _put(o_ref, h, o)
        _put(y_ref, h, oh * gain_ref[...] * (bg * _sigmoid(bg)))

    col, full, vec, mat = _sb_specs(t, tq)
    out = pl.BlockSpec((tq, 2 * B_D), lambda p, i, h: (i, p))
    in_specs = [col(4), full(5), full(6), col(7), vec, mat]
    out_specs = [out, out]
    out_shape = [_sds((t, GROUP)), _sds((t, GROUP))]
    operands = [proj_bf, proj_bf, proj_bf, proj, gain, strict]
    extra = {}
    if gather is not None:
        in_specs += [_ANY, _ANY]
        out_specs += [_ANY]
        out_shape += [_sds(gather[1].shape, gather[1].dtype)]
        operands += [gather[0], gather[1]]
        extra = dict(input_output_aliases={7: 2}, scratch_shapes=_gather_sems(gather[4]))
    return pl.pallas_call(
        body, name="sb_fwd" if gather is None else "sb_fwd_gather", grid=n_steps,
        in_specs=in_specs, out_specs=out_specs, out_shape=out_shape,
        compiler_params=_params(("arbitrary", "arbitrary", "arbitrary")), **extra,
    )(*operands)


def _sb_bwd(proj_bf, proj, o, dy, gain, exchange=None):
    t = proj.shape[0]
    tq = SB_TQ
    strict, incl = _sb_consts()

    def body(q_ref, k_ref, v_ref, bg_ref, o_ref, dy_ref, gain_ref, ms_ref, mi_ref,
             dq_ref, dk_ref, dv_ref, dbg_ref, dgain_ref):
        qi = pl.program_id(1)
        h = pl.program_id(2)
        lanes = _head_lanes(h)

        @pl.when((qi == 0) & (h == 0))
        def _():
            dk_ref[...] = jnp.zeros_like(dk_ref)
            dv_ref[...] = jnp.zeros_like(dv_ref)
            dgain_ref[...] = jnp.zeros_like(dgain_ref)

        qb = jnp.where(lanes, q_ref[...], jnp.zeros_like(q_ref))
        cmat = ms_ref[...]
        imat = mi_ref[...]
        o = jnp.where(lanes, o_ref[...], 0.0)
        dy = jnp.where(lanes, dy_ref[...], 0.0)
        bg = bg_ref[...]
        gain = gain_ref[...]
        r = lax.rsqrt(jnp.sum(o * o, axis=-1, keepdims=True) * (1.0 / B_D) + EPS)
        oh = o * r
        sig = _sigmoid(bg)
        sg = bg * sig
        _put(dbg_ref, h, dy * oh * gain * _silu_grad(bg, sig))
        dgain_ref[...] += jnp.sum(dy * oh * sg, axis=0, keepdims=True)
        doh = dy * gain * sg
        do = _bf(r * (doh - oh * (jnp.sum(doh * oh, axis=-1, keepdims=True) * (1.0 / B_D))))
        total = jnp.broadcast_to(jnp.sum(do.astype(F32) * o, axis=1, keepdims=True), (tq, SB_TK))

        def group(tiles, state):
            carry, gcarry, dq = state
            kv = [(k_ref[pl.ds(off, SB_TK), :], v_ref[pl.ds(off, SB_TK), :]) for off, _, _ in tiles]
            zs = [_dot_nt(qb[r0:], kb) for (_, r0, _), (kb, _) in zip(tiles, kv)]
            dws = [_dot_nt(do[r0:], vb) for (_, r0, _), (_, vb) in zip(tiles, kv)]
            sps = [_sb_softplus(z, masked) for z, (_, _, masked) in zip(zs, tiles)]
            css = [_sb_cumsum(sp, cmat) for _, sp, _ in sps]
            ws, gs = [], []
            for (mask, _, logsig), (cs, tot), dw, (_, r0, masked) in zip(sps, css, dws, tiles):
                w = jnp.exp2(logsig - cs - carry[r0:])
                w = jnp.where(mask, w, 0.0) if masked else w
                ws.append(_bf(w))
                gs.append(dw * w)
                carry = _set_rows(r0, carry, carry[r0:] + tot)
            s2s = []
            for g in gs:
                ghi, glo = _split(g)
                s2s.append(_dot(ghi, imat) + _dot(glo, imat))
            dzs = []
            for (mask, _, logsig), g, s2, (_, r0, masked) in zip(sps, gs, s2s, tiles):
                before = total[r0:] - gcarry[r0:] - s2
                dz = g - jnp.exp2(logsig) * (g + before)
                dzs.append(_bf(jnp.where(mask, dz, 0.0) if masked else dz))
                gcarry = _set_rows(r0, gcarry, gcarry[r0:] + _lane0(s2))
            for dz, wb, (kb, _), (off, r0, _) in zip(dzs, ws, kv, tiles):
                dq = _set_rows(r0, dq, dq[r0:] + _dot(dz, kb))
                dk_ref[pl.ds(off, SB_TK), :] += _dot_tn(dz, qb[r0:])
                dv_ref[pl.ds(off, SB_TK), :] += _dot_tn(wb, do[r0:])
            return carry, gcarry, dq

        zero = jnp.zeros((tq, SB_TK), F32)
        _, _, dq = _sb_sweep(qi, group, (zero, zero, jnp.zeros((tq, 2 * B_D), F32)))
        _put(dq_ref, h, jnp.where(lanes, dq * (B_D ** -0.5), 0.0))

    col, full, vec, mat = _sb_specs(t, tq)
    blk = pl.BlockSpec((tq, 2 * B_D), lambda p, i, h: (i, p))
    whole = pl.BlockSpec((t, 2 * B_D), lambda p, i, h: (0, p))
    grid = (B_HEADS // 2, t // tq, 2)
    in_specs = [col(4), full(5), full(6), col(7), blk, blk, vec, mat, mat]
    out_specs = [blk, whole, whole, blk, vec]
    out_shape = [_sds((t, GROUP))] * 4 + [_sds((1, GROUP))]
    operands = [proj_bf, proj_bf, proj_bf, proj, o, dy, gain, strict, incl]
    scratch = []
    if exchange is not None:
        step_of = lambda: (pl.program_id(0) * grid[1] + pl.program_id(1)) * grid[2] + pl.program_id(2)
        body = _with_exchange(body, 9, 5, exchange, step_of, grid[0] * grid[1] * grid[2])
        xi, xo, xs, scratch, xop = _exchange_args(exchange)
        in_specs, out_specs, out_shape, operands = in_specs + xi, out_specs + xo, out_shape + xs, operands + xop
    return pl.pallas_call(
        body, name="sb_bwd" if exchange is None else "sb_bwd_exchange", grid=grid,
        in_specs=in_specs, out_specs=out_specs, out_shape=out_shape, scratch_shapes=scratch,
        compiler_params=_params(("arbitrary", "arbitrary", "arbitrary")),
    )(*operands)


def _adamw(w, g, m, v):
    rows, cols = w.shape
    tr = rows
    for cand in (400, 256, 128, 64, 32, 16, 8):
        if rows % cand == 0:
            tr = cand
            break

    def body(w_ref, g_ref, m_ref, v_ref, d_ref, nm_ref, nv_ref):
        g_ = g_ref[...]
        m_ = ADAM_B1 * m_ref[...] + (1.0 - ADAM_B1) * g_
        v_ = ADAM_B2 * v_ref[...] + (1.0 - ADAM_B2) * (g_ * g_)
        m_hat = m_ / (1.0 - ADAM_B1 ** ADAM_STEP)
        v_hat = v_ / (1.0 - ADAM_B2 ** ADAM_STEP)
        d_ref[...] = -ADAM_LR * (m_hat / (jnp.sqrt(v_hat) + ADAM_EPS) + ADAM_WD * w_ref[...])
        nm_ref[...] = m_
        nv_ref[...] = v_

    spec = pl.BlockSpec((tr, cols), lambda i: (i, 0))
    return pl.pallas_call(
        body, name="adamw", grid=(rows // tr,), in_specs=[spec] * 4, out_specs=[spec] * 3,
        out_shape=[_sds((rows, cols))] * 3, compiler_params=_params(("arbitrary",)),
    )(w, g, m, v)


_ANY = pl.BlockSpec(memory_space=pl.ANY)


def _place():
    return lax.axis_index("x"), lax.axis_index("y"), lax.axis_index("c")


def _gather_plan(x_ref, out_ref, send_sems, recv_sems, row0, nrows, nc):
    x, y, c = _place()
    me = 2 * x + y
    sibling = (x, y, 1 - c)
    half = nrows // 2
    ch = half // nc
    peers = [me ^ k for k in (1, 2, 3)]

    def rows(shard, hc, r):
        return out_ref.at[shard, pl.ds(row0 + hc * half + r * ch, ch), :]

    def copy(k, shard, hc, r, to, src=None):
        return pltpu.make_async_remote_copy(
            src_ref=rows(shard, hc, r) if src is None else src, dst_ref=rows(shard, hc, r),
            send_sem=send_sems.at[k * nc + r], recv_sem=recv_sems.at[k * nc + r], device_id=to, device_id_type=MESH)

    def first(k, p, r):
        return copy(k, me, c, r, (p >> 1, p & 1, c), src=x_ref.at[pl.ds(row0 + c * half + r * ch, ch), :])

    def start():
        for k, p in enumerate(peers):
            for r in range(nc):
                first(k, p, r).start()

    def forward():
        for k, p in enumerate(peers):
            for r in range(nc):
                copy(k, p, c, r, sibling).wait_recv()
                copy(3 + k, p, c, r, sibling).start()

    def finish():
        for k, p in enumerate(peers):
            for r in range(nc):
                copy(3 + k, p, 1 - c, r, sibling).wait_recv()
        for k, p in enumerate(peers):
            for r in range(nc):
                first(k, p, r).wait_send()
                copy(3 + k, p, c, r, sibling).wait_send()

    return start, forward, finish


def _gather_sems(nc):
    return [pltpu.SemaphoreType.DMA((6 * nc,)), pltpu.SemaphoreType.DMA((6 * nc,))]


def _swap_plan(g_ref, out_ref, send_sems, recv_sems):
    half, ch, nc = HALF_LAYER, CHUNK_ROWS, RS_CHUNKS

    def copies():
        x, y, c = _place()
        return [pltpu.make_async_remote_copy(
            src_ref=g_ref.at[j, pl.ds((1 - c) * half + r * ch, ch), :], dst_ref=out_ref.at[j, pl.ds(r * ch, ch), :],
            send_sem=send_sems.at[j * nc + r], recv_sem=recv_sems.at[j * nc + r],
            device_id=(x, y, 1 - c), device_id_type=MESH) for j in range(N_SHARD) for r in range(nc)]

    def start():
        for cp in copies():
            cp.start()

    def finish():
        for cp in copies():
            cp.wait()

    return start, finish


def _scatter_plan(p_ref, out_ref, send_sems, recv_sems):
    ch, nc = CHUNK_ROWS, RS_CHUNKS

    def copies():
        x, y, c = _place()
        me = 2 * x + y
        return [pltpu.make_async_remote_copy(
            src_ref=p_ref.at[me ^ k, pl.ds(r * ch, ch), :], dst_ref=out_ref.at[k - 1, pl.ds(r * ch, ch), :],
            send_sem=send_sems.at[(k - 1) * nc + r], recv_sem=recv_sems.at[(k - 1) * nc + r],
            device_id=((me ^ k) >> 1, (me ^ k) & 1, c), device_id_type=MESH) for k in (1, 2, 3) for r in range(nc)]

    def start():
        for cp in copies():
            cp.start()

    def finish():
        for cp in copies():
            cp.wait()

    return start, finish


SWAP = (_swap_plan, (N_SHARD, HALF_LAYER, D_MODEL), F32, N_SHARD * RS_CHUNKS)
SCATTER = (_scatter_plan, (3, HALF_LAYER, D_MODEL), BF16, 3 * RS_CHUNKS)


def _exchange_call(kind, operand):
    plan, shape, dtype, n_sems = kind

    def body(in_ref, out_ref, send_sems, recv_sems):
        start, finish = plan(in_ref, out_ref, send_sems, recv_sems)
        start()
        finish()

    return pl.pallas_call(
        body, name="exchange", in_specs=[_ANY], out_specs=_ANY, out_shape=_sds(shape, dtype),
        scratch_shapes=[pltpu.SemaphoreType.DMA((n_sems,)), pltpu.SemaphoreType.DMA((n_sems,))],
    )(operand)


def _with_exchange(body, n_in, n_out, exchange, step_of, n_steps):
    def wrapped(*refs):
        ins, src = refs[:n_in], refs[n_in]
        outs, dst = refs[n_in + 1:n_in + 1 + n_out], refs[n_in + 1 + n_out]
        send_sems, recv_sems = refs[n_in + 2 + n_out:n_in + 4 + n_out]
        start, finish = exchange[0][0](src, dst, send_sems, recv_sems)
        pl.when(step_of() == 0)(start)
        body(*ins, *outs, *refs[n_in + 4 + n_out:])
        pl.when(step_of() == n_steps - 1)(finish)

    return wrapped


def _exchange_args(exchange):
    (plan, shape, dtype, n_sems), operand = exchange
    sems = [pltpu.SemaphoreType.DMA((n_sems,)), pltpu.SemaphoreType.DMA((n_sems,))]
    return [_ANY], [_ANY], [_sds(shape, dtype)], sems, [operand]


def _gather_weights(flat, row0, nrows, nc):
    def body(x_ref, out_ref, send_sems, recv_sems):
        start, forward, finish = _gather_plan(x_ref, out_ref, send_sems, recv_sems, row0, nrows, nc)
        start()
        forward()
        finish()

    return pl.pallas_call(
        body, name="gather_weights", in_specs=[_ANY], out_specs=_ANY,
        out_shape=_sds((N_SHARD, ROWS_FLAT, D_MODEL), BF16), scratch_shapes=_gather_sems(nc),
    )(flat)


def _add_my_half(grads, recv):
    tr = 400
    nb = HALF_LAYER // tr
    core = lax.axis_index("c").astype(jnp.int32).reshape(1)

    def body(c_ref, g_ref, r_ref, o_ref, ob_ref):
        acc = g_ref[...] + r_ref[...]
        o_ref[...] = acc
        ob_ref[...] = _bf(acc)

    out = pl.BlockSpec((None, tr, D_MODEL), lambda j, i, c_ref: (j, i, 0))
    return pl.pallas_call(
        body, name="add_my_half",
        grid_spec=pltpu.PrefetchScalarGridSpec(
            num_scalar_prefetch=1, grid=(N_SHARD, nb),
            in_specs=[pl.BlockSpec((None, tr, D_MODEL), lambda j, i, c_ref: (j, c_ref[0] * nb + i, 0)), out],
            out_specs=[out, out]),
        out_shape=[_sds((N_SHARD, HALF_LAYER, D_MODEL)), _sds((N_SHARD, HALF_LAYER, D_MODEL), BF16)],
        compiler_params=_params(("arbitrary", "arbitrary")),
    )(core, grads, recv)


def _sum_scattered(part, recv):
    tr = 400
    chip = (2 * lax.axis_index("x") + lax.axis_index("y")).astype(jnp.int32).reshape(1)

    def body(c_ref, p_ref, r_ref, o_ref):
        acc = p_ref[...]
        for k in range(3):
            acc = acc + r_ref[k].astype(F32)
        o_ref[...] = acc

    return pl.pallas_call(
        body, name="sum_scattered",
        grid_spec=pltpu.PrefetchScalarGridSpec(
            num_scalar_prefetch=1, grid=(HALF_LAYER // tr,),
            in_specs=[pl.BlockSpec((None, tr, D_MODEL), lambda i, c_ref: (c_ref[0], i, 0)),
                      pl.BlockSpec((3, tr, D_MODEL), lambda i, c_ref: (0, i, 0))],
            out_specs=pl.BlockSpec((tr, D_MODEL), lambda i, c_ref: (i, 0))),
        out_shape=_sds((HALF_LAYER, D_MODEL)), compiler_params=_params(("arbitrary",)),
    )(chip, part, recv)


def _swap_reduced(mine):
    ch, nc = CHUNK_ROWS, RS_CHUNKS

    def body(r_ref, out_ref, send_sems, recv_sems):
        x, y, c = _place()
        copies = [pltpu.make_async_remote_copy(
            src_ref=r_ref.at[l, pl.ds(r * ch, ch), :], dst_ref=out_ref.at[l, pl.ds(r * ch, ch), :],
            send_sem=send_sems.at[l * nc + r], recv_sem=recv_sems.at[l * nc + r],
            device_id=(x, y, 1 - c), device_id_type=MESH) for l in range(2) for r in range(nc)]
        for cp in copies:
            cp.start()
        for cp in copies:
            cp.wait()

    return pl.pallas_call(
        body, name="swap_reduced", in_specs=[_ANY], out_specs=_ANY,
        out_shape=_sds((2, HALF_LAYER, D_MODEL)),
        scratch_shapes=[pltpu.SemaphoreType.DMA((2 * nc,)), pltpu.SemaphoreType.DMA((2 * nc,))],
    )(mine)


def _allreduce_small(vec):
    def body(v_ref, out_ref, buf, send_sems, recv_sems):
        x, y, c = _place()
        me = 4 * x + 2 * y + c
        buf[me] = v_ref[...]
        peers = [me ^ k for k in range(1, N_DEV)]
        sends = [pltpu.make_async_remote_copy(
            src_ref=v_ref, dst_ref=buf.at[me], send_sem=send_sems.at[k], recv_sem=recv_sems.at[k],
            device_id=(p >> 2, (p >> 1) & 1, p & 1), device_id_type=MESH) for k, p in enumerate(peers)]
        for cp in sends:
            cp.start()
        for k, p in enumerate(peers):
            pltpu.make_async_remote_copy(
                src_ref=v_ref, dst_ref=buf.at[p], send_sem=send_sems.at[k], recv_sem=recv_sems.at[k],
                device_id=(p >> 2, (p >> 1) & 1, p & 1), device_id_type=MESH).wait_recv()
        for cp in sends:
            cp.wait_send()
        acc = buf[0]
        for d in range(1, N_DEV):
            acc = acc + buf[d]
        out_ref[...] = acc

    vm = pl.BlockSpec(memory_space=pltpu.VMEM)
    return pl.pallas_call(
        body, name="allreduce_small", in_specs=[vm], out_specs=vm, out_shape=_sds((SMALL_ROWS, 128)),
        scratch_shapes=[pltpu.VMEM((N_DEV, SMALL_ROWS, 128), F32),
                        pltpu.SemaphoreType.DMA((N_DEV - 1,)), pltpu.SemaphoreType.DMA((N_DEV - 1,))],
    )(vec)


def _flatten_shard(w_in, w_out, w_pg, w_pp):
    return jnp.concatenate([w_in.reshape(-1, D_MODEL), w_out.reshape(-1, D_MODEL), w_pg.reshape(-1, D_MODEL),
                            w_pp.reshape(-1, D_MODEL)], axis=0)


def _unflatten_layers(flats):
    a, b, c = D_MODEL, D_MODEL + D_MODEL // N_SHARD, D_MODEL + 2 * (D_MODEL // N_SHARD)
    q = D_MODEL // N_SHARD
    return (jnp.stack([f[:a] for f in flats]), jnp.stack([f[a:b] for f in flats]),
            jnp.stack([f[b:c] for f in flats]), jnp.stack([f[c:].reshape(D_PLE, q) for f in flats]))


def _full_w_pp(gathered):
    c = ROWS_W_IN + ROWS_W_OUT + ROWS_W_PG
    q = D_MODEL // N_SHARD
    rpp = ROWS_W_PP // 2
    return [gathered[:, c + l * rpp:c + (l + 1) * rpp, :].reshape(N_SHARD, D_PLE, q).transpose(1, 0, 2)
            .reshape(D_PLE, D_MODEL) for l in range(2)]


def _layer_grads(dw_in, dw_out, dw_pg, dw_pp):
    q = D_MODEL // N_SHARD
    rpp = ROWS_W_PP // 2
    return jnp.concatenate([dw_in, dw_out.reshape(N_SHARD, q, D_MODEL), dw_pg.reshape(N_SHARD, q, D_MODEL),
                            dw_pp.reshape(D_PLE, N_SHARD, q).transpose(1, 0, 2).reshape(N_SHARD, rpp, D_MODEL)], axis=1)


def _lower_bounds(lb_logits):
    sm = jax.nn.softmax(lb_logits.astype(F32), axis=0)
    return jnp.cumsum(sm, axis=0) - sm[0:1]


def kernel(x, p, norm_mix, w_in, a_out_norm, b_out_norm, w_out, lb_logits, ple_gate_norm, w_ple_gate, w_ple_proj, ple_post_norm, final_norm, loss_target, m_norm_mix, m_w_in, m_a_out_norm, m_b_out_norm, m_w_out, m_lb_logits, m_ple_gate_norm, m_w_ple_gate, m_w_ple_proj, m_ple_post_norm, m_final_norm, v_norm_mix, v_w_in, v_a_out_norm, v_b_out_norm, v_w_out, v_lb_logits, v_ple_gate_norm, v_w_ple_gate, v_w_ple_proj, v_ple_post_norm, v_final_norm):
    t = x.shape[1]
    h0 = x.reshape(t, D_MODEL)
    target = loss_target.reshape(t, D_MODEL)
    pl_in = p.reshape(2, t, D_PLE)

    w_flat_bf = _flatten_shard(_bf(w_in), _bf(w_out), _bf(w_ple_gate), _bf(w_ple_proj))
    chip = 2 * lax.axis_index("x") + lax.axis_index("y")
    gathered = lax.dynamic_update_slice(_gather_weights(w_flat_bf, 0, D_MODEL, 2), w_flat_bf[None], (chip, 0, 0))
    lbs, lbs_vjp = jax.vjp(_lower_bounds, lb_logits)

    saved = []
    h = h0
    for l in range(2):
        g_mix = norm_mix[l].reshape(1, D_MODEL)
        lb = lbs[l].reshape(1, GROUP)
        ga = a_out_norm[l].reshape(1, GROUP)
        gb = b_out_norm[l].reshape(1, GROUP)
        proj, proj_bf = _inproj(h, g_mix, gathered, l)
        ya, states = _hgrn_fwd(proj, lb, ga)
        if l == 0:
            ob, yb, gathered = _sb_fwd(proj_bf, proj, gb, (w_flat_bf, gathered, D_MODEL, ROWS_FLAT - D_MODEL, 4))
            w_pps = _full_w_pp(gathered)
        else:
            ob, yb = _sb_fwd(proj_bf, proj, gb)
        h1 = _outproj(h, ya, yb, gathered, l)
        g_post = ple_post_norm[l].reshape(1, D_MODEL)
        g_gate = ple_gate_norm[l].reshape(1, D_MODEL)
        h2 = _ple_fwd(h1, pl_in[l], w_pps[l], gathered, l, g_post, g_gate)
        saved.append((h, proj, proj_bf, states, ya, yb, ob, h1))
        h = h2

    dh, d_final, loss_part = _final(h, final_norm.reshape(1, D_MODEL), target)

    g_layer, chip_sum, scattered = [None] * 2, [None] * 2, [None] * 2
    d_mix, d_a, d_b, d_lb, d_gate, d_post = [None] * 2, [None] * 2, [None] * 2, [None] * 2, [None] * 2, [None] * 2
    for l in (1, 0):
        h_in, proj, proj_bf, states, ya, yb, ob, h1 = saved[l]
        g_mix = norm_mix[l].reshape(1, D_MODEL)
        lb = lbs[l].reshape(1, GROUP)
        ga = a_out_norm[l].reshape(1, GROUP)
        gb = b_out_norm[l].reshape(1, GROUP)
        g_post = ple_post_norm[l].reshape(1, D_MODEL)
        g_gate = ple_gate_norm[l].reshape(1, D_MODEL)
        if l == 1:
            dh1, dw_pg, dw_pp, d_gate[l], d_post[l] = _ple_bwd(dh, h1, pl_in[l], w_pps[l], gathered, l, g_post, g_gate)
            dya, dyb, dw_out = _outproj_bwd(dh1, ya, yb, gathered, l)
            dbq, dbk, dbv, dbg, d_b[l] = _sb_bwd(proj_bf, proj, ob, dyb, gb)
        else:
            dh1, dw_pg, dw_pp, d_gate[l], d_post[l], from_sibling = _ple_bwd(
                dh, h1, pl_in[l], w_pps[l], gathered, l, g_post, g_gate, (SWAP, g_layer[1]))
            chip_sum[1], chip_sum_bf = _add_my_half(g_layer[1], from_sibling)
            dya, dyb, dw_out = _outproj_bwd(dh1, ya, yb, gathered, l)
            dbq, dbk, dbv, dbg, d_b[l], scattered[1] = _sb_bwd(proj_bf, proj, ob, dyb, gb, (SCATTER, chip_sum_bf))
        da, d_lb[l], d_a[l] = _hgrn_bwd(proj, lb, ga, states, dya)
        db = jnp.stack([dbq, dbk * LN2, dbv, dbg]).astype(BF16)
        g_layer[l] = _layer_grads(_inproj_bwd_dw(h_in, g_mix, da, db), dw_out, dw_pg, dw_pp)
        dh, d_mix[l] = _inproj_bwd_dx(dh1, h_in, g_mix, gathered, l, da, db)
    grad_x = dh.reshape(x.shape)

    chip_sum[0], chip_sum_bf = _add_my_half(g_layer[0], _exchange_call(SWAP, g_layer[0]))
    scattered[0] = _exchange_call(SCATTER, chip_sum_bf)
    mine = jnp.stack([_sum_scattered(chip_sum[l], scattered[l]) for l in range(2)])
    other = _swap_reduced(mine)
    south = lax.axis_index("c") == 0
    g_w_in, g_w_out, g_w_pg, g_w_pp = _unflatten_layers(
        [jnp.concatenate([jnp.where(south, mine[l], other[l]), jnp.where(south, other[l], mine[l])]) for l in range(2)])

    small = jnp.concatenate([
        jnp.concatenate(d_mix).reshape(-1, 128), jnp.concatenate(d_a).reshape(-1, 128),
        jnp.concatenate(d_b).reshape(-1, 128), jnp.concatenate(d_lb).reshape(-1, 128),
        jnp.concatenate(d_gate).reshape(-1, 128), jnp.concatenate(d_post).reshape(-1, 128),
        d_final.reshape(-1, 128), jnp.broadcast_to(loss_part, (8, 128))], axis=0)
    small = _allreduce_small(small)
    loss = small[80, 0]
    g_norm_mix = small[0:16].reshape(2, D_MODEL)
    g_a = small[16:24].reshape(2, GROUP)
    g_b = small[24:32].reshape(2, GROUP)
    (g_lb,) = lbs_vjp(small[32:40].reshape(2, GROUP))
    g_gate = small[40:56].reshape(2, D_MODEL)
    g_post = small[56:72].reshape(2, D_MODEL)
    g_final = small[72:80].reshape(D_MODEL)

    def adam_matrix(w, g, m, v):
        d, nm, nv = _adamw(w.reshape(-1, D_MODEL), g.reshape(-1, D_MODEL), m.reshape(-1, D_MODEL), v.reshape(-1, D_MODEL))
        return d.reshape(w.shape), nm.reshape(w.shape), nv.reshape(w.shape)

    d_w_in, nm_w_in, nv_w_in = adam_matrix(w_in, g_w_in, m_w_in, v_w_in)
    d_w_out, nm_w_out, nv_w_out = adam_matrix(w_out, g_w_out, m_w_out, v_w_out)
    d_w_pg, nm_w_pg, nv_w_pg = adam_matrix(w_ple_gate, g_w_pg, m_w_ple_gate, v_w_ple_gate)
    d_w_pp, nm_w_pp, nv_w_pp = adam_matrix(w_ple_proj, g_w_pp, m_w_ple_proj, v_w_ple_proj)

    small_w = [norm_mix, a_out_norm, b_out_norm, lb_logits, ple_gate_norm, ple_post_norm, final_norm]
    small_g = [g_norm_mix, g_a, g_b, g_lb, g_gate, g_post, g_final]
    small_m = [m_norm_mix, m_a_out_norm, m_b_out_norm, m_lb_logits, m_ple_gate_norm, m_ple_post_norm, m_final_norm]
    small_v = [v_norm_mix, v_a_out_norm, v_b_out_norm, v_lb_logits, v_ple_gate_norm, v_ple_post_norm, v_final_norm]
    pack = lambda arrs: jnp.concatenate([a.reshape(-1, 128) for a in arrs], axis=0)
    ds, nms, nvs = _adamw(pack(small_w), pack(small_g), pack(small_m), pack(small_v))

    def unpack(packed):
        out, r = [], 0
        for a in small_w:
            n = a.size // 128
            out.append(packed[r:r + n].reshape(a.shape))
            r += n
        return out

    d_s, nm_s, nv_s = unpack(ds), unpack(nms), unpack(nvs)

    def ordered(s, big):
        return [s[0], big[0], s[1], s[2], big[1], s[3], s[4], big[2], big[3], s[5], s[6]]

    grads = ordered(small_g, [g_w_in, g_w_out, g_w_pg, g_w_pp])
    deltas = ordered(d_s, [d_w_in, d_w_out, d_w_pg, d_w_pp])
    new_m = ordered(nm_s, [nm_w_in, nm_w_out, nm_w_pg, nm_w_pp])
    new_v = ordered(nv_s, [nv_w_in, nv_w_out, nv_w_pg, nv_w_pp])
    return (loss, grad_x, *grads, *deltas, *new_m, *new_v)
```

```python
import functools
import math

import numpy as np
import jax
import jax.numpy as jnp
from jax import lax
from jax.experimental import pallas as pl
from jax.experimental.pallas import tpu as pltpu

F32 = jnp.float32
BF16 = jnp.bfloat16
MESH = pl.DeviceIdType.MESH

D_MODEL = 1024
D_PLE = 256
D_IN = 4096
A_HEADS, A_D = 4, 128
B_HEADS, B_D = 8, 64
GROUP = 512
EPS = 1e-6
N_SHARD = 4
N_DEV = 8

HG_CHUNK = 128
HG_LEVELS = 7
SB_TQ = 1024
SB_TK = 128
SB_GROUP_FWD, SB_GROUP_BWD = 4, 8
LOG2E = 1.4426950408889634
LN2 = 0.6931471805599453

ADAM_LR, ADAM_B1, ADAM_B2, ADAM_EPS, ADAM_WD, ADAM_STEP = 0.001, 0.9, 0.999, 1e-08, 0.01, 10

VMEM_LIMIT = 48 * 1024 * 1024
VMEM_LIMIT_BIG = 58 * 1024 * 1024

ROWS_W_IN = 2 * D_MODEL
ROWS_W_OUT = 2 * (D_MODEL // N_SHARD)
ROWS_W_PG = 2 * (D_MODEL // N_SHARD)
ROWS_W_PP = 2 * (D_PLE * (D_MODEL // N_SHARD) // D_MODEL)
ROWS_FLAT = ROWS_W_IN + ROWS_W_OUT + ROWS_W_PG + ROWS_W_PP
HALF_FLAT = ROWS_FLAT // 2
N_CHUNK = 10
CHUNK_ROWS = HALF_FLAT // N_CHUNK

ROWS_LAYER = ROWS_FLAT // 2
HALF_LAYER = ROWS_LAYER // 2
RS_CHUNKS = HALF_LAYER // CHUNK_ROWS

SMALL_ROWS = 88


def _sds(shape, dtype=F32):
    return jax.ShapeDtypeStruct(shape, dtype)


def _params(sem=None, vmem_limit=VMEM_LIMIT):
    kw = dict(vmem_limit_bytes=vmem_limit)
    if sem is not None:
        kw["dimension_semantics"] = sem
    return pltpu.CompilerParams(**kw)


def _dot(a, b, precision=None):
    return lax.dot_general(a, b, (((1,), (0,)), ((), ())), preferred_element_type=F32, precision=precision)


def _dot_nt(a, b, precision=None):
    return lax.dot_general(a, b, (((1,), (1,)), ((), ())), preferred_element_type=F32, precision=precision)


def _dot_tn(a, b, precision=None):
    return lax.dot_general(a, b, (((0,), (0,)), ((), ())), preferred_element_type=F32, precision=precision)


def _bf(x):
    return x.astype(BF16)


def _split(x):
    hi = x.astype(BF16)
    lo = (x - hi.astype(F32)).astype(BF16)
    return hi, lo


def _rms(x):
    r = lax.rsqrt(jnp.mean(x * x, axis=-1, keepdims=True) + EPS)
    return x * r, r


def _rms_bwd(dxh, xh, r):
    return r * (dxh - xh * jnp.mean(dxh * xh, axis=-1, keepdims=True))


def _sigmoid(x):
    return 1.0 / (1.0 + jnp.exp(-x))


def _silu_grad(x, sig):
    return sig * (1.0 + x * (1.0 - sig))


def _row_tile(t, want):
    return min(t, want)


def _inproj(h, g, gathered, layer):
    t = h.shape[0]
    tm = _row_tile(t, 512)
    tn = D_MODEL
    scale = jnp.ones((1, D_IN), F32).at[:, 4 * GROUP:5 * GROUP].set(B_D ** -0.5 * LOG2E)

    def body(h_ref, g_ref, w_ref, s_ref, o_ref, ob_ref):
        xh, _ = _rms(h_ref[...])
        acc = _dot(_bf(xh * g_ref[...]), w_ref[...])
        o_ref[...] = acc
        ob_ref[...] = _bf(acc * s_ref[...])

    return pl.pallas_call(
        body, name="inproj", grid=(D_IN // tn, t // tm),
        in_specs=[pl.BlockSpec((tm, D_MODEL), lambda j, i: (i, 0)),
                  pl.BlockSpec((1, D_MODEL), lambda j, i: (0, 0)),
                  pl.BlockSpec((None, D_MODEL, tn), lambda j, i: (j, layer, 0)),
                  pl.BlockSpec((1, tn), lambda j, i: (0, j))],
        out_specs=[pl.BlockSpec((tm, tn), lambda j, i: (i, j)), pl.BlockSpec((tm, tn), lambda j, i: (i, j))],
        out_shape=[_sds((t, D_IN)), _sds((t, D_IN), BF16)], compiler_params=_params(("arbitrary", "arbitrary")),
    )(h, g, gathered, scale)


def _rows_spec(first_row):
    q = D_MODEL // N_SHARD
    return pl.BlockSpec((N_SHARD, q, D_MODEL), lambda i: (0, first_row // q, 0))


def _outproj(h, ya, yb, gathered, layer):
    t = h.shape[0]
    tm = _row_tile(t, 512)

    def body(h_ref, ya_ref, yb_ref, w_ref, o_ref):
        o_ref[...] = (h_ref[...] + _dot(_bf(ya_ref[...]), w_ref[0:2].reshape(GROUP, D_MODEL))
                      + _dot(_bf(yb_ref[...]), w_ref[2:4].reshape(GROUP, D_MODEL)))

    return pl.pallas_call(
        body, name="outproj", grid=(t // tm,),
        in_specs=[pl.BlockSpec((tm, D_MODEL), lambda i: (i, 0)),
                  pl.BlockSpec((tm, GROUP), lambda i: (i, 0)),
                  pl.BlockSpec((tm, GROUP), lambda i: (i, 0)),
                  _rows_spec(ROWS_W_IN + layer * (D_MODEL // N_SHARD))],
        out_specs=pl.BlockSpec((tm, D_MODEL), lambda i: (i, 0)),
        out_shape=_sds((t, D_MODEL)), compiler_params=_params(("arbitrary",)),
    )(h, ya, yb, gathered)


def _ple_fwd(h, p, w_pp, gathered, layer, g_post, g_gate):
    t = h.shape[0]
    tm = _row_tile(t, 256)

    def body(h_ref, p_ref, wpp_ref, wpg_ref, gp_ref, gg_ref, o_ref):
        x = h_ref[...]
        ph, _ = _rms(_dot(_bf(p_ref[...]), wpp_ref[...]))
        xh, _ = _rms(x)
        gate = _sigmoid(_dot(_bf(xh * gg_ref[...]), wpg_ref[...].reshape(D_MODEL, D_MODEL)))
        o_ref[...] = x + gate * (ph * gp_ref[...])

    return pl.pallas_call(
        body, name="ple_fwd", grid=(t // tm,),
        in_specs=[pl.BlockSpec((tm, D_MODEL), lambda i: (i, 0)),
                  pl.BlockSpec((tm, D_PLE), lambda i: (i, 0)),
                  pl.BlockSpec((D_PLE, D_MODEL), lambda i: (0, 0)),
                  _rows_spec(ROWS_W_IN + ROWS_W_OUT + layer * (D_MODEL // N_SHARD)),
                  pl.BlockSpec((1, D_MODEL), lambda i: (0, 0)),
                  pl.BlockSpec((1, D_MODEL), lambda i: (0, 0))],
        out_specs=pl.BlockSpec((tm, D_MODEL), lambda i: (i, 0)),
        out_shape=_sds((t, D_MODEL)), compiler_params=_params(("arbitrary",)),
    )(h, p, w_pp, gathered, g_post, g_gate)


def _ple_bwd(dh2, h, p, w_pp, gathered, layer, g_post, g_gate, exchange=None):
    t = h.shape[0]
    tm = _row_tile(t, 256)

    def body(d_ref, h_ref, p_ref, wpp_ref, wpg_ref, gp_ref, gg_ref, dh_ref, dwpg_ref, dwpp_ref, dgg_ref, dgp_ref):
        @pl.when(pl.program_id(0) == 0)
        def _():
            dwpg_ref[...] = jnp.zeros_like(dwpg_ref)
            dwpp_ref[...] = jnp.zeros_like(dwpp_ref)
            dgg_ref[...] = jnp.zeros_like(dgg_ref)
            dgp_ref[...] = jnp.zeros_like(dgp_ref)

        d = d_ref[...]
        x = h_ref[...]
        gp = gp_ref[...]
        gg = gg_ref[...]
        pb = _bf(p_ref[...])
        ph, rp = _rms(_dot(pb, wpp_ref[...]))
        pe = ph * gp
        xh, rx = _rms(x)
        un = _bf(xh * gg)
        wpg = wpg_ref[...].reshape(D_MODEL, D_MODEL)
        gate = _sigmoid(_dot(un, wpg))
        dgpre = _bf(d * pe * gate * (1.0 - gate))
        dun = _dot_nt(dgpre, wpg)
        dh_ref[...] = d + _rms_bwd(dun * gg, xh, rx)
        dgg_ref[...] += jnp.sum(dun * xh, axis=0, keepdims=True)
        dwpg_ref[...] += _dot_tn(un, dgpre)
        dpe = d * gate
        dgp_ref[...] += jnp.sum(dpe * ph, axis=0, keepdims=True)
        dwpp_ref[...] += _dot_tn(pb, _bf(_rms_bwd(dpe * gp, ph, rp)))

    in_specs = [pl.BlockSpec((tm, D_MODEL), lambda i: (i, 0)),
                pl.BlockSpec((tm, D_MODEL), lambda i: (i, 0)),
                pl.BlockSpec((tm, D_PLE), lambda i: (i, 0)),
                pl.BlockSpec((D_PLE, D_MODEL), lambda i: (0, 0)),
                _rows_spec(ROWS_W_IN + ROWS_W_OUT + layer * (D_MODEL // N_SHARD)),
                pl.BlockSpec((1, D_MODEL), lambda i: (0, 0)),
                pl.BlockSpec((1, D_MODEL), lambda i: (0, 0))]
    out_specs = [pl.BlockSpec((tm, D_MODEL), lambda i: (i, 0)),
                 pl.BlockSpec((D_MODEL, D_MODEL), lambda i: (0, 0)),
                 pl.BlockSpec((D_PLE, D_MODEL), lambda i: (0, 0)),
                 pl.BlockSpec((1, D_MODEL), lambda i: (0, 0)),
                 pl.BlockSpec((1, D_MODEL), lambda i: (0, 0))]
    out_shape = [_sds((t, D_MODEL)), _sds((D_MODEL, D_MODEL)), _sds((D_PLE, D_MODEL)),
                 _sds((1, D_MODEL)), _sds((1, D_MODEL))]
    operands = [dh2, h, p, w_pp, gathered, g_post, g_gate]
    scratch = []
    if exchange is not None:
        body = _with_exchange(body, 7, 5, exchange, lambda: pl.program_id(0), t // tm)
        xi, xo, xs, scratch, xop = _exchange_args(exchange)
        in_specs, out_specs, out_shape, operands = in_specs + xi, out_specs + xo, out_shape + xs, operands + xop
    return pl.pallas_call(
        body, name="ple_bwd" if exchange is None else "ple_bwd_exchange", grid=(t // tm,),
        in_specs=in_specs, out_specs=out_specs, out_shape=out_shape, scratch_shapes=scratch,
        compiler_params=_params(("arbitrary",)),
    )(*operands)


def _outproj_bwd(dh, ya, yb, gathered, layer):
    t = dh.shape[0]
    tm = _row_tile(t, 512)

    def body(d_ref, ya_ref, yb_ref, w_ref, dya_ref, dyb_ref, dw_ref):
        @pl.when(pl.program_id(0) == 0)
        def _():
            dw_ref[...] = jnp.zeros_like(dw_ref)

        d = _bf(d_ref[...])
        dya_ref[...] = _dot_nt(d, w_ref[0:2].reshape(GROUP, D_MODEL))
        dyb_ref[...] = _dot_nt(d, w_ref[2:4].reshape(GROUP, D_MODEL))
        dw_ref[pl.ds(0, GROUP), :] += _dot_tn(_bf(ya_ref[...]), d)
        dw_ref[pl.ds(GROUP, GROUP), :] += _dot_tn(_bf(yb_ref[...]), d)

    return pl.pallas_call(
        body, name="outproj_bwd", grid=(t // tm,),
        in_specs=[pl.BlockSpec((tm, D_MODEL), lambda i: (i, 0)),
                  pl.BlockSpec((tm, GROUP), lambda i: (i, 0)),
                  pl.BlockSpec((tm, GROUP), lambda i: (i, 0)),
                  _rows_spec(ROWS_W_IN + layer * (D_MODEL // N_SHARD))],
        out_specs=[pl.BlockSpec((tm, GROUP), lambda i: (i, 0)),
                   pl.BlockSpec((tm, GROUP), lambda i: (i, 0)),
                   pl.BlockSpec((D_MODEL, D_MODEL), lambda i: (0, 0))],
        out_shape=[_sds((t, GROUP)), _sds((t, GROUP)), _sds((D_MODEL, D_MODEL))],
        compiler_params=_params(("arbitrary",)),
    )(dh, ya, yb, gathered)


def _inproj_bwd_dx(dres, h, g, gathered, layer, da, db, exchange=None):
    t = h.shape[0]
    tm = _row_tile(t, 256)

    def body(dres_ref, h_ref, g_ref, w_ref, da_ref, db_ref, dh_ref, dg_ref):
        @pl.when(pl.program_id(0) == 0)
        def _():
            dg_ref[...] = jnp.zeros_like(dg_ref)

        du = jnp.zeros((tm, D_MODEL), F32)
        for i in range(8):
            part = da_ref[i] if i < 4 else db_ref[i - 4]
            du = du + _dot_nt(part, w_ref[i // 2, :, pl.ds((i % 2) * GROUP, GROUP)])
        xh, r = _rms(h_ref[...])
        dg_ref[...] += jnp.sum(du * xh, axis=0, keepdims=True)
        dh_ref[...] = dres_ref[...] + _rms_bwd(du * g_ref[...], xh, r)

    in_specs = [pl.BlockSpec((tm, D_MODEL), lambda i: (i, 0)),
                pl.BlockSpec((tm, D_MODEL), lambda i: (i, 0)),
                pl.BlockSpec((1, D_MODEL), lambda i: (0, 0)),
                pl.BlockSpec((N_SHARD, D_MODEL, D_MODEL), lambda i: (0, layer, 0)),
                pl.BlockSpec((4, tm, GROUP), lambda i: (0, i, 0)),
                pl.BlockSpec((4, tm, GROUP), lambda i: (0, i, 0))]
    out_specs = [pl.BlockSpec((tm, D_MODEL), lambda i: (i, 0)), pl.BlockSpec((1, D_MODEL), lambda i: (0, 0))]
    out_shape = [_sds((t, D_MODEL)), _sds((1, D_MODEL))]
    operands = [dres, h, g, gathered, da, db]
    scratch = []
    if exchange is not None:
        body = _with_exchange(body, 6, 2, exchange, lambda: pl.program_id(0), t // tm)
        xi, xo, xs, scratch, xop = _exchange_args(exchange)
        in_specs, out_specs, out_shape, operands = in_specs + xi, out_specs + xo, out_shape + xs, operands + xop
    return pl.pallas_call(
        body, name="inproj_bwd_dx" if exchange is None else "inproj_bwd_dx_exchange", grid=(t // tm,),
        in_specs=in_specs, out_specs=out_specs, out_shape=out_shape, scratch_shapes=scratch,
        compiler_params=_params(("arbitrary",)),
    )(*operands)


def _inproj_bwd_dw(h, g, da, db):
    t = h.shape[0]
    tm = _row_tile(t, 512)

    def body(h_ref, g_ref, da_ref, db_ref, dw_ref):
        @pl.when(pl.program_id(0) == 0)
        def _():
            dw_ref[...] = jnp.zeros_like(dw_ref)

        xh, _ = _rms(h_ref[...])
        u = _bf(xh * g_ref[...])
        for i in range(8):
            dw_ref[i // 2, :, pl.ds((i % 2) * GROUP, GROUP)] += _dot_tn(u, da_ref[i] if i < 4 else db_ref[i - 4])

    return pl.pallas_call(
        body, name="inproj_bwd_dw", grid=(t // tm,),
        in_specs=[pl.BlockSpec((tm, D_MODEL), lambda i: (i, 0)),
                  pl.BlockSpec((1, D_MODEL), lambda i: (0, 0)),
                  pl.BlockSpec((4, tm, GROUP), lambda i: (0, i, 0)),
                  pl.BlockSpec((4, tm, GROUP), lambda i: (0, i, 0))],
        out_specs=pl.BlockSpec((N_SHARD, D_MODEL, D_MODEL), lambda i: (0, 0, 0)),
        out_shape=_sds((N_SHARD, D_MODEL, D_MODEL)), compiler_params=_params(("arbitrary",), VMEM_LIMIT_BIG),
    )(h, g, da, db)


def _final(h, g, target):
    t = h.shape[0]
    tm = _row_tile(t, 512)

    def body(h_ref, g_ref, t_ref, dh_ref, dg_ref, loss_ref):
        @pl.when(pl.program_id(0) == 0)
        def _():
            dg_ref[...] = jnp.zeros_like(dg_ref)
            loss_ref[...] = jnp.zeros_like(loss_ref)

        xh, r = _rms(h_ref[...])
        gg = g_ref[...]
        err = xh * gg - t_ref[...]
        part = 0.5 * jnp.sum(jnp.mean(err * err, axis=-1, keepdims=True), axis=0, keepdims=True)
        loss_ref[...] += jnp.broadcast_to(part, loss_ref.shape)
        dy = err * (1.0 / D_MODEL)
        dg_ref[...] += jnp.sum(dy * xh, axis=0, keepdims=True)
        dh_ref[...] = _rms_bwd(dy * gg, xh, r)

    return pl.pallas_call(
        body, name="final", grid=(t // tm,),
        in_specs=[pl.BlockSpec((tm, D_MODEL), lambda i: (i, 0)),
                  pl.BlockSpec((1, D_MODEL), lambda i: (0, 0)),
                  pl.BlockSpec((tm, D_MODEL), lambda i: (i, 0))],
        out_specs=[pl.BlockSpec((tm, D_MODEL), lambda i: (i, 0)),
                   pl.BlockSpec((1, D_MODEL), lambda i: (0, 0)),
                   pl.BlockSpec((1, 128), lambda i: (0, 0))],
        out_shape=[_sds((t, D_MODEL)), _sds((1, D_MODEL)), _sds((1, 128))],
        compiler_params=_params(("arbitrary",)),
    )(h, g, target)


def _hgrn_consts():
    c, nl = HG_CHUNK, HG_LEVELS
    t = np.arange(c)
    tril = np.tril(np.ones((c, c), np.float32))
    masks = np.zeros((nl + 1, c, c), np.float32)
    masks[0] = np.eye(c, dtype=np.float32)
    dmat = np.zeros(((nl + 2) * c, c), np.float32)
    dmat[0:c] = tril
    for l in range(nl):
        m = c >> (l + 1)
        blk = t // (2 * m)
        r = blk * 2 * m + m - 1
        upper = (t % (2 * m)) >= m
        masks[l + 1] = ((blk[:, None] == blk[None, :]) & upper[:, None] & (~upper)[None, :]).astype(np.float32)
        dmat[(l + 1) * c:(l + 2) * c] = tril[t] - tril[r]
    dmat[(nl + 1) * c:] = np.triu(np.ones((c, c), np.float32), k=1)
    return jnp.asarray(masks), jnp.asarray(dmat, BF16)


HG_HEADS = 4


def _hgrn_pre(aq, af, lb):
    sq = _sigmoid(aq)
    sneg = _sigmoid(-af)
    kk = (1.0 - lb) * sneg
    return sq, aq * sq, sneg, kk, jnp.log1p(-kk)


def _hgrn_x(logf, dmat_ref):
    dm = dmat_ref[pl.ds(0, (HG_LEVELS + 1) * HG_CHUNK), :]
    lhi, llo = _split(logf)
    return _dot(dm, lhi) + _dot(dm, llo)


def _hgrn_level(x_all, l, q, kk):
    c = HG_CHUNK
    x = x_all[(l + 1) * c:(l + 2) * c]
    qf = jnp.exp(jnp.minimum(x, 0.0))
    kf = jnp.exp(-jnp.maximum(x, 0.0))
    return qf, kf, _bf(q * qf), _bf(kk * kf)


def _hgrn_scores(xs, qs, kks, mask_ref):
    ps = [mask_ref[0] * _dot_nt(_bf(q), _bf(kk)) for q, kk in zip(qs, kks)]
    for l in range(HG_LEVELS):
        for i, (x_all, q, kk) in enumerate(zip(xs, qs, kks)):
            _, _, ql, kl = _hgrn_level(x_all, l, q, kk)
            ps[i] = ps[i] + mask_ref[l + 1] * _dot_nt(ql, kl)
    return ps


def _hgrn_specs(n_chunks, rev):
    c, w = HG_CHUNK, HG_HEADS * A_D
    cidx = (lambda n: n_chunks - 1 - n) if rev else (lambda n: n)
    col = lambda g: pl.BlockSpec((c, w), lambda h, n: (cidx(n), g * (A_HEADS // HG_HEADS) + h))
    vec = pl.BlockSpec((1, w), lambda h, n: (0, h))
    mask = pl.BlockSpec((HG_LEVELS + 1, c, c), lambda h, n: (0, 0, 0))
    dmat = pl.BlockSpec(((HG_LEVELS + 2) * c, c), lambda h, n: (0, 0))
    state = pl.BlockSpec((HG_HEADS, None, A_D, A_D), lambda h, n: (h, cidx(n), 0, 0))
    return cidx, col, vec, mask, dmat, state


def _lanes(i):
    return pl.ds(i * A_D, A_D)


def _hgrn_fwd(proj, lb, gain):
    t = proj.shape[0]
    c = HG_CHUNK
    nch = t // c
    masks, dmat = _hgrn_consts()
    cidx, col, vec, mask_spec, dmat_spec, state_spec = _hgrn_specs(nch, False)
    heads = range(HG_HEADS)

    def body(aq_ref, af_ref, ai_ref, ag_ref, lb_ref, gain_ref, mask_ref, dmat_ref, y_ref, st_ref, s_scr):
        @pl.when(pl.program_id(1) == 0)
        def _():
            s_scr[...] = jnp.zeros_like(s_scr)

        pre = [_hgrn_pre(aq_ref[:, _lanes(i)], af_ref[:, _lanes(i)], lb_ref[:, _lanes(i)]) for i in heads]
        qs, kks = [p[1] for p in pre], [p[3] for p in pre]
        xs = [_hgrn_x(p[4], dmat_ref) for p in pre]
        bs = [x[0:c] for x in xs]
        b_lasts = [jnp.sum(p[4], axis=0, keepdims=True) for p in pre]
        ps = _hgrn_scores(xs, qs, kks, mask_ref)
        ss = [s_scr[i] for i in heads]
        vbs = [_bf(ai_ref[:, _lanes(i)]) for i in heads]
        os_ = [_dot(_bf(ps[i]), vbs[i]) + _dot_nt(_bf(qs[i] * jnp.exp(bs[i])), _bf(ss[i])) for i in heads]
        for i in heads:
            st_ref[i] = ss[i]
            s_scr[i] = ss[i] * jnp.exp(b_lasts[i]) + _dot_tn(vbs[i], _bf(kks[i] * jnp.exp(b_lasts[i] - bs[i])))
            oh, _ = _rms(os_[i])
            ag = ag_ref[:, _lanes(i)]
            y_ref[:, _lanes(i)] = oh * gain_ref[:, _lanes(i)] * (ag * _sigmoid(ag))

    return pl.pallas_call(
        body, name="hgrn_fwd", grid=(A_HEADS // HG_HEADS, nch),
        in_specs=[col(0), col(1), col(2), col(3), vec, vec, mask_spec, dmat_spec],
        out_specs=[pl.BlockSpec((c, HG_HEADS * A_D), lambda h, n: (n, h)), state_spec],
        out_shape=[_sds((t, GROUP)), _sds((A_HEADS, nch, A_D, A_D))],
        scratch_shapes=[pltpu.VMEM((HG_HEADS, A_D, A_D), F32)],
        compiler_params=_params(("arbitrary", "arbitrary")),
    )(proj, proj, proj, proj, lb, gain, masks, dmat)


def _hgrn_bwd(proj, lb, gain, states, dya):
    t = proj.shape[0]
    c, nl = HG_CHUNK, HG_LEVELS
    nch = t // c
    masks, dmat = _hgrn_consts()
    cidx, col, vec, mask_spec, dmat_spec, state_spec = _hgrn_specs(nch, True)
    heads = range(HG_HEADS)

    def body(aq_ref, af_ref, ai_ref, ag_ref, lb_ref, gain_ref, mask_ref, dmat_ref, st_ref, dy_ref,
             da_ref, dlb_ref, dgain_ref, ds_scr, z_scr):
        @pl.when(pl.program_id(1) == 0)
        def _():
            ds_scr[...] = jnp.zeros_like(ds_scr)
            dlb_ref[...] = jnp.zeros_like(dlb_ref)
            dgain_ref[...] = jnp.zeros_like(dgain_ref)

        aqs = [aq_ref[:, _lanes(i)] for i in heads]
        lbs = [lb_ref[:, _lanes(i)] for i in heads]
        pre = [_hgrn_pre(aqs[i], af_ref[:, _lanes(i)], lbs[i]) for i in heads]
        sqs, qs, snegs, kks = ([p[j] for p in pre] for j in range(4))
        xs = [_hgrn_x(p[4], dmat_ref) for p in pre]
        bs = [x[0:c] for x in xs]
        b_lasts = [jnp.sum(p[4], axis=0, keepdims=True) for p in pre]
        ebs = [jnp.exp(b) for b in bs]
        ebls = [jnp.exp(bl - b) for bl, b in zip(b_lasts, bs)]
        ebl_rows = [jnp.exp(bl) for bl in b_lasts]
        qes = [_bf(q * eb) for q, eb in zip(qs, ebs)]
        kes = [_bf(kk * ebl) for kk, ebl in zip(kks, ebls)]
        vbs = [_bf(ai_ref[:, _lanes(i)]) for i in heads]
        ss = [st_ref[i] for i in heads]
        sbs = [_bf(s) for s in ss]
        dss = [ds_scr[i] for i in heads]
        dsbs = [_bf(ds) for ds in dss]

        pbs = [_bf(p) for p in _hgrn_scores(xs, qs, kks, mask_ref)]
        os_ = [_dot(pbs[i], vbs[i]) + _dot_nt(qes[i], sbs[i]) for i in heads]

        dos = []
        for i in heads:
            ag, gain, dy = ag_ref[:, _lanes(i)], gain_ref[:, _lanes(i)], dy_ref[:, _lanes(i)]
            oh, r = _rms(os_[i])
            sg_sig = _sigmoid(ag)
            sg = ag * sg_sig
            da_ref[3, :, _lanes(i)] = _bf(dy * oh * gain * _silu_grad(ag, sg_sig))
            dgain_ref[:, _lanes(i)] += jnp.sum(dy * oh * sg, axis=0, keepdims=True)
            dos.append(_bf(_rms_bwd(dy * gain * sg, oh, r)))

        dps = [_dot_nt(dos[i], vbs[i]) for i in heads]
        for i in heads:
            da_ref[2, :, _lanes(i)] = _bf(_dot_tn(pbs[i], dos[i]) + _dot_nt(kes[i], dsbs[i]))
        dq_ss = [ebs[i] * _dot(dos[i], sbs[i]) for i in heads]
        dk_ss = [ebls[i] * _dot(vbs[i], dsbs[i]) for i in heads]
        dqs, dks = [], []
        for i in heads:
            dpd = jnp.sum(mask_ref[0] * dps[i], axis=1, keepdims=True)
            z_scr[i, pl.ds(0, c), :] = qs[i] * dq_ss[i]
            z_scr[i, pl.ds((nl + 1) * c, c), :] = kks[i] * dk_ss[i]
            dqs.append(dq_ss[i] + dpd * kks[i])
            dks.append(dk_ss[i] + dpd * qs[i])
        for l in range(nl):
            for i in heads:
                qf, kf, ql, kl = _hgrn_level(xs[i], l, qs[i], kks[i])
                dpl = _bf(mask_ref[l + 1] * dps[i])
                dq_l = qf * _dot(dpl, kl)
                dk_l = kf * _dot_tn(dpl, ql)
                z_scr[i, pl.ds((l + 1) * c, c), :] = qs[i] * dq_l - kks[i] * dk_l
                dqs[i] = dqs[i] + dq_l
                dks[i] = dks[i] + dk_l

        zsplits = [_split(z_scr[i]) for i in heads]
        dlogfs = [_dot_tn(dmat_ref[...], zhi) + _dot_tn(dmat_ref[...], zlo) for zhi, zlo in zsplits]
        ds_new = [_dot_tn(dos[i], qes[i]) for i in heads]
        for i in heads:
            dlogf = dlogfs[i] + ebl_rows[i] * jnp.sum(dss[i] * ss[i], axis=0, keepdims=True)
            dkk = dks[i] - dlogf / (1.0 - kks[i])
            da_ref[1, :, _lanes(i)] = _bf(dkk * (1.0 - lbs[i]) * (-(snegs[i] * (1.0 - snegs[i]))))
            dlb_ref[:, _lanes(i)] += jnp.sum(dkk * (-snegs[i]), axis=0, keepdims=True)
            da_ref[0, :, _lanes(i)] = _bf(dqs[i] * _silu_grad(aqs[i], sqs[i]))
            ds_scr[i] = dss[i] * ebl_rows[i] + ds_new[i]

    w = HG_HEADS * A_D
    return pl.pallas_call(
        body, name="hgrn_bwd", grid=(A_HEADS // HG_HEADS, nch),
        in_specs=[col(0), col(1), col(2), col(3), vec, vec, mask_spec, dmat_spec, state_spec,
                  pl.BlockSpec((c, w), lambda h, n: (cidx(n), h))],
        out_specs=[pl.BlockSpec((4, c, w), lambda h, n: (0, cidx(n), h)), vec, vec],
        out_shape=[_sds((4, t, GROUP), BF16)] + [_sds((1, GROUP))] * 2,
        scratch_shapes=[pltpu.VMEM((HG_HEADS, A_D, A_D), F32), pltpu.VMEM((HG_HEADS, (nl + 2) * c, A_D), F32)],
        compiler_params=_params(("arbitrary", "arbitrary")),
    )(proj, proj, proj, proj, lb, gain, masks, dmat, states, dya)


def _sb_consts():
    j = np.arange(SB_TK)
    strict = (j[:, None] > j[None, :]).astype(np.float32)
    incl = (j[:, None] >= j[None, :]).astype(np.float32)
    return jnp.asarray(strict, BF16), jnp.asarray(incl, BF16)


def _lane0(x):
    return jnp.broadcast_to(x[:, 0:1], x.shape)


def _causal(x, masked):
    if not masked:
        return x
    n = (SB_TK, SB_TK)
    top = jnp.where(lax.broadcasted_iota(jnp.int32, n, 1) < lax.broadcasted_iota(jnp.int32, n, 0), x[:SB_TK], 0.0)
    return top if x.shape[0] == SB_TK else jnp.concatenate([top, x[SB_TK:]], axis=0)


def _sb_softplus(z, masked):
    logsig = jnp.minimum(z, 0.0) - jnp.log2(1.0 + jnp.exp2(-jnp.abs(z)))
    return _causal(z - logsig, masked), logsig


def _sb_cumsum(sp, cmat):
    cs = _dot(_bf(sp), cmat)
    return cs, _lane0(cs + sp)


def _sb_sweep(qi, group_fn, state, group):
    nd = SB_TQ // SB_TK

    def run(tiles, st):
        for i in range(0, len(tiles), group):
            st = group_fn(tiles[i:i + group], st)
        return st

    state = run([(pl.multiple_of((qi * nd + d) * SB_TK, SB_TK), d * SB_TK, True) for d in reversed(range(nd))], state)

    def step(j, st):
        return run([(pl.multiple_of(((qi - j) * nd - 1 - g) * SB_TK, SB_TK), 0, False) for g in range(nd)], st)

    return lax.fori_loop(0, qi, step, state)


def _set_rows(r0, full, new):
    return new if r0 == 0 else jnp.concatenate([full[:r0], new], axis=0)


def _sb_specs(t, tq):
    col = lambda g: pl.BlockSpec((tq, 2 * B_D), lambda p, i, h: (i, g * (GROUP // (2 * B_D)) + p))
    full = lambda g: pl.BlockSpec((t, 2 * B_D), lambda p, i, h: (0, g * (GROUP // (2 * B_D)) + p))
    vec = pl.BlockSpec((1, 2 * B_D), lambda p, i, h: (0, p))
    mat = pl.BlockSpec((SB_TK, SB_TK), lambda p, i, h: (0, 0))
    return col, full, vec, mat


def _head_lanes(h):
    return (lax.broadcasted_iota(jnp.int32, (1, 2 * B_D), 1) >= B_D) == (h == 1)


def _put(ref, h, val):
    @pl.when(h == 0)
    def _():
        ref[...] = val

    @pl.when(h == 1)
    def _():
        ref[...] += val


def _sb_fwd(proj_bf, proj, gain, gather=None):
    t = proj.shape[0]
    tq = SB_TQ
    strict, _ = _sb_consts()
    n_steps = (B_HEADS // 2, t // tq, 2)

    def body(q_ref, k_ref, v_ref, bg_ref, gain_ref, m_ref, *rest):
        if gather is None:
            o_ref, y_ref = rest
        else:
            flat_ref, _, o_ref, y_ref, gathered_ref, send_sems, recv_sems = rest
            start, forward, finish = _gather_plan(flat_ref, gathered_ref, send_sems, recv_sems, *gather[2:])
            step = (pl.program_id(0) * n_steps[1] + pl.program_id(1)) * n_steps[2] + pl.program_id(2)
            pl.when(step == 0)(start)
            pl.when(step == 2 * n_steps[1] * n_steps[2])(forward)
            pl.when(step == n_steps[0] * n_steps[1] * n_steps[2] - 1)(finish)
        h = pl.program_id(2)
        lanes = _head_lanes(h)
        qb = jnp.where(lanes, q_ref[...], jnp.zeros_like(q_ref))
        cmat = m_ref[...]

        def group(tiles, state):
            carry, acc = state
            kv = [(k_ref[pl.ds(off, SB_TK), :], v_ref[pl.ds(off, SB_TK), :]) for off, _, _ in tiles]
            zs = [_dot_nt(qb[r0:], kb) for (_, r0, _), (kb, _) in zip(tiles, kv)]
            sps = [_sb_softplus(z, masked) for z, (_, _, masked) in zip(zs, tiles)]
            css = [_sb_cumsum(sp, cmat) for sp, _ in sps]
            ws = []
            for (_, logsig), (cs, tot), (_, r0, masked) in zip(sps, css, tiles):
                ws.append(_split(_causal(jnp.exp2(logsig - cs - carry[r0:]), masked)))
                carry = _set_rows(r0, carry, carry[r0:] + tot)
            for (whi, wlo), (_, vb), (_, r0, _) in zip(ws, kv, tiles):
                acc = _set_rows(r0, acc, acc[r0:] + _dot(whi, vb) + _dot(wlo, vb))
            return carry, acc

        _, acc = _sb_sweep(pl.program_id(1), group, (jnp.zeros((tq, SB_TK), F32), jnp.zeros((tq, 2 * B_D), F32)),
                           SB_GROUP_FWD)
        o = jnp.where(lanes, acc, 0.0)
        oh = o * lax.rsqrt(jnp.sum(o * o, axis=-1, keepdims=True) * (1.0 / B_D) + EPS)
        bg = bg_ref[...]
        _put(o_ref, h, o)
        _put(y_ref, h, oh * gain_ref[...] * (bg * _sigmoid(bg)))

    col, full, vec, mat = _sb_specs(t, tq)
    out = pl.BlockSpec((tq, 2 * B_D), lambda p, i, h: (i, p))
    in_specs = [col(4), full(5), full(6), col(7), vec, mat]
    out_specs = [out, out]
    out_shape = [_sds((t, GROUP)), _sds((t, GROUP))]
    operands = [proj_bf, proj_bf, proj_bf, proj, gain, strict]
    extra = {}
    if gather is not None:
        in_specs += [_ANY, _ANY]
        out_specs += [_ANY]
        out_shape += [_sds(gather[1].shape, gather[1].dtype)]
        operands += [gather[0], gather[1]]
        extra = dict(input_output_aliases={7: 2}, scratch_shapes=_gather_sems(gather[4]))
    return pl.pallas_call(
        body, name="sb_fwd" if gather is None else "sb_fwd_gather", grid=n_steps,
        in_specs=in_specs, out_specs=out_specs, out_shape=out_shape,
        compiler_params=_params(("arbitrary", "arbitrary", "arbitrary")), **extra,
    )(*operands)


def _sb_bwd(proj_bf, proj, o, dy, gain, exchange=None):
    t = proj.shape[0]
    tq = SB_TQ
    strict, incl = _sb_consts()

    def body(q_ref, k_ref, v_ref, bg_ref, o_ref, dy_ref, gain_ref, ms_ref, mi_ref,
             dq_ref, dk_ref, dv_ref, dbg_ref, dgain_ref):
        qi = pl.program_id(1)
        h = pl.program_id(2)
        lanes = _head_lanes(h)

        @pl.when((qi == 0) & (h == 0))
        def _():
            dk_ref[...] = jnp.zeros_like(dk_ref)
            dv_ref[...] = jnp.zeros_like(dv_ref)
            dgain_ref[...] = jnp.zeros_like(dgain_ref)

        qb = jnp.where(lanes, q_ref[...], jnp.zeros_like(q_ref))
        cmat = ms_ref[...]
        imat = mi_ref[...]
        o = jnp.where(lanes, o_ref[...], 0.0)
        dy = jnp.where(lanes, dy_ref[...], 0.0)
        bg = bg_ref[...]
        gain = gain_ref[...]
        r = lax.rsqrt(jnp.sum(o * o, axis=-1, keepdims=True) * (1.0 / B_D) + EPS)
        oh = o * r
        sig = _sigmoid(bg)
        sg = bg * sig
        _put(dbg_ref, h, dy * oh * gain * _silu_grad(bg, sig))
        dgain_ref[...] += jnp.sum(dy * oh * sg, axis=0, keepdims=True)
        doh = dy * gain * sg
        do = _bf(r * (doh - oh * (jnp.sum(doh * oh, axis=-1, keepdims=True) * (1.0 / B_D))))
        total = jnp.broadcast_to(jnp.sum(do.astype(F32) * o, axis=1, keepdims=True), (tq, SB_TK))

        def group(tiles, state):
            carry, gcarry, dq = state
            kv = [(k_ref[pl.ds(off, SB_TK), :], v_ref[pl.ds(off, SB_TK), :]) for off, _, _ in tiles]
            zs = [_dot_nt(qb[r0:], kb) for (_, r0, _), (kb, _) in zip(tiles, kv)]
            dws = [_dot_nt(do[r0:], vb) for (_, r0, _), (_, vb) in zip(tiles, kv)]
            sps = [_sb_softplus(z, masked) for z, (_, _, masked) in zip(zs, tiles)]
            css = [_sb_cumsum(sp, cmat) for sp, _ in sps]
            ws, gs = [], []
            for (_, logsig), (cs, tot), dw, (_, r0, masked) in zip(sps, css, dws, tiles):
                w = _causal(jnp.exp2(logsig - cs - carry[r0:]), masked)
                ws.append(_bf(w))
                gs.append(dw * w)
                carry = _set_rows(r0, carry, carry[r0:] + tot)
            s2s = []
            for g in gs:
                ghi, glo = _split(g)
                s2s.append(_dot(ghi, imat) + _dot(glo, imat))
            dzs = []
            for (_, logsig), g, s2, (_, r0, masked) in zip(sps, gs, s2s, tiles):
                before = total[r0:] - gcarry[r0:] - s2
                dz = g - jnp.exp2(logsig) * (g + before)
                dzs.append(_bf(_causal(dz, masked)))
                gcarry = _set_rows(r0, gcarry, gcarry[r0:] + _lane0(s2))
            for dz, wb, (kb, _), (off, r0, _) in zip(dzs, ws, kv, tiles):
                dq = _set_rows(r0, dq, dq[r0:] + _dot(dz, kb))
                dk_ref[pl.ds(off, SB_TK), :] += _dot_tn(dz, qb[r0:])
                dv_ref[pl.ds(off, SB_TK), :] += _dot_tn(wb, do[r0:])
            return carry, gcarry, dq

        zero = jnp.zeros((tq, SB_TK), F32)
        _, _, dq = _sb_sweep(qi, group, (zero, zero, jnp.zeros((tq, 2 * B_D), F32)), SB_GROUP_BWD)
        _put(dq_ref, h, jnp.where(lanes, dq * (B_D ** -0.5), 0.0))

    col, full, vec, mat = _sb_specs(t, tq)
    blk = pl.BlockSpec((tq, 2 * B_D), lambda p, i, h: (i, p))
    whole = pl.BlockSpec((t, 2 * B_D), lambda p, i, h: (0, p))
    grid = (B_HEADS // 2, t // tq, 2)
    in_specs = [col(4), full(5), full(6), col(7), blk, blk, vec, mat, mat]
    out_specs = [blk, whole, whole, blk, vec]
    out_shape = [_sds((t, GROUP))] * 4 + [_sds((1, GROUP))]
    operands = [proj_bf, proj_bf, proj_bf, proj, o, dy, gain, strict, incl]
    scratch = []
    if exchange is not None:
        step_of = lambda: (pl.program_id(0) * grid[1] + pl.program_id(1)) * grid[2] + pl.program_id(2)
        body = _with_exchange(body, 9, 5, exchange, step_of, grid[0] * grid[1] * grid[2])
        xi, xo, xs, scratch, xop = _exchange_args(exchange)
        in_specs, out_specs, out_shape, operands = in_specs + xi, out_specs + xo, out_shape + xs, operands + xop
    return pl.pallas_call(
        body, name="sb_bwd" if exchange is None else "sb_bwd_exchange", grid=grid,
        in_specs=in_specs, out_specs=out_specs, out_shape=out_shape, scratch_shapes=scratch,
        compiler_params=_params(("arbitrary", "arbitrary", "arbitrary")),
    )(*operands)


def _adamw(w, g, m, v):
    rows, cols = w.shape
    tr = rows
    for cand in (400, 256, 128, 64, 32, 16, 8):
        if rows % cand == 0:
            tr = cand
            break

    def body(w_ref, g_ref, m_ref, v_ref, d_ref, nm_ref, nv_ref):
        g_ = g_ref[...]
        m_ = ADAM_B1 * m_ref[...] + (1.0 - ADAM_B1) * g_
        v_ = ADAM_B2 * v_ref[...] + (1.0 - ADAM_B2) * (g_ * g_)
        m_hat = m_ / (1.0 - ADAM_B1 ** ADAM_STEP)
        v_hat = v_ / (1.0 - ADAM_B2 ** ADAM_STEP)
        d_ref[...] = -ADAM_LR * (m_hat / (jnp.sqrt(v_hat) + ADAM_EPS) + ADAM_WD * w_ref[...])
        nm_ref[...] = m_
        nv_ref[...] = v_

    spec = pl.BlockSpec((tr, cols), lambda i: (i, 0))
    return pl.pallas_call(
        body, name="adamw", grid=(rows // tr,), in_specs=[spec] * 4, out_specs=[spec] * 3,
        out_shape=[_sds((rows, cols))] * 3, compiler_params=_params(("arbitrary",)),
    )(w, g, m, v)


_ANY = pl.BlockSpec(memory_space=pl.ANY)


def _place():
    return lax.axis_index("x"), lax.axis_index("y"), lax.axis_index("c")


def _gather_plan(x_ref, out_ref, send_sems, recv_sems, row0, nrows, nc):
    x, y, c = _place()
    me = 2 * x + y
    sibling = (x, y, 1 - c)
    half = nrows // 2
    ch = half // nc
    peers = [me ^ k for k in (1, 2, 3)]

    def rows(shard, hc, r):
        return out_ref.at[shard, pl.ds(row0 + hc * half + r * ch, ch), :]

    def copy(k, shard, hc, r, to, src=None):
        return pltpu.make_async_remote_copy(
            src_ref=rows(shard, hc, r) if src is None else src, dst_ref=rows(shard, hc, r),
            send_sem=send_sems.at[k * nc + r], recv_sem=recv_sems.at[k * nc + r], device_id=to, device_id_type=MESH)

    def first(k, p, r):
        return copy(k, me, c, r, (p >> 1, p & 1, c), src=x_ref.at[pl.ds(row0 + c * half + r * ch, ch), :])

    def start():
        for k, p in enumerate(peers):
            for r in range(nc):
                first(k, p, r).start()

    def forward():
        for k, p in enumerate(peers):
            for r in range(nc):
                copy(k, p, c, r, sibling).wait_recv()
                copy(3 + k, p, c, r, sibling).start()

    def finish():
        for k, p in enumerate(peers):
            for r in range(nc):
                copy(3 + k, p, 1 - c, r, sibling).wait_recv()
        for k, p in enumerate(peers):
            for r in range(nc):
                first(k, p, r).wait_send()
                copy(3 + k, p, c, r, sibling).wait_send()

    return start, forward, finish


def _gather_sems(nc):
    return [pltpu.SemaphoreType.DMA((6 * nc,)), pltpu.SemaphoreType.DMA((6 * nc,))]


def _swap_plan(g_ref, out_ref, send_sems, recv_sems):
    half, ch, nc = HALF_LAYER, CHUNK_ROWS, RS_CHUNKS

    def copies():
        x, y, c = _place()
        return [pltpu.make_async_remote_copy(
            src_ref=g_ref.at[j, pl.ds((1 - c) * half + r * ch, ch), :], dst_ref=out_ref.at[j, pl.ds(r * ch, ch), :],
            send_sem=send_sems.at[j * nc + r], recv_sem=recv_sems.at[j * nc + r],
            device_id=(x, y, 1 - c), device_id_type=MESH) for j in range(N_SHARD) for r in range(nc)]

    def start():
        for cp in copies():
            cp.start()

    def finish():
        for cp in copies():
            cp.wait()

    return start, finish


def _scatter_plan(p_ref, out_ref, send_sems, recv_sems):
    ch, nc = CHUNK_ROWS, RS_CHUNKS

    def copies():
        x, y, c = _place()
        me = 2 * x + y
        return [pltpu.make_async_remote_copy(
            src_ref=p_ref.at[me ^ k, pl.ds(r * ch, ch), :], dst_ref=out_ref.at[k - 1, pl.ds(r * ch, ch), :],
            send_sem=send_sems.at[(k - 1) * nc + r], recv_sem=recv_sems.at[(k - 1) * nc + r],
            device_id=((me ^ k) >> 1, (me ^ k) & 1, c), device_id_type=MESH) for k in (1, 2, 3) for r in range(nc)]

    def start():
        for cp in copies():
            cp.start()

    def finish():
        for cp in copies():
            cp.wait()

    return start, finish


SWAP = (_swap_plan, (N_SHARD, HALF_LAYER, D_MODEL), F32, N_SHARD * RS_CHUNKS)
SCATTER = (_scatter_plan, (3, HALF_LAYER, D_MODEL), BF16, 3 * RS_CHUNKS)


def _exchange_call(kind, operand):
    plan, shape, dtype, n_sems = kind

    def body(in_ref, out_ref, send_sems, recv_sems):
        start, finish = plan(in_ref, out_ref, send_sems, recv_sems)
        start()
        finish()

    return pl.pallas_call(
        body, name="exchange", in_specs=[_ANY], out_specs=_ANY, out_shape=_sds(shape, dtype),
        scratch_shapes=[pltpu.SemaphoreType.DMA((n_sems,)), pltpu.SemaphoreType.DMA((n_sems,))],
    )(operand)


def _with_exchange(body, n_in, n_out, exchange, step_of, n_steps):
    def wrapped(*refs):
        ins, src = refs[:n_in], refs[n_in]
        outs, dst = refs[n_in + 1:n_in + 1 + n_out], refs[n_in + 1 + n_out]
        send_sems, recv_sems = refs[n_in + 2 + n_out:n_in + 4 + n_out]
        start, finish = exchange[0][0](src, dst, send_sems, recv_sems)
        pl.when(step_of() == 0)(start)
        body(*ins, *outs, *refs[n_in + 4 + n_out:])
        pl.when(step_of() == n_steps - 1)(finish)

    return wrapped


def _exchange_args(exchange):
    (plan, shape, dtype, n_sems), operand = exchange
    sems = [pltpu.SemaphoreType.DMA((n_sems,)), pltpu.SemaphoreType.DMA((n_sems,))]
    return [_ANY], [_ANY], [_sds(shape, dtype)], sems, [operand]


def _gather_weights(flat, row0, nrows, nc):
    def body(x_ref, out_ref, send_sems, recv_sems):
        start, forward, finish = _gather_plan(x_ref, out_ref, send_sems, recv_sems, row0, nrows, nc)
        start()
        forward()
        finish()

    return pl.pallas_call(
        body, name="gather_weights", in_specs=[_ANY], out_specs=_ANY,
        out_shape=_sds((N_SHARD, ROWS_FLAT, D_MODEL), BF16), scratch_shapes=_gather_sems(nc),
    )(flat)


def _add_my_half(grads, recv):
    tr = 400
    nb = HALF_LAYER // tr
    core = lax.axis_index("c").astype(jnp.int32).reshape(1)

    def body(c_ref, g_ref, r_ref, o_ref, ob_ref):
        acc = g_ref[...] + r_ref[...]
        o_ref[...] = acc
        ob_ref[...] = _bf(acc)

    out = pl.BlockSpec((None, tr, D_MODEL), lambda j, i, c_ref: (j, i, 0))
    return pl.pallas_call(
        body, name="add_my_half",
        grid_spec=pltpu.PrefetchScalarGridSpec(
            num_scalar_prefetch=1, grid=(N_SHARD, nb),
            in_specs=[pl.BlockSpec((None, tr, D_MODEL), lambda j, i, c_ref: (j, c_ref[0] * nb + i, 0)), out],
            out_specs=[out, out]),
        out_shape=[_sds((N_SHARD, HALF_LAYER, D_MODEL)), _sds((N_SHARD, HALF_LAYER, D_MODEL), BF16)],
        compiler_params=_params(("arbitrary", "arbitrary")),
    )(core, grads, recv)


def _sum_scattered(part, recv):
    tr = 400
    chip = (2 * lax.axis_index("x") + lax.axis_index("y")).astype(jnp.int32).reshape(1)

    def body(c_ref, p_ref, r_ref, o_ref):
        acc = p_ref[...]
        for k in range(3):
            acc = acc + r_ref[k].astype(F32)
        o_ref[...] = acc

    return pl.pallas_call(
        body, name="sum_scattered",
        grid_spec=pltpu.PrefetchScalarGridSpec(
            num_scalar_prefetch=1, grid=(HALF_LAYER // tr,),
            in_specs=[pl.BlockSpec((None, tr, D_MODEL), lambda i, c_ref: (c_ref[0], i, 0)),
                      pl.BlockSpec((3, tr, D_MODEL), lambda i, c_ref: (0, i, 0))],
            out_specs=pl.BlockSpec((tr, D_MODEL), lambda i, c_ref: (i, 0))),
        out_shape=_sds((HALF_LAYER, D_MODEL)), compiler_params=_params(("arbitrary",)),
    )(chip, part, recv)


def _swap_reduced(mine):
    ch, nc = CHUNK_ROWS, RS_CHUNKS

    def body(r_ref, out_ref, send_sems, recv_sems):
        x, y, c = _place()
        copies = [pltpu.make_async_remote_copy(
            src_ref=r_ref.at[l, pl.ds(r * ch, ch), :], dst_ref=out_ref.at[l, pl.ds(r * ch, ch), :],
            send_sem=send_sems.at[l * nc + r], recv_sem=recv_sems.at[l * nc + r],
            device_id=(x, y, 1 - c), device_id_type=MESH) for l in range(2) for r in range(nc)]
        for cp in copies:
            cp.start()
        for cp in copies:
            cp.wait()

    return pl.pallas_call(
        body, name="swap_reduced", in_specs=[_ANY], out_specs=_ANY,
        out_shape=_sds((2, HALF_LAYER, D_MODEL)),
        scratch_shapes=[pltpu.SemaphoreType.DMA((2 * nc,)), pltpu.SemaphoreType.DMA((2 * nc,))],
    )(mine)


def _allreduce_small(vec):
    def body(v_ref, out_ref, buf, send_sems, recv_sems):
        x, y, c = _place()
        me = 4 * x + 2 * y + c
        buf[me] = v_ref[...]
        peers = [me ^ k for k in range(1, N_DEV)]
        sends = [pltpu.make_async_remote_copy(
            src_ref=v_ref, dst_ref=buf.at[me], send_sem=send_sems.at[k], recv_sem=recv_sems.at[k],
            device_id=(p >> 2, (p >> 1) & 1, p & 1), device_id_type=MESH) for k, p in enumerate(peers)]
        for cp in sends:
            cp.start()
        for k, p in enumerate(peers):
            pltpu.make_async_remote_copy(
                src_ref=v_ref, dst_ref=buf.at[p], send_sem=send_sems.at[k], recv_sem=recv_sems.at[k],
                device_id=(p >> 2, (p >> 1) & 1, p & 1), device_id_type=MESH).wait_recv()
        for cp in sends:
            cp.wait_send()
        acc = buf[0]
        for d in range(1, N_DEV):
            acc = acc + buf[d]
        out_ref[...] = acc

    vm = pl.BlockSpec(memory_space=pltpu.VMEM)
    return pl.pallas_call(
        body, name="allreduce_small", in_specs=[vm], out_specs=vm, out_shape=_sds((SMALL_ROWS, 128)),
        scratch_shapes=[pltpu.VMEM((N_DEV, SMALL_ROWS, 128), F32),
                        pltpu.SemaphoreType.DMA((N_DEV - 1,)), pltpu.SemaphoreType.DMA((N_DEV - 1,))],
    )(vec)


def _flatten_shard(w_in, w_out, w_pg, w_pp):
    return jnp.concatenate([w_in.reshape(-1, D_MODEL), w_out.reshape(-1, D_MODEL), w_pg.reshape(-1, D_MODEL),
                            w_pp.reshape(-1, D_MODEL)], axis=0)


def _unflatten_layers(flats):
    a, b, c = D_MODEL, D_MODEL + D_MODEL // N_SHARD, D_MODEL + 2 * (D_MODEL // N_SHARD)
    q = D_MODEL // N_SHARD
    return (jnp.stack([f[:a] for f in flats]), jnp.stack([f[a:b] for f in flats]),
            jnp.stack([f[b:c] for f in flats]), jnp.stack([f[c:].reshape(D_PLE, q) for f in flats]))


def _full_w_pp(gathered):
    c = ROWS_W_IN + ROWS_W_OUT + ROWS_W_PG
    q = D_MODEL // N_SHARD
    rpp = ROWS_W_PP // 2
    return [gathered[:, c + l * rpp:c + (l + 1) * rpp, :].reshape(N_SHARD, D_PLE, q).transpose(1, 0, 2)
            .reshape(D_PLE, D_MODEL) for l in range(2)]


def _layer_grads(dw_in, dw_out, dw_pg, dw_pp):
    q = D_MODEL // N_SHARD
    rpp = ROWS_W_PP // 2
    return jnp.concatenate([dw_in, dw_out.reshape(N_SHARD, q, D_MODEL), dw_pg.reshape(N_SHARD, q, D_MODEL),
                            dw_pp.reshape(D_PLE, N_SHARD, q).transpose(1, 0, 2).reshape(N_SHARD, rpp, D_MODEL)], axis=1)


def _lower_bounds(lb_logits):
    sm = jax.nn.softmax(lb_logits.astype(F32), axis=0)
    return jnp.cumsum(sm, axis=0) - sm[0:1]


def kernel(x, p, norm_mix, w_in, a_out_norm, b_out_norm, w_out, lb_logits, ple_gate_norm, w_ple_gate, w_ple_proj, ple_post_norm, final_norm, loss_target, m_norm_mix, m_w_in, m_a_out_norm, m_b_out_norm, m_w_out, m_lb_logits, m_ple_gate_norm, m_w_ple_gate, m_w_ple_proj, m_ple_post_norm, m_final_norm, v_norm_mix, v_w_in, v_a_out_norm, v_b_out_norm, v_w_out, v_lb_logits, v_ple_gate_norm, v_w_ple_gate, v_w_ple_proj, v_ple_post_norm, v_final_norm):
    t = x.shape[1]
    h0 = x.reshape(t, D_MODEL)
    target = loss_target.reshape(t, D_MODEL)
    pl_in = p.reshape(2, t, D_PLE)

    w_flat_bf = _flatten_shard(_bf(w_in), _bf(w_out), _bf(w_ple_gate), _bf(w_ple_proj))
    chip = 2 * lax.axis_index("x") + lax.axis_index("y")
    gathered = lax.dynamic_update_slice(_gather_weights(w_flat_bf, 0, D_MODEL, 2), w_flat_bf[None], (chip, 0, 0))
    lbs, lbs_vjp = jax.vjp(_lower_bounds, lb_logits)

    saved = []
    h = h0
    for l in range(2):
        g_mix = norm_mix[l].reshape(1, D_MODEL)
        lb = lbs[l].reshape(1, GROUP)
        ga = a_out_norm[l].reshape(1, GROUP)
        gb = b_out_norm[l].reshape(1, GROUP)
        proj, proj_bf = _inproj(h, g_mix, gathered, l)
        ya, states = _hgrn_fwd(proj, lb, ga)
        if l == 0:
            ob, yb, gathered = _sb_fwd(proj_bf, proj, gb, (w_flat_bf, gathered, D_MODEL, ROWS_FLAT - D_MODEL, 4))
            w_pps = _full_w_pp(gathered)
        else:
            ob, yb = _sb_fwd(proj_bf, proj, gb)
        h1 = _outproj(h, ya, yb, gathered, l)
        g_post = ple_post_norm[l].reshape(1, D_MODEL)
        g_gate = ple_gate_norm[l].reshape(1, D_MODEL)
        h2 = _ple_fwd(h1, pl_in[l], w_pps[l], gathered, l, g_post, g_gate)
        saved.append((h, proj, proj_bf, states, ya, yb, ob, h1))
        h = h2

    dh, d_final, loss_part = _final(h, final_norm.reshape(1, D_MODEL), target)

    g_layer, chip_sum, scattered = [None] * 2, [None] * 2, [None] * 2
    d_mix, d_a, d_b, d_lb, d_gate, d_post = [None] * 2, [None] * 2, [None] * 2, [None] * 2, [None] * 2, [None] * 2
    for l in (1, 0):
        h_in, proj, proj_bf, states, ya, yb, ob, h1 = saved[l]
        g_mix = norm_mix[l].reshape(1, D_MODEL)
        lb = lbs[l].reshape(1, GROUP)
        ga = a_out_norm[l].reshape(1, GROUP)
        gb = b_out_norm[l].reshape(1, GROUP)
        g_post = ple_post_norm[l].reshape(1, D_MODEL)
        g_gate = ple_gate_norm[l].reshape(1, D_MODEL)
        if l == 1:
            dh1, dw_pg, dw_pp, d_gate[l], d_post[l] = _ple_bwd(dh, h1, pl_in[l], w_pps[l], gathered, l, g_post, g_gate)
            dya, dyb, dw_out = _outproj_bwd(dh1, ya, yb, gathered, l)
            dbq, dbk, dbv, dbg, d_b[l] = _sb_bwd(proj_bf, proj, ob, dyb, gb)
        else:
            dh1, dw_pg, dw_pp, d_gate[l], d_post[l], from_sibling = _ple_bwd(
                dh, h1, pl_in[l], w_pps[l], gathered, l, g_post, g_gate, (SWAP, g_layer[1]))
            chip_sum[1], chip_sum_bf = _add_my_half(g_layer[1], from_sibling)
            dya, dyb, dw_out = _outproj_bwd(dh1, ya, yb, gathered, l)
            dbq, dbk, dbv, dbg, d_b[l], scattered[1] = _sb_bwd(proj_bf, proj, ob, dyb, gb, (SCATTER, chip_sum_bf))
        da, d_lb[l], d_a[l] = _hgrn_bwd(proj, lb, ga, states, dya)
        db = jnp.stack([dbq, dbk * LN2, dbv, dbg]).astype(BF16)
        g_layer[l] = _layer_grads(_inproj_bwd_dw(h_in, g_mix, da, db), dw_out, dw_pg, dw_pp)
        if l == 1:
            dh, d_mix[l] = _inproj_bwd_dx(dh1, h_in, g_mix, gathered, l, da, db)
        else:
            dh, d_mix[l], from_sibling = _inproj_bwd_dx(dh1, h_in, g_mix, gathered, l, da, db, (SWAP, g_layer[0]))
    grad_x = dh.reshape(x.shape)

    chip_sum[0], chip_sum_bf = _add_my_half(g_layer[0], from_sibling)
    scattered[0] = _exchange_call(SCATTER, chip_sum_bf)
    mine = jnp.stack([_sum_scattered(chip_sum[l], scattered[l]) for l in range(2)])
    other = _swap_reduced(mine)
    south = lax.axis_index("c") == 0
    g_w_in, g_w_out, g_w_pg, g_w_pp = _unflatten_layers(
        [jnp.concatenate([jnp.where(south, mine[l], other[l]), jnp.where(south, other[l], mine[l])]) for l in range(2)])

    small = jnp.concatenate([
        jnp.concatenate(d_mix).reshape(-1, 128), jnp.concatenate(d_a).reshape(-1, 128),
        jnp.concatenate(d_b).reshape(-1, 128), jnp.concatenate(d_lb).reshape(-1, 128),
        jnp.concatenate(d_gate).reshape(-1, 128), jnp.concatenate(d_post).reshape(-1, 128),
        d_final.reshape(-1, 128), jnp.broadcast_to(loss_part, (8, 128))], axis=0)
    small = _allreduce_small(small)
    loss = small[80, 0]
    g_norm_mix = small[0:16].reshape(2, D_MODEL)
    g_a = small[16:24].reshape(2, GROUP)
    g_b = small[24:32].reshape(2, GROUP)
    (g_lb,) = lbs_vjp(small[32:40].reshape(2, GROUP))
    g_gate = small[40:56].reshape(2, D_MODEL)
    g_post = small[56:72].reshape(2, D_MODEL)
    g_final = small[72:80].reshape(D_MODEL)

    def adam_matrix(w, g, m, v):
        d, nm, nv = _adamw(w.reshape(-1, D_MODEL), g.reshape(-1, D_MODEL), m.reshape(-1, D_MODEL), v.reshape(-1, D_MODEL))
        return d.reshape(w.shape), nm.reshape(w.shape), nv.reshape(w.shape)

    d_w_in, nm_w_in, nv_w_in = adam_matrix(w_in, g_w_in, m_w_in, v_w_in)
    d_w_out, nm_w_out, nv_w_out = adam_matrix(w_out, g_w_out, m_w_out, v_w_out)
    d_w_pg, nm_w_pg, nv_w_pg = adam_matrix(w_ple_gate, g_w_pg, m_w_ple_gate, v_w_ple_gate)
    d_w_pp, nm_w_pp, nv_w_pp = adam_matrix(w_ple_proj, g_w_pp, m_w_ple_proj, v_w_ple_proj)

    small_w = [norm_mix, a_out_norm, b_out_norm, lb_logits, ple_gate_norm, ple_post_norm, final_norm]
    small_g = [g_norm_mix, g_a, g_b, g_lb, g_gate, g_post, g_final]
    small_m = [m_norm_mix, m_a_out_norm, m_b_out_norm, m_lb_logits, m_ple_gate_norm, m_ple_post_norm, m_final_norm]
    small_v = [v_norm_mix, v_a_out_norm, v_b_out_norm, v_lb_logits, v_ple_gate_norm, v_ple_post_norm, v_final_norm]
    pack = lambda arrs: jnp.concatenate([a.reshape(-1, 128) for a in arrs], axis=0)
    ds, nms, nvs = _adamw(pack(small_w), pack(small_g), pack(small_m), pack(small_v))

    def unpack(packed):
        out, r = [], 0
        for a in small_w:
            n = a.size // 128
            out.append(packed[r:r + n].reshape(a.shape))
            r += n
        return out

    d_s, nm_s, nv_s = unpack(ds), unpack(nms), unpack(nvs)

    def ordered(s, big):
        return [s[0], big[0], s[1], s[2], big[1], s[3], s[4], big[2], big[3], s[5], s[6]]

    grads = ordered(small_g, [g_w_in, g_w_out, g_w_pg, g_w_pp])
    deltas = ordered(d_s, [d_w_in, d_w_out, d_w_pg, d_w_pp])
    new_m = ordered(nm_s, [nm_w_in, nm_w_out, nm_w_pg, nm_w_pp])
    new_v = ordered(nv_s, [nv_w_in, nv_w_out, nv_w_pg, nv_w_pp])
    return (loss, grad_x, *grads, *deltas, *new_m, *new_v)
```

```python
import functools
import math

import numpy as np
import jax
import jax.numpy as jnp
from jax import lax
from jax.experimental import pallas as pl
from jax.experimental.pallas import tpu as pltpu

F32 = jnp.float32
BF16 = jnp.bfloat16
MESH = pl.DeviceIdType.MESH

D_MODEL = 1024
D_PLE = 256
D_IN = 4096
A_HEADS, A_D = 4, 128
B_HEADS, B_D = 8, 64
GROUP = 512
EPS = 1e-6
N_SHARD = 4
N_DEV = 8

HG_CHUNK = 128
HG_LEVELS = 7
SB_TQ = 1024
SB_TK = 128
SB_GROUP_FWD, SB_GROUP_BWD = 4, 8
LOG2E = 1.4426950408889634
LN2 = 0.6931471805599453

ADAM_LR, ADAM_B1, ADAM_B2, ADAM_EPS, ADAM_WD, ADAM_STEP = 0.001, 0.9, 0.999, 1e-08, 0.01, 10

VMEM_LIMIT = 48 * 1024 * 1024
VMEM_LIMIT_BIG = 58 * 1024 * 1024

ROWS_W_IN = 2 * D_MODEL
ROWS_W_OUT = 2 * (D_MODEL // N_SHARD)
ROWS_W_PG = 2 * (D_MODEL // N_SHARD)
ROWS_W_PP = 2 * (D_PLE * (D_MODEL // N_SHARD) // D_MODEL)
ROWS_FLAT = ROWS_W_IN + ROWS_W_OUT + ROWS_W_PG + ROWS_W_PP
HALF_FLAT = ROWS_FLAT // 2
N_CHUNK = 10
CHUNK_ROWS = HALF_FLAT // N_CHUNK

ROWS_LAYER = ROWS_FLAT // 2
HALF_LAYER = ROWS_LAYER // 2
RS_CHUNKS = HALF_LAYER // CHUNK_ROWS

SMALL_ROWS = 88


def _sds(shape, dtype=F32):
    return jax.ShapeDtypeStruct(shape, dtype)


def _params(sem=None, vmem_limit=VMEM_LIMIT):
    kw = dict(vmem_limit_bytes=vmem_limit)
    if sem is not None:
        kw["dimension_semantics"] = sem
    return pltpu.CompilerParams(**kw)


def _dot(a, b, precision=None):
    return lax.dot_general(a, b, (((1,), (0,)), ((), ())), preferred_element_type=F32, precision=precision)


def _dot_nt(a, b, precision=None):
    return lax.dot_general(a, b, (((1,), (1,)), ((), ())), preferred_element_type=F32, precision=precision)


def _dot_tn(a, b, precision=None):
    return lax.dot_general(a, b, (((0,), (0,)), ((), ())), preferred_element_type=F32, precision=precision)


def _bf(x):
    return x.astype(BF16)


def _split(x):
    hi = x.astype(BF16)
    lo = (x - hi.astype(F32)).astype(BF16)
    return hi, lo


def _rms(x):
    r = lax.rsqrt(jnp.mean(x * x, axis=-1, keepdims=True) + EPS)
    return x * r, r


def _rms_bwd(dxh, xh, r):
    return r * (dxh - xh * jnp.mean(dxh * xh, axis=-1, keepdims=True))


def _sigmoid(x):
    return 1.0 / (1.0 + jnp.exp(-x))


def _silu_grad(x, sig):
    return sig * (1.0 + x * (1.0 - sig))


def _row_tile(t, want):
    return min(t, want)


def _inproj(h, g, gathered, layer):
    t = h.shape[0]
    tm = _row_tile(t, 512)

    def body(h_ref, g_ref, w_ref, pa_ref, qkv_ref, bg_ref):
        xh, _ = _rms(h_ref[...])
        u = _bf(xh * g_ref[...])
        for j in range(8):
            acc = _dot(u, w_ref[j // 2, :, pl.ds((j % 2) * GROUP, GROUP)])
            if j < 4:
                pa_ref[:, pl.ds(j * GROUP, GROUP)] = acc
            elif j == 4:
                qkv_ref[:, pl.ds(0, GROUP)] = _bf(acc * (B_D ** -0.5 * LOG2E))
            elif j < 7:
                qkv_ref[:, pl.ds((j - 4) * GROUP, GROUP)] = _bf(acc)
            else:
                bg_ref[...] = acc

    return pl.pallas_call(
        body, name="inproj", grid=(t // tm,),
        in_specs=[pl.BlockSpec((tm, D_MODEL), lambda i: (i, 0)),
                  pl.BlockSpec((1, D_MODEL), lambda i: (0, 0)),
                  pl.BlockSpec((N_SHARD, D_MODEL, D_MODEL), lambda i: (0, layer, 0))],
        out_specs=[pl.BlockSpec((tm, 4 * GROUP), lambda i: (i, 0)), pl.BlockSpec((tm, 3 * GROUP), lambda i: (i, 0)),
                   pl.BlockSpec((tm, GROUP), lambda i: (i, 0))],
        out_shape=[_sds((t, 4 * GROUP)), _sds((t, 3 * GROUP), BF16), _sds((t, GROUP))],
        compiler_params=_params(("arbitrary",)),
    )(h, g, gathered)


def _rows_spec(first_row):
    q = D_MODEL // N_SHARD
    return pl.BlockSpec((N_SHARD, q, D_MODEL), lambda i: (0, first_row // q, 0))


def _outproj(h, ya, yb, gathered, layer):
    t = h.shape[0]
    tm = _row_tile(t, 512)

    def body(h_ref, ya_ref, yb_ref, w_ref, o_ref):
        o_ref[...] = (h_ref[...] + _dot(_bf(ya_ref[...]), w_ref[0:2].reshape(GROUP, D_MODEL))
                      + _dot(_bf(yb_ref[...]), w_ref[2:4].reshape(GROUP, D_MODEL)))

    return pl.pallas_call(
        body, name="outproj", grid=(t // tm,),
        in_specs=[pl.BlockSpec((tm, D_MODEL), lambda i: (i, 0)),
                  pl.BlockSpec((tm, GROUP), lambda i: (i, 0)),
                  pl.BlockSpec((tm, GROUP), lambda i: (i, 0)),
                  _rows_spec(ROWS_W_IN + layer * (D_MODEL // N_SHARD))],
        out_specs=pl.BlockSpec((tm, D_MODEL), lambda i: (i, 0)),
        out_shape=_sds((t, D_MODEL)), compiler_params=_params(("arbitrary",)),
    )(h, ya, yb, gathered)


def _ple_fwd(h, p, w_pp, gathered, layer, g_post, g_gate):
    t = h.shape[0]
    tm = _row_tile(t, 256)

    def body(h_ref, p_ref, wpp_ref, wpg_ref, gp_ref, gg_ref, o_ref):
        x = h_ref[...]
        ph, _ = _rms(_dot(_bf(p_ref[...]), wpp_ref[...]))
        xh, _ = _rms(x)
        gate = _sigmoid(_dot(_bf(xh * gg_ref[...]), wpg_ref[...].reshape(D_MODEL, D_MODEL)))
        o_ref[...] = x + gate * (ph * gp_ref[...])

    return pl.pallas_call(
        body, name="ple_fwd", grid=(t // tm,),
        in_specs=[pl.BlockSpec((tm, D_MODEL), lambda i: (i, 0)),
                  pl.BlockSpec((tm, D_PLE), lambda i: (i, 0)),
                  pl.BlockSpec((D_PLE, D_MODEL), lambda i: (0, 0)),
                  _rows_spec(ROWS_W_IN + ROWS_W_OUT + layer * (D_MODEL // N_SHARD)),
                  pl.BlockSpec((1, D_MODEL), lambda i: (0, 0)),
                  pl.BlockSpec((1, D_MODEL), lambda i: (0, 0))],
        out_specs=pl.BlockSpec((tm, D_MODEL), lambda i: (i, 0)),
        out_shape=_sds((t, D_MODEL)), compiler_params=_params(("arbitrary",)),
    )(h, p, w_pp, gathered, g_post, g_gate)


def _ple_bwd(dh2, h, p, w_pp, gathered, layer, g_post, g_gate, exchange=None):
    t = h.shape[0]
    tm = _row_tile(t, 256)

    def body(d_ref, h_ref, p_ref, wpp_ref, wpg_ref, gp_ref, gg_ref, dh_ref, dwpg_ref, dwpp_ref, dgg_ref, dgp_ref):
        @pl.when(pl.program_id(0) == 0)
        def _():
            dwpg_ref[...] = jnp.zeros_like(dwpg_ref)
            dwpp_ref[...] = jnp.zeros_like(dwpp_ref)
            dgg_ref[...] = jnp.zeros_like(dgg_ref)
            dgp_ref[...] = jnp.zeros_like(dgp_ref)

        d = d_ref[...]
        x = h_ref[...]
        gp = gp_ref[...]
        gg = gg_ref[...]
        pb = _bf(p_ref[...])
        ph, rp = _rms(_dot(pb, wpp_ref[...]))
        pe = ph * gp
        xh, rx = _rms(x)
        un = _bf(xh * gg)
        wpg = wpg_ref[...].reshape(D_MODEL, D_MODEL)
        gate = _sigmoid(_dot(un, wpg))
        dgpre = _bf(d * pe * gate * (1.0 - gate))
        dun = _dot_nt(dgpre, wpg)
        dh_ref[...] = d + _rms_bwd(dun * gg, xh, rx)
        dgg_ref[...] += jnp.sum(dun * xh, axis=0, keepdims=True)
        dwpg_ref[...] += _dot_tn(un, dgpre)
        dpe = d * gate
        dgp_ref[...] += jnp.sum(dpe * ph, axis=0, keepdims=True)
        dwpp_ref[...] += _dot_tn(pb, _bf(_rms_bwd(dpe * gp, ph, rp)))

    in_specs = [pl.BlockSpec((tm, D_MODEL), lambda i: (i, 0)),
                pl.BlockSpec((tm, D_MODEL), lambda i: (i, 0)),
                pl.BlockSpec((tm, D_PLE), lambda i: (i, 0)),
                pl.BlockSpec((D_PLE, D_MODEL), lambda i: (0, 0)),
                _rows_spec(ROWS_W_IN + ROWS_W_OUT + layer * (D_MODEL // N_SHARD)),
                pl.BlockSpec((1, D_MODEL), lambda i: (0, 0)),
                pl.BlockSpec((1, D_MODEL), lambda i: (0, 0))]
    out_specs = [pl.BlockSpec((tm, D_MODEL), lambda i: (i, 0)),
                 pl.BlockSpec((D_MODEL, D_MODEL), lambda i: (0, 0)),
                 pl.BlockSpec((D_PLE, D_MODEL), lambda i: (0, 0)),
                 pl.BlockSpec((1, D_MODEL), lambda i: (0, 0)),
                 pl.BlockSpec((1, D_MODEL), lambda i: (0, 0))]
    out_shape = [_sds((t, D_MODEL)), _sds((D_MODEL, D_MODEL)), _sds((D_PLE, D_MODEL)),
                 _sds((1, D_MODEL)), _sds((1, D_MODEL))]
    operands = [dh2, h, p, w_pp, gathered, g_post, g_gate]
    scratch = []
    if exchange is not None:
        body = _with_exchange(body, 7, 5, exchange, lambda: pl.program_id(0), t // tm)
        xi, xo, xs, scratch, xop = _exchange_args(exchange)
        in_specs, out_specs, out_shape, operands = in_specs + xi, out_specs + xo, out_shape + xs, operands + xop
    return pl.pallas_call(
        body, name="ple_bwd" if exchange is None else "ple_bwd_exchange", grid=(t // tm,),
        in_specs=in_specs, out_specs=out_specs, out_shape=out_shape, scratch_shapes=scratch,
        compiler_params=_params(("arbitrary",)),
    )(*operands)


def _outproj_bwd(dh, ya, yb, gathered, layer):
    t = dh.shape[0]
    tm = _row_tile(t, 512)

    def body(d_ref, ya_ref, yb_ref, w_ref, dya_ref, dyb_ref, dw_ref):
        @pl.when(pl.program_id(0) == 0)
        def _():
            dw_ref[...] = jnp.zeros_like(dw_ref)

        d = _bf(d_ref[...])
        dya_ref[...] = _dot_nt(d, w_ref[0:2].reshape(GROUP, D_MODEL))
        dyb_ref[...] = _dot_nt(d, w_ref[2:4].reshape(GROUP, D_MODEL))
        dw_ref[pl.ds(0, GROUP), :] += _dot_tn(_bf(ya_ref[...]), d)
        dw_ref[pl.ds(GROUP, GROUP), :] += _dot_tn(_bf(yb_ref[...]), d)

    return pl.pallas_call(
        body, name="outproj_bwd", grid=(t // tm,),
        in_specs=[pl.BlockSpec((tm, D_MODEL), lambda i: (i, 0)),
                  pl.BlockSpec((tm, GROUP), lambda i: (i, 0)),
                  pl.BlockSpec((tm, GROUP), lambda i: (i, 0)),
                  _rows_spec(ROWS_W_IN + layer * (D_MODEL // N_SHARD))],
        out_specs=[pl.BlockSpec((tm, GROUP), lambda i: (i, 0)),
                   pl.BlockSpec((tm, GROUP), lambda i: (i, 0)),
                   pl.BlockSpec((D_MODEL, D_MODEL), lambda i: (0, 0))],
        out_shape=[_sds((t, GROUP)), _sds((t, GROUP)), _sds((D_MODEL, D_MODEL))],
        compiler_params=_params(("arbitrary",)),
    )(dh, ya, yb, gathered)


def _inproj_bwd_dx(dres, h, g, gathered, layer, da, db, exchange=None):
    t = h.shape[0]
    tm = _row_tile(t, 256)

    def body(dres_ref, h_ref, g_ref, w_ref, da_ref, db_ref, dh_ref, dg_ref):
        @pl.when(pl.program_id(0) == 0)
        def _():
            dg_ref[...] = jnp.zeros_like(dg_ref)

        du = jnp.zeros((tm, D_MODEL), F32)
        for i in range(8):
            part = da_ref[i] if i < 4 else db_ref[i - 4]
            du = du + _dot_nt(part, w_ref[i // 2, :, pl.ds((i % 2) * GROUP, GROUP)])
        xh, r = _rms(h_ref[...])
        dg_ref[...] += jnp.sum(du * xh, axis=0, keepdims=True)
        dh_ref[...] = dres_ref[...] + _rms_bwd(du * g_ref[...], xh, r)

    in_specs = [pl.BlockSpec((tm, D_MODEL), lambda i: (i, 0)),
                pl.BlockSpec((tm, D_MODEL), lambda i: (i, 0)),
                pl.BlockSpec((1, D_MODEL), lambda i: (0, 0)),
                pl.BlockSpec((N_SHARD, D_MODEL, D_MODEL), lambda i: (0, layer, 0)),
                pl.BlockSpec((4, tm, GROUP), lambda i: (0, i, 0)),
                pl.BlockSpec((4, tm, GROUP), lambda i: (0, i, 0))]
    out_specs = [pl.BlockSpec((tm, D_MODEL), lambda i: (i, 0)), pl.BlockSpec((1, D_MODEL), lambda i: (0, 0))]
    out_shape = [_sds((t, D_MODEL)), _sds((1, D_MODEL))]
    operands = [dres, h, g, gathered, da, db]
    scratch = []
    if exchange is not None:
        body = _with_exchange(body, 6, 2, exchange, lambda: pl.program_id(0), t // tm)
        xi, xo, xs, scratch, xop = _exchange_args(exchange)
        in_specs, out_specs, out_shape, operands = in_specs + xi, out_specs + xo, out_shape + xs, operands + xop
    return pl.pallas_call(
        body, name="inproj_bwd_dx" if exchange is None else "inproj_bwd_dx_exchange", grid=(t // tm,),
        in_specs=in_specs, out_specs=out_specs, out_shape=out_shape, scratch_shapes=scratch,
        compiler_params=_params(("arbitrary",)),
    )(*operands)


def _inproj_bwd_dw(h, g, da, db):
    t = h.shape[0]
    tm = _row_tile(t, 512)

    def body(h_ref, g_ref, da_ref, db_ref, dw_ref):
        @pl.when(pl.program_id(0) == 0)
        def _():
            dw_ref[...] = jnp.zeros_like(dw_ref)

        xh, _ = _rms(h_ref[...])
        u = _bf(xh * g_ref[...])
        for i in range(8):
            dw_ref[i // 2, :, pl.ds((i % 2) * GROUP, GROUP)] += _dot_tn(u, da_ref[i] if i < 4 else db_ref[i - 4])

    return pl.pallas_call(
        body, name="inproj_bwd_dw", grid=(t // tm,),
        in_specs=[pl.BlockSpec((tm, D_MODEL), lambda i: (i, 0)),
                  pl.BlockSpec((1, D_MODEL), lambda i: (0, 0)),
                  pl.BlockSpec((4, tm, GROUP), lambda i: (0, i, 0)),
                  pl.BlockSpec((4, tm, GROUP), lambda i: (0, i, 0))],
        out_specs=pl.BlockSpec((N_SHARD, D_MODEL, D_MODEL), lambda i: (0, 0, 0)),
        out_shape=_sds((N_SHARD, D_MODEL, D_MODEL)), compiler_params=_params(("arbitrary",), VMEM_LIMIT_BIG),
    )(h, g, da, db)


def _final(h, g, target):
    t = h.shape[0]
    tm = _row_tile(t, 512)

    def body(h_ref, g_ref, t_ref, dh_ref, dg_ref, loss_ref):
        @pl.when(pl.program_id(0) == 0)
        def _():
            dg_ref[...] = jnp.zeros_like(dg_ref)
            loss_ref[...] = jnp.zeros_like(loss_ref)

        xh, r = _rms(h_ref[...])
        gg = g_ref[...]
        err = xh * gg - t_ref[...]
        part = 0.5 * jnp.sum(jnp.mean(err * err, axis=-1, keepdims=True), axis=0, keepdims=True)
        loss_ref[...] += jnp.broadcast_to(part, loss_ref.shape)
        dy = err * (1.0 / D_MODEL)
        dg_ref[...] += jnp.sum(dy * xh, axis=0, keepdims=True)
        dh_ref[...] = _rms_bwd(dy * gg, xh, r)

    return pl.pallas_call(
        body, name="final", grid=(t // tm,),
        in_specs=[pl.BlockSpec((tm, D_MODEL), lambda i: (i, 0)),
                  pl.BlockSpec((1, D_MODEL), lambda i: (0, 0)),
                  pl.BlockSpec((tm, D_MODEL), lambda i: (i, 0))],
        out_specs=[pl.BlockSpec((tm, D_MODEL), lambda i: (i, 0)),
                   pl.BlockSpec((1, D_MODEL), lambda i: (0, 0)),
                   pl.BlockSpec((1, 128), lambda i: (0, 0))],
        out_shape=[_sds((t, D_MODEL)), _sds((1, D_MODEL)), _sds((1, 128))],
        compiler_params=_params(("arbitrary",)),
    )(h, g, target)


def _hgrn_consts():
    c, nl = HG_CHUNK, HG_LEVELS
    t = np.arange(c)
    tril = np.tril(np.ones((c, c), np.float32))
    masks = np.zeros((nl + 1, c, c), np.float32)
    masks[0] = np.eye(c, dtype=np.float32)
    dmat = np.zeros(((nl + 2) * c, c), np.float32)
    dmat[0:c] = tril
    for l in range(nl):
        m = c >> (l + 1)
        blk = t // (2 * m)
        r = blk * 2 * m + m - 1
        upper = (t % (2 * m)) >= m
        masks[l + 1] = ((blk[:, None] == blk[None, :]) & upper[:, None] & (~upper)[None, :]).astype(np.float32)
        dmat[(l + 1) * c:(l + 2) * c] = tril[t] - tril[r]
    dmat[(nl + 1) * c:] = np.triu(np.ones((c, c), np.float32), k=1)
    return jnp.asarray(masks), jnp.asarray(dmat, BF16)


HG_HEADS = 4


def _hgrn_pre(aq, af, lb):
    sq = _sigmoid(aq)
    sneg = _sigmoid(-af)
    kk = (1.0 - lb) * sneg
    return sq, aq * sq, sneg, kk, jnp.log1p(-kk)


def _hgrn_x(logf, dmat_ref):
    dm = dmat_ref[pl.ds(0, (HG_LEVELS + 1) * HG_CHUNK), :]
    lhi, llo = _split(logf)
    return _dot(dm, lhi) + _dot(dm, llo)


def _hgrn_level(x_all, l, q, kk):
    c = HG_CHUNK
    x = x_all[(l + 1) * c:(l + 2) * c]
    qf = jnp.exp(jnp.minimum(x, 0.0))
    kf = jnp.exp(-jnp.maximum(x, 0.0))
    return qf, kf, _bf(q * qf), _bf(kk * kf)


def _hgrn_scores(xs, qs, kks, mask_ref):
    ps = [mask_ref[0] * _dot_nt(_bf(q), _bf(kk)) for q, kk in zip(qs, kks)]
    for l in range(HG_LEVELS):
        for i, (x_all, q, kk) in enumerate(zip(xs, qs, kks)):
            _, _, ql, kl = _hgrn_level(x_all, l, q, kk)
            ps[i] = ps[i] + mask_ref[l + 1] * _dot_nt(ql, kl)
    return ps


def _hgrn_specs(n_chunks, rev):
    c, w = HG_CHUNK, HG_HEADS * A_D
    cidx = (lambda n: n_chunks - 1 - n) if rev else (lambda n: n)
    col = lambda g: pl.BlockSpec((c, w), lambda h, n: (cidx(n), g * (A_HEADS // HG_HEADS) + h))
    vec = pl.BlockSpec((1, w), lambda h, n: (0, h))
    mask = pl.BlockSpec((HG_LEVELS + 1, c, c), lambda h, n: (0, 0, 0))
    dmat = pl.BlockSpec(((HG_LEVELS + 2) * c, c), lambda h, n: (0, 0))
    state = pl.BlockSpec((HG_HEADS, None, A_D, A_D), lambda h, n: (h, cidx(n), 0, 0))
    return cidx, col, vec, mask, dmat, state


def _lanes(i):
    return pl.ds(i * A_D, A_D)


def _hgrn_fwd(proj, lb, gain):
    t = proj.shape[0]
    c = HG_CHUNK
    nch = t // c
    masks, dmat = _hgrn_consts()
    cidx, col, vec, mask_spec, dmat_spec, state_spec = _hgrn_specs(nch, False)
    heads = range(HG_HEADS)

    def body(aq_ref, af_ref, ai_ref, ag_ref, lb_ref, gain_ref, mask_ref, dmat_ref, y_ref, st_ref, s_scr):
        @pl.when(pl.program_id(1) == 0)
        def _():
            s_scr[...] = jnp.zeros_like(s_scr)

        pre = [_hgrn_pre(aq_ref[:, _lanes(i)], af_ref[:, _lanes(i)], lb_ref[:, _lanes(i)]) for i in heads]
        qs, kks = [p[1] for p in pre], [p[3] for p in pre]
        xs = [_hgrn_x(p[4], dmat_ref) for p in pre]
        bs = [x[0:c] for x in xs]
        b_lasts = [jnp.sum(p[4], axis=0, keepdims=True) for p in pre]
        ps = _hgrn_scores(xs, qs, kks, mask_ref)
        ss = [s_scr[i] for i in heads]
        vbs = [_bf(ai_ref[:, _lanes(i)]) for i in heads]
        os_ = [_dot(_bf(ps[i]), vbs[i]) + _dot_nt(_bf(qs[i] * jnp.exp(bs[i])), _bf(ss[i])) for i in heads]
        for i in heads:
            st_ref[i] = ss[i]
            s_scr[i] = ss[i] * jnp.exp(b_lasts[i]) + _dot_tn(vbs[i], _bf(kks[i] * jnp.exp(b_lasts[i] - bs[i])))
            oh, _ = _rms(os_[i])
            ag = ag_ref[:, _lanes(i)]
            y_ref[:, _lanes(i)] = oh * gain_ref[:, _lanes(i)] * (ag * _sigmoid(ag))

    return pl.pallas_call(
        body, name="hgrn_fwd", grid=(A_HEADS // HG_HEADS, nch),
        in_specs=[col(0), col(1), col(2), col(3), vec, vec, mask_spec, dmat_spec],
        out_specs=[pl.BlockSpec((c, HG_HEADS * A_D), lambda h, n: (n, h)), state_spec],
        out_shape=[_sds((t, GROUP)), _sds((A_HEADS, nch, A_D, A_D))],
        scratch_shapes=[pltpu.VMEM((HG_HEADS, A_D, A_D), F32)],
        compiler_params=_params(("arbitrary", "arbitrary")),
    )(proj, proj, proj, proj, lb, gain, masks, dmat)


def _hgrn_bwd(proj, lb, gain, states, dya):
    t = proj.shape[0]
    c, nl = HG_CHUNK, HG_LEVELS
    nch = t // c
    masks, dmat = _hgrn_consts()
    cidx, col, vec, mask_spec, dmat_spec, state_spec = _hgrn_specs(nch, True)
    heads = range(HG_HEADS)

    def body(aq_ref, af_ref, ai_ref, ag_ref, lb_ref, gain_ref, mask_ref, dmat_ref, st_ref, dy_ref,
             da_ref, dlb_ref, dgain_ref, ds_scr, z_scr):
        @pl.when(pl.program_id(1) == 0)
        def _():
            ds_scr[...] = jnp.zeros_like(ds_scr)
            dlb_ref[...] = jnp.zeros_like(dlb_ref)
            dgain_ref[...] = jnp.zeros_like(dgain_ref)

        aqs = [aq_ref[:, _lanes(i)] for i in heads]
        lbs = [lb_ref[:, _lanes(i)] for i in heads]
        pre = [_hgrn_pre(aqs[i], af_ref[:, _lanes(i)], lbs[i]) for i in heads]
        sqs, qs, snegs, kks = ([p[j] for p in pre] for j in range(4))
        xs = [_hgrn_x(p[4], dmat_ref) for p in pre]
        bs = [x[0:c] for x in xs]
        b_lasts = [jnp.sum(p[4], axis=0, keepdims=True) for p in pre]
        ebs = [jnp.exp(b) for b in bs]
        ebls = [jnp.exp(bl - b) for bl, b in zip(b_lasts, bs)]
        ebl_rows = [jnp.exp(bl) for bl in b_lasts]
        qes = [_bf(q * eb) for q, eb in zip(qs, ebs)]
        kes = [_bf(kk * ebl) for kk, ebl in zip(kks, ebls)]
        vbs = [_bf(ai_ref[:, _lanes(i)]) for i in heads]
        ss = [st_ref[i] for i in heads]
        sbs = [_bf(s) for s in ss]
        dss = [ds_scr[i] for i in heads]
        dsbs = [_bf(ds) for ds in dss]

        pbs = [_bf(p) for p in _hgrn_scores(xs, qs, kks, mask_ref)]
        os_ = [_dot(pbs[i], vbs[i]) + _dot_nt(qes[i], sbs[i]) for i in heads]

        dos = []
        for i in heads:
            ag, gain, dy = ag_ref[:, _lanes(i)], gain_ref[:, _lanes(i)], dy_ref[:, _lanes(i)]
            oh, r = _rms(os_[i])
            sg_sig = _sigmoid(ag)
            sg = ag * sg_sig
            da_ref[3, :, _lanes(i)] = _bf(dy * oh * gain * _silu_grad(ag, sg_sig))
            dgain_ref[:, _lanes(i)] += jnp.sum(dy * oh * sg, axis=0, keepdims=True)
            dos.append(_bf(_rms_bwd(dy * gain * sg, oh, r)))

        dps = [_dot_nt(dos[i], vbs[i]) for i in heads]
        for i in heads:
            da_ref[2, :, _lanes(i)] = _bf(_dot_tn(pbs[i], dos[i]) + _dot_nt(kes[i], dsbs[i]))
        dq_ss = [ebs[i] * _dot(dos[i], sbs[i]) for i in heads]
        dk_ss = [ebls[i] * _dot(vbs[i], dsbs[i]) for i in heads]
        dqs, dks = [], []
        for i in heads:
            dpd = jnp.sum(mask_ref[0] * dps[i], axis=1, keepdims=True)
            z_scr[i, pl.ds(0, c), :] = qs[i] * dq_ss[i]
            z_scr[i, pl.ds((nl + 1) * c, c), :] = kks[i] * dk_ss[i]
            dqs.append(dq_ss[i] + dpd * kks[i])
            dks.append(dk_ss[i] + dpd * qs[i])
        for l in range(nl):
            for i in heads:
                qf, kf, ql, kl = _hgrn_level(xs[i], l, qs[i], kks[i])
                dpl = _bf(mask_ref[l + 1] * dps[i])
                dq_l = qf * _dot(dpl, kl)
                dk_l = kf * _dot_tn(dpl, ql)
                z_scr[i, pl.ds((l + 1) * c, c), :] = qs[i] * dq_l - kks[i] * dk_l
                dqs[i] = dqs[i] + dq_l
                dks[i] = dks[i] + dk_l

        zsplits = [_split(z_scr[i]) for i in heads]
        dlogfs = [_dot_tn(dmat_ref[...], zhi) + _dot_tn(dmat_ref[...], zlo) for zhi, zlo in zsplits]
        ds_new = [_dot_tn(dos[i], qes[i]) for i in heads]
        for i in heads:
            dlogf = dlogfs[i] + ebl_rows[i] * jnp.sum(dss[i] * ss[i], axis=0, keepdims=True)
            dkk = dks[i] - dlogf / (1.0 - kks[i])
            da_ref[1, :, _lanes(i)] = _bf(dkk * (1.0 - lbs[i]) * (-(snegs[i] * (1.0 - snegs[i]))))
            dlb_ref[:, _lanes(i)] += jnp.sum(dkk * (-snegs[i]), axis=0, keepdims=True)
            da_ref[0, :, _lanes(i)] = _bf(dqs[i] * _silu_grad(aqs[i], sqs[i]))
            ds_scr[i] = dss[i] * ebl_rows[i] + ds_new[i]

    w = HG_HEADS * A_D
    return pl.pallas_call(
        body, name="hgrn_bwd", grid=(A_HEADS // HG_HEADS, nch),
        in_specs=[col(0), col(1), col(2), col(3), vec, vec, mask_spec, dmat_spec, state_spec,
                  pl.BlockSpec((c, w), lambda h, n: (cidx(n), h))],
        out_specs=[pl.BlockSpec((4, c, w), lambda h, n: (0, cidx(n), h)), vec, vec],
        out_shape=[_sds((4, t, GROUP), BF16)] + [_sds((1, GROUP))] * 2,
        scratch_shapes=[pltpu.VMEM((HG_HEADS, A_D, A_D), F32), pltpu.VMEM((HG_HEADS, (nl + 2) * c, A_D), F32)],
        compiler_params=_params(("arbitrary", "arbitrary")),
    )(proj, proj, proj, proj, lb, gain, masks, dmat, states, dya)


def _sb_consts():
    j = np.arange(SB_TK)
    strict = (j[:, None] > j[None, :]).astype(np.float32)
    incl = (j[:, None] >= j[None, :]).astype(np.float32)
    return jnp.asarray(strict, BF16), jnp.asarray(incl, BF16)


def _lane0(x):
    return jnp.broadcast_to(x[:, 0:1], x.shape)


def _causal(x, masked):
    if not masked:
        return x
    n = (SB_TK, SB_TK)
    top = jnp.where(lax.broadcasted_iota(jnp.int32, n, 1) < lax.broadcasted_iota(jnp.int32, n, 0), x[:SB_TK], 0.0)
    return top if x.shape[0] == SB_TK else jnp.concatenate([top, x[SB_TK:]], axis=0)


def _sb_softplus(z, masked):
    logsig = jnp.minimum(z, 0.0) - jnp.log2(1.0 + jnp.exp2(-jnp.abs(z)))
    return _causal(z - logsig, masked), logsig


def _sb_cumsum(sp, cmat):
    cs = _dot(_bf(sp), cmat)
    return cs, _lane0(cs + sp)


def _sb_sweep(qi, group_fn, state, group):
    nd = SB_TQ // SB_TK

    def run(tiles, st):
        for i in range(0, len(tiles), group):
            st = group_fn(tiles[i:i + group], st)
        return st

    state = run([(pl.multiple_of((qi * nd + d) * SB_TK, SB_TK), d * SB_TK, True) for d in reversed(range(nd))], state)

    def step(j, st):
        return run([(pl.multiple_of(((qi - j) * nd - 1 - g) * SB_TK, SB_TK), 0, False) for g in range(nd)], st)

    return lax.fori_loop(0, qi, step, state)


def _set_rows(r0, full, new):
    return new if r0 == 0 else jnp.concatenate([full[:r0], new], axis=0)


def _sb_specs(t, tq):
    col = lambda g: pl.BlockSpec((tq, 2 * B_D), lambda p, i, h: (i, g * (GROUP // (2 * B_D)) + p))
    full = lambda g: pl.BlockSpec((t, 2 * B_D), lambda p, i, h: (0, g * (GROUP // (2 * B_D)) + p))
    vec = pl.BlockSpec((1, 2 * B_D), lambda p, i, h: (0, p))
    mat = pl.BlockSpec((SB_TK, SB_TK), lambda p, i, h: (0, 0))
    return col, full, vec, mat


def _head_lanes(h):
    return (lax.broadcasted_iota(jnp.int32, (1, 2 * B_D), 1) >= B_D) == (h == 1)


def _put(ref, h, val):
    @pl.when(h == 0)
    def _():
        ref[...] = val

    @pl.when(h == 1)
    def _():
        ref[...] += val


def _sb_fwd(qkv, bgate, gain, gather=None):
    t = qkv.shape[0]
    tq = SB_TQ
    strict, _ = _sb_consts()
    n_steps = (B_HEADS // 2, t // tq, 2)

    def body(q_ref, k_ref, v_ref, bg_ref, gain_ref, m_ref, *rest):
        if gather is None:
            o_ref, y_ref = rest
        else:
            flat_ref, _, o_ref, y_ref, gathered_ref, send_sems, recv_sems = rest
            start, forward, finish = _gather_plan(flat_ref, gathered_ref, send_sems, recv_sems, *gather[2:])
            step = (pl.program_id(0) * n_steps[1] + pl.program_id(1)) * n_steps[2] + pl.program_id(2)
            pl.when(step == 0)(start)
            pl.when(step == 2 * n_steps[1] * n_steps[2])(forward)
            pl.when(step == n_steps[0] * n_steps[1] * n_steps[2] - 1)(finish)
        h = pl.program_id(2)
        lanes = _head_lanes(h)
        qb = jnp.where(lanes, q_ref[...], jnp.zeros_like(q_ref))
        cmat = m_ref[...]

        def group(tiles, state):
            carry, acc = state
            kv = [(k_ref[pl.ds(off, SB_TK), :], v_ref[pl.ds(off, SB_TK), :]) for off, _, _ in tiles]
            zs = [_dot_nt(qb[r0:], kb) for (_, r0, _), (kb, _) in zip(tiles, kv)]
            sps = [_sb_softplus(z, masked) for z, (_, _, masked) in zip(zs, tiles)]
            css = [_sb_cumsum(sp, cmat) for sp, _ in sps]
            ws = []
            for (_, logsig), (cs, tot), (_, r0, masked) in zip(sps, css, tiles):
                ws.append(_split(_causal(jnp.exp2(logsig - cs - carry[r0:]), masked)))
                carry = _set_rows(r0, carry, carry[r0:] + tot)
            for (whi, wlo), (_, vb), (_, r0, _) in zip(ws, kv, tiles):
                acc = _set_rows(r0, acc, acc[r0:] + _dot(whi, vb) + _dot(wlo, vb))
            return carry, acc

        _, acc = _sb_sweep(pl.program_id(1), group, (jnp.zeros((tq, SB_TK), F32), jnp.zeros((tq, 2 * B_D), F32)),
                           SB_GROUP_FWD)
        o = jnp.where(lanes, acc, 0.0)
        oh = o * lax.rsqrt(jnp.sum(o * o, axis=-1, keepdims=True) * (1.0 / B_D) + EPS)
        bg = bg_ref[...]
        _put(o_ref, h, o)
        _put(y_ref, h, oh * gain_ref[...] * (bg * _sigmoid(bg)))

    col, full, vec, mat = _sb_specs(t, tq)
    out = pl.BlockSpec((tq, 2 * B_D), lambda p, i, h: (i, p))
    in_specs = [col(0), full(1), full(2), col(0), vec, mat]
    out_specs = [out, out]
    out_shape = [_sds((t, GROUP)), _sds((t, GROUP))]
    operands = [qkv, qkv, qkv, bgate, gain, strict]
    extra = {}
    if gather is not None:
        in_specs += [_ANY, _ANY]
        out_specs += [_ANY]
        out_shape += [_sds(gather[1].shape, gather[1].dtype)]
        operands += [gather[0], gather[1]]
        extra = dict(input_output_aliases={7: 2}, scratch_shapes=_gather_sems(gather[4]))
    return pl.pallas_call(
        body, name="sb_fwd" if gather is None else "sb_fwd_gather", grid=n_steps,
        in_specs=in_specs, out_specs=out_specs, out_shape=out_shape,
        compiler_params=_params(("arbitrary", "arbitrary", "arbitrary")), **extra,
    )(*operands)


def _sb_bwd(qkv, bgate, o, dy, gain, exchange=None):
    t = qkv.shape[0]
    tq = SB_TQ
    strict, incl = _sb_consts()

    def body(q_ref, k_ref, v_ref, bg_ref, o_ref, dy_ref, gain_ref, ms_ref, mi_ref,
             dq_ref, dk_ref, dv_ref, dbg_ref, dgain_ref):
        qi = pl.program_id(1)
        h = pl.program_id(2)
        lanes = _head_lanes(h)

        @pl.when((qi == 0) & (h == 0))
        def _():
            dk_ref[...] = jnp.zeros_like(dk_ref)
            dv_ref[...] = jnp.zeros_like(dv_ref)
            dgain_ref[...] = jnp.zeros_like(dgain_ref)

        qb = jnp.where(lanes, q_ref[...], jnp.zeros_like(q_ref))
        cmat = ms_ref[...]
        imat = mi_ref[...]
        o = jnp.where(lanes, o_ref[...], 0.0)
        dy = jnp.where(lanes, dy_ref[...], 0.0)
        bg = bg_ref[...]
        gain = gain_ref[...]
        r = lax.rsqrt(jnp.sum(o * o, axis=-1, keepdims=True) * (1.0 / B_D) + EPS)
        oh = o * r
        sig = _sigmoid(bg)
        sg = bg * sig
        _put(dbg_ref, h, dy * oh * gain * _silu_grad(bg, sig))
        dgain_ref[...] += jnp.sum(dy * oh * sg, axis=0, keepdims=True)
        doh = dy * gain * sg
        do = _bf(r * (doh - oh * (jnp.sum(doh * oh, axis=-1, keepdims=True) * (1.0 / B_D))))
        total = jnp.broadcast_to(jnp.sum(do.astype(F32) * o, axis=1, keepdims=True), (tq, SB_TK))

        def group(tiles, state):
            carry, gcarry, dq = state
            kv = [(k_ref[pl.ds(off, SB_TK), :], v_ref[pl.ds(off, SB_TK), :]) for off, _, _ in tiles]
            zs = [_dot_nt(qb[r0:], kb) for (_, r0, _), (kb, _) in zip(tiles, kv)]
            dws = [_dot_nt(do[r0:], vb) for (_, r0, _), (_, vb) in zip(tiles, kv)]
            sps = [_sb_softplus(z, masked) for z, (_, _, masked) in zip(zs, tiles)]
            css = [_sb_cumsum(sp, cmat) for sp, _ in sps]
            ws, gs = [], []
            for (_, logsig), (cs, tot), dw, (_, r0, masked) in zip(sps, css, dws, tiles):
                w = _causal(jnp.exp2(logsig - cs - carry[r0:]), masked)
                ws.append(_bf(w))
                gs.append(dw * w)
                carry = _set_rows(r0, carry, carry[r0:] + tot)
            s2s = []
            for g in gs:
                ghi, glo = _split(g)
                s2s.append(_dot(ghi, imat) + _dot(glo, imat))
            dzs = []
            for (_, logsig), g, s2, (_, r0, masked) in zip(sps, gs, s2s, tiles):
                before = total[r0:] - gcarry[r0:] - s2
                dz = g - jnp.exp2(logsig) * (g + before)
                dzs.append(_bf(_causal(dz, masked)))
                gcarry = _set_rows(r0, gcarry, gcarry[r0:] + _lane0(s2))
            for dz, wb, (kb, _), (off, r0, _) in zip(dzs, ws, kv, tiles):
                dq = _set_rows(r0, dq, dq[r0:] + _dot(dz, kb))
                dk_ref[pl.ds(off, SB_TK), :] += _dot_tn(dz, qb[r0:])
                dv_ref[pl.ds(off, SB_TK), :] += _dot_tn(wb, do[r0:])
            return carry, gcarry, dq

        zero = jnp.zeros((tq, SB_TK), F32)
        _, _, dq = _sb_sweep(qi, group, (zero, zero, jnp.zeros((tq, 2 * B_D), F32)), SB_GROUP_BWD)
        _put(dq_ref, h, jnp.where(lanes, dq * (B_D ** -0.5), 0.0))

    col, full, vec, mat = _sb_specs(t, tq)
    blk = pl.BlockSpec((tq, 2 * B_D), lambda p, i, h: (i, p))
    whole = pl.BlockSpec((t, 2 * B_D), lambda p, i, h: (0, p))
    grid = (B_HEADS // 2, t // tq, 2)
    in_specs = [col(0), full(1), full(2), col(0), blk, blk, vec, mat, mat]
    out_specs = [blk, whole, whole, blk, vec]
    out_shape = [_sds((t, GROUP))] * 4 + [_sds((1, GROUP))]
    operands = [qkv, qkv, qkv, bgate, o, dy, gain, strict, incl]
    scratch = []
    if exchange is not None:
        step_of = lambda: (pl.program_id(0) * grid[1] + pl.program_id(1)) * grid[2] + pl.program_id(2)
        body = _with_exchange(body, 9, 5, exchange, step_of, grid[0] * grid[1] * grid[2])
        xi, xo, xs, scratch, xop = _exchange_args(exchange)
        in_specs, out_specs, out_shape, operands = in_specs + xi, out_specs + xo, out_shape + xs, operands + xop
    return pl.pallas_call(
        body, name="sb_bwd" if exchange is None else "sb_bwd_exchange", grid=grid,
        in_specs=in_specs, out_specs=out_specs, out_shape=out_shape, scratch_shapes=scratch,
        compiler_params=_params(("arbitrary", "arbitrary", "arbitrary")),
    )(*operands)


def _adamw(w, g, m, v):
    rows, cols = w.shape
    tr = rows
    for cand in (400, 256, 128, 64, 32, 16, 8):
        if rows % cand == 0:
            tr = cand
            break

    def body(w_ref, g_ref, m_ref, v_ref, d_ref, nm_ref, nv_ref):
        g_ = g_ref[...]
        m_ = ADAM_B1 * m_ref[...] + (1.0 - ADAM_B1) * g_
        v_ = ADAM_B2 * v_ref[...] + (1.0 - ADAM_B2) * (g_ * g_)
        m_hat = m_ / (1.0 - ADAM_B1 ** ADAM_STEP)
        v_hat = v_ / (1.0 - ADAM_B2 ** ADAM_STEP)
        d_ref[...] = -ADAM_LR * (m_hat / (jnp.sqrt(v_hat) + ADAM_EPS) + ADAM_WD * w_ref[...])
        nm_ref[...] = m_
        nv_ref[...] = v_

    spec = pl.BlockSpec((tr, cols), lambda i: (i, 0))
    return pl.pallas_call(
        body, name="adamw", grid=(rows // tr,), in_specs=[spec] * 4, out_specs=[spec] * 3,
        out_shape=[_sds((rows, cols))] * 3, compiler_params=_params(("arbitrary",)),
    )(w, g, m, v)


_ANY = pl.BlockSpec(memory_space=pl.ANY)


def _place():
    return lax.axis_index("x"), lax.axis_index("y"), lax.axis_index("c")


def _gather_plan(x_ref, out_ref, send_sems, recv_sems, row0, nrows, nc):
    x, y, c = _place()
    me = 2 * x + y
    sibling = (x, y, 1 - c)
    half = nrows // 2
    ch = half // nc
    peers = [me ^ k for k in (1, 2, 3)]

    def rows(shard, hc, r):
        return out_ref.at[shard, pl.ds(row0 + hc * half + r * ch, ch), :]

    def copy(k, shard, hc, r, to, src=None):
        return pltpu.make_async_remote_copy(
            src_ref=rows(shard, hc, r) if src is None else src, dst_ref=rows(shard, hc, r),
            send_sem=send_sems.at[k * nc + r], recv_sem=recv_sems.at[k * nc + r], device_id=to, device_id_type=MESH)

    def first(k, p, r):
        return copy(k, me, c, r, (p >> 1, p & 1, c), src=x_ref.at[pl.ds(row0 + c * half + r * ch, ch), :])

    def start():
        for k, p in enumerate(peers):
            for r in range(nc):
                first(k, p, r).start()

    def forward():
        for k, p in enumerate(peers):
            for r in range(nc):
                copy(k, p, c, r, sibling).wait_recv()
                copy(3 + k, p, c, r, sibling).start()

    def finish():
        for k, p in enumerate(peers):
            for r in range(nc):
                copy(3 + k, p, 1 - c, r, sibling).wait_recv()
        for k, p in enumerate(peers):
            for r in range(nc):
                first(k, p, r).wait_send()
                copy(3 + k, p, c, r, sibling).wait_send()

    return start, forward, finish


def _gather_sems(nc):
    return [pltpu.SemaphoreType.DMA((6 * nc,)), pltpu.SemaphoreType.DMA((6 * nc,))]


def _swap_plan(g_ref, out_ref, send_sems, recv_sems):
    half, ch, nc = HALF_LAYER, CHUNK_ROWS, RS_CHUNKS

    def copies():
        x, y, c = _place()
        return [pltpu.make_async_remote_copy(
            src_ref=g_ref.at[j, pl.ds((1 - c) * half + r * ch, ch), :], dst_ref=out_ref.at[j, pl.ds(r * ch, ch), :],
            send_sem=send_sems.at[j * nc + r], recv_sem=recv_sems.at[j * nc + r],
            device_id=(x, y, 1 - c), device_id_type=MESH) for j in range(N_SHARD) for r in range(nc)]

    def start():
        for cp in copies():
            cp.start()

    def finish():
        for cp in copies():
            cp.wait()

    return start, finish


def _scatter_plan(p_ref, out_ref, send_sems, recv_sems):
    ch, nc = CHUNK_ROWS, RS_CHUNKS

    def copies():
        x, y, c = _place()
        me = 2 * x + y
        return [pltpu.make_async_remote_copy(
            src_ref=p_ref.at[me ^ k, pl.ds(r * ch, ch), :], dst_ref=out_ref.at[k - 1, pl.ds(r * ch, ch), :],
            send_sem=send_sems.at[(k - 1) * nc + r], recv_sem=recv_sems.at[(k - 1) * nc + r],
            device_id=((me ^ k) >> 1, (me ^ k) & 1, c), device_id_type=MESH) for k in (1, 2, 3) for r in range(nc)]

    def start():
        for cp in copies():
            cp.start()

    def finish():
        for cp in copies():
            cp.wait()

    return start, finish


SWAP = (_swap_plan, (N_SHARD, HALF_LAYER, D_MODEL), F32, N_SHARD * RS_CHUNKS)
SCATTER = (_scatter_plan, (3, HALF_LAYER, D_MODEL), BF16, 3 * RS_CHUNKS)


def _exchange_call(kind, operand):
    plan, shape, dtype, n_sems = kind

    def body(in_ref, out_ref, send_sems, recv_sems):
        start, finish = plan(in_ref, out_ref, send_sems, recv_sems)
        start()
        finish()

    return pl.pallas_call(
        body, name="exchange", in_specs=[_ANY], out_specs=_ANY, out_shape=_sds(shape, dtype),
        scratch_shapes=[pltpu.SemaphoreType.DMA((n_sems,)), pltpu.SemaphoreType.DMA((n_sems,))],
    )(operand)


def _with_exchange(body, n_in, n_out, exchange, step_of, n_steps):
    def wrapped(*refs):
        ins, src = refs[:n_in], refs[n_in]
        outs, dst = refs[n_in + 1:n_in + 1 + n_out], refs[n_in + 1 + n_out]
        send_sems, recv_sems = refs[n_in + 2 + n_out:n_in + 4 + n_out]
        start, finish = exchange[0][0](src, dst, send_sems, recv_sems)
        pl.when(step_of() == 0)(start)
        body(*ins, *outs, *refs[n_in + 4 + n_out:])
        pl.when(step_of() == n_steps - 1)(finish)

    return wrapped


def _exchange_args(exchange):
    (plan, shape, dtype, n_sems), operand = exchange
    sems = [pltpu.SemaphoreType.DMA((n_sems,)), pltpu.SemaphoreType.DMA((n_sems,))]
    return [_ANY], [_ANY], [_sds(shape, dtype)], sems, [operand]


def _gather_weights(flat, row0, nrows, nc):
    def body(x_ref, out_ref, send_sems, recv_sems):
        start, forward, finish = _gather_plan(x_ref, out_ref, send_sems, recv_sems, row0, nrows, nc)
        start()
        forward()
        finish()

    return pl.pallas_call(
        body, name="gather_weights", in_specs=[_ANY], out_specs=_ANY,
        out_shape=_sds((N_SHARD, ROWS_FLAT, D_MODEL), BF16), scratch_shapes=_gather_sems(nc),
    )(flat)


def _add_my_half(grads, recv):
    tr = 400
    nb = HALF_LAYER // tr
    core = lax.axis_index("c").astype(jnp.int32).reshape(1)

    def body(c_ref, g_ref, r_ref, o_ref, ob_ref):
        acc = g_ref[...] + r_ref[...]
        o_ref[...] = acc
        ob_ref[...] = _bf(acc)

    out = pl.BlockSpec((None, tr, D_MODEL), lambda j, i, c_ref: (j, i, 0))
    return pl.pallas_call(
        body, name="add_my_half",
        grid_spec=pltpu.PrefetchScalarGridSpec(
            num_scalar_prefetch=1, grid=(N_SHARD, nb),
            in_specs=[pl.BlockSpec((None, tr, D_MODEL), lambda j, i, c_ref: (j, c_ref[0] * nb + i, 0)), out],
            out_specs=[out, out]),
        out_shape=[_sds((N_SHARD, HALF_LAYER, D_MODEL)), _sds((N_SHARD, HALF_LAYER, D_MODEL), BF16)],
        compiler_params=_params(("arbitrary", "arbitrary")),
    )(core, grads, recv)


def _sum_scattered(part, recv):
    tr = 400
    chip = (2 * lax.axis_index("x") + lax.axis_index("y")).astype(jnp.int32).reshape(1)

    def body(c_ref, p_ref, r_ref, o_ref):
        acc = p_ref[...]
        for k in range(3):
            acc = acc + r_ref[k].astype(F32)
        o_ref[...] = acc

    return pl.pallas_call(
        body, name="sum_scattered",
        grid_spec=pltpu.PrefetchScalarGridSpec(
            num_scalar_prefetch=1, grid=(HALF_LAYER // tr,),
            in_specs=[pl.BlockSpec((None, tr, D_MODEL), lambda i, c_ref: (c_ref[0], i, 0)),
                      pl.BlockSpec((3, tr, D_MODEL), lambda i, c_ref: (0, i, 0))],
            out_specs=pl.BlockSpec((tr, D_MODEL), lambda i, c_ref: (i, 0))),
        out_shape=_sds((HALF_LAYER, D_MODEL)), compiler_params=_params(("arbitrary",)),
    )(chip, part, recv)


def _swap_reduced(mine):
    ch, nc = CHUNK_ROWS, RS_CHUNKS

    def body(r_ref, out_ref, send_sems, recv_sems):
        x, y, c = _place()
        copies = [pltpu.make_async_remote_copy(
            src_ref=r_ref.at[l, pl.ds(r * ch, ch), :], dst_ref=out_ref.at[l, pl.ds(r * ch, ch), :],
            send_sem=send_sems.at[l * nc + r], recv_sem=recv_sems.at[l * nc + r],
            device_id=(x, y, 1 - c), device_id_type=MESH) for l in range(2) for r in range(nc)]
        for cp in copies:
            cp.start()
        for cp in copies:
            cp.wait()

    return pl.pallas_call(
        body, name="swap_reduced", in_specs=[_ANY], out_specs=_ANY,
        out_shape=_sds((2, HALF_LAYER, D_MODEL)),
        scratch_shapes=[pltpu.SemaphoreType.DMA((2 * nc,)), pltpu.SemaphoreType.DMA((2 * nc,))],
    )(mine)


def _allreduce_small(vec):
    def body(v_ref, out_ref, buf, send_sems, recv_sems):
        x, y, c = _place()
        me = 4 * x + 2 * y + c
        buf[me] = v_ref[...]
        peers = [me ^ k for k in range(1, N_DEV)]
        sends = [pltpu.make_async_remote_copy(
            src_ref=v_ref, dst_ref=buf.at[me], send_sem=send_sems.at[k], recv_sem=recv_sems.at[k],
            device_id=(p >> 2, (p >> 1) & 1, p & 1), device_id_type=MESH) for k, p in enumerate(peers)]
        for cp in sends:
            cp.start()
        for k, p in enumerate(peers):
            pltpu.make_async_remote_copy(
                src_ref=v_ref, dst_ref=buf.at[p], send_sem=send_sems.at[k], recv_sem=recv_sems.at[k],
                device_id=(p >> 2, (p >> 1) & 1, p & 1), device_id_type=MESH).wait_recv()
        for cp in sends:
            cp.wait_send()
        acc = buf[0]
        for d in range(1, N_DEV):
            acc = acc + buf[d]
        out_ref[...] = acc

    vm = pl.BlockSpec(memory_space=pltpu.VMEM)
    return pl.pallas_call(
        body, name="allreduce_small", in_specs=[vm], out_specs=vm, out_shape=_sds((SMALL_ROWS, 128)),
        scratch_shapes=[pltpu.VMEM((N_DEV, SMALL_ROWS, 128), F32),
                        pltpu.SemaphoreType.DMA((N_DEV - 1,)), pltpu.SemaphoreType.DMA((N_DEV - 1,))],
    )(vec)


def _flatten_shard(w_in, w_out, w_pg, w_pp):
    return jnp.concatenate([w_in.reshape(-1, D_MODEL), w_out.reshape(-1, D_MODEL), w_pg.reshape(-1, D_MODEL),
                            w_pp.reshape(-1, D_MODEL)], axis=0)


def _unflatten_layers(flats):
    a, b, c = D_MODEL, D_MODEL + D_MODEL // N_SHARD, D_MODEL + 2 * (D_MODEL // N_SHARD)
    q = D_MODEL // N_SHARD
    return (jnp.stack([f[:a] for f in flats]), jnp.stack([f[a:b] for f in flats]),
            jnp.stack([f[b:c] for f in flats]), jnp.stack([f[c:].reshape(D_PLE, q) for f in flats]))


def _full_w_pp(gathered):
    c = ROWS_W_IN + ROWS_W_OUT + ROWS_W_PG
    q = D_MODEL // N_SHARD
    rpp = ROWS_W_PP // 2
    return [gathered[:, c + l * rpp:c + (l + 1) * rpp, :].reshape(N_SHARD, D_PLE, q).transpose(1, 0, 2)
            .reshape(D_PLE, D_MODEL) for l in range(2)]


def _layer_grads(dw_in, dw_out, dw_pg, dw_pp):
    q = D_MODEL // N_SHARD
    rpp = ROWS_W_PP // 2
    return jnp.concatenate([dw_in, dw_out.reshape(N_SHARD, q, D_MODEL), dw_pg.reshape(N_SHARD, q, D_MODEL),
                            dw_pp.reshape(D_PLE, N_SHARD, q).transpose(1, 0, 2).reshape(N_SHARD, rpp, D_MODEL)], axis=1)


def _lower_bounds(lb_logits):
    sm = jax.nn.softmax(lb_logits.astype(F32), axis=0)
    return jnp.cumsum(sm, axis=0) - sm[0:1]


def kernel(x, p, norm_mix, w_in, a_out_norm, b_out_norm, w_out, lb_logits, ple_gate_norm, w_ple_gate, w_ple_proj, ple_post_norm, final_norm, loss_target, m_norm_mix, m_w_in, m_a_out_norm, m_b_out_norm, m_w_out, m_lb_logits, m_ple_gate_norm, m_w_ple_gate, m_w_ple_proj, m_ple_post_norm, m_final_norm, v_norm_mix, v_w_in, v_a_out_norm, v_b_out_norm, v_w_out, v_lb_logits, v_ple_gate_norm, v_w_ple_gate, v_w_ple_proj, v_ple_post_norm, v_final_norm):
    t = x.shape[1]
    h0 = x.reshape(t, D_MODEL)
    target = loss_target.reshape(t, D_MODEL)
    pl_in = p.reshape(2, t, D_PLE)

    w_flat_bf = _flatten_shard(_bf(w_in), _bf(w_out), _bf(w_ple_gate), _bf(w_ple_proj))
    chip = 2 * lax.axis_index("x") + lax.axis_index("y")
    gathered = lax.dynamic_update_slice(_gather_weights(w_flat_bf, 0, D_MODEL, 2), w_flat_bf[None], (chip, 0, 0))
    lbs, lbs_vjp = jax.vjp(_lower_bounds, lb_logits)

    saved = []
    h = h0
    for l in range(2):
        g_mix = norm_mix[l].reshape(1, D_MODEL)
        lb = lbs[l].reshape(1, GROUP)
        ga = a_out_norm[l].reshape(1, GROUP)
        gb = b_out_norm[l].reshape(1, GROUP)
        proj, qkv, bgate = _inproj(h, g_mix, gathered, l)
        ya, states = _hgrn_fwd(proj, lb, ga)
        if l == 0:
            ob, yb, gathered = _sb_fwd(qkv, bgate, gb, (w_flat_bf, gathered, D_MODEL, ROWS_FLAT - D_MODEL, 4))
            w_pps = _full_w_pp(gathered)
        else:
            ob, yb = _sb_fwd(qkv, bgate, gb)
        h1 = _outproj(h, ya, yb, gathered, l)
        g_post = ple_post_norm[l].reshape(1, D_MODEL)
        g_gate = ple_gate_norm[l].reshape(1, D_MODEL)
        h2 = _ple_fwd(h1, pl_in[l], w_pps[l], gathered, l, g_post, g_gate)
        saved.append((h, proj, qkv, bgate, states, ya, yb, ob, h1))
        h = h2

    dh, d_final, loss_part = _final(h, final_norm.reshape(1, D_MODEL), target)

    g_layer, chip_sum, scattered = [None] * 2, [None] * 2, [None] * 2
    d_mix, d_a, d_b, d_lb, d_gate, d_post = [None] * 2, [None] * 2, [None] * 2, [None] * 2, [None] * 2, [None] * 2
    for l in (1, 0):
        h_in, proj, qkv, bgate, states, ya, yb, ob, h1 = saved[l]
        g_mix = norm_mix[l].reshape(1, D_MODEL)
        lb = lbs[l].reshape(1, GROUP)
        ga = a_out_norm[l].reshape(1, GROUP)
        gb = b_out_norm[l].reshape(1, GROUP)
        g_post = ple_post_norm[l].reshape(1, D_MODEL)
        g_gate = ple_gate_norm[l].reshape(1, D_MODEL)
        if l == 1:
            dh1, dw_pg, dw_pp, d_gate[l], d_post[l] = _ple_bwd(dh, h1, pl_in[l], w_pps[l], gathered, l, g_post, g_gate)
            dya, dyb, dw_out = _outproj_bwd(dh1, ya, yb, gathered, l)
            dbq, dbk, dbv, dbg, d_b[l] = _sb_bwd(qkv, bgate, ob, dyb, gb)
        else:
            dh1, dw_pg, dw_pp, d_gate[l], d_post[l], from_sibling = _ple_bwd(
                dh, h1, pl_in[l], w_pps[l], gathered, l, g_post, g_gate, (SWAP, g_layer[1]))
            chip_sum[1], chip_sum_bf = _add_my_half(g_layer[1], from_sibling)
            dya, dyb, dw_out = _outproj_bwd(dh1, ya, yb, gathered, l)
            dbq, dbk, dbv, dbg, d_b[l], scattered[1] = _sb_bwd(qkv, bgate, ob, dyb, gb, (SCATTER, chip_sum_bf))
        da, d_lb[l], d_a[l] = _hgrn_bwd(proj, lb, ga, states, dya)
        db = jnp.stack([dbq, dbk * LN2, dbv, dbg]).astype(BF16)
        g_layer[l] = _layer_grads(_inproj_bwd_dw(h_in, g_mix, da, db), dw_out, dw_pg, dw_pp)
        if l == 1:
            dh, d_mix[l] = _inproj_bwd_dx(dh1, h_in, g_mix, gathered, l, da, db)
        else:
            dh, d_mix[l], from_sibling = _inproj_bwd_dx(dh1, h_in, g_mix, gathered, l, da, db, (SWAP, g_layer[0]))
    grad_x = dh.reshape(x.shape)

    chip_sum[0], chip_sum_bf = _add_my_half(g_layer[0], from_sibling)
    scattered[0] = _exchange_call(SCATTER, chip_sum_bf)
    mine = jnp.stack([_sum_scattered(chip_sum[l], scattered[l]) for l in range(2)])
    other = _swap_reduced(mine)
    south = lax.axis_index("c") == 0
    g_w_in, g_w_out, g_w_pg, g_w_pp = _unflatten_layers(
        [jnp.concatenate([jnp.where(south, mine[l], other[l]), jnp.where(south, other[l], mine[l])]) for l in range(2)])

    small = jnp.concatenate([
        jnp.concatenate(d_mix).reshape(-1, 128), jnp.concatenate(d_a).reshape(-1, 128),
        jnp.concatenate(d_b).reshape(-1, 128), jnp.concatenate(d_lb).reshape(-1, 128),
        jnp.concatenate(d_gate).reshape(-1, 128), jnp.concatenate(d_post).reshape(-1, 128),
        d_final.reshape(-1, 128), jnp.broadcast_to(loss_part, (8, 128))], axis=0)
    small = _allreduce_small(small)
    loss = small[80, 0]
    g_norm_mix = small[0:16].reshape(2, D_MODEL)
    g_a = small[16:24].reshape(2, GROUP)
    g_b = small[24:32].reshape(2, GROUP)
    (g_lb,) = lbs_vjp(small[32:40].reshape(2, GROUP))
    g_gate = small[40:56].reshape(2, D_MODEL)
    g_post = small[56:72].reshape(2, D_MODEL)
    g_final = small[72:80].reshape(D_MODEL)

    def adam_matrix(w, g, m, v):
        d, nm, nv = _adamw(w.reshape(-1, D_MODEL), g.reshape(-1, D_MODEL), m.reshape(-1, D_MODEL), v.reshape(-1, D_MODEL))
        return d.reshape(w.shape), nm.reshape(w.shape), nv.reshape(w.shape)

    d_w_in, nm_w_in, nv_w_in = adam_matrix(w_in, g_w_in, m_w_in, v_w_in)
    d_w_out, nm_w_out, nv_w_out = adam_matrix(w_out, g_w_out, m_w_out, v_w_out)
    d_w_pg, nm_w_pg, nv_w_pg = adam_matrix(w_ple_gate, g_w_pg, m_w_ple_gate, v_w_ple_gate)
    d_w_pp, nm_w_pp, nv_w_pp = adam_matrix(w_ple_proj, g_w_pp, m_w_ple_proj, v_w_ple_proj)

    small_w = [norm_mix, a_out_norm, b_out_norm, lb_logits, ple_gate_norm, ple_post_norm, final_norm]
    small_g = [g_norm_mix, g_a, g_b, g_lb, g_gate, g_post, g_final]
    small_m = [m_norm_mix, m_a_out_norm, m_b_out_norm, m_lb_logits, m_ple_gate_norm, m_ple_post_norm, m_final_norm]
    small_v = [v_norm_mix, v_a_out_norm, v_b_out_norm, v_lb_logits, v_ple_gate_norm, v_ple_post_norm, v_final_norm]
    pack = lambda arrs: jnp.concatenate([a.reshape(-1, 128) for a in arrs], axis=0)
    ds, nms, nvs = _adamw(pack(small_w), pack(small_g), pack(small_m), pack(small_v))

    def unpack(packed):
        out, r = [], 0
        for a in small_w:
            n = a.size // 128
            out.append(packed[r:r + n].reshape(a.shape))
            r += n
        return out

    d_s, nm_s, nv_s = unpack(ds), unpack(nms), unpack(nvs)

    def ordered(s, big):
        return [s[0], big[0], s[1], s[2], big[1], s[3], s[4], big[2], big[3], s[5], s[6]]

    grads = ordered(small_g, [g_w_in, g_w_out, g_w_pg, g_w_pp])
    deltas = ordered(d_s, [d_w_in, d_w_out, d_w_pg, d_w_pp])
    new_m = ordered(nm_s, [nm_w_in, nm_w_out, nm_w_pg, nm_w_pp])
    new_v = ordered(nv_s, [nv_w_in, nv_w_out, nv_w_pg, nv_w_pp])
    return (loss, grad_x, *grads, *deltas, *new_m, *new_v)
```

```python
import functools
import math

import numpy as np
import jax
import jax.numpy as jnp
from jax import lax
from jax.experimental import pallas as pl
from jax.experimental.pallas import tpu as pltpu

F32 = jnp.float32
BF16 = jnp.bfloat16
MESH = pl.DeviceIdType.MESH

D_MODEL = 1024
D_PLE = 256
D_IN = 4096
A_HEADS, A_D = 4, 128
B_HEADS, B_D = 8, 64
GROUP = 512
EPS = 1e-6
N_SHARD = 4
N_DEV = 8

HG_CHUNK = 128
HG_LEVELS = 7
SB_TQ = 1024
SB_TK = 128
SB_GROUP_FWD, SB_GROUP_BWD = 4, 8
LOG2E = 1.4426950408889634
LN2 = 0.6931471805599453

ADAM_LR, ADAM_B1, ADAM_B2, ADAM_EPS, ADAM_WD, ADAM_STEP = 0.001, 0.9, 0.999, 1e-08, 0.01, 10

VMEM_LIMIT = 48 * 1024 * 1024
VMEM_LIMIT_BIG = 58 * 1024 * 1024

ROWS_W_IN = 2 * D_MODEL
ROWS_W_OUT = 2 * (D_MODEL // N_SHARD)
ROWS_W_PG = 2 * (D_MODEL // N_SHARD)
ROWS_W_PP = 2 * (D_PLE * (D_MODEL // N_SHARD) // D_MODEL)
ROWS_FLAT = ROWS_W_IN + ROWS_W_OUT + ROWS_W_PG + ROWS_W_PP
HALF_FLAT = ROWS_FLAT // 2
N_CHUNK = 10
CHUNK_ROWS = HALF_FLAT // N_CHUNK

ROWS_LAYER = ROWS_FLAT // 2
HALF_LAYER = ROWS_LAYER // 2
RS_CHUNKS = HALF_LAYER // CHUNK_ROWS

SMALL_ROWS = 88


def _sds(shape, dtype=F32):
    return jax.ShapeDtypeStruct(shape, dtype)


def _params(sem=None, vmem_limit=VMEM_LIMIT):
    kw = dict(vmem_limit_bytes=vmem_limit)
    if sem is not None:
        kw["dimension_semantics"] = sem
    return pltpu.CompilerParams(**kw)


def _dot(a, b, precision=None):
    return lax.dot_general(a, b, (((1,), (0,)), ((), ())), preferred_element_type=F32, precision=precision)


def _dot_nt(a, b, precision=None):
    return lax.dot_general(a, b, (((1,), (1,)), ((), ())), preferred_element_type=F32, precision=precision)


def _dot_tn(a, b, precision=None):
    return lax.dot_general(a, b, (((0,), (0,)), ((), ())), preferred_element_type=F32, precision=precision)


def _bf(x):
    return x.astype(BF16)


def _split(x):
    hi = x.astype(BF16)
    lo = (x - hi.astype(F32)).astype(BF16)
    return hi, lo


def _rms(x):
    r = lax.rsqrt(jnp.mean(x * x, axis=-1, keepdims=True) + EPS)
    return x * r, r


def _rms_bwd(dxh, xh, r):
    return r * (dxh - xh * jnp.mean(dxh * xh, axis=-1, keepdims=True))


def _sigmoid(x):
    return 1.0 / (1.0 + jnp.exp(-x))


def _silu_grad(x, sig):
    return sig * (1.0 + x * (1.0 - sig))


def _row_tile(t, want):
    return min(t, want)


def _inproj(h, g, gathered, layer):
    t = h.shape[0]
    tm = _row_tile(t, 512)

    def body(h_ref, g_ref, w_ref, pa_ref, qkv_ref, bg_ref):
        xh, _ = _rms(h_ref[...])
        u = _bf(xh * g_ref[...])
        for j in range(8):
            acc = _dot(u, w_ref[j // 2, :, pl.ds((j % 2) * GROUP, GROUP)])
            if j < 4:
                pa_ref[:, pl.ds(j * GROUP, GROUP)] = acc
            elif j == 4:
                qkv_ref[:, pl.ds(0, GROUP)] = _bf(acc * (B_D ** -0.5 * LOG2E))
            elif j < 7:
                qkv_ref[:, pl.ds((j - 4) * GROUP, GROUP)] = _bf(acc)
            else:
                bg_ref[...] = acc

    return pl.pallas_call(
        body, name="inproj", grid=(t // tm,),
        in_specs=[pl.BlockSpec((tm, D_MODEL), lambda i: (i, 0)),
                  pl.BlockSpec((1, D_MODEL), lambda i: (0, 0)),
                  pl.BlockSpec((N_SHARD, D_MODEL, D_MODEL), lambda i: (0, layer, 0))],
        out_specs=[pl.BlockSpec((tm, 4 * GROUP), lambda i: (i, 0)), pl.BlockSpec((tm, 3 * GROUP), lambda i: (i, 0)),
                   pl.BlockSpec((tm, GROUP), lambda i: (i, 0))],
        out_shape=[_sds((t, 4 * GROUP)), _sds((t, 3 * GROUP), BF16), _sds((t, GROUP))],
        compiler_params=_params(("arbitrary",)),
    )(h, g, gathered)


def _rows_spec(first_row):
    q = D_MODEL // N_SHARD
    return pl.BlockSpec((N_SHARD, q, D_MODEL), lambda i: (0, first_row // q, 0))


def _outproj(h, ya, yb, gathered, layer):
    t = h.shape[0]
    tm = _row_tile(t, 512)

    def body(h_ref, ya_ref, yb_ref, w_ref, o_ref):
        o_ref[...] = (h_ref[...] + _dot(_bf(ya_ref[...]), w_ref[0:2].reshape(GROUP, D_MODEL))
                      + _dot(_bf(yb_ref[...]), w_ref[2:4].reshape(GROUP, D_MODEL)))

    return pl.pallas_call(
        body, name="outproj", grid=(t // tm,),
        in_specs=[pl.BlockSpec((tm, D_MODEL), lambda i: (i, 0)),
                  pl.BlockSpec((tm, GROUP), lambda i: (i, 0)),
                  pl.BlockSpec((tm, GROUP), lambda i: (i, 0)),
                  _rows_spec(ROWS_W_IN + layer * (D_MODEL // N_SHARD))],
        out_specs=pl.BlockSpec((tm, D_MODEL), lambda i: (i, 0)),
        out_shape=_sds((t, D_MODEL)), compiler_params=_params(("arbitrary",)),
    )(h, ya, yb, gathered)


def _ple_fwd(h, p, w_pp, gathered, layer, g_post, g_gate):
    t = h.shape[0]
    tm = _row_tile(t, 256)

    def body(h_ref, p_ref, wpp_ref, wpg_ref, gp_ref, gg_ref, o_ref):
        x = h_ref[...]
        ph, _ = _rms(_dot(_bf(p_ref[...]), wpp_ref[...]))
        xh, _ = _rms(x)
        gate = _sigmoid(_dot(_bf(xh * gg_ref[...]), wpg_ref[...].reshape(D_MODEL, D_MODEL)))
        o_ref[...] = x + gate * (ph * gp_ref[...])

    return pl.pallas_call(
        body, name="ple_fwd", grid=(t // tm,),
        in_specs=[pl.BlockSpec((tm, D_MODEL), lambda i: (i, 0)),
                  pl.BlockSpec((tm, D_PLE), lambda i: (i, 0)),
                  pl.BlockSpec((D_PLE, D_MODEL), lambda i: (0, 0)),
                  _rows_spec(ROWS_W_IN + ROWS_W_OUT + layer * (D_MODEL // N_SHARD)),
                  pl.BlockSpec((1, D_MODEL), lambda i: (0, 0)),
                  pl.BlockSpec((1, D_MODEL), lambda i: (0, 0))],
        out_specs=pl.BlockSpec((tm, D_MODEL), lambda i: (i, 0)),
        out_shape=_sds((t, D_MODEL)), compiler_params=_params(("arbitrary",)),
    )(h, p, w_pp, gathered, g_post, g_gate)


def _ple_bwd(dh2, h, p, w_pp, gathered, layer, g_post, g_gate, exchange=None):
    t = h.shape[0]
    tm = _row_tile(t, 256)

    def body(d_ref, h_ref, p_ref, wpp_ref, wpg_ref, gp_ref, gg_ref, dh_ref, dwpg_ref, dwpp_ref, dgg_ref, dgp_ref):
        @pl.when(pl.program_id(0) == 0)
        def _():
            dwpg_ref[...] = jnp.zeros_like(dwpg_ref)
            dwpp_ref[...] = jnp.zeros_like(dwpp_ref)
            dgg_ref[...] = jnp.zeros_like(dgg_ref)
            dgp_ref[...] = jnp.zeros_like(dgp_ref)

        d = d_ref[...]
        x = h_ref[...]
        gp = gp_ref[...]
        gg = gg_ref[...]
        pb = _bf(p_ref[...])
        ph, rp = _rms(_dot(pb, wpp_ref[...]))
        pe = ph * gp
        xh, rx = _rms(x)
        un = _bf(xh * gg)
        wpg = wpg_ref[...].reshape(D_MODEL, D_MODEL)
        gate = _sigmoid(_dot(un, wpg))
        dgpre = _bf(d * pe * gate * (1.0 - gate))
        dun = _dot_nt(dgpre, wpg)
        dh_ref[...] = d + _rms_bwd(dun * gg, xh, rx)
        dgg_ref[...] += jnp.sum(dun * xh, axis=0, keepdims=True)
        dwpg_ref[...] += _dot_tn(un, dgpre)
        dpe = d * gate
        dgp_ref[...] += jnp.sum(dpe * ph, axis=0, keepdims=True)
        dwpp_ref[...] += _dot_tn(pb, _bf(_rms_bwd(dpe * gp, ph, rp)))

    in_specs = [pl.BlockSpec((tm, D_MODEL), lambda i: (i, 0)),
                pl.BlockSpec((tm, D_MODEL), lambda i: (i, 0)),
                pl.BlockSpec((tm, D_PLE), lambda i: (i, 0)),
                pl.BlockSpec((D_PLE, D_MODEL), lambda i: (0, 0)),
                _rows_spec(ROWS_W_IN + ROWS_W_OUT + layer * (D_MODEL // N_SHARD)),
                pl.BlockSpec((1, D_MODEL), lambda i: (0, 0)),
                pl.BlockSpec((1, D_MODEL), lambda i: (0, 0))]
    out_specs = [pl.BlockSpec((tm, D_MODEL), lambda i: (i, 0)),
                 pl.BlockSpec((D_MODEL, D_MODEL), lambda i: (0, 0)),
                 pl.BlockSpec((D_PLE, D_MODEL), lambda i: (0, 0)),
                 pl.BlockSpec((1, D_MODEL), lambda i: (0, 0)),
                 pl.BlockSpec((1, D_MODEL), lambda i: (0, 0))]
    out_shape = [_sds((t, D_MODEL)), _sds((D_MODEL, D_MODEL)), _sds((D_PLE, D_MODEL)),
                 _sds((1, D_MODEL)), _sds((1, D_MODEL))]
    operands = [dh2, h, p, w_pp, gathered, g_post, g_gate]
    scratch = []
    if exchange is not None:
        body = _with_exchange(body, 7, 5, exchange, lambda: pl.program_id(0), t // tm)
        xi, xo, xs, scratch, xop = _exchange_args(exchange)
        in_specs, out_specs, out_shape, operands = in_specs + xi, out_specs + xo, out_shape + xs, operands + xop
    return pl.pallas_call(
        body, name="ple_bwd" if exchange is None else "ple_bwd_exchange", grid=(t // tm,),
        in_specs=in_specs, out_specs=out_specs, out_shape=out_shape, scratch_shapes=scratch,
        compiler_params=_params(("arbitrary",)),
    )(*operands)


def _outproj_bwd(dh, ya, yb, gathered, layer):
    t = dh.shape[0]
    tm = _row_tile(t, 512)

    def body(d_ref, ya_ref, yb_ref, w_ref, dya_ref, dyb_ref, dw_ref):
        @pl.when(pl.program_id(0) == 0)
        def _():
            dw_ref[...] = jnp.zeros_like(dw_ref)

        d = _bf(d_ref[...])
        dya_ref[...] = _dot_nt(d, w_ref[0:2].reshape(GROUP, D_MODEL))
        dyb_ref[...] = _dot_nt(d, w_ref[2:4].reshape(GROUP, D_MODEL))
        dw_ref[pl.ds(0, GROUP), :] += _dot_tn(_bf(ya_ref[...]), d)
        dw_ref[pl.ds(GROUP, GROUP), :] += _dot_tn(_bf(yb_ref[...]), d)

    return pl.pallas_call(
        body, name="outproj_bwd", grid=(t // tm,),
        in_specs=[pl.BlockSpec((tm, D_MODEL), lambda i: (i, 0)),
                  pl.BlockSpec((tm, GROUP), lambda i: (i, 0)),
                  pl.BlockSpec((tm, GROUP), lambda i: (i, 0)),
                  _rows_spec(ROWS_W_IN + layer * (D_MODEL // N_SHARD))],
        out_specs=[pl.BlockSpec((tm, GROUP), lambda i: (i, 0)),
                   pl.BlockSpec((tm, GROUP), lambda i: (i, 0)),
                   pl.BlockSpec((D_MODEL, D_MODEL), lambda i: (0, 0))],
        out_shape=[_sds((t, GROUP)), _sds((t, GROUP)), _sds((D_MODEL, D_MODEL))],
        compiler_params=_params(("arbitrary",)),
    )(dh, ya, yb, gathered)


def _inproj_bwd_dx(dres, h, g, gathered, layer, da, db, exchange=None):
    t = h.shape[0]
    tm = _row_tile(t, 256)

    def body(dres_ref, h_ref, g_ref, w_ref, da_ref, db_ref, dh_ref, dg_ref):
        @pl.when(pl.program_id(0) == 0)
        def _():
            dg_ref[...] = jnp.zeros_like(dg_ref)

        du = jnp.zeros((tm, D_MODEL), F32)
        for i in range(8):
            part = da_ref[i] if i < 4 else db_ref[i - 4]
            du = du + _dot_nt(part, w_ref[i // 2, :, pl.ds((i % 2) * GROUP, GROUP)])
        xh, r = _rms(h_ref[...])
        dg_ref[...] += jnp.sum(du * xh, axis=0, keepdims=True)
        dh_ref[...] = dres_ref[...] + _rms_bwd(du * g_ref[...], xh, r)

    in_specs = [pl.BlockSpec((tm, D_MODEL), lambda i: (i, 0)),
                pl.BlockSpec((tm, D_MODEL), lambda i: (i, 0)),
                pl.BlockSpec((1, D_MODEL), lambda i: (0, 0)),
                pl.BlockSpec((N_SHARD, D_MODEL, D_MODEL), lambda i: (0, layer, 0)),
                pl.BlockSpec((4, tm, GROUP), lambda i: (0, i, 0)),
                pl.BlockSpec((4, tm, GROUP), lambda i: (0, i, 0))]
    out_specs = [pl.BlockSpec((tm, D_MODEL), lambda i: (i, 0)), pl.BlockSpec((1, D_MODEL), lambda i: (0, 0))]
    out_shape = [_sds((t, D_MODEL)), _sds((1, D_MODEL))]
    operands = [dres, h, g, gathered, da, db]
    scratch = []
    if exchange is not None:
        body = _with_exchange(body, 6, 2, exchange, lambda: pl.program_id(0), t // tm)
        xi, xo, xs, scratch, xop = _exchange_args(exchange)
        in_specs, out_specs, out_shape, operands = in_specs + xi, out_specs + xo, out_shape + xs, operands + xop
    return pl.pallas_call(
        body, name="inproj_bwd_dx" if exchange is None else "inproj_bwd_dx_exchange", grid=(t // tm,),
        in_specs=in_specs, out_specs=out_specs, out_shape=out_shape, scratch_shapes=scratch,
        compiler_params=_params(("arbitrary",)),
    )(*operands)


def _inproj_bwd_dw(h, g, da, db):
    t = h.shape[0]
    tm = _row_tile(t, 512)

    def body(h_ref, g_ref, da_ref, db_ref, dw_ref):
        @pl.when(pl.program_id(0) == 0)
        def _():
            dw_ref[...] = jnp.zeros_like(dw_ref)

        xh, _ = _rms(h_ref[...])
        u = _bf(xh * g_ref[...])
        for i in range(8):
            dw_ref[i // 2, :, pl.ds((i % 2) * GROUP, GROUP)] += _dot_tn(u, da_ref[i] if i < 4 else db_ref[i - 4])

    return pl.pallas_call(
        body, name="inproj_bwd_dw", grid=(t // tm,),
        in_specs=[pl.BlockSpec((tm, D_MODEL), lambda i: (i, 0)),
                  pl.BlockSpec((1, D_MODEL), lambda i: (0, 0)),
                  pl.BlockSpec((4, tm, GROUP), lambda i: (0, i, 0)),
                  pl.BlockSpec((4, tm, GROUP), lambda i: (0, i, 0))],
        out_specs=pl.BlockSpec((N_SHARD, D_MODEL, D_MODEL), lambda i: (0, 0, 0)),
        out_shape=_sds((N_SHARD, D_MODEL, D_MODEL)), compiler_params=_params(("arbitrary",), VMEM_LIMIT_BIG),
    )(h, g, da, db)


def _final(h, g, target):
    t = h.shape[0]
    tm = _row_tile(t, 512)

    def body(h_ref, g_ref, t_ref, dh_ref, dg_ref, loss_ref):
        @pl.when(pl.program_id(0) == 0)
        def _():
            dg_ref[...] = jnp.zeros_like(dg_ref)
            loss_ref[...] = jnp.zeros_like(loss_ref)

        xh, r = _rms(h_ref[...])
        gg = g_ref[...]
        err = xh * gg - t_ref[...]
        part = 0.5 * jnp.sum(jnp.mean(err * err, axis=-1, keepdims=True), axis=0, keepdims=True)
        loss_ref[...] += jnp.broadcast_to(part, loss_ref.shape)
        dy = err * (1.0 / D_MODEL)
        dg_ref[...] += jnp.sum(dy * xh, axis=0, keepdims=True)
        dh_ref[...] = _rms_bwd(dy * gg, xh, r)

    return pl.pallas_call(
        body, name="final", grid=(t // tm,),
        in_specs=[pl.BlockSpec((tm, D_MODEL), lambda i: (i, 0)),
                  pl.BlockSpec((1, D_MODEL), lambda i: (0, 0)),
                  pl.BlockSpec((tm, D_MODEL), lambda i: (i, 0))],
        out_specs=[pl.BlockSpec((tm, D_MODEL), lambda i: (i, 0)),
                   pl.BlockSpec((1, D_MODEL), lambda i: (0, 0)),
                   pl.BlockSpec((1, 128), lambda i: (0, 0))],
        out_shape=[_sds((t, D_MODEL)), _sds((1, D_MODEL)), _sds((1, 128))],
        compiler_params=_params(("arbitrary",)),
    )(h, g, target)


def _hgrn_consts():
    c, nl = HG_CHUNK, HG_LEVELS
    t = np.arange(c)
    tril = np.tril(np.ones((c, c), np.float32))
    masks = np.zeros((nl + 1, c, c), np.float32)
    masks[0] = np.eye(c, dtype=np.float32)
    dmat = np.zeros(((nl + 2) * c, c), np.float32)
    dmat[0:c] = tril
    for l in range(nl):
        m = c >> (l + 1)
        blk = t // (2 * m)
        r = blk * 2 * m + m - 1
        upper = (t % (2 * m)) >= m
        masks[l + 1] = ((blk[:, None] == blk[None, :]) & upper[:, None] & (~upper)[None, :]).astype(np.float32)
        dmat[(l + 1) * c:(l + 2) * c] = tril[t] - tril[r]
    dmat[(nl + 1) * c:] = np.triu(np.ones((c, c), np.float32), k=1)
    return jnp.asarray(masks), jnp.asarray(dmat, BF16)


HG_HEADS = 4


def _hgrn_pre(aq, af, lb):
    sq = _sigmoid(aq)
    sneg = _sigmoid(-af)
    kk = (1.0 - lb) * sneg
    return sq, aq * sq, sneg, kk, jnp.log1p(-kk)


def _hgrn_x(logf, dmat_ref):
    dm = dmat_ref[pl.ds(0, (HG_LEVELS + 1) * HG_CHUNK), :]
    lhi, llo = _split(logf)
    return _dot(dm, lhi) + _dot(dm, llo)


def _hgrn_level(x_all, l, q, kk):
    c = HG_CHUNK
    x = x_all[(l + 1) * c:(l + 2) * c]
    qf = jnp.exp(jnp.minimum(x, 0.0))
    kf = jnp.exp(-jnp.maximum(x, 0.0))
    return qf, kf, _bf(q * qf), _bf(kk * kf)


def _hgrn_scores(xs, qs, kks, mask_ref):
    ps = [mask_ref[0] * _dot_nt(_bf(q), _bf(kk)) for q, kk in zip(qs, kks)]
    for l in range(HG_LEVELS):
        for i, (x_all, q, kk) in enumerate(zip(xs, qs, kks)):
            _, _, ql, kl = _hgrn_level(x_all, l, q, kk)
            ps[i] = ps[i] + mask_ref[l + 1] * _dot_nt(ql, kl)
    return ps


def _hgrn_specs(n_chunks, rev):
    c, w = HG_CHUNK, HG_HEADS * A_D
    cidx = (lambda n: n_chunks - 1 - n) if rev else (lambda n: n)
    col = lambda g: pl.BlockSpec((c, w), lambda h, n: (cidx(n), g * (A_HEADS // HG_HEADS) + h))
    vec = pl.BlockSpec((1, w), lambda h, n: (0, h))
    mask = pl.BlockSpec((HG_LEVELS + 1, c, c), lambda h, n: (0, 0, 0))
    dmat = pl.BlockSpec(((HG_LEVELS + 2) * c, c), lambda h, n: (0, 0))
    state = pl.BlockSpec((HG_HEADS, None, A_D, A_D), lambda h, n: (h, cidx(n), 0, 0))
    return cidx, col, vec, mask, dmat, state


def _lanes(i):
    return pl.ds(i * A_D, A_D)


def _hgrn_fwd(proj, lb, gain):
    t = proj.shape[0]
    c = HG_CHUNK
    nch = t // c
    masks, dmat = _hgrn_consts()
    cidx, col, vec, mask_spec, dmat_spec, state_spec = _hgrn_specs(nch, False)
    heads = range(HG_HEADS)

    def body(aq_ref, af_ref, ai_ref, ag_ref, lb_ref, gain_ref, mask_ref, dmat_ref, y_ref, st_ref, s_scr):
        @pl.when(pl.program_id(1) == 0)
        def _():
            s_scr[...] = jnp.zeros_like(s_scr)

        pre = [_hgrn_pre(aq_ref[:, _lanes(i)], af_ref[:, _lanes(i)], lb_ref[:, _lanes(i)]) for i in heads]
        qs, kks = [p[1] for p in pre], [p[3] for p in pre]
        xs = [_hgrn_x(p[4], dmat_ref) for p in pre]
        bs = [x[0:c] for x in xs]
        b_lasts = [jnp.sum(p[4], axis=0, keepdims=True) for p in pre]
        ps = _hgrn_scores(xs, qs, kks, mask_ref)
        ss = [s_scr[i] for i in heads]
        vbs = [_bf(ai_ref[:, _lanes(i)]) for i in heads]
        os_ = [_dot(_bf(ps[i]), vbs[i]) + _dot_nt(_bf(qs[i] * jnp.exp(bs[i])), _bf(ss[i])) for i in heads]
        for i in heads:
            st_ref[i] = ss[i]
            s_scr[i] = ss[i] * jnp.exp(b_lasts[i]) + _dot_tn(vbs[i], _bf(kks[i] * jnp.exp(b_lasts[i] - bs[i])))
            oh, _ = _rms(os_[i])
            ag = ag_ref[:, _lanes(i)]
            y_ref[:, _lanes(i)] = oh * gain_ref[:, _lanes(i)] * (ag * _sigmoid(ag))

    return pl.pallas_call(
        body, name="hgrn_fwd", grid=(A_HEADS // HG_HEADS, nch),
        in_specs=[col(0), col(1), col(2), col(3), vec, vec, mask_spec, dmat_spec],
        out_specs=[pl.BlockSpec((c, HG_HEADS * A_D), lambda h, n: (n, h)), state_spec],
        out_shape=[_sds((t, GROUP)), _sds((A_HEADS, nch, A_D, A_D))],
        scratch_shapes=[pltpu.VMEM((HG_HEADS, A_D, A_D), F32)],
        compiler_params=_params(("arbitrary", "arbitrary")),
    )(proj, proj, proj, proj, lb, gain, masks, dmat)


def _hgrn_bwd(proj, lb, gain, states, dya):
    t = proj.shape[0]
    c, nl = HG_CHUNK, HG_LEVELS
    nch = t // c
    masks, dmat = _hgrn_consts()
    cidx, col, vec, mask_spec, dmat_spec, state_spec = _hgrn_specs(nch, True)
    heads = range(HG_HEADS)

    def body(aq_ref, af_ref, ai_ref, ag_ref, lb_ref, gain_ref, mask_ref, dmat_ref, st_ref, dy_ref,
             da_ref, dlb_ref, dgain_ref, ds_scr, z_scr):
        @pl.when(pl.program_id(1) == 0)
        def _():
            ds_scr[...] = jnp.zeros_like(ds_scr)
            dlb_ref[...] = jnp.zeros_like(dlb_ref)
            dgain_ref[...] = jnp.zeros_like(dgain_ref)

        aqs = [aq_ref[:, _lanes(i)] for i in heads]
        lbs = [lb_ref[:, _lanes(i)] for i in heads]
        pre = [_hgrn_pre(aqs[i], af_ref[:, _lanes(i)], lbs[i]) for i in heads]
        sqs, qs, snegs, kks = ([p[j] for p in pre] for j in range(4))
        xs = [_hgrn_x(p[4], dmat_ref) for p in pre]
        bs = [x[0:c] for x in xs]
        b_lasts = [jnp.sum(p[4], axis=0, keepdims=True) for p in pre]
        ebs = [jnp.exp(b) for b in bs]
        ebls = [jnp.exp(bl - b) for bl, b in zip(b_lasts, bs)]
        ebl_rows = [jnp.exp(bl) for bl in b_lasts]
        qes = [_bf(q * eb) for q, eb in zip(qs, ebs)]
        kes = [_bf(kk * ebl) for kk, ebl in zip(kks, ebls)]
        vbs = [_bf(ai_ref[:, _lanes(i)]) for i in heads]
        ss = [st_ref[i] for i in heads]
        sbs = [_bf(s) for s in ss]
        dss = [ds_scr[i] for i in heads]
        dsbs = [_bf(ds) for ds in dss]

        pbs = [_bf(p) for p in _hgrn_scores(xs, qs, kks, mask_ref)]
        os_ = [_dot(pbs[i], vbs[i]) + _dot_nt(qes[i], sbs[i]) for i in heads]

        dos = []
        for i in heads:
            ag, gain, dy = ag_ref[:, _lanes(i)], gain_ref[:, _lanes(i)], dy_ref[:, _lanes(i)]
            oh, r = _rms(os_[i])
            sg_sig = _sigmoid(ag)
            sg = ag * sg_sig
            da_ref[3, :, _lanes(i)] = _bf(dy * oh * gain * _silu_grad(ag, sg_sig))
            dgain_ref[:, _lanes(i)] += jnp.sum(dy * oh * sg, axis=0, keepdims=True)
            dos.append(_bf(_rms_bwd(dy * gain * sg, oh, r)))

        dps = [_dot_nt(dos[i], vbs[i]) for i in heads]
        for i in heads:
            da_ref[2, :, _lanes(i)] = _bf(_dot_tn(pbs[i], dos[i]) + _dot_nt(kes[i], dsbs[i]))
        dq_ss = [ebs[i] * _dot(dos[i], sbs[i]) for i in heads]
        dk_ss = [ebls[i] * _dot(vbs[i], dsbs[i]) for i in heads]
        dqs, dks = [], []
        for i in heads:
            dpd = jnp.sum(mask_ref[0] * dps[i], axis=1, keepdims=True)
            z_scr[i, pl.ds(0, c), :] = qs[i] * dq_ss[i]
            z_scr[i, pl.ds((nl + 1) * c, c), :] = kks[i] * dk_ss[i]
            dqs.append(dq_ss[i] + dpd * kks[i])
            dks.append(dk_ss[i] + dpd * qs[i])
        for l in range(nl):
            for i in heads:
                qf, kf, ql, kl = _hgrn_level(xs[i], l, qs[i], kks[i])
                dpl = _bf(mask_ref[l + 1] * dps[i])
                dq_l = qf * _dot(dpl, kl)
                dk_l = kf * _dot_tn(dpl, ql)
                z_scr[i, pl.ds((l + 1) * c, c), :] = qs[i] * dq_l - kks[i] * dk_l
                dqs[i] = dqs[i] + dq_l
                dks[i] = dks[i] + dk_l

        zsplits = [_split(z_scr[i]) for i in heads]
        dlogfs = [_dot_tn(dmat_ref[...], zhi) + _dot_tn(dmat_ref[...], zlo) for zhi, zlo in zsplits]
        ds_new = [_dot_tn(dos[i], qes[i]) for i in heads]
        for i in heads:
            dlogf = dlogfs[i] + ebl_rows[i] * jnp.sum(dss[i] * ss[i], axis=0, keepdims=True)
            dkk = dks[i] - dlogf / (1.0 - kks[i])
            da_ref[1, :, _lanes(i)] = _bf(dkk * (1.0 - lbs[i]) * (-(snegs[i] * (1.0 - snegs[i]))))
            dlb_ref[:, _lanes(i)] += jnp.sum(dkk * (-snegs[i]), axis=0, keepdims=True)
            da_ref[0, :, _lanes(i)] = _bf(dqs[i] * _silu_grad(aqs[i], sqs[i]))
            ds_scr[i] = dss[i] * ebl_rows[i] + ds_new[i]

    w = HG_HEADS * A_D
    return pl.pallas_call(
        body, name="hgrn_bwd", grid=(A_HEADS // HG_HEADS, nch),
        in_specs=[col(0), col(1), col(2), col(3), vec, vec, mask_spec, dmat_spec, state_spec,
                  pl.BlockSpec((c, w), lambda h, n: (cidx(n), h))],
        out_specs=[pl.BlockSpec((4, c, w), lambda h, n: (0, cidx(n), h)), vec, vec],
        out_shape=[_sds((4, t, GROUP), BF16)] + [_sds((1, GROUP))] * 2,
        scratch_shapes=[pltpu.VMEM((HG_HEADS, A_D, A_D), F32), pltpu.VMEM((HG_HEADS, (nl + 2) * c, A_D), F32)],
        compiler_params=_params(("arbitrary", "arbitrary")),
    )(proj, proj, proj, proj, lb, gain, masks, dmat, states, dya)


def _sb_consts():
    j = np.arange(SB_TK)
    strict = (j[:, None] > j[None, :]).astype(np.float32)
    lower = (j[:, None] < j[None, :]).astype(np.float32)
    return jnp.asarray(strict, BF16), jnp.asarray(lower, BF16)


def _lane(x, k):
    return jnp.broadcast_to(x[:, k:k + 1], x.shape)


def _key_lane(kb):
    return lax.broadcasted_iota(jnp.int32, (1, SB_TK), 1) == kb


def _causal(x, masked):
    if not masked:
        return x
    n = (SB_TK, SB_TK)
    top = jnp.where(lax.broadcasted_iota(jnp.int32, n, 1) < lax.broadcasted_iota(jnp.int32, n, 0), x[:SB_TK], 0.0)
    return top if x.shape[0] == SB_TK else jnp.concatenate([top, x[SB_TK:]], axis=0)


def _sb_softplus(z, masked):
    logsig = jnp.minimum(z, 0.0) - jnp.log2(1.0 + jnp.exp2(-jnp.abs(z)))
    return _causal(z - logsig, masked), logsig


def _sb_sweep(qi, group_fn, state, group, ascending):
    nd = SB_TQ // SB_TK

    def tile(kb, r0, masked):
        return (pl.multiple_of(kb * SB_TK, SB_TK), r0, masked, kb)

    def run(tiles, st):
        for i in range(0, len(tiles), group):
            st = group_fn(tiles[i:i + group], st)
        return st

    diag = [tile(qi * nd + d, d * SB_TK, True) for d in range(nd)]
    if ascending:
        state = lax.fori_loop(0, qi, lambda j, st: run([tile(j * nd + g, 0, False) for g in range(nd)], st), state)
        return run(diag, state)
    state = run(diag[::-1], state)
    return lax.fori_loop(
        0, qi, lambda j, st: run([tile((qi - j) * nd - 1 - g, 0, False) for g in range(nd)], st), state)


def _set_rows(r0, full, new):
    return new if r0 == 0 else jnp.concatenate([full[:r0], new], axis=0)


def _sb_specs(t, tq):
    col = lambda g: pl.BlockSpec((tq, 2 * B_D), lambda p, i, h: (i, g * (GROUP // (2 * B_D)) + p))
    full = lambda g: pl.BlockSpec((t, 2 * B_D), lambda p, i, h: (0, g * (GROUP // (2 * B_D)) + p))
    vec = pl.BlockSpec((1, 2 * B_D), lambda p, i, h: (0, p))
    mat = pl.BlockSpec((SB_TK, SB_TK), lambda p, i, h: (0, 0))
    car = pl.BlockSpec((None, tq, SB_TK), lambda p, i, h: (2 * p + h, i, 0))
    return col, full, vec, mat, car


def _head_lanes(h):
    return (lax.broadcasted_iota(jnp.int32, (1, 2 * B_D), 1) >= B_D) == (h == 1)


def _put(ref, h, val):
    @pl.when(h == 0)
    def _():
        ref[...] = val

    @pl.when(h == 1)
    def _():
        ref[...] += val


def _sb_fwd(qkv, bgate, gain, gather=None):
    t = qkv.shape[0]
    tq = SB_TQ
    strict, _ = _sb_consts()
    n_steps = (B_HEADS // 2, t // tq, 2)

    def body(q_ref, k_ref, v_ref, bg_ref, gain_ref, m_ref, *rest):
        if gather is None:
            o_ref, y_ref, car_ref = rest
        else:
            flat_ref, _, o_ref, y_ref, car_ref, gathered_ref, send_sems, recv_sems = rest
            start, forward, finish = _gather_plan(flat_ref, gathered_ref, send_sems, recv_sems, *gather[2:])
            step = (pl.program_id(0) * n_steps[1] + pl.program_id(1)) * n_steps[2] + pl.program_id(2)
            pl.when(step == 0)(start)
            pl.when(step == 2 * n_steps[1] * n_steps[2])(forward)
            pl.when(step == n_steps[0] * n_steps[1] * n_steps[2] - 1)(finish)
        h = pl.program_id(2)
        lanes = _head_lanes(h)
        qb = jnp.where(lanes, q_ref[...], jnp.zeros_like(q_ref))
        cmat = m_ref[...]

        def group(tiles, state):
            carry, acc, cars = state
            kv = [(k_ref[pl.ds(off, SB_TK), :], v_ref[pl.ds(off, SB_TK), :]) for off, _, _, _ in tiles]
            zs = [_dot_nt(qb[r0:], kb) for (_, r0, _, _), (kb, _) in zip(tiles, kv)]
            sps = [_sb_softplus(z, masked) for z, (_, _, masked, _) in zip(zs, tiles)]
            css = [_dot(_bf(sp), cmat) for sp, _ in sps]
            ws = []
            for (sp, logsig), cs, (_, r0, masked, kb) in zip(sps, css, tiles):
                ws.append(_bf(_causal(jnp.exp2(logsig - cs - carry[r0:]), masked)))
                cars = _set_rows(r0, cars, jnp.where(_key_lane(kb), carry[r0:], cars[r0:]))
                carry = _set_rows(r0, carry, carry[r0:] + _lane(cs + sp, 0))
            for w, (_, vb), (_, r0, _, _) in zip(ws, kv, tiles):
                acc = _set_rows(r0, acc, acc[r0:] + _dot(w, vb))
            return carry, acc, cars

        zero = jnp.zeros((tq, SB_TK), F32)
        _, acc, cars = _sb_sweep(pl.program_id(1), group, (zero, jnp.zeros((tq, 2 * B_D), F32), zero),
                                 SB_GROUP_FWD, False)
        car_ref[...] = cars
        o = jnp.where(lanes, acc, 0.0)
        oh = o * lax.rsqrt(jnp.sum(o * o, axis=-1, keepdims=True) * (1.0 / B_D) + EPS)
        bg = bg_ref[...]
        _put(o_ref, h, o)
        _put(y_ref, h, oh * gain_ref[...] * (bg * _sigmoid(bg)))

    col, full, vec, mat, car = _sb_specs(t, tq)
    out = pl.BlockSpec((tq, 2 * B_D), lambda p, i, h: (i, p))
    in_specs = [col(0), full(1), full(2), col(0), vec, mat]
    out_specs = [out, out, car]
    out_shape = [_sds((t, GROUP)), _sds((t, GROUP)), _sds((B_HEADS, t, SB_TK))]
    operands = [qkv, qkv, qkv, bgate, gain, strict]
    extra = {}
    if gather is not None:
        in_specs += [_ANY, _ANY]
        out_specs += [_ANY]
        out_shape += [_sds(gather[1].shape, gather[1].dtype)]
        operands += [gather[0], gather[1]]
        extra = dict(input_output_aliases={7: 3}, scratch_shapes=_gather_sems(gather[4]))
    return pl.pallas_call(
        body, name="sb_fwd" if gather is None else "sb_fwd_gather", grid=n_steps,
        in_specs=in_specs, out_specs=out_specs, out_shape=out_shape,
        compiler_params=_params(("arbitrary", "arbitrary", "arbitrary")), **extra,
    )(*operands)


def _sb_bwd(qkv, bgate, o, carries, dy, gain, exchange=None):
    t = qkv.shape[0]
    tq = SB_TQ
    strict, lower = _sb_consts()

    def body(q_ref, k_ref, v_ref, bg_ref, o_ref, car_ref, dy_ref, gain_ref, ms_ref, ml_ref,
             dq_ref, dk_ref, dv_ref, dbg_ref, dgain_ref):
        qi = pl.program_id(1)
        h = pl.program_id(2)
        lanes = _head_lanes(h)

        @pl.when((qi == 0) & (h == 0))
        def _():
            dk_ref[...] = jnp.zeros_like(dk_ref)
            dv_ref[...] = jnp.zeros_like(dv_ref)
            dgain_ref[...] = jnp.zeros_like(dgain_ref)

        qb = jnp.where(lanes, q_ref[...], jnp.zeros_like(q_ref))
        cmat = ms_ref[...]
        lmat = ml_ref[...]
        cars = car_ref[...]
        o = jnp.where(lanes, o_ref[...], 0.0)
        dy = jnp.where(lanes, dy_ref[...], 0.0)
        bg = bg_ref[...]
        gain = gain_ref[...]
        r = lax.rsqrt(jnp.sum(o * o, axis=-1, keepdims=True) * (1.0 / B_D) + EPS)
        oh = o * r
        sig = _sigmoid(bg)
        sg = bg * sig
        _put(dbg_ref, h, dy * oh * gain * _silu_grad(bg, sig))
        dgain_ref[...] += jnp.sum(dy * oh * sg, axis=0, keepdims=True)
        doh = dy * gain * sg
        do = _bf(r * (doh - oh * (jnp.sum(doh * oh, axis=-1, keepdims=True) * (1.0 / B_D))))

        def group(tiles, state):
            gleft, dq = state
            kv = [(k_ref[pl.ds(off, SB_TK), :], v_ref[pl.ds(off, SB_TK), :]) for off, _, _, _ in tiles]
            zs = [_dot_nt(qb[r0:], kb) for (_, r0, _, _), (kb, _) in zip(tiles, kv)]
            dws = [_dot_nt(do[r0:], vb) for (_, r0, _, _), (_, vb) in zip(tiles, kv)]
            sps = [_sb_softplus(z, masked) for z, (_, _, masked, _) in zip(zs, tiles)]
            css = [_dot(_bf(sp), cmat) for sp, _ in sps]
            ws, gs = [], []
            for (_, logsig), cs, dw, (_, r0, masked, kb) in zip(sps, css, dws, tiles):
                right = jnp.sum(jnp.where(_key_lane(kb), cars[r0:], 0.0), axis=1, keepdims=True)
                w = _causal(jnp.exp2(logsig - cs - right), masked)
                ws.append(_bf(w))
                gs.append(dw * w)
            gps = [_dot(_bf(g), lmat) for g in gs]
            dzs = []
            for (_, logsig), g, gp, (_, r0, masked, _) in zip(sps, gs, gps, tiles):
                dz = g - jnp.exp2(logsig) * (g + gleft[r0:] + gp)
                dzs.append(_bf(_causal(dz, masked)))
                gleft = _set_rows(r0, gleft, gleft[r0:] + _lane(gp + g, SB_TK - 1))
            for dz, wb, (kb, _), (off, r0, _, _) in zip(dzs, ws, kv, tiles):
                dq = _set_rows(r0, dq, dq[r0:] + _dot(dz, kb))
                dk_ref[pl.ds(off, SB_TK), :] += _dot_tn(dz, qb[r0:])
                dv_ref[pl.ds(off, SB_TK), :] += _dot_tn(wb, do[r0:])
            return gleft, dq

        _, dq = _sb_sweep(qi, group, (jnp.zeros((tq, SB_TK), F32), jnp.zeros((tq, 2 * B_D), F32)), SB_GROUP_BWD, True)
        _put(dq_ref, h, jnp.where(lanes, dq * (B_D ** -0.5), 0.0))

    col, full, vec, mat, car = _sb_specs(t, tq)
    blk = pl.BlockSpec((tq, 2 * B_D), lambda p, i, h: (i, p))
    whole = pl.BlockSpec((t, 2 * B_D), lambda p, i, h: (0, p))
    grid = (B_HEADS // 2, t // tq, 2)
    in_specs = [col(0), full(1), full(2), col(0), blk, car, blk, vec, mat, mat]
    out_specs = [blk, whole, whole, blk, vec]
    out_shape = [_sds((t, GROUP))] * 4 + [_sds((1, GROUP))]
    operands = [qkv, qkv, qkv, bgate, o, carries, dy, gain, strict, lower]
    scratch = []
    if exchange is not None:
        step_of = lambda: (pl.program_id(0) * grid[1] + pl.program_id(1)) * grid[2] + pl.program_id(2)
        body = _with_exchange(body, 10, 5, exchange, step_of, grid[0] * grid[1] * grid[2])
        xi, xo, xs, scratch, xop = _exchange_args(exchange)
        in_specs, out_specs, out_shape, operands = in_specs + xi, out_specs + xo, out_shape + xs, operands + xop
    return pl.pallas_call(
        body, name="sb_bwd" if exchange is None else "sb_bwd_exchange", grid=grid,
        in_specs=in_specs, out_specs=out_specs, out_shape=out_shape, scratch_shapes=scratch,
        compiler_params=_params(("arbitrary", "arbitrary", "arbitrary")),
    )(*operands)


def _adamw(w, g, m, v):
    rows, cols = w.shape
    tr = rows
    for cand in (400, 256, 128, 64, 32, 16, 8):
        if rows % cand == 0:
            tr = cand
            break

    def body(w_ref, g_ref, m_ref, v_ref, d_ref, nm_ref, nv_ref):
        g_ = g_ref[...]
        m_ = ADAM_B1 * m_ref[...] + (1.0 - ADAM_B1) * g_
        v_ = ADAM_B2 * v_ref[...] + (1.0 - ADAM_B2) * (g_ * g_)
        m_hat = m_ / (1.0 - ADAM_B1 ** ADAM_STEP)
        v_hat = v_ / (1.0 - ADAM_B2 ** ADAM_STEP)
        d_ref[...] = -ADAM_LR * (m_hat / (jnp.sqrt(v_hat) + ADAM_EPS) + ADAM_WD * w_ref[...])
        nm_ref[...] = m_
        nv_ref[...] = v_

    spec = pl.BlockSpec((tr, cols), lambda i: (i, 0))
    return pl.pallas_call(
        body, name="adamw", grid=(rows // tr,), in_specs=[spec] * 4, out_specs=[spec] * 3,
        out_shape=[_sds((rows, cols))] * 3, compiler_params=_params(("arbitrary",)),
    )(w, g, m, v)


_ANY = pl.BlockSpec(memory_space=pl.ANY)


def _place():
    return lax.axis_index("x"), lax.axis_index("y"), lax.axis_index("c")


def _gather_plan(x_ref, out_ref, send_sems, recv_sems, row0, nrows, nc):
    x, y, c = _place()
    me = 2 * x + y
    sibling = (x, y, 1 - c)
    half = nrows // 2
    ch = half // nc
    peers = [me ^ k for k in (1, 2, 3)]

    def rows(shard, hc, r):
        return out_ref.at[shard, pl.ds(row0 + hc * half + r * ch, ch), :]

    def copy(k, shard, hc, r, to, src=None):
        return pltpu.make_async_remote_copy(
            src_ref=rows(shard, hc, r) if src is None else src, dst_ref=rows(shard, hc, r),
            send_sem=send_sems.at[k * nc + r], recv_sem=recv_sems.at[k * nc + r], device_id=to, device_id_type=MESH)

    def first(k, p, r):
        return copy(k, me, c, r, (p >> 1, p & 1, c), src=x_ref.at[pl.ds(row0 + c * half + r * ch, ch), :])

    def start():
        for k, p in enumerate(peers):
            for r in range(nc):
                first(k, p, r).start()

    def forward():
        for k, p in enumerate(peers):
            for r in range(nc):
                copy(k, p, c, r, sibling).wait_recv()
                copy(3 + k, p, c, r, sibling).start()

    def finish():
        for k, p in enumerate(peers):
            for r in range(nc):
                copy(3 + k, p, 1 - c, r, sibling).wait_recv()
        for k, p in enumerate(peers):
            for r in range(nc):
                first(k, p, r).wait_send()
                copy(3 + k, p, c, r, sibling).wait_send()

    return start, forward, finish


def _gather_sems(nc):
    return [pltpu.SemaphoreType.DMA((6 * nc,)), pltpu.SemaphoreType.DMA((6 * nc,))]


def _swap_plan(g_ref, out_ref, send_sems, recv_sems):
    half, ch, nc = HALF_LAYER, CHUNK_ROWS, RS_CHUNKS

    def copies():
        x, y, c = _place()
        return [pltpu.make_async_remote_copy(
            src_ref=g_ref.at[j, pl.ds((1 - c) * half + r * ch, ch), :], dst_ref=out_ref.at[j, pl.ds(r * ch, ch), :],
            send_sem=send_sems.at[j * nc + r], recv_sem=recv_sems.at[j * nc + r],
            device_id=(x, y, 1 - c), device_id_type=MESH) for j in range(N_SHARD) for r in range(nc)]

    def start():
        for cp in copies():
            cp.start()

    def finish():
        for cp in copies():
            cp.wait()

    return start, finish


def _scatter_plan(p_ref, out_ref, send_sems, recv_sems):
    ch, nc = CHUNK_ROWS, RS_CHUNKS

    def copies():
        x, y, c = _place()
        me = 2 * x + y
        return [pltpu.make_async_remote_copy(
            src_ref=p_ref.at[me ^ k, pl.ds(r * ch, ch), :], dst_ref=out_ref.at[k - 1, pl.ds(r * ch, ch), :],
            send_sem=send_sems.at[(k - 1) * nc + r], recv_sem=recv_sems.at[(k - 1) * nc + r],
            device_id=((me ^ k) >> 1, (me ^ k) & 1, c), device_id_type=MESH) for k in (1, 2, 3) for r in range(nc)]

    def start():
        for cp in copies():
            cp.start()

    def finish():
        for cp in copies():
            cp.wait()

    return start, finish


SWAP = (_swap_plan, (N_SHARD, HALF_LAYER, D_MODEL), F32, N_SHARD * RS_CHUNKS)
SCATTER = (_scatter_plan, (3, HALF_LAYER, D_MODEL), BF16, 3 * RS_CHUNKS)


def _exchange_call(kind, operand):
    plan, shape, dtype, n_sems = kind

    def body(in_ref, out_ref, send_sems, recv_sems):
        start, finish = plan(in_ref, out_ref, send_sems, recv_sems)
        start()
        finish()

    return pl.pallas_call(
        body, name="exchange", in_specs=[_ANY], out_specs=_ANY, out_shape=_sds(shape, dtype),
        scratch_shapes=[pltpu.SemaphoreType.DMA((n_sems,)), pltpu.SemaphoreType.DMA((n_sems,))],
    )(operand)


def _with_exchange(body, n_in, n_out, exchange, step_of, n_steps):
    def wrapped(*refs):
        ins, src = refs[:n_in], refs[n_in]
        outs, dst = refs[n_in + 1:n_in + 1 + n_out], refs[n_in + 1 + n_out]
        send_sems, recv_sems = refs[n_in + 2 + n_out:n_in + 4 + n_out]
        start, finish = exchange[0][0](src, dst, send_sems, recv_sems)
        pl.when(step_of() == 0)(start)
        body(*ins, *outs, *refs[n_in + 4 + n_out:])
        pl.when(step_of() == n_steps - 1)(finish)

    return wrapped


def _exchange_args(exchange):
    (plan, shape, dtype, n_sems), operand = exchange
    sems = [pltpu.SemaphoreType.DMA((n_sems,)), pltpu.SemaphoreType.DMA((n_sems,))]
    return [_ANY], [_ANY], [_sds(shape, dtype)], sems, [operand]


def _gather_weights(flat, row0, nrows, nc):
    def body(x_ref, out_ref, send_sems, recv_sems):
        start, forward, finish = _gather_plan(x_ref, out_ref, send_sems, recv_sems, row0, nrows, nc)
        start()
        forward()
        finish()

    return pl.pallas_call(
        body, name="gather_weights", in_specs=[_ANY], out_specs=_ANY,
        out_shape=_sds((N_SHARD, ROWS_FLAT, D_MODEL), BF16), scratch_shapes=_gather_sems(nc),
    )(flat)


def _add_my_half(grads, recv):
    tr = 400
    nb = HALF_LAYER // tr
    core = lax.axis_index("c").astype(jnp.int32).reshape(1)

    def body(c_ref, g_ref, r_ref, o_ref, ob_ref):
        acc = g_ref[...] + r_ref[...]
        o_ref[...] = acc
        ob_ref[...] = _bf(acc)

    out = pl.BlockSpec((None, tr, D_MODEL), lambda j, i, c_ref: (j, i, 0))
    return pl.pallas_call(
        body, name="add_my_half",
        grid_spec=pltpu.PrefetchScalarGridSpec(
            num_scalar_prefetch=1, grid=(N_SHARD, nb),
            in_specs=[pl.BlockSpec((None, tr, D_MODEL), lambda j, i, c_ref: (j, c_ref[0] * nb + i, 0)), out],
            out_specs=[out, out]),
        out_shape=[_sds((N_SHARD, HALF_LAYER, D_MODEL)), _sds((N_SHARD, HALF_LAYER, D_MODEL), BF16)],
        compiler_params=_params(("arbitrary", "arbitrary")),
    )(core, grads, recv)


def _sum_scattered(part, recv):
    tr = 400
    chip = (2 * lax.axis_index("x") + lax.axis_index("y")).astype(jnp.int32).reshape(1)

    def body(c_ref, p_ref, r_ref, o_ref):
        acc = p_ref[...]
        for k in range(3):
            acc = acc + r_ref[k].astype(F32)
        o_ref[...] = acc

    return pl.pallas_call(
        body, name="sum_scattered",
        grid_spec=pltpu.PrefetchScalarGridSpec(
            num_scalar_prefetch=1, grid=(HALF_LAYER // tr,),
            in_specs=[pl.BlockSpec((None, tr, D_MODEL), lambda i, c_ref: (c_ref[0], i, 0)),
                      pl.BlockSpec((3, tr, D_MODEL), lambda i, c_ref: (0, i, 0))],
            out_specs=pl.BlockSpec((tr, D_MODEL), lambda i, c_ref: (i, 0))),
        out_shape=_sds((HALF_LAYER, D_MODEL)), compiler_params=_params(("arbitrary",)),
    )(chip, part, recv)


def _swap_reduced(mine):
    ch, nc = CHUNK_ROWS, RS_CHUNKS

    def body(r_ref, out_ref, send_sems, recv_sems):
        x, y, c = _place()
        copies = [pltpu.make_async_remote_copy(
            src_ref=r_ref.at[l, pl.ds(r * ch, ch), :], dst_ref=out_ref.at[l, pl.ds(r * ch, ch), :],
            send_sem=send_sems.at[l * nc + r], recv_sem=recv_sems.at[l * nc + r],
            device_id=(x, y, 1 - c), device_id_type=MESH) for l in range(2) for r in range(nc)]
        for cp in copies:
            cp.start()
        for cp in copies:
            cp.wait()

    return pl.pallas_call(
        body, name="swap_reduced", in_specs=[_ANY], out_specs=_ANY,
        out_shape=_sds((2, HALF_LAYER, D_MODEL)),
        scratch_shapes=[pltpu.SemaphoreType.DMA((2 * nc,)), pltpu.SemaphoreType.DMA((2 * nc,))],
    )(mine)


def _allreduce_small(vec):
    def body(v_ref, out_ref, buf, send_sems, recv_sems):
        x, y, c = _place()
        me = 4 * x + 2 * y + c
        buf[me] = v_ref[...]
        peers = [me ^ k for k in range(1, N_DEV)]
        sends = [pltpu.make_async_remote_copy(
            src_ref=v_ref, dst_ref=buf.at[me], send_sem=send_sems.at[k], recv_sem=recv_sems.at[k],
            device_id=(p >> 2, (p >> 1) & 1, p & 1), device_id_type=MESH) for k, p in enumerate(peers)]
        for cp in sends:
            cp.start()
        for k, p in enumerate(peers):
            pltpu.make_async_remote_copy(
                src_ref=v_ref, dst_ref=buf.at[p], send_sem=send_sems.at[k], recv_sem=recv_sems.at[k],
                device_id=(p >> 2, (p >> 1) & 1, p & 1), device_id_type=MESH).wait_recv()
        for cp in sends:
            cp.wait_send()
        acc = buf[0]
        for d in range(1, N_DEV):
            acc = acc + buf[d]
        out_ref[...] = acc

    vm = pl.BlockSpec(memory_space=pltpu.VMEM)
    return pl.pallas_call(
        body, name="allreduce_small", in_specs=[vm], out_specs=vm, out_shape=_sds((SMALL_ROWS, 128)),
        scratch_shapes=[pltpu.VMEM((N_DEV, SMALL_ROWS, 128), F32),
                        pltpu.SemaphoreType.DMA((N_DEV - 1,)), pltpu.SemaphoreType.DMA((N_DEV - 1,))],
    )(vec)


def _flatten_shard(w_in, w_out, w_pg, w_pp):
    return jnp.concatenate([w_in.reshape(-1, D_MODEL), w_out.reshape(-1, D_MODEL), w_pg.reshape(-1, D_MODEL),
                            w_pp.reshape(-1, D_MODEL)], axis=0)


def _unflatten_layers(flats):
    a, b, c = D_MODEL, D_MODEL + D_MODEL // N_SHARD, D_MODEL + 2 * (D_MODEL // N_SHARD)
    q = D_MODEL // N_SHARD
    return (jnp.stack([f[:a] for f in flats]), jnp.stack([f[a:b] for f in flats]),
            jnp.stack([f[b:c] for f in flats]), jnp.stack([f[c:].reshape(D_PLE, q) for f in flats]))


def _full_w_pp(gathered):
    c = ROWS_W_IN + ROWS_W_OUT + ROWS_W_PG
    q = D_MODEL // N_SHARD
    rpp = ROWS_W_PP // 2
    return [gathered[:, c + l * rpp:c + (l + 1) * rpp, :].reshape(N_SHARD, D_PLE, q).transpose(1, 0, 2)
            .reshape(D_PLE, D_MODEL) for l in range(2)]


def _layer_grads(dw_in, dw_out, dw_pg, dw_pp):
    q = D_MODEL // N_SHARD
    rpp = ROWS_W_PP // 2
    return jnp.concatenate([dw_in, dw_out.reshape(N_SHARD, q, D_MODEL), dw_pg.reshape(N_SHARD, q, D_MODEL),
                            dw_pp.reshape(D_PLE, N_SHARD, q).transpose(1, 0, 2).reshape(N_SHARD, rpp, D_MODEL)], axis=1)


def _lower_bounds(lb_logits):
    sm = jax.nn.softmax(lb_logits.astype(F32), axis=0)
    return jnp.cumsum(sm, axis=0) - sm[0:1]


def kernel(x, p, norm_mix, w_in, a_out_norm, b_out_norm, w_out, lb_logits, ple_gate_norm, w_ple_gate, w_ple_proj, ple_post_norm, final_norm, loss_target, m_norm_mix, m_w_in, m_a_out_norm, m_b_out_norm, m_w_out, m_lb_logits, m_ple_gate_norm, m_w_ple_gate, m_w_ple_proj, m_ple_post_norm, m_final_norm, v_norm_mix, v_w_in, v_a_out_norm, v_b_out_norm, v_w_out, v_lb_logits, v_ple_gate_norm, v_w_ple_gate, v_w_ple_proj, v_ple_post_norm, v_final_norm):
    t = x.shape[1]
    h0 = x.reshape(t, D_MODEL)
    target = loss_target.reshape(t, D_MODEL)
    pl_in = p.reshape(2, t, D_PLE)

    w_flat_bf = _flatten_shard(_bf(w_in), _bf(w_out), _bf(w_ple_gate), _bf(w_ple_proj))
    chip = 2 * lax.axis_index("x") + lax.axis_index("y")
    gathered = lax.dynamic_update_slice(_gather_weights(w_flat_bf, 0, D_MODEL, 2), w_flat_bf[None], (chip, 0, 0))
    lbs, lbs_vjp = jax.vjp(_lower_bounds, lb_logits)

    saved = []
    h = h0
    for l in range(2):
        g_mix = norm_mix[l].reshape(1, D_MODEL)
        lb = lbs[l].reshape(1, GROUP)
        ga = a_out_norm[l].reshape(1, GROUP)
        gb = b_out_norm[l].reshape(1, GROUP)
        proj, qkv, bgate = _inproj(h, g_mix, gathered, l)
        ya, states = _hgrn_fwd(proj, lb, ga)
        if l == 0:
            ob, yb, cars, gathered = _sb_fwd(qkv, bgate, gb, (w_flat_bf, gathered, D_MODEL, ROWS_FLAT - D_MODEL, 4))
            w_pps = _full_w_pp(gathered)
        else:
            ob, yb, cars = _sb_fwd(qkv, bgate, gb)
        h1 = _outproj(h, ya, yb, gathered, l)
        g_post = ple_post_norm[l].reshape(1, D_MODEL)
        g_gate = ple_gate_norm[l].reshape(1, D_MODEL)
        h2 = _ple_fwd(h1, pl_in[l], w_pps[l], gathered, l, g_post, g_gate)
        saved.append((h, proj, qkv, bgate, states, ya, yb, ob, cars, h1))
        h = h2

    dh, d_final, loss_part = _final(h, final_norm.reshape(1, D_MODEL), target)

    g_layer, chip_sum, scattered = [None] * 2, [None] * 2, [None] * 2
    d_mix, d_a, d_b, d_lb, d_gate, d_post = [None] * 2, [None] * 2, [None] * 2, [None] * 2, [None] * 2, [None] * 2
    for l in (1, 0):
        h_in, proj, qkv, bgate, states, ya, yb, ob, cars, h1 = saved[l]
        g_mix = norm_mix[l].reshape(1, D_MODEL)
        lb = lbs[l].reshape(1, GROUP)
        ga = a_out_norm[l].reshape(1, GROUP)
        gb = b_out_norm[l].reshape(1, GROUP)
        g_post = ple_post_norm[l].reshape(1, D_MODEL)
        g_gate = ple_gate_norm[l].reshape(1, D_MODEL)
        if l == 1:
            dh1, dw_pg, dw_pp, d_gate[l], d_post[l] = _ple_bwd(dh, h1, pl_in[l], w_pps[l], gathered, l, g_post, g_gate)
            dya, dyb, dw_out = _outproj_bwd(dh1, ya, yb, gathered, l)
            dbq, dbk, dbv, dbg, d_b[l] = _sb_bwd(qkv, bgate, ob, cars, dyb, gb)
        else:
            dh1, dw_pg, dw_pp, d_gate[l], d_post[l], from_sibling = _ple_bwd(
                dh, h1, pl_in[l], w_pps[l], gathered, l, g_post, g_gate, (SWAP, g_layer[1]))
            chip_sum[1], chip_sum_bf = _add_my_half(g_layer[1], from_sibling)
            dya, dyb, dw_out = _outproj_bwd(dh1, ya, yb, gathered, l)
            dbq, dbk, dbv, dbg, d_b[l], scattered[1] = _sb_bwd(qkv, bgate, ob, cars, dyb, gb, (SCATTER, chip_sum_bf))
        da, d_lb[l], d_a[l] = _hgrn_bwd(proj, lb, ga, states, dya)
        db = jnp.stack([dbq, dbk * LN2, dbv, dbg]).astype(BF16)
        g_layer[l] = _layer_grads(_inproj_bwd_dw(h_in, g_mix, da, db), dw_out, dw_pg, dw_pp)
        if l == 1:
            dh, d_mix[l] = _inproj_bwd_dx(dh1, h_in, g_mix, gathered, l, da, db)
        else:
            dh, d_mix[l], from_sibling = _inproj_bwd_dx(dh1, h_in, g_mix, gathered, l, da, db, (SWAP, g_layer[0]))
    grad_x = dh.reshape(x.shape)

    chip_sum[0], chip_sum_bf = _add_my_half(g_layer[0], from_sibling)
    scattered[0] = _exchange_call(SCATTER, chip_sum_bf)
    mine = jnp.stack([_sum_scattered(chip_sum[l], scattered[l]) for l in range(2)])
    other = _swap_reduced(mine)
    south = lax.axis_index("c") == 0
    g_w_in, g_w_out, g_w_pg, g_w_pp = _unflatten_layers(
        [jnp.concatenate([jnp.where(south, mine[l], other[l]), jnp.where(south, other[l], mine[l])]) for l in range(2)])

    small = jnp.concatenate([
        jnp.concatenate(d_mix).reshape(-1, 128), jnp.concatenate(d_a).reshape(-1, 128),
        jnp.concatenate(d_b).reshape(-1, 128), jnp.concatenate(d_lb).reshape(-1, 128),
        jnp.concatenate(d_gate).reshape(-1, 128), jnp.concatenate(d_post).reshape(-1, 128),
        d_final.reshape(-1, 128), jnp.broadcast_to(loss_part, (8, 128))], axis=0)
    small = _allreduce_small(small)
    loss = small[80, 0]
    g_norm_mix = small[0:16].reshape(2, D_MODEL)
    g_a = small[16:24].reshape(2, GROUP)
    g_b = small[24:32].reshape(2, GROUP)
    (g_lb,) = lbs_vjp(small[32:40].reshape(2, GROUP))
    g_gate = small[40:56].reshape(2, D_MODEL)
    g_post = small[56:72].reshape(2, D_MODEL)
    g_final = small[72:80].reshape(D_MODEL)

    def adam_matrix(w, g, m, v):
        d, nm, nv = _adamw(w.reshape(-1, D_MODEL), g.reshape(-1, D_MODEL), m.reshape(-1, D_MODEL), v.reshape(-1, D_MODEL))
        return d.reshape(w.shape), nm.reshape(w.shape), nv.reshape(w.shape)

    d_w_in, nm_w_in, nv_w_in = adam_matrix(w_in, g_w_in, m_w_in, v_w_in)
    d_w_out, nm_w_out, nv_w_out = adam_matrix(w_out, g_w_out, m_w_out, v_w_out)
    d_w_pg, nm_w_pg, nv_w_pg = adam_matrix(w_ple_gate, g_w_pg, m_w_ple_gate, v_w_ple_gate)
    d_w_pp, nm_w_pp, nv_w_pp = adam_matrix(w_ple_proj, g_w_pp, m_w_ple_proj, v_w_ple_proj)

    small_w = [norm_mix, a_out_norm, b_out_norm, lb_logits, ple_gate_norm, ple_post_norm, final_norm]
    small_g = [g_norm_mix, g_a, g_b, g_lb, g_gate, g_post, g_final]
    small_m = [m_norm_mix, m_a_out_norm, m_b_out_norm, m_lb_logits, m_ple_gate_norm, m_ple_post_norm, m_final_norm]
    small_v = [v_norm_mix, v_a_out_norm, v_b_out_norm, v_lb_logits, v_ple_gate_norm, v_ple_post_norm, v_final_norm]
    pack = lambda arrs: jnp.concatenate([a.reshape(-1, 128) for a in arrs], axis=0)
    ds, nms, nvs = _adamw(pack(small_w), pack(small_g), pack(small_m), pack(small_v))

    def unpack(packed):
        out, r = [], 0
        for a in small_w:
            n = a.size // 128
            out.append(packed[r:r + n].reshape(a.shape))
            r += n
        return out

    d_s, nm_s, nv_s = unpack(ds), unpack(nms), unpack(nvs)

    def ordered(s, big):
        return [s[0], big[0], s[1], s[2], big[1], s[3], s[4], big[2], big[3], s[5], s[6]]

    grads = ordered(small_g, [g_w_in, g_w_out, g_w_pg, g_w_pp])
    deltas = ordered(d_s, [d_w_in, d_w_out, d_w_pg, d_w_pp])
    new_m = ordered(nm_s, [nm_w_in, nm_w_out, nm_w_pg, nm_w_pp])
    new_v = ordered(nv_s, [nv_w_in, nv_w_out, nv_w_pg, nv_w_pp])
    return (loss, grad_x, *grads, *deltas, *new_m, *new_v)
```

```python
import functools
import math

import numpy as np
import jax
import jax.numpy as jnp
from jax import lax
from jax.experimental import pallas as pl
from jax.experimental.pallas import tpu as pltpu

F32 = jnp.float32
BF16 = jnp.bfloat16
MESH = pl.DeviceIdType.MESH

D_MODEL = 1024
D_PLE = 256
D_IN = 4096
A_HEADS, A_D = 4, 128
B_HEADS, B_D = 8, 64
GROUP = 512
EPS = 1e-6
N_SHARD = 4
N_DEV = 8

HG_CHUNK = 128
HG_LEVELS = 7
SB_TQ = 1024
SB_TK = 128
SB_GROUP_FWD, SB_GROUP_BWD = 4, 8
LOG2E = 1.4426950408889634
LN2 = 0.6931471805599453

ADAM_LR, ADAM_B1, ADAM_B2, ADAM_EPS, ADAM_WD, ADAM_STEP = 0.001, 0.9, 0.999, 1e-08, 0.01, 10

VMEM_LIMIT = 48 * 1024 * 1024
VMEM_LIMIT_BIG = 58 * 1024 * 1024

ROWS_W_IN = 2 * D_MODEL
ROWS_W_OUT = 2 * (D_MODEL // N_SHARD)
ROWS_W_PG = 2 * (D_MODEL // N_SHARD)
ROWS_W_PP = 2 * (D_PLE * (D_MODEL // N_SHARD) // D_MODEL)
ROWS_FLAT = ROWS_W_IN + ROWS_W_OUT + ROWS_W_PG + ROWS_W_PP
HALF_FLAT = ROWS_FLAT // 2
N_CHUNK = 10
CHUNK_ROWS = HALF_FLAT // N_CHUNK

ROWS_LAYER = ROWS_FLAT // 2
HALF_LAYER = ROWS_LAYER // 2
RS_CHUNKS = HALF_LAYER // CHUNK_ROWS

SMALL_ROWS = 88


def _sds(shape, dtype=F32):
    return jax.ShapeDtypeStruct(shape, dtype)


def _params(sem=None, vmem_limit=VMEM_LIMIT):
    kw = dict(vmem_limit_bytes=vmem_limit)
    if sem is not None:
        kw["dimension_semantics"] = sem
    return pltpu.CompilerParams(**kw)


def _dot(a, b, precision=None):
    return lax.dot_general(a, b, (((1,), (0,)), ((), ())), preferred_element_type=F32, precision=precision)


def _dot_nt(a, b, precision=None):
    return lax.dot_general(a, b, (((1,), (1,)), ((), ())), preferred_element_type=F32, precision=precision)


def _dot_tn(a, b, precision=None):
    return lax.dot_general(a, b, (((0,), (0,)), ((), ())), preferred_element_type=F32, precision=precision)


def _bf(x):
    return x.astype(BF16)


def _split(x):
    hi = x.astype(BF16)
    lo = (x - hi.astype(F32)).astype(BF16)
    return hi, lo


def _rms(x):
    r = lax.rsqrt(jnp.mean(x * x, axis=-1, keepdims=True) + EPS)
    return x * r, r


def _rms_bwd(dxh, xh, r):
    return r * (dxh - xh * jnp.mean(dxh * xh, axis=-1, keepdims=True))


def _sigmoid(x):
    return 1.0 / (1.0 + jnp.exp(-x))


def _silu_grad(x, sig):
    return sig * (1.0 + x * (1.0 - sig))


def _row_tile(t, want):
    return min(t, want)


def _inproj(h, g, gathered, layer):
    t = h.shape[0]
    tm = _row_tile(t, 512)

    def body(h_ref, g_ref, w_ref, pa_ref, qkv_ref, bg_ref):
        xh, _ = _rms(h_ref[...])
        u = _bf(xh * g_ref[...])
        for j in range(8):
            acc = _dot(u, w_ref[j // 2, :, pl.ds((j % 2) * GROUP, GROUP)])
            if j < 4:
                pa_ref[:, pl.ds(j * GROUP, GROUP)] = acc
            elif j == 4:
                qkv_ref[:, pl.ds(0, GROUP)] = _bf(acc * (B_D ** -0.5 * LOG2E))
            elif j < 7:
                qkv_ref[:, pl.ds((j - 4) * GROUP, GROUP)] = _bf(acc)
            else:
                bg_ref[...] = acc

    return pl.pallas_call(
        body, name="inproj", grid=(t // tm,),
        in_specs=[pl.BlockSpec((tm, D_MODEL), lambda i: (i, 0)),
                  pl.BlockSpec((1, D_MODEL), lambda i: (0, 0)),
                  pl.BlockSpec((N_SHARD, D_MODEL, D_MODEL), lambda i: (0, layer, 0))],
        out_specs=[pl.BlockSpec((tm, 4 * GROUP), lambda i: (i, 0)), pl.BlockSpec((tm, 3 * GROUP), lambda i: (i, 0)),
                   pl.BlockSpec((tm, GROUP), lambda i: (i, 0))],
        out_shape=[_sds((t, 4 * GROUP)), _sds((t, 3 * GROUP), BF16), _sds((t, GROUP))],
        compiler_params=_params(("arbitrary",)),
    )(h, g, gathered)


def _rows_spec(first_row):
    q = D_MODEL // N_SHARD
    return pl.BlockSpec((N_SHARD, q, D_MODEL), lambda i: (0, first_row // q, 0))


def _outproj(h, ya, yb, gathered, layer):
    t = h.shape[0]
    tm = _row_tile(t, 512)

    def body(h_ref, ya_ref, yb_ref, w_ref, o_ref):
        o_ref[...] = (h_ref[...] + _dot(_bf(ya_ref[...]), w_ref[0:2].reshape(GROUP, D_MODEL))
                      + _dot(_bf(yb_ref[...]), w_ref[2:4].reshape(GROUP, D_MODEL)))

    return pl.pallas_call(
        body, name="outproj", grid=(t // tm,),
        in_specs=[pl.BlockSpec((tm, D_MODEL), lambda i: (i, 0)),
                  pl.BlockSpec((tm, GROUP), lambda i: (i, 0)),
                  pl.BlockSpec((tm, GROUP), lambda i: (i, 0)),
                  _rows_spec(ROWS_W_IN + layer * (D_MODEL // N_SHARD))],
        out_specs=pl.BlockSpec((tm, D_MODEL), lambda i: (i, 0)),
        out_shape=_sds((t, D_MODEL)), compiler_params=_params(("arbitrary",)),
    )(h, ya, yb, gathered)


def _ple_mix(x, p_ref, wpp_ref, wpg_ref, gp_ref, gg_ref):
    ph, _ = _rms(_dot(_bf(p_ref[...]), wpp_ref[...]))
    xh, _ = _rms(x)
    gate = _sigmoid(_dot(_bf(xh * gg_ref[...]), wpg_ref[...].reshape(D_MODEL, D_MODEL)))
    return x + gate * (ph * gp_ref[...])


def _ple_specs(tm, layer):
    return [pl.BlockSpec((tm, D_MODEL), lambda i: (i, 0)),
            pl.BlockSpec((tm, D_PLE), lambda i: (i, 0)),
            pl.BlockSpec((D_PLE, D_MODEL), lambda i: (0, 0)),
            _rows_spec(ROWS_W_IN + ROWS_W_OUT + layer * (D_MODEL // N_SHARD)),
            pl.BlockSpec((1, D_MODEL), lambda i: (0, 0)),
            pl.BlockSpec((1, D_MODEL), lambda i: (0, 0))]


def _ple_fwd(h, p, w_pp, gathered, layer, g_post, g_gate):
    t = h.shape[0]
    tm = _row_tile(t, 256)

    def body(h_ref, p_ref, wpp_ref, wpg_ref, gp_ref, gg_ref, o_ref):
        o_ref[...] = _ple_mix(h_ref[...], p_ref, wpp_ref, wpg_ref, gp_ref, gg_ref)

    return pl.pallas_call(
        body, name="ple_fwd", grid=(t // tm,), in_specs=_ple_specs(tm, layer),
        out_specs=pl.BlockSpec((tm, D_MODEL), lambda i: (i, 0)),
        out_shape=_sds((t, D_MODEL)), compiler_params=_params(("arbitrary",)),
    )(h, p, w_pp, gathered, g_post, g_gate)


def _ple_fwd_final(h, p, w_pp, gathered, layer, g_post, g_gate, g_final, target):
    t = h.shape[0]
    tm = _row_tile(t, 256)

    def body(h_ref, p_ref, wpp_ref, wpg_ref, gp_ref, gg_ref, gf_ref, t_ref, dh_ref, dg_ref, loss_ref):
        @pl.when(pl.program_id(0) == 0)
        def _():
            dg_ref[...] = jnp.zeros_like(dg_ref)
            loss_ref[...] = jnp.zeros_like(loss_ref)

        xh, r = _rms(_ple_mix(h_ref[...], p_ref, wpp_ref, wpg_ref, gp_ref, gg_ref))
        gf = gf_ref[...]
        err = xh * gf - t_ref[...]
        part = 0.5 * jnp.sum(jnp.mean(err * err, axis=-1, keepdims=True), axis=0, keepdims=True)
        loss_ref[...] += jnp.broadcast_to(part, loss_ref.shape)
        dy = err * (1.0 / D_MODEL)
        dg_ref[...] += jnp.sum(dy * xh, axis=0, keepdims=True)
        dh_ref[...] = _rms_bwd(dy * gf, xh, r)

    return pl.pallas_call(
        body, name="ple_fwd_final", grid=(t // tm,),
        in_specs=_ple_specs(tm, layer) + [pl.BlockSpec((1, D_MODEL), lambda i: (0, 0)),
                                          pl.BlockSpec((tm, D_MODEL), lambda i: (i, 0))],
        out_specs=[pl.BlockSpec((tm, D_MODEL), lambda i: (i, 0)),
                   pl.BlockSpec((1, D_MODEL), lambda i: (0, 0)),
                   pl.BlockSpec((1, 128), lambda i: (0, 0))],
        out_shape=[_sds((t, D_MODEL)), _sds((1, D_MODEL)), _sds((1, 128))],
        compiler_params=_params(("arbitrary",)),
    )(h, p, w_pp, gathered, g_post, g_gate, g_final, target)


def _ple_bwd(dh2, h, p, w_pp, gathered, layer, g_post, g_gate, exchange=None):
    t = h.shape[0]
    tm = _row_tile(t, 256)

    def body(d_ref, h_ref, p_ref, wpp_ref, wpg_ref, gp_ref, gg_ref, dh_ref, dwpg_ref, dwpp_ref, dgg_ref, dgp_ref):
        @pl.when(pl.program_id(0) == 0)
        def _():
            dwpg_ref[...] = jnp.zeros_like(dwpg_ref)
            dwpp_ref[...] = jnp.zeros_like(dwpp_ref)
            dgg_ref[...] = jnp.zeros_like(dgg_ref)
            dgp_ref[...] = jnp.zeros_like(dgp_ref)

        d = d_ref[...]
        x = h_ref[...]
        gp = gp_ref[...]
        gg = gg_ref[...]
        pb = _bf(p_ref[...])
        ph, rp = _rms(_dot(pb, wpp_ref[...]))
        pe = ph * gp
        xh, rx = _rms(x)
        un = _bf(xh * gg)
        wpg = wpg_ref[...].reshape(D_MODEL, D_MODEL)
        gate = _sigmoid(_dot(un, wpg))
        dgpre = _bf(d * pe * gate * (1.0 - gate))
        dun = _dot_nt(dgpre, wpg)
        dh_ref[...] = d + _rms_bwd(dun * gg, xh, rx)
        dgg_ref[...] += jnp.sum(dun * xh, axis=0, keepdims=True)
        dwpg_ref[...] += _dot_tn(un, dgpre)
        dpe = d * gate
        dgp_ref[...] += jnp.sum(dpe * ph, axis=0, keepdims=True)
        dwpp_ref[...] += _dot_tn(pb, _bf(_rms_bwd(dpe * gp, ph, rp)))

    in_specs = [pl.BlockSpec((tm, D_MODEL), lambda i: (i, 0)),
                pl.BlockSpec((tm, D_MODEL), lambda i: (i, 0)),
                pl.BlockSpec((tm, D_PLE), lambda i: (i, 0)),
                pl.BlockSpec((D_PLE, D_MODEL), lambda i: (0, 0)),
                _rows_spec(ROWS_W_IN + ROWS_W_OUT + layer * (D_MODEL // N_SHARD)),
                pl.BlockSpec((1, D_MODEL), lambda i: (0, 0)),
                pl.BlockSpec((1, D_MODEL), lambda i: (0, 0))]
    out_specs = [pl.BlockSpec((tm, D_MODEL), lambda i: (i, 0)),
                 pl.BlockSpec((D_MODEL, D_MODEL), lambda i: (0, 0)),
                 pl.BlockSpec((D_PLE, D_MODEL), lambda i: (0, 0)),
                 pl.BlockSpec((1, D_MODEL), lambda i: (0, 0)),
                 pl.BlockSpec((1, D_MODEL), lambda i: (0, 0))]
    out_shape = [_sds((t, D_MODEL)), _sds((D_MODEL, D_MODEL)), _sds((D_PLE, D_MODEL)),
                 _sds((1, D_MODEL)), _sds((1, D_MODEL))]
    operands = [dh2, h, p, w_pp, gathered, g_post, g_gate]
    scratch = []
    if exchange is not None:
        body = _with_exchange(body, 7, 5, exchange, lambda: pl.program_id(0), t // tm)
        xi, xo, xs, scratch, xop = _exchange_args(exchange)
        in_specs, out_specs, out_shape, operands = in_specs + xi, out_specs + xo, out_shape + xs, operands + xop
    return pl.pallas_call(
        body, name="ple_bwd" if exchange is None else "ple_bwd_exchange", grid=(t // tm,),
        in_specs=in_specs, out_specs=out_specs, out_shape=out_shape, scratch_shapes=scratch,
        compiler_params=_params(("arbitrary",)),
    )(*operands)


def _outproj_bwd(dh, ya, yb, gathered, layer):
    t = dh.shape[0]
    tm = _row_tile(t, 512)

    def body(d_ref, ya_ref, yb_ref, w_ref, dya_ref, dyb_ref, dw_ref):
        @pl.when(pl.program_id(0) == 0)
        def _():
            dw_ref[...] = jnp.zeros_like(dw_ref)

        d = _bf(d_ref[...])
        dya_ref[...] = _dot_nt(d, w_ref[0:2].reshape(GROUP, D_MODEL))
        dyb_ref[...] = _dot_nt(d, w_ref[2:4].reshape(GROUP, D_MODEL))
        dw_ref[pl.ds(0, GROUP), :] += _dot_tn(_bf(ya_ref[...]), d)
        dw_ref[pl.ds(GROUP, GROUP), :] += _dot_tn(_bf(yb_ref[...]), d)

    return pl.pallas_call(
        body, name="outproj_bwd", grid=(t // tm,),
        in_specs=[pl.BlockSpec((tm, D_MODEL), lambda i: (i, 0)),
                  pl.BlockSpec((tm, GROUP), lambda i: (i, 0)),
                  pl.BlockSpec((tm, GROUP), lambda i: (i, 0)),
                  _rows_spec(ROWS_W_IN + layer * (D_MODEL // N_SHARD))],
        out_specs=[pl.BlockSpec((tm, GROUP), lambda i: (i, 0)),
                   pl.BlockSpec((tm, GROUP), lambda i: (i, 0)),
                   pl.BlockSpec((D_MODEL, D_MODEL), lambda i: (0, 0))],
        out_shape=[_sds((t, GROUP)), _sds((t, GROUP)), _sds((D_MODEL, D_MODEL))],
        compiler_params=_params(("arbitrary",)),
    )(dh, ya, yb, gathered)


def _inproj_bwd_dx(dres, h, g, gathered, layer, da, db, exchange=None):
    t = h.shape[0]
    tm = _row_tile(t, 256)

    def body(dres_ref, h_ref, g_ref, w_ref, da_ref, db_ref, dh_ref, dg_ref):
        @pl.when(pl.program_id(0) == 0)
        def _():
            dg_ref[...] = jnp.zeros_like(dg_ref)

        du = jnp.zeros((tm, D_MODEL), F32)
        for i in range(8):
            part = da_ref[i] if i < 4 else db_ref[i - 4]
            du = du + _dot_nt(part, w_ref[i // 2, :, pl.ds((i % 2) * GROUP, GROUP)])
        xh, r = _rms(h_ref[...])
        dg_ref[...] += jnp.sum(du * xh, axis=0, keepdims=True)
        dh_ref[...] = dres_ref[...] + _rms_bwd(du * g_ref[...], xh, r)

    in_specs = [pl.BlockSpec((tm, D_MODEL), lambda i: (i, 0)),
                pl.BlockSpec((tm, D_MODEL), lambda i: (i, 0)),
                pl.BlockSpec((1, D_MODEL), lambda i: (0, 0)),
                pl.BlockSpec((N_SHARD, D_MODEL, D_MODEL), lambda i: (0, layer, 0)),
                pl.BlockSpec((4, tm, GROUP), lambda i: (0, i, 0)),
                pl.BlockSpec((4, tm, GROUP), lambda i: (0, i, 0))]
    out_specs = [pl.BlockSpec((tm, D_MODEL), lambda i: (i, 0)), pl.BlockSpec((1, D_MODEL), lambda i: (0, 0))]
    out_shape = [_sds((t, D_MODEL)), _sds((1, D_MODEL))]
    operands = [dres, h, g, gathered, da, db]
    scratch = []
    if exchange is not None:
        body = _with_exchange(body, 6, 2, exchange, lambda: pl.program_id(0), t // tm)
        xi, xo, xs, scratch, xop = _exchange_args(exchange)
        in_specs, out_specs, out_shape, operands = in_specs + xi, out_specs + xo, out_shape + xs, operands + xop
    return pl.pallas_call(
        body, name="inproj_bwd_dx" if exchange is None else "inproj_bwd_dx_exchange", grid=(t // tm,),
        in_specs=in_specs, out_specs=out_specs, out_shape=out_shape, scratch_shapes=scratch,
        compiler_params=_params(("arbitrary",)),
    )(*operands)


def _inproj_bwd_dw(h, g, da, db):
    t = h.shape[0]
    tm = _row_tile(t, 512)

    def body(h_ref, g_ref, da_ref, db_ref, dw_ref):
        @pl.when(pl.program_id(0) == 0)
        def _():
            dw_ref[...] = jnp.zeros_like(dw_ref)

        xh, _ = _rms(h_ref[...])
        u = _bf(xh * g_ref[...])
        for i in range(8):
            dw_ref[i // 2, :, pl.ds((i % 2) * GROUP, GROUP)] += _dot_tn(u, da_ref[i] if i < 4 else db_ref[i - 4])

    return pl.pallas_call(
        body, name="inproj_bwd_dw", grid=(t // tm,),
        in_specs=[pl.BlockSpec((tm, D_MODEL), lambda i: (i, 0)),
                  pl.BlockSpec((1, D_MODEL), lambda i: (0, 0)),
                  pl.BlockSpec((4, tm, GROUP), lambda i: (0, i, 0)),
                  pl.BlockSpec((4, tm, GROUP), lambda i: (0, i, 0))],
        out_specs=pl.BlockSpec((N_SHARD, D_MODEL, D_MODEL), lambda i: (0, 0, 0)),
        out_shape=_sds((N_SHARD, ROWS_LAYER, D_MODEL)), compiler_params=_params(("arbitrary",), VMEM_LIMIT_BIG),
    )(h, g, da, db)


def _hgrn_consts():
    c, nl = HG_CHUNK, HG_LEVELS
    t = np.arange(c)
    tril = np.tril(np.ones((c, c), np.float32))
    masks = np.zeros((nl + 1, c, c), np.float32)
    masks[0] = np.eye(c, dtype=np.float32)
    dmat = np.zeros(((nl + 2) * c, c), np.float32)
    dmat[0:c] = tril
    for l in range(nl):
        m = c >> (l + 1)
        blk = t // (2 * m)
        r = blk * 2 * m + m - 1
        upper = (t % (2 * m)) >= m
        masks[l + 1] = ((blk[:, None] == blk[None, :]) & upper[:, None] & (~upper)[None, :]).astype(np.float32)
        dmat[(l + 1) * c:(l + 2) * c] = tril[t] - tril[r]
    dmat[(nl + 1) * c:] = np.triu(np.ones((c, c), np.float32), k=1)
    return jnp.asarray(masks), jnp.asarray(dmat, BF16)


HG_HEADS = 4


def _hgrn_pre(aq, af, lb):
    sq = _sigmoid(aq)
    sneg = _sigmoid(-af)
    kk = (1.0 - lb) * sneg
    return sq, aq * sq, sneg, kk, jnp.log1p(-kk)


def _hgrn_x(logf, dmat_ref):
    dm = dmat_ref[pl.ds(0, (HG_LEVELS + 1) * HG_CHUNK), :]
    lhi, llo = _split(logf)
    return _dot(dm, lhi) + _dot(dm, llo)


def _hgrn_level(x_all, l, q, kk):
    c = HG_CHUNK
    x = x_all[(l + 1) * c:(l + 2) * c]
    qf = jnp.exp(jnp.minimum(x, 0.0))
    kf = jnp.exp(-jnp.maximum(x, 0.0))
    return qf, kf, _bf(q * qf), _bf(kk * kf)


def _hgrn_scores(xs, qs, kks, mask_ref):
    ps = [mask_ref[0] * _dot_nt(_bf(q), _bf(kk)) for q, kk in zip(qs, kks)]
    for l in range(HG_LEVELS):
        for i, (x_all, q, kk) in enumerate(zip(xs, qs, kks)):
            _, _, ql, kl = _hgrn_level(x_all, l, q, kk)
            ps[i] = ps[i] + mask_ref[l + 1] * _dot_nt(ql, kl)
    return ps


def _hgrn_specs(n_chunks, rev):
    c, w = HG_CHUNK, HG_HEADS * A_D
    cidx = (lambda n: n_chunks - 1 - n) if rev else (lambda n: n)
    col = lambda g: pl.BlockSpec((c, w), lambda h, n: (cidx(n), g * (A_HEADS // HG_HEADS) + h))
    vec = pl.BlockSpec((1, w), lambda h, n: (0, h))
    mask = pl.BlockSpec((HG_LEVELS + 1, c, c), lambda h, n: (0, 0, 0))
    dmat = pl.BlockSpec(((HG_LEVELS + 2) * c, c), lambda h, n: (0, 0))
    state = pl.BlockSpec((HG_HEADS, None, A_D, A_D), lambda h, n: (h, cidx(n), 0, 0))
    return cidx, col, vec, mask, dmat, state


def _lanes(i):
    return pl.ds(i * A_D, A_D)


def _hgrn_fwd(proj, lb, gain):
    t = proj.shape[0]
    c = HG_CHUNK
    nch = t // c
    masks, dmat = _hgrn_consts()
    cidx, col, vec, mask_spec, dmat_spec, state_spec = _hgrn_specs(nch, False)
    heads = range(HG_HEADS)

    def body(aq_ref, af_ref, ai_ref, ag_ref, lb_ref, gain_ref, mask_ref, dmat_ref, y_ref, st_ref, s_scr):
        @pl.when(pl.program_id(1) == 0)
        def _():
            s_scr[...] = jnp.zeros_like(s_scr)

        pre = [_hgrn_pre(aq_ref[:, _lanes(i)], af_ref[:, _lanes(i)], lb_ref[:, _lanes(i)]) for i in heads]
        qs, kks = [p[1] for p in pre], [p[3] for p in pre]
        xs = [_hgrn_x(p[4], dmat_ref) for p in pre]
        bs = [x[0:c] for x in xs]
        b_lasts = [jnp.sum(p[4], axis=0, keepdims=True) for p in pre]
        ps = _hgrn_scores(xs, qs, kks, mask_ref)
        ss = [s_scr[i] for i in heads]
        vbs = [_bf(ai_ref[:, _lanes(i)]) for i in heads]
        os_ = [_dot(_bf(ps[i]), vbs[i]) + _dot_nt(_bf(qs[i] * jnp.exp(bs[i])), _bf(ss[i])) for i in heads]
        for i in heads:
            st_ref[i] = ss[i]
            s_scr[i] = ss[i] * jnp.exp(b_lasts[i]) + _dot_tn(vbs[i], _bf(kks[i] * jnp.exp(b_lasts[i] - bs[i])))
            oh, _ = _rms(os_[i])
            ag = ag_ref[:, _lanes(i)]
            y_ref[:, _lanes(i)] = oh * gain_ref[:, _lanes(i)] * (ag * _sigmoid(ag))

    return pl.pallas_call(
        body, name="hgrn_fwd", grid=(A_HEADS // HG_HEADS, nch),
        in_specs=[col(0), col(1), col(2), col(3), vec, vec, mask_spec, dmat_spec],
        out_specs=[pl.BlockSpec((c, HG_HEADS * A_D), lambda h, n: (n, h)), state_spec],
        out_shape=[_sds((t, GROUP)), _sds((A_HEADS, nch, A_D, A_D))],
        scratch_shapes=[pltpu.VMEM((HG_HEADS, A_D, A_D), F32)],
        compiler_params=_params(("arbitrary", "arbitrary")),
    )(proj, proj, proj, proj, lb, gain, masks, dmat)


def _hgrn_bwd(proj, lb, gain, states, dya):
    t = proj.shape[0]
    c, nl = HG_CHUNK, HG_LEVELS
    nch = t // c
    masks, dmat = _hgrn_consts()
    cidx, col, vec, mask_spec, dmat_spec, state_spec = _hgrn_specs(nch, True)
    heads = range(HG_HEADS)

    def body(aq_ref, af_ref, ai_ref, ag_ref, lb_ref, gain_ref, mask_ref, dmat_ref, st_ref, dy_ref,
             da_ref, dlb_ref, dgain_ref, ds_scr, z_scr):
        @pl.when(pl.program_id(1) == 0)
        def _():
            ds_scr[...] = jnp.zeros_like(ds_scr)
            dlb_ref[...] = jnp.zeros_like(dlb_ref)
            dgain_ref[...] = jnp.zeros_like(dgain_ref)

        aqs = [aq_ref[:, _lanes(i)] for i in heads]
        lbs = [lb_ref[:, _lanes(i)] for i in heads]
        pre = [_hgrn_pre(aqs[i], af_ref[:, _lanes(i)], lbs[i]) for i in heads]
        sqs, qs, snegs, kks = ([p[j] for p in pre] for j in range(4))
        xs = [_hgrn_x(p[4], dmat_ref) for p in pre]
        bs = [x[0:c] for x in xs]
        b_lasts = [jnp.sum(p[4], axis=0, keepdims=True) for p in pre]
        ebs = [jnp.exp(b) for b in bs]
        ebls = [jnp.exp(bl - b) for bl, b in zip(b_lasts, bs)]
        ebl_rows = [jnp.exp(bl) for bl in b_lasts]
        qes = [_bf(q * eb) for q, eb in zip(qs, ebs)]
        kes = [_bf(kk * ebl) for kk, ebl in zip(kks, ebls)]
        vbs = [_bf(ai_ref[:, _lanes(i)]) for i in heads]
        ss = [st_ref[i] for i in heads]
        sbs = [_bf(s) for s in ss]
        dss = [ds_scr[i] for i in heads]
        dsbs = [_bf(ds) for ds in dss]

        pbs = [_bf(p) for p in _hgrn_scores(xs, qs, kks, mask_ref)]
        os_ = [_dot(pbs[i], vbs[i]) + _dot_nt(qes[i], sbs[i]) for i in heads]

        dos = []
        for i in heads:
            ag, gain, dy = ag_ref[:, _lanes(i)], gain_ref[:, _lanes(i)], dy_ref[:, _lanes(i)]
            oh, r = _rms(os_[i])
            sg_sig = _sigmoid(ag)
            sg = ag * sg_sig
            da_ref[3, :, _lanes(i)] = _bf(dy * oh * gain * _silu_grad(ag, sg_sig))
            dgain_ref[:, _lanes(i)] += jnp.sum(dy * oh * sg, axis=0, keepdims=True)
            dos.append(_bf(_rms_bwd(dy * gain * sg, oh, r)))

        dps = [_dot_nt(dos[i], vbs[i]) for i in heads]
        for i in heads:
            da_ref[2, :, _lanes(i)] = _bf(_dot_tn(pbs[i], dos[i]) + _dot_nt(kes[i], dsbs[i]))
        dq_ss = [ebs[i] * _dot(dos[i], sbs[i]) for i in heads]
        dk_ss = [ebls[i] * _dot(vbs[i], dsbs[i]) for i in heads]
        dqs, dks = [], []
        for i in heads:
            dpd = jnp.sum(mask_ref[0] * dps[i], axis=1, keepdims=True)
            z_scr[i, pl.ds(0, c), :] = qs[i] * dq_ss[i]
            z_scr[i, pl.ds((nl + 1) * c, c), :] = kks[i] * dk_ss[i]
            dqs.append(dq_ss[i] + dpd * kks[i])
            dks.append(dk_ss[i] + dpd * qs[i])
        for l in range(nl):
            for i in heads:
                qf, kf, ql, kl = _hgrn_level(xs[i], l, qs[i], kks[i])
                dpl = _bf(mask_ref[l + 1] * dps[i])
                dq_l = qf * _dot(dpl, kl)
                dk_l = kf * _dot_tn(dpl, ql)
                z_scr[i, pl.ds((l + 1) * c, c), :] = qs[i] * dq_l - kks[i] * dk_l
                dqs[i] = dqs[i] + dq_l
                dks[i] = dks[i] + dk_l

        zsplits = [_split(z_scr[i]) for i in heads]
        dlogfs = [_dot_tn(dmat_ref[...], zhi) + _dot_tn(dmat_ref[...], zlo) for zhi, zlo in zsplits]
        ds_new = [_dot_tn(dos[i], qes[i]) for i in heads]
        for i in heads:
            dlogf = dlogfs[i] + ebl_rows[i] * jnp.sum(dss[i] * ss[i], axis=0, keepdims=True)
            dkk = dks[i] - dlogf / (1.0 - kks[i])
            da_ref[1, :, _lanes(i)] = _bf(dkk * (1.0 - lbs[i]) * (-(snegs[i] * (1.0 - snegs[i]))))
            dlb_ref[:, _lanes(i)] += jnp.sum(dkk * (-snegs[i]), axis=0, keepdims=True)
            da_ref[0, :, _lanes(i)] = _bf(dqs[i] * _silu_grad(aqs[i], sqs[i]))
            ds_scr[i] = dss[i] * ebl_rows[i] + ds_new[i]

    w = HG_HEADS * A_D
    return pl.pallas_call(
        body, name="hgrn_bwd", grid=(A_HEADS // HG_HEADS, nch),
        in_specs=[col(0), col(1), col(2), col(3), vec, vec, mask_spec, dmat_spec, state_spec,
                  pl.BlockSpec((c, w), lambda h, n: (cidx(n), h))],
        out_specs=[pl.BlockSpec((4, c, w), lambda h, n: (0, cidx(n), h)), vec, vec],
        out_shape=[_sds((4, t, GROUP), BF16)] + [_sds((1, GROUP))] * 2,
        scratch_shapes=[pltpu.VMEM((HG_HEADS, A_D, A_D), F32), pltpu.VMEM((HG_HEADS, (nl + 2) * c, A_D), F32)],
        compiler_params=_params(("arbitrary", "arbitrary")),
    )(proj, proj, proj, proj, lb, gain, masks, dmat, states, dya)


def _sb_consts():
    j = np.arange(SB_TK)
    strict = (j[:, None] > j[None, :]).astype(np.float32)
    lower = (j[:, None] < j[None, :]).astype(np.float32)
    return jnp.asarray(strict, BF16), jnp.asarray(lower, BF16)


def _lane(x, k):
    return jnp.broadcast_to(x[:, k:k + 1], x.shape)


def _key_lane(kb):
    return lax.broadcasted_iota(jnp.int32, (1, SB_TK), 1) == kb


def _causal(x, masked):
    if not masked:
        return x
    n = (SB_TK, SB_TK)
    top = jnp.where(lax.broadcasted_iota(jnp.int32, n, 1) < lax.broadcasted_iota(jnp.int32, n, 0), x[:SB_TK], 0.0)
    return top if x.shape[0] == SB_TK else jnp.concatenate([top, x[SB_TK:]], axis=0)


def _sb_softplus(z, masked):
    logsig = jnp.minimum(z, 0.0) - jnp.log2(1.0 + jnp.exp2(-jnp.abs(z)))
    return _causal(z - logsig, masked), logsig


def _sb_sweep(qi, group_fn, state, group, ascending):
    nd = SB_TQ // SB_TK

    def tile(kb, r0, masked):
        return (pl.multiple_of(kb * SB_TK, SB_TK), r0, masked, kb)

    def run(tiles, st):
        for i in range(0, len(tiles), group):
            st = group_fn(tiles[i:i + group], st)
        return st

    diag = [tile(qi * nd + d, d * SB_TK, True) for d in range(nd)]
    if ascending:
        state = lax.fori_loop(0, qi, lambda j, st: run([tile(j * nd + g, 0, False) for g in range(nd)], st), state)
        return run(diag, state)
    state = run(diag[::-1], state)
    return lax.fori_loop(
        0, qi, lambda j, st: run([tile((qi - j) * nd - 1 - g, 0, False) for g in range(nd)], st), state)


def _set_rows(r0, full, new):
    return new if r0 == 0 else jnp.concatenate([full[:r0], new], axis=0)


def _sb_specs(t, tq):
    col = lambda g: pl.BlockSpec((tq, 2 * B_D), lambda p, i, h: (i, g * (GROUP // (2 * B_D)) + p))
    full = lambda g: pl.BlockSpec((t, 2 * B_D), lambda p, i, h: (0, g * (GROUP // (2 * B_D)) + p))
    vec = pl.BlockSpec((1, 2 * B_D), lambda p, i, h: (0, p))
    mat = pl.BlockSpec((SB_TK, SB_TK), lambda p, i, h: (0, 0))
    car = pl.BlockSpec((None, tq, SB_TK), lambda p, i, h: (2 * p + h, i, 0))
    return col, full, vec, mat, car


def _head_lanes(h):
    return (lax.broadcasted_iota(jnp.int32, (1, 2 * B_D), 1) >= B_D) == (h == 1)


def _put(ref, h, val):
    @pl.when(h == 0)
    def _():
        ref[...] = val

    @pl.when(h == 1)
    def _():
        ref[...] += val


def _sb_fwd(qkv, bgate, gain, gather=None):
    t = qkv.shape[0]
    tq = SB_TQ
    strict, _ = _sb_consts()
    n_steps = (B_HEADS // 2, t // tq, 2)

    def body(q_ref, k_ref, v_ref, bg_ref, gain_ref, m_ref, *rest):
        if gather is None:
            o_ref, y_ref, car_ref = rest
        else:
            flat_ref, _, o_ref, y_ref, car_ref, gathered_ref, send_sems, recv_sems = rest
            start, forward, finish = _gather_plan(flat_ref, gathered_ref, send_sems, recv_sems, *gather[2:])
            step = (pl.program_id(0) * n_steps[1] + pl.program_id(1)) * n_steps[2] + pl.program_id(2)
            pl.when(step == 0)(start)
            pl.when(step == 2 * n_steps[1] * n_steps[2])(forward)
            pl.when(step == n_steps[0] * n_steps[1] * n_steps[2] - 1)(finish)
        h = pl.program_id(2)
        lanes = _head_lanes(h)
        qb = jnp.where(lanes, q_ref[...], jnp.zeros_like(q_ref))
        cmat = m_ref[...]

        def group(tiles, state):
            carry, acc, cars = state
            kv = [(k_ref[pl.ds(off, SB_TK), :], v_ref[pl.ds(off, SB_TK), :]) for off, _, _, _ in tiles]
            zs = [_dot_nt(qb[r0:], kb) for (_, r0, _, _), (kb, _) in zip(tiles, kv)]
            sps = [_sb_softplus(z, masked) for z, (_, _, masked, _) in zip(zs, tiles)]
            css = [_dot(_bf(sp), cmat) for sp, _ in sps]
            ws = []
            for (sp, logsig), cs, (_, r0, masked, kb) in zip(sps, css, tiles):
                ws.append(_bf(_causal(jnp.exp2(logsig - cs - carry[r0:]), masked)))
                cars = _set_rows(r0, cars, jnp.where(_key_lane(kb), carry[r0:], cars[r0:]))
                carry = _set_rows(r0, carry, carry[r0:] + _lane(cs + sp, 0))
            for w, (_, vb), (_, r0, _, _) in zip(ws, kv, tiles):
                acc = _set_rows(r0, acc, acc[r0:] + _dot(w, vb))
            return carry, acc, cars

        zero = jnp.zeros((tq, SB_TK), F32)
        _, acc, cars = _sb_sweep(pl.program_id(1), group, (zero, jnp.zeros((tq, 2 * B_D), F32), zero),
                                 SB_GROUP_FWD, False)
        car_ref[...] = cars
        o = jnp.where(lanes, acc, 0.0)
        oh = o * lax.rsqrt(jnp.sum(o * o, axis=-1, keepdims=True) * (1.0 / B_D) + EPS)
        bg = bg_ref[...]
        _put(o_ref, h, o)
        _put(y_ref, h, oh * gain_ref[...] * (bg * _sigmoid(bg)))

    col, full, vec, mat, car = _sb_specs(t, tq)
    out = pl.BlockSpec((tq, 2 * B_D), lambda p, i, h: (i, p))
    in_specs = [col(0), full(1), full(2), col(0), vec, mat]
    out_specs = [out, out, car]
    out_shape = [_sds((t, GROUP)), _sds((t, GROUP)), _sds((B_HEADS, t, SB_TK))]
    operands = [qkv, qkv, qkv, bgate, gain, strict]
    extra = {}
    if gather is not None:
        in_specs += [_ANY, _ANY]
        out_specs += [_ANY]
        out_shape += [_sds(gather[1].shape, gather[1].dtype)]
        operands += [gather[0], gather[1]]
        extra = dict(input_output_aliases={7: 3}, scratch_shapes=_gather_sems(gather[4]))
    return pl.pallas_call(
        body, name="sb_fwd" if gather is None else "sb_fwd_gather", grid=n_steps,
        in_specs=in_specs, out_specs=out_specs, out_shape=out_shape,
        compiler_params=_params(("arbitrary", "arbitrary", "arbitrary")), **extra,
    )(*operands)


def _sb_bwd(qkv, bgate, o, carries, dy, gain, exchange=None):
    t = qkv.shape[0]
    tq = SB_TQ
    strict, lower = _sb_consts()

    def body(q_ref, k_ref, v_ref, bg_ref, o_ref, car_ref, dy_ref, gain_ref, ms_ref, ml_ref,
             dq_ref, dk_ref, dv_ref, dbg_ref, dgain_ref):
        qi = pl.program_id(1)
        h = pl.program_id(2)
        lanes = _head_lanes(h)

        @pl.when((qi == 0) & (h == 0))
        def _():
            dk_ref[...] = jnp.zeros_like(dk_ref)
            dv_ref[...] = jnp.zeros_like(dv_ref)
            dgain_ref[...] = jnp.zeros_like(dgain_ref)

        qb = jnp.where(lanes, q_ref[...], jnp.zeros_like(q_ref))
        cmat = ms_ref[...]
        lmat = ml_ref[...]
        cars = car_ref[...]
        o = jnp.where(lanes, o_ref[...], 0.0)
        dy = jnp.where(lanes, dy_ref[...], 0.0)
        bg = bg_ref[...]
        gain = gain_ref[...]
        r = lax.rsqrt(jnp.sum(o * o, axis=-1, keepdims=True) * (1.0 / B_D) + EPS)
        oh = o * r
        sig = _sigmoid(bg)
        sg = bg * sig
        _put(dbg_ref, h, dy * oh * gain * _silu_grad(bg, sig))
        dgain_ref[...] += jnp.sum(dy * oh * sg, axis=0, keepdims=True)
        doh = dy * gain * sg
        do = _bf(r * (doh - oh * (jnp.sum(doh * oh, axis=-1, keepdims=True) * (1.0 / B_D))))

        def group(tiles, state):
            gleft, dq = state
            kv = [(k_ref[pl.ds(off, SB_TK), :], v_ref[pl.ds(off, SB_TK), :]) for off, _, _, _ in tiles]
            zs = [_dot_nt(qb[r0:], kb) for (_, r0, _, _), (kb, _) in zip(tiles, kv)]
            dws = [_dot_nt(do[r0:], vb) for (_, r0, _, _), (_, vb) in zip(tiles, kv)]
            sps = [_sb_softplus(z, masked) for z, (_, _, masked, _) in zip(zs, tiles)]
            css = [_dot(_bf(sp), cmat) for sp, _ in sps]
            ws, gs = [], []
            for (_, logsig), cs, dw, (_, r0, masked, kb) in zip(sps, css, dws, tiles):
                right = jnp.sum(jnp.where(_key_lane(kb), cars[r0:], 0.0), axis=1, keepdims=True)
                w = _causal(jnp.exp2(logsig - cs - right), masked)
                ws.append(_bf(w))
                gs.append(dw * w)
            gps = [_dot(_bf(g), lmat) for g in gs]
            dzs = []
            for (_, logsig), g, gp, (_, r0, masked, _) in zip(sps, gs, gps, tiles):
                dz = g - jnp.exp2(logsig) * (g + gleft[r0:] + gp)
                dzs.append(_bf(_causal(dz, masked)))
                gleft = _set_rows(r0, gleft, gleft[r0:] + _lane(gp + g, SB_TK - 1))
            for dz, wb, (kb, _), (off, r0, _, _) in zip(dzs, ws, kv, tiles):
                dq = _set_rows(r0, dq, dq[r0:] + _dot(dz, kb))
                dk_ref[pl.ds(off, SB_TK), :] += _dot_tn(dz, qb[r0:])
                dv_ref[pl.ds(off, SB_TK), :] += _dot_tn(wb, do[r0:])
            return gleft, dq

        _, dq = _sb_sweep(qi, group, (jnp.zeros((tq, SB_TK), F32), jnp.zeros((tq, 2 * B_D), F32)), SB_GROUP_BWD, True)
        _put(dq_ref, h, jnp.where(lanes, dq * (B_D ** -0.5), 0.0))

    col, full, vec, mat, car = _sb_specs(t, tq)
    blk = pl.BlockSpec((tq, 2 * B_D), lambda p, i, h: (i, p))
    whole = pl.BlockSpec((t, 2 * B_D), lambda p, i, h: (0, p))
    grid = (B_HEADS // 2, t // tq, 2)
    in_specs = [col(0), full(1), full(2), col(0), blk, car, blk, vec, mat, mat]
    out_specs = [blk, whole, whole, blk, vec]
    out_shape = [_sds((t, GROUP))] * 4 + [_sds((1, GROUP))]
    operands = [qkv, qkv, qkv, bgate, o, carries, dy, gain, strict, lower]
    scratch = []
    if exchange is not None:
        step_of = lambda: (pl.program_id(0) * grid[1] + pl.program_id(1)) * grid[2] + pl.program_id(2)
        body = _with_exchange(body, 10, 5, exchange, step_of, grid[0] * grid[1] * grid[2])
        xi, xo, xs, scratch, xop = _exchange_args(exchange)
        in_specs, out_specs, out_shape, operands = in_specs + xi, out_specs + xo, out_shape + xs, operands + xop
    return pl.pallas_call(
        body, name="sb_bwd" if exchange is None else "sb_bwd_exchange", grid=grid,
        in_specs=in_specs, out_specs=out_specs, out_shape=out_shape, scratch_shapes=scratch,
        compiler_params=_params(("arbitrary", "arbitrary", "arbitrary")),
    )(*operands)


def _adamw(w, g, m, v):
    rows, cols = w.shape
    tr = rows
    for cand in (400, 256, 128, 64, 32, 16, 8):
        if rows % cand == 0:
            tr = cand
            break

    def body(w_ref, g_ref, m_ref, v_ref, d_ref, nm_ref, nv_ref):
        g_ = g_ref[...]
        m_ = ADAM_B1 * m_ref[...] + (1.0 - ADAM_B1) * g_
        v_ = ADAM_B2 * v_ref[...] + (1.0 - ADAM_B2) * (g_ * g_)
        m_hat = m_ / (1.0 - ADAM_B1 ** ADAM_STEP)
        v_hat = v_ / (1.0 - ADAM_B2 ** ADAM_STEP)
        d_ref[...] = -ADAM_LR * (m_hat / (jnp.sqrt(v_hat) + ADAM_EPS) + ADAM_WD * w_ref[...])
        nm_ref[...] = m_
        nv_ref[...] = v_

    spec = pl.BlockSpec((tr, cols), lambda i: (i, 0))
    return pl.pallas_call(
        body, name="adamw", grid=(rows // tr,), in_specs=[spec] * 4, out_specs=[spec] * 3,
        out_shape=[_sds((rows, cols))] * 3, compiler_params=_params(("arbitrary",)),
    )(w, g, m, v)


_ANY = pl.BlockSpec(memory_space=pl.ANY)


def _place():
    return lax.axis_index("x"), lax.axis_index("y"), lax.axis_index("c")


def _gather_plan(x_ref, out_ref, send_sems, recv_sems, row0, nrows, nc):
    x, y, c = _place()
    me = 2 * x + y
    sibling = (x, y, 1 - c)
    half = nrows // 2
    ch = half // nc
    peers = [me ^ k for k in (1, 2, 3)]

    def rows(shard, hc, r):
        return out_ref.at[shard, pl.ds(row0 + hc * half + r * ch, ch), :]

    def copy(k, shard, hc, r, to, src=None):
        return pltpu.make_async_remote_copy(
            src_ref=rows(shard, hc, r) if src is None else src, dst_ref=rows(shard, hc, r),
            send_sem=send_sems.at[k * nc + r], recv_sem=recv_sems.at[k * nc + r], device_id=to, device_id_type=MESH)

    def first(k, p, r):
        return copy(k, me, c, r, (p >> 1, p & 1, c), src=x_ref.at[pl.ds(row0 + c * half + r * ch, ch), :])

    def start():
        for k, p in enumerate(peers):
            for r in range(nc):
                first(k, p, r).start()

    def forward():
        for k, p in enumerate(peers):
            for r in range(nc):
                copy(k, p, c, r, sibling).wait_recv()
                copy(3 + k, p, c, r, sibling).start()

    def finish():
        for k, p in enumerate(peers):
            for r in range(nc):
                copy(3 + k, p, 1 - c, r, sibling).wait_recv()
        for k, p in enumerate(peers):
            for r in range(nc):
                first(k, p, r).wait_send()
                copy(3 + k, p, c, r, sibling).wait_send()

    return start, forward, finish


def _gather_sems(nc):
    return [pltpu.SemaphoreType.DMA((6 * nc,)), pltpu.SemaphoreType.DMA((6 * nc,))]


def _swap_plan(g_ref, out_ref, send_sems, recv_sems):
    half, ch, nc = HALF_LAYER, CHUNK_ROWS, RS_CHUNKS

    def copies():
        x, y, c = _place()
        return [pltpu.make_async_remote_copy(
            src_ref=g_ref.at[j, pl.ds((1 - c) * half + r * ch, ch), :], dst_ref=out_ref.at[j, pl.ds(r * ch, ch), :],
            send_sem=send_sems.at[j * nc + r], recv_sem=recv_sems.at[j * nc + r],
            device_id=(x, y, 1 - c), device_id_type=MESH) for j in range(N_SHARD) for r in range(nc)]

    def start():
        for cp in copies():
            cp.start()

    def finish():
        for cp in copies():
            cp.wait()

    return start, finish


def _scatter_plan(p_ref, out_ref, send_sems, recv_sems):
    ch, nc = CHUNK_ROWS, RS_CHUNKS

    def copies():
        x, y, c = _place()
        me = 2 * x + y
        return [pltpu.make_async_remote_copy(
            src_ref=p_ref.at[me ^ k, pl.ds(r * ch, ch), :], dst_ref=out_ref.at[k - 1, pl.ds(r * ch, ch), :],
            send_sem=send_sems.at[(k - 1) * nc + r], recv_sem=recv_sems.at[(k - 1) * nc + r],
            device_id=((me ^ k) >> 1, (me ^ k) & 1, c), device_id_type=MESH) for k in (1, 2, 3) for r in range(nc)]

    def start():
        for cp in copies():
            cp.start()

    def finish():
        for cp in copies():
            cp.wait()

    return start, finish


SWAP = (_swap_plan, (N_SHARD, HALF_LAYER, D_MODEL), F32, N_SHARD * RS_CHUNKS)
SCATTER = (_scatter_plan, (3, HALF_LAYER, D_MODEL), BF16, 3 * RS_CHUNKS)


def _exchange_call(kind, operand):
    plan, shape, dtype, n_sems = kind

    def body(in_ref, out_ref, send_sems, recv_sems):
        start, finish = plan(in_ref, out_ref, send_sems, recv_sems)
        start()
        finish()

    return pl.pallas_call(
        body, name="exchange", in_specs=[_ANY], out_specs=_ANY, out_shape=_sds(shape, dtype),
        scratch_shapes=[pltpu.SemaphoreType.DMA((n_sems,)), pltpu.SemaphoreType.DMA((n_sems,))],
    )(operand)


def _with_exchange(body, n_in, n_out, exchange, step_of, n_steps):
    def wrapped(*refs):
        ins, src = refs[:n_in], refs[n_in]
        outs, dst = refs[n_in + 1:n_in + 1 + n_out], refs[n_in + 1 + n_out]
        send_sems, recv_sems = refs[n_in + 2 + n_out:n_in + 4 + n_out]
        start, finish = exchange[0][0](src, dst, send_sems, recv_sems)
        pl.when(step_of() == 0)(start)
        body(*ins, *outs, *refs[n_in + 4 + n_out:])
        pl.when(step_of() == n_steps - 1)(finish)

    return wrapped


def _exchange_args(exchange):
    (plan, shape, dtype, n_sems), operand = exchange
    sems = [pltpu.SemaphoreType.DMA((n_sems,)), pltpu.SemaphoreType.DMA((n_sems,))]
    return [_ANY], [_ANY], [_sds(shape, dtype)], sems, [operand]


def _gather_weights(flat, row0, nrows, nc):
    def body(x_ref, out_ref, send_sems, recv_sems):
        start, forward, finish = _gather_plan(x_ref, out_ref, send_sems, recv_sems, row0, nrows, nc)
        start()
        forward()
        finish()

    return pl.pallas_call(
        body, name="gather_weights", in_specs=[_ANY], out_specs=_ANY,
        out_shape=_sds((N_SHARD, ROWS_FLAT, D_MODEL), BF16), scratch_shapes=_gather_sems(nc),
    )(flat)


def _add_my_half(grads, recv):
    tr = 400
    nb = HALF_LAYER // tr
    core = lax.axis_index("c").astype(jnp.int32).reshape(1)

    def body(c_ref, g_ref, r_ref, o_ref, ob_ref):
        acc = g_ref[...] + r_ref[...]
        o_ref[...] = acc
        ob_ref[...] = _bf(acc)

    out = pl.BlockSpec((None, tr, D_MODEL), lambda j, i, c_ref: (j, i, 0))
    return pl.pallas_call(
        body, name="add_my_half",
        grid_spec=pltpu.PrefetchScalarGridSpec(
            num_scalar_prefetch=1, grid=(N_SHARD, nb),
            in_specs=[pl.BlockSpec((None, tr, D_MODEL), lambda j, i, c_ref: (j, c_ref[0] * nb + i, 0)), out],
            out_specs=[out, out]),
        out_shape=[_sds((N_SHARD, HALF_LAYER, D_MODEL)), _sds((N_SHARD, HALF_LAYER, D_MODEL), BF16)],
        compiler_params=_params(("arbitrary", "arbitrary")),
    )(core, grads, recv)


def _sum_scattered(part, recv):
    tr = 400
    chip = (2 * lax.axis_index("x") + lax.axis_index("y")).astype(jnp.int32).reshape(1)

    def body(c_ref, p_ref, r_ref, o_ref):
        acc = p_ref[...]
        for k in range(3):
            acc = acc + r_ref[k].astype(F32)
        o_ref[...] = acc

    return pl.pallas_call(
        body, name="sum_scattered",
        grid_spec=pltpu.PrefetchScalarGridSpec(
            num_scalar_prefetch=1, grid=(HALF_LAYER // tr,),
            in_specs=[pl.BlockSpec((None, tr, D_MODEL), lambda i, c_ref: (c_ref[0], i, 0)),
                      pl.BlockSpec((3, tr, D_MODEL), lambda i, c_ref: (0, i, 0))],
            out_specs=pl.BlockSpec((tr, D_MODEL), lambda i, c_ref: (i, 0))),
        out_shape=_sds((HALF_LAYER, D_MODEL)), compiler_params=_params(("arbitrary",)),
    )(chip, part, recv)


def _swap_reduced(mine):
    ch, nc = CHUNK_ROWS, RS_CHUNKS

    def body(r_ref, out_ref, send_sems, recv_sems):
        x, y, c = _place()
        copies = [pltpu.make_async_remote_copy(
            src_ref=r_ref.at[l, pl.ds(r * ch, ch), :], dst_ref=out_ref.at[l, pl.ds(r * ch, ch), :],
            send_sem=send_sems.at[l * nc + r], recv_sem=recv_sems.at[l * nc + r],
            device_id=(x, y, 1 - c), device_id_type=MESH) for l in range(2) for r in range(nc)]
        for cp in copies:
            cp.start()
        for cp in copies:
            cp.wait()

    return pl.pallas_call(
        body, name="swap_reduced", in_specs=[_ANY], out_specs=_ANY,
        out_shape=_sds((2, HALF_LAYER, D_MODEL)),
        scratch_shapes=[pltpu.SemaphoreType.DMA((2 * nc,)), pltpu.SemaphoreType.DMA((2 * nc,))],
    )(mine)


def _allreduce_small(vec):
    def body(v_ref, out_ref, buf, send_sems, recv_sems):
        x, y, c = _place()
        me = 4 * x + 2 * y + c
        buf[me] = v_ref[...]
        peers = [me ^ k for k in range(1, N_DEV)]
        sends = [pltpu.make_async_remote_copy(
            src_ref=v_ref, dst_ref=buf.at[me], send_sem=send_sems.at[k], recv_sem=recv_sems.at[k],
            device_id=(p >> 2, (p >> 1) & 1, p & 1), device_id_type=MESH) for k, p in enumerate(peers)]
        for cp in sends:
            cp.start()
        for k, p in enumerate(peers):
            pltpu.make_async_remote_copy(
                src_ref=v_ref, dst_ref=buf.at[p], send_sem=send_sems.at[k], recv_sem=recv_sems.at[k],
                device_id=(p >> 2, (p >> 1) & 1, p & 1), device_id_type=MESH).wait_recv()
        for cp in sends:
            cp.wait_send()
        acc = buf[0]
        for d in range(1, N_DEV):
            acc = acc + buf[d]
        out_ref[...] = acc

    vm = pl.BlockSpec(memory_space=pltpu.VMEM)
    return pl.pallas_call(
        body, name="allreduce_small", in_specs=[vm], out_specs=vm, out_shape=_sds((SMALL_ROWS, 128)),
        scratch_shapes=[pltpu.VMEM((N_DEV, SMALL_ROWS, 128), F32),
                        pltpu.SemaphoreType.DMA((N_DEV - 1,)), pltpu.SemaphoreType.DMA((N_DEV - 1,))],
    )(vec)


def _flatten_shard(w_in, w_out, w_pg, w_pp):
    return jnp.concatenate([w_in.reshape(-1, D_MODEL), w_out.reshape(-1, D_MODEL), w_pg.reshape(-1, D_MODEL),
                            w_pp.reshape(-1, D_MODEL)], axis=0)


def _unflatten_layers(flats):
    a, b, c = D_MODEL, D_MODEL + D_MODEL // N_SHARD, D_MODEL + 2 * (D_MODEL // N_SHARD)
    q = D_MODEL // N_SHARD
    return (jnp.stack([f[:a] for f in flats]), jnp.stack([f[a:b] for f in flats]),
            jnp.stack([f[b:c] for f in flats]), jnp.stack([f[c:].reshape(D_PLE, q) for f in flats]))


def _full_w_pp(gathered):
    c = ROWS_W_IN + ROWS_W_OUT + ROWS_W_PG
    q = D_MODEL // N_SHARD
    rpp = ROWS_W_PP // 2
    return [gathered[:, c + l * rpp:c + (l + 1) * rpp, :].reshape(N_SHARD, D_PLE, q).transpose(1, 0, 2)
            .reshape(D_PLE, D_MODEL) for l in range(2)]


def _layer_grads(dw_in, dw_out, dw_pg, dw_pp):
    q = D_MODEL // N_SHARD
    rpp = ROWS_W_PP // 2
    rest = jnp.concatenate([dw_out.reshape(N_SHARD, q, D_MODEL), dw_pg.reshape(N_SHARD, q, D_MODEL),
                            dw_pp.reshape(D_PLE, N_SHARD, q).transpose(1, 0, 2).reshape(N_SHARD, rpp, D_MODEL)], axis=1)
    return lax.dynamic_update_slice(dw_in, rest, (0, D_MODEL, 0))


def _lower_bounds(lb_logits):
    sm = jax.nn.softmax(lb_logits.astype(F32), axis=0)
    return jnp.cumsum(sm, axis=0) - sm[0:1]


def kernel(x, p, norm_mix, w_in, a_out_norm, b_out_norm, w_out, lb_logits, ple_gate_norm, w_ple_gate, w_ple_proj, ple_post_norm, final_norm, loss_target, m_norm_mix, m_w_in, m_a_out_norm, m_b_out_norm, m_w_out, m_lb_logits, m_ple_gate_norm, m_w_ple_gate, m_w_ple_proj, m_ple_post_norm, m_final_norm, v_norm_mix, v_w_in, v_a_out_norm, v_b_out_norm, v_w_out, v_lb_logits, v_ple_gate_norm, v_w_ple_gate, v_w_ple_proj, v_ple_post_norm, v_final_norm):
    t = x.shape[1]
    h0 = x.reshape(t, D_MODEL)
    target = loss_target.reshape(t, D_MODEL)
    pl_in = p.reshape(2, t, D_PLE)

    w_flat_bf = _flatten_shard(_bf(w_in), _bf(w_out), _bf(w_ple_gate), _bf(w_ple_proj))
    chip = 2 * lax.axis_index("x") + lax.axis_index("y")
    gathered = lax.dynamic_update_slice(_gather_weights(w_flat_bf, 0, D_MODEL, 2), w_flat_bf[None], (chip, 0, 0))
    lbs, lbs_vjp = jax.vjp(_lower_bounds, lb_logits)

    saved = []
    h = h0
    for l in range(2):
        g_mix = norm_mix[l].reshape(1, D_MODEL)
        lb = lbs[l].reshape(1, GROUP)
        ga = a_out_norm[l].reshape(1, GROUP)
        gb = b_out_norm[l].reshape(1, GROUP)
        proj, qkv, bgate = _inproj(h, g_mix, gathered, l)
        ya, states = _hgrn_fwd(proj, lb, ga)
        if l == 0:
            ob, yb, cars, gathered = _sb_fwd(qkv, bgate, gb, (w_flat_bf, gathered, D_MODEL, ROWS_FLAT - D_MODEL, 4))
            w_pps = _full_w_pp(gathered)
        else:
            ob, yb, cars = _sb_fwd(qkv, bgate, gb)
        h1 = _outproj(h, ya, yb, gathered, l)
        g_post = ple_post_norm[l].reshape(1, D_MODEL)
        g_gate = ple_gate_norm[l].reshape(1, D_MODEL)
        saved.append((h, proj, qkv, bgate, states, ya, yb, ob, cars, h1))
        if l == 0:
            h = _ple_fwd(h1, pl_in[l], w_pps[l], gathered, l, g_post, g_gate)
        else:
            dh, d_final, loss_part = _ple_fwd_final(h1, pl_in[l], w_pps[l], gathered, l, g_post, g_gate,
                                                    final_norm.reshape(1, D_MODEL), target)

    g_layer, chip_sum, scattered = [None] * 2, [None] * 2, [None] * 2
    d_mix, d_a, d_b, d_lb, d_gate, d_post = [None] * 2, [None] * 2, [None] * 2, [None] * 2, [None] * 2, [None] * 2
    for l in (1, 0):
        h_in, proj, qkv, bgate, states, ya, yb, ob, cars, h1 = saved[l]
        g_mix = norm_mix[l].reshape(1, D_MODEL)
        lb = lbs[l].reshape(1, GROUP)
        ga = a_out_norm[l].reshape(1, GROUP)
        gb = b_out_norm[l].reshape(1, GROUP)
        g_post = ple_post_norm[l].reshape(1, D_MODEL)
        g_gate = ple_gate_norm[l].reshape(1, D_MODEL)
        if l == 1:
            dh1, dw_pg, dw_pp, d_gate[l], d_post[l] = _ple_bwd(dh, h1, pl_in[l], w_pps[l], gathered, l, g_post, g_gate)
            dya, dyb, dw_out = _outproj_bwd(dh1, ya, yb, gathered, l)
            dbq, dbk, dbv, dbg, d_b[l] = _sb_bwd(qkv, bgate, ob, cars, dyb, gb)
        else:
            dh1, dw_pg, dw_pp, d_gate[l], d_post[l], from_sibling = _ple_bwd(
                dh, h1, pl_in[l], w_pps[l], gathered, l, g_post, g_gate, (SWAP, g_layer[1]))
            chip_sum[1], chip_sum_bf = _add_my_half(g_layer[1], from_sibling)
            dya, dyb, dw_out = _outproj_bwd(dh1, ya, yb, gathered, l)
            dbq, dbk, dbv, dbg, d_b[l], scattered[1] = _sb_bwd(qkv, bgate, ob, cars, dyb, gb, (SCATTER, chip_sum_bf))
        da, d_lb[l], d_a[l] = _hgrn_bwd(proj, lb, ga, states, dya)
        db = jnp.stack([dbq, dbk * LN2, dbv, dbg]).astype(BF16)
        g_layer[l] = _layer_grads(_inproj_bwd_dw(h_in, g_mix, da, db), dw_out, dw_pg, dw_pp)
        if l == 1:
            dh, d_mix[l] = _inproj_bwd_dx(dh1, h_in, g_mix, gathered, l, da, db)
        else:
            dh, d_mix[l], from_sibling = _inproj_bwd_dx(dh1, h_in, g_mix, gathered, l, da, db, (SWAP, g_layer[0]))
    grad_x = dh.reshape(x.shape)

    chip_sum[0], chip_sum_bf = _add_my_half(g_layer[0], from_sibling)
    scattered[0] = _exchange_call(SCATTER, chip_sum_bf)
    mine = jnp.stack([_sum_scattered(chip_sum[l], scattered[l]) for l in range(2)])
    other = _swap_reduced(mine)
    south = lax.axis_index("c") == 0
    g_w_in, g_w_out, g_w_pg, g_w_pp = _unflatten_layers(
        [jnp.concatenate([jnp.where(south, mine[l], other[l]), jnp.where(south, other[l], mine[l])]) for l in range(2)])

    small = jnp.concatenate([
        jnp.concatenate(d_mix).reshape(-1, 128), jnp.concatenate(d_a).reshape(-1, 128),
        jnp.concatenate(d_b).reshape(-1, 128), jnp.concatenate(d_lb).reshape(-1, 128),
        jnp.concatenate(d_gate).reshape(-1, 128), jnp.concatenate(d_post).reshape(-1, 128),
        d_final.reshape(-1, 128), jnp.broadcast_to(loss_part, (8, 128))], axis=0)
    small = _allreduce_small(small)
    loss = small[80, 0]
    g_norm_mix = small[0:16].reshape(2, D_MODEL)
    g_a = small[16:24].reshape(2, GROUP)
    g_b = small[24:32].reshape(2, GROUP)
    (g_lb,) = lbs_vjp(small[32:40].reshape(2, GROUP))
    g_gate = small[40:56].reshape(2, D_MODEL)
    g_post = small[56:72].reshape(2, D_MODEL)
    g_final = small[72:80].reshape(D_MODEL)

    def adam_matrix(w, g, m, v):
        d, nm, nv = _adamw(w.reshape(-1, D_MODEL), g.reshape(-1, D_MODEL), m.reshape(-1, D_MODEL), v.reshape(-1, D_MODEL))
        return d.reshape(w.shape), nm.reshape(w.shape), nv.reshape(w.shape)

    d_w_in, nm_w_in, nv_w_in = adam_matrix(w_in, g_w_in, m_w_in, v_w_in)
    d_w_out, nm_w_out, nv_w_out = adam_matrix(w_out, g_w_out, m_w_out, v_w_out)
    d_w_pg, nm_w_pg, nv_w_pg = adam_matrix(w_ple_gate, g_w_pg, m_w_ple_gate, v_w_ple_gate)
    d_w_pp, nm_w_pp, nv_w_pp = adam_matrix(w_ple_proj, g_w_pp, m_w_ple_proj, v_w_ple_proj)

    small_w = [norm_mix, a_out_norm, b_out_norm, lb_logits, ple_gate_norm, ple_post_norm, final_norm]
    small_g = [g_norm_mix, g_a, g_b, g_lb, g_gate, g_post, g_final]
    small_m = [m_norm_mix, m_a_out_norm, m_b_out_norm, m_lb_logits, m_ple_gate_norm, m_ple_post_norm, m_final_norm]
    small_v = [v_norm_mix, v_a_out_norm, v_b_out_norm, v_lb_logits, v_ple_gate_norm, v_ple_post_norm, v_final_norm]
    pack = lambda arrs: jnp.concatenate([a.reshape(-1, 128) for a in arrs], axis=0)
    ds, nms, nvs = _adamw(pack(small_w), pack(small_g), pack(small_m), pack(small_v))

    def unpack(packed):
        out, r = [], 0
        for a in small_w:
            n = a.size // 128
            out.append(packed[r:r + n].reshape(a.shape))
            r += n
        return out

    d_s, nm_s, nv_s = unpack(ds), unpack(nms), unpack(nvs)

    def ordered(s, big):
        return [s[0], big[0], s[1], s[2], big[1], s[3], s[4], big[2], big[3], s[5], s[6]]

    grads = ordered(small_g, [g_w_in, g_w_out, g_w_pg, g_w_pp])
    deltas = ordered(d_s, [d_w_in, d_w_out, d_w_pg, d_w_pp])
    new_m = ordered(nm_s, [nm_w_in, nm_w_out, nm_w_pg, nm_w_pp])
    new_v = ordered(nv_s, [nv_w_in, nv_w_out, nv_w_pg, nv_w_pp])
    return (loss, grad_x, *grads, *deltas, *new_m, *new_v)
```

```python
import functools
import math

import numpy as np
import jax
import jax.numpy as jnp
from jax import lax
from jax.experimental import pallas as pl
from jax.experimental.pallas import tpu as pltpu

F32 = jnp.float32
BF16 = jnp.bfloat16
MESH = pl.DeviceIdType.MESH

D_MODEL = 1024
D_PLE = 256
D_IN = 4096
A_HEADS, A_D = 4, 128
B_HEADS, B_D = 8, 64
GROUP = 512
EPS = 1e-6
N_SHARD = 4
N_DEV = 8

HG_CHUNK = 128
HG_LEVELS = 7
SB_TQ = 1024
SB_TK = 128
SB_GROUP_FWD, SB_GROUP_BWD = 4, 8
LOG2E = 1.4426950408889634
LN2 = 0.6931471805599453

ADAM_LR, ADAM_B1, ADAM_B2, ADAM_EPS, ADAM_WD, ADAM_STEP = 0.001, 0.9, 0.999, 1e-08, 0.01, 10

VMEM_LIMIT = 48 * 1024 * 1024
VMEM_LIMIT_BIG = 58 * 1024 * 1024

ROWS_W_IN = 2 * D_MODEL
ROWS_W_OUT = 2 * (D_MODEL // N_SHARD)
ROWS_W_PG = 2 * (D_MODEL // N_SHARD)
ROWS_W_PP = 2 * (D_PLE * (D_MODEL // N_SHARD) // D_MODEL)
ROWS_FLAT = ROWS_W_IN + ROWS_W_OUT + ROWS_W_PG + ROWS_W_PP
HALF_FLAT = ROWS_FLAT // 2
N_CHUNK = 10
CHUNK_ROWS = HALF_FLAT // N_CHUNK

ROWS_LAYER = ROWS_FLAT // 2
HALF_LAYER = ROWS_LAYER // 2
RS_CHUNKS = HALF_LAYER // CHUNK_ROWS

SMALL_ROWS = 88


def _sds(shape, dtype=F32):
    return jax.ShapeDtypeStruct(shape, dtype)


def _params(sem=None, vmem_limit=VMEM_LIMIT):
    kw = dict(vmem_limit_bytes=vmem_limit)
    if sem is not None:
        kw["dimension_semantics"] = sem
    return pltpu.CompilerParams(**kw)


def _dot(a, b, precision=None):
    return lax.dot_general(a, b, (((1,), (0,)), ((), ())), preferred_element_type=F32, precision=precision)


def _dot_nt(a, b, precision=None):
    return lax.dot_general(a, b, (((1,), (1,)), ((), ())), preferred_element_type=F32, precision=precision)


def _dot_tn(a, b, precision=None):
    return lax.dot_general(a, b, (((0,), (0,)), ((), ())), preferred_element_type=F32, precision=precision)


def _bf(x):
    return x.astype(BF16)


def _split(x):
    hi = x.astype(BF16)
    lo = (x - hi.astype(F32)).astype(BF16)
    return hi, lo


def _rms(x):
    r = lax.rsqrt(jnp.mean(x * x, axis=-1, keepdims=True) + EPS)
    return x * r, r


def _rms_bwd(dxh, xh, r):
    return r * (dxh - xh * jnp.mean(dxh * xh, axis=-1, keepdims=True))


def _sigmoid(x):
    return 1.0 / (1.0 + jnp.exp(-x))


def _silu_grad(x, sig):
    return sig * (1.0 + x * (1.0 - sig))


def _row_tile(t, want):
    return min(t, want)


def _inproj(h, g, gathered, layer):
    t = h.shape[0]
    tm = _row_tile(t, 512)

    def body(h_ref, g_ref, w_ref, pa_ref, qkv_ref, bg_ref):
        xh, _ = _rms(h_ref[...])
        u = _bf(xh * g_ref[...])
        for j in range(8):
            acc = _dot(u, w_ref[j // 2, :, pl.ds((j % 2) * GROUP, GROUP)])
            if j < 4:
                pa_ref[:, pl.ds(j * GROUP, GROUP)] = acc
            elif j == 4:
                qkv_ref[:, pl.ds(0, GROUP)] = _bf(acc * (B_D ** -0.5 * LOG2E))
            elif j < 7:
                qkv_ref[:, pl.ds((j - 4) * GROUP, GROUP)] = _bf(acc)
            else:
                bg_ref[...] = acc

    return pl.pallas_call(
        body, name="inproj", grid=(t // tm,),
        in_specs=[pl.BlockSpec((tm, D_MODEL), lambda i: (i, 0)),
                  pl.BlockSpec((1, D_MODEL), lambda i: (0, 0)),
                  pl.BlockSpec((N_SHARD, D_MODEL, D_MODEL), lambda i: (0, layer, 0))],
        out_specs=[pl.BlockSpec((tm, 4 * GROUP), lambda i: (i, 0)), pl.BlockSpec((tm, 3 * GROUP), lambda i: (i, 0)),
                   pl.BlockSpec((tm, GROUP), lambda i: (i, 0))],
        out_shape=[_sds((t, 4 * GROUP)), _sds((t, 3 * GROUP), BF16), _sds((t, GROUP))],
        compiler_params=_params(("arbitrary",)),
    )(h, g, gathered)


def _rows_spec(first_row):
    q = D_MODEL // N_SHARD
    return pl.BlockSpec((N_SHARD, q, D_MODEL), lambda i: (0, first_row // q, 0))


def _outproj(h, ya, yb, gathered, layer):
    t = h.shape[0]
    tm = _row_tile(t, 512)

    def body(h_ref, ya_ref, yb_ref, w_ref, o_ref):
        o_ref[...] = (h_ref[...] + _dot(_bf(ya_ref[...]), w_ref[0:2].reshape(GROUP, D_MODEL))
                      + _dot(_bf(yb_ref[...]), w_ref[2:4].reshape(GROUP, D_MODEL)))

    return pl.pallas_call(
        body, name="outproj", grid=(t // tm,),
        in_specs=[pl.BlockSpec((tm, D_MODEL), lambda i: (i, 0)),
                  pl.BlockSpec((tm, GROUP), lambda i: (i, 0)),
                  pl.BlockSpec((tm, GROUP), lambda i: (i, 0)),
                  _rows_spec(ROWS_W_IN + layer * (D_MODEL // N_SHARD))],
        out_specs=pl.BlockSpec((tm, D_MODEL), lambda i: (i, 0)),
        out_shape=_sds((t, D_MODEL)), compiler_params=_params(("arbitrary",)),
    )(h, ya, yb, gathered)


def _ple_mix(x, p_ref, wpp_ref, wpg_ref, gp_ref, gg_ref):
    ph, _ = _rms(_dot(_bf(p_ref[...]), wpp_ref[...]))
    xh, _ = _rms(x)
    gate = _sigmoid(_dot(_bf(xh * gg_ref[...]), wpg_ref[...].reshape(D_MODEL, D_MODEL)))
    return x + gate * (ph * gp_ref[...])


def _ple_specs(tm, layer):
    return [pl.BlockSpec((tm, D_MODEL), lambda i: (i, 0)),
            pl.BlockSpec((tm, D_PLE), lambda i: (i, 0)),
            pl.BlockSpec((D_PLE, D_MODEL), lambda i: (0, 0)),
            _rows_spec(ROWS_W_IN + ROWS_W_OUT + layer * (D_MODEL // N_SHARD)),
            pl.BlockSpec((1, D_MODEL), lambda i: (0, 0)),
            pl.BlockSpec((1, D_MODEL), lambda i: (0, 0))]


def _ple_fwd(h, p, w_pp, gathered, layer, g_post, g_gate):
    t = h.shape[0]
    tm = _row_tile(t, 256)

    def body(h_ref, p_ref, wpp_ref, wpg_ref, gp_ref, gg_ref, o_ref):
        o_ref[...] = _ple_mix(h_ref[...], p_ref, wpp_ref, wpg_ref, gp_ref, gg_ref)

    return pl.pallas_call(
        body, name="ple_fwd", grid=(t // tm,), in_specs=_ple_specs(tm, layer),
        out_specs=pl.BlockSpec((tm, D_MODEL), lambda i: (i, 0)),
        out_shape=_sds((t, D_MODEL)), compiler_params=_params(("arbitrary",)),
    )(h, p, w_pp, gathered, g_post, g_gate)


def _ple_fwd_final(h, p, w_pp, gathered, layer, g_post, g_gate, g_final, target):
    t = h.shape[0]
    tm = _row_tile(t, 256)

    def body(h_ref, p_ref, wpp_ref, wpg_ref, gp_ref, gg_ref, gf_ref, t_ref, dh_ref, dg_ref, loss_ref):
        @pl.when(pl.program_id(0) == 0)
        def _():
            dg_ref[...] = jnp.zeros_like(dg_ref)
            loss_ref[...] = jnp.zeros_like(loss_ref)

        xh, r = _rms(_ple_mix(h_ref[...], p_ref, wpp_ref, wpg_ref, gp_ref, gg_ref))
        gf = gf_ref[...]
        err = xh * gf - t_ref[...]
        part = 0.5 * jnp.sum(jnp.mean(err * err, axis=-1, keepdims=True), axis=0, keepdims=True)
        loss_ref[...] += jnp.broadcast_to(part, loss_ref.shape)
        dy = err * (1.0 / D_MODEL)
        dg_ref[...] += jnp.sum(dy * xh, axis=0, keepdims=True)
        dh_ref[...] = _rms_bwd(dy * gf, xh, r)

    return pl.pallas_call(
        body, name="ple_fwd_final", grid=(t // tm,),
        in_specs=_ple_specs(tm, layer) + [pl.BlockSpec((1, D_MODEL), lambda i: (0, 0)),
                                          pl.BlockSpec((tm, D_MODEL), lambda i: (i, 0))],
        out_specs=[pl.BlockSpec((tm, D_MODEL), lambda i: (i, 0)),
                   pl.BlockSpec((1, D_MODEL), lambda i: (0, 0)),
                   pl.BlockSpec((1, 128), lambda i: (0, 0))],
        out_shape=[_sds((t, D_MODEL)), _sds((1, D_MODEL)), _sds((1, 128))],
        compiler_params=_params(("arbitrary",)),
    )(h, p, w_pp, gathered, g_post, g_gate, g_final, target)


def _ple_bwd(dh2, h, p, w_pp, gathered, layer, g_post, g_gate, exchange=None):
    t = h.shape[0]
    tm = _row_tile(t, 512)

    def body(d_ref, h_ref, p_ref, wpp_ref, wpg_ref, gp_ref, gg_ref, dh_ref, dwpg_ref, dwpp_ref, dgg_ref, dgp_ref):
        @pl.when(pl.program_id(0) == 0)
        def _():
            dwpg_ref[...] = jnp.zeros_like(dwpg_ref)
            dwpp_ref[...] = jnp.zeros_like(dwpp_ref)
            dgg_ref[...] = jnp.zeros_like(dgg_ref)
            dgp_ref[...] = jnp.zeros_like(dgp_ref)

        d = d_ref[...]
        x = h_ref[...]
        gp = gp_ref[...]
        gg = gg_ref[...]
        pb = _bf(p_ref[...])
        ph, rp = _rms(_dot(pb, wpp_ref[...]))
        pe = ph * gp
        xh, rx = _rms(x)
        un = _bf(xh * gg)
        wpg = wpg_ref[...].reshape(D_MODEL, D_MODEL)
        gate = _sigmoid(_dot(un, wpg))
        dgpre = _bf(d * pe * gate * (1.0 - gate))
        dun = _dot_nt(dgpre, wpg)
        dh_ref[...] = d + _rms_bwd(dun * gg, xh, rx)
        dgg_ref[...] += jnp.sum(dun * xh, axis=0, keepdims=True)
        dwpg_ref[...] += _dot_tn(un, dgpre)
        dpe = d * gate
        dgp_ref[...] += jnp.sum(dpe * ph, axis=0, keepdims=True)
        dwpp_ref[...] += _dot_tn(pb, _bf(_rms_bwd(dpe * gp, ph, rp)))

    in_specs = [pl.BlockSpec((tm, D_MODEL), lambda i: (i, 0)),
                pl.BlockSpec((tm, D_MODEL), lambda i: (i, 0)),
                pl.BlockSpec((tm, D_PLE), lambda i: (i, 0)),
                pl.BlockSpec((D_PLE, D_MODEL), lambda i: (0, 0)),
                _rows_spec(ROWS_W_IN + ROWS_W_OUT + layer * (D_MODEL // N_SHARD)),
                pl.BlockSpec((1, D_MODEL), lambda i: (0, 0)),
                pl.BlockSpec((1, D_MODEL), lambda i: (0, 0))]
    out_specs = [pl.BlockSpec((tm, D_MODEL), lambda i: (i, 0)),
                 pl.BlockSpec((D_MODEL, D_MODEL), lambda i: (0, 0)),
                 pl.BlockSpec((D_PLE, D_MODEL), lambda i: (0, 0)),
                 pl.BlockSpec((1, D_MODEL), lambda i: (0, 0)),
                 pl.BlockSpec((1, D_MODEL), lambda i: (0, 0))]
    out_shape = [_sds((t, D_MODEL)), _sds((D_MODEL, D_MODEL)), _sds((D_PLE, D_MODEL)),
                 _sds((1, D_MODEL)), _sds((1, D_MODEL))]
    operands = [dh2, h, p, w_pp, gathered, g_post, g_gate]
    scratch = []
    if exchange is not None:
        body = _with_exchange(body, 7, 5, exchange, lambda: pl.program_id(0), t // tm)
        xi, xo, xs, scratch, xop = _exchange_args(exchange)
        in_specs, out_specs, out_shape, operands = in_specs + xi, out_specs + xo, out_shape + xs, operands + xop
    return pl.pallas_call(
        body, name="ple_bwd" if exchange is None else "ple_bwd_exchange", grid=(t // tm,),
        in_specs=in_specs, out_specs=out_specs, out_shape=out_shape, scratch_shapes=scratch,
        compiler_params=_params(("arbitrary",)),
    )(*operands)


def _outproj_bwd(dh, ya, yb, gathered, layer):
    t = dh.shape[0]
    tm = _row_tile(t, 512)

    def body(d_ref, ya_ref, yb_ref, w_ref, dya_ref, dyb_ref, dw_ref):
        @pl.when(pl.program_id(0) == 0)
        def _():
            dw_ref[...] = jnp.zeros_like(dw_ref)

        d = _bf(d_ref[...])
        dya_ref[...] = _dot_nt(d, w_ref[0:2].reshape(GROUP, D_MODEL))
        dyb_ref[...] = _dot_nt(d, w_ref[2:4].reshape(GROUP, D_MODEL))
        dw_ref[pl.ds(0, GROUP), :] += _dot_tn(_bf(ya_ref[...]), d)
        dw_ref[pl.ds(GROUP, GROUP), :] += _dot_tn(_bf(yb_ref[...]), d)

    return pl.pallas_call(
        body, name="outproj_bwd", grid=(t // tm,),
        in_specs=[pl.BlockSpec((tm, D_MODEL), lambda i: (i, 0)),
                  pl.BlockSpec((tm, GROUP), lambda i: (i, 0)),
                  pl.BlockSpec((tm, GROUP), lambda i: (i, 0)),
                  _rows_spec(ROWS_W_IN + layer * (D_MODEL // N_SHARD))],
        out_specs=[pl.BlockSpec((tm, GROUP), lambda i: (i, 0)),
                   pl.BlockSpec((tm, GROUP), lambda i: (i, 0)),
                   pl.BlockSpec((D_MODEL, D_MODEL), lambda i: (0, 0))],
        out_shape=[_sds((t, GROUP)), _sds((t, GROUP)), _sds((D_MODEL, D_MODEL))],
        compiler_params=_params(("arbitrary",)),
    )(dh, ya, yb, gathered)


def _inproj_bwd_dx(dres, h, g, gathered, layer, da, db, exchange=None):
    t = h.shape[0]
    tm = _row_tile(t, 512)

    def body(dres_ref, h_ref, g_ref, w_ref, da_ref, db_ref, dh_ref, dg_ref):
        @pl.when(pl.program_id(0) == 0)
        def _():
            dg_ref[...] = jnp.zeros_like(dg_ref)

        du = jnp.zeros((tm, D_MODEL), F32)
        for i in range(8):
            part = da_ref[i] if i < 4 else db_ref[i - 4]
            du = du + _dot_nt(part, w_ref[i // 2, :, pl.ds((i % 2) * GROUP, GROUP)])
        xh, r = _rms(h_ref[...])
        dg_ref[...] += jnp.sum(du * xh, axis=0, keepdims=True)
        dh_ref[...] = dres_ref[...] + _rms_bwd(du * g_ref[...], xh, r)

    in_specs = [pl.BlockSpec((tm, D_MODEL), lambda i: (i, 0)),
                pl.BlockSpec((tm, D_MODEL), lambda i: (i, 0)),
                pl.BlockSpec((1, D_MODEL), lambda i: (0, 0)),
                pl.BlockSpec((N_SHARD, D_MODEL, D_MODEL), lambda i: (0, layer, 0)),
                pl.BlockSpec((4, tm, GROUP), lambda i: (0, i, 0)),
                pl.BlockSpec((4, tm, GROUP), lambda i: (0, i, 0))]
    out_specs = [pl.BlockSpec((tm, D_MODEL), lambda i: (i, 0)), pl.BlockSpec((1, D_MODEL), lambda i: (0, 0))]
    out_shape = [_sds((t, D_MODEL)), _sds((1, D_MODEL))]
    operands = [dres, h, g, gathered, da, db]
    scratch = []
    if exchange is not None:
        body = _with_exchange(body, 6, 2, exchange, lambda: pl.program_id(0), t // tm)
        xi, xo, xs, scratch, xop = _exchange_args(exchange)
        in_specs, out_specs, out_shape, operands = in_specs + xi, out_specs + xo, out_shape + xs, operands + xop
    return pl.pallas_call(
        body, name="inproj_bwd_dx" if exchange is None else "inproj_bwd_dx_exchange", grid=(t // tm,),
        in_specs=in_specs, out_specs=out_specs, out_shape=out_shape, scratch_shapes=scratch,
        compiler_params=_params(("arbitrary",)),
    )(*operands)


def _inproj_bwd_dw(h, g, da, db):
    t = h.shape[0]
    tm = _row_tile(t, 512)

    def body(h_ref, g_ref, da_ref, db_ref, dw_ref):
        @pl.when(pl.program_id(0) == 0)
        def _():
            dw_ref[...] = jnp.zeros_like(dw_ref)

        xh, _ = _rms(h_ref[...])
        u = _bf(xh * g_ref[...])
        for i in range(8):
            dw_ref[i // 2, :, pl.ds((i % 2) * GROUP, GROUP)] += _dot_tn(u, da_ref[i] if i < 4 else db_ref[i - 4])

    return pl.pallas_call(
        body, name="inproj_bwd_dw", grid=(t // tm,),
        in_specs=[pl.BlockSpec((tm, D_MODEL), lambda i: (i, 0)),
                  pl.BlockSpec((1, D_MODEL), lambda i: (0, 0)),
                  pl.BlockSpec((4, tm, GROUP), lambda i: (0, i, 0)),
                  pl.BlockSpec((4, tm, GROUP), lambda i: (0, i, 0))],
        out_specs=pl.BlockSpec((N_SHARD, D_MODEL, D_MODEL), lambda i: (0, 0, 0)),
        out_shape=_sds((N_SHARD, ROWS_LAYER, D_MODEL)), compiler_params=_params(("arbitrary",), VMEM_LIMIT_BIG),
    )(h, g, da, db)


def _hgrn_consts():
    c, nl = HG_CHUNK, HG_LEVELS
    t = np.arange(c)
    tril = np.tril(np.ones((c, c), np.float32))
    masks = np.zeros((nl + 1, c, c), np.float32)
    masks[0] = np.eye(c, dtype=np.float32)
    dmat = np.zeros(((nl + 2) * c, c), np.float32)
    dmat[0:c] = tril
    for l in range(nl):
        m = c >> (l + 1)
        blk = t // (2 * m)
        r = blk * 2 * m + m - 1
        upper = (t % (2 * m)) >= m
        masks[l + 1] = ((blk[:, None] == blk[None, :]) & upper[:, None] & (~upper)[None, :]).astype(np.float32)
        dmat[(l + 1) * c:(l + 2) * c] = tril[t] - tril[r]
    dmat[(nl + 1) * c:] = np.triu(np.ones((c, c), np.float32), k=1)
    return jnp.asarray(masks), jnp.asarray(dmat, BF16)


HG_HEADS = 4


def _hgrn_pre(aq, af, lb):
    sq = _sigmoid(aq)
    sneg = _sigmoid(-af)
    kk = (1.0 - lb) * sneg
    return sq, aq * sq, sneg, kk, jnp.log1p(-kk)


def _hgrn_x(logf, dmat_ref):
    dm = dmat_ref[pl.ds(0, (HG_LEVELS + 1) * HG_CHUNK), :]
    lhi, llo = _split(logf)
    return _dot(dm, lhi) + _dot(dm, llo)


def _hgrn_level(x_all, l, q, kk):
    c = HG_CHUNK
    x = x_all[(l + 1) * c:(l + 2) * c]
    qf = jnp.exp(jnp.minimum(x, 0.0))
    kf = jnp.exp(-jnp.maximum(x, 0.0))
    return qf, kf, _bf(q * qf), _bf(kk * kf)


def _hgrn_scores(xs, qs, kks, mask_ref):
    ps = [mask_ref[0] * _dot_nt(_bf(q), _bf(kk)) for q, kk in zip(qs, kks)]
    for l in range(HG_LEVELS):
        for i, (x_all, q, kk) in enumerate(zip(xs, qs, kks)):
            _, _, ql, kl = _hgrn_level(x_all, l, q, kk)
            ps[i] = ps[i] + mask_ref[l + 1] * _dot_nt(ql, kl)
    return ps


def _hgrn_specs(n_chunks, rev):
    c, w = HG_CHUNK, HG_HEADS * A_D
    cidx = (lambda n: n_chunks - 1 - n) if rev else (lambda n: n)
    col = lambda g: pl.BlockSpec((c, w), lambda h, n: (cidx(n), g * (A_HEADS // HG_HEADS) + h))
    vec = pl.BlockSpec((1, w), lambda h, n: (0, h))
    mask = pl.BlockSpec((HG_LEVELS + 1, c, c), lambda h, n: (0, 0, 0))
    dmat = pl.BlockSpec(((HG_LEVELS + 2) * c, c), lambda h, n: (0, 0))
    state = pl.BlockSpec((HG_HEADS, None, A_D, A_D), lambda h, n: (h, cidx(n), 0, 0))
    return cidx, col, vec, mask, dmat, state


def _lanes(i):
    return pl.ds(i * A_D, A_D)


def _hgrn_fwd(proj, lb, gain):
    t = proj.shape[0]
    c = HG_CHUNK
    nch = t // c
    masks, dmat = _hgrn_consts()
    cidx, col, vec, mask_spec, dmat_spec, state_spec = _hgrn_specs(nch, False)
    heads = range(HG_HEADS)

    def body(aq_ref, af_ref, ai_ref, ag_ref, lb_ref, gain_ref, mask_ref, dmat_ref, y_ref, st_ref, s_scr):
        @pl.when(pl.program_id(1) == 0)
        def _():
            s_scr[...] = jnp.zeros_like(s_scr)

        pre = [_hgrn_pre(aq_ref[:, _lanes(i)], af_ref[:, _lanes(i)], lb_ref[:, _lanes(i)]) for i in heads]
        qs, kks = [p[1] for p in pre], [p[3] for p in pre]
        xs = [_hgrn_x(p[4], dmat_ref) for p in pre]
        bs = [x[0:c] for x in xs]
        b_lasts = [jnp.sum(p[4], axis=0, keepdims=True) for p in pre]
        ps = _hgrn_scores(xs, qs, kks, mask_ref)
        ss = [s_scr[i] for i in heads]
        vbs = [_bf(ai_ref[:, _lanes(i)]) for i in heads]
        os_ = [_dot(_bf(ps[i]), vbs[i]) + _dot_nt(_bf(qs[i] * jnp.exp(bs[i])), _bf(ss[i])) for i in heads]
        for i in heads:
            st_ref[i] = ss[i]
            s_scr[i] = ss[i] * jnp.exp(b_lasts[i]) + _dot_tn(vbs[i], _bf(kks[i] * jnp.exp(b_lasts[i] - bs[i])))
            oh, _ = _rms(os_[i])
            ag = ag_ref[:, _lanes(i)]
            y_ref[:, _lanes(i)] = oh * gain_ref[:, _lanes(i)] * (ag * _sigmoid(ag))

    return pl.pallas_call(
        body, name="hgrn_fwd", grid=(A_HEADS // HG_HEADS, nch),
        in_specs=[col(0), col(1), col(2), col(3), vec, vec, mask_spec, dmat_spec],
        out_specs=[pl.BlockSpec((c, HG_HEADS * A_D), lambda h, n: (n, h)), state_spec],
        out_shape=[_sds((t, GROUP)), _sds((A_HEADS, nch, A_D, A_D))],
        scratch_shapes=[pltpu.VMEM((HG_HEADS, A_D, A_D), F32)],
        compiler_params=_params(("arbitrary", "arbitrary")),
    )(proj, proj, proj, proj, lb, gain, masks, dmat)


def _hgrn_bwd(proj, lb, gain, states, dya):
    t = proj.shape[0]
    c, nl = HG_CHUNK, HG_LEVELS
    nch = t // c
    masks, dmat = _hgrn_consts()
    cidx, col, vec, mask_spec, dmat_spec, state_spec = _hgrn_specs(nch, True)
    heads = range(HG_HEADS)

    def body(aq_ref, af_ref, ai_ref, ag_ref, lb_ref, gain_ref, mask_ref, dmat_ref, st_ref, dy_ref,
             da_ref, dlb_ref, dgain_ref, ds_scr, z_scr):
        @pl.when(pl.program_id(1) == 0)
        def _():
            ds_scr[...] = jnp.zeros_like(ds_scr)
            dlb_ref[...] = jnp.zeros_like(dlb_ref)
            dgain_ref[...] = jnp.zeros_like(dgain_ref)

        aqs = [aq_ref[:, _lanes(i)] for i in heads]
        lbs = [lb_ref[:, _lanes(i)] for i in heads]
        pre = [_hgrn_pre(aqs[i], af_ref[:, _lanes(i)], lbs[i]) for i in heads]
        sqs, qs, snegs, kks = ([p[j] for p in pre] for j in range(4))
        xs = [_hgrn_x(p[4], dmat_ref) for p in pre]
        bs = [x[0:c] for x in xs]
        b_lasts = [jnp.sum(p[4], axis=0, keepdims=True) for p in pre]
        ebs = [jnp.exp(b) for b in bs]
        ebls = [jnp.exp(bl - b) for bl, b in zip(b_lasts, bs)]
        ebl_rows = [jnp.exp(bl) for bl in b_lasts]
        qes = [_bf(q * eb) for q, eb in zip(qs, ebs)]
        kes = [_bf(kk * ebl) for kk, ebl in zip(kks, ebls)]
        vbs = [_bf(ai_ref[:, _lanes(i)]) for i in heads]
        ss = [st_ref[i] for i in heads]
        sbs = [_bf(s) for s in ss]
        dss = [ds_scr[i] for i in heads]
        dsbs = [_bf(ds) for ds in dss]

        pbs = [_bf(p) for p in _hgrn_scores(xs, qs, kks, mask_ref)]
        os_ = [_dot(pbs[i], vbs[i]) + _dot_nt(qes[i], sbs[i]) for i in heads]

        dos = []
        for i in heads:
            ag, gain, dy = ag_ref[:, _lanes(i)], gain_ref[:, _lanes(i)], dy_ref[:, _lanes(i)]
            oh, r = _rms(os_[i])
            sg_sig = _sigmoid(ag)
            sg = ag * sg_sig
            da_ref[3, :, _lanes(i)] = _bf(dy * oh * gain * _silu_grad(ag, sg_sig))
            dgain_ref[:, _lanes(i)] += jnp.sum(dy * oh * sg, axis=0, keepdims=True)
            dos.append(_bf(_rms_bwd(dy * gain * sg, oh, r)))

        dps = [_dot_nt(dos[i], vbs[i]) for i in heads]
        for i in heads:
            da_ref[2, :, _lanes(i)] = _bf(_dot_tn(pbs[i], dos[i]) + _dot_nt(kes[i], dsbs[i]))
        dq_ss = [ebs[i] * _dot(dos[i], sbs[i]) for i in heads]
        dk_ss = [ebls[i] * _dot(vbs[i], dsbs[i]) for i in heads]
        dqs, dks = [], []
        for i in heads:
            dpd = jnp.sum(mask_ref[0] * dps[i], axis=1, keepdims=True)
            z_scr[i, pl.ds(0, c), :] = qs[i] * dq_ss[i]
            z_scr[i, pl.ds((nl + 1) * c, c), :] = kks[i] * dk_ss[i]
            dqs.append(dq_ss[i] + dpd * kks[i])
            dks.append(dk_ss[i] + dpd * qs[i])
        for l in range(nl):
            for i in heads:
                qf, kf, ql, kl = _hgrn_level(xs[i], l, qs[i], kks[i])
                dpl = _bf(mask_ref[l + 1] * dps[i])
                dq_l = qf * _dot(dpl, kl)
                dk_l = kf * _dot_tn(dpl, ql)
                z_scr[i, pl.ds((l + 1) * c, c), :] = qs[i] * dq_l - kks[i] * dk_l
                dqs[i] = dqs[i] + dq_l
                dks[i] = dks[i] + dk_l

        zsplits = [_split(z_scr[i]) for i in heads]
        dlogfs = [_dot_tn(dmat_ref[...], zhi) + _dot_tn(dmat_ref[...], zlo) for zhi, zlo in zsplits]
        ds_new = [_dot_tn(dos[i], qes[i]) for i in heads]
        for i in heads:
            dlogf = dlogfs[i] + ebl_rows[i] * jnp.sum(dss[i] * ss[i], axis=0, keepdims=True)
            dkk = dks[i] - dlogf / (1.0 - kks[i])
            da_ref[1, :, _lanes(i)] = _bf(dkk * (1.0 - lbs[i]) * (-(snegs[i] * (1.0 - snegs[i]))))
            dlb_ref[:, _lanes(i)] += jnp.sum(dkk * (-snegs[i]), axis=0, keepdims=True)
            da_ref[0, :, _lanes(i)] = _bf(dqs[i] * _silu_grad(aqs[i], sqs[i]))
            ds_scr[i] = dss[i] * ebl_rows[i] + ds_new[i]

    w = HG_HEADS * A_D
    return pl.pallas_call(
        body, name="hgrn_bwd", grid=(A_HEADS // HG_HEADS, nch),
        in_specs=[col(0), col(1), col(2), col(3), vec, vec, mask_spec, dmat_spec, state_spec,
                  pl.BlockSpec((c, w), lambda h, n: (cidx(n), h))],
        out_specs=[pl.BlockSpec((4, c, w), lambda h, n: (0, cidx(n), h)), vec, vec],
        out_shape=[_sds((4, t, GROUP), BF16)] + [_sds((1, GROUP))] * 2,
        scratch_shapes=[pltpu.VMEM((HG_HEADS, A_D, A_D), F32), pltpu.VMEM((HG_HEADS, (nl + 2) * c, A_D), F32)],
        compiler_params=_params(("arbitrary", "arbitrary")),
    )(proj, proj, proj, proj, lb, gain, masks, dmat, states, dya)


def _sb_consts():
    j = np.arange(SB_TK)
    strict = (j[:, None] > j[None, :]).astype(np.float32)
    lower = (j[:, None] < j[None, :]).astype(np.float32)
    return jnp.asarray(strict, BF16), jnp.asarray(lower, BF16)


def _lane(x, k):
    return jnp.broadcast_to(x[:, k:k + 1], x.shape)


def _key_lane(kb):
    return lax.broadcasted_iota(jnp.int32, (1, SB_TK), 1) == kb


def _causal(x, masked):
    if not masked:
        return x
    n = (SB_TK, SB_TK)
    top = jnp.where(lax.broadcasted_iota(jnp.int32, n, 1) < lax.broadcasted_iota(jnp.int32, n, 0), x[:SB_TK], 0.0)
    return top if x.shape[0] == SB_TK else jnp.concatenate([top, x[SB_TK:]], axis=0)


def _sb_softplus(z, masked):
    logsig = jnp.minimum(z, 0.0) - jnp.log2(1.0 + jnp.exp2(-jnp.abs(z)))
    return _causal(z - logsig, masked), logsig


def _sb_sweep(qi, group_fn, state, group, ascending):
    nd = SB_TQ // SB_TK

    def tile(kb, r0, masked):
        return (pl.multiple_of(kb * SB_TK, SB_TK), r0, masked, kb)

    def run(tiles, st):
        for i in range(0, len(tiles), group):
            st = group_fn(tiles[i:i + group], st)
        return st

    diag = [tile(qi * nd + d, d * SB_TK, True) for d in range(nd)]
    if ascending:
        state = lax.fori_loop(0, qi, lambda j, st: run([tile(j * nd + g, 0, False) for g in range(nd)], st), state)
        return run(diag, state)
    state = run(diag[::-1], state)
    return lax.fori_loop(
        0, qi, lambda j, st: run([tile((qi - j) * nd - 1 - g, 0, False) for g in range(nd)], st), state)


def _set_rows(r0, full, new):
    return new if r0 == 0 else jnp.concatenate([full[:r0], new], axis=0)


def _sb_specs(t, tq):
    col = lambda g: pl.BlockSpec((tq, 2 * B_D), lambda p, i, h: (i, g * (GROUP // (2 * B_D)) + p))
    full = lambda g: pl.BlockSpec((t, 2 * B_D), lambda p, i, h: (0, g * (GROUP // (2 * B_D)) + p))
    vec = pl.BlockSpec((1, 2 * B_D), lambda p, i, h: (0, p))
    mat = pl.BlockSpec((SB_TK, SB_TK), lambda p, i, h: (0, 0))
    car = pl.BlockSpec((None, tq, SB_TK), lambda p, i, h: (2 * p + h, i, 0))
    return col, full, vec, mat, car


def _head_lanes(h):
    return (lax.broadcasted_iota(jnp.int32, (1, 2 * B_D), 1) >= B_D) == (h == 1)


def _put(ref, h, val):
    @pl.when(h == 0)
    def _():
        ref[...] = val

    @pl.when(h == 1)
    def _():
        ref[...] += val


def _sb_fwd(qkv, bgate, gain, gather=None):
    t = qkv.shape[0]
    tq = SB_TQ
    strict, _ = _sb_consts()
    n_steps = (B_HEADS // 2, t // tq, 2)

    def body(q_ref, k_ref, v_ref, bg_ref, gain_ref, m_ref, *rest):
        if gather is None:
            o_ref, y_ref, car_ref = rest
        else:
            flat_ref, _, o_ref, y_ref, car_ref, gathered_ref, send_sems, recv_sems = rest
            start, forward, finish = _gather_plan(flat_ref, gathered_ref, send_sems, recv_sems, *gather[2:])
            step = (pl.program_id(0) * n_steps[1] + pl.program_id(1)) * n_steps[2] + pl.program_id(2)
            pl.when(step == 0)(start)
            pl.when(step == 2 * n_steps[1] * n_steps[2])(forward)
            pl.when(step == n_steps[0] * n_steps[1] * n_steps[2] - 1)(finish)
        h = pl.program_id(2)
        lanes = _head_lanes(h)
        qb = jnp.where(lanes, q_ref[...], jnp.zeros_like(q_ref))
        cmat = m_ref[...]

        def group(tiles, state):
            carry, acc, cars = state
            kv = [(k_ref[pl.ds(off, SB_TK), :], v_ref[pl.ds(off, SB_TK), :]) for off, _, _, _ in tiles]
            zs = [_dot_nt(qb[r0:], kb) for (_, r0, _, _), (kb, _) in zip(tiles, kv)]
            sps = [_sb_softplus(z, masked) for z, (_, _, masked, _) in zip(zs, tiles)]
            css = [_dot(_bf(sp), cmat) for sp, _ in sps]
            ws = []
            for (sp, logsig), cs, (_, r0, masked, kb) in zip(sps, css, tiles):
                ws.append(_bf(_causal(jnp.exp2(logsig - cs - carry[r0:]), masked)))
                cars = _set_rows(r0, cars, jnp.where(_key_lane(kb), carry[r0:], cars[r0:]))
                carry = _set_rows(r0, carry, carry[r0:] + _lane(cs + sp, 0))
            for w, (_, vb), (_, r0, _, _) in zip(ws, kv, tiles):
                acc = _set_rows(r0, acc, acc[r0:] + _dot(w, vb))
            return carry, acc, cars

        zero = jnp.zeros((tq, SB_TK), F32)
        _, acc, cars = _sb_sweep(pl.program_id(1), group, (zero, jnp.zeros((tq, 2 * B_D), F32), zero),
                                 SB_GROUP_FWD, False)
        car_ref[...] = cars
        o = jnp.where(lanes, acc, 0.0)
        oh = o * lax.rsqrt(jnp.sum(o * o, axis=-1, keepdims=True) * (1.0 / B_D) + EPS)
        bg = bg_ref[...]
        _put(o_ref, h, o)
        _put(y_ref, h, oh * gain_ref[...] * (bg * _sigmoid(bg)))

    col, full, vec, mat, car = _sb_specs(t, tq)
    out = pl.BlockSpec((tq, 2 * B_D), lambda p, i, h: (i, p))
    in_specs = [col(0), full(1), full(2), col(0), vec, mat]
    out_specs = [out, out, car]
    out_shape = [_sds((t, GROUP)), _sds((t, GROUP)), _sds((B_HEADS, t, SB_TK))]
    operands = [qkv, qkv, qkv, bgate, gain, strict]
    extra = {}
    if gather is not None:
        in_specs += [_ANY, _ANY]
        out_specs += [_ANY]
        out_shape += [_sds(gather[1].shape, gather[1].dtype)]
        operands += [gather[0], gather[1]]
        extra = dict(input_output_aliases={7: 3}, scratch_shapes=_gather_sems(gather[4]))
    return pl.pallas_call(
        body, name="sb_fwd" if gather is None else "sb_fwd_gather", grid=n_steps,
        in_specs=in_specs, out_specs=out_specs, out_shape=out_shape,
        compiler_params=_params(("arbitrary", "arbitrary", "arbitrary")), **extra,
    )(*operands)


def _sb_bwd(qkv, bgate, o, carries, dy, gain, exchange=None):
    t = qkv.shape[0]
    tq = SB_TQ
    strict, lower = _sb_consts()

    def body(q_ref, k_ref, v_ref, bg_ref, o_ref, car_ref, dy_ref, gain_ref, ms_ref, ml_ref,
             dq_ref, dk_ref, dv_ref, dbg_ref, dgain_ref):
        qi = pl.program_id(1)
        h = pl.program_id(2)
        lanes = _head_lanes(h)

        @pl.when((qi == 0) & (h == 0))
        def _():
            dk_ref[...] = jnp.zeros_like(dk_ref)
            dv_ref[...] = jnp.zeros_like(dv_ref)
            dgain_ref[...] = jnp.zeros_like(dgain_ref)

        qb = jnp.where(lanes, q_ref[...], jnp.zeros_like(q_ref))
        cmat = ms_ref[...]
        lmat = ml_ref[...]
        cars = car_ref[...]
        o = jnp.where(lanes, o_ref[...], 0.0)
        dy = jnp.where(lanes, dy_ref[...], 0.0)
        bg = bg_ref[...]
        gain = gain_ref[...]
        r = lax.rsqrt(jnp.sum(o * o, axis=-1, keepdims=True) * (1.0 / B_D) + EPS)
        oh = o * r
        sig = _sigmoid(bg)
        sg = bg * sig
        _put(dbg_ref, h, dy * oh * gain * _silu_grad(bg, sig))
        dgain_ref[...] += jnp.sum(dy * oh * sg, axis=0, keepdims=True)
        doh = dy * gain * sg
        do = _bf(r * (doh - oh * (jnp.sum(doh * oh, axis=-1, keepdims=True) * (1.0 / B_D))))

        def group(tiles, state):
            gleft, dq = state
            kv = [(k_ref[pl.ds(off, SB_TK), :], v_ref[pl.ds(off, SB_TK), :]) for off, _, _, _ in tiles]
            zs = [_dot_nt(qb[r0:], kb) for (_, r0, _, _), (kb, _) in zip(tiles, kv)]
            dws = [_dot_nt(do[r0:], vb) for (_, r0, _, _), (_, vb) in zip(tiles, kv)]
            sps = [_sb_softplus(z, masked) for z, (_, _, masked, _) in zip(zs, tiles)]
            css = [_dot(_bf(sp), cmat) for sp, _ in sps]
            ws, gs = [], []
            for (_, logsig), cs, dw, (_, r0, masked, kb) in zip(sps, css, dws, tiles):
                right = jnp.sum(jnp.where(_key_lane(kb), cars[r0:], 0.0), axis=1, keepdims=True)
                w = _causal(jnp.exp2(logsig - cs - right), masked)
                ws.append(_bf(w))
                gs.append(dw * w)
            gps = [_dot(_bf(g), lmat) for g in gs]
            dzs = []
            for (_, logsig), g, gp, (_, r0, masked, _) in zip(sps, gs, gps, tiles):
                dz = g - jnp.exp2(logsig) * (g + gleft[r0:] + gp)
                dzs.append(_bf(_causal(dz, masked)))
                gleft = _set_rows(r0, gleft, gleft[r0:] + _lane(gp + g, SB_TK - 1))
            for dz, wb, (kb, _), (off, r0, _, _) in zip(dzs, ws, kv, tiles):
                dq = _set_rows(r0, dq, dq[r0:] + _dot(dz, kb))
                dk_ref[pl.ds(off, SB_TK), :] += _dot_tn(dz, qb[r0:])
                dv_ref[pl.ds(off, SB_TK), :] += _dot_tn(wb, do[r0:])
            return gleft, dq

        _, dq = _sb_sweep(qi, group, (jnp.zeros((tq, SB_TK), F32), jnp.zeros((tq, 2 * B_D), F32)), SB_GROUP_BWD, True)
        _put(dq_ref, h, jnp.where(lanes, dq * (B_D ** -0.5), 0.0))

    col, full, vec, mat, car = _sb_specs(t, tq)
    blk = pl.BlockSpec((tq, 2 * B_D), lambda p, i, h: (i, p))
    whole = pl.BlockSpec((t, 2 * B_D), lambda p, i, h: (0, p))
    grid = (B_HEADS // 2, t // tq, 2)
    in_specs = [col(0), full(1), full(2), col(0), blk, car, blk, vec, mat, mat]
    out_specs = [blk, whole, whole, blk, vec]
    out_shape = [_sds((t, GROUP))] * 4 + [_sds((1, GROUP))]
    operands = [qkv, qkv, qkv, bgate, o, carries, dy, gain, strict, lower]
    scratch = []
    if exchange is not None:
        step_of = lambda: (pl.program_id(0) * grid[1] + pl.program_id(1)) * grid[2] + pl.program_id(2)
        body = _with_exchange(body, 10, 5, exchange, step_of, grid[0] * grid[1] * grid[2])
        xi, xo, xs, scratch, xop = _exchange_args(exchange)
        in_specs, out_specs, out_shape, operands = in_specs + xi, out_specs + xo, out_shape + xs, operands + xop
    return pl.pallas_call(
        body, name="sb_bwd" if exchange is None else "sb_bwd_exchange", grid=grid,
        in_specs=in_specs, out_specs=out_specs, out_shape=out_shape, scratch_shapes=scratch,
        compiler_params=_params(("arbitrary", "arbitrary", "arbitrary")),
    )(*operands)


def _adamw(w, g, m, v):
    rows, cols = w.shape
    tr = rows
    for cand in (400, 256, 128, 64, 32, 16, 8):
        if rows % cand == 0:
            tr = cand
            break

    def body(w_ref, g_ref, m_ref, v_ref, d_ref, nm_ref, nv_ref):
        g_ = g_ref[...]
        m_ = ADAM_B1 * m_ref[...] + (1.0 - ADAM_B1) * g_
        v_ = ADAM_B2 * v_ref[...] + (1.0 - ADAM_B2) * (g_ * g_)
        m_hat = m_ / (1.0 - ADAM_B1 ** ADAM_STEP)
        v_hat = v_ / (1.0 - ADAM_B2 ** ADAM_STEP)
        d_ref[...] = -ADAM_LR * (m_hat / (jnp.sqrt(v_hat) + ADAM_EPS) + ADAM_WD * w_ref[...])
        nm_ref[...] = m_
        nv_ref[...] = v_

    spec = pl.BlockSpec((tr, cols), lambda i: (i, 0))
    return pl.pallas_call(
        body, name="adamw", grid=(rows // tr,), in_specs=[spec] * 4, out_specs=[spec] * 3,
        out_shape=[_sds((rows, cols))] * 3, compiler_params=_params(("arbitrary",)),
    )(w, g, m, v)


_ANY = pl.BlockSpec(memory_space=pl.ANY)


def _place():
    return lax.axis_index("x"), lax.axis_index("y"), lax.axis_index("c")


def _gather_plan(x_ref, out_ref, send_sems, recv_sems, row0, nrows, nc):
    x, y, c = _place()
    me = 2 * x + y
    sibling = (x, y, 1 - c)
    half = nrows // 2
    ch = half // nc
    peers = [me ^ k for k in (1, 2, 3)]

    def rows(shard, hc, r):
        return out_ref.at[shard, pl.ds(row0 + hc * half + r * ch, ch), :]

    def copy(k, shard, hc, r, to, src=None):
        return pltpu.make_async_remote_copy(
            src_ref=rows(shard, hc, r) if src is None else src, dst_ref=rows(shard, hc, r),
            send_sem=send_sems.at[k * nc + r], recv_sem=recv_sems.at[k * nc + r], device_id=to, device_id_type=MESH)

    def first(k, p, r):
        return copy(k, me, c, r, (p >> 1, p & 1, c), src=x_ref.at[pl.ds(row0 + c * half + r * ch, ch), :])

    def start():
        for k, p in enumerate(peers):
            for r in range(nc):
                first(k, p, r).start()

    def forward():
        for k, p in enumerate(peers):
            for r in range(nc):
                copy(k, p, c, r, sibling).wait_recv()
                copy(3 + k, p, c, r, sibling).start()

    def finish():
        for k, p in enumerate(peers):
            for r in range(nc):
                copy(3 + k, p, 1 - c, r, sibling).wait_recv()
        for k, p in enumerate(peers):
            for r in range(nc):
                first(k, p, r).wait_send()
                copy(3 + k, p, c, r, sibling).wait_send()

    return start, forward, finish


def _gather_sems(nc):
    return [pltpu.SemaphoreType.DMA((6 * nc,)), pltpu.SemaphoreType.DMA((6 * nc,))]


def _swap_plan(g_ref, out_ref, send_sems, recv_sems):
    half, ch, nc = HALF_LAYER, CHUNK_ROWS, RS_CHUNKS

    def copies():
        x, y, c = _place()
        return [pltpu.make_async_remote_copy(
            src_ref=g_ref.at[j, pl.ds((1 - c) * half + r * ch, ch), :], dst_ref=out_ref.at[j, pl.ds(r * ch, ch), :],
            send_sem=send_sems.at[j * nc + r], recv_sem=recv_sems.at[j * nc + r],
            device_id=(x, y, 1 - c), device_id_type=MESH) for j in range(N_SHARD) for r in range(nc)]

    def start():
        for cp in copies():
            cp.start()

    def finish():
        for cp in copies():
            cp.wait()

    return start, finish


def _scatter_plan(p_ref, out_ref, send_sems, recv_sems):
    ch, nc = CHUNK_ROWS, RS_CHUNKS

    def copies():
        x, y, c = _place()
        me = 2 * x + y
        return [pltpu.make_async_remote_copy(
            src_ref=p_ref.at[me ^ k, pl.ds(r * ch, ch), :], dst_ref=out_ref.at[k - 1, pl.ds(r * ch, ch), :],
            send_sem=send_sems.at[(k - 1) * nc + r], recv_sem=recv_sems.at[(k - 1) * nc + r],
            device_id=((me ^ k) >> 1, (me ^ k) & 1, c), device_id_type=MESH) for k in (1, 2, 3) for r in range(nc)]

    def start():
        for cp in copies():
            cp.start()

    def finish():
        for cp in copies():
            cp.wait()

    return start, finish


SWAP = (_swap_plan, (N_SHARD, HALF_LAYER, D_MODEL), F32, N_SHARD * RS_CHUNKS)
SCATTER = (_scatter_plan, (3, HALF_LAYER, D_MODEL), BF16, 3 * RS_CHUNKS)


def _exchange_call(kind, operand):
    plan, shape, dtype, n_sems = kind

    def body(in_ref, out_ref, send_sems, recv_sems):
        start, finish = plan(in_ref, out_ref, send_sems, recv_sems)
        start()
        finish()

    return pl.pallas_call(
        body, name="exchange", in_specs=[_ANY], out_specs=_ANY, out_shape=_sds(shape, dtype),
        scratch_shapes=[pltpu.SemaphoreType.DMA((n_sems,)), pltpu.SemaphoreType.DMA((n_sems,))],
    )(operand)


def _with_exchange(body, n_in, n_out, exchange, step_of, n_steps):
    def wrapped(*refs):
        ins, src = refs[:n_in], refs[n_in]
        outs, dst = refs[n_in + 1:n_in + 1 + n_out], refs[n_in + 1 + n_out]
        send_sems, recv_sems = refs[n_in + 2 + n_out:n_in + 4 + n_out]
        start, finish = exchange[0][0](src, dst, send_sems, recv_sems)
        pl.when(step_of() == 0)(start)
        body(*ins, *outs, *refs[n_in + 4 + n_out:])
        pl.when(step_of() == n_steps - 1)(finish)

    return wrapped


def _exchange_args(exchange):
    (plan, shape, dtype, n_sems), operand = exchange
    sems = [pltpu.SemaphoreType.DMA((n_sems,)), pltpu.SemaphoreType.DMA((n_sems,))]
    return [_ANY], [_ANY], [_sds(shape, dtype)], sems, [operand]


def _gather_weights(flat, row0, nrows, nc):
    def body(x_ref, out_ref, send_sems, recv_sems):
        start, forward, finish = _gather_plan(x_ref, out_ref, send_sems, recv_sems, row0, nrows, nc)
        start()
        forward()
        finish()

    return pl.pallas_call(
        body, name="gather_weights", in_specs=[_ANY], out_specs=_ANY,
        out_shape=_sds((N_SHARD, ROWS_FLAT, D_MODEL), BF16), scratch_shapes=_gather_sems(nc),
    )(flat)


def _add_my_half(grads, recv):
    tr = 400
    nb = HALF_LAYER // tr
    core = lax.axis_index("c").astype(jnp.int32).reshape(1)

    def body(c_ref, g_ref, r_ref, o_ref, ob_ref):
        acc = g_ref[...] + r_ref[...]
        o_ref[...] = acc
        ob_ref[...] = _bf(acc)

    out = pl.BlockSpec((None, tr, D_MODEL), lambda j, i, c_ref: (j, i, 0))
    return pl.pallas_call(
        body, name="add_my_half",
        grid_spec=pltpu.PrefetchScalarGridSpec(
            num_scalar_prefetch=1, grid=(N_SHARD, nb),
            in_specs=[pl.BlockSpec((None, tr, D_MODEL), lambda j, i, c_ref: (j, c_ref[0] * nb + i, 0)), out],
            out_specs=[out, out]),
        out_shape=[_sds((N_SHARD, HALF_LAYER, D_MODEL)), _sds((N_SHARD, HALF_LAYER, D_MODEL), BF16)],
        compiler_params=_params(("arbitrary", "arbitrary")),
    )(core, grads, recv)


def _sum_scattered(part, recv):
    tr = 400
    chip = (2 * lax.axis_index("x") + lax.axis_index("y")).astype(jnp.int32).reshape(1)

    def body(c_ref, p_ref, r_ref, o_ref):
        acc = p_ref[...]
        for k in range(3):
            acc = acc + r_ref[k].astype(F32)
        o_ref[...] = acc

    return pl.pallas_call(
        body, name="sum_scattered",
        grid_spec=pltpu.PrefetchScalarGridSpec(
            num_scalar_prefetch=1, grid=(HALF_LAYER // tr,),
            in_specs=[pl.BlockSpec((None, tr, D_MODEL), lambda i, c_ref: (c_ref[0], i, 0)),
                      pl.BlockSpec((3, tr, D_MODEL), lambda i, c_ref: (0, i, 0))],
            out_specs=pl.BlockSpec((tr, D_MODEL), lambda i, c_ref: (i, 0))),
        out_shape=_sds((HALF_LAYER, D_MODEL)), compiler_params=_params(("arbitrary",)),
    )(chip, part, recv)


def _swap_reduced(mine):
    ch, nc = CHUNK_ROWS, RS_CHUNKS

    def body(r_ref, out_ref, send_sems, recv_sems):
        x, y, c = _place()
        copies = [pltpu.make_async_remote_copy(
            src_ref=r_ref.at[l, pl.ds(r * ch, ch), :], dst_ref=out_ref.at[l, pl.ds(r * ch, ch), :],
            send_sem=send_sems.at[l * nc + r], recv_sem=recv_sems.at[l * nc + r],
            device_id=(x, y, 1 - c), device_id_type=MESH) for l in range(2) for r in range(nc)]
        for cp in copies:
            cp.start()
        for cp in copies:
            cp.wait()

    return pl.pallas_call(
        body, name="swap_reduced", in_specs=[_ANY], out_specs=_ANY,
        out_shape=_sds((2, HALF_LAYER, D_MODEL)),
        scratch_shapes=[pltpu.SemaphoreType.DMA((2 * nc,)), pltpu.SemaphoreType.DMA((2 * nc,))],
    )(mine)


def _allreduce_small(vec):
    def body(v_ref, out_ref, buf, send_sems, recv_sems):
        x, y, c = _place()
        me = 4 * x + 2 * y + c
        buf[me] = v_ref[...]
        peers = [me ^ k for k in range(1, N_DEV)]
        sends = [pltpu.make_async_remote_copy(
            src_ref=v_ref, dst_ref=buf.at[me], send_sem=send_sems.at[k], recv_sem=recv_sems.at[k],
            device_id=(p >> 2, (p >> 1) & 1, p & 1), device_id_type=MESH) for k, p in enumerate(peers)]
        for cp in sends:
            cp.start()
        for k, p in enumerate(peers):
            pltpu.make_async_remote_copy(
                src_ref=v_ref, dst_ref=buf.at[p], send_sem=send_sems.at[k], recv_sem=recv_sems.at[k],
                device_id=(p >> 2, (p >> 1) & 1, p & 1), device_id_type=MESH).wait_recv()
        for cp in sends:
            cp.wait_send()
        acc = buf[0]
        for d in range(1, N_DEV):
            acc = acc + buf[d]
        out_ref[...] = acc

    vm = pl.BlockSpec(memory_space=pltpu.VMEM)
    return pl.pallas_call(
        body, name="allreduce_small", in_specs=[vm], out_specs=vm, out_shape=_sds((SMALL_ROWS, 128)),
        scratch_shapes=[pltpu.VMEM((N_DEV, SMALL_ROWS, 128), F32),
                        pltpu.SemaphoreType.DMA((N_DEV - 1,)), pltpu.SemaphoreType.DMA((N_DEV - 1,))],
    )(vec)


def _flatten_shard(w_in, w_out, w_pg, w_pp):
    return jnp.concatenate([w_in.reshape(-1, D_MODEL), w_out.reshape(-1, D_MODEL), w_pg.reshape(-1, D_MODEL),
                            w_pp.reshape(-1, D_MODEL)], axis=0)


def _unflatten_layers(flats):
    a, b, c = D_MODEL, D_MODEL + D_MODEL // N_SHARD, D_MODEL + 2 * (D_MODEL // N_SHARD)
    q = D_MODEL // N_SHARD
    return (jnp.stack([f[:a] for f in flats]), jnp.stack([f[a:b] for f in flats]),
            jnp.stack([f[b:c] for f in flats]), jnp.stack([f[c:].reshape(D_PLE, q) for f in flats]))


def _full_w_pp(gathered):
    c = ROWS_W_IN + ROWS_W_OUT + ROWS_W_PG
    q = D_MODEL // N_SHARD
    rpp = ROWS_W_PP // 2
    return [gathered[:, c + l * rpp:c + (l + 1) * rpp, :].reshape(N_SHARD, D_PLE, q).transpose(1, 0, 2)
            .reshape(D_PLE, D_MODEL) for l in range(2)]


def _layer_grads(dw_in, dw_out, dw_pg, dw_pp):
    q = D_MODEL // N_SHARD
    rpp = ROWS_W_PP // 2
    rest = jnp.concatenate([dw_out.reshape(N_SHARD, q, D_MODEL), dw_pg.reshape(N_SHARD, q, D_MODEL),
                            dw_pp.reshape(D_PLE, N_SHARD, q).transpose(1, 0, 2).reshape(N_SHARD, rpp, D_MODEL)], axis=1)
    return lax.dynamic_update_slice(dw_in, rest, (0, D_MODEL, 0))


def _lower_bounds(lb_logits):
    sm = jax.nn.softmax(lb_logits.astype(F32), axis=0)
    return jnp.cumsum(sm, axis=0) - sm[0:1]


def kernel(x, p, norm_mix, w_in, a_out_norm, b_out_norm, w_out, lb_logits, ple_gate_norm, w_ple_gate, w_ple_proj, ple_post_norm, final_norm, loss_target, m_norm_mix, m_w_in, m_a_out_norm, m_b_out_norm, m_w_out, m_lb_logits, m_ple_gate_norm, m_w_ple_gate, m_w_ple_proj, m_ple_post_norm, m_final_norm, v_norm_mix, v_w_in, v_a_out_norm, v_b_out_norm, v_w_out, v_lb_logits, v_ple_gate_norm, v_w_ple_gate, v_w_ple_proj, v_ple_post_norm, v_final_norm):
    t = x.shape[1]
    h0 = x.reshape(t, D_MODEL)
    target = loss_target.reshape(t, D_MODEL)
    pl_in = p.reshape(2, t, D_PLE)

    w_flat_bf = _flatten_shard(_bf(w_in), _bf(w_out), _bf(w_ple_gate), _bf(w_ple_proj))
    chip = 2 * lax.axis_index("x") + lax.axis_index("y")
    gathered = lax.dynamic_update_slice(_gather_weights(w_flat_bf, 0, D_MODEL, 2), w_flat_bf[None], (chip, 0, 0))
    lbs, lbs_vjp = jax.vjp(_lower_bounds, lb_logits)

    saved = []
    h = h0
    for l in range(2):
        g_mix = norm_mix[l].reshape(1, D_MODEL)
        lb = lbs[l].reshape(1, GROUP)
        ga = a_out_norm[l].reshape(1, GROUP)
        gb = b_out_norm[l].reshape(1, GROUP)
        proj, qkv, bgate = _inproj(h, g_mix, gathered, l)
        ya, states = _hgrn_fwd(proj, lb, ga)
        if l == 0:
            ob, yb, cars, gathered = _sb_fwd(qkv, bgate, gb, (w_flat_bf, gathered, D_MODEL, ROWS_FLAT - D_MODEL, 4))
            w_pps = _full_w_pp(gathered)
        else:
            ob, yb, cars = _sb_fwd(qkv, bgate, gb)
        h1 = _outproj(h, ya, yb, gathered, l)
        g_post = ple_post_norm[l].reshape(1, D_MODEL)
        g_gate = ple_gate_norm[l].reshape(1, D_MODEL)
        saved.append((h, proj, qkv, bgate, states, ya, yb, ob, cars, h1))
        if l == 0:
            h = _ple_fwd(h1, pl_in[l], w_pps[l], gathered, l, g_post, g_gate)
        else:
            dh, d_final, loss_part = _ple_fwd_final(h1, pl_in[l], w_pps[l], gathered, l, g_post, g_gate,
                                                    final_norm.reshape(1, D_MODEL), target)

    g_layer, chip_sum, scattered = [None] * 2, [None] * 2, [None] * 2
    d_mix, d_a, d_b, d_lb, d_gate, d_post = [None] * 2, [None] * 2, [None] * 2, [None] * 2, [None] * 2, [None] * 2
    for l in (1, 0):
        h_in, proj, qkv, bgate, states, ya, yb, ob, cars, h1 = saved[l]
        g_mix = norm_mix[l].reshape(1, D_MODEL)
        lb = lbs[l].reshape(1, GROUP)
        ga = a_out_norm[l].reshape(1, GROUP)
        gb = b_out_norm[l].reshape(1, GROUP)
        g_post = ple_post_norm[l].reshape(1, D_MODEL)
        g_gate = ple_gate_norm[l].reshape(1, D_MODEL)
        if l == 1:
            dh1, dw_pg, dw_pp, d_gate[l], d_post[l] = _ple_bwd(dh, h1, pl_in[l], w_pps[l], gathered, l, g_post, g_gate)
            dya, dyb, dw_out = _outproj_bwd(dh1, ya, yb, gathered, l)
            dbq, dbk, dbv, dbg, d_b[l] = _sb_bwd(qkv, bgate, ob, cars, dyb, gb)
        else:
            dh1, dw_pg, dw_pp, d_gate[l], d_post[l], from_sibling = _ple_bwd(
                dh, h1, pl_in[l], w_pps[l], gathered, l, g_post, g_gate, (SWAP, g_layer[1]))
            chip_sum[1], chip_sum_bf = _add_my_half(g_layer[1], from_sibling)
            dya, dyb, dw_out = _outproj_bwd(dh1, ya, yb, gathered, l)
            dbq, dbk, dbv, dbg, d_b[l], scattered[1] = _sb_bwd(qkv, bgate, ob, cars, dyb, gb, (SCATTER, chip_sum_bf))
        da, d_lb[l], d_a[l] = _hgrn_bwd(proj, lb, ga, states, dya)
        db = jnp.stack([dbq, dbk * LN2, dbv, dbg]).astype(BF16)
        g_layer[l] = _layer_grads(_inproj_bwd_dw(h_in, g_mix, da, db), dw_out, dw_pg, dw_pp)
        if l == 1:
            dh, d_mix[l] = _inproj_bwd_dx(dh1, h_in, g_mix, gathered, l, da, db)
        else:
            dh, d_mix[l], from_sibling = _inproj_bwd_dx(dh1, h_in, g_mix, gathered, l, da, db, (SWAP, g_layer[0]))
    grad_x = dh.reshape(x.shape)

    chip_sum[0], chip_sum_bf = _add_my_half(g_layer[0], from_sibling)
    scattered[0] = _exchange_call(SCATTER, chip_sum_bf)
    mine = jnp.stack([_sum_scattered(chip_sum[l], scattered[l]) for l in range(2)])
    other = _swap_reduced(mine)
    south = lax.axis_index("c") == 0
    g_w_in, g_w_out, g_w_pg, g_w_pp = _unflatten_layers(
        [jnp.concatenate([jnp.where(south, mine[l], other[l]), jnp.where(south, other[l], mine[l])]) for l in range(2)])

    small = jnp.concatenate([
        jnp.concatenate(d_mix).reshape(-1, 128), jnp.concatenate(d_a).reshape(-1, 128),
        jnp.concatenate(d_b).reshape(-1, 128), jnp.concatenate(d_lb).reshape(-1, 128),
        jnp.concatenate(d_gate).reshape(-1, 128), jnp.concatenate(d_post).reshape(-1, 128),
        d_final.reshape(-1, 128), jnp.broadcast_to(loss_part, (8, 128))], axis=0)
    small = _allreduce_small(small)
    loss = small[80, 0]
    g_norm_mix = small[0:16].reshape(2, D_MODEL)
    g_a = small[16:24].reshape(2, GROUP)
    g_b = small[24:32].reshape(2, GROUP)
    (g_lb,) = lbs_vjp(small[32:40].reshape(2, GROUP))
    g_gate = small[40:56].reshape(2, D_MODEL)
    g_post = small[56:72].reshape(2, D_MODEL)
    g_final = small[72:80].reshape(D_MODEL)

    def adam_matrix(w, g, m, v):
        d, nm, nv = _adamw(w.reshape(-1, D_MODEL), g.reshape(-1, D_MODEL), m.reshape(-1, D_MODEL), v.reshape(-1, D_MODEL))
        return d.reshape(w.shape), nm.reshape(w.shape), nv.reshape(w.shape)

    d_w_in, nm_w_in, nv_w_in = adam_matrix(w_in, g_w_in, m_w_in, v_w_in)
    d_w_out, nm_w_out, nv_w_out = adam_matrix(w_out, g_w_out, m_w_out, v_w_out)
    d_w_pg, nm_w_pg, nv_w_pg = adam_matrix(w_ple_gate, g_w_pg, m_w_ple_gate, v_w_ple_gate)
    d_w_pp, nm_w_pp, nv_w_pp = adam_matrix(w_ple_proj, g_w_pp, m_w_ple_proj, v_w_ple_proj)

    small_w = [norm_mix, a_out_norm, b_out_norm, lb_logits, ple_gate_norm, ple_post_norm, final_norm]
    small_g = [g_norm_mix, g_a, g_b, g_lb, g_gate, g_post, g_final]
    small_m = [m_norm_mix, m_a_out_norm, m_b_out_norm, m_lb_logits, m_ple_gate_norm, m_ple_post_norm, m_final_norm]
    small_v = [v_norm_mix, v_a_out_norm, v_b_out_norm, v_lb_logits, v_ple_gate_norm, v_ple_post_norm, v_final_norm]
    pack = lambda arrs: jnp.concatenate([a.reshape(-1, 128) for a in arrs], axis=0)
    ds, nms, nvs = _adamw(pack(small_w), pack(small_g), pack(small_m), pack(small_v))

    def unpack(packed):
        out, r = [], 0
        for a in small_w:
            n = a.size // 128
            out.append(packed[r:r + n].reshape(a.shape))
            r += n
        return out

    d_s, nm_s, nv_s = unpack(ds), unpack(nms), unpack(nvs)

    def ordered(s, big):
        return [s[0], big[0], s[1], s[2], big[1], s[3], s[4], big[2], big[3], s[5], s[6]]

    grads = ordered(small_g, [g_w_in, g_w_out, g_w_pg, g_w_pp])
    deltas = ordered(d_s, [d_w_in, d_w_out, d_w_pg, d_w_pp])
    new_m = ordered(nm_s, [nm_w_in, nm_w_out, nm_w_pg, nm_w_pp])
    new_v = ordered(nv_s, [nv_w_in, nv_w_out, nv_w_pg, nv_w_pp])
    return (loss, grad_x, *grads, *deltas, *new_m, *new_v)
```

```python
import functools
import math

import numpy as np
import jax
import jax.numpy as jnp
from jax import lax
from jax.experimental import pallas as pl
from jax.experimental.pallas import tpu as pltpu

F32 = jnp.float32
BF16 = jnp.bfloat16
MESH = pl.DeviceIdType.MESH

D_MODEL = 1024
D_PLE = 256
D_IN = 4096
A_HEADS, A_D = 4, 128
B_HEADS, B_D = 8, 64
GROUP = 512
EPS = 1e-6
N_SHARD = 4
N_DEV = 8

HG_CHUNK = 128
HG_LEVELS = 7
SB_TQ = 1024
SB_TK = 128
SB_GROUP_FWD, SB_GROUP_BWD = (2, 2), (8, 2)
LOG2E = 1.4426950408889634
LN2 = 0.6931471805599453

ADAM_LR, ADAM_B1, ADAM_B2, ADAM_EPS, ADAM_WD, ADAM_STEP = 0.001, 0.9, 0.999, 1e-08, 0.01, 10

VMEM_LIMIT = 48 * 1024 * 1024
VMEM_LIMIT_BIG = 58 * 1024 * 1024

ROWS_W_IN = 2 * D_MODEL
ROWS_W_OUT = 2 * (D_MODEL // N_SHARD)
ROWS_W_PG = 2 * (D_MODEL // N_SHARD)
ROWS_W_PP = 2 * (D_PLE * (D_MODEL // N_SHARD) // D_MODEL)
ROWS_FLAT = ROWS_W_IN + ROWS_W_OUT + ROWS_W_PG + ROWS_W_PP
HALF_FLAT = ROWS_FLAT // 2
N_CHUNK = 10
CHUNK_ROWS = HALF_FLAT // N_CHUNK

ROWS_LAYER = ROWS_FLAT // 2
HALF_LAYER = ROWS_LAYER // 2
RS_CHUNKS = HALF_LAYER // CHUNK_ROWS

SMALL_ROWS = 88


def _sds(shape, dtype=F32):
    return jax.ShapeDtypeStruct(shape, dtype)


def _params(sem=None, vmem_limit=VMEM_LIMIT):
    kw = dict(vmem_limit_bytes=vmem_limit)
    if sem is not None:
        kw["dimension_semantics"] = sem
    return pltpu.CompilerParams(**kw)


def _dot(a, b, precision=None):
    return lax.dot_general(a, b, (((1,), (0,)), ((), ())), preferred_element_type=F32, precision=precision)


def _dot_nt(a, b, precision=None):
    return lax.dot_general(a, b, (((1,), (1,)), ((), ())), preferred_element_type=F32, precision=precision)


def _dot_tn(a, b, precision=None):
    return lax.dot_general(a, b, (((0,), (0,)), ((), ())), preferred_element_type=F32, precision=precision)


def _bf(x):
    return x.astype(BF16)


def _split(x):
    hi = x.astype(BF16)
    lo = (x - hi.astype(F32)).astype(BF16)
    return hi, lo


def _rms(x):
    r = lax.rsqrt(jnp.mean(x * x, axis=-1, keepdims=True) + EPS)
    return x * r, r


def _rms_bwd(dxh, xh, r):
    return r * (dxh - xh * jnp.mean(dxh * xh, axis=-1, keepdims=True))


def _sigmoid(x):
    return 1.0 / (1.0 + jnp.exp(-x))


def _silu_grad(x, sig):
    return sig * (1.0 + x * (1.0 - sig))


def _row_tile(t, want):
    return min(t, want)


def _inproj(h, g, gathered, layer):
    t = h.shape[0]
    tm = _row_tile(t, 512)

    def body(h_ref, g_ref, w_ref, pa_ref, qkv_ref, bg_ref):
        xh, _ = _rms(h_ref[...])
        u = _bf(xh * g_ref[...])
        for j in range(8):
            acc = _dot(u, w_ref[j // 2, :, pl.ds((j % 2) * GROUP, GROUP)])
            if j < 4:
                pa_ref[:, pl.ds(j * GROUP, GROUP)] = acc
            elif j == 4:
                qkv_ref[:, pl.ds(0, GROUP)] = _bf(acc * (B_D ** -0.5 * LOG2E))
            elif j < 7:
                qkv_ref[:, pl.ds((j - 4) * GROUP, GROUP)] = _bf(acc)
            else:
                bg_ref[...] = acc

    return pl.pallas_call(
        body, name="inproj", grid=(t // tm,),
        in_specs=[pl.BlockSpec((tm, D_MODEL), lambda i: (i, 0)),
                  pl.BlockSpec((1, D_MODEL), lambda i: (0, 0)),
                  pl.BlockSpec((N_SHARD, D_MODEL, D_MODEL), lambda i: (0, layer, 0))],
        out_specs=[pl.BlockSpec((tm, 4 * GROUP), lambda i: (i, 0)), pl.BlockSpec((tm, 3 * GROUP), lambda i: (i, 0)),
                   pl.BlockSpec((tm, GROUP), lambda i: (i, 0))],
        out_shape=[_sds((t, 4 * GROUP)), _sds((t, 3 * GROUP), BF16), _sds((t, GROUP))],
        compiler_params=_params(("arbitrary",)),
    )(h, g, gathered)


def _rows_spec(first_row):
    q = D_MODEL // N_SHARD
    return pl.BlockSpec((N_SHARD, q, D_MODEL), lambda i: (0, first_row // q, 0))


def _outproj(h, ya, yb, gathered, layer):
    t = h.shape[0]
    tm = _row_tile(t, 512)

    def body(h_ref, ya_ref, yb_ref, w_ref, o_ref):
        o_ref[...] = (h_ref[...] + _dot(_bf(ya_ref[...]), w_ref[0:2].reshape(GROUP, D_MODEL))
                      + _dot(_bf(yb_ref[...]), w_ref[2:4].reshape(GROUP, D_MODEL)))

    return pl.pallas_call(
        body, name="outproj", grid=(t // tm,),
        in_specs=[pl.BlockSpec((tm, D_MODEL), lambda i: (i, 0)),
                  pl.BlockSpec((tm, GROUP), lambda i: (i, 0)),
                  pl.BlockSpec((tm, GROUP), lambda i: (i, 0)),
                  _rows_spec(ROWS_W_IN + layer * (D_MODEL // N_SHARD))],
        out_specs=pl.BlockSpec((tm, D_MODEL), lambda i: (i, 0)),
        out_shape=_sds((t, D_MODEL)), compiler_params=_params(("arbitrary",)),
    )(h, ya, yb, gathered)


def _ple_mix(x, p_ref, wpp_ref, wpg_ref, gp_ref, gg_ref):
    ph, _ = _rms(_dot(_bf(p_ref[...]), wpp_ref[...]))
    xh, _ = _rms(x)
    gate = _sigmoid(_dot(_bf(xh * gg_ref[...]), wpg_ref[...].reshape(D_MODEL, D_MODEL)))
    return x + gate * (ph * gp_ref[...])


def _ple_specs(tm, layer):
    return [pl.BlockSpec((tm, D_MODEL), lambda i: (i, 0)),
            pl.BlockSpec((tm, D_PLE), lambda i: (i, 0)),
            pl.BlockSpec((D_PLE, D_MODEL), lambda i: (0, 0)),
            _rows_spec(ROWS_W_IN + ROWS_W_OUT + layer * (D_MODEL // N_SHARD)),
            pl.BlockSpec((1, D_MODEL), lambda i: (0, 0)),
            pl.BlockSpec((1, D_MODEL), lambda i: (0, 0))]


def _ple_fwd(h, p, w_pp, gathered, layer, g_post, g_gate):
    t = h.shape[0]
    tm = _row_tile(t, 256)

    def body(h_ref, p_ref, wpp_ref, wpg_ref, gp_ref, gg_ref, o_ref):
        o_ref[...] = _ple_mix(h_ref[...], p_ref, wpp_ref, wpg_ref, gp_ref, gg_ref)

    return pl.pallas_call(
        body, name="ple_fwd", grid=(t // tm,), in_specs=_ple_specs(tm, layer),
        out_specs=pl.BlockSpec((tm, D_MODEL), lambda i: (i, 0)),
        out_shape=_sds((t, D_MODEL)), compiler_params=_params(("arbitrary",)),
    )(h, p, w_pp, gathered, g_post, g_gate)


def _ple_fwd_final(h, p, w_pp, gathered, layer, g_post, g_gate, g_final, target):
    t = h.shape[0]
    tm = _row_tile(t, 256)

    def body(h_ref, p_ref, wpp_ref, wpg_ref, gp_ref, gg_ref, gf_ref, t_ref, dh_ref, dg_ref, loss_ref):
        @pl.when(pl.program_id(0) == 0)
        def _():
            dg_ref[...] = jnp.zeros_like(dg_ref)
            loss_ref[...] = jnp.zeros_like(loss_ref)

        xh, r = _rms(_ple_mix(h_ref[...], p_ref, wpp_ref, wpg_ref, gp_ref, gg_ref))
        gf = gf_ref[...]
        err = xh * gf - t_ref[...]
        part = 0.5 * jnp.sum(jnp.mean(err * err, axis=-1, keepdims=True), axis=0, keepdims=True)
        loss_ref[...] += jnp.broadcast_to(part, loss_ref.shape)
        dy = err * (1.0 / D_MODEL)
        dg_ref[...] += jnp.sum(dy * xh, axis=0, keepdims=True)
        dh_ref[...] = _rms_bwd(dy * gf, xh, r)

    return pl.pallas_call(
        body, name="ple_fwd_final", grid=(t // tm,),
        in_specs=_ple_specs(tm, layer) + [pl.BlockSpec((1, D_MODEL), lambda i: (0, 0)),
                                          pl.BlockSpec((tm, D_MODEL), lambda i: (i, 0))],
        out_specs=[pl.BlockSpec((tm, D_MODEL), lambda i: (i, 0)),
                   pl.BlockSpec((1, D_MODEL), lambda i: (0, 0)),
                   pl.BlockSpec((1, 128), lambda i: (0, 0))],
        out_shape=[_sds((t, D_MODEL)), _sds((1, D_MODEL)), _sds((1, 128))],
        compiler_params=_params(("arbitrary",)),
    )(h, p, w_pp, gathered, g_post, g_gate, g_final, target)


def _ple_bwd(dh2, h, p, w_pp, gathered, layer, g_post, g_gate, exchange=None):
    t = h.shape[0]
    tm = _row_tile(t, 512)

    def body(d_ref, h_ref, p_ref, wpp_ref, wpg_ref, gp_ref, gg_ref, dh_ref, dwpg_ref, dwpp_ref, dgg_ref, dgp_ref):
        @pl.when(pl.program_id(0) == 0)
        def _():
            dwpg_ref[...] = jnp.zeros_like(dwpg_ref)
            dwpp_ref[...] = jnp.zeros_like(dwpp_ref)
            dgg_ref[...] = jnp.zeros_like(dgg_ref)
            dgp_ref[...] = jnp.zeros_like(dgp_ref)

        d = d_ref[...]
        x = h_ref[...]
        gp = gp_ref[...]
        gg = gg_ref[...]
        pb = _bf(p_ref[...])
        ph, rp = _rms(_dot(pb, wpp_ref[...]))
        pe = ph * gp
        xh, rx = _rms(x)
        un = _bf(xh * gg)
        wpg = wpg_ref[...].reshape(D_MODEL, D_MODEL)
        gate = _sigmoid(_dot(un, wpg))
        dgpre = _bf(d * pe * gate * (1.0 - gate))
        dun = _dot_nt(dgpre, wpg)
        dh_ref[...] = d + _rms_bwd(dun * gg, xh, rx)
        dgg_ref[...] += jnp.sum(dun * xh, axis=0, keepdims=True)
        dwpg_ref[...] += _dot_tn(un, dgpre)
        dpe = d * gate
        dgp_ref[...] += jnp.sum(dpe * ph, axis=0, keepdims=True)
        dwpp_ref[...] += _dot_tn(pb, _bf(_rms_bwd(dpe * gp, ph, rp)))

    in_specs = [pl.BlockSpec((tm, D_MODEL), lambda i: (i, 0)),
                pl.BlockSpec((tm, D_MODEL), lambda i: (i, 0)),
                pl.BlockSpec((tm, D_PLE), lambda i: (i, 0)),
                pl.BlockSpec((D_PLE, D_MODEL), lambda i: (0, 0)),
                _rows_spec(ROWS_W_IN + ROWS_W_OUT + layer * (D_MODEL // N_SHARD)),
                pl.BlockSpec((1, D_MODEL), lambda i: (0, 0)),
                pl.BlockSpec((1, D_MODEL), lambda i: (0, 0))]
    out_specs = [pl.BlockSpec((tm, D_MODEL), lambda i: (i, 0)),
                 pl.BlockSpec((D_MODEL, D_MODEL), lambda i: (0, 0)),
                 pl.BlockSpec((D_PLE, D_MODEL), lambda i: (0, 0)),
                 pl.BlockSpec((1, D_MODEL), lambda i: (0, 0)),
                 pl.BlockSpec((1, D_MODEL), lambda i: (0, 0))]
    out_shape = [_sds((t, D_MODEL)), _sds((D_MODEL, D_MODEL)), _sds((D_PLE, D_MODEL)),
                 _sds((1, D_MODEL)), _sds((1, D_MODEL))]
    operands = [dh2, h, p, w_pp, gathered, g_post, g_gate]
    scratch = []
    if exchange is not None:
        body = _with_exchange(body, 7, 5, exchange, lambda: pl.program_id(0), t // tm)
        xi, xo, xs, scratch, xop = _exchange_args(exchange)
        in_specs, out_specs, out_shape, operands = in_specs + xi, out_specs + xo, out_shape + xs, operands + xop
    return pl.pallas_call(
        body, name="ple_bwd" if exchange is None else "ple_bwd_exchange", grid=(t // tm,),
        in_specs=in_specs, out_specs=out_specs, out_shape=out_shape, scratch_shapes=scratch,
        compiler_params=_params(("arbitrary",)),
    )(*operands)


def _outproj_bwd(dh, ya, yb, gathered, layer):
    t = dh.shape[0]
    tm = _row_tile(t, 512)

    def body(d_ref, ya_ref, yb_ref, w_ref, dya_ref, dyb_ref, dw_ref):
        @pl.when(pl.program_id(0) == 0)
        def _():
            dw_ref[...] = jnp.zeros_like(dw_ref)

        d = _bf(d_ref[...])
        dya_ref[...] = _dot_nt(d, w_ref[0:2].reshape(GROUP, D_MODEL))
        dyb_ref[...] = _dot_nt(d, w_ref[2:4].reshape(GROUP, D_MODEL))
        dw_ref[pl.ds(0, GROUP), :] += _dot_tn(_bf(ya_ref[...]), d)
        dw_ref[pl.ds(GROUP, GROUP), :] += _dot_tn(_bf(yb_ref[...]), d)

    return pl.pallas_call(
        body, name="outproj_bwd", grid=(t // tm,),
        in_specs=[pl.BlockSpec((tm, D_MODEL), lambda i: (i, 0)),
                  pl.BlockSpec((tm, GROUP), lambda i: (i, 0)),
                  pl.BlockSpec((tm, GROUP), lambda i: (i, 0)),
                  _rows_spec(ROWS_W_IN + layer * (D_MODEL // N_SHARD))],
        out_specs=[pl.BlockSpec((tm, GROUP), lambda i: (i, 0)),
                   pl.BlockSpec((tm, GROUP), lambda i: (i, 0)),
                   pl.BlockSpec((D_MODEL, D_MODEL), lambda i: (0, 0))],
        out_shape=[_sds((t, GROUP)), _sds((t, GROUP)), _sds((D_MODEL, D_MODEL))],
        compiler_params=_params(("arbitrary",)),
    )(dh, ya, yb, gathered)


def _inproj_bwd_dx(dres, h, g, gathered, layer, da, db, exchange=None):
    t = h.shape[0]
    tm = _row_tile(t, 512)

    def body(dres_ref, h_ref, g_ref, w_ref, da_ref, db_ref, dh_ref, dg_ref):
        @pl.when(pl.program_id(0) == 0)
        def _():
            dg_ref[...] = jnp.zeros_like(dg_ref)

        du = jnp.zeros((tm, D_MODEL), F32)
        for i in range(8):
            part = da_ref[i] if i < 4 else db_ref[i - 4]
            du = du + _dot_nt(part, w_ref[i // 2, :, pl.ds((i % 2) * GROUP, GROUP)])
        xh, r = _rms(h_ref[...])
        dg_ref[...] += jnp.sum(du * xh, axis=0, keepdims=True)
        dh_ref[...] = dres_ref[...] + _rms_bwd(du * g_ref[...], xh, r)

    in_specs = [pl.BlockSpec((tm, D_MODEL), lambda i: (i, 0)),
                pl.BlockSpec((tm, D_MODEL), lambda i: (i, 0)),
                pl.BlockSpec((1, D_MODEL), lambda i: (0, 0)),
                pl.BlockSpec((N_SHARD, D_MODEL, D_MODEL), lambda i: (0, layer, 0)),
                pl.BlockSpec((4, tm, GROUP), lambda i: (0, i, 0)),
                pl.BlockSpec((4, tm, GROUP), lambda i: (0, i, 0))]
    out_specs = [pl.BlockSpec((tm, D_MODEL), lambda i: (i, 0)), pl.BlockSpec((1, D_MODEL), lambda i: (0, 0))]
    out_shape = [_sds((t, D_MODEL)), _sds((1, D_MODEL))]
    operands = [dres, h, g, gathered, da, db]
    scratch = []
    if exchange is not None:
        body = _with_exchange(body, 6, 2, exchange, lambda: pl.program_id(0), t // tm)
        xi, xo, xs, scratch, xop = _exchange_args(exchange)
        in_specs, out_specs, out_shape, operands = in_specs + xi, out_specs + xo, out_shape + xs, operands + xop
    return pl.pallas_call(
        body, name="inproj_bwd_dx" if exchange is None else "inproj_bwd_dx_exchange", grid=(t // tm,),
        in_specs=in_specs, out_specs=out_specs, out_shape=out_shape, scratch_shapes=scratch,
        compiler_params=_params(("arbitrary",)),
    )(*operands)


def _inproj_bwd_dw(h, g, da, db):
    t = h.shape[0]
    tm = _row_tile(t, 512)

    def body(h_ref, g_ref, da_ref, db_ref, dw_ref):
        @pl.when(pl.program_id(0) == 0)
        def _():
            dw_ref[...] = jnp.zeros_like(dw_ref)

        xh, _ = _rms(h_ref[...])
        u = _bf(xh * g_ref[...])
        for i in range(8):
            dw_ref[i // 2, :, pl.ds((i % 2) * GROUP, GROUP)] += _dot_tn(u, da_ref[i] if i < 4 else db_ref[i - 4])

    return pl.pallas_call(
        body, name="inproj_bwd_dw", grid=(t // tm,),
        in_specs=[pl.BlockSpec((tm, D_MODEL), lambda i: (i, 0)),
                  pl.BlockSpec((1, D_MODEL), lambda i: (0, 0)),
                  pl.BlockSpec((4, tm, GROUP), lambda i: (0, i, 0)),
                  pl.BlockSpec((4, tm, GROUP), lambda i: (0, i, 0))],
        out_specs=pl.BlockSpec((N_SHARD, D_MODEL, D_MODEL), lambda i: (0, 0, 0)),
        out_shape=_sds((N_SHARD, ROWS_LAYER, D_MODEL)), compiler_params=_params(("arbitrary",), VMEM_LIMIT_BIG),
    )(h, g, da, db)


def _hgrn_consts():
    c, nl = HG_CHUNK, HG_LEVELS
    t = np.arange(c)
    tril = np.tril(np.ones((c, c), np.float32))
    masks = np.zeros((nl + 1, c, c), np.float32)
    masks[0] = np.eye(c, dtype=np.float32)
    dmat = np.zeros(((nl + 2) * c, c), np.float32)
    dmat[0:c] = tril
    for l in range(nl):
        m = c >> (l + 1)
        blk = t // (2 * m)
        r = blk * 2 * m + m - 1
        upper = (t % (2 * m)) >= m
        masks[l + 1] = ((blk[:, None] == blk[None, :]) & upper[:, None] & (~upper)[None, :]).astype(np.float32)
        dmat[(l + 1) * c:(l + 2) * c] = tril[t] - tril[r]
    dmat[(nl + 1) * c:] = np.triu(np.ones((c, c), np.float32), k=1)
    return jnp.asarray(masks), jnp.asarray(dmat, BF16)


HG_HEADS = 4


def _hgrn_pre(aq, af, lb):
    sq = _sigmoid(aq)
    sneg = _sigmoid(-af)
    kk = (1.0 - lb) * sneg
    return sq, aq * sq, sneg, kk, jnp.log1p(-kk)


def _hgrn_x(logf, dmat_ref):
    dm = dmat_ref[pl.ds(0, (HG_LEVELS + 1) * HG_CHUNK), :]
    lhi, llo = _split(logf)
    return _dot(dm, lhi) + _dot(dm, llo)


def _hgrn_level(x_all, l, q, kk):
    c = HG_CHUNK
    x = x_all[(l + 1) * c:(l + 2) * c]
    qf = jnp.exp(jnp.minimum(x, 0.0))
    kf = jnp.exp(-jnp.maximum(x, 0.0))
    return qf, kf, _bf(q * qf), _bf(kk * kf)


def _hgrn_scores(xs, qs, kks, mask_ref):
    ps = [mask_ref[0] * _dot_nt(_bf(q), _bf(kk)) for q, kk in zip(qs, kks)]
    for l in range(HG_LEVELS):
        for i, (x_all, q, kk) in enumerate(zip(xs, qs, kks)):
            _, _, ql, kl = _hgrn_level(x_all, l, q, kk)
            ps[i] = ps[i] + mask_ref[l + 1] * _dot_nt(ql, kl)
    return ps


def _hgrn_specs(n_chunks, rev):
    c, w = HG_CHUNK, HG_HEADS * A_D
    cidx = (lambda n: n_chunks - 1 - n) if rev else (lambda n: n)
    col = lambda g: pl.BlockSpec((c, w), lambda h, n: (cidx(n), g * (A_HEADS // HG_HEADS) + h))
    vec = pl.BlockSpec((1, w), lambda h, n: (0, h))
    mask = pl.BlockSpec((HG_LEVELS + 1, c, c), lambda h, n: (0, 0, 0))
    dmat = pl.BlockSpec(((HG_LEVELS + 2) * c, c), lambda h, n: (0, 0))
    state = pl.BlockSpec((HG_HEADS, None, A_D, A_D), lambda h, n: (h, cidx(n), 0, 0))
    return cidx, col, vec, mask, dmat, state


def _lanes(i):
    return pl.ds(i * A_D, A_D)


def _hgrn_fwd(proj, lb, gain):
    t = proj.shape[0]
    c = HG_CHUNK
    nch = t // c
    masks, dmat = _hgrn_consts()
    cidx, col, vec, mask_spec, dmat_spec, state_spec = _hgrn_specs(nch, False)
    heads = range(HG_HEADS)

    def body(aq_ref, af_ref, ai_ref, ag_ref, lb_ref, gain_ref, mask_ref, dmat_ref, y_ref, st_ref, s_scr):
        @pl.when(pl.program_id(1) == 0)
        def _():
            s_scr[...] = jnp.zeros_like(s_scr)

        pre = [_hgrn_pre(aq_ref[:, _lanes(i)], af_ref[:, _lanes(i)], lb_ref[:, _lanes(i)]) for i in heads]
        qs, kks = [p[1] for p in pre], [p[3] for p in pre]
        xs = [_hgrn_x(p[4], dmat_ref) for p in pre]
        bs = [x[0:c] for x in xs]
        b_lasts = [jnp.sum(p[4], axis=0, keepdims=True) for p in pre]
        ps = _hgrn_scores(xs, qs, kks, mask_ref)
        ss = [s_scr[i] for i in heads]
        vbs = [_bf(ai_ref[:, _lanes(i)]) for i in heads]
        os_ = [_dot(_bf(ps[i]), vbs[i]) + _dot_nt(_bf(qs[i] * jnp.exp(bs[i])), _bf(ss[i])) for i in heads]
        for i in heads:
            st_ref[i] = ss[i]
            s_scr[i] = ss[i] * jnp.exp(b_lasts[i]) + _dot_tn(vbs[i], _bf(kks[i] * jnp.exp(b_lasts[i] - bs[i])))
            oh, _ = _rms(os_[i])
            ag = ag_ref[:, _lanes(i)]
            y_ref[:, _lanes(i)] = oh * gain_ref[:, _lanes(i)] * (ag * _sigmoid(ag))

    return pl.pallas_call(
        body, name="hgrn_fwd", grid=(A_HEADS // HG_HEADS, nch),
        in_specs=[col(0), col(1), col(2), col(3), vec, vec, mask_spec, dmat_spec],
        out_specs=[pl.BlockSpec((c, HG_HEADS * A_D), lambda h, n: (n, h)), state_spec],
        out_shape=[_sds((t, GROUP)), _sds((A_HEADS, nch, A_D, A_D))],
        scratch_shapes=[pltpu.VMEM((HG_HEADS, A_D, A_D), F32)],
        compiler_params=_params(("arbitrary", "arbitrary")),
    )(proj, proj, proj, proj, lb, gain, masks, dmat)


def _hgrn_bwd(proj, lb, gain, states, dya):
    t = proj.shape[0]
    c, nl = HG_CHUNK, HG_LEVELS
    nch = t // c
    masks, dmat = _hgrn_consts()
    cidx, col, vec, mask_spec, dmat_spec, state_spec = _hgrn_specs(nch, True)
    heads = range(HG_HEADS)

    def body(aq_ref, af_ref, ai_ref, ag_ref, lb_ref, gain_ref, mask_ref, dmat_ref, st_ref, dy_ref,
             da_ref, dlb_ref, dgain_ref, ds_scr, z_scr):
        @pl.when(pl.program_id(1) == 0)
        def _():
            ds_scr[...] = jnp.zeros_like(ds_scr)
            dlb_ref[...] = jnp.zeros_like(dlb_ref)
            dgain_ref[...] = jnp.zeros_like(dgain_ref)

        aqs = [aq_ref[:, _lanes(i)] for i in heads]
        lbs = [lb_ref[:, _lanes(i)] for i in heads]
        pre = [_hgrn_pre(aqs[i], af_ref[:, _lanes(i)], lbs[i]) for i in heads]
        sqs, qs, snegs, kks = ([p[j] for p in pre] for j in range(4))
        xs = [_hgrn_x(p[4], dmat_ref) for p in pre]
        bs = [x[0:c] for x in xs]
        b_lasts = [jnp.sum(p[4], axis=0, keepdims=True) for p in pre]
        ebs = [jnp.exp(b) for b in bs]
        ebls = [jnp.exp(bl - b) for bl, b in zip(b_lasts, bs)]
        ebl_rows = [jnp.exp(bl) for bl in b_lasts]
        qes = [_bf(q * eb) for q, eb in zip(qs, ebs)]
        kes = [_bf(kk * ebl) for kk, ebl in zip(kks, ebls)]
        vbs = [_bf(ai_ref[:, _lanes(i)]) for i in heads]
        ss = [st_ref[i] for i in heads]
        sbs = [_bf(s) for s in ss]
        dss = [ds_scr[i] for i in heads]
        dsbs = [_bf(ds) for ds in dss]

        pbs = [_bf(p) for p in _hgrn_scores(xs, qs, kks, mask_ref)]
        os_ = [_dot(pbs[i], vbs[i]) + _dot_nt(qes[i], sbs[i]) for i in heads]

        dos = []
        for i in heads:
            ag, gain, dy = ag_ref[:, _lanes(i)], gain_ref[:, _lanes(i)], dy_ref[:, _lanes(i)]
            oh, r = _rms(os_[i])
            sg_sig = _sigmoid(ag)
            sg = ag * sg_sig
            da_ref[3, :, _lanes(i)] = _bf(dy * oh * gain * _silu_grad(ag, sg_sig))
            dgain_ref[:, _lanes(i)] += jnp.sum(dy * oh * sg, axis=0, keepdims=True)
            dos.append(_bf(_rms_bwd(dy * gain * sg, oh, r)))

        dps = [_dot_nt(dos[i], vbs[i]) for i in heads]
        for i in heads:
            da_ref[2, :, _lanes(i)] = _bf(_dot_tn(pbs[i], dos[i]) + _dot_nt(kes[i], dsbs[i]))
        dq_ss = [ebs[i] * _dot(dos[i], sbs[i]) for i in heads]
        dk_ss = [ebls[i] * _dot(vbs[i], dsbs[i]) for i in heads]
        dqs, dks = [], []
        for i in heads:
            dpd = jnp.sum(mask_ref[0] * dps[i], axis=1, keepdims=True)
            z_scr[i, pl.ds(0, c), :] = qs[i] * dq_ss[i]
            z_scr[i, pl.ds((nl + 1) * c, c), :] = kks[i] * dk_ss[i]
            dqs.append(dq_ss[i] + dpd * kks[i])
            dks.append(dk_ss[i] + dpd * qs[i])
        for l in range(nl):
            for i in heads:
                qf, kf, ql, kl = _hgrn_level(xs[i], l, qs[i], kks[i])
                dpl = _bf(mask_ref[l + 1] * dps[i])
                dq_l = qf * _dot(dpl, kl)
                dk_l = kf * _dot_tn(dpl, ql)
                z_scr[i, pl.ds((l + 1) * c, c), :] = qs[i] * dq_l - kks[i] * dk_l
                dqs[i] = dqs[i] + dq_l
                dks[i] = dks[i] + dk_l

        zsplits = [_split(z_scr[i]) for i in heads]
        dlogfs = [_dot_tn(dmat_ref[...], zhi) + _dot_tn(dmat_ref[...], zlo) for zhi, zlo in zsplits]
        ds_new = [_dot_tn(dos[i], qes[i]) for i in heads]
        for i in heads:
            dlogf = dlogfs[i] + ebl_rows[i] * jnp.sum(dss[i] * ss[i], axis=0, keepdims=True)
            dkk = dks[i] - dlogf / (1.0 - kks[i])
            da_ref[1, :, _lanes(i)] = _bf(dkk * (1.0 - lbs[i]) * (-(snegs[i] * (1.0 - snegs[i]))))
            dlb_ref[:, _lanes(i)] += jnp.sum(dkk * (-snegs[i]), axis=0, keepdims=True)
            da_ref[0, :, _lanes(i)] = _bf(dqs[i] * _silu_grad(aqs[i], sqs[i]))
            ds_scr[i] = dss[i] * ebl_rows[i] + ds_new[i]

    w = HG_HEADS * A_D
    return pl.pallas_call(
        body, name="hgrn_bwd", grid=(A_HEADS // HG_HEADS, nch),
        in_specs=[col(0), col(1), col(2), col(3), vec, vec, mask_spec, dmat_spec, state_spec,
                  pl.BlockSpec((c, w), lambda h, n: (cidx(n), h))],
        out_specs=[pl.BlockSpec((4, c, w), lambda h, n: (0, cidx(n), h)), vec, vec],
        out_shape=[_sds((4, t, GROUP), BF16)] + [_sds((1, GROUP))] * 2,
        scratch_shapes=[pltpu.VMEM((HG_HEADS, A_D, A_D), F32), pltpu.VMEM((HG_HEADS, (nl + 2) * c, A_D), F32)],
        compiler_params=_params(("arbitrary", "arbitrary")),
    )(proj, proj, proj, proj, lb, gain, masks, dmat, states, dya)


def _sb_consts():
    j = np.arange(SB_TK)
    strict = (j[:, None] > j[None, :]).astype(np.float32)
    lower = (j[:, None] < j[None, :]).astype(np.float32)
    return jnp.asarray(strict, BF16), jnp.asarray(lower, BF16)


def _lane(x, k):
    return jnp.broadcast_to(x[:, k:k + 1], x.shape)


def _key_lane(kb):
    return lax.broadcasted_iota(jnp.int32, (1, SB_TK), 1) == kb


def _causal(x, masked):
    if not masked:
        return x
    n = (SB_TK, SB_TK)
    top = jnp.where(lax.broadcasted_iota(jnp.int32, n, 1) < lax.broadcasted_iota(jnp.int32, n, 0), x[:SB_TK], 0.0)
    return top if x.shape[0] == SB_TK else jnp.concatenate([top, x[SB_TK:]], axis=0)


def _sb_softplus(z, masked):
    logsig = jnp.minimum(z, 0.0) - jnp.log2(1.0 + jnp.exp2(-jnp.abs(z)))
    return _causal(z - logsig, masked), logsig


def _sb_sweep(qi, group_fn, state, groups, ascending):
    nd = SB_TQ // SB_TK

    def tile(kb, r0, masked):
        return (pl.multiple_of(kb * SB_TK, SB_TK), r0, masked, kb)

    def run(tiles, st, group):
        for i in range(0, len(tiles), group):
            st = group_fn(tiles[i:i + group], st)
        return st

    diag = [tile(qi * nd + d, d * SB_TK, True) for d in range(nd)]
    if ascending:
        state = lax.fori_loop(
            0, qi, lambda j, st: run([tile(j * nd + g, 0, False) for g in range(nd)], st, groups[1]), state)
        return run(diag, state, groups[0])
    state = run(diag[::-1], state, groups[0])
    return lax.fori_loop(
        0, qi, lambda j, st: run([tile((qi - j) * nd - 1 - g, 0, False) for g in range(nd)], st, groups[1]), state)


def _set_rows(r0, full, new):
    return new if r0 == 0 else jnp.concatenate([full[:r0], new], axis=0)


def _sb_specs(t, tq):
    col = lambda g: pl.BlockSpec((tq, 2 * B_D), lambda p, i, h: (i, g * (GROUP // (2 * B_D)) + p))
    full = lambda g: pl.BlockSpec((t, 2 * B_D), lambda p, i, h: (0, g * (GROUP // (2 * B_D)) + p))
    vec = pl.BlockSpec((1, 2 * B_D), lambda p, i, h: (0, p))
    mat = pl.BlockSpec((SB_TK, SB_TK), lambda p, i, h: (0, 0))
    car = pl.BlockSpec((None, tq, SB_TK), lambda p, i, h: (2 * p + h, i, 0))
    return col, full, vec, mat, car


def _head_lanes(h):
    return (lax.broadcasted_iota(jnp.int32, (1, 2 * B_D), 1) >= B_D) == (h == 1)


def _put(ref, h, val):
    @pl.when(h == 0)
    def _():
        ref[...] = val

    @pl.when(h == 1)
    def _():
        ref[...] += val


def _sb_fwd(qkv, bgate, gain, gather=None):
    t = qkv.shape[0]
    tq = SB_TQ
    strict, _ = _sb_consts()
    n_steps = (B_HEADS // 2, t // tq, 2)

    def body(q_ref, k_ref, v_ref, bg_ref, gain_ref, m_ref, *rest):
        if gather is None:
            o_ref, y_ref, car_ref = rest
        else:
            flat_ref, _, o_ref, y_ref, car_ref, gathered_ref, send_sems, recv_sems = rest
            start, forward, finish = _gather_plan(flat_ref, gathered_ref, send_sems, recv_sems, *gather[2:])
            step = (pl.program_id(0) * n_steps[1] + pl.program_id(1)) * n_steps[2] + pl.program_id(2)
            pl.when(step == 0)(start)
            pl.when(step == 2 * n_steps[1] * n_steps[2])(forward)
            pl.when(step == n_steps[0] * n_steps[1] * n_steps[2] - 1)(finish)
        h = pl.program_id(2)
        lanes = _head_lanes(h)
        qb = jnp.where(lanes, q_ref[...], jnp.zeros_like(q_ref))
        cmat = m_ref[...]

        def group(tiles, state):
            carry, acc, cars = state
            kv = [(k_ref[pl.ds(off, SB_TK), :], v_ref[pl.ds(off, SB_TK), :]) for off, _, _, _ in tiles]
            zs = [_dot_nt(qb[r0:], kb) for (_, r0, _, _), (kb, _) in zip(tiles, kv)]
            sps = [_sb_softplus(z, masked) for z, (_, _, masked, _) in zip(zs, tiles)]
            css = [_dot(_bf(sp), cmat) for sp, _ in sps]
            ws = []
            for (sp, logsig), cs, (_, r0, masked, kb) in zip(sps, css, tiles):
                ws.append(_bf(_causal(jnp.exp2(logsig - cs - carry[r0:]), masked)))
                cars = _set_rows(r0, cars, jnp.where(_key_lane(kb), carry[r0:], cars[r0:]))
                carry = _set_rows(r0, carry, carry[r0:] + _lane(cs + sp, 0))
            for w, (_, vb), (_, r0, _, _) in zip(ws, kv, tiles):
                acc = _set_rows(r0, acc, acc[r0:] + _dot(w, vb))
            return carry, acc, cars

        zero = jnp.zeros((tq, SB_TK), F32)
        _, acc, cars = _sb_sweep(pl.program_id(1), group, (zero, jnp.zeros((tq, 2 * B_D), F32), zero),
                                 SB_GROUP_FWD, False)
        car_ref[...] = cars
        o = jnp.where(lanes, acc, 0.0)
        oh = o * lax.rsqrt(jnp.sum(o * o, axis=-1, keepdims=True) * (1.0 / B_D) + EPS)
        bg = bg_ref[...]
        _put(o_ref, h, o)
        _put(y_ref, h, oh * gain_ref[...] * (bg * _sigmoid(bg)))

    col, full, vec, mat, car = _sb_specs(t, tq)
    out = pl.BlockSpec((tq, 2 * B_D), lambda p, i, h: (i, p))
    in_specs = [col(0), full(1), full(2), col(0), vec, mat]
    out_specs = [out, out, car]
    out_shape = [_sds((t, GROUP)), _sds((t, GROUP)), _sds((B_HEADS, t, SB_TK))]
    operands = [qkv, qkv, qkv, bgate, gain, strict]
    extra = {}
    if gather is not None:
        in_specs += [_ANY, _ANY]
        out_specs += [_ANY]
        out_shape += [_sds(gather[1].shape, gather[1].dtype)]
        operands += [gather[0], gather[1]]
        extra = dict(input_output_aliases={7: 3}, scratch_shapes=_gather_sems(gather[4]))
    return pl.pallas_call(
        body, name="sb_fwd" if gather is None else "sb_fwd_gather", grid=n_steps,
        in_specs=in_specs, out_specs=out_specs, out_shape=out_shape,
        compiler_params=_params(("arbitrary", "arbitrary", "arbitrary")), **extra,
    )(*operands)


def _sb_bwd(qkv, bgate, o, carries, dy, gain, exchange=None):
    t = qkv.shape[0]
    tq = SB_TQ
    strict, lower = _sb_consts()

    def body(q_ref, k_ref, v_ref, bg_ref, o_ref, car_ref, dy_ref, gain_ref, ms_ref, ml_ref,
             dq_ref, dk_ref, dv_ref, dbg_ref, dgain_ref):
        qi = pl.program_id(1)
        h = pl.program_id(2)
        lanes = _head_lanes(h)

        @pl.when((qi == 0) & (h == 0))
        def _():
            dk_ref[...] = jnp.zeros_like(dk_ref)
            dv_ref[...] = jnp.zeros_like(dv_ref)
            dgain_ref[...] = jnp.zeros_like(dgain_ref)

        qb = jnp.where(lanes, q_ref[...], jnp.zeros_like(q_ref))
        cmat = ms_ref[...]
        lmat = ml_ref[...]
        cars = car_ref[...]
        o = jnp.where(lanes, o_ref[...], 0.0)
        dy = jnp.where(lanes, dy_ref[...], 0.0)
        bg = bg_ref[...]
        gain = gain_ref[...]
        r = lax.rsqrt(jnp.sum(o * o, axis=-1, keepdims=True) * (1.0 / B_D) + EPS)
        oh = o * r
        sig = _sigmoid(bg)
        sg = bg * sig
        _put(dbg_ref, h, dy * oh * gain * _silu_grad(bg, sig))
        dgain_ref[...] += jnp.sum(dy * oh * sg, axis=0, keepdims=True)
        doh = dy * gain * sg
        do = _bf(r * (doh - oh * (jnp.sum(doh * oh, axis=-1, keepdims=True) * (1.0 / B_D))))

        def group(tiles, state):
            gleft, dq = state
            kv = [(k_ref[pl.ds(off, SB_TK), :], v_ref[pl.ds(off, SB_TK), :]) for off, _, _, _ in tiles]
            zs = [_dot_nt(qb[r0:], kb) for (_, r0, _, _), (kb, _) in zip(tiles, kv)]
            dws = [_dot_nt(do[r0:], vb) for (_, r0, _, _), (_, vb) in zip(tiles, kv)]
            sps = [_sb_softplus(z, masked) for z, (_, _, masked, _) in zip(zs, tiles)]
            css = [_dot(_bf(sp), cmat) for sp, _ in sps]
            ws, gs = [], []
            for (_, logsig), cs, dw, (_, r0, masked, kb) in zip(sps, css, dws, tiles):
                right = jnp.sum(jnp.where(_key_lane(kb), cars[r0:], 0.0), axis=1, keepdims=True)
                w = _causal(jnp.exp2(logsig - cs - right), masked)
                ws.append(_bf(w))
                gs.append(dw * w)
            gps = [_dot(_bf(g), lmat) for g in gs]
            dzs = []
            for (_, logsig), g, gp, (_, r0, masked, _) in zip(sps, gs, gps, tiles):
                dz = g - jnp.exp2(logsig) * (g + gleft[r0:] + gp)
                dzs.append(_bf(_causal(dz, masked)))
                gleft = _set_rows(r0, gleft, gleft[r0:] + _lane(gp + g, SB_TK - 1))
            for dz, wb, (kb, _), (off, r0, _, _) in zip(dzs, ws, kv, tiles):
                dq = _set_rows(r0, dq, dq[r0:] + _dot(dz, kb))
                dk_ref[pl.ds(off, SB_TK), :] += _dot_tn(dz, qb[r0:])
                dv_ref[pl.ds(off, SB_TK), :] += _dot_tn(wb, do[r0:])
            return gleft, dq

        _, dq = _sb_sweep(qi, group, (jnp.zeros((tq, SB_TK), F32), jnp.zeros((tq, 2 * B_D), F32)), SB_GROUP_BWD, True)
        _put(dq_ref, h, jnp.where(lanes, dq * (B_D ** -0.5), 0.0))

    col, full, vec, mat, car = _sb_specs(t, tq)
    blk = pl.BlockSpec((tq, 2 * B_D), lambda p, i, h: (i, p))
    whole = pl.BlockSpec((t, 2 * B_D), lambda p, i, h: (0, p))
    grid = (B_HEADS // 2, t // tq, 2)
    in_specs = [col(0), full(1), full(2), col(0), blk, car, blk, vec, mat, mat]
    out_specs = [blk, whole, whole, blk, vec]
    out_shape = [_sds((t, GROUP))] * 4 + [_sds((1, GROUP))]
    operands = [qkv, qkv, qkv, bgate, o, carries, dy, gain, strict, lower]
    scratch = []
    if exchange is not None:
        step_of = lambda: (pl.program_id(0) * grid[1] + pl.program_id(1)) * grid[2] + pl.program_id(2)
        body = _with_exchange(body, 10, 5, exchange, step_of, grid[0] * grid[1] * grid[2])
        xi, xo, xs, scratch, xop = _exchange_args(exchange)
        in_specs, out_specs, out_shape, operands = in_specs + xi, out_specs + xo, out_shape + xs, operands + xop
    return pl.pallas_call(
        body, name="sb_bwd" if exchange is None else "sb_bwd_exchange", grid=grid,
        in_specs=in_specs, out_specs=out_specs, out_shape=out_shape, scratch_shapes=scratch,
        compiler_params=_params(("arbitrary", "arbitrary", "arbitrary")),
    )(*operands)


def _adamw(w, g, m, v):
    rows, cols = w.shape
    tr = rows
    for cand in (400, 256, 128, 64, 32, 16, 8):
        if rows % cand == 0:
            tr = cand
            break

    def body(w_ref, g_ref, m_ref, v_ref, d_ref, nm_ref, nv_ref):
        g_ = g_ref[...]
        m_ = ADAM_B1 * m_ref[...] + (1.0 - ADAM_B1) * g_
        v_ = ADAM_B2 * v_ref[...] + (1.0 - ADAM_B2) * (g_ * g_)
        m_hat = m_ / (1.0 - ADAM_B1 ** ADAM_STEP)
        v_hat = v_ / (1.0 - ADAM_B2 ** ADAM_STEP)
        d_ref[...] = -ADAM_LR * (m_hat / (jnp.sqrt(v_hat) + ADAM_EPS) + ADAM_WD * w_ref[...])
        nm_ref[...] = m_
        nv_ref[...] = v_

    spec = pl.BlockSpec((tr, cols), lambda i: (i, 0))
    return pl.pallas_call(
        body, name="adamw", grid=(rows // tr,), in_specs=[spec] * 4, out_specs=[spec] * 3,
        out_shape=[_sds((rows, cols))] * 3, compiler_params=_params(("arbitrary",)),
    )(w, g, m, v)


_ANY = pl.BlockSpec(memory_space=pl.ANY)


def _place():
    return lax.axis_index("x"), lax.axis_index("y"), lax.axis_index("c")


def _gather_plan(x_ref, out_ref, send_sems, recv_sems, row0, nrows, nc):
    x, y, c = _place()
    me = 2 * x + y
    sibling = (x, y, 1 - c)
    half = nrows // 2
    ch = half // nc
    peers = [me ^ k for k in (1, 2, 3)]

    def rows(shard, hc, r):
        return out_ref.at[shard, pl.ds(row0 + hc * half + r * ch, ch), :]

    def copy(k, shard, hc, r, to, src=None):
        return pltpu.make_async_remote_copy(
            src_ref=rows(shard, hc, r) if src is None else src, dst_ref=rows(shard, hc, r),
            send_sem=send_sems.at[k * nc + r], recv_sem=recv_sems.at[k * nc + r], device_id=to, device_id_type=MESH)

    def first(k, p, r):
        return copy(k, me, c, r, (p >> 1, p & 1, c), src=x_ref.at[pl.ds(row0 + c * half + r * ch, ch), :])

    def start():
        for k, p in enumerate(peers):
            for r in range(nc):
                first(k, p, r).start()

    def forward():
        for k, p in enumerate(peers):
            for r in range(nc):
                copy(k, p, c, r, sibling).wait_recv()
                copy(3 + k, p, c, r, sibling).start()

    def finish():
        for k, p in enumerate(peers):
            for r in range(nc):
                copy(3 + k, p, 1 - c, r, sibling).wait_recv()
        for k, p in enumerate(peers):
            for r in range(nc):
                first(k, p, r).wait_send()
                copy(3 + k, p, c, r, sibling).wait_send()

    return start, forward, finish


def _gather_sems(nc):
    return [pltpu.SemaphoreType.DMA((6 * nc,)), pltpu.SemaphoreType.DMA((6 * nc,))]


def _swap_plan(g_ref, out_ref, send_sems, recv_sems):
    half, ch, nc = HALF_LAYER, CHUNK_ROWS, RS_CHUNKS

    def copies():
        x, y, c = _place()
        return [pltpu.make_async_remote_copy(
            src_ref=g_ref.at[j, pl.ds((1 - c) * half + r * ch, ch), :], dst_ref=out_ref.at[j, pl.ds(r * ch, ch), :],
            send_sem=send_sems.at[j * nc + r], recv_sem=recv_sems.at[j * nc + r],
            device_id=(x, y, 1 - c), device_id_type=MESH) for j in range(N_SHARD) for r in range(nc)]

    def start():
        for cp in copies():
            cp.start()

    def finish():
        for cp in copies():
            cp.wait()

    return start, finish


def _scatter_plan(p_ref, out_ref, send_sems, recv_sems):
    ch, nc = CHUNK_ROWS, RS_CHUNKS

    def copies():
        x, y, c = _place()
        me = 2 * x + y
        return [pltpu.make_async_remote_copy(
            src_ref=p_ref.at[me ^ k, pl.ds(r * ch, ch), :], dst_ref=out_ref.at[k - 1, pl.ds(r * ch, ch), :],
            send_sem=send_sems.at[(k - 1) * nc + r], recv_sem=recv_sems.at[(k - 1) * nc + r],
            device_id=((me ^ k) >> 1, (me ^ k) & 1, c), device_id_type=MESH) for k in (1, 2, 3) for r in range(nc)]

    def start():
        for cp in copies():
            cp.start()

    def finish():
        for cp in copies():
            cp.wait()

    return start, finish


SWAP = (_swap_plan, (N_SHARD, HALF_LAYER, D_MODEL), F32, N_SHARD * RS_CHUNKS)
SCATTER = (_scatter_plan, (3, HALF_LAYER, D_MODEL), BF16, 3 * RS_CHUNKS)


def _exchange_call(kind, operand):
    plan, shape, dtype, n_sems = kind

    def body(in_ref, out_ref, send_sems, recv_sems):
        start, finish = plan(in_ref, out_ref, send_sems, recv_sems)
        start()
        finish()

    return pl.pallas_call(
        body, name="exchange", in_specs=[_ANY], out_specs=_ANY, out_shape=_sds(shape, dtype),
        scratch_shapes=[pltpu.SemaphoreType.DMA((n_sems,)), pltpu.SemaphoreType.DMA((n_sems,))],
    )(operand)


def _with_exchange(body, n_in, n_out, exchange, step_of, n_steps):
    def wrapped(*refs):
        ins, src = refs[:n_in], refs[n_in]
        outs, dst = refs[n_in + 1:n_in + 1 + n_out], refs[n_in + 1 + n_out]
        send_sems, recv_sems = refs[n_in + 2 + n_out:n_in + 4 + n_out]
        start, finish = exchange[0][0](src, dst, send_sems, recv_sems)
        pl.when(step_of() == 0)(start)
        body(*ins, *outs, *refs[n_in + 4 + n_out:])
        pl.when(step_of() == n_steps - 1)(finish)

    return wrapped


def _exchange_args(exchange):
    (plan, shape, dtype, n_sems), operand = exchange
    sems = [pltpu.SemaphoreType.DMA((n_sems,)), pltpu.SemaphoreType.DMA((n_sems,))]
    return [_ANY], [_ANY], [_sds(shape, dtype)], sems, [operand]


def _gather_weights(flat, row0, nrows, nc):
    def body(x_ref, out_ref, send_sems, recv_sems):
        start, forward, finish = _gather_plan(x_ref, out_ref, send_sems, recv_sems, row0, nrows, nc)
        start()
        forward()
        finish()

    return pl.pallas_call(
        body, name="gather_weights", in_specs=[_ANY], out_specs=_ANY,
        out_shape=_sds((N_SHARD, ROWS_FLAT, D_MODEL), BF16), scratch_shapes=_gather_sems(nc),
    )(flat)


def _add_my_half(grads, recv):
    tr = 400
    nb = HALF_LAYER // tr
    core = lax.axis_index("c").astype(jnp.int32).reshape(1)

    def body(c_ref, g_ref, r_ref, o_ref, ob_ref):
        acc = g_ref[...] + r_ref[...]
        o_ref[...] = acc
        ob_ref[...] = _bf(acc)

    out = pl.BlockSpec((None, tr, D_MODEL), lambda j, i, c_ref: (j, i, 0))
    return pl.pallas_call(
        body, name="add_my_half",
        grid_spec=pltpu.PrefetchScalarGridSpec(
            num_scalar_prefetch=1, grid=(N_SHARD, nb),
            in_specs=[pl.BlockSpec((None, tr, D_MODEL), lambda j, i, c_ref: (j, c_ref[0] * nb + i, 0)), out],
            out_specs=[out, out]),
        out_shape=[_sds((N_SHARD, HALF_LAYER, D_MODEL)), _sds((N_SHARD, HALF_LAYER, D_MODEL), BF16)],
        compiler_params=_params(("arbitrary", "arbitrary")),
    )(core, grads, recv)


def _sum_scattered(part, recv):
    tr = 400
    chip = (2 * lax.axis_index("x") + lax.axis_index("y")).astype(jnp.int32).reshape(1)

    def body(c_ref, p_ref, r_ref, o_ref):
        acc = p_ref[...]
        for k in range(3):
            acc = acc + r_ref[k].astype(F32)
        o_ref[...] = acc

    return pl.pallas_call(
        body, name="sum_scattered",
        grid_spec=pltpu.PrefetchScalarGridSpec(
            num_scalar_prefetch=1, grid=(HALF_LAYER // tr,),
            in_specs=[pl.BlockSpec((None, tr, D_MODEL), lambda i, c_ref: (c_ref[0], i, 0)),
                      pl.BlockSpec((3, tr, D_MODEL), lambda i, c_ref: (0, i, 0))],
            out_specs=pl.BlockSpec((tr, D_MODEL), lambda i, c_ref: (i, 0))),
        out_shape=_sds((HALF_LAYER, D_MODEL)), compiler_params=_params(("arbitrary",)),
    )(chip, part, recv)


def _swap_reduced(mine):
    ch, nc = CHUNK_ROWS, RS_CHUNKS

    def body(r_ref, out_ref, send_sems, recv_sems):
        x, y, c = _place()
        copies = [pltpu.make_async_remote_copy(
            src_ref=r_ref.at[l, pl.ds(r * ch, ch), :], dst_ref=out_ref.at[l, pl.ds(r * ch, ch), :],
            send_sem=send_sems.at[l * nc + r], recv_sem=recv_sems.at[l * nc + r],
            device_id=(x, y, 1 - c), device_id_type=MESH) for l in range(2) for r in range(nc)]
        for cp in copies:
            cp.start()
        for cp in copies:
            cp.wait()

    return pl.pallas_call(
        body, name="swap_reduced", in_specs=[_ANY], out_specs=_ANY,
        out_shape=_sds((2, HALF_LAYER, D_MODEL)),
        scratch_shapes=[pltpu.SemaphoreType.DMA((2 * nc,)), pltpu.SemaphoreType.DMA((2 * nc,))],
    )(mine)


def _allreduce_small(vec):
    def body(v_ref, out_ref, buf, send_sems, recv_sems):
        x, y, c = _place()
        me = 4 * x + 2 * y + c
        buf[me] = v_ref[...]
        peers = [me ^ k for k in range(1, N_DEV)]
        sends = [pltpu.make_async_remote_copy(
            src_ref=v_ref, dst_ref=buf.at[me], send_sem=send_sems.at[k], recv_sem=recv_sems.at[k],
            device_id=(p >> 2, (p >> 1) & 1, p & 1), device_id_type=MESH) for k, p in enumerate(peers)]
        for cp in sends:
            cp.start()
        for k, p in enumerate(peers):
            pltpu.make_async_remote_copy(
                src_ref=v_ref, dst_ref=buf.at[p], send_sem=send_sems.at[k], recv_sem=recv_sems.at[k],
                device_id=(p >> 2, (p >> 1) & 1, p & 1), device_id_type=MESH).wait_recv()
        for cp in sends:
            cp.wait_send()
        acc = buf[0]
        for d in range(1, N_DEV):
            acc = acc + buf[d]
        out_ref[...] = acc

    vm = pl.BlockSpec(memory_space=pltpu.VMEM)
    return pl.pallas_call(
        body, name="allreduce_small", in_specs=[vm], out_specs=vm, out_shape=_sds((SMALL_ROWS, 128)),
        scratch_shapes=[pltpu.VMEM((N_DEV, SMALL_ROWS, 128), F32),
                        pltpu.SemaphoreType.DMA((N_DEV - 1,)), pltpu.SemaphoreType.DMA((N_DEV - 1,))],
    )(vec)


def _flatten_shard(w_in, w_out, w_pg, w_pp):
    return jnp.concatenate([w_in.reshape(-1, D_MODEL), w_out.reshape(-1, D_MODEL), w_pg.reshape(-1, D_MODEL),
                            w_pp.reshape(-1, D_MODEL)], axis=0)


def _unflatten_layers(flats):
    a, b, c = D_MODEL, D_MODEL + D_MODEL // N_SHARD, D_MODEL + 2 * (D_MODEL // N_SHARD)
    q = D_MODEL // N_SHARD
    return (jnp.stack([f[:a] for f in flats]), jnp.stack([f[a:b] for f in flats]),
            jnp.stack([f[b:c] for f in flats]), jnp.stack([f[c:].reshape(D_PLE, q) for f in flats]))


def _full_w_pp(gathered):
    c = ROWS_W_IN + ROWS_W_OUT + ROWS_W_PG
    q = D_MODEL // N_SHARD
    rpp = ROWS_W_PP // 2
    return [gathered[:, c + l * rpp:c + (l + 1) * rpp, :].reshape(N_SHARD, D_PLE, q).transpose(1, 0, 2)
            .reshape(D_PLE, D_MODEL) for l in range(2)]


def _layer_grads(dw_in, dw_out, dw_pg, dw_pp):
    q = D_MODEL // N_SHARD
    rpp = ROWS_W_PP // 2
    rest = jnp.concatenate([dw_out.reshape(N_SHARD, q, D_MODEL), dw_pg.reshape(N_SHARD, q, D_MODEL),
                            dw_pp.reshape(D_PLE, N_SHARD, q).transpose(1, 0, 2).reshape(N_SHARD, rpp, D_MODEL)], axis=1)
    return lax.dynamic_update_slice(dw_in, rest, (0, D_MODEL, 0))


def _lower_bounds(lb_logits):
    sm = jax.nn.softmax(lb_logits.astype(F32), axis=0)
    return jnp.cumsum(sm, axis=0) - sm[0:1]


def kernel(x, p, norm_mix, w_in, a_out_norm, b_out_norm, w_out, lb_logits, ple_gate_norm, w_ple_gate, w_ple_proj, ple_post_norm, final_norm, loss_target, m_norm_mix, m_w_in, m_a_out_norm, m_b_out_norm, m_w_out, m_lb_logits, m_ple_gate_norm, m_w_ple_gate, m_w_ple_proj, m_ple_post_norm, m_final_norm, v_norm_mix, v_w_in, v_a_out_norm, v_b_out_norm, v_w_out, v_lb_logits, v_ple_gate_norm, v_w_ple_gate, v_w_ple_proj, v_ple_post_norm, v_final_norm):
    t = x.shape[1]
    h0 = x.reshape(t, D_MODEL)
    target = loss_target.reshape(t, D_MODEL)
    pl_in = p.reshape(2, t, D_PLE)

    w_flat_bf = _flatten_shard(_bf(w_in), _bf(w_out), _bf(w_ple_gate), _bf(w_ple_proj))
    chip = 2 * lax.axis_index("x") + lax.axis_index("y")
    gathered = lax.dynamic_update_slice(_gather_weights(w_flat_bf, 0, D_MODEL, 2), w_flat_bf[None], (chip, 0, 0))
    lbs, lbs_vjp = jax.vjp(_lower_bounds, lb_logits)

    saved = []
    h = h0
    for l in range(2):
        g_mix = norm_mix[l].reshape(1, D_MODEL)
        lb = lbs[l].reshape(1, GROUP)
        ga = a_out_norm[l].reshape(1, GROUP)
        gb = b_out_norm[l].reshape(1, GROUP)
        proj, qkv, bgate = _inproj(h, g_mix, gathered, l)
        ya, states = _hgrn_fwd(proj, lb, ga)
        if l == 0:
            ob, yb, cars, gathered = _sb_fwd(qkv, bgate, gb, (w_flat_bf, gathered, D_MODEL, ROWS_FLAT - D_MODEL, 4))
            w_pps = _full_w_pp(gathered)
        else:
            ob, yb, cars = _sb_fwd(qkv, bgate, gb)
        h1 = _outproj(h, ya, yb, gathered, l)
        g_post = ple_post_norm[l].reshape(1, D_MODEL)
        g_gate = ple_gate_norm[l].reshape(1, D_MODEL)
        saved.append((h, proj, qkv, bgate, states, ya, yb, ob, cars, h1))
        if l == 0:
            h = _ple_fwd(h1, pl_in[l], w_pps[l], gathered, l, g_post, g_gate)
        else:
            dh, d_final, loss_part = _ple_fwd_final(h1, pl_in[l], w_pps[l], gathered, l, g_post, g_gate,
                                                    final_norm.reshape(1, D_MODEL), target)

    g_layer, chip_sum, scattered = [None] * 2, [None] * 2, [None] * 2
    d_mix, d_a, d_b, d_lb, d_gate, d_post = [None] * 2, [None] * 2, [None] * 2, [None] * 2, [None] * 2, [None] * 2
    for l in (1, 0):
        h_in, proj, qkv, bgate, states, ya, yb, ob, cars, h1 = saved[l]
        g_mix = norm_mix[l].reshape(1, D_MODEL)
        lb = lbs[l].reshape(1, GROUP)
        ga = a_out_norm[l].reshape(1, GROUP)
        gb = b_out_norm[l].reshape(1, GROUP)
        g_post = ple_post_norm[l].reshape(1, D_MODEL)
        g_gate = ple_gate_norm[l].reshape(1, D_MODEL)
        if l == 1:
            dh1, dw_pg, dw_pp, d_gate[l], d_post[l] = _ple_bwd(dh, h1, pl_in[l], w_pps[l], gathered, l, g_post, g_gate)
            dya, dyb, dw_out = _outproj_bwd(dh1, ya, yb, gathered, l)
            dbq, dbk, dbv, dbg, d_b[l] = _sb_bwd(qkv, bgate, ob, cars, dyb, gb)
        else:
            dh1, dw_pg, dw_pp, d_gate[l], d_post[l], from_sibling = _ple_bwd(
                dh, h1, pl_in[l], w_pps[l], gathered, l, g_post, g_gate, (SWAP, g_layer[1]))
            chip_sum[1], chip_sum_bf = _add_my_half(g_layer[1], from_sibling)
            dya, dyb, dw_out = _outproj_bwd(dh1, ya, yb, gathered, l)
            dbq, dbk, dbv, dbg, d_b[l], scattered[1] = _sb_bwd(qkv, bgate, ob, cars, dyb, gb, (SCATTER, chip_sum_bf))
        da, d_lb[l], d_a[l] = _hgrn_bwd(proj, lb, ga, states, dya)
        db = jnp.stack([dbq, dbk * LN2, dbv, dbg]).astype(BF16)
        g_layer[l] = _layer_grads(_inproj_bwd_dw(h_in, g_mix, da, db), dw_out, dw_pg, dw_pp)
        if l == 1:
            dh, d_mix[l] = _inproj_bwd_dx(dh1, h_in, g_mix, gathered, l, da, db)
        else:
            dh, d_mix[l], from_sibling = _inproj_bwd_dx(dh1, h_in, g_mix, gathered, l, da, db, (SWAP, g_layer[0]))
    grad_x = dh.reshape(x.shape)

    chip_sum[0], chip_sum_bf = _add_my_half(g_layer[0], from_sibling)
    scattered[0] = _exchange_call(SCATTER, chip_sum_bf)
    mine = jnp.stack([_sum_scattered(chip_sum[l], scattered[l]) for l in range(2)])
    other = _swap_reduced(mine)
    south = lax.axis_index("c") == 0
    g_w_in, g_w_out, g_w_pg, g_w_pp = _unflatten_layers(
        [jnp.concatenate([jnp.where(south, mine[l], other[l]), jnp.where(south, other[l], mine[l])]) for l in range(2)])

    small = jnp.concatenate([
        jnp.concatenate(d_mix).reshape(-1, 128), jnp.concatenate(d_a).reshape(-1, 128),
        jnp.concatenate(d_b).reshape(-1, 128), jnp.concatenate(d_lb).reshape(-1, 128),
        jnp.concatenate(d_gate).reshape(-1, 128), jnp.concatenate(d_post).reshape(-1, 128),
        d_final.reshape(-1, 128), jnp.broadcast_to(loss_part, (8, 128))], axis=0)
    small = _allreduce_small(small)
    loss = small[80, 0]
    g_norm_mix = small[0:16].reshape(2, D_MODEL)
    g_a = small[16:24].reshape(2, GROUP)
    g_b = small[24:32].reshape(2, GROUP)
    (g_lb,) = lbs_vjp(small[32:40].reshape(2, GROUP))
    g_gate = small[40:56].reshape(2, D_MODEL)
    g_post = small[56:72].reshape(2, D_MODEL)
    g_final = small[72:80].reshape(D_MODEL)

    def adam_matrix(w, g, m, v):
        d, nm, nv = _adamw(w.reshape(-1, D_MODEL), g.reshape(-1, D_MODEL), m.reshape(-1, D_MODEL), v.reshape(-1, D_MODEL))
        return d.reshape(w.shape), nm.reshape(w.shape), nv.reshape(w.shape)

    d_w_in, nm_w_in, nv_w_in = adam_matrix(w_in, g_w_in, m_w_in, v_w_in)
    d_w_out, nm_w_out, nv_w_out = adam_matrix(w_out, g_w_out, m_w_out, v_w_out)
    d_w_pg, nm_w_pg, nv_w_pg = adam_matrix(w_ple_gate, g_w_pg, m_w_ple_gate, v_w_ple_gate)
    d_w_pp, nm_w_pp, nv_w_pp = adam_matrix(w_ple_proj, g_w_pp, m_w_ple_proj, v_w_ple_proj)

    small_w = [norm_mix, a_out_norm, b_out_norm, lb_logits, ple_gate_norm, ple_post_norm, final_norm]
    small_g = [g_norm_mix, g_a, g_b, g_lb, g_gate, g_post, g_final]
    small_m = [m_norm_mix, m_a_out_norm, m_b_out_norm, m_lb_logits, m_ple_gate_norm, m_ple_post_norm, m_final_norm]
    small_v = [v_norm_mix, v_a_out_norm, v_b_out_norm, v_lb_logits, v_ple_gate_norm, v_ple_post_norm, v_final_norm]
    pack = lambda arrs: jnp.concatenate([a.reshape(-1, 128) for a in arrs], axis=0)
    ds, nms, nvs = _adamw(pack(small_w), pack(small_g), pack(small_m), pack(small_v))

    def unpack(packed):
        out, r = [], 0
        for a in small_w:
            n = a.size // 128
            out.append(packed[r:r + n].reshape(a.shape))
            r += n
        return out

    d_s, nm_s, nv_s = unpack(ds), unpack(nms), unpack(nvs)

    def ordered(s, big):
        return [s[0], big[0], s[1], s[2], big[1], s[3], s[4], big[2], big[3], s[5], s[6]]

    grads = ordered(small_g, [g_w_in, g_w_out, g_w_pg, g_w_pp])
    deltas = ordered(d_s, [d_w_in, d_w_out, d_w_pg, d_w_pp])
    new_m = ordered(nm_s, [nm_w_in, nm_w_out, nm_w_pg, nm_w_pp])
    new_v = ordered(nv_s, [nv_w_in, nv_w_out, nv_w_pg, nv_w_pp])
    return (loss, grad_x, *grads, *deltas, *new_m, *new_v)
```

```python
import functools
import math

import numpy as np
import jax
import jax.numpy as jnp
from jax import lax
from jax.experimental import pallas as pl
from jax.experimental.pallas import tpu as pltpu

F32 = jnp.float32
BF16 = jnp.bfloat16
MESH = pl.DeviceIdType.MESH

D_MODEL = 1024
D_PLE = 256
D_IN = 4096
A_HEADS, A_D = 4, 128
B_HEADS, B_D = 8, 64
GROUP = 512
EPS = 1e-6
N_SHARD = 4
N_DEV = 8

HG_CHUNK = 128
HG_LEVELS = 7
SB_TQ = 1024
SB_TK = 128
SB_GROUP_FWD, SB_GROUP_BWD = (2, 2), (8, 2)
LOG2E = 1.4426950408889634
LN2 = 0.6931471805599453

ADAM_LR, ADAM_B1, ADAM_B2, ADAM_EPS, ADAM_WD, ADAM_STEP = 0.001, 0.9, 0.999, 1e-08, 0.01, 10

VMEM_LIMIT = 48 * 1024 * 1024
VMEM_LIMIT_BIG = 58 * 1024 * 1024

ROWS_W_IN = 2 * D_MODEL
ROWS_W_OUT = 2 * (D_MODEL // N_SHARD)
ROWS_W_PG = 2 * (D_MODEL // N_SHARD)
ROWS_W_PP = 2 * (D_PLE * (D_MODEL // N_SHARD) // D_MODEL)
ROWS_FLAT = ROWS_W_IN + ROWS_W_OUT + ROWS_W_PG + ROWS_W_PP
HALF_FLAT = ROWS_FLAT // 2
N_CHUNK = 10
CHUNK_ROWS = HALF_FLAT // N_CHUNK

ROWS_LAYER = ROWS_FLAT // 2
HALF_LAYER = ROWS_LAYER // 2
RS_CHUNKS = HALF_LAYER // CHUNK_ROWS

SMALL_ROWS = 88


def _sds(shape, dtype=F32):
    return jax.ShapeDtypeStruct(shape, dtype)


def _params(sem=None, vmem_limit=VMEM_LIMIT):
    kw = dict(vmem_limit_bytes=vmem_limit)
    if sem is not None:
        kw["dimension_semantics"] = sem
    return pltpu.CompilerParams(**kw)


def _dot(a, b, precision=None):
    return lax.dot_general(a, b, (((1,), (0,)), ((), ())), preferred_element_type=F32, precision=precision)


def _dot_nt(a, b, precision=None):
    return lax.dot_general(a, b, (((1,), (1,)), ((), ())), preferred_element_type=F32, precision=precision)


def _dot_tn(a, b, precision=None):
    return lax.dot_general(a, b, (((0,), (0,)), ((), ())), preferred_element_type=F32, precision=precision)


def _bf(x):
    return x.astype(BF16)


def _split(x):
    hi = x.astype(BF16)
    lo = (x - hi.astype(F32)).astype(BF16)
    return hi, lo


def _rms(x):
    r = lax.rsqrt(jnp.mean(x * x, axis=-1, keepdims=True) + EPS)
    return x * r, r


def _rms_bwd(dxh, xh, r):
    return r * (dxh - xh * jnp.mean(dxh * xh, axis=-1, keepdims=True))


def _sigmoid(x):
    return 1.0 / (1.0 + jnp.exp(-x))


def _silu_grad(x, sig):
    return sig * (1.0 + x * (1.0 - sig))


def _row_tile(t, want):
    return min(t, want)


def _inproj(h, g, gathered, layer):
    t = h.shape[0]
    tm = _row_tile(t, 512)

    def body(h_ref, g_ref, w_ref, pa_ref, qkv_ref, bg_ref):
        xh, _ = _rms(h_ref[...])
        u = _bf(xh * g_ref[...])
        for j in range(8):
            acc = _dot(u, w_ref[j // 2, :, pl.ds((j % 2) * GROUP, GROUP)])
            if j < 4:
                pa_ref[:, pl.ds(j * GROUP, GROUP)] = acc
            elif j == 4:
                qkv_ref[:, pl.ds(0, GROUP)] = _bf(acc * (B_D ** -0.5 * LOG2E))
            elif j < 7:
                qkv_ref[:, pl.ds((j - 4) * GROUP, GROUP)] = _bf(acc)
            else:
                bg_ref[...] = acc

    return pl.pallas_call(
        body, name="inproj", grid=(t // tm,),
        in_specs=[pl.BlockSpec((tm, D_MODEL), lambda i: (i, 0)),
                  pl.BlockSpec((1, D_MODEL), lambda i: (0, 0)),
                  pl.BlockSpec((N_SHARD, D_MODEL, D_MODEL), lambda i: (0, layer, 0))],
        out_specs=[pl.BlockSpec((tm, 4 * GROUP), lambda i: (i, 0)), pl.BlockSpec((tm, 3 * GROUP), lambda i: (i, 0)),
                   pl.BlockSpec((tm, GROUP), lambda i: (i, 0))],
        out_shape=[_sds((t, 4 * GROUP)), _sds((t, 3 * GROUP), BF16), _sds((t, GROUP))],
        compiler_params=_params(("arbitrary",)),
    )(h, g, gathered)


def _rows_spec(first_row):
    q = D_MODEL // N_SHARD
    return pl.BlockSpec((N_SHARD, q, D_MODEL), lambda i: (0, first_row // q, 0))


def _outproj(h, ya, yb, gathered, layer):
    t = h.shape[0]
    tm = _row_tile(t, 512)

    def body(h_ref, ya_ref, yb_ref, w_ref, o_ref):
        o_ref[...] = (h_ref[...] + _dot(_bf(ya_ref[...]), w_ref[0:2].reshape(GROUP, D_MODEL))
                      + _dot(_bf(yb_ref[...]), w_ref[2:4].reshape(GROUP, D_MODEL)))

    return pl.pallas_call(
        body, name="outproj", grid=(t // tm,),
        in_specs=[pl.BlockSpec((tm, D_MODEL), lambda i: (i, 0)),
                  pl.BlockSpec((tm, GROUP), lambda i: (i, 0)),
                  pl.BlockSpec((tm, GROUP), lambda i: (i, 0)),
                  _rows_spec(ROWS_W_IN + layer * (D_MODEL // N_SHARD))],
        out_specs=pl.BlockSpec((tm, D_MODEL), lambda i: (i, 0)),
        out_shape=_sds((t, D_MODEL)), compiler_params=_params(("arbitrary",)),
    )(h, ya, yb, gathered)


def _ple_mix(x, p_ref, wpp_ref, wpg_ref, gp_ref, gg_ref):
    ph, _ = _rms(_dot(_bf(p_ref[...]), wpp_ref[...]))
    xh, _ = _rms(x)
    gate = _sigmoid(_dot(_bf(xh * gg_ref[...]), wpg_ref[...].reshape(D_MODEL, D_MODEL)))
    return x + gate * (ph * gp_ref[...])


def _ple_specs(tm, layer):
    return [pl.BlockSpec((tm, D_MODEL), lambda i: (i, 0)),
            pl.BlockSpec((tm, D_PLE), lambda i: (i, 0)),
            pl.BlockSpec((D_PLE, D_MODEL), lambda i: (0, 0)),
            _rows_spec(ROWS_W_IN + ROWS_W_OUT + layer * (D_MODEL // N_SHARD)),
            pl.BlockSpec((1, D_MODEL), lambda i: (0, 0)),
            pl.BlockSpec((1, D_MODEL), lambda i: (0, 0))]


def _ple_fwd(h, p, w_pp, gathered, layer, g_post, g_gate):
    t = h.shape[0]
    tm = _row_tile(t, 256)

    def body(h_ref, p_ref, wpp_ref, wpg_ref, gp_ref, gg_ref, o_ref):
        o_ref[...] = _ple_mix(h_ref[...], p_ref, wpp_ref, wpg_ref, gp_ref, gg_ref)

    return pl.pallas_call(
        body, name="ple_fwd", grid=(t // tm,), in_specs=_ple_specs(tm, layer),
        out_specs=pl.BlockSpec((tm, D_MODEL), lambda i: (i, 0)),
        out_shape=_sds((t, D_MODEL)), compiler_params=_params(("arbitrary",)),
    )(h, p, w_pp, gathered, g_post, g_gate)


def _ple_fwd_final(h, p, w_pp, gathered, layer, g_post, g_gate, g_final, target):
    t = h.shape[0]
    tm = _row_tile(t, 256)

    def body(h_ref, p_ref, wpp_ref, wpg_ref, gp_ref, gg_ref, gf_ref, t_ref, dh_ref, dg_ref, loss_ref):
        @pl.when(pl.program_id(0) == 0)
        def _():
            dg_ref[...] = jnp.zeros_like(dg_ref)
            loss_ref[...] = jnp.zeros_like(loss_ref)

        xh, r = _rms(_ple_mix(h_ref[...], p_ref, wpp_ref, wpg_ref, gp_ref, gg_ref))
        gf = gf_ref[...]
        err = xh * gf - t_ref[...]
        part = 0.5 * jnp.sum(jnp.mean(err * err, axis=-1, keepdims=True), axis=0, keepdims=True)
        loss_ref[...] += jnp.broadcast_to(part, loss_ref.shape)
        dy = err * (1.0 / D_MODEL)
        dg_ref[...] += jnp.sum(dy * xh, axis=0, keepdims=True)
        dh_ref[...] = _rms_bwd(dy * gf, xh, r)

    return pl.pallas_call(
        body, name="ple_fwd_final", grid=(t // tm,),
        in_specs=_ple_specs(tm, layer) + [pl.BlockSpec((1, D_MODEL), lambda i: (0, 0)),
                                          pl.BlockSpec((tm, D_MODEL), lambda i: (i, 0))],
        out_specs=[pl.BlockSpec((tm, D_MODEL), lambda i: (i, 0)),
                   pl.BlockSpec((1, D_MODEL), lambda i: (0, 0)),
                   pl.BlockSpec((1, 128), lambda i: (0, 0))],
        out_shape=[_sds((t, D_MODEL)), _sds((1, D_MODEL)), _sds((1, 128))],
        compiler_params=_params(("arbitrary",)),
    )(h, p, w_pp, gathered, g_post, g_gate, g_final, target)


def _ple_bwd(dh2, h, p, w_pp, gathered, layer, g_post, g_gate, exchange=None):
    t = h.shape[0]
    tm = _row_tile(t, 512)

    def body(d_ref, h_ref, p_ref, wpp_ref, wpg_ref, gp_ref, gg_ref, dh_ref, dwpg_ref, dwpp_ref, dgg_ref, dgp_ref):
        @pl.when(pl.program_id(0) == 0)
        def _():
            dwpg_ref[...] = jnp.zeros_like(dwpg_ref)
            dwpp_ref[...] = jnp.zeros_like(dwpp_ref)
            dgg_ref[...] = jnp.zeros_like(dgg_ref)
            dgp_ref[...] = jnp.zeros_like(dgp_ref)

        d = d_ref[...]
        x = h_ref[...]
        gp = gp_ref[...]
        gg = gg_ref[...]
        pb = _bf(p_ref[...])
        ph, rp = _rms(_dot(pb, wpp_ref[...]))
        pe = ph * gp
        xh, rx = _rms(x)
        un = _bf(xh * gg)
        wpg = wpg_ref[...].reshape(D_MODEL, D_MODEL)
        gate = _sigmoid(_dot(un, wpg))
        dgpre = _bf(d * pe * gate * (1.0 - gate))
        dun = _dot_nt(dgpre, wpg)
        dh_ref[...] = d + _rms_bwd(dun * gg, xh, rx)
        dgg_ref[...] += jnp.sum(dun * xh, axis=0, keepdims=True)
        dwpg_ref[...] += _dot_tn(un, dgpre)
        dpe = d * gate
        dgp_ref[...] += jnp.sum(dpe * ph, axis=0, keepdims=True)
        dwpp_ref[...] += _dot_tn(pb, _bf(_rms_bwd(dpe * gp, ph, rp)))

    in_specs = [pl.BlockSpec((tm, D_MODEL), lambda i: (i, 0)),
                pl.BlockSpec((tm, D_MODEL), lambda i: (i, 0)),
                pl.BlockSpec((tm, D_PLE), lambda i: (i, 0)),
                pl.BlockSpec((D_PLE, D_MODEL), lambda i: (0, 0)),
                _rows_spec(ROWS_W_IN + ROWS_W_OUT + layer * (D_MODEL // N_SHARD)),
                pl.BlockSpec((1, D_MODEL), lambda i: (0, 0)),
                pl.BlockSpec((1, D_MODEL), lambda i: (0, 0))]
    out_specs = [pl.BlockSpec((tm, D_MODEL), lambda i: (i, 0)),
                 pl.BlockSpec((D_MODEL, D_MODEL), lambda i: (0, 0)),
                 pl.BlockSpec((D_PLE, D_MODEL), lambda i: (0, 0)),
                 pl.BlockSpec((1, D_MODEL), lambda i: (0, 0)),
                 pl.BlockSpec((1, D_MODEL), lambda i: (0, 0))]
    out_shape = [_sds((t, D_MODEL)), _sds((D_MODEL, D_MODEL)), _sds((D_PLE, D_MODEL)),
                 _sds((1, D_MODEL)), _sds((1, D_MODEL))]
    operands = [dh2, h, p, w_pp, gathered, g_post, g_gate]
    scratch = []
    if exchange is not None:
        body = _with_exchange(body, 7, 5, exchange, lambda: pl.program_id(0), t // tm)
        xi, xo, xs, scratch, xop = _exchange_args(exchange)
        in_specs, out_specs, out_shape, operands = in_specs + xi, out_specs + xo, out_shape + xs, operands + xop
    return pl.pallas_call(
        body, name="ple_bwd" if exchange is None else "ple_bwd_exchange", grid=(t // tm,),
        in_specs=in_specs, out_specs=out_specs, out_shape=out_shape, scratch_shapes=scratch,
        compiler_params=_params(("arbitrary",)),
    )(*operands)


def _outproj_bwd(dh, ya, yb, gathered, layer):
    t = dh.shape[0]
    tm = _row_tile(t, 512)

    def body(d_ref, ya_ref, yb_ref, w_ref, dya_ref, dyb_ref, dw_ref):
        @pl.when(pl.program_id(0) == 0)
        def _():
            dw_ref[...] = jnp.zeros_like(dw_ref)

        d = _bf(d_ref[...])
        dya_ref[...] = _dot_nt(d, w_ref[0:2].reshape(GROUP, D_MODEL))
        dyb_ref[...] = _dot_nt(d, w_ref[2:4].reshape(GROUP, D_MODEL))
        dw_ref[pl.ds(0, GROUP), :] += _dot_tn(_bf(ya_ref[...]), d)
        dw_ref[pl.ds(GROUP, GROUP), :] += _dot_tn(_bf(yb_ref[...]), d)

    return pl.pallas_call(
        body, name="outproj_bwd", grid=(t // tm,),
        in_specs=[pl.BlockSpec((tm, D_MODEL), lambda i: (i, 0)),
                  pl.BlockSpec((tm, GROUP), lambda i: (i, 0)),
                  pl.BlockSpec((tm, GROUP), lambda i: (i, 0)),
                  _rows_spec(ROWS_W_IN + layer * (D_MODEL // N_SHARD))],
        out_specs=[pl.BlockSpec((tm, GROUP), lambda i: (i, 0)),
                   pl.BlockSpec((tm, GROUP), lambda i: (i, 0)),
                   pl.BlockSpec((D_MODEL, D_MODEL), lambda i: (0, 0))],
        out_shape=[_sds((t, GROUP)), _sds((t, GROUP)), _sds((D_MODEL, D_MODEL))],
        compiler_params=_params(("arbitrary",)),
    )(dh, ya, yb, gathered)


def _inproj_bwd_dx(dres, h, g, gathered, layer, da, db, exchange=None):
    t = h.shape[0]
    tm = _row_tile(t, 512)

    def body(dres_ref, h_ref, g_ref, w_ref, da_ref, db_ref, dh_ref, dg_ref):
        @pl.when(pl.program_id(0) == 0)
        def _():
            dg_ref[...] = jnp.zeros_like(dg_ref)

        du = jnp.zeros((tm, D_MODEL), F32)
        for i in range(8):
            part = da_ref[i] if i < 4 else db_ref[i - 4]
            du = du + _dot_nt(part, w_ref[i // 2, :, pl.ds((i % 2) * GROUP, GROUP)])
        xh, r = _rms(h_ref[...])
        dg_ref[...] += jnp.sum(du * xh, axis=0, keepdims=True)
        dh_ref[...] = dres_ref[...] + _rms_bwd(du * g_ref[...], xh, r)

    in_specs = [pl.BlockSpec((tm, D_MODEL), lambda i: (i, 0)),
                pl.BlockSpec((tm, D_MODEL), lambda i: (i, 0)),
                pl.BlockSpec((1, D_MODEL), lambda i: (0, 0)),
                pl.BlockSpec((N_SHARD, D_MODEL, D_MODEL), lambda i: (0, layer, 0)),
                pl.BlockSpec((4, tm, GROUP), lambda i: (0, i, 0)),
                pl.BlockSpec((4, tm, GROUP), lambda i: (0, i, 0))]
    out_specs = [pl.BlockSpec((tm, D_MODEL), lambda i: (i, 0)), pl.BlockSpec((1, D_MODEL), lambda i: (0, 0))]
    out_shape = [_sds((t, D_MODEL)), _sds((1, D_MODEL))]
    operands = [dres, h, g, gathered, da, db]
    scratch = []
    if exchange is not None:
        body = _with_exchange(body, 6, 2, exchange, lambda: pl.program_id(0), t // tm)
        xi, xo, xs, scratch, xop = _exchange_args(exchange)
        in_specs, out_specs, out_shape, operands = in_specs + xi, out_specs + xo, out_shape + xs, operands + xop
    return pl.pallas_call(
        body, name="inproj_bwd_dx" if exchange is None else "inproj_bwd_dx_exchange", grid=(t // tm,),
        in_specs=in_specs, out_specs=out_specs, out_shape=out_shape, scratch_shapes=scratch,
        compiler_params=_params(("arbitrary",)),
    )(*operands)


def _inproj_bwd_dw(h, g, da, db):
    t = h.shape[0]
    tm = _row_tile(t, 512)

    def body(h_ref, g_ref, da_ref, db_ref, dw_ref):
        @pl.when(pl.program_id(0) == 0)
        def _():
            dw_ref[...] = jnp.zeros_like(dw_ref)

        xh, _ = _rms(h_ref[...])
        u = _bf(xh * g_ref[...])
        for i in range(8):
            dw_ref[i // 2, :, pl.ds((i % 2) * GROUP, GROUP)] += _dot_tn(u, da_ref[i] if i < 4 else db_ref[i - 4])

    return pl.pallas_call(
        body, name="inproj_bwd_dw", grid=(t // tm,),
        in_specs=[pl.BlockSpec((tm, D_MODEL), lambda i: (i, 0)),
                  pl.BlockSpec((1, D_MODEL), lambda i: (0, 0)),
                  pl.BlockSpec((4, tm, GROUP), lambda i: (0, i, 0)),
                  pl.BlockSpec((4, tm, GROUP), lambda i: (0, i, 0))],
        out_specs=pl.BlockSpec((N_SHARD, D_MODEL, D_MODEL), lambda i: (0, 0, 0)),
        out_shape=_sds((N_SHARD, ROWS_LAYER, D_MODEL)), compiler_params=_params(("arbitrary",), VMEM_LIMIT_BIG),
    )(h, g, da, db)


def _hgrn_consts():
    c, nl = HG_CHUNK, HG_LEVELS
    t = np.arange(c)
    tril = np.tril(np.ones((c, c), np.float32))
    masks = np.zeros((nl + 1, c, c), np.float32)
    masks[0] = np.eye(c, dtype=np.float32)
    dmat = np.zeros(((nl + 2) * c, c), np.float32)
    dmat[0:c] = tril
    for l in range(nl):
        m = c >> (l + 1)
        blk = t // (2 * m)
        r = blk * 2 * m + m - 1
        upper = (t % (2 * m)) >= m
        masks[l + 1] = ((blk[:, None] == blk[None, :]) & upper[:, None] & (~upper)[None, :]).astype(np.float32)
        dmat[(l + 1) * c:(l + 2) * c] = tril[t] - tril[r]
    dmat[(nl + 1) * c:] = np.triu(np.ones((c, c), np.float32), k=1)
    return jnp.asarray(masks), jnp.asarray(dmat, BF16)


HG_HEADS = 4


def _hgrn_pre(aq, af, lb):
    sq = _sigmoid(aq)
    sneg = _sigmoid(-af)
    kk = (1.0 - lb) * sneg
    return sq, aq * sq, sneg, kk, jnp.log1p(-kk)


def _hgrn_x(logf, dmat_ref):
    dm = dmat_ref[pl.ds(0, (HG_LEVELS + 1) * HG_CHUNK), :]
    lhi, llo = _split(logf)
    return _dot(dm, lhi) + _dot(dm, llo)


def _hgrn_level(x_all, l, q, kk):
    c = HG_CHUNK
    x = x_all[(l + 1) * c:(l + 2) * c]
    qf = jnp.exp(jnp.minimum(x, 0.0))
    kf = jnp.exp(-jnp.maximum(x, 0.0))
    return qf, kf, _bf(q * qf), _bf(kk * kf)


def _hgrn_scores(xs, qs, kks, mask_ref):
    ps = [mask_ref[0] * _dot_nt(_bf(q), _bf(kk)) for q, kk in zip(qs, kks)]
    for l in range(HG_LEVELS):
        for i, (x_all, q, kk) in enumerate(zip(xs, qs, kks)):
            _, _, ql, kl = _hgrn_level(x_all, l, q, kk)
            ps[i] = ps[i] + mask_ref[l + 1] * _dot_nt(ql, kl)
    return ps


def _hgrn_specs(n_chunks, rev):
    c, w = HG_CHUNK, HG_HEADS * A_D
    cidx = (lambda n: n_chunks - 1 - n) if rev else (lambda n: n)
    col = lambda g: pl.BlockSpec((c, w), lambda h, n: (cidx(n), g * (A_HEADS // HG_HEADS) + h))
    vec = pl.BlockSpec((1, w), lambda h, n: (0, h))
    mask = pl.BlockSpec((HG_LEVELS + 1, c, c), lambda h, n: (0, 0, 0))
    dmat = pl.BlockSpec(((HG_LEVELS + 2) * c, c), lambda h, n: (0, 0))
    state = pl.BlockSpec((HG_HEADS, None, A_D, A_D), lambda h, n: (h, cidx(n), 0, 0))
    return cidx, col, vec, mask, dmat, state


def _lanes(i):
    return pl.ds(i * A_D, A_D)


def _hgrn_fwd(proj, lb, gain):
    t = proj.shape[0]
    c = HG_CHUNK
    nch = t // c
    masks, dmat = _hgrn_consts()
    cidx, col, vec, mask_spec, dmat_spec, state_spec = _hgrn_specs(nch, False)
    heads = range(HG_HEADS)

    def body(aq_ref, af_ref, ai_ref, ag_ref, lb_ref, gain_ref, mask_ref, dmat_ref, y_ref, st_ref, s_scr):
        @pl.when(pl.program_id(1) == 0)
        def _():
            s_scr[...] = jnp.zeros_like(s_scr)

        pre = [_hgrn_pre(aq_ref[:, _lanes(i)], af_ref[:, _lanes(i)], lb_ref[:, _lanes(i)]) for i in heads]
        qs, kks = [p[1] for p in pre], [p[3] for p in pre]
        xs = [_hgrn_x(p[4], dmat_ref) for p in pre]
        bs = [x[0:c] for x in xs]
        b_lasts = [jnp.sum(p[4], axis=0, keepdims=True) for p in pre]
        ps = _hgrn_scores(xs, qs, kks, mask_ref)
        ss = [s_scr[i] for i in heads]
        vbs = [_bf(ai_ref[:, _lanes(i)]) for i in heads]
        os_ = [_dot(_bf(ps[i]), vbs[i]) + _dot_nt(_bf(qs[i] * jnp.exp(bs[i])), _bf(ss[i])) for i in heads]
        for i in heads:
            st_ref[i] = ss[i]
            s_scr[i] = ss[i] * jnp.exp(b_lasts[i]) + _dot_tn(vbs[i], _bf(kks[i] * jnp.exp(b_lasts[i] - bs[i])))
            oh, _ = _rms(os_[i])
            ag = ag_ref[:, _lanes(i)]
            y_ref[:, _lanes(i)] = oh * gain_ref[:, _lanes(i)] * (ag * _sigmoid(ag))

    return pl.pallas_call(
        body, name="hgrn_fwd", grid=(A_HEADS // HG_HEADS, nch),
        in_specs=[col(0), col(1), col(2), col(3), vec, vec, mask_spec, dmat_spec],
        out_specs=[pl.BlockSpec((c, HG_HEADS * A_D), lambda h, n: (n, h)), state_spec],
        out_shape=[_sds((t, GROUP)), _sds((A_HEADS, nch, A_D, A_D))],
        scratch_shapes=[pltpu.VMEM((HG_HEADS, A_D, A_D), F32)],
        compiler_params=_params(("arbitrary", "arbitrary")),
    )(proj, proj, proj, proj, lb, gain, masks, dmat)


def _hgrn_bwd(proj, lb, gain, states, dya):
    t = proj.shape[0]
    c, nl = HG_CHUNK, HG_LEVELS
    nch = t // c
    masks, dmat = _hgrn_consts()
    cidx, col, vec, mask_spec, dmat_spec, state_spec = _hgrn_specs(nch, True)
    heads = range(HG_HEADS)

    def body(aq_ref, af_ref, ai_ref, ag_ref, lb_ref, gain_ref, mask_ref, dmat_ref, st_ref, dy_ref,
             da_ref, dlb_ref, dgain_ref, ds_scr, z_scr):
        @pl.when(pl.program_id(1) == 0)
        def _():
            ds_scr[...] = jnp.zeros_like(ds_scr)
            dlb_ref[...] = jnp.zeros_like(dlb_ref)
            dgain_ref[...] = jnp.zeros_like(dgain_ref)

        aqs = [aq_ref[:, _lanes(i)] for i in heads]
        lbs = [lb_ref[:, _lanes(i)] for i in heads]
        pre = [_hgrn_pre(aqs[i], af_ref[:, _lanes(i)], lbs[i]) for i in heads]
        sqs, qs, snegs, kks = ([p[j] for p in pre] for j in range(4))
        xs = [_hgrn_x(p[4], dmat_ref) for p in pre]
        bs = [x[0:c] for x in xs]
        b_lasts = [jnp.sum(p[4], axis=0, keepdims=True) for p in pre]
        ebs = [jnp.exp(b) for b in bs]
        ebls = [jnp.exp(bl - b) for bl, b in zip(b_lasts, bs)]
        ebl_rows = [jnp.exp(bl) for bl in b_lasts]
        qes = [_bf(q * eb) for q, eb in zip(qs, ebs)]
        kes = [_bf(kk * ebl) for kk, ebl in zip(kks, ebls)]
        vbs = [_bf(ai_ref[:, _lanes(i)]) for i in heads]
        ss = [st_ref[i] for i in heads]
        sbs = [_bf(s) for s in ss]
        dss = [ds_scr[i] for i in heads]
        dsbs = [_bf(ds) for ds in dss]

        pbs = [_bf(p) for p in _hgrn_scores(xs, qs, kks, mask_ref)]
        os_ = [_dot(pbs[i], vbs[i]) + _dot_nt(qes[i], sbs[i]) for i in heads]

        dos = []
        for i in heads:
            ag, gain, dy = ag_ref[:, _lanes(i)], gain_ref[:, _lanes(i)], dy_ref[:, _lanes(i)]
            oh, r = _rms(os_[i])
            sg_sig = _sigmoid(ag)
            sg = ag * sg_sig
            da_ref[3, :, _lanes(i)] = _bf(dy * oh * gain * _silu_grad(ag, sg_sig))
            dgain_ref[:, _lanes(i)] += jnp.sum(dy * oh * sg, axis=0, keepdims=True)
            dos.append(_bf(_rms_bwd(dy * gain * sg, oh, r)))

        dps = [_dot_nt(dos[i], vbs[i]) for i in heads]
        for i in heads:
            da_ref[2, :, _lanes(i)] = _bf(_dot_tn(pbs[i], dos[i]) + _dot_nt(kes[i], dsbs[i]))
        dq_ss = [ebs[i] * _dot(dos[i], sbs[i]) for i in heads]
        dk_ss = [ebls[i] * _dot(vbs[i], dsbs[i]) for i in heads]
        dqs, dks = [], []
        for i in heads:
            dpd = jnp.sum(mask_ref[0] * dps[i], axis=1, keepdims=True)
            z_scr[i, pl.ds(0, c), :] = qs[i] * dq_ss[i]
            z_scr[i, pl.ds((nl + 1) * c, c), :] = kks[i] * dk_ss[i]
            dqs.append(dq_ss[i] + dpd * kks[i])
            dks.append(dk_ss[i] + dpd * qs[i])
        for l in range(nl):
            for i in heads:
                qf, kf, ql, kl = _hgrn_level(xs[i], l, qs[i], kks[i])
                dpl = _bf(mask_ref[l + 1] * dps[i])
                dq_l = qf * _dot(dpl, kl)
                dk_l = kf * _dot_tn(dpl, ql)
                z_scr[i, pl.ds((l + 1) * c, c), :] = qs[i] * dq_l - kks[i] * dk_l
                dqs[i] = dqs[i] + dq_l
                dks[i] = dks[i] + dk_l

        zsplits = [_split(z_scr[i]) for i in heads]
        dlogfs = [_dot_tn(dmat_ref[...], zhi) + _dot_tn(dmat_ref[...], zlo) for zhi, zlo in zsplits]
        ds_new = [_dot_tn(dos[i], qes[i]) for i in heads]
        for i in heads:
            dlogf = dlogfs[i] + ebl_rows[i] * jnp.sum(dss[i] * ss[i], axis=0, keepdims=True)
            dkk = dks[i] - dlogf / (1.0 - kks[i])
            da_ref[1, :, _lanes(i)] = _bf(dkk * (1.0 - lbs[i]) * (-(snegs[i] * (1.0 - snegs[i]))))
            dlb_ref[:, _lanes(i)] += jnp.sum(dkk * (-snegs[i]), axis=0, keepdims=True)
            da_ref[0, :, _lanes(i)] = _bf(dqs[i] * _silu_grad(aqs[i], sqs[i]))
            ds_scr[i] = dss[i] * ebl_rows[i] + ds_new[i]

    w = HG_HEADS * A_D
    return pl.pallas_call(
        body, name="hgrn_bwd", grid=(A_HEADS // HG_HEADS, nch),
        in_specs=[col(0), col(1), col(2), col(3), vec, vec, mask_spec, dmat_spec, state_spec,
                  pl.BlockSpec((c, w), lambda h, n: (cidx(n), h))],
        out_specs=[pl.BlockSpec((4, c, w), lambda h, n: (0, cidx(n), h)), vec, vec],
        out_shape=[_sds((4, t, GROUP), BF16)] + [_sds((1, GROUP))] * 2,
        scratch_shapes=[pltpu.VMEM((HG_HEADS, A_D, A_D), F32), pltpu.VMEM((HG_HEADS, (nl + 2) * c, A_D), F32)],
        compiler_params=_params(("arbitrary", "arbitrary")),
    )(proj, proj, proj, proj, lb, gain, masks, dmat, states, dya)


def _sb_consts():
    j = np.arange(SB_TK)
    strict = (j[:, None] > j[None, :]).astype(np.float32)
    lower = (j[:, None] < j[None, :]).astype(np.float32)
    return jnp.asarray(strict, BF16), jnp.asarray(lower, BF16)


def _lane(x, k):
    return jnp.broadcast_to(x[:, k:k + 1], x.shape)


def _key_lane(kb):
    return lax.broadcasted_iota(jnp.int32, (1, SB_TK), 1) == kb


def _causal(x, masked):
    if not masked:
        return x
    n = (SB_TK, SB_TK)
    top = jnp.where(lax.broadcasted_iota(jnp.int32, n, 1) < lax.broadcasted_iota(jnp.int32, n, 0), x[:SB_TK], 0.0)
    return top if x.shape[0] == SB_TK else jnp.concatenate([top, x[SB_TK:]], axis=0)


def _sb_softplus(z, masked):
    logsig = jnp.minimum(z, 0.0) - jnp.log2(1.0 + jnp.exp2(-jnp.abs(z)))
    return _causal(z - logsig, masked), logsig


def _sb_sweep(qi, group_fn, state, groups, ascending):
    nd = SB_TQ // SB_TK

    def tile(kb, r0, masked):
        return (pl.multiple_of(kb * SB_TK, SB_TK), r0, masked, kb)

    def run(tiles, st, group):
        for i in range(0, len(tiles), group):
            st = group_fn(tiles[i:i + group], st)
        return st

    diag = [tile(qi * nd + d, d * SB_TK, True) for d in range(nd)]
    if ascending:
        state = lax.fori_loop(
            0, qi, lambda j, st: run([tile(j * nd + g, 0, False) for g in range(nd)], st, groups[1]), state)
        return run(diag, state, groups[0])
    state = run(diag[::-1], state, groups[0])
    return lax.fori_loop(
        0, qi, lambda j, st: run([tile((qi - j) * nd - 1 - g, 0, False) for g in range(nd)], st, groups[1]), state)


def _set_rows(r0, full, new):
    return new if r0 == 0 else jnp.concatenate([full[:r0], new], axis=0)


def _sb_specs(t, tq):
    col = lambda g: pl.BlockSpec((tq, 2 * B_D), lambda p, i, h: (i, g * (GROUP // (2 * B_D)) + p))
    full = lambda g: pl.BlockSpec((t, 2 * B_D), lambda p, i, h: (0, g * (GROUP // (2 * B_D)) + p))
    vec = pl.BlockSpec((1, 2 * B_D), lambda p, i, h: (0, p))
    mat = pl.BlockSpec((SB_TK, SB_TK), lambda p, i, h: (0, 0))
    car = pl.BlockSpec((None, tq, SB_TK), lambda p, i, h: (2 * p + h, i, 0))
    return col, full, vec, mat, car


def _head_lanes(h):
    return (lax.broadcasted_iota(jnp.int32, (1, 2 * B_D), 1) >= B_D) == (h == 1)


def _put(ref, h, val):
    @pl.when(h == 0)
    def _():
        ref[...] = val

    @pl.when(h == 1)
    def _():
        ref[...] += val


def _sb_fwd(qkv, bgate, gain, gather=None):
    t = qkv.shape[0]
    tq = SB_TQ
    strict, _ = _sb_consts()
    n_steps = (B_HEADS // 2, t // tq, 2)

    def body(q_ref, k_ref, v_ref, bg_ref, gain_ref, m_ref, *rest):
        if gather is None:
            o_ref, y_ref, car_ref = rest
        else:
            flat_ref, _, o_ref, y_ref, car_ref, gathered_ref, send_sems, recv_sems = rest
            start, forward, finish = _gather_plan(flat_ref, gathered_ref, send_sems, recv_sems, *gather[2:])
            step = (pl.program_id(0) * n_steps[1] + pl.program_id(1)) * n_steps[2] + pl.program_id(2)
            pl.when(step == 0)(start)
            pl.when(step == 2 * n_steps[1] * n_steps[2])(forward)
            pl.when(step == n_steps[0] * n_steps[1] * n_steps[2] - 1)(finish)
        h = pl.program_id(2)
        lanes = _head_lanes(h)
        qb = jnp.where(lanes, q_ref[...], jnp.zeros_like(q_ref))
        cmat = m_ref[...]

        def group(tiles, state):
            carry, acc, cars = state
            kv = [(k_ref[pl.ds(off, SB_TK), :], v_ref[pl.ds(off, SB_TK), :]) for off, _, _, _ in tiles]
            zs = [_dot_nt(qb[r0:], kb) for (_, r0, _, _), (kb, _) in zip(tiles, kv)]
            sps = [_sb_softplus(z, masked) for z, (_, _, masked, _) in zip(zs, tiles)]
            css = [_dot(_bf(sp), cmat) for sp, _ in sps]
            ws = []
            for (sp, logsig), cs, (_, r0, masked, kb) in zip(sps, css, tiles):
                ws.append(_bf(_causal(jnp.exp2(logsig - cs - carry[r0:]), masked)))
                cars = _set_rows(r0, cars, jnp.where(_key_lane(kb), carry[r0:], cars[r0:]))
                carry = _set_rows(r0, carry, carry[r0:] + _lane(cs + sp, 0))
            for w, (_, vb), (_, r0, _, _) in zip(ws, kv, tiles):
                acc = _set_rows(r0, acc, acc[r0:] + _dot(w, vb))
            return carry, acc, cars

        zero = jnp.zeros((tq, SB_TK), F32)
        _, acc, cars = _sb_sweep(pl.program_id(1), group, (zero, jnp.zeros((tq, 2 * B_D), F32), zero),
                                 SB_GROUP_FWD, False)
        car_ref[...] = cars
        o = jnp.where(lanes, acc, 0.0)
        oh = o * lax.rsqrt(jnp.sum(o * o, axis=-1, keepdims=True) * (1.0 / B_D) + EPS)
        bg = bg_ref[...]
        _put(o_ref, h, o)
        _put(y_ref, h, oh * gain_ref[...] * (bg * _sigmoid(bg)))

    col, full, vec, mat, car = _sb_specs(t, tq)
    out = pl.BlockSpec((tq, 2 * B_D), lambda p, i, h: (i, p))
    in_specs = [col(0), full(1), full(2), col(0), vec, mat]
    out_specs = [out, out, car]
    out_shape = [_sds((t, GROUP)), _sds((t, GROUP)), _sds((B_HEADS, t, SB_TK))]
    operands = [qkv, qkv, qkv, bgate, gain, strict]
    extra = {}
    if gather is not None:
        in_specs += [_ANY, _ANY]
        out_specs += [_ANY]
        out_shape += [_sds(gather[1].shape, gather[1].dtype)]
        operands += [gather[0], gather[1]]
        extra = dict(input_output_aliases={7: 3}, scratch_shapes=_gather_sems(gather[4]))
    return pl.pallas_call(
        body, name="sb_fwd" if gather is None else "sb_fwd_gather", grid=n_steps,
        in_specs=in_specs, out_specs=out_specs, out_shape=out_shape,
        compiler_params=_params(("arbitrary", "arbitrary", "arbitrary")), **extra,
    )(*operands)


def _sb_bwd(qkv, bgate, o, carries, dy, gain, exchange=None):
    t = qkv.shape[0]
    tq = SB_TQ
    strict, lower = _sb_consts()

    def body(q_ref, k_ref, v_ref, bg_ref, o_ref, car_ref, dy_ref, gain_ref, ms_ref, ml_ref,
             dq_ref, dk_ref, dv_ref, dbg_ref, dgain_ref, qb_scr, do_scr):
        qi = pl.program_id(1)
        h = pl.program_id(2)
        lanes = _head_lanes(h)

        @pl.when((qi == 0) & (h == 0))
        def _():
            dk_ref[...] = jnp.zeros_like(dk_ref)
            dv_ref[...] = jnp.zeros_like(dv_ref)
            dgain_ref[...] = jnp.zeros_like(dgain_ref)

        qb_scr[...] = jnp.where(lanes, q_ref[...], jnp.zeros_like(q_ref))
        cmat = ms_ref[...]
        lmat = ml_ref[...]
        o = jnp.where(lanes, o_ref[...], 0.0)
        dy = jnp.where(lanes, dy_ref[...], 0.0)
        bg = bg_ref[...]
        gain = gain_ref[...]
        r = lax.rsqrt(jnp.sum(o * o, axis=-1, keepdims=True) * (1.0 / B_D) + EPS)
        oh = o * r
        sig = _sigmoid(bg)
        sg = bg * sig
        _put(dbg_ref, h, dy * oh * gain * _silu_grad(bg, sig))
        dgain_ref[...] += jnp.sum(dy * oh * sg, axis=0, keepdims=True)
        doh = dy * gain * sg
        do_scr[...] = _bf(r * (doh - oh * (jnp.sum(doh * oh, axis=-1, keepdims=True) * (1.0 / B_D))))

        def group(tiles, state):
            gleft, dq = state
            kv = [(k_ref[pl.ds(off, SB_TK), :], v_ref[pl.ds(off, SB_TK), :]) for off, _, _, _ in tiles]
            zs = [_dot_nt(qb_scr[r0:, :], kb) for (_, r0, _, _), (kb, _) in zip(tiles, kv)]
            dws = [_dot_nt(do_scr[r0:, :], vb) for (_, r0, _, _), (_, vb) in zip(tiles, kv)]
            sps = [_sb_softplus(z, masked) for z, (_, _, masked, _) in zip(zs, tiles)]
            css = [_dot(_bf(sp), cmat) for sp, _ in sps]
            ws, gs = [], []
            for (_, logsig), cs, dw, (_, r0, masked, kb) in zip(sps, css, dws, tiles):
                right = jnp.sum(jnp.where(_key_lane(kb), car_ref[r0:, :], 0.0), axis=1, keepdims=True)
                w = _causal(jnp.exp2(logsig - cs - right), masked)
                ws.append(_bf(w))
                gs.append(dw * w)
            gps = [_dot(_bf(g), lmat) for g in gs]
            dzs = []
            for (_, logsig), g, gp, (_, r0, masked, _) in zip(sps, gs, gps, tiles):
                dz = g - jnp.exp2(logsig) * (g + gleft[r0:] + gp)
                dzs.append(_bf(_causal(dz, masked)))
                gleft = _set_rows(r0, gleft, gleft[r0:] + _lane(gp + g, SB_TK - 1))
            for dz, wb, (kb, _), (off, r0, _, _) in zip(dzs, ws, kv, tiles):
                dq = _set_rows(r0, dq, dq[r0:] + _dot(dz, kb))
                dk_ref[pl.ds(off, SB_TK), :] += _dot_tn(dz, qb_scr[r0:, :])
                dv_ref[pl.ds(off, SB_TK), :] += _dot_tn(wb, do_scr[r0:, :])
            return gleft, dq

        _, dq = _sb_sweep(qi, group, (jnp.zeros((tq, SB_TK), F32), jnp.zeros((tq, 2 * B_D), F32)), SB_GROUP_BWD, True)
        _put(dq_ref, h, jnp.where(lanes, dq * (B_D ** -0.5), 0.0))

    col, full, vec, mat, car = _sb_specs(t, tq)
    blk = pl.BlockSpec((tq, 2 * B_D), lambda p, i, h: (i, p))
    whole = pl.BlockSpec((t, 2 * B_D), lambda p, i, h: (0, p))
    grid = (B_HEADS // 2, t // tq, 2)
    in_specs = [col(0), full(1), full(2), col(0), blk, car, blk, vec, mat, mat]
    out_specs = [blk, whole, whole, blk, vec]
    out_shape = [_sds((t, GROUP))] * 4 + [_sds((1, GROUP))]
    operands = [qkv, qkv, qkv, bgate, o, carries, dy, gain, strict, lower]
    scratch = []
    if exchange is not None:
        step_of = lambda: (pl.program_id(0) * grid[1] + pl.program_id(1)) * grid[2] + pl.program_id(2)
        body = _with_exchange(body, 10, 5, exchange, step_of, grid[0] * grid[1] * grid[2])
        xi, xo, xs, scratch, xop = _exchange_args(exchange)
        in_specs, out_specs, out_shape, operands = in_specs + xi, out_specs + xo, out_shape + xs, operands + xop
    scratch = scratch + [pltpu.VMEM((tq, 2 * B_D), BF16), pltpu.VMEM((tq, 2 * B_D), BF16)]
    return pl.pallas_call(
        body, name="sb_bwd" if exchange is None else "sb_bwd_exchange", grid=grid,
        in_specs=in_specs, out_specs=out_specs, out_shape=out_shape, scratch_shapes=scratch,
        compiler_params=_params(("arbitrary", "arbitrary", "arbitrary")),
    )(*operands)


def _adamw(w, g, m, v):
    rows, cols = w.shape
    tr = rows
    for cand in (400, 256, 128, 64, 32, 16, 8):
        if rows % cand == 0:
            tr = cand
            break

    def body(w_ref, g_ref, m_ref, v_ref, d_ref, nm_ref, nv_ref):
        g_ = g_ref[...]
        m_ = ADAM_B1 * m_ref[...] + (1.0 - ADAM_B1) * g_
        v_ = ADAM_B2 * v_ref[...] + (1.0 - ADAM_B2) * (g_ * g_)
        m_hat = m_ / (1.0 - ADAM_B1 ** ADAM_STEP)
        v_hat = v_ / (1.0 - ADAM_B2 ** ADAM_STEP)
        d_ref[...] = -ADAM_LR * (m_hat / (jnp.sqrt(v_hat) + ADAM_EPS) + ADAM_WD * w_ref[...])
        nm_ref[...] = m_
        nv_ref[...] = v_

    spec = pl.BlockSpec((tr, cols), lambda i: (i, 0))
    return pl.pallas_call(
        body, name="adamw", grid=(rows // tr,), in_specs=[spec] * 4, out_specs=[spec] * 3,
        out_shape=[_sds((rows, cols))] * 3, compiler_params=_params(("arbitrary",)),
    )(w, g, m, v)


_ANY = pl.BlockSpec(memory_space=pl.ANY)


def _place():
    return lax.axis_index("x"), lax.axis_index("y"), lax.axis_index("c")


def _gather_plan(x_ref, out_ref, send_sems, recv_sems, row0, nrows, nc):
    x, y, c = _place()
    me = 2 * x + y
    sibling = (x, y, 1 - c)
    half = nrows // 2
    ch = half // nc
    peers = [me ^ k for k in (1, 2, 3)]

    def rows(shard, hc, r):
        return out_ref.at[shard, pl.ds(row0 + hc * half + r * ch, ch), :]

    def copy(k, shard, hc, r, to, src=None):
        return pltpu.make_async_remote_copy(
            src_ref=rows(shard, hc, r) if src is None else src, dst_ref=rows(shard, hc, r),
            send_sem=send_sems.at[k * nc + r], recv_sem=recv_sems.at[k * nc + r], device_id=to, device_id_type=MESH)

    def first(k, p, r):
        return copy(k, me, c, r, (p >> 1, p & 1, c), src=x_ref.at[pl.ds(row0 + c * half + r * ch, ch), :])

    def start():
        for k, p in enumerate(peers):
            for r in range(nc):
                first(k, p, r).start()

    def forward():
        for k, p in enumerate(peers):
            for r in range(nc):
                copy(k, p, c, r, sibling).wait_recv()
                copy(3 + k, p, c, r, sibling).start()

    def finish():
        for k, p in enumerate(peers):
            for r in range(nc):
                copy(3 + k, p, 1 - c, r, sibling).wait_recv()
        for k, p in enumerate(peers):
            for r in range(nc):
                first(k, p, r).wait_send()
                copy(3 + k, p, c, r, sibling).wait_send()

    return start, forward, finish


def _gather_sems(nc):
    return [pltpu.SemaphoreType.DMA((6 * nc,)), pltpu.SemaphoreType.DMA((6 * nc,))]


def _swap_plan(g_ref, out_ref, send_sems, recv_sems):
    half, ch, nc = HALF_LAYER, CHUNK_ROWS, RS_CHUNKS

    def copies():
        x, y, c = _place()
        return [pltpu.make_async_remote_copy(
            src_ref=g_ref.at[j, pl.ds((1 - c) * half + r * ch, ch), :], dst_ref=out_ref.at[j, pl.ds(r * ch, ch), :],
            send_sem=send_sems.at[j * nc + r], recv_sem=recv_sems.at[j * nc + r],
            device_id=(x, y, 1 - c), device_id_type=MESH) for j in range(N_SHARD) for r in range(nc)]

    def start():
        for cp in copies():
            cp.start()

    def finish():
        for cp in copies():
            cp.wait()

    return start, finish


def _scatter_plan(p_ref, out_ref, send_sems, recv_sems):
    ch, nc = CHUNK_ROWS, RS_CHUNKS

    def copies():
        x, y, c = _place()
        me = 2 * x + y
        return [pltpu.make_async_remote_copy(
            src_ref=p_ref.at[me ^ k, pl.ds(r * ch, ch), :], dst_ref=out_ref.at[k - 1, pl.ds(r * ch, ch), :],
            send_sem=send_sems.at[(k - 1) * nc + r], recv_sem=recv_sems.at[(k - 1) * nc + r],
            device_id=((me ^ k) >> 1, (me ^ k) & 1, c), device_id_type=MESH) for k in (1, 2, 3) for r in range(nc)]

    def start():
        for cp in copies():
            cp.start()

    def finish():
        for cp in copies():
            cp.wait()

    return start, finish


SWAP = (_swap_plan, (N_SHARD, HALF_LAYER, D_MODEL), F32, N_SHARD * RS_CHUNKS)
SCATTER = (_scatter_plan, (3, HALF_LAYER, D_MODEL), BF16, 3 * RS_CHUNKS)


def _exchange_call(kind, operand):
    plan, shape, dtype, n_sems = kind

    def body(in_ref, out_ref, send_sems, recv_sems):
        start, finish = plan(in_ref, out_ref, send_sems, recv_sems)
        start()
        finish()

    return pl.pallas_call(
        body, name="exchange", in_specs=[_ANY], out_specs=_ANY, out_shape=_sds(shape, dtype),
        scratch_shapes=[pltpu.SemaphoreType.DMA((n_sems,)), pltpu.SemaphoreType.DMA((n_sems,))],
    )(operand)


def _with_exchange(body, n_in, n_out, exchange, step_of, n_steps):
    def wrapped(*refs):
        ins, src = refs[:n_in], refs[n_in]
        outs, dst = refs[n_in + 1:n_in + 1 + n_out], refs[n_in + 1 + n_out]
        send_sems, recv_sems = refs[n_in + 2 + n_out:n_in + 4 + n_out]
        start, finish = exchange[0][0](src, dst, send_sems, recv_sems)
        pl.when(step_of() == 0)(start)
        body(*ins, *outs, *refs[n_in + 4 + n_out:])
        pl.when(step_of() == n_steps - 1)(finish)

    return wrapped


def _exchange_args(exchange):
    (plan, shape, dtype, n_sems), operand = exchange
    sems = [pltpu.SemaphoreType.DMA((n_sems,)), pltpu.SemaphoreType.DMA((n_sems,))]
    return [_ANY], [_ANY], [_sds(shape, dtype)], sems, [operand]


def _gather_weights(flat, row0, nrows, nc):
    def body(x_ref, out_ref, send_sems, recv_sems):
        start, forward, finish = _gather_plan(x_ref, out_ref, send_sems, recv_sems, row0, nrows, nc)
        start()
        forward()
        finish()

    return pl.pallas_call(
        body, name="gather_weights", in_specs=[_ANY], out_specs=_ANY,
        out_shape=_sds((N_SHARD, ROWS_FLAT, D_MODEL), BF16), scratch_shapes=_gather_sems(nc),
    )(flat)


def _add_my_half(grads, recv):
    tr = 400
    nb = HALF_LAYER // tr
    core = lax.axis_index("c").astype(jnp.int32).reshape(1)

    def body(c_ref, g_ref, r_ref, o_ref, ob_ref):
        acc = g_ref[...] + r_ref[...]
        o_ref[...] = acc
        ob_ref[...] = _bf(acc)

    out = pl.BlockSpec((None, tr, D_MODEL), lambda j, i, c_ref: (j, i, 0))
    return pl.pallas_call(
        body, name="add_my_half",
        grid_spec=pltpu.PrefetchScalarGridSpec(
            num_scalar_prefetch=1, grid=(N_SHARD, nb),
            in_specs=[pl.BlockSpec((None, tr, D_MODEL), lambda j, i, c_ref: (j, c_ref[0] * nb + i, 0)), out],
            out_specs=[out, out]),
        out_shape=[_sds((N_SHARD, HALF_LAYER, D_MODEL)), _sds((N_SHARD, HALF_LAYER, D_MODEL), BF16)],
        compiler_params=_params(("arbitrary", "arbitrary")),
    )(core, grads, recv)


def _sum_scattered(part, recv):
    tr = 400
    chip = (2 * lax.axis_index("x") + lax.axis_index("y")).astype(jnp.int32).reshape(1)

    def body(c_ref, p_ref, r_ref, o_ref):
        acc = p_ref[...]
        for k in range(3):
            acc = acc + r_ref[k].astype(F32)
        o_ref[...] = acc

    return pl.pallas_call(
        body, name="sum_scattered",
        grid_spec=pltpu.PrefetchScalarGridSpec(
            num_scalar_prefetch=1, grid=(HALF_LAYER // tr,),
            in_specs=[pl.BlockSpec((None, tr, D_MODEL), lambda i, c_ref: (c_ref[0], i, 0)),
                      pl.BlockSpec((3, tr, D_MODEL), lambda i, c_ref: (0, i, 0))],
            out_specs=pl.BlockSpec((tr, D_MODEL), lambda i, c_ref: (i, 0))),
        out_shape=_sds((HALF_LAYER, D_MODEL)), compiler_params=_params(("arbitrary",)),
    )(chip, part, recv)


def _swap_reduced(mine):
    ch, nc = CHUNK_ROWS, RS_CHUNKS

    def body(r_ref, out_ref, send_sems, recv_sems):
        x, y, c = _place()
        copies = [pltpu.make_async_remote_copy(
            src_ref=r_ref.at[l, pl.ds(r * ch, ch), :], dst_ref=out_ref.at[l, pl.ds(r * ch, ch), :],
            send_sem=send_sems.at[l * nc + r], recv_sem=recv_sems.at[l * nc + r],
            device_id=(x, y, 1 - c), device_id_type=MESH) for l in range(2) for r in range(nc)]
        for cp in copies:
            cp.start()
        for cp in copies:
            cp.wait()

    return pl.pallas_call(
        body, name="swap_reduced", in_specs=[_ANY], out_specs=_ANY,
        out_shape=_sds((2, HALF_LAYER, D_MODEL)),
        scratch_shapes=[pltpu.SemaphoreType.DMA((2 * nc,)), pltpu.SemaphoreType.DMA((2 * nc,))],
    )(mine)


def _allreduce_small(vec):
    def body(v_ref, out_ref, buf, send_sems, recv_sems):
        x, y, c = _place()
        me = 4 * x + 2 * y + c
        buf[me] = v_ref[...]
        peers = [me ^ k for k in range(1, N_DEV)]
        sends = [pltpu.make_async_remote_copy(
            src_ref=v_ref, dst_ref=buf.at[me], send_sem=send_sems.at[k], recv_sem=recv_sems.at[k],
            device_id=(p >> 2, (p >> 1) & 1, p & 1), device_id_type=MESH) for k, p in enumerate(peers)]
        for cp in sends:
            cp.start()
        for k, p in enumerate(peers):
            pltpu.make_async_remote_copy(
                src_ref=v_ref, dst_ref=buf.at[p], send_sem=send_sems.at[k], recv_sem=recv_sems.at[k],
                device_id=(p >> 2, (p >> 1) & 1, p & 1), device_id_type=MESH).wait_recv()
        for cp in sends:
            cp.wait_send()
        acc = buf[0]
        for d in range(1, N_DEV):
            acc = acc + buf[d]
        out_ref[...] = acc

    vm = pl.BlockSpec(memory_space=pltpu.VMEM)
    return pl.pallas_call(
        body, name="allreduce_small", in_specs=[vm], out_specs=vm, out_shape=_sds((SMALL_ROWS, 128)),
        scratch_shapes=[pltpu.VMEM((N_DEV, SMALL_ROWS, 128), F32),
                        pltpu.SemaphoreType.DMA((N_DEV - 1,)), pltpu.SemaphoreType.DMA((N_DEV - 1,))],
    )(vec)


def _flatten_shard(w_in, w_out, w_pg, w_pp):
    return jnp.concatenate([w_in.reshape(-1, D_MODEL), w_out.reshape(-1, D_MODEL), w_pg.reshape(-1, D_MODEL),
                            w_pp.reshape(-1, D_MODEL)], axis=0)


def _unflatten_layers(flats):
    a, b, c = D_MODEL, D_MODEL + D_MODEL // N_SHARD, D_MODEL + 2 * (D_MODEL // N_SHARD)
    q = D_MODEL // N_SHARD
    return (jnp.stack([f[:a] for f in flats]), jnp.stack([f[a:b] for f in flats]),
            jnp.stack([f[b:c] for f in flats]), jnp.stack([f[c:].reshape(D_PLE, q) for f in flats]))


def _full_w_pp(gathered):
    c = ROWS_W_IN + ROWS_W_OUT + ROWS_W_PG
    q = D_MODEL // N_SHARD
    rpp = ROWS_W_PP // 2
    return [gathered[:, c + l * rpp:c + (l + 1) * rpp, :].reshape(N_SHARD, D_PLE, q).transpose(1, 0, 2)
            .reshape(D_PLE, D_MODEL) for l in range(2)]


def _layer_grads(dw_in, dw_out, dw_pg, dw_pp):
    q = D_MODEL // N_SHARD
    rpp = ROWS_W_PP // 2
    rest = jnp.concatenate([dw_out.reshape(N_SHARD, q, D_MODEL), dw_pg.reshape(N_SHARD, q, D_MODEL),
                            dw_pp.reshape(D_PLE, N_SHARD, q).transpose(1, 0, 2).reshape(N_SHARD, rpp, D_MODEL)], axis=1)
    return lax.dynamic_update_slice(dw_in, rest, (0, D_MODEL, 0))


def _lower_bounds(lb_logits):
    sm = jax.nn.softmax(lb_logits.astype(F32), axis=0)
    return jnp.cumsum(sm, axis=0) - sm[0:1]


def kernel(x, p, norm_mix, w_in, a_out_norm, b_out_norm, w_out, lb_logits, ple_gate_norm, w_ple_gate, w_ple_proj, ple_post_norm, final_norm, loss_target, m_norm_mix, m_w_in, m_a_out_norm, m_b_out_norm, m_w_out, m_lb_logits, m_ple_gate_norm, m_w_ple_gate, m_w_ple_proj, m_ple_post_norm, m_final_norm, v_norm_mix, v_w_in, v_a_out_norm, v_b_out_norm, v_w_out, v_lb_logits, v_ple_gate_norm, v_w_ple_gate, v_w_ple_proj, v_ple_post_norm, v_final_norm):
    t = x.shape[1]
    h0 = x.reshape(t, D_MODEL)
    target = loss_target.reshape(t, D_MODEL)
    pl_in = p.reshape(2, t, D_PLE)

    w_flat_bf = _flatten_shard(_bf(w_in), _bf(w_out), _bf(w_ple_gate), _bf(w_ple_proj))
    chip = 2 * lax.axis_index("x") + lax.axis_index("y")
    gathered = lax.dynamic_update_slice(_gather_weights(w_flat_bf, 0, D_MODEL, 2), w_flat_bf[None], (chip, 0, 0))
    lbs, lbs_vjp = jax.vjp(_lower_bounds, lb_logits)

    saved = []
    h = h0
    for l in range(2):
        g_mix = norm_mix[l].reshape(1, D_MODEL)
        lb = lbs[l].reshape(1, GROUP)
        ga = a_out_norm[l].reshape(1, GROUP)
        gb = b_out_norm[l].reshape(1, GROUP)
        proj, qkv, bgate = _inproj(h, g_mix, gathered, l)
        ya, states = _hgrn_fwd(proj, lb, ga)
        if l == 0:
            ob, yb, cars, gathered = _sb_fwd(qkv, bgate, gb, (w_flat_bf, gathered, D_MODEL, ROWS_FLAT - D_MODEL, 4))
            w_pps = _full_w_pp(gathered)
        else:
            ob, yb, cars = _sb_fwd(qkv, bgate, gb)
        h1 = _outproj(h, ya, yb, gathered, l)
        g_post = ple_post_norm[l].reshape(1, D_MODEL)
        g_gate = ple_gate_norm[l].reshape(1, D_MODEL)
        saved.append((h, proj, qkv, bgate, states, ya, yb, ob, cars, h1))
        if l == 0:
            h = _ple_fwd(h1, pl_in[l], w_pps[l], gathered, l, g_post, g_gate)
        else:
            dh, d_final, loss_part = _ple_fwd_final(h1, pl_in[l], w_pps[l], gathered, l, g_post, g_gate,
                                                    final_norm.reshape(1, D_MODEL), target)

    g_layer, chip_sum, scattered = [None] * 2, [None] * 2, [None] * 2
    d_mix, d_a, d_b, d_lb, d_gate, d_post = [None] * 2, [None] * 2, [None] * 2, [None] * 2, [None] * 2, [None] * 2
    for l in (1, 0):
        h_in, proj, qkv, bgate, states, ya, yb, ob, cars, h1 = saved[l]
        g_mix = norm_mix[l].reshape(1, D_MODEL)
        lb = lbs[l].reshape(1, GROUP)
        ga = a_out_norm[l].reshape(1, GROUP)
        gb = b_out_norm[l].reshape(1, GROUP)
        g_post = ple_post_norm[l].reshape(1, D_MODEL)
        g_gate = ple_gate_norm[l].reshape(1, D_MODEL)
        if l == 1:
            dh1, dw_pg, dw_pp, d_gate[l], d_post[l] = _ple_bwd(dh, h1, pl_in[l], w_pps[l], gathered, l, g_post, g_gate)
            dya, dyb, dw_out = _outproj_bwd(dh1, ya, yb, gathered, l)
            dbq, dbk, dbv, dbg, d_b[l] = _sb_bwd(qkv, bgate, ob, cars, dyb, gb)
        else:
            dh1, dw_pg, dw_pp, d_gate[l], d_post[l], from_sibling = _ple_bwd(
                dh, h1, pl_in[l], w_pps[l], gathered, l, g_post, g_gate, (SWAP, g_layer[1]))
            chip_sum[1], chip_sum_bf = _add_my_half(g_layer[1], from_sibling)
            dya, dyb, dw_out = _outproj_bwd(dh1, ya, yb, gathered, l)
            dbq, dbk, dbv, dbg, d_b[l], scattered[1] = _sb_bwd(qkv, bgate, ob, cars, dyb, gb, (SCATTER, chip_sum_bf))
        da, d_lb[l], d_a[l] = _hgrn_bwd(proj, lb, ga, states, dya)
        db = jnp.stack([dbq, dbk * LN2, dbv, dbg]).astype(BF16)
        g_layer[l] = _layer_grads(_inproj_bwd_dw(h_in, g_mix, da, db), dw_out, dw_pg, dw_pp)
        if l == 1:
            dh, d_mix[l] = _inproj_bwd_dx(dh1, h_in, g_mix, gathered, l, da, db)
        else:
            dh, d_mix[l], from_sibling = _inproj_bwd_dx(dh1, h_in, g_mix, gathered, l, da, db, (SWAP, g_layer[0]))
    grad_x = dh.reshape(x.shape)

    chip_sum[0], chip_sum_bf = _add_my_half(g_layer[0], from_sibling)
    scattered[0] = _exchange_call(SCATTER, chip_sum_bf)
    mine = jnp.stack([_sum_scattered(chip_sum[l], scattered[l]) for l in range(2)])
    other = _swap_reduced(mine)
    south = lax.axis_index("c") == 0
    g_w_in, g_w_out, g_w_pg, g_w_pp = _unflatten_layers(
        [jnp.concatenate([jnp.where(south, mine[l], other[l]), jnp.where(south, other[l], mine[l])]) for l in range(2)])

    small = jnp.concatenate([
        jnp.concatenate(d_mix).reshape(-1, 128), jnp.concatenate(d_a).reshape(-1, 128),
        jnp.concatenate(d_b).reshape(-1, 128), jnp.concatenate(d_lb).reshape(-1, 128),
        jnp.concatenate(d_gate).reshape(-1, 128), jnp.concatenate(d_post).reshape(-1, 128),
        d_final.reshape(-1, 128), jnp.broadcast_to(loss_part, (8, 128))], axis=0)
    small = _allreduce_small(small)
    loss = small[80, 0]
    g_norm_mix = small[0:16].reshape(2, D_MODEL)
    g_a = small[16:24].reshape(2, GROUP)
    g_b = small[24:32].reshape(2, GROUP)
    (g_lb,) = lbs_vjp(small[32:40].reshape(2, GROUP))
    g_gate = small[40:56].reshape(2, D_MODEL)
    g_post = small[56:72].reshape(2, D_MODEL)
    g_final = small[72:80].reshape(D_MODEL)

    def adam_matrix(w, g, m, v):
        d, nm, nv = _adamw(w.reshape(-1, D_MODEL), g.reshape(-1, D_MODEL), m.reshape(-1, D_MODEL), v.reshape(-1, D_MODEL))
        return d.reshape(w.shape), nm.reshape(w.shape), nv.reshape(w.shape)

    d_w_in, nm_w_in, nv_w_in = adam_matrix(w_in, g_w_in, m_w_in, v_w_in)
    d_w_out, nm_w_out, nv_w_out = adam_matrix(w_out, g_w_out, m_w_out, v_w_out)
    d_w_pg, nm_w_pg, nv_w_pg = adam_matrix(w_ple_gate, g_w_pg, m_w_ple_gate, v_w_ple_gate)
    d_w_pp, nm_w_pp, nv_w_pp = adam_matrix(w_ple_proj, g_w_pp, m_w_ple_proj, v_w_ple_proj)

    small_w = [norm_mix, a_out_norm, b_out_norm, lb_logits, ple_gate_norm, ple_post_norm, final_norm]
    small_g = [g_norm_mix, g_a, g_b, g_lb, g_gate, g_post, g_final]
    small_m = [m_norm_mix, m_a_out_norm, m_b_out_norm, m_lb_logits, m_ple_gate_norm, m_ple_post_norm, m_final_norm]
    small_v = [v_norm_mix, v_a_out_norm, v_b_out_norm, v_lb_logits, v_ple_gate_norm, v_ple_post_norm, v_final_norm]
    pack = lambda arrs: jnp.concatenate([a.reshape(-1, 128) for a in arrs], axis=0)
    ds, nms, nvs = _adamw(pack(small_w), pack(small_g), pack(small_m), pack(small_v))

    def unpack(packed):
        out, r = [], 0
        for a in small_w:
            n = a.size // 128
            out.append(packed[r:r + n].reshape(a.shape))
            r += n
        return out

    d_s, nm_s, nv_s = unpack(ds), unpack(nms), unpack(nvs)

    def ordered(s, big):
        return [s[0], big[0], s[1], s[2], big[1], s[3], s[4], big[2], big[3], s[5], s[6]]

    grads = ordered(small_g, [g_w_in, g_w_out, g_w_pg, g_w_pp])
    deltas = ordered(d_s, [d_w_in, d_w_out, d_w_pg, d_w_pp])
    new_m = ordered(nm_s, [nm_w_in, nm_w_out, nm_w_pg, nm_w_pp])
    new_v = ordered(nv_s, [nv_w_in, nv_w_out, nv_w_pg, nv_w_pp])
    return (loss, grad_x, *grads, *deltas, *new_m, *new_v)
```

```python
import functools
import math

import numpy as np
import jax
import jax.numpy as jnp
from jax import lax
from jax.experimental import pallas as pl
from jax.experimental.pallas import tpu as pltpu

F32 = jnp.float32
BF16 = jnp.bfloat16
MESH = pl.DeviceIdType.MESH

D_MODEL = 1024
D_PLE = 256
D_IN = 4096
A_HEADS, A_D = 4, 128
B_HEADS, B_D = 8, 64
GROUP = 512
EPS = 1e-6
N_SHARD = 4
N_DEV = 8

HG_CHUNK = 128
HG_LEVELS = 7
SB_TQ = 1024
SB_TK = 128
SB_GROUP_FWD, SB_GROUP_BWD = (2, 2), (8, 2)
LOG2E = 1.4426950408889634
LN2 = 0.6931471805599453

ADAM_LR, ADAM_B1, ADAM_B2, ADAM_EPS, ADAM_WD, ADAM_STEP = 0.001, 0.9, 0.999, 1e-08, 0.01, 10

VMEM_LIMIT = 48 * 1024 * 1024
VMEM_LIMIT_BIG = 58 * 1024 * 1024

ROWS_W_IN = 2 * D_MODEL
ROWS_W_OUT = 2 * (D_MODEL // N_SHARD)
ROWS_W_PG = 2 * (D_MODEL // N_SHARD)
ROWS_W_PP = 2 * (D_PLE * (D_MODEL // N_SHARD) // D_MODEL)
ROWS_FLAT = ROWS_W_IN + ROWS_W_OUT + ROWS_W_PG + ROWS_W_PP
HALF_FLAT = ROWS_FLAT // 2
N_CHUNK = 10
CHUNK_ROWS = HALF_FLAT // N_CHUNK

ROWS_LAYER = ROWS_FLAT // 2
HALF_LAYER = ROWS_LAYER // 2
RS_CHUNKS = HALF_LAYER // CHUNK_ROWS

SMALL_ROWS = 88


def _sds(shape, dtype=F32):
    return jax.ShapeDtypeStruct(shape, dtype)


def _params(sem=None, vmem_limit=VMEM_LIMIT):
    kw = dict(vmem_limit_bytes=vmem_limit)
    if sem is not None:
        kw["dimension_semantics"] = sem
    return pltpu.CompilerParams(**kw)


def _dot(a, b, precision=None):
    return lax.dot_general(a, b, (((1,), (0,)), ((), ())), preferred_element_type=F32, precision=precision)


def _dot_nt(a, b, precision=None):
    return lax.dot_general(a, b, (((1,), (1,)), ((), ())), preferred_element_type=F32, precision=precision)


def _dot_tn(a, b, precision=None):
    return lax.dot_general(a, b, (((0,), (0,)), ((), ())), preferred_element_type=F32, precision=precision)


def _bf(x):
    return x.astype(BF16)


def _split(x):
    hi = x.astype(BF16)
    lo = (x - hi.astype(F32)).astype(BF16)
    return hi, lo


def _rms(x):
    r = lax.rsqrt(jnp.mean(x * x, axis=-1, keepdims=True) + EPS)
    return x * r, r


def _rms_bwd(dxh, xh, r):
    return r * (dxh - xh * jnp.mean(dxh * xh, axis=-1, keepdims=True))


def _sigmoid(x):
    return 1.0 / (1.0 + jnp.exp(-x))


def _silu_grad(x, sig):
    return sig * (1.0 + x * (1.0 - sig))


def _row_tile(t, want):
    return min(t, want)


def _inproj(h, g, gathered, layer):
    t = h.shape[0]
    tm = _row_tile(t, 512)

    def body(h_ref, g_ref, w_ref, pa_ref, qkv_ref, bg_ref):
        xh, _ = _rms(h_ref[...])
        u = _bf(xh * g_ref[...])
        for j in range(8):
            acc = _dot(u, w_ref[j // 2, :, pl.ds((j % 2) * GROUP, GROUP)])
            if j < 4:
                pa_ref[:, pl.ds(j * GROUP, GROUP)] = acc
            elif j == 4:
                qkv_ref[:, pl.ds(0, GROUP)] = _bf(acc * (B_D ** -0.5 * LOG2E))
            elif j < 7:
                qkv_ref[:, pl.ds((j - 4) * GROUP, GROUP)] = _bf(acc)
            else:
                bg_ref[...] = acc

    return pl.pallas_call(
        body, name="inproj", grid=(t // tm,),
        in_specs=[pl.BlockSpec((tm, D_MODEL), lambda i: (i, 0)),
                  pl.BlockSpec((1, D_MODEL), lambda i: (0, 0)),
                  pl.BlockSpec((N_SHARD, D_MODEL, D_MODEL), lambda i: (0, layer, 0))],
        out_specs=[pl.BlockSpec((tm, 4 * GROUP), lambda i: (i, 0)), pl.BlockSpec((tm, 3 * GROUP), lambda i: (i, 0)),
                   pl.BlockSpec((tm, GROUP), lambda i: (i, 0))],
        out_shape=[_sds((t, 4 * GROUP)), _sds((t, 3 * GROUP), BF16), _sds((t, GROUP))],
        compiler_params=_params(("arbitrary",)),
    )(h, g, gathered)


def _rows_spec(first_row):
    q = D_MODEL // N_SHARD
    return pl.BlockSpec((N_SHARD, q, D_MODEL), lambda i: (0, first_row // q, 0))


def _outproj(h, ya, yb, gathered, layer):
    t = h.shape[0]
    tm = _row_tile(t, 512)

    def body(h_ref, ya_ref, yb_ref, w_ref, o_ref):
        o_ref[...] = (h_ref[...] + _dot(_bf(ya_ref[...]), w_ref[0:2].reshape(GROUP, D_MODEL))
                      + _dot(_bf(yb_ref[...]), w_ref[2:4].reshape(GROUP, D_MODEL)))

    return pl.pallas_call(
        body, name="outproj", grid=(t // tm,),
        in_specs=[pl.BlockSpec((tm, D_MODEL), lambda i: (i, 0)),
                  pl.BlockSpec((tm, GROUP), lambda i: (i, 0)),
                  pl.BlockSpec((tm, GROUP), lambda i: (i, 0)),
                  _rows_spec(ROWS_W_IN + layer * (D_MODEL // N_SHARD))],
        out_specs=pl.BlockSpec((tm, D_MODEL), lambda i: (i, 0)),
        out_shape=_sds((t, D_MODEL)), compiler_params=_params(("arbitrary",)),
    )(h, ya, yb, gathered)


def _ple_mix(x, p_ref, wpp_ref, wpg_ref, gp_ref, gg_ref):
    ph, _ = _rms(_dot(_bf(p_ref[...]), wpp_ref[...]))
    xh, _ = _rms(x)
    gate = _sigmoid(_dot(_bf(xh * gg_ref[...]), wpg_ref[...].reshape(D_MODEL, D_MODEL)))
    return x + gate * (ph * gp_ref[...])


def _ple_specs(tm, layer):
    return [pl.BlockSpec((tm, D_MODEL), lambda i: (i, 0)),
            pl.BlockSpec((tm, D_PLE), lambda i: (i, 0)),
            pl.BlockSpec((D_PLE, D_MODEL), lambda i: (0, 0)),
            _rows_spec(ROWS_W_IN + ROWS_W_OUT + layer * (D_MODEL // N_SHARD)),
            pl.BlockSpec((1, D_MODEL), lambda i: (0, 0)),
            pl.BlockSpec((1, D_MODEL), lambda i: (0, 0))]


def _ple_fwd(h, p, w_pp, gathered, layer, g_post, g_gate):
    t = h.shape[0]
    tm = _row_tile(t, 256)

    def body(h_ref, p_ref, wpp_ref, wpg_ref, gp_ref, gg_ref, o_ref):
        o_ref[...] = _ple_mix(h_ref[...], p_ref, wpp_ref, wpg_ref, gp_ref, gg_ref)

    return pl.pallas_call(
        body, name="ple_fwd", grid=(t // tm,), in_specs=_ple_specs(tm, layer),
        out_specs=pl.BlockSpec((tm, D_MODEL), lambda i: (i, 0)),
        out_shape=_sds((t, D_MODEL)), compiler_params=_params(("arbitrary",)),
    )(h, p, w_pp, gathered, g_post, g_gate)


def _ple_fwd_final(h, p, w_pp, gathered, layer, g_post, g_gate, g_final, target):
    t = h.shape[0]
    tm = _row_tile(t, 256)

    def body(h_ref, p_ref, wpp_ref, wpg_ref, gp_ref, gg_ref, gf_ref, t_ref, dh_ref, dg_ref, loss_ref):
        @pl.when(pl.program_id(0) == 0)
        def _():
            dg_ref[...] = jnp.zeros_like(dg_ref)
            loss_ref[...] = jnp.zeros_like(loss_ref)

        xh, r = _rms(_ple_mix(h_ref[...], p_ref, wpp_ref, wpg_ref, gp_ref, gg_ref))
        gf = gf_ref[...]
        err = xh * gf - t_ref[...]
        part = 0.5 * jnp.sum(jnp.mean(err * err, axis=-1, keepdims=True), axis=0, keepdims=True)
        loss_ref[...] += jnp.broadcast_to(part, loss_ref.shape)
        dy = err * (1.0 / D_MODEL)
        dg_ref[...] += jnp.sum(dy * xh, axis=0, keepdims=True)
        dh_ref[...] = _rms_bwd(dy * gf, xh, r)

    return pl.pallas_call(
        body, name="ple_fwd_final", grid=(t // tm,),
        in_specs=_ple_specs(tm, layer) + [pl.BlockSpec((1, D_MODEL), lambda i: (0, 0)),
                                          pl.BlockSpec((tm, D_MODEL), lambda i: (i, 0))],
        out_specs=[pl.BlockSpec((tm, D_MODEL), lambda i: (i, 0)),
                   pl.BlockSpec((1, D_MODEL), lambda i: (0, 0)),
                   pl.BlockSpec((1, 128), lambda i: (0, 0))],
        out_shape=[_sds((t, D_MODEL)), _sds((1, D_MODEL)), _sds((1, 128))],
        compiler_params=_params(("arbitrary",)),
    )(h, p, w_pp, gathered, g_post, g_gate, g_final, target)


def _ple_bwd(dh2, h, p, w_pp, gathered, layer, g_post, g_gate, exchange=None):
    t = h.shape[0]
    tm = _row_tile(t, 512)

    def body(d_ref, h_ref, p_ref, wpp_ref, wpg_ref, gp_ref, gg_ref, dh_ref, dwpg_ref, dwpp_ref, dgg_ref, dgp_ref):
        @pl.when(pl.program_id(0) == 0)
        def _():
            dwpg_ref[...] = jnp.zeros_like(dwpg_ref)
            dwpp_ref[...] = jnp.zeros_like(dwpp_ref)
            dgg_ref[...] = jnp.zeros_like(dgg_ref)
            dgp_ref[...] = jnp.zeros_like(dgp_ref)

        d = d_ref[...]
        x = h_ref[...]
        gp = gp_ref[...]
        gg = gg_ref[...]
        pb = _bf(p_ref[...])
        ph, rp = _rms(_dot(pb, wpp_ref[...]))
        pe = ph * gp
        xh, rx = _rms(x)
        un = _bf(xh * gg)
        wpg = wpg_ref[...].reshape(D_MODEL, D_MODEL)
        gate = _sigmoid(_dot(un, wpg))
        dgpre = _bf(d * pe * gate * (1.0 - gate))
        dun = _dot_nt(dgpre, wpg)
        dh_ref[...] = d + _rms_bwd(dun * gg, xh, rx)
        dgg_ref[...] += jnp.sum(dun * xh, axis=0, keepdims=True)
        dwpg_ref[...] += _dot_tn(un, dgpre)
        dpe = d * gate
        dgp_ref[...] += jnp.sum(dpe * ph, axis=0, keepdims=True)
        dwpp_ref[...] += _dot_tn(pb, _bf(_rms_bwd(dpe * gp, ph, rp)))

    in_specs = [pl.BlockSpec((tm, D_MODEL), lambda i: (i, 0)),
                pl.BlockSpec((tm, D_MODEL), lambda i: (i, 0)),
                pl.BlockSpec((tm, D_PLE), lambda i: (i, 0)),
                pl.BlockSpec((D_PLE, D_MODEL), lambda i: (0, 0)),
                _rows_spec(ROWS_W_IN + ROWS_W_OUT + layer * (D_MODEL // N_SHARD)),
                pl.BlockSpec((1, D_MODEL), lambda i: (0, 0)),
                pl.BlockSpec((1, D_MODEL), lambda i: (0, 0))]
    out_specs = [pl.BlockSpec((tm, D_MODEL), lambda i: (i, 0)),
                 pl.BlockSpec((D_MODEL, D_MODEL), lambda i: (0, 0)),
                 pl.BlockSpec((D_PLE, D_MODEL), lambda i: (0, 0)),
                 pl.BlockSpec((1, D_MODEL), lambda i: (0, 0)),
                 pl.BlockSpec((1, D_MODEL), lambda i: (0, 0))]
    out_shape = [_sds((t, D_MODEL)), _sds((D_MODEL, D_MODEL)), _sds((D_PLE, D_MODEL)),
                 _sds((1, D_MODEL)), _sds((1, D_MODEL))]
    operands = [dh2, h, p, w_pp, gathered, g_post, g_gate]
    scratch = []
    if exchange is not None:
        body = _with_exchange(body, 7, 5, exchange, lambda: pl.program_id(0), t // tm)
        xi, xo, xs, scratch, xop = _exchange_args(exchange)
        in_specs, out_specs, out_shape, operands = in_specs + xi, out_specs + xo, out_shape + xs, operands + xop
    return pl.pallas_call(
        body, name="ple_bwd" if exchange is None else "ple_bwd_exchange", grid=(t // tm,),
        in_specs=in_specs, out_specs=out_specs, out_shape=out_shape, scratch_shapes=scratch,
        compiler_params=_params(("arbitrary",)),
    )(*operands)


def _outproj_bwd(dh, ya, yb, gathered, layer):
    t = dh.shape[0]
    tm = _row_tile(t, 512)

    def body(d_ref, ya_ref, yb_ref, w_ref, dya_ref, dyb_ref, dw_ref):
        @pl.when(pl.program_id(0) == 0)
        def _():
            dw_ref[...] = jnp.zeros_like(dw_ref)

        d = _bf(d_ref[...])
        dya_ref[...] = _dot_nt(d, w_ref[0:2].reshape(GROUP, D_MODEL))
        dyb_ref[...] = _dot_nt(d, w_ref[2:4].reshape(GROUP, D_MODEL))
        dw_ref[pl.ds(0, GROUP), :] += _dot_tn(_bf(ya_ref[...]), d)
        dw_ref[pl.ds(GROUP, GROUP), :] += _dot_tn(_bf(yb_ref[...]), d)

    return pl.pallas_call(
        body, name="outproj_bwd", grid=(t // tm,),
        in_specs=[pl.BlockSpec((tm, D_MODEL), lambda i: (i, 0)),
                  pl.BlockSpec((tm, GROUP), lambda i: (i, 0)),
                  pl.BlockSpec((tm, GROUP), lambda i: (i, 0)),
                  _rows_spec(ROWS_W_IN + layer * (D_MODEL // N_SHARD))],
        out_specs=[pl.BlockSpec((tm, GROUP), lambda i: (i, 0)),
                   pl.BlockSpec((tm, GROUP), lambda i: (i, 0)),
                   pl.BlockSpec((D_MODEL, D_MODEL), lambda i: (0, 0))],
        out_shape=[_sds((t, GROUP)), _sds((t, GROUP)), _sds((D_MODEL, D_MODEL))],
        compiler_params=_params(("arbitrary",)),
    )(dh, ya, yb, gathered)


def _inproj_bwd_dx(dres, h, g, gathered, layer, da, db, exchange=None):
    t = h.shape[0]
    tm = _row_tile(t, 512)

    def body(dres_ref, h_ref, g_ref, w_ref, da_ref, db_ref, dh_ref, dg_ref):
        @pl.when(pl.program_id(0) == 0)
        def _():
            dg_ref[...] = jnp.zeros_like(dg_ref)

        du = jnp.zeros((tm, D_MODEL), F32)
        for i in range(8):
            part = da_ref[i] if i < 4 else db_ref[i - 4]
            du = du + _dot_nt(part, w_ref[i // 2, :, pl.ds((i % 2) * GROUP, GROUP)])
        xh, r = _rms(h_ref[...])
        dg_ref[...] += jnp.sum(du * xh, axis=0, keepdims=True)
        dh_ref[...] = dres_ref[...] + _rms_bwd(du * g_ref[...], xh, r)

    in_specs = [pl.BlockSpec((tm, D_MODEL), lambda i: (i, 0)),
                pl.BlockSpec((tm, D_MODEL), lambda i: (i, 0)),
                pl.BlockSpec((1, D_MODEL), lambda i: (0, 0)),
                pl.BlockSpec((N_SHARD, D_MODEL, D_MODEL), lambda i: (0, layer, 0)),
                pl.BlockSpec((4, tm, GROUP), lambda i: (0, i, 0)),
                pl.BlockSpec((4, tm, GROUP), lambda i: (0, i, 0))]
    out_specs = [pl.BlockSpec((tm, D_MODEL), lambda i: (i, 0)), pl.BlockSpec((1, D_MODEL), lambda i: (0, 0))]
    out_shape = [_sds((t, D_MODEL)), _sds((1, D_MODEL))]
    operands = [dres, h, g, gathered, da, db]
    scratch = []
    if exchange is not None:
        body = _with_exchange(body, 6, 2, exchange, lambda: pl.program_id(0), t // tm)
        xi, xo, xs, scratch, xop = _exchange_args(exchange)
        in_specs, out_specs, out_shape, operands = in_specs + xi, out_specs + xo, out_shape + xs, operands + xop
    return pl.pallas_call(
        body, name="inproj_bwd_dx" if exchange is None else "inproj_bwd_dx_exchange", grid=(t // tm,),
        in_specs=in_specs, out_specs=out_specs, out_shape=out_shape, scratch_shapes=scratch,
        compiler_params=_params(("arbitrary",)),
    )(*operands)


def _inproj_bwd_dw(h, g, da, db):
    t = h.shape[0]
    tm = _row_tile(t, 512)

    def body(h_ref, g_ref, da_ref, db_ref, dw_ref):
        @pl.when(pl.program_id(0) == 0)
        def _():
            dw_ref[...] = jnp.zeros_like(dw_ref)

        xh, _ = _rms(h_ref[...])
        u = _bf(xh * g_ref[...])
        for i in range(8):
            dw_ref[i // 2, :, pl.ds((i % 2) * GROUP, GROUP)] += _dot_tn(u, da_ref[i] if i < 4 else db_ref[i - 4])

    return pl.pallas_call(
        body, name="inproj_bwd_dw", grid=(t // tm,),
        in_specs=[pl.BlockSpec((tm, D_MODEL), lambda i: (i, 0)),
                  pl.BlockSpec((1, D_MODEL), lambda i: (0, 0)),
                  pl.BlockSpec((4, tm, GROUP), lambda i: (0, i, 0)),
                  pl.BlockSpec((4, tm, GROUP), lambda i: (0, i, 0))],
        out_specs=pl.BlockSpec((N_SHARD, D_MODEL, D_MODEL), lambda i: (0, 0, 0)),
        out_shape=_sds((N_SHARD, ROWS_LAYER, D_MODEL)), compiler_params=_params(("arbitrary",), VMEM_LIMIT_BIG),
    )(h, g, da, db)


def _hgrn_consts():
    c, nl = HG_CHUNK, HG_LEVELS
    t = np.arange(c)
    tril = np.tril(np.ones((c, c), np.float32))
    masks = np.zeros((nl + 1, c, c), np.float32)
    masks[0] = np.eye(c, dtype=np.float32)
    dmat = np.zeros(((nl + 2) * c, c), np.float32)
    dmat[0:c] = tril
    for l in range(nl):
        m = c >> (l + 1)
        blk = t // (2 * m)
        r = blk * 2 * m + m - 1
        upper = (t % (2 * m)) >= m
        masks[l + 1] = ((blk[:, None] == blk[None, :]) & upper[:, None] & (~upper)[None, :]).astype(np.float32)
        dmat[(l + 1) * c:(l + 2) * c] = tril[t] - tril[r]
    dmat[(nl + 1) * c:] = np.triu(np.ones((c, c), np.float32), k=1)
    return jnp.asarray(masks), jnp.asarray(dmat, BF16)


HG_HEADS = 4


def _hgrn_pre(aq, af, lb):
    sq = _sigmoid(aq)
    sneg = _sigmoid(-af)
    kk = (1.0 - lb) * sneg
    return sq, aq * sq, sneg, kk, jnp.log1p(-kk)


def _hgrn_x(logf, dmat_ref):
    dm = dmat_ref[pl.ds(0, (HG_LEVELS + 1) * HG_CHUNK), :]
    lhi, llo = _split(logf)
    return _dot(dm, lhi) + _dot(dm, llo)


def _hgrn_level(x_all, l, q, kk):
    c = HG_CHUNK
    x = x_all[(l + 1) * c:(l + 2) * c]
    qf = jnp.exp(jnp.minimum(x, 0.0))
    kf = jnp.exp(-jnp.maximum(x, 0.0))
    return qf, kf, _bf(q * qf), _bf(kk * kf)


def _hgrn_scores(xs, qs, kks, mask_ref):
    ps = [mask_ref[0] * _dot_nt(_bf(q), _bf(kk)) for q, kk in zip(qs, kks)]
    for l in range(HG_LEVELS):
        for i, (x_all, q, kk) in enumerate(zip(xs, qs, kks)):
            _, _, ql, kl = _hgrn_level(x_all, l, q, kk)
            ps[i] = ps[i] + mask_ref[l + 1] * _dot_nt(ql, kl)
    return ps


def _hgrn_specs(n_chunks, rev):
    c, w = HG_CHUNK, HG_HEADS * A_D
    cidx = (lambda n: n_chunks - 1 - n) if rev else (lambda n: n)
    col = lambda g: pl.BlockSpec((c, w), lambda h, n: (cidx(n), g * (A_HEADS // HG_HEADS) + h))
    vec = pl.BlockSpec((1, w), lambda h, n: (0, h))
    mask = pl.BlockSpec((HG_LEVELS + 1, c, c), lambda h, n: (0, 0, 0))
    dmat = pl.BlockSpec(((HG_LEVELS + 2) * c, c), lambda h, n: (0, 0))
    state = pl.BlockSpec((HG_HEADS, None, A_D, A_D), lambda h, n: (h, cidx(n), 0, 0))
    return cidx, col, vec, mask, dmat, state


def _lanes(i):
    return pl.ds(i * A_D, A_D)


def _hgrn_fwd(proj, lb, gain):
    t = proj.shape[0]
    c = HG_CHUNK
    nch = t // c
    masks, dmat = _hgrn_consts()
    cidx, col, vec, mask_spec, dmat_spec, state_spec = _hgrn_specs(nch, False)
    heads = range(HG_HEADS)

    def body(aq_ref, af_ref, ai_ref, ag_ref, lb_ref, gain_ref, mask_ref, dmat_ref, y_ref, st_ref, s_scr):
        @pl.when(pl.program_id(1) == 0)
        def _():
            s_scr[...] = jnp.zeros_like(s_scr)

        pre = [_hgrn_pre(aq_ref[:, _lanes(i)], af_ref[:, _lanes(i)], lb_ref[:, _lanes(i)]) for i in heads]
        qs, kks = [p[1] for p in pre], [p[3] for p in pre]
        xs = [_hgrn_x(p[4], dmat_ref) for p in pre]
        bs = [x[0:c] for x in xs]
        b_lasts = [jnp.sum(p[4], axis=0, keepdims=True) for p in pre]
        ps = _hgrn_scores(xs, qs, kks, mask_ref)
        ss = [s_scr[i] for i in heads]
        vbs = [_bf(ai_ref[:, _lanes(i)]) for i in heads]
        os_ = [_dot(_bf(ps[i]), vbs[i]) + _dot_nt(_bf(qs[i] * jnp.exp(bs[i])), _bf(ss[i])) for i in heads]
        for i in heads:
            st_ref[i] = ss[i]
            s_scr[i] = ss[i] * jnp.exp(b_lasts[i]) + _dot_tn(vbs[i], _bf(kks[i] * jnp.exp(b_lasts[i] - bs[i])))
            oh, _ = _rms(os_[i])
            ag = ag_ref[:, _lanes(i)]
            y_ref[:, _lanes(i)] = oh * gain_ref[:, _lanes(i)] * (ag * _sigmoid(ag))

    return pl.pallas_call(
        body, name="hgrn_fwd", grid=(A_HEADS // HG_HEADS, nch),
        in_specs=[col(0), col(1), col(2), col(3), vec, vec, mask_spec, dmat_spec],
        out_specs=[pl.BlockSpec((c, HG_HEADS * A_D), lambda h, n: (n, h)), state_spec],
        out_shape=[_sds((t, GROUP)), _sds((A_HEADS, nch, A_D, A_D))],
        scratch_shapes=[pltpu.VMEM((HG_HEADS, A_D, A_D), F32)],
        compiler_params=_params(("arbitrary", "arbitrary")),
    )(proj, proj, proj, proj, lb, gain, masks, dmat)


def _hgrn_bwd(proj, lb, gain, states, dya):
    t = proj.shape[0]
    c, nl = HG_CHUNK, HG_LEVELS
    nch = t // c
    masks, dmat = _hgrn_consts()
    cidx, col, vec, mask_spec, dmat_spec, state_spec = _hgrn_specs(nch, True)
    heads = range(HG_HEADS)

    def body(aq_ref, af_ref, ai_ref, ag_ref, lb_ref, gain_ref, mask_ref, dmat_ref, st_ref, dy_ref,
             da_ref, dlb_ref, dgain_ref, ds_scr, z_scr):
        @pl.when(pl.program_id(1) == 0)
        def _():
            ds_scr[...] = jnp.zeros_like(ds_scr)
            dlb_ref[...] = jnp.zeros_like(dlb_ref)
            dgain_ref[...] = jnp.zeros_like(dgain_ref)

        aqs = [aq_ref[:, _lanes(i)] for i in heads]
        lbs = [lb_ref[:, _lanes(i)] for i in heads]
        pre = [_hgrn_pre(aqs[i], af_ref[:, _lanes(i)], lbs[i]) for i in heads]
        sqs, qs, snegs, kks = ([p[j] for p in pre] for j in range(4))
        xs = [_hgrn_x(p[4], dmat_ref) for p in pre]
        bs = [x[0:c] for x in xs]
        b_lasts = [jnp.sum(p[4], axis=0, keepdims=True) for p in pre]
        ebs = [jnp.exp(b) for b in bs]
        ebls = [jnp.exp(bl - b) for bl, b in zip(b_lasts, bs)]
        ebl_rows = [jnp.exp(bl) for bl in b_lasts]
        qes = [_bf(q * eb) for q, eb in zip(qs, ebs)]
        kes = [_bf(kk * ebl) for kk, ebl in zip(kks, ebls)]
        vbs = [_bf(ai_ref[:, _lanes(i)]) for i in heads]
        ss = [st_ref[i] for i in heads]
        sbs = [_bf(s) for s in ss]
        dss = [ds_scr[i] for i in heads]
        dsbs = [_bf(ds) for ds in dss]

        pbs = [_bf(p) for p in _hgrn_scores(xs, qs, kks, mask_ref)]
        os_ = [_dot(pbs[i], vbs[i]) + _dot_nt(qes[i], sbs[i]) for i in heads]

        dos = []
        for i in heads:
            ag, gain, dy = ag_ref[:, _lanes(i)], gain_ref[:, _lanes(i)], dy_ref[:, _lanes(i)]
            oh, r = _rms(os_[i])
            sg_sig = _sigmoid(ag)
            sg = ag * sg_sig
            da_ref[3, :, _lanes(i)] = _bf(dy * oh * gain * _silu_grad(ag, sg_sig))
            dgain_ref[:, _lanes(i)] += jnp.sum(dy * oh * sg, axis=0, keepdims=True)
            dos.append(_bf(_rms_bwd(dy * gain * sg, oh, r)))

        dps = [_dot_nt(dos[i], vbs[i]) for i in heads]
        for i in heads:
            da_ref[2, :, _lanes(i)] = _bf(_dot_tn(pbs[i], dos[i]) + _dot_nt(kes[i], dsbs[i]))
        dq_ss = [ebs[i] * _dot(dos[i], sbs[i]) for i in heads]
        dk_ss = [ebls[i] * _dot(vbs[i], dsbs[i]) for i in heads]
        dqs, dks = [], []
        for i in heads:
            dpd = jnp.sum(mask_ref[0] * dps[i], axis=1, keepdims=True)
            z_scr[i, pl.ds(0, c), :] = qs[i] * dq_ss[i]
            z_scr[i, pl.ds((nl + 1) * c, c), :] = kks[i] * dk_ss[i]
            dqs.append(dq_ss[i] + dpd * kks[i])
            dks.append(dk_ss[i] + dpd * qs[i])
        for l in range(nl):
            for i in heads:
                qf, kf, ql, kl = _hgrn_level(xs[i], l, qs[i], kks[i])
                dpl = _bf(mask_ref[l + 1] * dps[i])
                dq_l = qf * _dot(dpl, kl)
                dk_l = kf * _dot_tn(dpl, ql)
                z_scr[i, pl.ds((l + 1) * c, c), :] = qs[i] * dq_l - kks[i] * dk_l
                dqs[i] = dqs[i] + dq_l
                dks[i] = dks[i] + dk_l

        zsplits = [_split(z_scr[i]) for i in heads]
        dlogfs = [_dot_tn(dmat_ref[...], zhi) + _dot_tn(dmat_ref[...], zlo) for zhi, zlo in zsplits]
        ds_new = [_dot_tn(dos[i], qes[i]) for i in heads]
        for i in heads:
            dlogf = dlogfs[i] + ebl_rows[i] * jnp.sum(dss[i] * ss[i], axis=0, keepdims=True)
            dkk = dks[i] - dlogf / (1.0 - kks[i])
            da_ref[1, :, _lanes(i)] = _bf(dkk * (1.0 - lbs[i]) * (-(snegs[i] * (1.0 - snegs[i]))))
            dlb_ref[:, _lanes(i)] += jnp.sum(dkk * (-snegs[i]), axis=0, keepdims=True)
            da_ref[0, :, _lanes(i)] = _bf(dqs[i] * _silu_grad(aqs[i], sqs[i]))
            ds_scr[i] = dss[i] * ebl_rows[i] + ds_new[i]

    w = HG_HEADS * A_D
    return pl.pallas_call(
        body, name="hgrn_bwd", grid=(A_HEADS // HG_HEADS, nch),
        in_specs=[col(0), col(1), col(2), col(3), vec, vec, mask_spec, dmat_spec, state_spec,
                  pl.BlockSpec((c, w), lambda h, n: (cidx(n), h))],
        out_specs=[pl.BlockSpec((4, c, w), lambda h, n: (0, cidx(n), h)), vec, vec],
        out_shape=[_sds((4, t, GROUP), BF16)] + [_sds((1, GROUP))] * 2,
        scratch_shapes=[pltpu.VMEM((HG_HEADS, A_D, A_D), F32), pltpu.VMEM((HG_HEADS, (nl + 2) * c, A_D), F32)],
        compiler_params=_params(("arbitrary", "arbitrary")),
    )(proj, proj, proj, proj, lb, gain, masks, dmat, states, dya)


def _sb_consts():
    j = np.arange(SB_TK)
    strict = (j[:, None] > j[None, :]).astype(np.float32)
    lower = (j[:, None] < j[None, :]).astype(np.float32)
    return jnp.asarray(strict, BF16), jnp.asarray(lower, BF16)


def _lane(x, k):
    return jnp.broadcast_to(x[:, k:k + 1], x.shape)


def _key_lane(kb):
    return lax.broadcasted_iota(jnp.int32, (1, SB_TK), 1) == kb


def _causal(x, masked):
    if not masked:
        return x
    n = (SB_TK, SB_TK)
    top = jnp.where(lax.broadcasted_iota(jnp.int32, n, 1) < lax.broadcasted_iota(jnp.int32, n, 0), x[:SB_TK], 0.0)
    return top if x.shape[0] == SB_TK else jnp.concatenate([top, x[SB_TK:]], axis=0)


def _sb_softplus(z, masked):
    logsig = jnp.minimum(z, 0.0) - jnp.log2(1.0 + jnp.exp2(-jnp.abs(z)))
    return _causal(z - logsig, masked), logsig


def _sb_sweep(qi, group_fn, state, groups, ascending):
    nd = SB_TQ // SB_TK

    def tile(kb, r0, masked):
        return (pl.multiple_of(kb * SB_TK, SB_TK), r0, masked, kb)

    def run(tiles, st, group):
        for i in range(0, len(tiles), group):
            st = group_fn(tiles[i:i + group], st)
        return st

    diag = [tile(qi * nd + d, d * SB_TK, True) for d in range(nd)]
    if ascending:
        state = lax.fori_loop(
            0, qi, lambda j, st: run([tile(j * nd + g, 0, False) for g in range(nd)], st, groups[1]), state)
        return run(diag, state, groups[0])
    state = run(diag[::-1], state, groups[0])
    return lax.fori_loop(
        0, qi, lambda j, st: run([tile((qi - j) * nd - 1 - g, 0, False) for g in range(nd)], st, groups[1]), state)


def _set_rows(r0, full, new):
    return new if r0 == 0 else jnp.concatenate([full[:r0], new], axis=0)


def _sb_specs(t, tq):
    col = lambda g: pl.BlockSpec((tq, 2 * B_D), lambda p, i, h: (i, g * (GROUP // (2 * B_D)) + p))
    full = lambda g: pl.BlockSpec((t, 2 * B_D), lambda p, i, h: (0, g * (GROUP // (2 * B_D)) + p))
    vec = pl.BlockSpec((1, 2 * B_D), lambda p, i, h: (0, p))
    mat = pl.BlockSpec((SB_TK, SB_TK), lambda p, i, h: (0, 0))
    car = pl.BlockSpec((None, tq, SB_TK), lambda p, i, h: (2 * p + h, i, 0))
    return col, full, vec, mat, car


def _head_lanes(h):
    return (lax.broadcasted_iota(jnp.int32, (1, 2 * B_D), 1) >= B_D) == (h == 1)


def _put(ref, h, val):
    @pl.when(h == 0)
    def _():
        ref[...] = val

    @pl.when(h == 1)
    def _():
        ref[...] += val


def _sb_fwd(qkv, bgate, gain, gather=None):
    t = qkv.shape[0]
    tq = SB_TQ
    strict, _ = _sb_consts()
    n_steps = (B_HEADS // 2, t // tq, 2)

    def body(q_ref, k_ref, v_ref, bg_ref, gain_ref, m_ref, *rest):
        if gather is None:
            o_ref, y_ref, car_ref = rest
        else:
            flat_ref, _, o_ref, y_ref, car_ref, gathered_ref, send_sems, recv_sems = rest
            start, forward, finish = _gather_plan(flat_ref, gathered_ref, send_sems, recv_sems, *gather[2:])
            step = (pl.program_id(0) * n_steps[1] + pl.program_id(1)) * n_steps[2] + pl.program_id(2)
            pl.when(step == 0)(start)
            pl.when(step == 2 * n_steps[1] * n_steps[2])(forward)
            pl.when(step == n_steps[0] * n_steps[1] * n_steps[2] - 1)(finish)
        h = pl.program_id(2)
        lanes = _head_lanes(h)
        qb = jnp.where(lanes, q_ref[...], jnp.zeros_like(q_ref))
        cmat = m_ref[...]

        car_ref[...] = jnp.zeros_like(car_ref)

        def group(tiles, state):
            carry, acc = state
            kv = [(k_ref[pl.ds(off, SB_TK), :], v_ref[pl.ds(off, SB_TK), :]) for off, _, _, _ in tiles]
            zs = [_dot_nt(qb[r0:], kb) for (_, r0, _, _), (kb, _) in zip(tiles, kv)]
            sps = [_sb_softplus(z, masked) for z, (_, _, masked, _) in zip(zs, tiles)]
            css = [_dot(_bf(sp), cmat) for sp, _ in sps]
            ws = []
            for (sp, logsig), cs, (_, r0, masked, kb) in zip(sps, css, tiles):
                ws.append(_bf(_causal(jnp.exp2(logsig - cs - carry[r0:]), masked)))
                car_ref[r0:, :] = jnp.where(_key_lane(kb), carry[r0:], car_ref[r0:, :])
                carry = _set_rows(r0, carry, carry[r0:] + _lane(cs + sp, 0))
            for w, (_, vb), (_, r0, _, _) in zip(ws, kv, tiles):
                acc = _set_rows(r0, acc, acc[r0:] + _dot(w, vb))
            return carry, acc

        _, acc = _sb_sweep(pl.program_id(1), group, (jnp.zeros((tq, SB_TK), F32), jnp.zeros((tq, 2 * B_D), F32)),
                           SB_GROUP_FWD, False)
        o = jnp.where(lanes, acc, 0.0)
        oh = o * lax.rsqrt(jnp.sum(o * o, axis=-1, keepdims=True) * (1.0 / B_D) + EPS)
        bg = bg_ref[...]
        _put(o_ref, h, o)
        _put(y_ref, h, oh * gain_ref[...] * (bg * _sigmoid(bg)))

    col, full, vec, mat, car = _sb_specs(t, tq)
    out = pl.BlockSpec((tq, 2 * B_D), lambda p, i, h: (i, p))
    in_specs = [col(0), full(1), full(2), col(0), vec, mat]
    out_specs = [out, out, car]
    out_shape = [_sds((t, GROUP)), _sds((t, GROUP)), _sds((B_HEADS, t, SB_TK))]
    operands = [qkv, qkv, qkv, bgate, gain, strict]
    extra = {}
    if gather is not None:
        in_specs += [_ANY, _ANY]
        out_specs += [_ANY]
        out_shape += [_sds(gather[1].shape, gather[1].dtype)]
        operands += [gather[0], gather[1]]
        extra = dict(input_output_aliases={7: 3}, scratch_shapes=_gather_sems(gather[4]))
    return pl.pallas_call(
        body, name="sb_fwd" if gather is None else "sb_fwd_gather", grid=n_steps,
        in_specs=in_specs, out_specs=out_specs, out_shape=out_shape,
        compiler_params=_params(("arbitrary", "arbitrary", "arbitrary")), **extra,
    )(*operands)


def _sb_bwd(qkv, bgate, o, carries, dy, gain, exchange=None):
    t = qkv.shape[0]
    tq = SB_TQ
    strict, lower = _sb_consts()

    def body(q_ref, k_ref, v_ref, bg_ref, o_ref, car_ref, dy_ref, gain_ref, ms_ref, ml_ref,
             dq_ref, dk_ref, dv_ref, dbg_ref, dgain_ref, qb_scr, do_scr):
        qi = pl.program_id(1)
        h = pl.program_id(2)
        lanes = _head_lanes(h)

        @pl.when((qi == 0) & (h == 0))
        def _():
            dk_ref[...] = jnp.zeros_like(dk_ref)
            dv_ref[...] = jnp.zeros_like(dv_ref)
            dgain_ref[...] = jnp.zeros_like(dgain_ref)

        qb_scr[...] = jnp.where(lanes, q_ref[...], jnp.zeros_like(q_ref))
        cmat = ms_ref[...]
        lmat = ml_ref[...]
        o = jnp.where(lanes, o_ref[...], 0.0)
        dy = jnp.where(lanes, dy_ref[...], 0.0)
        bg = bg_ref[...]
        gain = gain_ref[...]
        r = lax.rsqrt(jnp.sum(o * o, axis=-1, keepdims=True) * (1.0 / B_D) + EPS)
        oh = o * r
        sig = _sigmoid(bg)
        sg = bg * sig
        _put(dbg_ref, h, dy * oh * gain * _silu_grad(bg, sig))
        dgain_ref[...] += jnp.sum(dy * oh * sg, axis=0, keepdims=True)
        doh = dy * gain * sg
        do_scr[...] = _bf(r * (doh - oh * (jnp.sum(doh * oh, axis=-1, keepdims=True) * (1.0 / B_D))))

        def group(tiles, state):
            gleft, dq = state
            kv = [(k_ref[pl.ds(off, SB_TK), :], v_ref[pl.ds(off, SB_TK), :]) for off, _, _, _ in tiles]
            zs = [_dot_nt(qb_scr[r0:, :], kb) for (_, r0, _, _), (kb, _) in zip(tiles, kv)]
            dws = [_dot_nt(do_scr[r0:, :], vb) for (_, r0, _, _), (_, vb) in zip(tiles, kv)]
            sps = [_sb_softplus(z, masked) for z, (_, _, masked, _) in zip(zs, tiles)]
            css = [_dot(_bf(sp), cmat) for sp, _ in sps]
            ws, gs = [], []
            for (_, logsig), cs, dw, (_, r0, masked, kb) in zip(sps, css, dws, tiles):
                right = jnp.sum(jnp.where(_key_lane(kb), car_ref[r0:, :], 0.0), axis=1, keepdims=True)
                w = _causal(jnp.exp2(logsig - cs - right), masked)
                ws.append(_bf(w))
                gs.append(dw * w)
            gps = [_dot(_bf(g), lmat) for g in gs]
            dzs = []
            for (_, logsig), g, gp, (_, r0, masked, _) in zip(sps, gs, gps, tiles):
                dz = g - jnp.exp2(logsig) * (g + gleft[r0:] + gp)
                dzs.append(_bf(_causal(dz, masked)))
                gleft = _set_rows(r0, gleft, gleft[r0:] + _lane(gp + g, SB_TK - 1))
            for dz, wb, (kb, _), (off, r0, _, _) in zip(dzs, ws, kv, tiles):
                dq = _set_rows(r0, dq, dq[r0:] + _dot(dz, kb))
                dk_ref[pl.ds(off, SB_TK), :] += _dot_tn(dz, qb_scr[r0:, :])
                dv_ref[pl.ds(off, SB_TK), :] += _dot_tn(wb, do_scr[r0:, :])
            return gleft, dq

        _, dq = _sb_sweep(qi, group, (jnp.zeros((tq, SB_TK), F32), jnp.zeros((tq, 2 * B_D), F32)), SB_GROUP_BWD, True)
        _put(dq_ref, h, jnp.where(lanes, dq * (B_D ** -0.5), 0.0))

    col, full, vec, mat, car = _sb_specs(t, tq)
    blk = pl.BlockSpec((tq, 2 * B_D), lambda p, i, h: (i, p))
    whole = pl.BlockSpec((t, 2 * B_D), lambda p, i, h: (0, p))
    grid = (B_HEADS // 2, t // tq, 2)
    in_specs = [col(0), full(1), full(2), col(0), blk, car, blk, vec, mat, mat]
    out_specs = [blk, whole, whole, blk, vec]
    out_shape = [_sds((t, GROUP))] * 4 + [_sds((1, GROUP))]
    operands = [qkv, qkv, qkv, bgate, o, carries, dy, gain, strict, lower]
    scratch = []
    if exchange is not None:
        step_of = lambda: (pl.program_id(0) * grid[1] + pl.program_id(1)) * grid[2] + pl.program_id(2)
        body = _with_exchange(body, 10, 5, exchange, step_of, grid[0] * grid[1] * grid[2])
        xi, xo, xs, scratch, xop = _exchange_args(exchange)
        in_specs, out_specs, out_shape, operands = in_specs + xi, out_specs + xo, out_shape + xs, operands + xop
    scratch = scratch + [pltpu.VMEM((tq, 2 * B_D), BF16), pltpu.VMEM((tq, 2 * B_D), BF16)]
    return pl.pallas_call(
        body, name="sb_bwd" if exchange is None else "sb_bwd_exchange", grid=grid,
        in_specs=in_specs, out_specs=out_specs, out_shape=out_shape, scratch_shapes=scratch,
        compiler_params=_params(("arbitrary", "arbitrary", "arbitrary")),
    )(*operands)


def _adamw(w, g, m, v):
    rows, cols = w.shape
    tr = rows
    for cand in (400, 256, 128, 64, 32, 16, 8):
        if rows % cand == 0:
            tr = cand
            break

    def body(w_ref, g_ref, m_ref, v_ref, d_ref, nm_ref, nv_ref):
        g_ = g_ref[...]
        m_ = ADAM_B1 * m_ref[...] + (1.0 - ADAM_B1) * g_
        v_ = ADAM_B2 * v_ref[...] + (1.0 - ADAM_B2) * (g_ * g_)
        m_hat = m_ / (1.0 - ADAM_B1 ** ADAM_STEP)
        v_hat = v_ / (1.0 - ADAM_B2 ** ADAM_STEP)
        d_ref[...] = -ADAM_LR * (m_hat / (jnp.sqrt(v_hat) + ADAM_EPS) + ADAM_WD * w_ref[...])
        nm_ref[...] = m_
        nv_ref[...] = v_

    spec = pl.BlockSpec((tr, cols), lambda i: (i, 0))
    return pl.pallas_call(
        body, name="adamw", grid=(rows // tr,), in_specs=[spec] * 4, out_specs=[spec] * 3,
        out_shape=[_sds((rows, cols))] * 3, compiler_params=_params(("arbitrary",)),
    )(w, g, m, v)


_ANY = pl.BlockSpec(memory_space=pl.ANY)


def _place():
    return lax.axis_index("x"), lax.axis_index("y"), lax.axis_index("c")


def _gather_plan(x_ref, out_ref, send_sems, recv_sems, row0, nrows, nc):
    x, y, c = _place()
    me = 2 * x + y
    sibling = (x, y, 1 - c)
    half = nrows // 2
    ch = half // nc
    peers = [me ^ k for k in (1, 2, 3)]

    def rows(shard, hc, r):
        return out_ref.at[shard, pl.ds(row0 + hc * half + r * ch, ch), :]

    def copy(k, shard, hc, r, to, src=None):
        return pltpu.make_async_remote_copy(
            src_ref=rows(shard, hc, r) if src is None else src, dst_ref=rows(shard, hc, r),
            send_sem=send_sems.at[k * nc + r], recv_sem=recv_sems.at[k * nc + r], device_id=to, device_id_type=MESH)

    def first(k, p, r):
        return copy(k, me, c, r, (p >> 1, p & 1, c), src=x_ref.at[pl.ds(row0 + c * half + r * ch, ch), :])

    def start():
        for k, p in enumerate(peers):
            for r in range(nc):
                first(k, p, r).start()

    def forward():
        for k, p in enumerate(peers):
            for r in range(nc):
                copy(k, p, c, r, sibling).wait_recv()
                copy(3 + k, p, c, r, sibling).start()

    def finish():
        for k, p in enumerate(peers):
            for r in range(nc):
                copy(3 + k, p, 1 - c, r, sibling).wait_recv()
        for k, p in enumerate(peers):
            for r in range(nc):
                first(k, p, r).wait_send()
                copy(3 + k, p, c, r, sibling).wait_send()

    return start, forward, finish


def _gather_sems(nc):
    return [pltpu.SemaphoreType.DMA((6 * nc,)), pltpu.SemaphoreType.DMA((6 * nc,))]


def _swap_plan(g_ref, out_ref, send_sems, recv_sems):
    half, ch, nc = HALF_LAYER, CHUNK_ROWS, RS_CHUNKS

    def copies():
        x, y, c = _place()
        return [pltpu.make_async_remote_copy(
            src_ref=g_ref.at[j, pl.ds((1 - c) * half + r * ch, ch), :], dst_ref=out_ref.at[j, pl.ds(r * ch, ch), :],
            send_sem=send_sems.at[j * nc + r], recv_sem=recv_sems.at[j * nc + r],
            device_id=(x, y, 1 - c), device_id_type=MESH) for j in range(N_SHARD) for r in range(nc)]

    def start():
        for cp in copies():
            cp.start()

    def finish():
        for cp in copies():
            cp.wait()

    return start, finish


def _scatter_plan(p_ref, out_ref, send_sems, recv_sems):
    ch, nc = CHUNK_ROWS, RS_CHUNKS

    def copies():
        x, y, c = _place()
        me = 2 * x + y
        return [pltpu.make_async_remote_copy(
            src_ref=p_ref.at[me ^ k, pl.ds(r * ch, ch), :], dst_ref=out_ref.at[k - 1, pl.ds(r * ch, ch), :],
            send_sem=send_sems.at[(k - 1) * nc + r], recv_sem=recv_sems.at[(k - 1) * nc + r],
            device_id=((me ^ k) >> 1, (me ^ k) & 1, c), device_id_type=MESH) for k in (1, 2, 3) for r in range(nc)]

    def start():
        for cp in copies():
            cp.start()

    def finish():
        for cp in copies():
            cp.wait()

    return start, finish


SWAP = (_swap_plan, (N_SHARD, HALF_LAYER, D_MODEL), F32, N_SHARD * RS_CHUNKS)
SCATTER = (_scatter_plan, (3, HALF_LAYER, D_MODEL), BF16, 3 * RS_CHUNKS)


def _exchange_call(kind, operand):
    plan, shape, dtype, n_sems = kind

    def body(in_ref, out_ref, send_sems, recv_sems):
        start, finish = plan(in_ref, out_ref, send_sems, recv_sems)
        start()
        finish()

    return pl.pallas_call(
        body, name="exchange", in_specs=[_ANY], out_specs=_ANY, out_shape=_sds(shape, dtype),
        scratch_shapes=[pltpu.SemaphoreType.DMA((n_sems,)), pltpu.SemaphoreType.DMA((n_sems,))],
    )(operand)


def _with_exchange(body, n_in, n_out, exchange, step_of, n_steps):
    def wrapped(*refs):
        ins, src = refs[:n_in], refs[n_in]
        outs, dst = refs[n_in + 1:n_in + 1 + n_out], refs[n_in + 1 + n_out]
        send_sems, recv_sems = refs[n_in + 2 + n_out:n_in + 4 + n_out]
        start, finish = exchange[0][0](src, dst, send_sems, recv_sems)
        pl.when(step_of() == 0)(start)
        body(*ins, *outs, *refs[n_in + 4 + n_out:])
        pl.when(step_of() == n_steps - 1)(finish)

    return wrapped


def _exchange_args(exchange):
    (plan, shape, dtype, n_sems), operand = exchange
    sems = [pltpu.SemaphoreType.DMA((n_sems,)), pltpu.SemaphoreType.DMA((n_sems,))]
    return [_ANY], [_ANY], [_sds(shape, dtype)], sems, [operand]


def _gather_weights(flat, row0, nrows, nc):
    def body(x_ref, out_ref, send_sems, recv_sems):
        start, forward, finish = _gather_plan(x_ref, out_ref, send_sems, recv_sems, row0, nrows, nc)
        start()
        forward()
        finish()

    return pl.pallas_call(
        body, name="gather_weights", in_specs=[_ANY], out_specs=_ANY,
        out_shape=_sds((N_SHARD, ROWS_FLAT, D_MODEL), BF16), scratch_shapes=_gather_sems(nc),
    )(flat)


def _add_my_half(grads, recv):
    tr = 400
    nb = HALF_LAYER // tr
    core = lax.axis_index("c").astype(jnp.int32).reshape(1)

    def body(c_ref, g_ref, r_ref, o_ref, ob_ref):
        acc = g_ref[...] + r_ref[...]
        o_ref[...] = acc
        ob_ref[...] = _bf(acc)

    out = pl.BlockSpec((None, tr, D_MODEL), lambda j, i, c_ref: (j, i, 0))
    return pl.pallas_call(
        body, name="add_my_half",
        grid_spec=pltpu.PrefetchScalarGridSpec(
            num_scalar_prefetch=1, grid=(N_SHARD, nb),
            in_specs=[pl.BlockSpec((None, tr, D_MODEL), lambda j, i, c_ref: (j, c_ref[0] * nb + i, 0)), out],
            out_specs=[out, out]),
        out_shape=[_sds((N_SHARD, HALF_LAYER, D_MODEL)), _sds((N_SHARD, HALF_LAYER, D_MODEL), BF16)],
        compiler_params=_params(("arbitrary", "arbitrary")),
    )(core, grads, recv)


def _sum_scattered(part, recv):
    tr = 400
    chip = (2 * lax.axis_index("x") + lax.axis_index("y")).astype(jnp.int32).reshape(1)

    def body(c_ref, p_ref, r_ref, o_ref):
        acc = p_ref[...]
        for k in range(3):
            acc = acc + r_ref[k].astype(F32)
        o_ref[...] = acc

    return pl.pallas_call(
        body, name="sum_scattered",
        grid_spec=pltpu.PrefetchScalarGridSpec(
            num_scalar_prefetch=1, grid=(HALF_LAYER // tr,),
            in_specs=[pl.BlockSpec((None, tr, D_MODEL), lambda i, c_ref: (c_ref[0], i, 0)),
                      pl.BlockSpec((3, tr, D_MODEL), lambda i, c_ref: (0, i, 0))],
            out_specs=pl.BlockSpec((tr, D_MODEL), lambda i, c_ref: (i, 0))),
        out_shape=_sds((HALF_LAYER, D_MODEL)), compiler_params=_params(("arbitrary",)),
    )(chip, part, recv)


def _swap_reduced(mine):
    ch, nc = CHUNK_ROWS, RS_CHUNKS

    def body(r_ref, out_ref, send_sems, recv_sems):
        x, y, c = _place()
        copies = [pltpu.make_async_remote_copy(
            src_ref=r_ref.at[l, pl.ds(r * ch, ch), :], dst_ref=out_ref.at[l, pl.ds(r * ch, ch), :],
            send_sem=send_sems.at[l * nc + r], recv_sem=recv_sems.at[l * nc + r],
            device_id=(x, y, 1 - c), device_id_type=MESH) for l in range(2) for r in range(nc)]
        for cp in copies:
            cp.start()
        for cp in copies:
            cp.wait()

    return pl.pallas_call(
        body, name="swap_reduced", in_specs=[_ANY], out_specs=_ANY,
        out_shape=_sds((2, HALF_LAYER, D_MODEL)),
        scratch_shapes=[pltpu.SemaphoreType.DMA((2 * nc,)), pltpu.SemaphoreType.DMA((2 * nc,))],
    )(mine)


def _allreduce_small(vec):
    def body(v_ref, out_ref, buf, send_sems, recv_sems):
        x, y, c = _place()
        me = 4 * x + 2 * y + c
        buf[me] = v_ref[...]
        peers = [me ^ k for k in range(1, N_DEV)]
        sends = [pltpu.make_async_remote_copy(
            src_ref=v_ref, dst_ref=buf.at[me], send_sem=send_sems.at[k], recv_sem=recv_sems.at[k],
            device_id=(p >> 2, (p >> 1) & 1, p & 1), device_id_type=MESH) for k, p in enumerate(peers)]
        for cp in sends:
            cp.start()
        for k, p in enumerate(peers):
            pltpu.make_async_remote_copy(
                src_ref=v_ref, dst_ref=buf.at[p], send_sem=send_sems.at[k], recv_sem=recv_sems.at[k],
                device_id=(p >> 2, (p >> 1) & 1, p & 1), device_id_type=MESH).wait_recv()
        for cp in sends:
            cp.wait_send()
        acc = buf[0]
        for d in range(1, N_DEV):
            acc = acc + buf[d]
        out_ref[...] = acc

    vm = pl.BlockSpec(memory_space=pltpu.VMEM)
    return pl.pallas_call(
        body, name="allreduce_small", in_specs=[vm], out_specs=vm, out_shape=_sds((SMALL_ROWS, 128)),
        scratch_shapes=[pltpu.VMEM((N_DEV, SMALL_ROWS, 128), F32),
                        pltpu.SemaphoreType.DMA((N_DEV - 1,)), pltpu.SemaphoreType.DMA((N_DEV - 1,))],
    )(vec)


def _flatten_shard(w_in, w_out, w_pg, w_pp):
    return jnp.concatenate([w_in.reshape(-1, D_MODEL), w_out.reshape(-1, D_MODEL), w_pg.reshape(-1, D_MODEL),
                            w_pp.reshape(-1, D_MODEL)], axis=0)


def _unflatten_layers(flats):
    a, b, c = D_MODEL, D_MODEL + D_MODEL // N_SHARD, D_MODEL + 2 * (D_MODEL // N_SHARD)
    q = D_MODEL // N_SHARD
    return (jnp.stack([f[:a] for f in flats]), jnp.stack([f[a:b] for f in flats]),
            jnp.stack([f[b:c] for f in flats]), jnp.stack([f[c:].reshape(D_PLE, q) for f in flats]))


def _full_w_pp(gathered):
    c = ROWS_W_IN + ROWS_W_OUT + ROWS_W_PG
    q = D_MODEL // N_SHARD
    rpp = ROWS_W_PP // 2
    return [gathered[:, c + l * rpp:c + (l + 1) * rpp, :].reshape(N_SHARD, D_PLE, q).transpose(1, 0, 2)
            .reshape(D_PLE, D_MODEL) for l in range(2)]


def _layer_grads(dw_in, dw_out, dw_pg, dw_pp):
    q = D_MODEL // N_SHARD
    rpp = ROWS_W_PP // 2
    rest = jnp.concatenate([dw_out.reshape(N_SHARD, q, D_MODEL), dw_pg.reshape(N_SHARD, q, D_MODEL),
                            dw_pp.reshape(D_PLE, N_SHARD, q).transpose(1, 0, 2).reshape(N_SHARD, rpp, D_MODEL)], axis=1)
    return lax.dynamic_update_slice(dw_in, rest, (0, D_MODEL, 0))


def _lower_bounds(lb_logits):
    sm = jax.nn.softmax(lb_logits.astype(F32), axis=0)
    return jnp.cumsum(sm, axis=0) - sm[0:1]


def kernel(x, p, norm_mix, w_in, a_out_norm, b_out_norm, w_out, lb_logits, ple_gate_norm, w_ple_gate, w_ple_proj, ple_post_norm, final_norm, loss_target, m_norm_mix, m_w_in, m_a_out_norm, m_b_out_norm, m_w_out, m_lb_logits, m_ple_gate_norm, m_w_ple_gate, m_w_ple_proj, m_ple_post_norm, m_final_norm, v_norm_mix, v_w_in, v_a_out_norm, v_b_out_norm, v_w_out, v_lb_logits, v_ple_gate_norm, v_w_ple_gate, v_w_ple_proj, v_ple_post_norm, v_final_norm):
    t = x.shape[1]
    h0 = x.reshape(t, D_MODEL)
    target = loss_target.reshape(t, D_MODEL)
    pl_in = p.reshape(2, t, D_PLE)

    w_flat_bf = _flatten_shard(_bf(w_in), _bf(w_out), _bf(w_ple_gate), _bf(w_ple_proj))
    chip = 2 * lax.axis_index("x") + lax.axis_index("y")
    gathered = lax.dynamic_update_slice(_gather_weights(w_flat_bf, 0, D_MODEL, 2), w_flat_bf[None], (chip, 0, 0))
    lbs, lbs_vjp = jax.vjp(_lower_bounds, lb_logits)

    saved = []
    h = h0
    for l in range(2):
        g_mix = norm_mix[l].reshape(1, D_MODEL)
        lb = lbs[l].reshape(1, GROUP)
        ga = a_out_norm[l].reshape(1, GROUP)
        gb = b_out_norm[l].reshape(1, GROUP)
        proj, qkv, bgate = _inproj(h, g_mix, gathered, l)
        ya, states = _hgrn_fwd(proj, lb, ga)
        if l == 0:
            ob, yb, cars, gathered = _sb_fwd(qkv, bgate, gb, (w_flat_bf, gathered, D_MODEL, ROWS_FLAT - D_MODEL, 4))
            w_pps = _full_w_pp(gathered)
        else:
            ob, yb, cars = _sb_fwd(qkv, bgate, gb)
        h1 = _outproj(h, ya, yb, gathered, l)
        g_post = ple_post_norm[l].reshape(1, D_MODEL)
        g_gate = ple_gate_norm[l].reshape(1, D_MODEL)
        saved.append((h, proj, qkv, bgate, states, ya, yb, ob, cars, h1))
        if l == 0:
            h = _ple_fwd(h1, pl_in[l], w_pps[l], gathered, l, g_post, g_gate)
        else:
            dh, d_final, loss_part = _ple_fwd_final(h1, pl_in[l], w_pps[l], gathered, l, g_post, g_gate,
                                                    final_norm.reshape(1, D_MODEL), target)

    g_layer, chip_sum, scattered = [None] * 2, [None] * 2, [None] * 2
    d_mix, d_a, d_b, d_lb, d_gate, d_post = [None] * 2, [None] * 2, [None] * 2, [None] * 2, [None] * 2, [None] * 2
    for l in (1, 0):
        h_in, proj, qkv, bgate, states, ya, yb, ob, cars, h1 = saved[l]
        g_mix = norm_mix[l].reshape(1, D_MODEL)
        lb = lbs[l].reshape(1, GROUP)
        ga = a_out_norm[l].reshape(1, GROUP)
        gb = b_out_norm[l].reshape(1, GROUP)
        g_post = ple_post_norm[l].reshape(1, D_MODEL)
        g_gate = ple_gate_norm[l].reshape(1, D_MODEL)
        if l == 1:
            dh1, dw_pg, dw_pp, d_gate[l], d_post[l] = _ple_bwd(dh, h1, pl_in[l], w_pps[l], gathered, l, g_post, g_gate)
            dya, dyb, dw_out = _outproj_bwd(dh1, ya, yb, gathered, l)
            dbq, dbk, dbv, dbg, d_b[l] = _sb_bwd(qkv, bgate, ob, cars, dyb, gb)
        else:
            dh1, dw_pg, dw_pp, d_gate[l], d_post[l], from_sibling = _ple_bwd(
                dh, h1, pl_in[l], w_pps[l], gathered, l, g_post, g_gate, (SWAP, g_layer[1]))
            chip_sum[1], chip_sum_bf = _add_my_half(g_layer[1], from_sibling)
            dya, dyb, dw_out = _outproj_bwd(dh1, ya, yb, gathered, l)
            dbq, dbk, dbv, dbg, d_b[l], scattered[1] = _sb_bwd(qkv, bgate, ob, cars, dyb, gb, (SCATTER, chip_sum_bf))
        da, d_lb[l], d_a[l] = _hgrn_bwd(proj, lb, ga, states, dya)
        db = jnp.stack([dbq, dbk * LN2, dbv, dbg]).astype(BF16)
        g_layer[l] = _layer_grads(_inproj_bwd_dw(h_in, g_mix, da, db), dw_out, dw_pg, dw_pp)
        if l == 1:
            dh, d_mix[l] = _inproj_bwd_dx(dh1, h_in, g_mix, gathered, l, da, db)
        else:
            dh, d_mix[l], from_sibling = _inproj_bwd_dx(dh1, h_in, g_mix, gathered, l, da, db, (SWAP, g_layer[0]))
    grad_x = dh.reshape(x.shape)

    chip_sum[0], chip_sum_bf = _add_my_half(g_layer[0], from_sibling)
    scattered[0] = _exchange_call(SCATTER, chip_sum_bf)
    mine = jnp.stack([_sum_scattered(chip_sum[l], scattered[l]) for l in range(2)])
    other = _swap_reduced(mine)
    south = lax.axis_index("c") == 0
    g_w_in, g_w_out, g_w_pg, g_w_pp = _unflatten_layers(
        [jnp.concatenate([jnp.where(south, mine[l], other[l]), jnp.where(south, other[l], mine[l])]) for l in range(2)])

    small = jnp.concatenate([
        jnp.concatenate(d_mix).reshape(-1, 128), jnp.concatenate(d_a).reshape(-1, 128),
        jnp.concatenate(d_b).reshape(-1, 128), jnp.concatenate(d_lb).reshape(-1, 128),
        jnp.concatenate(d_gate).reshape(-1, 128), jnp.concatenate(d_post).reshape(-1, 128),
        d_final.reshape(-1, 128), jnp.broadcast_to(loss_part, (8, 128))], axis=0)
    small = _allreduce_small(small)
    loss = small[80, 0]
    g_norm_mix = small[0:16].reshape(2, D_MODEL)
    g_a = small[16:24].reshape(2, GROUP)
    g_b = small[24:32].reshape(2, GROUP)
    (g_lb,) = lbs_vjp(small[32:40].reshape(2, GROUP))
    g_gate = small[40:56].reshape(2, D_MODEL)
    g_post = small[56:72].reshape(2, D_MODEL)
    g_final = small[72:80].reshape(D_MODEL)

    def adam_matrix(w, g, m, v):
        d, nm, nv = _adamw(w.reshape(-1, D_MODEL), g.reshape(-1, D_MODEL), m.reshape(-1, D_MODEL), v.reshape(-1, D_MODEL))
        return d.reshape(w.shape), nm.reshape(w.shape), nv.reshape(w.shape)

    d_w_in, nm_w_in, nv_w_in = adam_matrix(w_in, g_w_in, m_w_in, v_w_in)
    d_w_out, nm_w_out, nv_w_out = adam_matrix(w_out, g_w_out, m_w_out, v_w_out)
    d_w_pg, nm_w_pg, nv_w_pg = adam_matrix(w_ple_gate, g_w_pg, m_w_ple_gate, v_w_ple_gate)
    d_w_pp, nm_w_pp, nv_w_pp = adam_matrix(w_ple_proj, g_w_pp, m_w_ple_proj, v_w_ple_proj)

    small_w = [norm_mix, a_out_norm, b_out_norm, lb_logits, ple_gate_norm, ple_post_norm, final_norm]
    small_g = [g_norm_mix, g_a, g_b, g_lb, g_gate, g_post, g_final]
    small_m = [m_norm_mix, m_a_out_norm, m_b_out_norm, m_lb_logits, m_ple_gate_norm, m_ple_post_norm, m_final_norm]
    small_v = [v_norm_mix, v_a_out_norm, v_b_out_norm, v_lb_logits, v_ple_gate_norm, v_ple_post_norm, v_final_norm]
    pack = lambda arrs: jnp.concatenate([a.reshape(-1, 128) for a in arrs], axis=0)
    ds, nms, nvs = _adamw(pack(small_w), pack(small_g), pack(small_m), pack(small_v))

    def unpack(packed):
        out, r = [], 0
        for a in small_w:
            n = a.size // 128
            out.append(packed[r:r + n].reshape(a.shape))
            r += n
        return out

    d_s, nm_s, nv_s = unpack(ds), unpack(nms), unpack(nvs)

    def ordered(s, big):
        return [s[0], big[0], s[1], s[2], big[1], s[3], s[4], big[2], big[3], s[5], s[6]]

    grads = ordered(small_g, [g_w_in, g_w_out, g_w_pg, g_w_pp])
    deltas = ordered(d_s, [d_w_in, d_w_out, d_w_pg, d_w_pp])
    new_m = ordered(nm_s, [nm_w_in, nm_w_out, nm_w_pg, nm_w_pp])
    new_v = ordered(nv_s, [nv_w_in, nv_w_out, nv_w_pg, nv_w_pp])
    return (loss, grad_x, *grads, *deltas, *new_m, *new_v)
```

```python
import functools
import math

import numpy as np
import jax
import jax.numpy as jnp
from jax import lax
from jax.experimental import pallas as pl
from jax.experimental.pallas import tpu as pltpu

F32 = jnp.float32
BF16 = jnp.bfloat16
MESH = pl.DeviceIdType.MESH

D_MODEL = 1024
D_PLE = 256
D_IN = 4096
A_HEADS, A_D = 4, 128
B_HEADS, B_D = 8, 64
GROUP = 512
EPS = 1e-6
N_SHARD = 4
N_DEV = 8

HG_CHUNK = 128
HG_LEVELS = 7
SB_TQ = 1024
SB_TK = 128
SB_GROUP_FWD, SB_GROUP_BWD = (2, 2), (8, 2)
LOG2E = 1.4426950408889634
LN2 = 0.6931471805599453

ADAM_LR, ADAM_B1, ADAM_B2, ADAM_EPS, ADAM_WD, ADAM_STEP = 0.001, 0.9, 0.999, 1e-08, 0.01, 10

VMEM_LIMIT = 48 * 1024 * 1024
VMEM_LIMIT_BIG = 58 * 1024 * 1024

ROWS_W_IN = 2 * D_MODEL
ROWS_W_OUT = 2 * (D_MODEL // N_SHARD)
ROWS_W_PG = 2 * (D_MODEL // N_SHARD)
ROWS_W_PP = 2 * (D_PLE * (D_MODEL // N_SHARD) // D_MODEL)
ROWS_FLAT = ROWS_W_IN + ROWS_W_OUT + ROWS_W_PG + ROWS_W_PP
HALF_FLAT = ROWS_FLAT // 2
N_CHUNK = 10
CHUNK_ROWS = HALF_FLAT // N_CHUNK

ROWS_LAYER = ROWS_FLAT // 2
HALF_LAYER = ROWS_LAYER // 2
RS_CHUNKS = HALF_LAYER // CHUNK_ROWS

SMALL_ROWS = 88


def _sds(shape, dtype=F32):
    return jax.ShapeDtypeStruct(shape, dtype)


def _params(sem=None, vmem_limit=VMEM_LIMIT):
    kw = dict(vmem_limit_bytes=vmem_limit)
    if sem is not None:
        kw["dimension_semantics"] = sem
    return pltpu.CompilerParams(**kw)


def _dot(a, b, precision=None):
    return lax.dot_general(a, b, (((1,), (0,)), ((), ())), preferred_element_type=F32, precision=precision)


def _dot_nt(a, b, precision=None):
    return lax.dot_general(a, b, (((1,), (1,)), ((), ())), preferred_element_type=F32, precision=precision)


def _dot_tn(a, b, precision=None):
    return lax.dot_general(a, b, (((0,), (0,)), ((), ())), preferred_element_type=F32, precision=precision)


def _bf(x):
    return x.astype(BF16)


def _split(x):
    hi = x.astype(BF16)
    lo = (x - hi.astype(F32)).astype(BF16)
    return hi, lo


def _rms(x):
    r = lax.rsqrt(jnp.mean(x * x, axis=-1, keepdims=True) + EPS)
    return x * r, r


def _rms_bwd(dxh, xh, r):
    return r * (dxh - xh * jnp.mean(dxh * xh, axis=-1, keepdims=True))


def _sigmoid(x):
    return 1.0 / (1.0 + jnp.exp(-x))


def _silu_grad(x, sig):
    return sig * (1.0 + x * (1.0 - sig))


def _row_tile(t, want):
    return min(t, want)


def _inproj(h, g, gathered, layer):
    t = h.shape[0]
    tm = _row_tile(t, 512)

    def body(h_ref, g_ref, w_ref, pa_ref, qkv_ref, bg_ref):
        xh, _ = _rms(h_ref[...])
        u = _bf(xh * g_ref[...])
        for j in range(8):
            acc = _dot(u, w_ref[j // 2, :, pl.ds((j % 2) * GROUP, GROUP)])
            if j < 4:
                pa_ref[:, pl.ds(j * GROUP, GROUP)] = acc
            elif j == 4:
                qkv_ref[:, pl.ds(0, GROUP)] = _bf(acc * (B_D ** -0.5 * LOG2E))
            elif j < 7:
                qkv_ref[:, pl.ds((j - 4) * GROUP, GROUP)] = _bf(acc)
            else:
                bg_ref[...] = acc

    return pl.pallas_call(
        body, name="inproj", grid=(t // tm,),
        in_specs=[pl.BlockSpec((tm, D_MODEL), lambda i: (i, 0)),
                  pl.BlockSpec((1, D_MODEL), lambda i: (0, 0)),
                  pl.BlockSpec((N_SHARD, D_MODEL, D_MODEL), lambda i: (0, layer, 0))],
        out_specs=[pl.BlockSpec((tm, 4 * GROUP), lambda i: (i, 0)), pl.BlockSpec((tm, 3 * GROUP), lambda i: (i, 0)),
                   pl.BlockSpec((tm, GROUP), lambda i: (i, 0))],
        out_shape=[_sds((t, 4 * GROUP)), _sds((t, 3 * GROUP), BF16), _sds((t, GROUP))],
        compiler_params=_params(("arbitrary",)),
    )(h, g, gathered)


def _rows_spec(first_row):
    q = D_MODEL // N_SHARD
    return pl.BlockSpec((N_SHARD, q, D_MODEL), lambda i: (0, first_row // q, 0))


def _outproj(h, ya, yb, gathered, layer):
    t = h.shape[0]
    tm = _row_tile(t, 512)

    def body(h_ref, ya_ref, yb_ref, w_ref, o_ref):
        o_ref[...] = (h_ref[...] + _dot(_bf(ya_ref[...]), w_ref[0:2].reshape(GROUP, D_MODEL))
                      + _dot(_bf(yb_ref[...]), w_ref[2:4].reshape(GROUP, D_MODEL)))

    return pl.pallas_call(
        body, name="outproj", grid=(t // tm,),
        in_specs=[pl.BlockSpec((tm, D_MODEL), lambda i: (i, 0)),
                  pl.BlockSpec((tm, GROUP), lambda i: (i, 0)),
                  pl.BlockSpec((tm, GROUP), lambda i: (i, 0)),
                  _rows_spec(ROWS_W_IN + layer * (D_MODEL // N_SHARD))],
        out_specs=pl.BlockSpec((tm, D_MODEL), lambda i: (i, 0)),
        out_shape=_sds((t, D_MODEL)), compiler_params=_params(("arbitrary",)),
    )(h, ya, yb, gathered)


def _ple_mix(x, p_ref, wpp_ref, wpg_ref, gp_ref, gg_ref):
    ph, _ = _rms(_dot(_bf(p_ref[...]), wpp_ref[...]))
    xh, _ = _rms(x)
    gate = _sigmoid(_dot(_bf(xh * gg_ref[...]), wpg_ref[...].reshape(D_MODEL, D_MODEL)))
    return x + gate * (ph * gp_ref[...])


def _ple_specs(tm, layer):
    return [pl.BlockSpec((tm, D_MODEL), lambda i: (i, 0)),
            pl.BlockSpec((tm, D_PLE), lambda i: (i, 0)),
            pl.BlockSpec((D_PLE, D_MODEL), lambda i: (0, 0)),
            _rows_spec(ROWS_W_IN + ROWS_W_OUT + layer * (D_MODEL // N_SHARD)),
            pl.BlockSpec((1, D_MODEL), lambda i: (0, 0)),
            pl.BlockSpec((1, D_MODEL), lambda i: (0, 0))]


def _ple_fwd(h, p, w_pp, gathered, layer, g_post, g_gate):
    t = h.shape[0]
    tm = _row_tile(t, 256)

    def body(h_ref, p_ref, wpp_ref, wpg_ref, gp_ref, gg_ref, o_ref):
        o_ref[...] = _ple_mix(h_ref[...], p_ref, wpp_ref, wpg_ref, gp_ref, gg_ref)

    return pl.pallas_call(
        body, name="ple_fwd", grid=(t // tm,), in_specs=_ple_specs(tm, layer),
        out_specs=pl.BlockSpec((tm, D_MODEL), lambda i: (i, 0)),
        out_shape=_sds((t, D_MODEL)), compiler_params=_params(("arbitrary",)),
    )(h, p, w_pp, gathered, g_post, g_gate)


def _ple_fwd_final(h, p, w_pp, gathered, layer, g_post, g_gate, g_final, target):
    t = h.shape[0]
    tm = _row_tile(t, 256)

    def body(h_ref, p_ref, wpp_ref, wpg_ref, gp_ref, gg_ref, gf_ref, t_ref, dh_ref, dg_ref, loss_ref):
        @pl.when(pl.program_id(0) == 0)
        def _():
            dg_ref[...] = jnp.zeros_like(dg_ref)
            loss_ref[...] = jnp.zeros_like(loss_ref)

        xh, r = _rms(_ple_mix(h_ref[...], p_ref, wpp_ref, wpg_ref, gp_ref, gg_ref))
        gf = gf_ref[...]
        err = xh * gf - t_ref[...]
        part = 0.5 * jnp.sum(jnp.mean(err * err, axis=-1, keepdims=True), axis=0, keepdims=True)
        loss_ref[...] += jnp.broadcast_to(part, loss_ref.shape)
        dy = err * (1.0 / D_MODEL)
        dg_ref[...] += jnp.sum(dy * xh, axis=0, keepdims=True)
        dh_ref[...] = _rms_bwd(dy * gf, xh, r)

    return pl.pallas_call(
        body, name="ple_fwd_final", grid=(t // tm,),
        in_specs=_ple_specs(tm, layer) + [pl.BlockSpec((1, D_MODEL), lambda i: (0, 0)),
                                          pl.BlockSpec((tm, D_MODEL), lambda i: (i, 0))],
        out_specs=[pl.BlockSpec((tm, D_MODEL), lambda i: (i, 0)),
                   pl.BlockSpec((1, D_MODEL), lambda i: (0, 0)),
                   pl.BlockSpec((1, 128), lambda i: (0, 0))],
        out_shape=[_sds((t, D_MODEL)), _sds((1, D_MODEL)), _sds((1, 128))],
        compiler_params=_params(("arbitrary",)),
    )(h, p, w_pp, gathered, g_post, g_gate, g_final, target)


def _ple_bwd(dh2, h, p, w_pp, gathered, layer, g_post, g_gate, exchange=None):
    t = h.shape[0]
    tm = _row_tile(t, 512)

    def body(d_ref, h_ref, p_ref, wpp_ref, wpg_ref, gp_ref, gg_ref, dh_ref, dwpg_ref, dwpp_ref, dgg_ref, dgp_ref):
        @pl.when(pl.program_id(0) == 0)
        def _():
            dwpg_ref[...] = jnp.zeros_like(dwpg_ref)
            dwpp_ref[...] = jnp.zeros_like(dwpp_ref)
            dgg_ref[...] = jnp.zeros_like(dgg_ref)
            dgp_ref[...] = jnp.zeros_like(dgp_ref)

        d = d_ref[...]
        x = h_ref[...]
        gp = gp_ref[...]
        gg = gg_ref[...]
        pb = _bf(p_ref[...])
        ph, rp = _rms(_dot(pb, wpp_ref[...]))
        pe = ph * gp
        xh, rx = _rms(x)
        un = _bf(xh * gg)
        wpg = wpg_ref[...].reshape(D_MODEL, D_MODEL)
        gate = _sigmoid(_dot(un, wpg))
        dgpre = _bf(d * pe * gate * (1.0 - gate))
        dun = _dot_nt(dgpre, wpg)
        dh_ref[...] = d + _rms_bwd(dun * gg, xh, rx)
        dgg_ref[...] += jnp.sum(dun * xh, axis=0, keepdims=True)
        dwpg_ref[...] += _dot_tn(un, dgpre)
        dpe = d * gate
        dgp_ref[...] += jnp.sum(dpe * ph, axis=0, keepdims=True)
        dwpp_ref[...] += _dot_tn(pb, _bf(_rms_bwd(dpe * gp, ph, rp)))

    in_specs = [pl.BlockSpec((tm, D_MODEL), lambda i: (i, 0)),
                pl.BlockSpec((tm, D_MODEL), lambda i: (i, 0)),
                pl.BlockSpec((tm, D_PLE), lambda i: (i, 0)),
                pl.BlockSpec((D_PLE, D_MODEL), lambda i: (0, 0)),
                _rows_spec(ROWS_W_IN + ROWS_W_OUT + layer * (D_MODEL // N_SHARD)),
                pl.BlockSpec((1, D_MODEL), lambda i: (0, 0)),
                pl.BlockSpec((1, D_MODEL), lambda i: (0, 0))]
    out_specs = [pl.BlockSpec((tm, D_MODEL), lambda i: (i, 0)),
                 pl.BlockSpec((D_MODEL, D_MODEL), lambda i: (0, 0)),
                 pl.BlockSpec((D_PLE, D_MODEL), lambda i: (0, 0)),
                 pl.BlockSpec((1, D_MODEL), lambda i: (0, 0)),
                 pl.BlockSpec((1, D_MODEL), lambda i: (0, 0))]
    out_shape = [_sds((t, D_MODEL)), _sds((D_MODEL, D_MODEL)), _sds((D_PLE, D_MODEL)),
                 _sds((1, D_MODEL)), _sds((1, D_MODEL))]
    operands = [dh2, h, p, w_pp, gathered, g_post, g_gate]
    scratch = []
    if exchange is not None:
        body = _with_exchange(body, 7, 5, exchange, lambda: pl.program_id(0), t // tm)
        xi, xo, xs, scratch, xop = _exchange_args(exchange)
        in_specs, out_specs, out_shape, operands = in_specs + xi, out_specs + xo, out_shape + xs, operands + xop
    return pl.pallas_call(
        body, name="ple_bwd" if exchange is None else "ple_bwd_exchange", grid=(t // tm,),
        in_specs=in_specs, out_specs=out_specs, out_shape=out_shape, scratch_shapes=scratch,
        compiler_params=_params(("arbitrary",)),
    )(*operands)


def _outproj_bwd(dh, ya, yb, gathered, layer):
    t = dh.shape[0]
    tm = _row_tile(t, 512)

    def body(d_ref, ya_ref, yb_ref, w_ref, dya_ref, dyb_ref, dw_ref):
        @pl.when(pl.program_id(0) == 0)
        def _():
            dw_ref[...] = jnp.zeros_like(dw_ref)

        d = _bf(d_ref[...])
        dya_ref[...] = _dot_nt(d, w_ref[0:2].reshape(GROUP, D_MODEL))
        dyb_ref[...] = _dot_nt(d, w_ref[2:4].reshape(GROUP, D_MODEL))
        dw_ref[pl.ds(0, GROUP), :] += _dot_tn(_bf(ya_ref[...]), d)
        dw_ref[pl.ds(GROUP, GROUP), :] += _dot_tn(_bf(yb_ref[...]), d)

    return pl.pallas_call(
        body, name="outproj_bwd", grid=(t // tm,),
        in_specs=[pl.BlockSpec((tm, D_MODEL), lambda i: (i, 0)),
                  pl.BlockSpec((tm, GROUP), lambda i: (i, 0)),
                  pl.BlockSpec((tm, GROUP), lambda i: (i, 0)),
                  _rows_spec(ROWS_W_IN + layer * (D_MODEL // N_SHARD))],
        out_specs=[pl.BlockSpec((tm, GROUP), lambda i: (i, 0)),
                   pl.BlockSpec((tm, GROUP), lambda i: (i, 0)),
                   pl.BlockSpec((D_MODEL, D_MODEL), lambda i: (0, 0))],
        out_shape=[_sds((t, GROUP)), _sds((t, GROUP)), _sds((D_MODEL, D_MODEL))],
        compiler_params=_params(("arbitrary",)),
    )(dh, ya, yb, gathered)


def _inproj_bwd_dx(dres, h, g, gathered, layer, da, db, exchange=None):
    t = h.shape[0]
    tm = _row_tile(t, 512)

    def body(dres_ref, h_ref, g_ref, w_ref, da_ref, db_ref, dh_ref, dg_ref):
        @pl.when(pl.program_id(0) == 0)
        def _():
            dg_ref[...] = jnp.zeros_like(dg_ref)

        du = jnp.zeros((tm, D_MODEL), F32)
        for i in range(8):
            part = da_ref[i] if i < 4 else db_ref[i - 4]
            du = du + _dot_nt(part, w_ref[i // 2, :, pl.ds((i % 2) * GROUP, GROUP)])
        xh, r = _rms(h_ref[...])
        dg_ref[...] += jnp.sum(du * xh, axis=0, keepdims=True)
        dh_ref[...] = dres_ref[...] + _rms_bwd(du * g_ref[...], xh, r)

    in_specs = [pl.BlockSpec((tm, D_MODEL), lambda i: (i, 0)),
                pl.BlockSpec((tm, D_MODEL), lambda i: (i, 0)),
                pl.BlockSpec((1, D_MODEL), lambda i: (0, 0)),
                pl.BlockSpec((N_SHARD, D_MODEL, D_MODEL), lambda i: (0, layer, 0)),
                pl.BlockSpec((4, tm, GROUP), lambda i: (0, i, 0)),
                pl.BlockSpec((4, tm, GROUP), lambda i: (0, i, 0))]
    out_specs = [pl.BlockSpec((tm, D_MODEL), lambda i: (i, 0)), pl.BlockSpec((1, D_MODEL), lambda i: (0, 0))]
    out_shape = [_sds((t, D_MODEL)), _sds((1, D_MODEL))]
    operands = [dres, h, g, gathered, da, db]
    scratch = []
    if exchange is not None:
        body = _with_exchange(body, 6, 2, exchange, lambda: pl.program_id(0), t // tm)
        xi, xo, xs, scratch, xop = _exchange_args(exchange)
        in_specs, out_specs, out_shape, operands = in_specs + xi, out_specs + xo, out_shape + xs, operands + xop
    return pl.pallas_call(
        body, name="inproj_bwd_dx" if exchange is None else "inproj_bwd_dx_exchange", grid=(t // tm,),
        in_specs=in_specs, out_specs=out_specs, out_shape=out_shape, scratch_shapes=scratch,
        compiler_params=_params(("arbitrary",)),
    )(*operands)


def _inproj_bwd_dw(h, g, da, db):
    t = h.shape[0]
    tm = _row_tile(t, 512)

    def body(h_ref, g_ref, da_ref, db_ref, dw_ref):
        @pl.when(pl.program_id(0) == 0)
        def _():
            dw_ref[...] = jnp.zeros_like(dw_ref)

        xh, _ = _rms(h_ref[...])
        u = _bf(xh * g_ref[...])
        for i in range(8):
            dw_ref[i // 2, :, pl.ds((i % 2) * GROUP, GROUP)] += _dot_tn(u, da_ref[i] if i < 4 else db_ref[i - 4])

    return pl.pallas_call(
        body, name="inproj_bwd_dw", grid=(t // tm,),
        in_specs=[pl.BlockSpec((tm, D_MODEL), lambda i: (i, 0)),
                  pl.BlockSpec((1, D_MODEL), lambda i: (0, 0)),
                  pl.BlockSpec((4, tm, GROUP), lambda i: (0, i, 0)),
                  pl.BlockSpec((4, tm, GROUP), lambda i: (0, i, 0))],
        out_specs=pl.BlockSpec((N_SHARD, D_MODEL, D_MODEL), lambda i: (0, 0, 0)),
        out_shape=_sds((N_SHARD, ROWS_LAYER, D_MODEL)), compiler_params=_params(("arbitrary",), VMEM_LIMIT_BIG),
    )(h, g, da, db)


def _hgrn_consts():
    c, nl = HG_CHUNK, HG_LEVELS
    t = np.arange(c)
    tril = np.tril(np.ones((c, c), np.float32))
    masks = np.zeros((nl + 1, c, c), np.float32)
    masks[0] = np.eye(c, dtype=np.float32)
    dmat = np.zeros(((nl + 2) * c, c), np.float32)
    dmat[0:c] = tril
    for l in range(nl):
        m = c >> (l + 1)
        blk = t // (2 * m)
        r = blk * 2 * m + m - 1
        upper = (t % (2 * m)) >= m
        masks[l + 1] = ((blk[:, None] == blk[None, :]) & upper[:, None] & (~upper)[None, :]).astype(np.float32)
        dmat[(l + 1) * c:(l + 2) * c] = tril[t] - tril[r]
    dmat[(nl + 1) * c:] = np.triu(np.ones((c, c), np.float32), k=1)
    return jnp.asarray(masks), jnp.asarray(dmat, BF16)


HG_HEADS = 4


def _hgrn_pre(aq, af, lb):
    sq = _sigmoid(aq)
    sneg = _sigmoid(-af)
    kk = (1.0 - lb) * sneg
    return sq, aq * sq, sneg, kk, jnp.log1p(-kk)


def _hgrn_x(logf, dmat_ref):
    dm = dmat_ref[pl.ds(0, (HG_LEVELS + 1) * HG_CHUNK), :]
    lhi, llo = _split(logf)
    return _dot(dm, lhi) + _dot(dm, llo)


def _hgrn_level(x_all, l, q, kk):
    c = HG_CHUNK
    x = x_all[(l + 1) * c:(l + 2) * c]
    qf = jnp.exp(jnp.minimum(x, 0.0))
    kf = jnp.exp(-jnp.maximum(x, 0.0))
    return qf, kf, _bf(q * qf), _bf(kk * kf)


def _hgrn_scores(xs, qs, kks, mask_ref):
    ps = [mask_ref[0] * _dot_nt(_bf(q), _bf(kk)) for q, kk in zip(qs, kks)]
    for l in range(HG_LEVELS):
        for i, (x_all, q, kk) in enumerate(zip(xs, qs, kks)):
            _, _, ql, kl = _hgrn_level(x_all, l, q, kk)
            ps[i] = ps[i] + mask_ref[l + 1] * _dot_nt(ql, kl)
    return ps


def _hgrn_specs(n_chunks, rev):
    c, w = HG_CHUNK, HG_HEADS * A_D
    cidx = (lambda n: n_chunks - 1 - n) if rev else (lambda n: n)
    col = lambda g: pl.BlockSpec((c, w), lambda h, n: (cidx(n), g * (A_HEADS // HG_HEADS) + h))
    vec = pl.BlockSpec((1, w), lambda h, n: (0, h))
    mask = pl.BlockSpec((HG_LEVELS + 1, c, c), lambda h, n: (0, 0, 0))
    dmat = pl.BlockSpec(((HG_LEVELS + 2) * c, c), lambda h, n: (0, 0))
    state = pl.BlockSpec((HG_HEADS, None, A_D, A_D), lambda h, n: (h, cidx(n), 0, 0))
    return cidx, col, vec, mask, dmat, state


def _lanes(i):
    return pl.ds(i * A_D, A_D)


def _hgrn_fwd(proj, lb, gain):
    t = proj.shape[0]
    c = HG_CHUNK
    nch = t // c
    masks, dmat = _hgrn_consts()
    cidx, col, vec, mask_spec, dmat_spec, state_spec = _hgrn_specs(nch, False)
    heads = range(HG_HEADS)

    def body(aq_ref, af_ref, ai_ref, ag_ref, lb_ref, gain_ref, mask_ref, dmat_ref, y_ref, st_ref, s_scr):
        @pl.when(pl.program_id(1) == 0)
        def _():
            s_scr[...] = jnp.zeros_like(s_scr)

        pre = [_hgrn_pre(aq_ref[:, _lanes(i)], af_ref[:, _lanes(i)], lb_ref[:, _lanes(i)]) for i in heads]
        qs, kks = [p[1] for p in pre], [p[3] for p in pre]
        xs = [_hgrn_x(p[4], dmat_ref) for p in pre]
        bs = [x[0:c] for x in xs]
        b_lasts = [jnp.sum(p[4], axis=0, keepdims=True) for p in pre]
        ps = _hgrn_scores(xs, qs, kks, mask_ref)
        ss = [s_scr[i] for i in heads]
        vbs = [_bf(ai_ref[:, _lanes(i)]) for i in heads]
        os_ = [_dot(_bf(ps[i]), vbs[i]) + _dot_nt(_bf(qs[i] * jnp.exp(bs[i])), _bf(ss[i])) for i in heads]
        for i in heads:
            st_ref[i] = ss[i]
            s_scr[i] = ss[i] * jnp.exp(b_lasts[i]) + _dot_tn(vbs[i], _bf(kks[i] * jnp.exp(b_lasts[i] - bs[i])))
            oh, _ = _rms(os_[i])
            ag = ag_ref[:, _lanes(i)]
            y_ref[:, _lanes(i)] = oh * gain_ref[:, _lanes(i)] * (ag * _sigmoid(ag))

    return pl.pallas_call(
        body, name="hgrn_fwd", grid=(A_HEADS // HG_HEADS, nch),
        in_specs=[col(0), col(1), col(2), col(3), vec, vec, mask_spec, dmat_spec],
        out_specs=[pl.BlockSpec((c, HG_HEADS * A_D), lambda h, n: (n, h)), state_spec],
        out_shape=[_sds((t, GROUP)), _sds((A_HEADS, nch, A_D, A_D))],
        scratch_shapes=[pltpu.VMEM((HG_HEADS, A_D, A_D), F32)],
        compiler_params=_params(("arbitrary", "arbitrary")),
    )(proj, proj, proj, proj, lb, gain, masks, dmat)


def _hgrn_bwd(proj, lb, gain, states, dya):
    t = proj.shape[0]
    c, nl = HG_CHUNK, HG_LEVELS
    nch = t // c
    masks, dmat = _hgrn_consts()
    cidx, col, vec, mask_spec, dmat_spec, state_spec = _hgrn_specs(nch, True)
    heads = range(HG_HEADS)

    def body(aq_ref, af_ref, ai_ref, ag_ref, lb_ref, gain_ref, mask_ref, dmat_ref, st_ref, dy_ref,
             da_ref, dlb_ref, dgain_ref, ds_scr, z_scr):
        @pl.when(pl.program_id(1) == 0)
        def _():
            ds_scr[...] = jnp.zeros_like(ds_scr)
            dlb_ref[...] = jnp.zeros_like(dlb_ref)
            dgain_ref[...] = jnp.zeros_like(dgain_ref)

        aqs = [aq_ref[:, _lanes(i)] for i in heads]
        lbs = [lb_ref[:, _lanes(i)] for i in heads]
        pre = [_hgrn_pre(aqs[i], af_ref[:, _lanes(i)], lbs[i]) for i in heads]
        sqs, qs, snegs, kks = ([p[j] for p in pre] for j in range(4))
        xs = [_hgrn_x(p[4], dmat_ref) for p in pre]
        bs = [x[0:c] for x in xs]
        b_lasts = [jnp.sum(p[4], axis=0, keepdims=True) for p in pre]
        ebs = [jnp.exp(b) for b in bs]
        ebls = [jnp.exp(bl - b) for bl, b in zip(b_lasts, bs)]
        ebl_rows = [jnp.exp(bl) for bl in b_lasts]
        qes = [_bf(q * eb) for q, eb in zip(qs, ebs)]
        kes = [_bf(kk * ebl) for kk, ebl in zip(kks, ebls)]
        vbs = [_bf(ai_ref[:, _lanes(i)]) for i in heads]
        ss = [st_ref[i] for i in heads]
        sbs = [_bf(s) for s in ss]
        dss = [ds_scr[i] for i in heads]
        dsbs = [_bf(ds) for ds in dss]

        pbs = [_bf(p) for p in _hgrn_scores(xs, qs, kks, mask_ref)]
        os_ = [_dot(pbs[i], vbs[i]) + _dot_nt(qes[i], sbs[i]) for i in heads]

        dos = []
        for i in heads:
            ag, gain, dy = ag_ref[:, _lanes(i)], gain_ref[:, _lanes(i)], dy_ref[:, _lanes(i)]
            oh, r = _rms(os_[i])
            sg_sig = _sigmoid(ag)
            sg = ag * sg_sig
            da_ref[3, :, _lanes(i)] = _bf(dy * oh * gain * _silu_grad(ag, sg_sig))
            dgain_ref[:, _lanes(i)] += jnp.sum(dy * oh * sg, axis=0, keepdims=True)
            dos.append(_bf(_rms_bwd(dy * gain * sg, oh, r)))

        dps = [_dot_nt(dos[i], vbs[i]) for i in heads]
        for i in heads:
            da_ref[2, :, _lanes(i)] = _bf(_dot_tn(pbs[i], dos[i]) + _dot_nt(kes[i], dsbs[i]))
        dq_ss = [ebs[i] * _dot(dos[i], sbs[i]) for i in heads]
        dk_ss = [ebls[i] * _dot(vbs[i], dsbs[i]) for i in heads]
        dqs, dks = [], []
        for i in heads:
            dpd = jnp.sum(mask_ref[0] * dps[i], axis=1, keepdims=True)
            z_scr[i, pl.ds(0, c), :] = qs[i] * dq_ss[i]
            z_scr[i, pl.ds((nl + 1) * c, c), :] = kks[i] * dk_ss[i]
            dqs.append(dq_ss[i] + dpd * kks[i])
            dks.append(dk_ss[i] + dpd * qs[i])
        for l in range(nl):
            for i in heads:
                qf, kf, ql, kl = _hgrn_level(xs[i], l, qs[i], kks[i])
                dpl = _bf(mask_ref[l + 1] * dps[i])
                dq_l = qf * _dot(dpl, kl)
                dk_l = kf * _dot_tn(dpl, ql)
                z_scr[i, pl.ds((l + 1) * c, c), :] = qs[i] * dq_l - kks[i] * dk_l
                dqs[i] = dqs[i] + dq_l
                dks[i] = dks[i] + dk_l

        zsplits = [_split(z_scr[i]) for i in heads]
        dlogfs = [_dot_tn(dmat_ref[...], zhi) + _dot_tn(dmat_ref[...], zlo) for zhi, zlo in zsplits]
        ds_new = [_dot_tn(dos[i], qes[i]) for i in heads]
        for i in heads:
            dlogf = dlogfs[i] + ebl_rows[i] * jnp.sum(dss[i] * ss[i], axis=0, keepdims=True)
            dkk = dks[i] - dlogf / (1.0 - kks[i])
            da_ref[1, :, _lanes(i)] = _bf(dkk * (1.0 - lbs[i]) * (-(snegs[i] * (1.0 - snegs[i]))))
            dlb_ref[:, _lanes(i)] += jnp.sum(dkk * (-snegs[i]), axis=0, keepdims=True)
            da_ref[0, :, _lanes(i)] = _bf(dqs[i] * _silu_grad(aqs[i], sqs[i]))
            ds_scr[i] = dss[i] * ebl_rows[i] + ds_new[i]

    w = HG_HEADS * A_D
    return pl.pallas_call(
        body, name="hgrn_bwd", grid=(A_HEADS // HG_HEADS, nch),
        in_specs=[col(0), col(1), col(2), col(3), vec, vec, mask_spec, dmat_spec, state_spec,
                  pl.BlockSpec((c, w), lambda h, n: (cidx(n), h))],
        out_specs=[pl.BlockSpec((4, c, w), lambda h, n: (0, cidx(n), h)), vec, vec],
        out_shape=[_sds((4, t, GROUP), BF16)] + [_sds((1, GROUP))] * 2,
        scratch_shapes=[pltpu.VMEM((HG_HEADS, A_D, A_D), F32), pltpu.VMEM((HG_HEADS, (nl + 2) * c, A_D), F32)],
        compiler_params=_params(("arbitrary", "arbitrary")),
    )(proj, proj, proj, proj, lb, gain, masks, dmat, states, dya)


def _sb_consts():
    j = np.arange(SB_TK)
    strict = (j[:, None] > j[None, :]).astype(np.float32)
    lower = (j[:, None] < j[None, :]).astype(np.float32)
    return jnp.asarray(strict, BF16), jnp.asarray(lower, BF16)


def _lane(x, k):
    return jnp.broadcast_to(x[:, k:k + 1], x.shape)


def _key_lane(kb):
    return lax.broadcasted_iota(jnp.int32, (1, SB_TK), 1) == kb


def _causal(x, masked):
    if not masked:
        return x
    n = (SB_TK, SB_TK)
    top = jnp.where(lax.broadcasted_iota(jnp.int32, n, 1) < lax.broadcasted_iota(jnp.int32, n, 0), x[:SB_TK], 0.0)
    return top if x.shape[0] == SB_TK else jnp.concatenate([top, x[SB_TK:]], axis=0)


def _sb_softplus(z, masked):
    logsig = jnp.minimum(z, 0.0) - jnp.log2(1.0 + jnp.exp2(-jnp.abs(z)))
    return _causal(z - logsig, masked), logsig


def _sb_sweep(qi, group_fn, state, groups, ascending):
    nd = SB_TQ // SB_TK

    def tile(kb, r0, masked):
        return (pl.multiple_of(kb * SB_TK, SB_TK), r0, masked, kb)

    def run(tiles, st, group):
        for i in range(0, len(tiles), group):
            st = group_fn(tiles[i:i + group], st)
        return st

    diag = [tile(qi * nd + d, d * SB_TK, True) for d in range(nd)]
    if ascending:
        state = lax.fori_loop(
            0, qi, lambda j, st: run([tile(j * nd + g, 0, False) for g in range(nd)], st, groups[1]), state)
        return run(diag, state, groups[0])
    state = run(diag[::-1], state, groups[0])
    return lax.fori_loop(
        0, qi, lambda j, st: run([tile((qi - j) * nd - 1 - g, 0, False) for g in range(nd)], st, groups[1]), state)


def _set_rows(r0, full, new):
    return new if r0 == 0 else jnp.concatenate([full[:r0], new], axis=0)


def _sb_specs(t, tq):
    col = lambda g: pl.BlockSpec((tq, 2 * B_D), lambda p, i, h: (i, g * (GROUP // (2 * B_D)) + p))
    full = lambda g: pl.BlockSpec((t, 2 * B_D), lambda p, i, h: (0, g * (GROUP // (2 * B_D)) + p))
    vec = pl.BlockSpec((1, 2 * B_D), lambda p, i, h: (0, p))
    mat = pl.BlockSpec((SB_TK, SB_TK), lambda p, i, h: (0, 0))
    car = pl.BlockSpec((None, tq, SB_TK), lambda p, i, h: (2 * p + h, i, 0))
    return col, full, vec, mat, car


def _head_lanes(h):
    return (lax.broadcasted_iota(jnp.int32, (1, 2 * B_D), 1) >= B_D) == (h == 1)


def _put(ref, h, val):
    @pl.when(h == 0)
    def _():
        ref[...] = val

    @pl.when(h == 1)
    def _():
        ref[...] += val


def _sb_fwd(qkv, bgate, gain, gather=None):
    t = qkv.shape[0]
    tq = SB_TQ
    strict, _ = _sb_consts()
    n_steps = (B_HEADS // 2, t // tq, 2)

    def body(q_ref, k_ref, v_ref, bg_ref, gain_ref, m_ref, *rest):
        if gather is None:
            o_ref, y_ref, car_ref = rest
        else:
            flat_ref, _, o_ref, y_ref, car_ref, gathered_ref, send_sems, recv_sems = rest
            start, forward, finish = _gather_plan(flat_ref, gathered_ref, send_sems, recv_sems, *gather[2:])
            step = (pl.program_id(0) * n_steps[1] + pl.program_id(1)) * n_steps[2] + pl.program_id(2)
            pl.when(step == 0)(start)
            pl.when(step == 2 * n_steps[1] * n_steps[2])(forward)
            pl.when(step == n_steps[0] * n_steps[1] * n_steps[2] - 1)(finish)
        h = pl.program_id(2)
        lanes = _head_lanes(h)
        qb = jnp.where(lanes, q_ref[...], jnp.zeros_like(q_ref))
        cmat = m_ref[...]

        car_ref[...] = jnp.zeros_like(car_ref)

        def group(tiles, state):
            carry, acc = state
            kv = [(k_ref[pl.ds(off, SB_TK), :], v_ref[pl.ds(off, SB_TK), :]) for off, _, _, _ in tiles]
            zs = [_dot_nt(qb[r0:], kb) for (_, r0, _, _), (kb, _) in zip(tiles, kv)]
            sps = [_sb_softplus(z, masked) for z, (_, _, masked, _) in zip(zs, tiles)]
            css = [_dot(_bf(sp), cmat) for sp, _ in sps]
            ws = []
            for (sp, logsig), cs, (_, r0, masked, kb) in zip(sps, css, tiles):
                ws.append(_bf(_causal(jnp.exp2(logsig - cs - carry[r0:]), masked)))
                car_ref[r0:, :] = jnp.where(_key_lane(kb), carry[r0:], car_ref[r0:, :])
                carry = _set_rows(r0, carry, carry[r0:] + _lane(cs + sp, 0))
            for w, (_, vb), (_, r0, _, _) in zip(ws, kv, tiles):
                acc = _set_rows(r0, acc, acc[r0:] + _dot(w, vb))
            return carry, acc

        _, acc = _sb_sweep(pl.program_id(1), group, (jnp.zeros((tq, SB_TK), F32), jnp.zeros((tq, 2 * B_D), F32)),
                           SB_GROUP_FWD, False)
        o = jnp.where(lanes, acc, 0.0)
        oh = o * lax.rsqrt(jnp.sum(o * o, axis=-1, keepdims=True) * (1.0 / B_D) + EPS)
        bg = bg_ref[...]
        _put(o_ref, h, o)
        _put(y_ref, h, oh * gain_ref[...] * (bg * _sigmoid(bg)))

    col, full, vec, mat, car = _sb_specs(t, tq)
    out = pl.BlockSpec((tq, 2 * B_D), lambda p, i, h: (i, p))
    in_specs = [col(0), full(1), full(2), col(0), vec, mat]
    out_specs = [out, out, car]
    out_shape = [_sds((t, GROUP)), _sds((t, GROUP)), _sds((B_HEADS, t, SB_TK))]
    operands = [qkv, qkv, qkv, bgate, gain, strict]
    extra = {}
    if gather is not None:
        in_specs += [_ANY, _ANY]
        out_specs += [_ANY]
        out_shape += [_sds(gather[1].shape, gather[1].dtype)]
        operands += [gather[0], gather[1]]
        extra = dict(input_output_aliases={7: 3}, scratch_shapes=_gather_sems(gather[4]))
    return pl.pallas_call(
        body, name="sb_fwd" if gather is None else "sb_fwd_gather", grid=n_steps,
        in_specs=in_specs, out_specs=out_specs, out_shape=out_shape,
        compiler_params=_params(("arbitrary", "arbitrary", "arbitrary")), **extra,
    )(*operands)


def _sb_bwd(qkv, bgate, o, carries, dy, gain, exchange=None):
    t = qkv.shape[0]
    tq = SB_TQ
    strict, lower = _sb_consts()

    def body(q_ref, k_ref, v_ref, bg_ref, o_ref, car_ref, dy_ref, gain_ref, ms_ref, ml_ref,
             dq_ref, dk_ref, dv_ref, dbg_ref, dgain_ref, qb_scr, do_scr, gl_scr, dq_scr):
        qi = pl.program_id(1)
        h = pl.program_id(2)
        lanes = _head_lanes(h)

        @pl.when((qi == 0) & (h == 0))
        def _():
            dk_ref[...] = jnp.zeros_like(dk_ref)
            dv_ref[...] = jnp.zeros_like(dv_ref)
            dgain_ref[...] = jnp.zeros_like(dgain_ref)

        qb_scr[...] = jnp.where(lanes, q_ref[...], jnp.zeros_like(q_ref))
        cmat = ms_ref[...]
        lmat = ml_ref[...]
        o = jnp.where(lanes, o_ref[...], 0.0)
        dy = jnp.where(lanes, dy_ref[...], 0.0)
        bg = bg_ref[...]
        gain = gain_ref[...]
        r = lax.rsqrt(jnp.sum(o * o, axis=-1, keepdims=True) * (1.0 / B_D) + EPS)
        oh = o * r
        sig = _sigmoid(bg)
        sg = bg * sig
        _put(dbg_ref, h, dy * oh * gain * _silu_grad(bg, sig))
        dgain_ref[...] += jnp.sum(dy * oh * sg, axis=0, keepdims=True)
        doh = dy * gain * sg
        do_scr[...] = _bf(r * (doh - oh * (jnp.sum(doh * oh, axis=-1, keepdims=True) * (1.0 / B_D))))

        def group(tiles, state):
            kv = [(k_ref[pl.ds(off, SB_TK), :], v_ref[pl.ds(off, SB_TK), :]) for off, _, _, _ in tiles]
            zs = [_dot_nt(qb_scr[r0:, :], kb) for (_, r0, _, _), (kb, _) in zip(tiles, kv)]
            dws = [_dot_nt(do_scr[r0:, :], vb) for (_, r0, _, _), (_, vb) in zip(tiles, kv)]
            sps = [_sb_softplus(z, masked) for z, (_, _, masked, _) in zip(zs, tiles)]
            css = [_dot(_bf(sp), cmat) for sp, _ in sps]
            ws, gs = [], []
            for (_, logsig), cs, dw, (_, r0, masked, kb) in zip(sps, css, dws, tiles):
                right = jnp.sum(jnp.where(_key_lane(kb), car_ref[r0:, :], 0.0), axis=1, keepdims=True)
                w = _causal(jnp.exp2(logsig - cs - right), masked)
                ws.append(_bf(w))
                gs.append(dw * w)
            gps = [_dot(_bf(g), lmat) for g in gs]
            dzs = []
            for (_, logsig), g, gp, (_, r0, masked, _) in zip(sps, gs, gps, tiles):
                dz = g - jnp.exp2(logsig) * (g + gl_scr[r0:, :] + gp)
                dzs.append(_bf(_causal(dz, masked)))
                gl_scr[r0:, :] += _lane(gp + g, SB_TK - 1)
            for dz, wb, (kb, _), (off, r0, _, _) in zip(dzs, ws, kv, tiles):
                dq_scr[r0:, :] += _dot(dz, kb)
                dk_ref[pl.ds(off, SB_TK), :] += _dot_tn(dz, qb_scr[r0:, :])
                dv_ref[pl.ds(off, SB_TK), :] += _dot_tn(wb, do_scr[r0:, :])
            return state

        gl_scr[...] = jnp.zeros_like(gl_scr)
        dq_scr[...] = jnp.zeros_like(dq_scr)
        _sb_sweep(qi, group, jnp.int32(0), SB_GROUP_BWD, True)
        _put(dq_ref, h, jnp.where(lanes, dq_scr[...] * (B_D ** -0.5), 0.0))

    col, full, vec, mat, car = _sb_specs(t, tq)
    blk = pl.BlockSpec((tq, 2 * B_D), lambda p, i, h: (i, p))
    whole = pl.BlockSpec((t, 2 * B_D), lambda p, i, h: (0, p))
    grid = (B_HEADS // 2, t // tq, 2)
    in_specs = [col(0), full(1), full(2), col(0), blk, car, blk, vec, mat, mat]
    out_specs = [blk, whole, whole, blk, vec]
    out_shape = [_sds((t, GROUP))] * 4 + [_sds((1, GROUP))]
    operands = [qkv, qkv, qkv, bgate, o, carries, dy, gain, strict, lower]
    scratch = []
    if exchange is not None:
        step_of = lambda: (pl.program_id(0) * grid[1] + pl.program_id(1)) * grid[2] + pl.program_id(2)
        body = _with_exchange(body, 10, 5, exchange, step_of, grid[0] * grid[1] * grid[2])
        xi, xo, xs, scratch, xop = _exchange_args(exchange)
        in_specs, out_specs, out_shape, operands = in_specs + xi, out_specs + xo, out_shape + xs, operands + xop
    scratch = scratch + [pltpu.VMEM((tq, 2 * B_D), BF16), pltpu.VMEM((tq, 2 * B_D), BF16),
                         pltpu.VMEM((tq, SB_TK), F32), pltpu.VMEM((tq, 2 * B_D), F32)]
    return pl.pallas_call(
        body, name="sb_bwd" if exchange is None else "sb_bwd_exchange", grid=grid,
        in_specs=in_specs, out_specs=out_specs, out_shape=out_shape, scratch_shapes=scratch,
        compiler_params=_params(("arbitrary", "arbitrary", "arbitrary")),
    )(*operands)


def _adamw(w, g, m, v):
    rows, cols = w.shape
    tr = rows
    for cand in (400, 256, 128, 64, 32, 16, 8):
        if rows % cand == 0:
            tr = cand
            break

    def body(w_ref, g_ref, m_ref, v_ref, d_ref, nm_ref, nv_ref):
        g_ = g_ref[...]
        m_ = ADAM_B1 * m_ref[...] + (1.0 - ADAM_B1) * g_
        v_ = ADAM_B2 * v_ref[...] + (1.0 - ADAM_B2) * (g_ * g_)
        m_hat = m_ / (1.0 - ADAM_B1 ** ADAM_STEP)
        v_hat = v_ / (1.0 - ADAM_B2 ** ADAM_STEP)
        d_ref[...] = -ADAM_LR * (m_hat / (jnp.sqrt(v_hat) + ADAM_EPS) + ADAM_WD * w_ref[...])
        nm_ref[...] = m_
        nv_ref[...] = v_

    spec = pl.BlockSpec((tr, cols), lambda i: (i, 0))
    return pl.pallas_call(
        body, name="adamw", grid=(rows // tr,), in_specs=[spec] * 4, out_specs=[spec] * 3,
        out_shape=[_sds((rows, cols))] * 3, compiler_params=_params(("arbitrary",)),
    )(w, g, m, v)


_ANY = pl.BlockSpec(memory_space=pl.ANY)


def _place():
    return lax.axis_index("x"), lax.axis_index("y"), lax.axis_index("c")


def _gather_plan(x_ref, out_ref, send_sems, recv_sems, row0, nrows, nc):
    x, y, c = _place()
    me = 2 * x + y
    sibling = (x, y, 1 - c)
    half = nrows // 2
    ch = half // nc
    peers = [me ^ k for k in (1, 2, 3)]

    def rows(shard, hc, r):
        return out_ref.at[shard, pl.ds(row0 + hc * half + r * ch, ch), :]

    def copy(k, shard, hc, r, to, src=None):
        return pltpu.make_async_remote_copy(
            src_ref=rows(shard, hc, r) if src is None else src, dst_ref=rows(shard, hc, r),
            send_sem=send_sems.at[k * nc + r], recv_sem=recv_sems.at[k * nc + r], device_id=to, device_id_type=MESH)

    def first(k, p, r):
        return copy(k, me, c, r, (p >> 1, p & 1, c), src=x_ref.at[pl.ds(row0 + c * half + r * ch, ch), :])

    def start():
        for k, p in enumerate(peers):
            for r in range(nc):
                first(k, p, r).start()

    def forward():
        for k, p in enumerate(peers):
            for r in range(nc):
                copy(k, p, c, r, sibling).wait_recv()
                copy(3 + k, p, c, r, sibling).start()

    def finish():
        for k, p in enumerate(peers):
            for r in range(nc):
                copy(3 + k, p, 1 - c, r, sibling).wait_recv()
        for k, p in enumerate(peers):
            for r in range(nc):
                first(k, p, r).wait_send()
                copy(3 + k, p, c, r, sibling).wait_send()

    return start, forward, finish


def _gather_sems(nc):
    return [pltpu.SemaphoreType.DMA((6 * nc,)), pltpu.SemaphoreType.DMA((6 * nc,))]


def _swap_plan(g_ref, out_ref, send_sems, recv_sems):
    half, ch, nc = HALF_LAYER, CHUNK_ROWS, RS_CHUNKS

    def copies():
        x, y, c = _place()
        return [pltpu.make_async_remote_copy(
            src_ref=g_ref.at[j, pl.ds((1 - c) * half + r * ch, ch), :], dst_ref=out_ref.at[j, pl.ds(r * ch, ch), :],
            send_sem=send_sems.at[j * nc + r], recv_sem=recv_sems.at[j * nc + r],
            device_id=(x, y, 1 - c), device_id_type=MESH) for j in range(N_SHARD) for r in range(nc)]

    def start():
        for cp in copies():
            cp.start()

    def finish():
        for cp in copies():
            cp.wait()

    return start, finish


def _scatter_plan(p_ref, out_ref, send_sems, recv_sems):
    ch, nc = CHUNK_ROWS, RS_CHUNKS

    def copies():
        x, y, c = _place()
        me = 2 * x + y
        return [pltpu.make_async_remote_copy(
            src_ref=p_ref.at[me ^ k, pl.ds(r * ch, ch), :], dst_ref=out_ref.at[k - 1, pl.ds(r * ch, ch), :],
            send_sem=send_sems.at[(k - 1) * nc + r], recv_sem=recv_sems.at[(k - 1) * nc + r],
            device_id=((me ^ k) >> 1, (me ^ k) & 1, c), device_id_type=MESH) for k in (1, 2, 3) for r in range(nc)]

    def start():
        for cp in copies():
            cp.start()

    def finish():
        for cp in copies():
            cp.wait()

    return start, finish


SWAP = (_swap_plan, (N_SHARD, HALF_LAYER, D_MODEL), F32, N_SHARD * RS_CHUNKS)
SCATTER = (_scatter_plan, (3, HALF_LAYER, D_MODEL), BF16, 3 * RS_CHUNKS)


def _exchange_call(kind, operand):
    plan, shape, dtype, n_sems = kind

    def body(in_ref, out_ref, send_sems, recv_sems):
        start, finish = plan(in_ref, out_ref, send_sems, recv_sems)
        start()
        finish()

    return pl.pallas_call(
        body, name="exchange", in_specs=[_ANY], out_specs=_ANY, out_shape=_sds(shape, dtype),
        scratch_shapes=[pltpu.SemaphoreType.DMA((n_sems,)), pltpu.SemaphoreType.DMA((n_sems,))],
    )(operand)


def _with_exchange(body, n_in, n_out, exchange, step_of, n_steps):
    def wrapped(*refs):
        ins, src = refs[:n_in], refs[n_in]
        outs, dst = refs[n_in + 1:n_in + 1 + n_out], refs[n_in + 1 + n_out]
        send_sems, recv_sems = refs[n_in + 2 + n_out:n_in + 4 + n_out]
        start, finish = exchange[0][0](src, dst, send_sems, recv_sems)
        pl.when(step_of() == 0)(start)
        body(*ins, *outs, *refs[n_in + 4 + n_out:])
        pl.when(step_of() == n_steps - 1)(finish)

    return wrapped


def _exchange_args(exchange):
    (plan, shape, dtype, n_sems), operand = exchange
    sems = [pltpu.SemaphoreType.DMA((n_sems,)), pltpu.SemaphoreType.DMA((n_sems,))]
    return [_ANY], [_ANY], [_sds(shape, dtype)], sems, [operand]


def _gather_weights(flat, row0, nrows, nc):
    def body(x_ref, out_ref, send_sems, recv_sems):
        start, forward, finish = _gather_plan(x_ref, out_ref, send_sems, recv_sems, row0, nrows, nc)
        start()
        forward()
        finish()

    return pl.pallas_call(
        body, name="gather_weights", in_specs=[_ANY], out_specs=_ANY,
        out_shape=_sds((N_SHARD, ROWS_FLAT, D_MODEL), BF16), scratch_shapes=_gather_sems(nc),
    )(flat)


def _add_my_half(grads, recv):
    tr = 400
    nb = HALF_LAYER // tr
    core = lax.axis_index("c").astype(jnp.int32).reshape(1)

    def body(c_ref, g_ref, r_ref, o_ref, ob_ref):
        acc = g_ref[...] + r_ref[...]
        o_ref[...] = acc
        ob_ref[...] = _bf(acc)

    out = pl.BlockSpec((None, tr, D_MODEL), lambda j, i, c_ref: (j, i, 0))
    return pl.pallas_call(
        body, name="add_my_half",
        grid_spec=pltpu.PrefetchScalarGridSpec(
            num_scalar_prefetch=1, grid=(N_SHARD, nb),
            in_specs=[pl.BlockSpec((None, tr, D_MODEL), lambda j, i, c_ref: (j, c_ref[0] * nb + i, 0)), out],
            out_specs=[out, out]),
        out_shape=[_sds((N_SHARD, HALF_LAYER, D_MODEL)), _sds((N_SHARD, HALF_LAYER, D_MODEL), BF16)],
        compiler_params=_params(("arbitrary", "arbitrary")),
    )(core, grads, recv)


def _sum_scattered(part, recv):
    tr = 400
    chip = (2 * lax.axis_index("x") + lax.axis_index("y")).astype(jnp.int32).reshape(1)

    def body(c_ref, p_ref, r_ref, o_ref):
        acc = p_ref[...]
        for k in range(3):
            acc = acc + r_ref[k].astype(F32)
        o_ref[...] = acc

    return pl.pallas_call(
        body, name="sum_scattered",
        grid_spec=pltpu.PrefetchScalarGridSpec(
            num_scalar_prefetch=1, grid=(HALF_LAYER // tr,),
            in_specs=[pl.BlockSpec((None, tr, D_MODEL), lambda i, c_ref: (c_ref[0], i, 0)),
                      pl.BlockSpec((3, tr, D_MODEL), lambda i, c_ref: (0, i, 0))],
            out_specs=pl.BlockSpec((tr, D_MODEL), lambda i, c_ref: (i, 0))),
        out_shape=_sds((HALF_LAYER, D_MODEL)), compiler_params=_params(("arbitrary",)),
    )(chip, part, recv)


def _swap_reduced(mine):
    ch, nc = CHUNK_ROWS, RS_CHUNKS

    def body(r_ref, out_ref, send_sems, recv_sems):
        x, y, c = _place()
        copies = [pltpu.make_async_remote_copy(
            src_ref=r_ref.at[l, pl.ds(r * ch, ch), :], dst_ref=out_ref.at[l, pl.ds(r * ch, ch), :],
            send_sem=send_sems.at[l * nc + r], recv_sem=recv_sems.at[l * nc + r],
            device_id=(x, y, 1 - c), device_id_type=MESH) for l in range(2) for r in range(nc)]
        for cp in copies:
            cp.start()
        for cp in copies:
            cp.wait()

    return pl.pallas_call(
        body, name="swap_reduced", in_specs=[_ANY], out_specs=_ANY,
        out_shape=_sds((2, HALF_LAYER, D_MODEL)),
        scratch_shapes=[pltpu.SemaphoreType.DMA((2 * nc,)), pltpu.SemaphoreType.DMA((2 * nc,))],
    )(mine)


def _allreduce_small(vec):
    def body(v_ref, out_ref, buf, send_sems, recv_sems):
        x, y, c = _place()
        me = 4 * x + 2 * y + c
        buf[me] = v_ref[...]
        peers = [me ^ k for k in range(1, N_DEV)]
        sends = [pltpu.make_async_remote_copy(
            src_ref=v_ref, dst_ref=buf.at[me], send_sem=send_sems.at[k], recv_sem=recv_sems.at[k],
            device_id=(p >> 2, (p >> 1) & 1, p & 1), device_id_type=MESH) for k, p in enumerate(peers)]
        for cp in sends:
            cp.start()
        for k, p in enumerate(peers):
            pltpu.make_async_remote_copy(
                src_ref=v_ref, dst_ref=buf.at[p], send_sem=send_sems.at[k], recv_sem=recv_sems.at[k],
                device_id=(p >> 2, (p >> 1) & 1, p & 1), device_id_type=MESH).wait_recv()
        for cp in sends:
            cp.wait_send()
        acc = buf[0]
        for d in range(1, N_DEV):
            acc = acc + buf[d]
        out_ref[...] = acc

    vm = pl.BlockSpec(memory_space=pltpu.VMEM)
    return pl.pallas_call(
        body, name="allreduce_small", in_specs=[vm], out_specs=vm, out_shape=_sds((SMALL_ROWS, 128)),
        scratch_shapes=[pltpu.VMEM((N_DEV, SMALL_ROWS, 128), F32),
                        pltpu.SemaphoreType.DMA((N_DEV - 1,)), pltpu.SemaphoreType.DMA((N_DEV - 1,))],
    )(vec)


def _flatten_shard(w_in, w_out, w_pg, w_pp):
    return jnp.concatenate([w_in.reshape(-1, D_MODEL), w_out.reshape(-1, D_MODEL), w_pg.reshape(-1, D_MODEL),
                            w_pp.reshape(-1, D_MODEL)], axis=0)


def _unflatten_layers(flats):
    a, b, c = D_MODEL, D_MODEL + D_MODEL // N_SHARD, D_MODEL + 2 * (D_MODEL // N_SHARD)
    q = D_MODEL // N_SHARD
    return (jnp.stack([f[:a] for f in flats]), jnp.stack([f[a:b] for f in flats]),
            jnp.stack([f[b:c] for f in flats]), jnp.stack([f[c:].reshape(D_PLE, q) for f in flats]))


def _full_w_pp(gathered):
    c = ROWS_W_IN + ROWS_W_OUT + ROWS_W_PG
    q = D_MODEL // N_SHARD
    rpp = ROWS_W_PP // 2
    return [gathered[:, c + l * rpp:c + (l + 1) * rpp, :].reshape(N_SHARD, D_PLE, q).transpose(1, 0, 2)
            .reshape(D_PLE, D_MODEL) for l in range(2)]


def _layer_grads(dw_in, dw_out, dw_pg, dw_pp):
    q = D_MODEL // N_SHARD
    rpp = ROWS_W_PP // 2
    rest = jnp.concatenate([dw_out.reshape(N_SHARD, q, D_MODEL), dw_pg.reshape(N_SHARD, q, D_MODEL),
                            dw_pp.reshape(D_PLE, N_SHARD, q).transpose(1, 0, 2).reshape(N_SHARD, rpp, D_MODEL)], axis=1)
    return lax.dynamic_update_slice(dw_in, rest, (0, D_MODEL, 0))


def _lower_bounds(lb_logits):
    sm = jax.nn.softmax(lb_logits.astype(F32), axis=0)
    return jnp.cumsum(sm, axis=0) - sm[0:1]


def kernel(x, p, norm_mix, w_in, a_out_norm, b_out_norm, w_out, lb_logits, ple_gate_norm, w_ple_gate, w_ple_proj, ple_post_norm, final_norm, loss_target, m_norm_mix, m_w_in, m_a_out_norm, m_b_out_norm, m_w_out, m_lb_logits, m_ple_gate_norm, m_w_ple_gate, m_w_ple_proj, m_ple_post_norm, m_final_norm, v_norm_mix, v_w_in, v_a_out_norm, v_b_out_norm, v_w_out, v_lb_logits, v_ple_gate_norm, v_w_ple_gate, v_w_ple_proj, v_ple_post_norm, v_final_norm):
    t = x.shape[1]
    h0 = x.reshape(t, D_MODEL)
    target = loss_target.reshape(t, D_MODEL)
    pl_in = p.reshape(2, t, D_PLE)

    w_flat_bf = _flatten_shard(_bf(w_in), _bf(w_out), _bf(w_ple_gate), _bf(w_ple_proj))
    chip = 2 * lax.axis_index("x") + lax.axis_index("y")
    gathered = lax.dynamic_update_slice(_gather_weights(w_flat_bf, 0, D_MODEL, 2), w_flat_bf[None], (chip, 0, 0))
    lbs, lbs_vjp = jax.vjp(_lower_bounds, lb_logits)

    saved = []
    h = h0
    for l in range(2):
        g_mix = norm_mix[l].reshape(1, D_MODEL)
        lb = lbs[l].reshape(1, GROUP)
        ga = a_out_norm[l].reshape(1, GROUP)
        gb = b_out_norm[l].reshape(1, GROUP)
        proj, qkv, bgate = _inproj(h, g_mix, gathered, l)
        ya, states = _hgrn_fwd(proj, lb, ga)
        if l == 0:
            ob, yb, cars, gathered = _sb_fwd(qkv, bgate, gb, (w_flat_bf, gathered, D_MODEL, ROWS_FLAT - D_MODEL, 4))
            w_pps = _full_w_pp(gathered)
        else:
            ob, yb, cars = _sb_fwd(qkv, bgate, gb)
        h1 = _outproj(h, ya, yb, gathered, l)
        g_post = ple_post_norm[l].reshape(1, D_MODEL)
        g_gate = ple_gate_norm[l].reshape(1, D_MODEL)
        saved.append((h, proj, qkv, bgate, states, ya, yb, ob, cars, h1))
        if l == 0:
            h = _ple_fwd(h1, pl_in[l], w_pps[l], gathered, l, g_post, g_gate)
        else:
            dh, d_final, loss_part = _ple_fwd_final(h1, pl_in[l], w_pps[l], gathered, l, g_post, g_gate,
                                                    final_norm.reshape(1, D_MODEL), target)

    g_layer, chip_sum, scattered = [None] * 2, [None] * 2, [None] * 2
    d_mix, d_a, d_b, d_lb, d_gate, d_post = [None] * 2, [None] * 2, [None] * 2, [None] * 2, [None] * 2, [None] * 2
    for l in (1, 0):
        h_in, proj, qkv, bgate, states, ya, yb, ob, cars, h1 = saved[l]
        g_mix = norm_mix[l].reshape(1, D_MODEL)
        lb = lbs[l].reshape(1, GROUP)
        ga = a_out_norm[l].reshape(1, GROUP)
        gb = b_out_norm[l].reshape(1, GROUP)
        g_post = ple_post_norm[l].reshape(1, D_MODEL)
        g_gate = ple_gate_norm[l].reshape(1, D_MODEL)
        if l == 1:
            dh1, dw_pg, dw_pp, d_gate[l], d_post[l] = _ple_bwd(dh, h1, pl_in[l], w_pps[l], gathered, l, g_post, g_gate)
            dya, dyb, dw_out = _outproj_bwd(dh1, ya, yb, gathered, l)
            dbq, dbk, dbv, dbg, d_b[l] = _sb_bwd(qkv, bgate, ob, cars, dyb, gb)
        else:
            dh1, dw_pg, dw_pp, d_gate[l], d_post[l], from_sibling = _ple_bwd(
                dh, h1, pl_in[l], w_pps[l], gathered, l, g_post, g_gate, (SWAP, g_layer[1]))
            chip_sum[1], chip_sum_bf = _add_my_half(g_layer[1], from_sibling)
            dya, dyb, dw_out = _outproj_bwd(dh1, ya, yb, gathered, l)
            dbq, dbk, dbv, dbg, d_b[l], scattered[1] = _sb_bwd(qkv, bgate, ob, cars, dyb, gb, (SCATTER, chip_sum_bf))
        da, d_lb[l], d_a[l] = _hgrn_bwd(proj, lb, ga, states, dya)
        db = jnp.stack([dbq, dbk * LN2, dbv, dbg]).astype(BF16)
        g_layer[l] = _layer_grads(_inproj_bwd_dw(h_in, g_mix, da, db), dw_out, dw_pg, dw_pp)
        if l == 1:
            dh, d_mix[l] = _inproj_bwd_dx(dh1, h_in, g_mix, gathered, l, da, db)
        else:
            dh, d_mix[l], from_sibling = _inproj_bwd_dx(dh1, h_in, g_mix, gathered, l, da, db, (SWAP, g_layer[0]))
    grad_x = dh.reshape(x.shape)

    chip_sum[0], chip_sum_bf = _add_my_half(g_layer[0], from_sibling)
    scattered[0] = _exchange_call(SCATTER, chip_sum_bf)
    mine = jnp.stack([_sum_scattered(chip_sum[l], scattered[l]) for l in range(2)])
    other = _swap_reduced(mine)
    south = lax.axis_index("c") == 0
    g_w_in, g_w_out, g_w_pg, g_w_pp = _unflatten_layers(
        [jnp.concatenate([jnp.where(south, mine[l], other[l]), jnp.where(south, other[l], mine[l])]) for l in range(2)])

    small = jnp.concatenate([
        jnp.concatenate(d_mix).reshape(-1, 128), jnp.concatenate(d_a).reshape(-1, 128),
        jnp.concatenate(d_b).reshape(-1, 128), jnp.concatenate(d_lb).reshape(-1, 128),
        jnp.concatenate(d_gate).reshape(-1, 128), jnp.concatenate(d_post).reshape(-1, 128),
        d_final.reshape(-1, 128), jnp.broadcast_to(loss_part, (8, 128))], axis=0)
    small = _allreduce_small(small)
    loss = small[80, 0]
    g_norm_mix = small[0:16].reshape(2, D_MODEL)
    g_a = small[16:24].reshape(2, GROUP)
    g_b = small[24:32].reshape(2, GROUP)
    (g_lb,) = lbs_vjp(small[32:40].reshape(2, GROUP))
    g_gate = small[40:56].reshape(2, D_MODEL)
    g_post = small[56:72].reshape(2, D_MODEL)
    g_final = small[72:80].reshape(D_MODEL)

    def adam_matrix(w, g, m, v):
        d, nm, nv = _adamw(w.reshape(-1, D_MODEL), g.reshape(-1, D_MODEL), m.reshape(-1, D_MODEL), v.reshape(-1, D_MODEL))
        return d.reshape(w.shape), nm.reshape(w.shape), nv.reshape(w.shape)

    d_w_in, nm_w_in, nv_w_in = adam_matrix(w_in, g_w_in, m_w_in, v_w_in)
    d_w_out, nm_w_out, nv_w_out = adam_matrix(w_out, g_w_out, m_w_out, v_w_out)
    d_w_pg, nm_w_pg, nv_w_pg = adam_matrix(w_ple_gate, g_w_pg, m_w_ple_gate, v_w_ple_gate)
    d_w_pp, nm_w_pp, nv_w_pp = adam_matrix(w_ple_proj, g_w_pp, m_w_ple_proj, v_w_ple_proj)

    small_w = [norm_mix, a_out_norm, b_out_norm, lb_logits, ple_gate_norm, ple_post_norm, final_norm]
    small_g = [g_norm_mix, g_a, g_b, g_lb, g_gate, g_post, g_final]
    small_m = [m_norm_mix, m_a_out_norm, m_b_out_norm, m_lb_logits, m_ple_gate_norm, m_ple_post_norm, m_final_norm]
    small_v = [v_norm_mix, v_a_out_norm, v_b_out_norm, v_lb_logits, v_ple_gate_norm, v_ple_post_norm, v_final_norm]
    pack = lambda arrs: jnp.concatenate([a.reshape(-1, 128) for a in arrs], axis=0)
    ds, nms, nvs = _adamw(pack(small_w), pack(small_g), pack(small_m), pack(small_v))

    def unpack(packed):
        out, r = [], 0
        for a in small_w:
            n = a.size // 128
            out.append(packed[r:r + n].reshape(a.shape))
            r += n
        return out

    d_s, nm_s, nv_s = unpack(ds), unpack(nms), unpack(nvs)

    def ordered(s, big):
        return [s[0], big[0], s[1], s[2], big[1], s[3], s[4], big[2], big[3], s[5], s[6]]

    grads = ordered(small_g, [g_w_in, g_w_out, g_w_pg, g_w_pp])
    deltas = ordered(d_s, [d_w_in, d_w_out, d_w_pg, d_w_pp])
    new_m = ordered(nm_s, [nm_w_in, nm_w_out, nm_w_pg, nm_w_pp])
    new_v = ordered(nv_s, [nv_w_in, nv_w_out, nv_w_pg, nv_w_pp])
    return (loss, grad_x, *grads, *deltas, *new_m, *new_v)
```
